```python
import jax, jax.numpy as jnp
from jax import lax
import numpy as np

D_MODEL = 2048
BATCH = 8
SEQ = 2048
DEPTH = 2
DEC_BATCH = 4
DEC_SEQ = 2048
PAST_LEN = 128

MIX_HALF = D_MODEL // 2
HEAD_DIM = 128
EPS = 1e-6
NEG = -1e30

A_HEADS = MIX_HALF // HEAD_DIM
A_WIDTH = A_HEADS * HEAD_DIM
A_PATTERNS = ((128, 1), (512, 4), (2048, 16))
A_BLOCK = 64
N_BUCKETS = 32
MAX_DISTANCE = 1024

B_HEADS = 4
B_DV = MIX_HALF // B_HEADS
B_DK = B_DV // 2
B_WIDTH = B_HEADS * B_DV
B_KEYW = B_HEADS * B_DK
B_GATE_RANK = 16
B_GATE_TAU = 16.0
B_CHUNK = 64

C_HEADS = MIX_HALF // 128
C_Q_RANK = 512
C_KV_RANK = 256
C_NOPE = 128
C_ROPE = 64
C_V = 128
C_WIDTH = C_HEADS * C_V
C_QBLOCK = 128
ROPE_THETA = 10000.0

D_HEAD = 64
D_HEADS = MIX_HALF // D_HEAD
D_WIDTH = D_HEADS * D_HEAD
D_W_RANK = 64
D_A_RANK = 64
D_G_RANK = 128
D_LN_EPS = 64e-5

N_GROUPS = 4
EXPERTS_PER_GROUP = 4
N_EXPERTS = N_GROUPS * EXPERTS_PER_GROUP
EXPERT_FF = 512
TOP_K_IN_GROUP = 2

N_EVEN = (DEPTH + 1) // 2
N_ODD = DEPTH // 2

EVEN_SPLITS = (A_WIDTH, A_WIDTH, A_WIDTH, B_KEYW, B_KEYW, B_WIDTH, B_WIDTH, B_GATE_RANK, B_GATE_RANK)
EVEN_IN = 3 * A_WIDTH + 2 * B_KEYW + 2 * B_WIDTH + 2 * B_GATE_RANK
D_SPLITS = (D_WIDTH, D_WIDTH, D_WIDTH, D_W_RANK, D_W_RANK, D_A_RANK, D_G_RANK)
D_SHIFT = 3 * D_WIDTH + 2 * D_W_RANK + D_A_RANK + D_G_RANK
C_IN = C_Q_RANK + C_KV_RANK + C_ROPE
ODD_IN = C_IN + D_SHIFT

kernel_name = "hybrid_bidir_dilated_gla_mla_rwkv7_hmoe"


def _cuts(sizes):
    out, acc = [], 0
    for n in sizes[:-1]:
        acc += n
        out.append(acc)
    return out


def rmsnorm(x, g):
    xf = x.astype(jnp.float32)
    y = xf * lax.rsqrt(jnp.mean(xf * xf, axis=-1, keepdims=True) + EPS)
    return (y * g.astype(jnp.float32)).astype(x.dtype)


def t5_bucket(rel):
    half = N_BUCKETS // 2
    exact = half // 2
    n = np.abs(rel)
    large = exact + (np.log(np.maximum(n, 1) / exact) / np.log(MAX_DISTANCE / exact) * (half - exact)).astype(np.int64)
    large = np.minimum(large, half - 1)
    return ((rel > 0) * half + np.where(n < exact, n, large)).astype(np.int32)


def dilated_attention(q, k, v, rel_bias):
    b, s, h, e = q.shape
    f32 = jnp.float32
    scale = e ** -0.5
    outs, lses = [], []
    for window, dil in A_PATTERNS:
        n = window // (2 * dil)
        L = s // dil
        nblk = -(-L // A_BLOCK)
        Lp = nblk * A_BLOCK
        span = A_BLOCK + 2 * n
        sub = lambda t: t.reshape(b, L, dil, h, e).transpose(0, 2, 1, 3, 4)
        qs = jnp.pad(sub(q), ((0, 0), (0, 0), (0, Lp - L), (0, 0), (0, 0))).reshape(b, dil, nblk, A_BLOCK, h, e)
        kv_pad = ((0, 0), (0, 0), (n, Lp - L + n), (0, 0), (0, 0))
        idx = np.arange(nblk)[:, None] * A_BLOCK + np.arange(span)[None, :]
        kb = jnp.pad(sub(k), kv_pad)[:, :, idx]
        vb = jnp.pad(sub(v), kv_pad)[:, :, idx]
        rel = np.arange(span)[None, :] - np.arange(A_BLOCK)[:, None] - n
        key_l = idx - n
        mask = (np.abs(rel)[None] <= n) & (key_l[:, None, :] >= 0) & (key_l[:, None, :] < L)
        bias = jnp.transpose(rel_bias[t5_bucket(rel * dil)], (2, 0, 1)).astype(f32)
        sc = jnp.einsum('bdcqhe,bdckhe->bdchqk', qs, kb, preferred_element_type=f32) * scale + bias
        sc = jnp.where(mask[:, None], sc, NEG)
        m = jnp.max(sc, axis=-1, keepdims=True)
        p = jnp.exp(sc - m)
        den = jnp.sum(p, axis=-1, keepdims=True)
        o = jnp.einsum('bdchqk,bdckhe->bdchqe', p, vb.astype(f32)) / den
        lse = (m + jnp.log(den))[..., 0]
        o = o.transpose(0, 2, 4, 1, 3, 5).reshape(b, Lp, dil, h, e)[:, :L].reshape(b, s, h, e)
        lse = lse.transpose(0, 2, 4, 1, 3).reshape(b, Lp, dil, h)[:, :L].reshape(b, s, h)
        outs.append(o)
        lses.append(lse)
    wts = jax.nn.softmax(jnp.stack(lses), axis=0)[..., None]
    out = jnp.sum(wts * jnp.stack(outs), axis=0)
    return out.reshape(b, s, h * e).astype(q.dtype)


def gla_chunked(q, k, v, la):
    b, s, h, dk = q.shape
    dv = v.shape[-1]
    n = s // B_CHUNK
    chunks = lambda t: t.reshape(b, n, B_CHUNK, h, t.shape[-1]).transpose(1, 0, 3, 2, 4)
    qc, kc, vc, gc = chunks(q), chunks(k), chunks(v), chunks(la)
    gcum = jnp.cumsum(gc, axis=3)
    gend = gcum[:, :, :, -1:, :]
    q_in = qc * jnp.exp(gcum)
    k_in = kc * jnp.exp(-gcum)
    k_out = kc * jnp.exp(gend - gcum)
    causal = jnp.tril(jnp.ones((B_CHUNK, B_CHUNK), dtype=bool))
    att = jnp.where(causal, jnp.einsum('nbhqd,nbhkd->nbhqk', q_in, k_in), 0.0)
    intra = jnp.einsum('nbhqk,nbhkv->nbhqv', att, vc)

    def step(state, inp):
        qi, ko, vi, ge = inp
        o = jnp.einsum('bhqd,bhdv->bhqv', qi, state)
        state = state * jnp.exp(ge)[:, :, 0, :, None] + jnp.einsum('bhkd,bhkv->bhdv', ko, vi)
        return state, o

    _, inter = lax.scan(step, jnp.zeros((b, h, dk, dv), jnp.float32), (q_in, k_out, vc, gend))
    return (intra + inter).transpose(1, 0, 3, 2, 4).reshape(b, s, h, dv)


def gla_mixer(q, k, v, g, zf, zb, w2_f, b_f, w2_b, b_b, onorm):
    b, s, _ = q.shape
    f32 = jnp.float32
    heads = lambda t, e: t.astype(f32).reshape(b, s, B_HEADS, e)
    q = heads(q, B_DK) * (B_DK ** -0.5)
    k = heads(k, B_DK)
    v = heads(v, B_DV)
    la_f = heads(jax.nn.log_sigmoid((zf @ w2_f + b_f).astype(f32)), B_DK) / B_GATE_TAU
    la_b = heads(jax.nn.log_sigmoid((zb @ w2_b + b_b).astype(f32)), B_DK) / B_GATE_TAU
    rev = lambda t: jnp.flip(t, axis=1)
    o_f = gla_chunked(q, k, v, la_f)
    o_b = rev(gla_chunked(rev(q), rev(k), rev(v), rev(la_b)))
    o = rmsnorm(o_f + o_b, onorm).reshape(b, s, B_WIDTH) * jax.nn.silu(g.astype(f32))
    return o.astype(g.dtype)


def even_mixer(h, rel_bias, w_in, w_out, w2_f, b_f, w2_b, b_b, onorm):
    b, s, _ = h.shape
    qa, ka, va, qb, kb, vb, gb, zf, zb = jnp.split(h @ w_in, _cuts(EVEN_SPLITS), axis=-1)
    heads_a = lambda t: t.reshape(b, s, A_HEADS, HEAD_DIM)
    ya = dilated_attention(heads_a(qa), heads_a(ka), heads_a(va), rel_bias)
    yb = gla_mixer(qb, kb, vb, gb, zf, zb, w2_f, b_f, w2_b, b_b, onorm)
    return jnp.concatenate([ya, yb], axis=-1) @ w_out


def rope_cos_sin(s, dim):
    inv = 1.0 / (ROPE_THETA ** (jnp.arange(0, dim, 2, dtype=jnp.float32) / dim))
    ang = jnp.arange(s, dtype=jnp.float32)[:, None] * inv[None, :]
    return jnp.cos(ang), jnp.sin(ang)


def apply_rope(x, cos, sin):
    x1, x2 = jnp.split(x.astype(jnp.float32), 2, axis=-1)
    return jnp.concatenate([x1 * cos - x2 * sin, x1 * sin + x2 * cos], axis=-1).astype(x.dtype)


def mla_attention(cq, ckv, kr, q_norm, w_uq, kv_norm, w_ukv):
    b, s, _ = cq.shape
    f32 = jnp.float32
    q = (rmsnorm(cq, q_norm) @ w_uq).reshape(b, s, C_HEADS, C_NOPE + C_ROPE)
    kv = (rmsnorm(ckv, kv_norm) @ w_ukv).reshape(b, s, C_HEADS, C_NOPE + C_V)
    q_nope, q_rope = q[..., :C_NOPE], q[..., C_NOPE:]
    k_nope, v = kv[..., :C_NOPE], kv[..., C_NOPE:]
    cos, sin = rope_cos_sin(s, C_ROPE)
    q_rope = apply_rope(q_rope, cos[:, None, :], sin[:, None, :])
    k_rope = apply_rope(kr, cos, sin)
    nq = s // C_QBLOCK
    blk = lambda t: jnp.swapaxes(t.reshape(b, nq, C_QBLOCK, *t.shape[2:]), 0, 1)
    scale = (C_NOPE + C_ROPE) ** -0.5

    def attend(inp):
        qn, qr = inp
        sc = (jnp.einsum('bqhe,bkhe->bhqk', qn, k_nope, preferred_element_type=f32)
              + jnp.einsum('bqhe,bke->bhqk', qr, k_rope, preferred_element_type=f32)) * scale
        p = jax.nn.softmax(sc, axis=-1)
        return jnp.einsum('bhqk,bkhe->bqhe', p, v.astype(f32))

    o = lax.map(attend, (blk(q_nope), blk(q_rope)))
    return jnp.swapaxes(o, 0, 1).reshape(b, s, C_WIDTH).astype(cq.dtype)


def rwkv7_scan(r, w, k, v, a, bb, reverse):
    b, s, h, n = r.shape
    xs = tuple(jnp.moveaxis(t, 1, 0) for t in (r, w, k, v, a, bb))

    def step(S, inp):
        r_t, w_t, k_t, v_t, a_t, b_t = inp
        sa = jnp.einsum('bhij,bhj->bhi', S, a_t)
        S = S * w_t[:, :, None, :] + sa[..., None] * b_t[:, :, None, :] + v_t[..., None] * k_t[:, :, None, :]
        return S, jnp.einsum('bhij,bhj->bhi', S, r_t)

    _, y = lax.scan(step, jnp.zeros((b, h, n, n), jnp.float32), xs, reverse=reverse)
    return jnp.moveaxis(y, 0, 1)


def rwkv7_mixer(dcols, mu, w0_f, w2_f, w0_b, w2_b, a0, a2, g2, k_k, k_a, r_k, ln_g, ln_b):
    b, s, _ = dcols.shape
    t = dcols.astype(jnp.float32)
    prev = jnp.pad(t, ((0, 0), (1, 0), (0, 0)))[:, :-1]
    nxt = jnp.pad(t, ((0, 0), (0, 1), (0, 0)))[:, 1:]
    t = t + mu * (0.5 * (prev + nxt) - t)
    r, k, v, zwf, zwb, za, zg = jnp.split(t, _cuts(D_SPLITS), axis=-1)
    heads = lambda u: u.reshape(b, s, D_HEADS, D_HEAD)

    def decay(z, w0, w2):
        w = -jax.nn.softplus(-(w0 + jnp.tanh(z) @ w2)) - 0.5
        return heads(jnp.exp(-jnp.exp(w)))

    dec_f = decay(zwf, w0_f, w2_f)
    dec_b = decay(zwb, w0_b, w2_b)
    a = jax.nn.sigmoid(a0 + za @ a2)
    g = jax.nn.sigmoid(zg) @ g2
    kk = heads(k * k_k)
    kk = kk / jnp.maximum(jnp.sqrt(jnp.sum(kk * kk, axis=-1, keepdims=True)), 1e-12)
    k = k * (1.0 + (a - 1.0) * k_a)
    r_h, k_h, v_h, a_h = heads(r), heads(k), heads(v), heads(a)
    aa, bb = -kk, kk * a_h
    y = rwkv7_scan(r_h, dec_f, k_h, v_h, aa, bb, False) + rwkv7_scan(r_h, dec_b, k_h, v_h, aa, bb, True)
    mean = jnp.mean(y, axis=-1, keepdims=True)
    var = jnp.mean(jnp.square(y - mean), axis=-1, keepdims=True)
    y = ((y - mean) * lax.rsqrt(var + D_LN_EPS)).reshape(b, s, D_WIDTH) * ln_g + ln_b
    y = y + (jnp.sum(r_h * k_h * r_k, axis=-1, keepdims=True) * v_h).reshape(b, s, D_WIDTH)
    return (y * g).astype(dcols.dtype)


def odd_mixer(h, w_in, w_out, q_norm, w_uq, kv_norm, w_ukv, mu, w0_f, w2_f, w0_b, w2_b,
              a0, a2, g2, k_k, k_a, r_k, ln_g, ln_b):
    cq, ckv, kr, dcols = jnp.split(h @ w_in, _cuts((C_Q_RANK, C_KV_RANK, C_ROPE, D_SHIFT)), axis=-1)
    yc = mla_attention(cq, ckv, kr, q_norm, w_uq, kv_norm, w_ukv)
    yd = rwkv7_mixer(dcols, mu, w0_f, w2_f, w0_b, w2_b, a0, a2, g2, k_k, k_a, r_k, ln_g, ln_b)
    return jnp.concatenate([yc, yd], axis=-1) @ w_out


def hier_moe(x, w_grp, b_grp, w_exp, b_exp, w_gate, w_up, w_down):
    b, s, d = x.shape
    f32 = jnp.float32
    t = x.reshape(b * s, d)
    g_prob = jax.nn.softmax(jnp.dot(t, w_grp, preferred_element_type=f32) + b_grp, axis=-1)
    p_grp, i_grp = lax.top_k(g_prob, 1)
    e_logits = (jnp.dot(t, w_exp, preferred_element_type=f32) + b_exp).reshape(-1, N_GROUPS, EXPERTS_PER_GROUP)
    e_logits = e_logits[jnp.arange(b * s), i_grp[:, 0]]
    top_l, top_i = lax.top_k(e_logits, TOP_K_IN_GROUP)
    wts = jax.nn.softmax(top_l, axis=-1) * p_grp
    expert_id = i_grp * EXPERTS_PER_GROUP + top_i
    combine = jnp.sum(jax.nn.one_hot(expert_id, N_EXPERTS, dtype=f32) * wts[..., None], axis=1)
    hg = jnp.einsum('td,edf->tef', t, w_gate)
    hu = jnp.einsum('td,edf->tef', t, w_up)
    hid = jax.nn.silu(hg) * hu * combine[..., None].astype(t.dtype)
    y = jnp.einsum('tef,efd->td', hid, w_down)
    return y.reshape(b, s, d).astype(x.dtype)


def setup_inputs(seed: int = 0) -> dict:
    key = jax.random.key(seed)
    keys = iter(jax.random.split(key, 64))

    def nrm(shape, scale):
        return jax.random.normal(next(keys), shape, jnp.float32) * scale

    def gain(shape):
        return 1.0 + nrm(shape, 0.02)

    def unif(shape, lo, hi):
        return jax.random.uniform(next(keys), shape, jnp.float32, lo, hi)

    D = D_MODEL
    return {
        "x_prompt": nrm((BATCH, SEQ, D), 1.0),
        "x_sample": nrm((DEC_BATCH, DEC_SEQ, D), 1.0),
        "rel_bias": nrm((N_BUCKETS, A_HEADS), 0.1),
        "norm_mix": gain((DEPTH, D)),
        "norm_ffn": gain((DEPTH, D)),
        "norm_final": gain((D,)),
        "ev_w_in": nrm((N_EVEN, D, EVEN_IN), D ** -0.5),
        "ev_w_out": nrm((N_EVEN, A_WIDTH + B_WIDTH, D), (A_WIDTH + B_WIDTH) ** -0.5),
        "ev_gla_w2_f": nrm((N_EVEN, B_GATE_RANK, B_KEYW), B_GATE_RANK ** -0.5),
        "ev_gla_b_f": nrm((N_EVEN, B_KEYW), 0.1),
        "ev_gla_w2_b": nrm((N_EVEN, B_GATE_RANK, B_KEYW), B_GATE_RANK ** -0.5),
        "ev_gla_b_b": nrm((N_EVEN, B_KEYW), 0.1),
        "ev_gla_onorm": gain((N_EVEN, B_DV)),
        "od_w_in": nrm((N_ODD, D, ODD_IN), D ** -0.5),
        "od_w_out": nrm((N_ODD, C_WIDTH + D_WIDTH, D), (C_WIDTH + D_WIDTH) ** -0.5),
        "od_q_norm": gain((N_ODD, C_Q_RANK)),
        "od_w_uq": nrm((N_ODD, C_Q_RANK, C_HEADS * (C_NOPE + C_ROPE)), C_Q_RANK ** -0.5),
        "od_kv_norm": gain((N_ODD, C_KV_RANK)),
        "od_w_ukv": nrm((N_ODD, C_KV_RANK, C_HEADS * (C_NOPE + C_V)), C_KV_RANK ** -0.5),
        "od_mu": unif((N_ODD, D_SHIFT), 0.0, 1.0),
        "od_w0_f": unif((N_ODD, D_WIDTH), -5.0, -1.0),
        "od_w2_f": nrm((N_ODD, D_W_RANK, D_WIDTH), 0.1),
        "od_w0_b": unif((N_ODD, D_WIDTH), -5.0, -1.0),
        "od_w2_b": nrm((N_ODD, D_W_RANK, D_WIDTH), 0.1),
        "od_a0": nrm((N_ODD, D_WIDTH), 0.1),
        "od_a2": nrm((N_ODD, D_A_RANK, D_WIDTH), D_A_RANK ** -0.5),
        "od_g2": nrm((N_ODD, D_G_RANK, D_WIDTH), D_G_RANK ** -0.5),
        "od_k_k": 0.85 + nrm((N_ODD, D_WIDTH), 0.05),
        "od_k_a": 1.0 + nrm((N_ODD, D_WIDTH), 0.05),
        "od_r_k": nrm((N_ODD, D_HEADS, D_HEAD), 0.1),
        "od_ln_g": gain((N_ODD, D_WIDTH)),
        "od_ln_b": nrm((N_ODD, D_WIDTH), 0.01),
        "moe_w_grp": nrm((DEPTH, D, N_GROUPS), D ** -0.5),
        "moe_b_grp": nrm((DEPTH, N_GROUPS), 0.01),
        "moe_w_exp": nrm((DEPTH, D, N_EXPERTS), D ** -0.5),
        "moe_b_exp": nrm((DEPTH, N_EXPERTS), 0.01),
        "moe_w_gate": nrm((DEPTH, N_EXPERTS, D, EXPERT_FF), D ** -0.5),
        "moe_w_up": nrm((DEPTH, N_EXPERTS, D, EXPERT_FF), D ** -0.5),
        "moe_w_down": nrm((DEPTH, N_EXPERTS, EXPERT_FF, D), EXPERT_FF ** -0.5),
    }


def reference(x_prompt, x_sample, rel_bias, norm_mix, norm_ffn, norm_final,
              ev_w_in, ev_w_out, ev_gla_w2_f, ev_gla_b_f, ev_gla_w2_b, ev_gla_b_b, ev_gla_onorm,
              od_w_in, od_w_out, od_q_norm, od_w_uq, od_kv_norm, od_w_ukv,
              od_mu, od_w0_f, od_w2_f, od_w0_b, od_w2_b, od_a0, od_a2, od_g2,
              od_k_k, od_k_a, od_r_k, od_ln_g, od_ln_b,
              moe_w_grp, moe_b_grp, moe_w_exp, moe_b_exp, moe_w_gate, moe_w_up, moe_w_down):
    def run(x):
        for i in range(DEPTH):
            j = i // 2
            h = rmsnorm(x, norm_mix[i])
            if i % 2 == 0:
                mix = even_mixer(h, rel_bias, ev_w_in[j], ev_w_out[j], ev_gla_w2_f[j], ev_gla_b_f[j],
                                 ev_gla_w2_b[j], ev_gla_b_b[j], ev_gla_onorm[j])
            else:
                mix = odd_mixer(h, od_w_in[j], od_w_out[j], od_q_norm[j], od_w_uq[j], od_kv_norm[j], od_w_ukv[j],
                                od_mu[j], od_w0_f[j], od_w2_f[j], od_w0_b[j], od_w2_b[j], od_a0[j], od_a2[j],
                                od_g2[j], od_k_k[j], od_k_a[j], od_r_k[j], od_ln_g[j], od_ln_b[j])
            x = x + mix
            x = x + hier_moe(rmsnorm(x, norm_ffn[i]), moe_w_grp[i], moe_b_grp[i], moe_w_exp[i], moe_b_exp[i],
                             moe_w_gate[i], moe_w_up[i], moe_w_down[i])
        return rmsnorm(x, norm_final)

    y_prompt = run(x_prompt)
    y_sample = run(x_sample)
    return (y_prompt, y_sample)
```

```python
import functools

import jax, jax.numpy as jnp
from jax import lax
import numpy as np
from jax.experimental import pallas as pl
from jax.experimental.pallas import tpu as pltpu

D_MODEL = 2048
DEPTH = 2

MIX_HALF = D_MODEL // 2
HEAD_DIM = 128
EPS = 1e-6
NEG = -1e30

A_HEADS = MIX_HALF // HEAD_DIM
A_WIDTH = A_HEADS * HEAD_DIM
A_PATTERNS = ((128, 1), (512, 4), (2048, 16))
A_BLOCK = 64
N_BUCKETS = 32
MAX_DISTANCE = 1024

B_HEADS = 4
B_DV = MIX_HALF // B_HEADS
B_DK = B_DV // 2
B_WIDTH = B_HEADS * B_DV
B_KEYW = B_HEADS * B_DK
B_GATE_RANK = 16
B_GATE_TAU = 16.0
B_CHUNK = 64

C_HEADS = MIX_HALF // 128
C_Q_RANK = 512
C_KV_RANK = 256
C_NOPE = 128
C_ROPE = 64
C_V = 128
C_WIDTH = C_HEADS * C_V
C_QBLOCK = 128
ROPE_THETA = 10000.0

D_HEAD = 64
D_HEADS = MIX_HALF // D_HEAD
D_WIDTH = D_HEADS * D_HEAD
D_W_RANK = 64
D_A_RANK = 64
D_G_RANK = 128
D_LN_EPS = 64e-5

N_GROUPS = 4
EXPERTS_PER_GROUP = 4
N_EXPERTS = N_GROUPS * EXPERTS_PER_GROUP
EXPERT_FF = 512
TOP_K_IN_GROUP = 2

EVEN_SPLITS = (A_WIDTH, A_WIDTH, A_WIDTH, B_KEYW, B_KEYW, B_WIDTH, B_WIDTH, B_GATE_RANK, B_GATE_RANK)
EVEN_IN = 3 * A_WIDTH + 2 * B_KEYW + 2 * B_WIDTH + 2 * B_GATE_RANK
D_SPLITS = (D_WIDTH, D_WIDTH, D_WIDTH, D_W_RANK, D_W_RANK, D_A_RANK, D_G_RANK)
D_SHIFT = 3 * D_WIDTH + 2 * D_W_RANK + D_A_RANK + D_G_RANK
C_IN = C_Q_RANK + C_KV_RANK + C_ROPE
ODD_IN = C_IN + D_SHIFT


def _cuts(sizes):
    out, acc = [], 0
    for n in sizes[:-1]:
        acc += n
        out.append(acc)
    return out


def rmsnorm(x, g):
    xf = x.astype(jnp.float32)
    y = xf * lax.rsqrt(jnp.mean(xf * xf, axis=-1, keepdims=True) + EPS)
    return (y * g.astype(jnp.float32)).astype(x.dtype)


def t5_bucket(rel):
    half = N_BUCKETS // 2
    exact = half // 2
    n = np.abs(rel)
    large = exact + (np.log(np.maximum(n, 1) / exact) / np.log(MAX_DISTANCE / exact) * (half - exact)).astype(np.int64)
    large = np.minimum(large, half - 1)
    return ((rel > 0) * half + np.where(n < exact, n, large)).astype(np.int32)


def dilated_attention(q, k, v, rel_bias):
    b, s, h, e = q.shape
    f32 = jnp.float32
    scale = e ** -0.5
    outs, lses = [], []
    for window, dil in A_PATTERNS:
        n = window // (2 * dil)
        L = s // dil
        nblk = -(-L // A_BLOCK)
        Lp = nblk * A_BLOCK
        span = A_BLOCK + 2 * n
        sub = lambda t: t.reshape(b, L, dil, h, e).transpose(0, 2, 1, 3, 4)
        qs = jnp.pad(sub(q), ((0, 0), (0, 0), (0, Lp - L), (0, 0), (0, 0))).reshape(b, dil, nblk, A_BLOCK, h, e)
        kv_pad = ((0, 0), (0, 0), (n, Lp - L + n), (0, 0), (0, 0))
        idx = np.arange(nblk)[:, None] * A_BLOCK + np.arange(span)[None, :]
        kb = jnp.pad(sub(k), kv_pad)[:, :, idx]
        vb = jnp.pad(sub(v), kv_pad)[:, :, idx]
        rel = np.arange(span)[None, :] - np.arange(A_BLOCK)[:, None] - n
        key_l = idx - n
        mask = (np.abs(rel)[None] <= n) & (key_l[:, None, :] >= 0) & (key_l[:, None, :] < L)
        bias = jnp.transpose(rel_bias[t5_bucket(rel * dil)], (2, 0, 1)).astype(f32)
        sc = jnp.einsum('bdcqhe,bdckhe->bdchqk', qs, kb, preferred_element_type=f32) * scale + bias
        sc = jnp.where(mask[:, None], sc, NEG)
        m = jnp.max(sc, axis=-1, keepdims=True)
        p = jnp.exp(sc - m)
        den = jnp.sum(p, axis=-1, keepdims=True)
        o = jnp.einsum('bdchqk,bdckhe->bdchqe', p, vb.astype(f32)) / den
        lse = (m + jnp.log(den))[..., 0]
        o = o.transpose(0, 2, 4, 1, 3, 5).reshape(b, Lp, dil, h, e)[:, :L].reshape(b, s, h, e)
        lse = lse.transpose(0, 2, 4, 1, 3).reshape(b, Lp, dil, h)[:, :L].reshape(b, s, h)
        outs.append(o)
        lses.append(lse)
    wts = jax.nn.softmax(jnp.stack(lses), axis=0)[..., None]
    out = jnp.sum(wts * jnp.stack(outs), axis=0)
    return out.reshape(b, s, h * e).astype(q.dtype)


def gla_chunked(q, k, v, la):
    b, s, h, dk = q.shape
    dv = v.shape[-1]
    n = s // B_CHUNK
    chunks = lambda t: t.reshape(b, n, B_CHUNK, h, t.shape[-1]).transpose(1, 0, 3, 2, 4)
    qc, kc, vc, gc = chunks(q), chunks(k), chunks(v), chunks(la)
    gcum = jnp.cumsum(gc, axis=3)
    gend = gcum[:, :, :, -1:, :]
    q_in = qc * jnp.exp(gcum)
    k_in = kc * jnp.exp(-gcum)
    k_out = kc * jnp.exp(gend - gcum)
    causal = jnp.tril(jnp.ones((B_CHUNK, B_CHUNK), dtype=bool))
    att = jnp.where(causal, jnp.einsum('nbhqd,nbhkd->nbhqk', q_in, k_in), 0.0)
    intra = jnp.einsum('nbhqk,nbhkv->nbhqv', att, vc)

    def step(state, inp):
        qi, ko, vi, ge = inp
        o = jnp.einsum('bhqd,bhdv->bhqv', qi, state)
        state = state * jnp.exp(ge)[:, :, 0, :, None] + jnp.einsum('bhkd,bhkv->bhdv', ko, vi)
        return state, o

    _, inter = lax.scan(step, jnp.zeros((b, h, dk, dv), jnp.float32), (q_in, k_out, vc, gend))
    return (intra + inter).transpose(1, 0, 3, 2, 4).reshape(b, s, h, dv)


def gla_mixer(q, k, v, g, zf, zb, w2_f, b_f, w2_b, b_b, onorm):
    b, s, _ = q.shape
    f32 = jnp.float32
    heads = lambda t, e: t.astype(f32).reshape(b, s, B_HEADS, e)
    q = heads(q, B_DK) * (B_DK ** -0.5)
    k = heads(k, B_DK)
    v = heads(v, B_DV)
    la_f = heads(jax.nn.log_sigmoid((zf @ w2_f + b_f).astype(f32)), B_DK) / B_GATE_TAU
    la_b = heads(jax.nn.log_sigmoid((zb @ w2_b + b_b).astype(f32)), B_DK) / B_GATE_TAU
    rev = lambda t: jnp.flip(t, axis=1)
    o_f = gla_chunked(q, k, v, la_f)
    o_b = rev(gla_chunked(rev(q), rev(k), rev(v), rev(la_b)))
    o = rmsnorm(o_f + o_b, onorm).reshape(b, s, B_WIDTH) * jax.nn.silu(g.astype(f32))
    return o.astype(g.dtype)


def even_mixer(h, rel_bias, w_in, w_out, w2_f, b_f, w2_b, b_b, onorm):
    b, s, _ = h.shape
    qa, ka, va, qb, kb, vb, gb, zf, zb = jnp.split(h @ w_in, _cuts(EVEN_SPLITS), axis=-1)
    heads_a = lambda t: t.reshape(b, s, A_HEADS, HEAD_DIM)
    ya = dilated_attention(heads_a(qa), heads_a(ka), heads_a(va), rel_bias)
    yb = gla_mixer(qb, kb, vb, gb, zf, zb, w2_f, b_f, w2_b, b_b, onorm)
    return jnp.concatenate([ya, yb], axis=-1) @ w_out


def rope_cos_sin(s, dim):
    inv = 1.0 / (ROPE_THETA ** (jnp.arange(0, dim, 2, dtype=jnp.float32) / dim))
    ang = jnp.arange(s, dtype=jnp.float32)[:, None] * inv[None, :]
    return jnp.cos(ang), jnp.sin(ang)


def apply_rope(x, cos, sin):
    x1, x2 = jnp.split(x.astype(jnp.float32), 2, axis=-1)
    return jnp.concatenate([x1 * cos - x2 * sin, x1 * sin + x2 * cos], axis=-1).astype(x.dtype)


def mla_attention(cq, ckv, kr, q_norm, w_uq, kv_norm, w_ukv):
    b, s, _ = cq.shape
    f32 = jnp.float32
    q = (rmsnorm(cq, q_norm) @ w_uq).reshape(b, s, C_HEADS, C_NOPE + C_ROPE)
    kv = (rmsnorm(ckv, kv_norm) @ w_ukv).reshape(b, s, C_HEADS, C_NOPE + C_V)
    q_nope, q_rope = q[..., :C_NOPE], q[..., C_NOPE:]
    k_nope, v = kv[..., :C_NOPE], kv[..., C_NOPE:]
    cos, sin = rope_cos_sin(s, C_ROPE)
    q_rope = apply_rope(q_rope, cos[:, None, :], sin[:, None, :])
    k_rope = apply_rope(kr, cos, sin)
    nq = s // C_QBLOCK
    blk = lambda t: jnp.swapaxes(t.reshape(b, nq, C_QBLOCK, *t.shape[2:]), 0, 1)
    scale = (C_NOPE + C_ROPE) ** -0.5

    def attend(inp):
        qn, qr = inp
        sc = (jnp.einsum('bqhe,bkhe->bhqk', qn, k_nope, preferred_element_type=f32)
              + jnp.einsum('bqhe,bke->bhqk', qr, k_rope, preferred_element_type=f32)) * scale
        p = jax.nn.softmax(sc, axis=-1)
        return jnp.einsum('bhqk,bkhe->bqhe', p, v.astype(f32))

    o = lax.map(attend, (blk(q_nope), blk(q_rope)))
    return jnp.swapaxes(o, 0, 1).reshape(b, s, C_WIDTH).astype(cq.dtype)


def rwkv7_scan(r, w, k, v, a, bb, reverse):
    b, s, h, n = r.shape
    xs = tuple(jnp.moveaxis(t, 1, 0) for t in (r, w, k, v, a, bb))

    def step(S, inp):
        r_t, w_t, k_t, v_t, a_t, b_t = inp
        sa = jnp.einsum('bhij,bhj->bhi', S, a_t)
        S = S * w_t[:, :, None, :] + sa[..., None] * b_t[:, :, None, :] + v_t[..., None] * k_t[:, :, None, :]
        return S, jnp.einsum('bhij,bhj->bhi', S, r_t)

    _, y = lax.scan(step, jnp.zeros((b, h, n, n), jnp.float32), xs, reverse=reverse)
    return jnp.moveaxis(y, 0, 1)


def rwkv7_mixer(dcols, mu, w0_f, w2_f, w0_b, w2_b, a0, a2, g2, k_k, k_a, r_k, ln_g, ln_b):
    b, s, _ = dcols.shape
    t = dcols.astype(jnp.float32)
    prev = jnp.pad(t, ((0, 0), (1, 0), (0, 0)))[:, :-1]
    nxt = jnp.pad(t, ((0, 0), (0, 1), (0, 0)))[:, 1:]
    t = t + mu * (0.5 * (prev + nxt) - t)
    r, k, v, zwf, zwb, za, zg = jnp.split(t, _cuts(D_SPLITS), axis=-1)
    heads = lambda u: u.reshape(b, s, D_HEADS, D_HEAD)

    def decay(z, w0, w2):
        w = -jax.nn.softplus(-(w0 + jnp.tanh(z) @ w2)) - 0.5
        return heads(jnp.exp(-jnp.exp(w)))

    dec_f = decay(zwf, w0_f, w2_f)
    dec_b = decay(zwb, w0_b, w2_b)
    a = jax.nn.sigmoid(a0 + za @ a2)
    g = jax.nn.sigmoid(zg) @ g2
    kk = heads(k * k_k)
    kk = kk / jnp.maximum(jnp.sqrt(jnp.sum(kk * kk, axis=-1, keepdims=True)), 1e-12)
    k = k * (1.0 + (a - 1.0) * k_a)
    r_h, k_h, v_h, a_h = heads(r), heads(k), heads(v), heads(a)
    aa, bb = -kk, kk * a_h
    y = rwkv7_scan(r_h, dec_f, k_h, v_h, aa, bb, False) + rwkv7_scan(r_h, dec_b, k_h, v_h, aa, bb, True)
    mean = jnp.mean(y, axis=-1, keepdims=True)
    var = jnp.mean(jnp.square(y - mean), axis=-1, keepdims=True)
    y = ((y - mean) * lax.rsqrt(var + D_LN_EPS)).reshape(b, s, D_WIDTH) * ln_g + ln_b
    y = y + (jnp.sum(r_h * k_h * r_k, axis=-1, keepdims=True) * v_h).reshape(b, s, D_WIDTH)
    return (y * g).astype(dcols.dtype)


def odd_mixer(h, w_in, w_out, q_norm, w_uq, kv_norm, w_ukv, mu, w0_f, w2_f, w0_b, w2_b,
              a0, a2, g2, k_k, k_a, r_k, ln_g, ln_b):
    cq, ckv, kr, dcols = jnp.split(h @ w_in, _cuts((C_Q_RANK, C_KV_RANK, C_ROPE, D_SHIFT)), axis=-1)
    yc = mla_attention(cq, ckv, kr, q_norm, w_uq, kv_norm, w_ukv)
    yd = rwkv7_mixer(dcols, mu, w0_f, w2_f, w0_b, w2_b, a0, a2, g2, k_k, k_a, r_k, ln_g, ln_b)
    return jnp.concatenate([yc, yd], axis=-1) @ w_out


def hier_moe(x, w_grp, b_grp, w_exp, b_exp, w_gate, w_up, w_down):
    b, s, d = x.shape
    f32 = jnp.float32
    t = x.reshape(b * s, d)
    g_prob = jax.nn.softmax(jnp.dot(t, w_grp, preferred_element_type=f32) + b_grp, axis=-1)
    p_grp, i_grp = lax.top_k(g_prob, 1)
    e_logits = (jnp.dot(t, w_exp, preferred_element_type=f32) + b_exp).reshape(-1, N_GROUPS, EXPERTS_PER_GROUP)
    e_logits = e_logits[jnp.arange(b * s), i_grp[:, 0]]
    top_l, top_i = lax.top_k(e_logits, TOP_K_IN_GROUP)
    wts = jax.nn.softmax(top_l, axis=-1) * p_grp
    expert_id = i_grp * EXPERTS_PER_GROUP + top_i
    combine = jnp.sum(jax.nn.one_hot(expert_id, N_EXPERTS, dtype=f32) * wts[..., None], axis=1)
    hg = jnp.einsum('td,edf->tef', t, w_gate)
    hu = jnp.einsum('td,edf->tef', t, w_up)
    hid = jax.nn.silu(hg) * hu * combine[..., None].astype(t.dtype)
    y = jnp.einsum('tef,efd->td', hid, w_down)
    return y.reshape(b, s, d).astype(x.dtype)


def _final_norm_kernel(x_ref, g_ref, o_ref):
    x = x_ref[...]
    y = x * lax.rsqrt(jnp.mean(x * x, axis=-1, keepdims=True) + EPS)
    o_ref[...] = y * g_ref[...]


def final_norm(x, g):
    t, d = x.shape
    tm = 512
    return pl.pallas_call(
        _final_norm_kernel,
        out_shape=jax.ShapeDtypeStruct((t, d), jnp.float32),
        grid=(t // tm,),
        in_specs=[pl.BlockSpec((tm, d), lambda i: (i, 0)), pl.BlockSpec((1, d), lambda i: (0, 0))],
        out_specs=pl.BlockSpec((tm, d), lambda i: (i, 0)),
        name="final_norm",
    )(x, g.reshape(1, d))


def kernel(x_prompt, x_sample, rel_bias, norm_mix, norm_ffn, norm_final, ev_w_in, ev_w_out, ev_gla_w2_f, ev_gla_b_f, ev_gla_w2_b, ev_gla_b_b, ev_gla_onorm, od_w_in, od_w_out, od_q_norm, od_w_uq, od_kv_norm, od_w_ukv, od_mu, od_w0_f, od_w2_f, od_w0_b, od_w2_b, od_a0, od_a2, od_g2, od_k_k, od_k_a, od_r_k, od_ln_g, od_ln_b, moe_w_grp, moe_b_grp, moe_w_exp, moe_b_exp, moe_w_gate, moe_w_up, moe_w_down):
    nb_p = x_prompt.shape[0]
    x = jnp.concatenate([x_prompt, x_sample], axis=0)
    for i in range(DEPTH):
        j = i // 2
        h = rmsnorm(x, norm_mix[i])
        if i % 2 == 0:
            mix = even_mixer(h, rel_bias, ev_w_in[j], ev_w_out[j], ev_gla_w2_f[j], ev_gla_b_f[j],
                             ev_gla_w2_b[j], ev_gla_b_b[j], ev_gla_onorm[j])
        else:
            mix = odd_mixer(h, od_w_in[j], od_w_out[j], od_q_norm[j], od_w_uq[j], od_kv_norm[j], od_w_ukv[j],
                            od_mu[j], od_w0_f[j], od_w2_f[j], od_w0_b[j], od_w2_b[j], od_a0[j], od_a2[j],
                            od_g2[j], od_k_k[j], od_k_a[j], od_r_k[j], od_ln_g[j], od_ln_b[j])
        x = x + mix
        x = x + hier_moe(rmsnorm(x, norm_ffn[i]), moe_w_grp[i], moe_b_grp[i], moe_w_exp[i], moe_b_exp[i],
                         moe_w_gate[i], moe_w_up[i], moe_w_down[i])
    b, s, d = x.shape
    y = final_norm(x.reshape(b * s, d), norm_final).reshape(b, s, d)
    return (y[:nb_p], y[nb_p:])
```

```python
import functools

import jax, jax.numpy as jnp
from jax import lax
import numpy as np
from jax.experimental import pallas as pl
from jax.experimental.pallas import tpu as pltpu

F32, BF16 = jnp.float32, jnp.bfloat16

D_MODEL = 2048
DEPTH = 2
MIX_HALF = D_MODEL // 2
HEAD_DIM = 128
EPS = 1e-6
NEG = -1e30

A_HEADS = MIX_HALF // HEAD_DIM
A_WIDTH = A_HEADS * HEAD_DIM
A_PATTERNS = ((128, 1), (512, 4), (2048, 16))
N_BUCKETS = 32
MAX_DISTANCE = 1024

B_HEADS = 4
B_DV = MIX_HALF // B_HEADS
B_DK = B_DV // 2
B_WIDTH = B_HEADS * B_DV
B_KEYW = B_HEADS * B_DK
B_GATE_RANK = 16
B_GATE_TAU = 16.0
B_CHUNK = 64

C_HEADS = MIX_HALF // 128
C_Q_RANK = 512
C_KV_RANK = 256
C_NOPE = 128
C_ROPE = 64
C_V = 128
C_WIDTH = C_HEADS * C_V
C_QK = 256
ROPE_THETA = 10000.0

D_HEAD = 64
D_HEADS = MIX_HALF // D_HEAD
D_WIDTH = D_HEADS * D_HEAD
D_W_RANK = 64
D_A_RANK = 64
D_G_RANK = 128
D_LN_EPS = 64e-5
D_SPLITS = (D_WIDTH, D_WIDTH, D_WIDTH, D_W_RANK, D_W_RANK, D_A_RANK, D_G_RANK)
D_SHIFT = 3 * D_WIDTH + 2 * D_W_RANK + D_A_RANK + D_G_RANK
PAIR = 2 * D_HEAD

N_GROUPS = 4
EXPERTS_PER_GROUP = 4
N_EXPERTS = N_GROUPS * EXPERTS_PER_GROUP
EXPERT_FF = 512

EVEN_IN = 3 * A_WIDTH + 2 * B_KEYW + 2 * B_WIDTH + 2 * B_GATE_RANK
C_IN = C_Q_RANK + C_KV_RANK + C_ROPE
ODD_IN = C_IN + D_SHIFT

LANE = 128
VMEM_LIMIT = 52 * 1024 * 1024


def _params(*sem):
    return pltpu.CompilerParams(dimension_semantics=sem, vmem_limit_bytes=VMEM_LIMIT)


def _round_up(n, m):
    return -(-n // m) * m


def _pick_tile(n, target):
    best = LANE
    for t in range(LANE, target + 1, LANE):
        if n % t == 0:
            best = t
    return best


def _norm_linear_kernel(x_ref, g_ref, w_ref, o_ref, xn_ref):
    @pl.when(pl.program_id(1) == 0)
    def _():
        x = x_ref[...]
        y = x * lax.rsqrt(jnp.mean(x * x, axis=-1, keepdims=True) + EPS) * g_ref[...]
        xn_ref[...] = y.astype(BF16)

    o_ref[...] = jnp.dot(xn_ref[...], w_ref[...], preferred_element_type=F32)


def norm_linear(x, g, w, *, tm=1024, tn_target=1024):
    t, k = x.shape
    n = w.shape[1]
    tn = _pick_tile(n, tn_target)
    return pl.pallas_call(
        _norm_linear_kernel,
        out_shape=jax.ShapeDtypeStruct((t, n), F32),
        grid=(t // tm, n // tn),
        in_specs=[pl.BlockSpec((tm, k), lambda i, j: (i, 0)),
                  pl.BlockSpec((1, k), lambda i, j: (0, 0)),
                  pl.BlockSpec((k, tn), lambda i, j: (0, j))],
        out_specs=pl.BlockSpec((tm, tn), lambda i, j: (i, j)),
        scratch_shapes=[pltpu.VMEM((tm, k), BF16)],
        compiler_params=_params("parallel", "arbitrary"),
        name="norm_linear",
    )(x, g.reshape(1, k), w)


def _out_proj_kernel(a_ref, b_ref, wa_ref, wb_ref, x_ref, o_ref):
    acc = jnp.dot(a_ref[...].astype(BF16), wa_ref[...], preferred_element_type=F32)
    acc += jnp.dot(b_ref[...].astype(BF16), wb_ref[...], preferred_element_type=F32)
    o_ref[...] = x_ref[...] + acc


def out_proj(a, b, w, x, *, tm=1024, tn=512):
    t, ka = a.shape
    kb = b.shape[1]
    n = w.shape[1]
    return pl.pallas_call(
        _out_proj_kernel,
        out_shape=jax.ShapeDtypeStruct((t, n), F32),
        grid=(t // tm, n // tn),
        in_specs=[pl.BlockSpec((tm, ka), lambda i, j: (i, 0)),
                  pl.BlockSpec((tm, kb), lambda i, j: (i, 0)),
                  pl.BlockSpec((ka, tn), lambda i, j: (0, j)),
                  pl.BlockSpec((kb, tn), lambda i, j: (0, j)),
                  pl.BlockSpec((tm, tn), lambda i, j: (i, j))],
        out_specs=pl.BlockSpec((tm, tn), lambda i, j: (i, j)),
        compiler_params=_params("parallel", "arbitrary"),
        name="out_proj",
    )(a, b, w[:ka], w[ka:], x)


def _final_norm_kernel(x_ref, g_ref, o_ref):
    x = x_ref[...]
    o_ref[...] = x * lax.rsqrt(jnp.mean(x * x, axis=-1, keepdims=True) + EPS) * g_ref[...]


def final_norm(x, g, *, tm=1024):
    t, d = x.shape
    return pl.pallas_call(
        _final_norm_kernel,
        out_shape=jax.ShapeDtypeStruct((t, d), F32),
        grid=(t // tm,),
        in_specs=[pl.BlockSpec((tm, d), lambda i: (i, 0)), pl.BlockSpec((1, d), lambda i: (0, 0))],
        out_specs=pl.BlockSpec((tm, d), lambda i: (i, 0)),
        compiler_params=_params("parallel"),
        name="final_norm",
    )(x, g.reshape(1, d))


def _attn_kernel(*refs, scale, nq, tq, has_bias):
    if has_bias:
        q_ref, k_ref, v_ref, bias_ref, o_ref = refs
    else:
        q_ref, k_ref, v_ref, o_ref = refs
    s_len = k_ref.shape[0]
    q = (q_ref[...] * scale).astype(BF16)
    s = lax.dot_general(q, k_ref[...].astype(BF16), (((1,), (1,)), ((), ())), preferred_element_type=F32)
    if has_bias:
        start = pl.multiple_of((nq - 1 - pl.program_id(2)) * tq, LANE)
        s = s + bias_ref[:, pl.ds(start, s_len)]
    m = jnp.max(s, axis=-1, keepdims=True)
    p = jnp.exp(s - m)
    den = jnp.sum(p, axis=-1, keepdims=True)
    o = jnp.dot(p.astype(BF16), v_ref[...].astype(BF16), preferred_element_type=F32)
    o_ref[...] = o / den


def attention(q, k, v, *, heads, dq, dv, q_off, k_off, v_off, scale, bias=None, tq=256):
    b, s, _ = q.shape
    nq = s // tq
    in_specs = [pl.BlockSpec((None, tq, dq), lambda bi, h, qi: (bi, qi, q_off + h)),
                pl.BlockSpec((None, s, dq), lambda bi, h, qi: (bi, 0, k_off + h)),
                pl.BlockSpec((None, s, dv), lambda bi, h, qi: (bi, 0, v_off + h))]
    args = [q, k, v]
    if bias is not None:
        in_specs.append(pl.BlockSpec((None, tq, 2 * s - tq), lambda bi, h, qi: (h, 0, 0)))
        args.append(bias)
    return pl.pallas_call(
        functools.partial(_attn_kernel, scale=scale, nq=nq, tq=tq, has_bias=bias is not None),
        out_shape=jax.ShapeDtypeStruct((b, s, heads * dv), F32),
        grid=(b, heads, nq),
        in_specs=in_specs,
        out_specs=pl.BlockSpec((None, tq, dv), lambda bi, h, qi: (bi, qi, h)),
        compiler_params=_params("parallel", "parallel", "arbitrary"),
        name="attention_bias" if bias is not None else "attention",
    )(*args)


def _t5_bucket(rel):
    half = N_BUCKETS // 2
    exact = half // 2
    n = np.abs(rel)
    large = exact + (np.log(np.maximum(n, 1) / exact) / np.log(MAX_DISTANCE / exact) * (half - exact)).astype(np.int64)
    large = np.minimum(large, half - 1)
    return ((rel > 0) * half + np.where(n < exact, n, large)).astype(np.int32)


def dilated_bias_table(rel_bias, s, tq):
    d = np.arange(2 * s - tq)[None, :] - np.arange(tq)[:, None] - (s - tq)
    count = np.zeros(d.shape, np.float32)
    for window, dil in A_PATTERNS:
        count += ((d % dil == 0) & (np.abs(d) <= (window // (2 * dil)) * dil)).astype(np.float32)
    logc = np.where(count > 0, np.log(np.maximum(count, 1.0)), NEG).astype(np.float32)
    table = rel_bias[_t5_bucket(d)]
    return jnp.transpose(table, (2, 0, 1)).astype(F32) + logc[None]


def _gla_kernel(q_ref, k_ref, v_ref, g_ref, z_ref, w2f_ref, w2b_ref, bf_ref, bb_ref, on_ref, o_ref,
                laf_ref, lab_ref, acc_ref, st_ref):
    s_len = q_ref.shape[0]
    c = B_CHUNK
    nchunk = s_len // c
    z = z_ref[...].astype(BF16)
    gate = lambda w2_ref, b_ref: jax.nn.log_sigmoid(
        jnp.dot(z, w2_ref[...], preferred_element_type=F32) + b_ref[...]) * (1.0 / B_GATE_TAU)
    laf_ref[...] = gate(w2f_ref, bf_ref)
    lab_ref[...] = gate(w2b_ref, bb_ref)

    ri = lax.broadcasted_iota(jnp.int32, (c, c), 0)
    ci = lax.broadcasted_iota(jnp.int32, (c, c), 1)
    hp = lax.Precision.HIGHEST

    def run(la_ref, backward, first):
        keep = (ri <= ci) if backward else (ri >= ci)
        tri = keep.astype(F32)
        st_ref[...] = jnp.zeros_like(st_ref)

        def chunk(n, carry):
            n = (nchunk - 1 - n) if backward else n
            rows = pl.ds(pl.multiple_of(n * c, c), c)
            la = la_ref[rows, :]
            gcum = jnp.dot(tri, la, preferred_element_type=F32, precision=hp)
            gend = jnp.sum(la, axis=0, keepdims=True)
            q_in = (q_ref[rows, :] * (B_DK ** -0.5) * jnp.exp(gcum)).astype(BF16)
            kc = k_ref[rows, :]
            k_in = (kc * jnp.exp(-gcum)).astype(BF16)
            k_out = (kc * jnp.exp(gend - gcum)).astype(BF16)
            vc = v_ref[rows, :].astype(BF16)
            att = lax.dot_general(q_in, k_in, (((1,), (1,)), ((), ())), preferred_element_type=F32)
            att = jnp.where(keep, att, 0.0).astype(BF16)
            state = st_ref[...]
            o = jnp.dot(att, vc, preferred_element_type=F32)
            o += lax.dot_general(q_in, state.astype(BF16), (((1,), (1,)), ((), ())), preferred_element_type=F32)
            upd = lax.dot_general(vc, k_out, (((0,), (0,)), ((), ())), preferred_element_type=F32)
            st_ref[...] = state * jnp.exp(gend) + upd
            if first:
                acc_ref[rows, :] = o
            else:
                acc_ref[rows, :] += o
            return carry

        lax.fori_loop(0, nchunk, chunk, 0)

    run(laf_ref, False, True)
    run(lab_ref, True, False)
    o = acc_ref[...]
    o = o * lax.rsqrt(jnp.mean(o * o, axis=-1, keepdims=True) + EPS) * on_ref[...]
    g = g_ref[...]
    o_ref[...] = o * (g * jax.nn.sigmoid(g))


def gla_mixer(proj, w2f, w2b, b_f, b_b, onorm, *, q_col, k_col, v_col, g_col, z_col):
    b, s, _ = proj.shape
    hm = lambda blk: (lambda bi, h: (bi, 0, blk + h))
    w2f_p = jnp.zeros((LANE, B_KEYW), F32).at[:B_GATE_RANK].set(w2f).astype(BF16)
    w2b_p = jnp.zeros((LANE, B_KEYW), F32).at[B_GATE_RANK:2 * B_GATE_RANK].set(w2b).astype(BF16)
    return pl.pallas_call(
        _gla_kernel,
        out_shape=jax.ShapeDtypeStruct((b, s, B_WIDTH), F32),
        grid=(b, B_HEADS),
        in_specs=[pl.BlockSpec((None, s, B_DK), hm(q_col // B_DK)),
                  pl.BlockSpec((None, s, B_DK), hm(k_col // B_DK)),
                  pl.BlockSpec((None, s, B_DV), hm(v_col // B_DV)),
                  pl.BlockSpec((None, s, B_DV), hm(g_col // B_DV)),
                  pl.BlockSpec((None, s, LANE), lambda bi, h: (bi, 0, z_col // LANE)),
                  pl.BlockSpec((LANE, B_DK), lambda bi, h: (0, h)),
                  pl.BlockSpec((LANE, B_DK), lambda bi, h: (0, h)),
                  pl.BlockSpec((1, B_DK), lambda bi, h: (0, h)),
                  pl.BlockSpec((1, B_DK), lambda bi, h: (0, h)),
                  pl.BlockSpec((1, B_DV), lambda bi, h: (0, 0))],
        out_specs=pl.BlockSpec((None, s, B_DV), lambda bi, h: (bi, 0, h)),
        scratch_shapes=[pltpu.VMEM((s, B_DK), F32), pltpu.VMEM((s, B_DK), F32),
                        pltpu.VMEM((s, B_DV), F32), pltpu.VMEM((B_DV, B_DK), F32)],
        compiler_params=_params("parallel", "arbitrary"),
        name="gla_mixer",
    )(proj, proj, proj, proj, proj, w2f_p, w2b_p, b_f.reshape(1, -1), b_b.reshape(1, -1), onorm.reshape(1, -1))


def _mla_up_kernel(cq_ref, ckv_ref, kr_ref, qn_ref, kvn_ref, wq_ref, wqr_ref, wkv_ref, cos_ref, sin_ref,
                   q_ref, k_ref, v_ref):
    def rms(x, g):
        return (x * lax.rsqrt(jnp.mean(x * x, axis=-1, keepdims=True) + EPS) * g).astype(BF16)

    cq = rms(cq_ref[...], qn_ref[...])
    ckv = rms(ckv_ref[...], kvn_ref[...])
    cos, sin = cos_ref[...], sin_ref[...]
    kr = kr_ref[...]
    k_rope = kr * cos + pltpu.roll(kr, LANE - C_ROPE, 1) * sin
    lane = lax.broadcasted_iota(jnp.int32, k_rope.shape, 1)
    k_rope = jnp.where(lane < C_ROPE, k_rope, 0.0)
    for h in range(C_HEADS):
        q = jnp.dot(cq, wq_ref[:, h * C_QK:(h + 1) * C_QK], preferred_element_type=F32)
        qp = jnp.dot(cq, wqr_ref[:, h * LANE:(h + 1) * LANE], preferred_element_type=F32)
        q_ref[:, h * C_QK:h * C_QK + C_NOPE] = q[:, :C_NOPE]
        q_ref[:, h * C_QK + C_NOPE:(h + 1) * C_QK] = q[:, C_NOPE:] * cos + qp * sin
        kv = jnp.dot(ckv, wkv_ref[:, h * 2 * LANE:(h + 1) * 2 * LANE], preferred_element_type=F32)
        k_ref[:, h * C_QK:h * C_QK + C_NOPE] = kv[:, :C_NOPE]
        k_ref[:, h * C_QK + C_NOPE:(h + 1) * C_QK] = k_rope
        v_ref[:, h * C_V:(h + 1) * C_V] = kv[:, C_NOPE:]


def _rot_half_cols(w):
    half = w.shape[-1] // 2
    return jnp.concatenate([-w[..., half:], w[..., :half]], axis=-1)


def mla_up(proj, q_norm, w_uq, kv_norm, w_ukv, cos, sin, *, kr_block, tm=512):
    t = proj.shape[0]
    wq = w_uq.reshape(C_Q_RANK, C_HEADS, C_NOPE + C_ROPE)
    wq_main = jnp.pad(wq, ((0, 0), (0, 0), (0, C_QK - C_NOPE - C_ROPE))).reshape(C_Q_RANK, C_HEADS * C_QK)
    wq_rot = jnp.pad(_rot_half_cols(wq[..., C_NOPE:]), ((0, 0), (0, 0), (0, LANE - C_ROPE)))
    wq_rot = wq_rot.reshape(C_Q_RANK, C_HEADS * LANE)
    row = lambda i: (i, 0)
    full = lambda arr: pl.BlockSpec(arr.shape, lambda i: (0, 0))
    g_q, g_kv = q_norm.reshape(1, -1), kv_norm.reshape(1, -1)
    wq_main, wq_rot, wkv = wq_main.astype(BF16), wq_rot.astype(BF16), w_ukv.astype(BF16)
    return pl.pallas_call(
        _mla_up_kernel,
        out_shape=(jax.ShapeDtypeStruct((t, C_HEADS * C_QK), F32),
                   jax.ShapeDtypeStruct((t, C_HEADS * C_QK), F32),
                   jax.ShapeDtypeStruct((t, C_WIDTH), F32)),
        grid=(t // tm,),
        in_specs=[pl.BlockSpec((tm, C_Q_RANK), row),
                  pl.BlockSpec((tm, C_KV_RANK), lambda i: (i, C_Q_RANK // C_KV_RANK)),
                  pl.BlockSpec((tm, LANE), lambda i: (i, kr_block)),
                  full(g_q), full(g_kv), full(wq_main), full(wq_rot), full(wkv),
                  pl.BlockSpec((tm, LANE), row), pl.BlockSpec((tm, LANE), row)],
        out_specs=(pl.BlockSpec((tm, C_HEADS * C_QK), row),
                   pl.BlockSpec((tm, C_HEADS * C_QK), row),
                   pl.BlockSpec((tm, C_WIDTH), row)),
        compiler_params=_params("parallel"),
        name="mla_up",
    )(proj, proj, proj, g_q, g_kv, wq_main, wq_rot, wkv, cos, sin)


def _rwkv_scan_kernel(r_ref, w_ref, k_ref, v_ref, a_ref, b_ref, sel_ref, bo_ref, y_ref, s_ref, *,
                      tb, npairs, reverse):
    @pl.when(pl.program_id(2) == 0)
    def _():
        s_ref[...] = jnp.zeros_like(s_ref)

    lane = lax.broadcasted_iota(jnp.int32, (D_HEAD, PAIR), 1)
    row = lax.broadcasted_iota(jnp.int32, (D_HEAD, PAIR), 0)
    left = lane < D_HEAD
    eye = (lane & (D_HEAD - 1)) == row

    def segsum(p):
        sl = jnp.sum(jnp.where(left, p, 0.0), axis=1, keepdims=True)
        sr = jnp.sum(jnp.where(left, 0.0, p), axis=1, keepdims=True)
        return jnp.where(left, sl, sr)

    def split3(x):
        hi = x.astype(BF16)
        r1 = x - hi.astype(F32)
        mid = r1.astype(BF16)
        lo = (r1 - mid.astype(F32)).astype(BF16)
        return hi, mid, lo

    nsub = tb // 8
    sel = sel_ref[...]
    bo = bo_ref[...]

    def group(i, carry):
        i = (nsub - 1 - i) if reverse else i
        t0 = pl.multiple_of(i * 8, 8)
        lanes = [slice(PAIR * g, PAIR * (g + 1)) for g in range(npairs)]
        ld = lambda ref, g: ref[pl.ds(t0, 8), lanes[g]]
        a8 = [ld(a_ref, g) for g in range(npairs)]
        w8 = [ld(w_ref, g) for g in range(npairs)]
        b8 = [ld(b_ref, g) for g in range(npairs)]
        k8 = [ld(k_ref, g) for g in range(npairs)]
        r8 = [ld(r_ref, g) for g in range(npairs)]
        vt8 = []
        for g in range(npairs):
            v8 = ld(v_ref, g)
            vs = jnp.concatenate([v8[:, :D_HEAD], v8[:, D_HEAD:]], axis=0)
            lhs_t = jnp.concatenate(split3(vs), axis=0)
            vt8.append(lax.dot_general(lhs_t, sel, (((0,), (0,)), ((), ())), preferred_element_type=F32))
        rows = [[None] * 8 for _ in range(npairs)]
        for s in (range(7, -1, -1) if reverse else range(8)):
            row_of = lambda x: x[s:s + 1, :]
            for g in range(npairs):
                S = s_ref[g]
                sa = segsum(S * row_of(a8[g]))
                vt = vt8[g][:, PAIR * s:PAIR * (s + 1)]
                S = S * row_of(w8[g]) + sa * row_of(b8[g]) + vt * row_of(k8[g])
                s_ref[g] = S
                yt = jnp.dot((S * row_of(r8[g])).astype(BF16), bo, preferred_element_type=F32)
                rows[g][s] = jnp.sum(jnp.where(eye, yt, 0.0), axis=0, keepdims=True)
        for g in range(npairs):
            y_ref[pl.ds(t0, 8), lanes[g]] = jnp.concatenate(rows[g], axis=0)
        return carry

    lax.fori_loop(0, nsub, group, 0)


def _scan_constants():
    h = np.arange(2)[:, None, None, None]
    sp = np.arange(8)[None, :, None, None]
    st = np.arange(8)[None, None, :, None]
    ln = np.arange(PAIR)[None, None, None, :]
    sel = ((sp == st) & ((ln // D_HEAD) == h)).astype(np.float32).reshape(16, 8 * PAIR)
    sel = np.concatenate([sel, sel, sel], axis=0)
    ln = np.arange(PAIR)
    bo = ((ln[:, None] // D_HEAD) == (ln[None, :] // D_HEAD)).astype(np.float32)
    return jnp.asarray(sel, BF16), jnp.asarray(bo, BF16)


def rwkv_scan(r, w, k, v, a, b, *, reverse, tb=128, npairs=8):
    bsz, t, wd = r.shape
    ln = PAIR * npairs
    nt = t // tb
    tmap = (lambda it: nt - 1 - it) if reverse else (lambda it: it)
    xspec = pl.BlockSpec((None, tb, ln), lambda bi, g, it: (bi, tmap(it), g))
    sel, bo = _scan_constants()
    cspec = lambda arr: pl.BlockSpec(arr.shape, lambda bi, g, it: (0, 0))
    return pl.pallas_call(
        functools.partial(_rwkv_scan_kernel, tb=tb, npairs=npairs, reverse=reverse),
        out_shape=jax.ShapeDtypeStruct((bsz, t, wd), F32),
        grid=(bsz, wd // ln, nt),
        in_specs=[xspec] * 6 + [cspec(sel), cspec(bo)],
        out_specs=xspec,
        scratch_shapes=[pltpu.VMEM((npairs, D_HEAD, PAIR), F32)],
        compiler_params=_params("parallel", "parallel", "arbitrary"),
        name="rwkv_scan_bwd" if reverse else "rwkv_scan_fwd",
    )(r, w, k, v, a, b, sel, bo)


def rwkv7_mixer(dcols, mu, w0_f, w2_f, w0_b, w2_b, a0, a2, g2, k_k, k_a, r_k, ln_g, ln_b):
    b, s, _ = dcols.shape
    t = dcols
    prev = jnp.pad(t, ((0, 0), (1, 0), (0, 0)))[:, :-1]
    nxt = jnp.pad(t, ((0, 0), (0, 1), (0, 0)))[:, 1:]
    t = t + mu * (0.5 * (prev + nxt) - t)
    cuts = np.cumsum(D_SPLITS)[:-1].tolist()
    r, k, v, zwf, zwb, za, zg = jnp.split(t, cuts, axis=-1)
    heads = lambda u: u.reshape(b, s, D_HEADS, D_HEAD)

    def decay(z, w0, w2):
        w = -jax.nn.softplus(-(w0 + jnp.tanh(z) @ w2)) - 0.5
        return jnp.exp(-jnp.exp(w))

    dec_f = decay(zwf, w0_f, w2_f)
    dec_b = decay(zwb, w0_b, w2_b)
    a = jax.nn.sigmoid(a0 + za @ a2)
    g = jax.nn.sigmoid(zg) @ g2
    kk = heads(k * k_k)
    kk = (kk / jnp.maximum(jnp.sqrt(jnp.sum(kk * kk, axis=-1, keepdims=True)), 1e-12)).reshape(b, s, D_WIDTH)
    k = k * (1.0 + (a - 1.0) * k_a)
    aa, bb = -kk, kk * a
    y = rwkv_scan(r, dec_f, k, v, aa, bb, reverse=False) + rwkv_scan(r, dec_b, k, v, aa, bb, reverse=True)
    y = heads(y)
    r_h, k_h, v_h = heads(r), heads(k), heads(v)
    mean = jnp.mean(y, axis=-1, keepdims=True)
    var = jnp.mean(jnp.square(y - mean), axis=-1, keepdims=True)
    y = ((y - mean) * lax.rsqrt(var + D_LN_EPS)).reshape(b, s, D_WIDTH) * ln_g + ln_b
    y = y + (jnp.sum(r_h * k_h * r_k, axis=-1, keepdims=True) * v_h).reshape(b, s, D_WIDTH)
    return y * g


def _route_kernel(x_ref, g_ref, wr_ref, br_ref, hn_ref, comb_ref):
    x = x_ref[...]
    hn = (x * lax.rsqrt(jnp.mean(x * x, axis=-1, keepdims=True) + EPS) * g_ref[...]).astype(BF16)
    hn_ref[...] = hn
    logit = jnp.dot(hn, wr_ref[...], preferred_element_type=F32) + br_ref[...]
    lane = lax.broadcasted_iota(jnp.int32, logit.shape, 1)
    big = jnp.int32(LANE)
    first_at = lambda mask: jnp.min(jnp.where(mask, lane, big), axis=-1, keepdims=True)
    is_grp = lane < N_GROUPS
    gl = jnp.where(is_grp, logit, NEG)
    gmax = jnp.max(gl, axis=-1, keepdims=True)
    p_grp = 1.0 / jnp.sum(jnp.where(is_grp, jnp.exp(gl - gmax), 0.0), axis=-1, keepdims=True)
    i_grp = first_at(is_grp & (gl == gmax))
    lo = N_GROUPS + i_grp * EXPERTS_PER_GROUP
    in_grp = (lane >= lo) & (lane < lo + EXPERTS_PER_GROUP)
    el = jnp.where(in_grp, logit, NEG)
    l1 = jnp.max(el, axis=-1, keepdims=True)
    i1 = first_at(in_grp & (el == l1))
    rest = in_grp & (lane != i1)
    el2 = jnp.where(rest, logit, NEG)
    l2 = jnp.max(el2, axis=-1, keepdims=True)
    i2 = first_at(rest & (el2 == l2))
    e2 = jnp.exp(l2 - l1)
    w1 = p_grp / (1.0 + e2)
    w2 = p_grp * e2 / (1.0 + e2)
    comb_ref[...] = jnp.where(lane == i1, w1, jnp.where(lane == i2, w2, 0.0))


def moe_route(x, g, w_grp, b_grp, w_exp, b_exp, *, tm=512):
    t, d = x.shape
    nr = N_GROUPS + N_EXPERTS
    wr = jnp.pad(jnp.concatenate([w_grp, w_exp], axis=1), ((0, 0), (0, LANE - nr))).astype(BF16)
    br = jnp.pad(jnp.concatenate([b_grp, b_exp]), (0, LANE - nr)).reshape(1, LANE)
    hn, comb = pl.pallas_call(
        _route_kernel,
        out_shape=(jax.ShapeDtypeStruct((t, d), BF16), jax.ShapeDtypeStruct((t, LANE), F32)),
        grid=(t // tm,),
        in_specs=[pl.BlockSpec((tm, d), lambda i: (i, 0)),
                  pl.BlockSpec((1, d), lambda i: (0, 0)),
                  pl.BlockSpec((d, LANE), lambda i: (0, 0)),
                  pl.BlockSpec((1, LANE), lambda i: (0, 0))],
        out_specs=(pl.BlockSpec((tm, d), lambda i: (i, 0)), pl.BlockSpec((tm, LANE), lambda i: (i, 0))),
        compiler_params=_params("parallel"),
        name="moe_route",
    )(x, g.reshape(1, d), wr, br)
    return hn, comb[:, N_GROUPS:nr]


def _moe_kernel(hn_ref, c_ref, wg_ref, wu_ref, wd_ref, x_ref, o_ref):
    e = pl.program_id(1)

    @pl.when(e == 0)
    def _():
        o_ref[...] = x_ref[...]

    hn = hn_ref[...]
    hg = jnp.dot(hn, wg_ref[...], preferred_element_type=F32)
    hu = jnp.dot(hn, wu_ref[...], preferred_element_type=F32)
    hid = (hg * jax.nn.sigmoid(hg)) * hu * c_ref[...]
    o_ref[...] += jnp.dot(hid.astype(BF16), wd_ref[...], preferred_element_type=F32)


def moe_experts(hn, comb, w_gate, w_up, w_down, x, *, tm=512):
    t, d = hn.shape
    ne, _, ff = w_gate.shape
    comb_e = jnp.transpose(comb)[:, :, None]
    return pl.pallas_call(
        _moe_kernel,
        out_shape=jax.ShapeDtypeStruct((t, d), F32),
        grid=(t // tm, ne),
        in_specs=[pl.BlockSpec((tm, d), lambda i, e: (i, 0)),
                  pl.BlockSpec((None, tm, 1), lambda i, e: (e, i, 0)),
                  pl.BlockSpec((None, d, ff), lambda i, e: (e, 0, 0)),
                  pl.BlockSpec((None, d, ff), lambda i, e: (e, 0, 0)),
                  pl.BlockSpec((None, ff, d), lambda i, e: (e, 0, 0)),
                  pl.BlockSpec((tm, d), lambda i, e: (i, 0))],
        out_specs=pl.BlockSpec((tm, d), lambda i, e: (i, 0)),
        compiler_params=_params("parallel", "arbitrary"),
        name="moe_experts",
    )(hn, comb_e, w_gate, w_up, w_down, x)


def hier_moe(x, g, w_grp, b_grp, w_exp, b_exp, w_gate, w_up, w_down):
    hn, comb = moe_route(x, g, w_grp, b_grp, w_exp, b_exp)
    return moe_experts(hn, comb, w_gate.astype(BF16), w_up.astype(BF16), w_down.astype(BF16), x)


def even_layer(x, bsz, s, norm_g, rel_bias, w_in, w_out, w2_f, b_f, w2_b, b_b, onorm):
    n_pad = _round_up(EVEN_IN, LANE)
    w_in_p = jnp.pad(w_in, ((0, 0), (0, n_pad - EVEN_IN))).astype(BF16)
    proj = norm_linear(x, norm_g, w_in_p, tn_target=896).reshape(bsz, s, n_pad)
    tq = 256
    ya = attention(proj, proj, proj, heads=A_HEADS, dq=HEAD_DIM, dv=HEAD_DIM,
                   q_off=0, k_off=A_HEADS, v_off=2 * A_HEADS, scale=HEAD_DIM ** -0.5,
                   bias=dilated_bias_table(rel_bias, s, tq), tq=tq)
    q_col = 3 * A_WIDTH
    yb = gla_mixer(proj, w2_f, w2_b, b_f, b_b, onorm, q_col=q_col, k_col=q_col + B_KEYW,
                   v_col=q_col + 2 * B_KEYW, g_col=q_col + 2 * B_KEYW + B_WIDTH,
                   z_col=q_col + 2 * B_KEYW + 2 * B_WIDTH)
    t = bsz * s
    return out_proj(ya.reshape(t, A_WIDTH), yb.reshape(t, B_WIDTH), w_out.astype(BF16), x)


def odd_layer(x, bsz, s, norm_g, w_in, w_out, q_norm, w_uq, kv_norm, w_ukv, mu, w0_f, w2_f, w0_b, w2_b,
              a0, a2, g2, k_k, k_a, r_k, ln_g, ln_b):
    t = bsz * s
    w_kr = w_in[:, C_Q_RANK + C_KV_RANK:C_IN]
    w_ext = jnp.concatenate([w_in[:, :C_IN], _rot_half_cols(w_kr), w_in[:, C_IN:]], axis=1)
    n_pad = _round_up(w_ext.shape[1], 4 * LANE)
    w_ext = jnp.pad(w_ext, ((0, 0), (0, n_pad - w_ext.shape[1]))).astype(BF16)
    proj = norm_linear(x, norm_g, w_ext, tn_target=1152)
    inv = 1.0 / (ROPE_THETA ** (jnp.arange(0, C_ROPE, 2, dtype=F32) / C_ROPE))
    ang = jnp.arange(s, dtype=F32)[:, None] * inv[None, :]
    cos = jnp.pad(jnp.concatenate([jnp.cos(ang)] * 2, axis=1), ((0, 0), (0, LANE - C_ROPE)), constant_values=1.0)
    sin = jnp.pad(jnp.concatenate([jnp.sin(ang)] * 2, axis=1), ((0, 0), (0, LANE - C_ROPE)))
    q, k, v = mla_up(proj, q_norm, w_uq, kv_norm, w_ukv, jnp.tile(cos, (bsz, 1)), jnp.tile(sin, (bsz, 1)),
                     kr_block=(C_Q_RANK + C_KV_RANK) // LANE)
    r3 = lambda u: u.reshape(bsz, s, -1)
    yc = attention(r3(q), r3(k), r3(v), heads=C_HEADS, dq=C_QK, dv=C_V, q_off=0, k_off=0, v_off=0,
                   scale=(C_NOPE + C_ROPE) ** -0.5)
    d0 = C_IN + C_ROPE
    dcols = r3(proj[:, d0:d0 + D_SHIFT])
    yd = rwkv7_mixer(dcols, mu, w0_f, w2_f, w0_b, w2_b, a0, a2, g2, k_k, k_a, r_k, ln_g, ln_b)
    return out_proj(yc.reshape(t, C_WIDTH), yd.reshape(t, D_WIDTH), w_out.astype(BF16), x)


def kernel(x_prompt, x_sample, rel_bias, norm_mix, norm_ffn, norm_final, ev_w_in, ev_w_out, ev_gla_w2_f, ev_gla_b_f, ev_gla_w2_b, ev_gla_b_b, ev_gla_onorm, od_w_in, od_w_out, od_q_norm, od_w_uq, od_kv_norm, od_w_ukv, od_mu, od_w0_f, od_w2_f, od_w0_b, od_w2_b, od_a0, od_a2, od_g2, od_k_k, od_k_a, od_r_k, od_ln_g, od_ln_b, moe_w_grp, moe_b_grp, moe_w_exp, moe_b_exp, moe_w_gate, moe_w_up, moe_w_down):
    nb_p = x_prompt.shape[0]
    x = jnp.concatenate([x_prompt, x_sample], axis=0)
    bsz, s, d = x.shape
    x = x.reshape(bsz * s, d)
    for i in range(DEPTH):
        j = i // 2
        if i % 2 == 0:
            x = even_layer(x, bsz, s, norm_mix[i], rel_bias, ev_w_in[j], ev_w_out[j], ev_gla_w2_f[j],
                           ev_gla_b_f[j], ev_gla_w2_b[j], ev_gla_b_b[j], ev_gla_onorm[j])
        else:
            x = odd_layer(x, bsz, s, norm_mix[i], od_w_in[j], od_w_out[j], od_q_norm[j], od_w_uq[j],
                          od_kv_norm[j], od_w_ukv[j], od_mu[j], od_w0_f[j], od_w2_f[j], od_w0_b[j],
                          od_w2_b[j], od_a0[j], od_a2[j], od_g2[j], od_k_k[j], od_k_a[j], od_r_k[j],
                          od_ln_g[j], od_ln_b[j])
        x = hier_moe(x, norm_ffn[i], moe_w_grp[i], moe_b_grp[i], moe_w_exp[i], moe_b_exp[i],
                     moe_w_gate[i], moe_w_up[i], moe_w_down[i])
    y = final_norm(x, norm_final).reshape(bsz, s, d)
    return (y[:nb_p], y[nb_p:])
```

```python
import functools

import jax, jax.numpy as jnp
from jax import lax
import numpy as np
from jax.experimental import pallas as pl
from jax.experimental.pallas import tpu as pltpu

F32, BF16 = jnp.float32, jnp.bfloat16

D_MODEL = 2048
DEPTH = 2
MIX_HALF = D_MODEL // 2
HEAD_DIM = 128
EPS = 1e-6
NEG = -1e30

A_HEADS = MIX_HALF // HEAD_DIM
A_WIDTH = A_HEADS * HEAD_DIM
A_PATTERNS = ((128, 1), (512, 4), (2048, 16))
N_BUCKETS = 32
MAX_DISTANCE = 1024

B_HEADS = 4
B_DV = MIX_HALF // B_HEADS
B_DK = B_DV // 2
B_WIDTH = B_HEADS * B_DV
B_KEYW = B_HEADS * B_DK
B_GATE_RANK = 16
B_GATE_TAU = 16.0
B_CHUNK = 64

C_HEADS = MIX_HALF // 128
C_Q_RANK = 512
C_KV_RANK = 256
C_NOPE = 128
C_ROPE = 64
C_V = 128
C_WIDTH = C_HEADS * C_V
C_QK = 256
ROPE_THETA = 10000.0

D_HEAD = 64
D_HEADS = MIX_HALF // D_HEAD
D_WIDTH = D_HEADS * D_HEAD
D_W_RANK = 64
D_A_RANK = 64
D_G_RANK = 128
D_LN_EPS = 64e-5
D_SPLITS = (D_WIDTH, D_WIDTH, D_WIDTH, D_W_RANK, D_W_RANK, D_A_RANK, D_G_RANK)
D_SHIFT = 3 * D_WIDTH + 2 * D_W_RANK + D_A_RANK + D_G_RANK

N_GROUPS = 4
EXPERTS_PER_GROUP = 4
N_EXPERTS = N_GROUPS * EXPERTS_PER_GROUP
EXPERT_FF = 512

EVEN_IN = 3 * A_WIDTH + 2 * B_KEYW + 2 * B_WIDTH + 2 * B_GATE_RANK
C_IN = C_Q_RANK + C_KV_RANK + C_ROPE
ODD_IN = C_IN + D_SHIFT

LANE = 128
VMEM_LIMIT = 52 * 1024 * 1024


def _params(*sem):
    return pltpu.CompilerParams(dimension_semantics=sem, vmem_limit_bytes=VMEM_LIMIT)


def _round_up(n, m):
    return -(-n // m) * m


def _pick_tile(n, target):
    best = LANE
    for t in range(LANE, target + 1, LANE):
        if n % t == 0:
            best = t
    return best


def _norm_linear_kernel(x_ref, g_ref, w_ref, o_ref, xn_ref):
    @pl.when(pl.program_id(1) == 0)
    def _():
        x = x_ref[...]
        y = x * lax.rsqrt(jnp.mean(x * x, axis=-1, keepdims=True) + EPS) * g_ref[...]
        xn_ref[...] = y.astype(BF16)

    o_ref[...] = jnp.dot(xn_ref[...], w_ref[...], preferred_element_type=F32)


def norm_linear(x, g, w, *, tm=1024, tn_target=1024):
    t, k = x.shape
    n = w.shape[1]
    tn = _pick_tile(n, tn_target)
    return pl.pallas_call(
        _norm_linear_kernel,
        out_shape=jax.ShapeDtypeStruct((t, n), F32),
        grid=(t // tm, n // tn),
        in_specs=[pl.BlockSpec((tm, k), lambda i, j: (i, 0)),
                  pl.BlockSpec((1, k), lambda i, j: (0, 0)),
                  pl.BlockSpec((k, tn), lambda i, j: (0, j))],
        out_specs=pl.BlockSpec((tm, tn), lambda i, j: (i, j)),
        scratch_shapes=[pltpu.VMEM((tm, k), BF16)],
        compiler_params=_params("parallel", "arbitrary"),
        name="norm_linear",
    )(x, g.reshape(1, k), w)


def _out_proj_kernel(a_ref, b_ref, wa_ref, wb_ref, x_ref, o_ref):
    acc = jnp.dot(a_ref[...].astype(BF16), wa_ref[...], preferred_element_type=F32)
    acc += jnp.dot(b_ref[...].astype(BF16), wb_ref[...], preferred_element_type=F32)
    o_ref[...] = x_ref[...] + acc


def out_proj(a, b, w, x, *, tm=1024, tn=512):
    t, ka = a.shape
    kb = b.shape[1]
    n = w.shape[1]
    return pl.pallas_call(
        _out_proj_kernel,
        out_shape=jax.ShapeDtypeStruct((t, n), F32),
        grid=(t // tm, n // tn),
        in_specs=[pl.BlockSpec((tm, ka), lambda i, j: (i, 0)),
                  pl.BlockSpec((tm, kb), lambda i, j: (i, 0)),
                  pl.BlockSpec((ka, tn), lambda i, j: (0, j)),
                  pl.BlockSpec((kb, tn), lambda i, j: (0, j)),
                  pl.BlockSpec((tm, tn), lambda i, j: (i, j))],
        out_specs=pl.BlockSpec((tm, tn), lambda i, j: (i, j)),
        compiler_params=_params("parallel", "arbitrary"),
        name="out_proj",
    )(a, b, w[:ka], w[ka:], x)


def _final_norm_kernel(x_ref, g_ref, o_ref):
    x = x_ref[...]
    o_ref[...] = x * lax.rsqrt(jnp.mean(x * x, axis=-1, keepdims=True) + EPS) * g_ref[...]


def final_norm(x, g, *, tm=1024):
    t, d = x.shape
    return pl.pallas_call(
        _final_norm_kernel,
        out_shape=jax.ShapeDtypeStruct((t, d), F32),
        grid=(t // tm,),
        in_specs=[pl.BlockSpec((tm, d), lambda i: (i, 0)), pl.BlockSpec((1, d), lambda i: (0, 0))],
        out_specs=pl.BlockSpec((tm, d), lambda i: (i, 0)),
        compiler_params=_params("parallel"),
        name="final_norm",
    )(x, g.reshape(1, d))


def _attn_kernel(*refs, scale, nq, tq, has_bias):
    if has_bias:
        q_ref, k_ref, v_ref, bias_ref, o_ref = refs
    else:
        q_ref, k_ref, v_ref, o_ref = refs
    s_len = k_ref.shape[0]
    q = (q_ref[...] * scale).astype(BF16)
    s = lax.dot_general(q, k_ref[...].astype(BF16), (((1,), (1,)), ((), ())), preferred_element_type=F32)
    if has_bias:
        start = pl.multiple_of((nq - 1 - pl.program_id(2)) * tq, LANE)
        s = s + bias_ref[:, pl.ds(start, s_len)]
    m = jnp.max(s, axis=-1, keepdims=True)
    p = jnp.exp(s - m)
    den = jnp.sum(p, axis=-1, keepdims=True)
    o = jnp.dot(p.astype(BF16), v_ref[...].astype(BF16), preferred_element_type=F32)
    o_ref[...] = o / den


def attention(q, k, v, *, heads, dq, dv, q_off, k_off, v_off, scale, bias=None, tq=256):
    b, s, _ = q.shape
    nq = s // tq
    in_specs = [pl.BlockSpec((None, tq, dq), lambda bi, h, qi: (bi, qi, q_off + h)),
                pl.BlockSpec((None, s, dq), lambda bi, h, qi: (bi, 0, k_off + h)),
                pl.BlockSpec((None, s, dv), lambda bi, h, qi: (bi, 0, v_off + h))]
    args = [q, k, v]
    if bias is not None:
        in_specs.append(pl.BlockSpec((None, tq, 2 * s - tq), lambda bi, h, qi: (h, 0, 0)))
        args.append(bias)
    return pl.pallas_call(
        functools.partial(_attn_kernel, scale=scale, nq=nq, tq=tq, has_bias=bias is not None),
        out_shape=jax.ShapeDtypeStruct((b, s, heads * dv), F32),
        grid=(b, heads, nq),
        in_specs=in_specs,
        out_specs=pl.BlockSpec((None, tq, dv), lambda bi, h, qi: (bi, qi, h)),
        compiler_params=_params("parallel", "parallel", "arbitrary"),
        name="attention_bias" if bias is not None else "attention",
    )(*args)


def _t5_bucket(rel):
    half = N_BUCKETS // 2
    exact = half // 2
    n = np.abs(rel)
    large = exact + (np.log(np.maximum(n, 1) / exact) / np.log(MAX_DISTANCE / exact) * (half - exact)).astype(np.int64)
    large = np.minimum(large, half - 1)
    return ((rel > 0) * half + np.where(n < exact, n, large)).astype(np.int32)


def dilated_bias_table(rel_bias, s, tq):
    heads = rel_bias.shape[1]
    d = np.arange(-(s - 1), s)
    count = np.zeros(d.shape, np.float32)
    for window, dil in A_PATTERNS:
        count += ((d % dil == 0) & (np.abs(d) <= (window // (2 * dil)) * dil)).astype(np.float32)
    logc = np.where(count > 0, np.log(np.maximum(count, 1.0)), NEG).astype(np.float32)
    onehot = (_t5_bucket(d)[:, None] == np.arange(N_BUCKETS)[None, :]).astype(np.float32)
    line = jnp.transpose(jnp.dot(onehot, rel_bias.astype(F32), precision=lax.Precision.HIGHEST)) + logc[None]
    n = 2 * s - 1
    flat = jnp.broadcast_to(jnp.pad(line, ((0, 0), (0, 1)))[:, None, :], (heads, tq, n + 1)).reshape(heads, -1)
    skew = flat[:, :tq * n].reshape(heads, tq, n)
    return skew[:, :, tq - 1:tq - 1 + 2 * s - tq]


def _gla_kernel(q_ref, k_ref, v_ref, g_ref, z_ref, w2f_ref, w2b_ref, bf_ref, bb_ref, on_ref, o_ref,
                laf_ref, lab_ref, acc_ref, st_ref):
    s_len = q_ref.shape[0]
    c = B_CHUNK
    nchunk = s_len // c
    z = z_ref[...].astype(BF16)
    gate = lambda w2_ref, b_ref: jax.nn.log_sigmoid(
        jnp.dot(z, w2_ref[...], preferred_element_type=F32) + b_ref[...]) * (1.0 / B_GATE_TAU)
    laf_ref[...] = gate(w2f_ref, bf_ref)
    lab_ref[...] = gate(w2b_ref, bb_ref)

    ri = lax.broadcasted_iota(jnp.int32, (c, c), 0)
    ci = lax.broadcasted_iota(jnp.int32, (c, c), 1)
    hp = lax.Precision.HIGHEST

    def run(la_ref, backward, first):
        keep = (ri <= ci) if backward else (ri >= ci)
        tri = keep.astype(F32)
        st_ref[...] = jnp.zeros_like(st_ref)

        def chunk(n, carry):
            n = (nchunk - 1 - n) if backward else n
            rows = pl.ds(pl.multiple_of(n * c, c), c)
            la = la_ref[rows, :]
            gcum = jnp.dot(tri, la, preferred_element_type=F32, precision=hp)
            gend = jnp.sum(la, axis=0, keepdims=True)
            q_in = (q_ref[rows, :] * (B_DK ** -0.5) * jnp.exp(gcum)).astype(BF16)
            kc = k_ref[rows, :]
            k_in = (kc * jnp.exp(-gcum)).astype(BF16)
            k_out = (kc * jnp.exp(gend - gcum)).astype(BF16)
            vc = v_ref[rows, :].astype(BF16)
            att = lax.dot_general(q_in, k_in, (((1,), (1,)), ((), ())), preferred_element_type=F32)
            att = jnp.where(keep, att, 0.0).astype(BF16)
            state = st_ref[...]
            o = jnp.dot(att, vc, preferred_element_type=F32)
            o += lax.dot_general(q_in, state.astype(BF16), (((1,), (1,)), ((), ())), preferred_element_type=F32)
            upd = lax.dot_general(vc, k_out, (((0,), (0,)), ((), ())), preferred_element_type=F32)
            st_ref[...] = state * jnp.exp(gend) + upd
            if first:
                acc_ref[rows, :] = o
            else:
                acc_ref[rows, :] += o
            return carry

        lax.fori_loop(0, nchunk, chunk, 0)

    run(laf_ref, False, True)
    run(lab_ref, True, False)
    o = acc_ref[...]
    o = o * lax.rsqrt(jnp.mean(o * o, axis=-1, keepdims=True) + EPS) * on_ref[...]
    g = g_ref[...]
    o_ref[...] = o * (g * jax.nn.sigmoid(g))


def gla_mixer(proj, w2f, w2b, b_f, b_b, onorm, *, q_col, k_col, v_col, g_col, z_col):
    b, s, _ = proj.shape
    hm = lambda blk: (lambda bi, h: (bi, 0, blk + h))
    w2f_p = jnp.zeros((LANE, B_KEYW), F32).at[:B_GATE_RANK].set(w2f).astype(BF16)
    w2b_p = jnp.zeros((LANE, B_KEYW), F32).at[B_GATE_RANK:2 * B_GATE_RANK].set(w2b).astype(BF16)
    return pl.pallas_call(
        _gla_kernel,
        out_shape=jax.ShapeDtypeStruct((b, s, B_WIDTH), F32),
        grid=(b, B_HEADS),
        in_specs=[pl.BlockSpec((None, s, B_DK), hm(q_col // B_DK)),
                  pl.BlockSpec((None, s, B_DK), hm(k_col // B_DK)),
                  pl.BlockSpec((None, s, B_DV), hm(v_col // B_DV)),
                  pl.BlockSpec((None, s, B_DV), hm(g_col // B_DV)),
                  pl.BlockSpec((None, s, LANE), lambda bi, h: (bi, 0, z_col // LANE)),
                  pl.BlockSpec((LANE, B_DK), lambda bi, h: (0, h)),
                  pl.BlockSpec((LANE, B_DK), lambda bi, h: (0, h)),
                  pl.BlockSpec((1, B_DK), lambda bi, h: (0, h)),
                  pl.BlockSpec((1, B_DK), lambda bi, h: (0, h)),
                  pl.BlockSpec((1, B_DV), lambda bi, h: (0, 0))],
        out_specs=pl.BlockSpec((None, s, B_DV), lambda bi, h: (bi, 0, h)),
        scratch_shapes=[pltpu.VMEM((s, B_DK), F32), pltpu.VMEM((s, B_DK), F32),
                        pltpu.VMEM((s, B_DV), F32), pltpu.VMEM((B_DV, B_DK), F32)],
        compiler_params=_params("parallel", "arbitrary"),
        name="gla_mixer",
    )(proj, proj, proj, proj, proj, w2f_p, w2b_p, b_f.reshape(1, -1), b_b.reshape(1, -1), onorm.reshape(1, -1))


def _mla_up_kernel(cq_ref, ckv_ref, kr_ref, qn_ref, kvn_ref, wq_ref, wqr_ref, wkv_ref, cos_ref, sin_ref,
                   q_ref, k_ref, v_ref):
    def rms(x, g):
        return (x * lax.rsqrt(jnp.mean(x * x, axis=-1, keepdims=True) + EPS) * g).astype(BF16)

    cq = rms(cq_ref[...], qn_ref[...])
    ckv = rms(ckv_ref[...], kvn_ref[...])
    cos, sin = cos_ref[...], sin_ref[...]
    kr = kr_ref[...]
    k_rope = kr * cos + pltpu.roll(kr, LANE - C_ROPE, 1) * sin
    lane = lax.broadcasted_iota(jnp.int32, k_rope.shape, 1)
    k_rope = jnp.where(lane < C_ROPE, k_rope, 0.0)
    for h in range(C_HEADS):
        q = jnp.dot(cq, wq_ref[:, h * C_QK:(h + 1) * C_QK], preferred_element_type=F32)
        qp = jnp.dot(cq, wqr_ref[:, h * LANE:(h + 1) * LANE], preferred_element_type=F32)
        q_ref[:, h * C_QK:h * C_QK + C_NOPE] = q[:, :C_NOPE]
        q_ref[:, h * C_QK + C_NOPE:(h + 1) * C_QK] = q[:, C_NOPE:] * cos + qp * sin
        kv = jnp.dot(ckv, wkv_ref[:, h * 2 * LANE:(h + 1) * 2 * LANE], preferred_element_type=F32)
        k_ref[:, h * C_QK:h * C_QK + C_NOPE] = kv[:, :C_NOPE]
        k_ref[:, h * C_QK + C_NOPE:(h + 1) * C_QK] = k_rope
        v_ref[:, h * C_V:(h + 1) * C_V] = kv[:, C_NOPE:]


def _rot_half_cols(w):
    half = w.shape[-1] // 2
    return jnp.concatenate([-w[..., half:], w[..., :half]], axis=-1)


def mla_up(proj, q_norm, w_uq, kv_norm, w_ukv, cos, sin, *, col0, tm=512):
    t = proj.shape[0]
    wq = w_uq.reshape(C_Q_RANK, C_HEADS, C_NOPE + C_ROPE)
    wq_main = jnp.pad(wq, ((0, 0), (0, 0), (0, C_QK - C_NOPE - C_ROPE))).reshape(C_Q_RANK, C_HEADS * C_QK)
    wq_rot = jnp.pad(_rot_half_cols(wq[..., C_NOPE:]), ((0, 0), (0, 0), (0, LANE - C_ROPE)))
    wq_rot = wq_rot.reshape(C_Q_RANK, C_HEADS * LANE)
    row = lambda i: (i, 0)
    full = lambda arr: pl.BlockSpec(arr.shape, lambda i: (0, 0))
    g_q, g_kv = q_norm.reshape(1, -1), kv_norm.reshape(1, -1)
    wq_main, wq_rot, wkv = wq_main.astype(BF16), wq_rot.astype(BF16), w_ukv.astype(BF16)
    return pl.pallas_call(
        _mla_up_kernel,
        out_shape=(jax.ShapeDtypeStruct((t, C_HEADS * C_QK), F32),
                   jax.ShapeDtypeStruct((t, C_HEADS * C_QK), F32),
                   jax.ShapeDtypeStruct((t, C_WIDTH), F32)),
        grid=(t // tm,),
        in_specs=[pl.BlockSpec((tm, C_Q_RANK), lambda i: (i, col0 // C_Q_RANK)),
                  pl.BlockSpec((tm, C_KV_RANK), lambda i: (i, (col0 + C_Q_RANK) // C_KV_RANK)),
                  pl.BlockSpec((tm, LANE), lambda i: (i, (col0 + C_Q_RANK + C_KV_RANK) // LANE)),
                  full(g_q), full(g_kv), full(wq_main), full(wq_rot), full(wkv),
                  pl.BlockSpec((tm, LANE), row), pl.BlockSpec((tm, LANE), row)],
        out_specs=(pl.BlockSpec((tm, C_HEADS * C_QK), row),
                   pl.BlockSpec((tm, C_HEADS * C_QK), row),
                   pl.BlockSpec((tm, C_WIDTH), row)),
        compiler_params=_params("parallel"),
        name="mla_up",
    )(proj, proj, proj, g_q, g_kv, wq_main, wq_rot, wkv, cos, sin)


RG = 4
RGW = RG * D_HEAD
RCH = 64
NT_DIMS = (((1,), (1,)), ((), ()))
TN_DIMS = (((0,), (0,)), ((), ()))
DC_R, DC_K, DC_V = 0, D_WIDTH, 2 * D_WIDTH
DC_ZG = 3 * D_WIDTH
DC_ZW = DC_ZG + D_G_RANK
DC_ZA = DC_ZW + 2 * D_W_RANK
DC_PAD = 7 * 512


def _split3(x):
    hi = x.astype(BF16)
    r1 = x - hi.astype(F32)
    mid = r1.astype(BF16)
    lo = (r1 - mid.astype(F32)).astype(BF16)
    return hi, mid, lo


def _head_sums(x, bo):
    return jnp.concatenate(
        [jnp.dot(x[:, RGW * g:RGW * (g + 1)], bo, preferred_element_type=F32, precision=lax.Precision.HIGHEST)
         for g in range(x.shape[1] // RGW)], axis=1)


def _block_ones():
    i = np.arange(RGW)
    return jnp.asarray((i[:, None] // D_HEAD) == (i[None, :] // D_HEAD), F32)


def _rwkv_prep_kernel(x_ref, xp_ref, xn_ref, mu_ref, w2f_ref, w2b_ref, a2_ref, g2_ref, w0f_ref, w0b_ref,
                      a0_ref, kk_ref, ka_ref, rk_ref, bo_ref,
                      r_ref, k_ref, v_ref, a_ref, b_ref, lwf_ref, lwb_ref, g_ref, bonus_ref, *, tiles_per_seq):
    i = pl.program_id(0) % tiles_per_seq
    x = x_ref[...]
    tm = x.shape[0]
    row = lax.broadcasted_iota(jnp.int32, x.shape, 0)
    prev_row = jnp.where(i == 0, 0.0, xp_ref[7:8, :])
    next_row = jnp.where(i == tiles_per_seq - 1, 0.0, xn_ref[0:1, :])
    prev = jnp.where(row == 0, prev_row, pltpu.roll(x, 1, 0))
    nxt = jnp.where(row == tm - 1, next_row, pltpu.roll(x, tm - 1, 0))
    x = x + mu_ref[...] * (0.5 * (prev + nxt) - x)
    r, k, v = x[:, DC_R:DC_R + D_WIDTH], x[:, DC_K:DC_K + D_WIDTH], x[:, DC_V:DC_V + D_WIDTH]
    zg = x[:, DC_ZG:DC_ZG + LANE]
    zw = x[:, DC_ZW:DC_ZW + LANE]
    za = x[:, DC_ZA:DC_ZA + LANE]
    tz = jnp.tanh(zw).astype(BF16)
    log_decay = lambda w0_ref, w2_ref: -np.exp(-0.5).astype(np.float32) * jax.nn.sigmoid(
        w0_ref[...] + jnp.dot(tz, w2_ref[...], preferred_element_type=F32))
    lwf_ref[...] = log_decay(w0f_ref, w2f_ref)
    lwb_ref[...] = log_decay(w0b_ref, w2b_ref)
    ag = jax.nn.sigmoid(a0_ref[...] + jnp.dot(za.astype(BF16), a2_ref[...], preferred_element_type=F32))
    g_ref[...] = jnp.dot(jax.nn.sigmoid(zg).astype(BF16), g2_ref[...], preferred_element_type=F32)
    bo = bo_ref[...]
    kk = k * kk_ref[...]
    kk = kk / jnp.maximum(jnp.sqrt(_head_sums(kk * kk, bo)), 1e-12)
    k = k * (1.0 + (ag - 1.0) * ka_ref[...])
    r_ref[...] = r
    k_ref[...] = k
    v_ref[...] = v
    a_ref[...] = -kk
    b_ref[...] = kk * ag
    bonus_ref[...] = _head_sums(r * k * rk_ref[...], bo) * v


def rwkv_prep(proj, mu, w0_f, w2_f, w0_b, w2_b, a0, a2, g2, k_k, k_a, r_k, *, seq, tm=256):
    t = proj.shape[0]
    tiles_per_seq = seq // tm
    hb = tm // 8
    nblk8 = t // 8
    pad_rows = lambda w, lo: jnp.zeros((LANE, D_WIDTH), F32).at[lo:lo + w.shape[0]].set(w).astype(BF16)
    vec = lambda u: u.reshape(1, -1)
    consts = [vec(mu), pad_rows(w2_f, 0), pad_rows(w2_b, D_W_RANK), pad_rows(a2, 0), g2.astype(BF16),
              vec(w0_f), vec(w0_b), vec(a0), vec(k_k), vec(k_a), vec(r_k), _block_ones()]
    full = lambda arr: pl.BlockSpec(arr.shape, lambda i: (0, 0))
    out_spec = pl.BlockSpec((tm, D_WIDTH), lambda i: (i, 0))
    return pl.pallas_call(
        functools.partial(_rwkv_prep_kernel, tiles_per_seq=tiles_per_seq),
        out_shape=tuple(jax.ShapeDtypeStruct((t, D_WIDTH), F32) for _ in range(9)),
        grid=(t // tm,),
        in_specs=[pl.BlockSpec((tm, DC_PAD), lambda i: (i, 0)),
                  pl.BlockSpec((8, DC_PAD), lambda i: (jnp.maximum(i * hb - 1, 0), 0)),
                  pl.BlockSpec((8, DC_PAD), lambda i: (jnp.minimum((i + 1) * hb, nblk8 - 1), 0))]
                 + [full(c) for c in consts],
        out_specs=tuple(out_spec for _ in range(9)),
        compiler_params=_params("parallel"),
        name="rwkv_prep",
    )(proj, proj, proj, *consts)


def _rwkv_chunk_kernel(*refs, ngroups):
    ins, (yf_ref, yb_ref, mt_ref) = refs[:12], refs[12:]

    @pl.when(pl.program_id(1) == 0)
    def _():
        mt_ref[...] = jnp.zeros_like(mt_ref)

    row = lax.broadcasted_iota(jnp.int32, (RCH, RGW), 0)
    col = lax.broadcasted_iota(jnp.int32, (RCH, RGW), 1) & (RCH - 1)
    bdmask = (lax.broadcasted_iota(jnp.int32, (RGW, RGW), 0) // D_HEAD
              == lax.broadcasted_iota(jnp.int32, (RGW, RGW), 1) // D_HEAD)
    tr = lax.broadcasted_iota(jnp.int32, (RCH, RCH), 0)
    tc = lax.broadcasted_iota(jnp.int32, (RCH, RCH), 1)
    zero = jnp.zeros((), F32)

    def bd(z):
        zb = z.astype(BF16)
        return jnp.where(bdmask, jnp.concatenate([zb] * RG, axis=0), jnp.zeros((), BF16))

    def mm(x, y, dims=None):
        x = x.astype(BF16)
        if dims is None:
            return jnp.dot(x, y, preferred_element_type=F32)
        return lax.dot_general(x, y, dims, preferred_element_type=F32)

    chains = [(d, g) for d in range(2) for g in range(ngroups)]
    st = []
    for d, g in chains:
        backward = d == 1
        r_ref, k_ref, v_ref, a_ref, b_ref, lw_ref = ins[6 * d:6 * d + 6]
        tri = ((tc >= tr) if backward else (tc <= tr)).astype(BF16)
        sl = slice(RGW * g, RGW * (g + 1))
        r, k, v, a, b, lw = (ref[:, sl] for ref in (r_ref, k_ref, v_ref, a_ref, b_ref, lw_ref))
        lam = jnp.dot(jnp.concatenate([tri] * 3, axis=1), jnp.concatenate(_split3(lw), axis=0),
                      preferred_element_type=F32)
        lamc = lam[0:1] if backward else lam[RCH - 1:RCH]
        e_inv = jnp.exp(-lam)
        e_out = jnp.exp(lamc - lam)
        ar = jnp.concatenate([a * jnp.exp(lam - lw), r * jnp.exp(lam)], axis=0).astype(BF16)
        bk = jnp.concatenate([b * e_out, k * e_out], axis=0).astype(BF16)
        st.append(dict(ar=ar, bk=bk, v=v, lamc=lamc, sl=sl,
                       gb=mm(ar, bd(b * e_inv), NT_DIMS), gk=mm(ar, bd(k * e_inv), NT_DIMS)))
    for (d, g), c in zip(chains, st):
        strict = (col > row) if d == 1 else (col < row)
        incl = (col >= row) if d == 1 else (col <= row)
        c["lp"] = jnp.where(strict, c["gb"][:RCH], zero)
        lak = jnp.where(strict, c["gk"][:RCH], zero)
        c["grb"] = jnp.where(incl, c["gb"][RCH:], zero).astype(BF16)
        c["grk"] = jnp.where(incl, c["gk"][RCH:], zero).astype(BF16)
        c["mt"] = mt_ref[d, g]
        amrm = mm(c["ar"], c["mt"].astype(BF16), NT_DIMS)
        c["bdv"] = bd(c["v"])
        c["u"] = amrm[:RCH] + mm(lak, c["bdv"])
        c["rm"] = amrm[RCH:]
    for rnd in range(6):
        for c in st:
            lpb = c["lp"].astype(BF16)
            c["u"] = c["u"] + mm(lpb, bd(c["u"]))
            if rnd < 5:
                c["lp"] = mm(lpb, bd(c["lp"]))
    for (d, g), c in zip(chains, st):
        y_ref = yb_ref if d == 1 else yf_ref
        y_ref[:, c["sl"]] = c["rm"] + mm(c["grb"], bd(c["u"])) + mm(c["grk"], c["bdv"])
        uv = jnp.concatenate([c["u"], c["v"]], axis=0).astype(BF16)
        upd = lax.dot_general(uv, c["bk"], TN_DIMS, preferred_element_type=F32)
        mt_ref[d, g] = c["mt"] * jnp.exp(c["lamc"]) + jnp.where(bdmask, upd, zero)


def rwkv_chunked(r, k, v, a, b, lwf, lwb):
    bsz, s, wd = r.shape
    nc = s // RCH
    fspec = pl.BlockSpec((None, RCH, wd), lambda bi, n: (bi, n, 0))
    bspec = pl.BlockSpec((None, RCH, wd), lambda bi, n: (bi, nc - 1 - n, 0))
    return pl.pallas_call(
        functools.partial(_rwkv_chunk_kernel, ngroups=wd // RGW),
        out_shape=(jax.ShapeDtypeStruct((bsz, s, wd), F32), jax.ShapeDtypeStruct((bsz, s, wd), F32)),
        grid=(bsz, nc),
        in_specs=[fspec] * 6 + [bspec] * 6,
        out_specs=(fspec, bspec),
        scratch_shapes=[pltpu.VMEM((2, wd // RGW, RGW, RGW), F32)],
        compiler_params=_params("parallel", "arbitrary"),
        name="rwkv_chunked",
    )(r, k, v, a, b, lwf, r, k, v, a, b, lwb)


def _rwkv_post_kernel(yf_ref, yb_ref, bonus_ref, g_ref, lng_ref, lnb_ref, bo_ref, o_ref):
    bo = bo_ref[...]
    y = yf_ref[...] + yb_ref[...]
    yc = y - _head_sums(y, bo) * (1.0 / D_HEAD)
    var = _head_sums(yc * yc, bo) * (1.0 / D_HEAD)
    y = yc * lax.rsqrt(var + D_LN_EPS) * lng_ref[...] + lnb_ref[...]
    o_ref[...] = (y + bonus_ref[...]) * g_ref[...]


def rwkv_post(yf, yb, bonus, g, ln_g, ln_b, *, tm=512):
    t, wd = yf.shape
    row = pl.BlockSpec((tm, wd), lambda i: (i, 0))
    vec = pl.BlockSpec((1, wd), lambda i: (0, 0))
    bo = _block_ones()
    return pl.pallas_call(
        _rwkv_post_kernel,
        out_shape=jax.ShapeDtypeStruct((t, wd), F32),
        grid=(t // tm,),
        in_specs=[row, row, row, row, vec, vec, pl.BlockSpec(bo.shape, lambda i: (0, 0))],
        out_specs=row,
        compiler_params=_params("parallel"),
        name="rwkv_post",
    )(yf, yb, bonus, g, ln_g.reshape(1, wd), ln_b.reshape(1, wd), bo)


def rwkv7_mixer(proj, bsz, s, mu, w0_f, w2_f, w0_b, w2_b, a0, a2, g2, k_k, k_a, r_k, ln_g, ln_b):
    r, k, v, a, b, lwf, lwb, g, bonus = rwkv_prep(proj, mu, w0_f, w2_f, w0_b, w2_b, a0, a2, g2, k_k, k_a,
                                                  r_k.reshape(-1), seq=s)
    r3 = lambda u: u.reshape(bsz, s, D_WIDTH)
    yf, yb = rwkv_chunked(r3(r), r3(k), r3(v), r3(a), r3(b), r3(lwf), r3(lwb))
    return rwkv_post(yf.reshape(-1, D_WIDTH), yb.reshape(-1, D_WIDTH), bonus, g, ln_g, ln_b)


def _route_kernel(x_ref, g_ref, wr_ref, br_ref, hn_ref, comb_ref):
    x = x_ref[...]
    hn = (x * lax.rsqrt(jnp.mean(x * x, axis=-1, keepdims=True) + EPS) * g_ref[...]).astype(BF16)
    hn_ref[...] = hn
    logit = jnp.dot(hn, wr_ref[...], preferred_element_type=F32) + br_ref[...]
    lane = lax.broadcasted_iota(jnp.int32, logit.shape, 1)
    big = jnp.int32(LANE)
    first_at = lambda mask: jnp.min(jnp.where(mask, lane, big), axis=-1, keepdims=True)
    is_grp = lane < N_GROUPS
    gl = jnp.where(is_grp, logit, NEG)
    gmax = jnp.max(gl, axis=-1, keepdims=True)
    p_grp = 1.0 / jnp.sum(jnp.where(is_grp, jnp.exp(gl - gmax), 0.0), axis=-1, keepdims=True)
    i_grp = first_at(is_grp & (gl == gmax))
    lo = N_GROUPS + i_grp * EXPERTS_PER_GROUP
    in_grp = (lane >= lo) & (lane < lo + EXPERTS_PER_GROUP)
    el = jnp.where(in_grp, logit, NEG)
    l1 = jnp.max(el, axis=-1, keepdims=True)
    i1 = first_at(in_grp & (el == l1))
    rest = in_grp & (lane != i1)
    el2 = jnp.where(rest, logit, NEG)
    l2 = jnp.max(el2, axis=-1, keepdims=True)
    i2 = first_at(rest & (el2 == l2))
    e2 = jnp.exp(l2 - l1)
    w1 = p_grp / (1.0 + e2)
    w2 = p_grp * e2 / (1.0 + e2)
    comb_ref[...] = jnp.where(lane == i1, w1, jnp.where(lane == i2, w2, 0.0))


def moe_route(x, g, w_grp, b_grp, w_exp, b_exp, *, tm=512):
    t, d = x.shape
    nr = N_GROUPS + N_EXPERTS
    wr = jnp.pad(jnp.concatenate([w_grp, w_exp], axis=1), ((0, 0), (0, LANE - nr))).astype(BF16)
    br = jnp.pad(jnp.concatenate([b_grp, b_exp]), (0, LANE - nr)).reshape(1, LANE)
    hn, comb = pl.pallas_call(
        _route_kernel,
        out_shape=(jax.ShapeDtypeStruct((t, d), BF16), jax.ShapeDtypeStruct((t, LANE), F32)),
        grid=(t // tm,),
        in_specs=[pl.BlockSpec((tm, d), lambda i: (i, 0)),
                  pl.BlockSpec((1, d), lambda i: (0, 0)),
                  pl.BlockSpec((d, LANE), lambda i: (0, 0)),
                  pl.BlockSpec((1, LANE), lambda i: (0, 0))],
        out_specs=(pl.BlockSpec((tm, d), lambda i: (i, 0)), pl.BlockSpec((tm, LANE), lambda i: (i, 0))),
        compiler_params=_params("parallel"),
        name="moe_route",
    )(x, g.reshape(1, d), wr, br)
    return hn, comb[:, N_GROUPS:nr]


def _moe_kernel(hn_ref, c_ref, wg_ref, wu_ref, wd_ref, x_ref, o_ref):
    e = pl.program_id(1)

    @pl.when(e == 0)
    def _():
        o_ref[...] = x_ref[...]

    hn = hn_ref[...]
    hg = jnp.dot(hn, wg_ref[...], preferred_element_type=F32)
    hu = jnp.dot(hn, wu_ref[...], preferred_element_type=F32)
    hid = (hg * jax.nn.sigmoid(hg)) * hu * c_ref[...]
    o_ref[...] += jnp.dot(hid.astype(BF16), wd_ref[...], preferred_element_type=F32)


def moe_experts(hn, comb, w_gate, w_up, w_down, x, *, tm=512):
    t, d = hn.shape
    ne, _, ff = w_gate.shape
    comb_e = jnp.transpose(comb)[:, :, None]
    return pl.pallas_call(
        _moe_kernel,
        out_shape=jax.ShapeDtypeStruct((t, d), F32),
        grid=(t // tm, ne),
        in_specs=[pl.BlockSpec((tm, d), lambda i, e: (i, 0)),
                  pl.BlockSpec((None, tm, 1), lambda i, e: (e, i, 0)),
                  pl.BlockSpec((None, d, ff), lambda i, e: (e, 0, 0)),
                  pl.BlockSpec((None, d, ff), lambda i, e: (e, 0, 0)),
                  pl.BlockSpec((None, ff, d), lambda i, e: (e, 0, 0)),
                  pl.BlockSpec((tm, d), lambda i, e: (i, 0))],
        out_specs=pl.BlockSpec((tm, d), lambda i, e: (i, 0)),
        compiler_params=_params("parallel", "arbitrary"),
        name="moe_experts",
    )(hn, comb_e, w_gate, w_up, w_down, x)


def hier_moe(x, g, w_grp, b_grp, w_exp, b_exp, w_gate, w_up, w_down):
    hn, comb = moe_route(x, g, w_grp, b_grp, w_exp, b_exp)
    return moe_experts(hn, comb, w_gate.astype(BF16), w_up.astype(BF16), w_down.astype(BF16), x)


def even_layer(x, bsz, s, norm_g, rel_bias, w_in, w_out, w2_f, b_f, w2_b, b_b, onorm):
    n_pad = _round_up(EVEN_IN, LANE)
    w_in_p = jnp.pad(w_in, ((0, 0), (0, n_pad - EVEN_IN))).astype(BF16)
    proj = norm_linear(x, norm_g, w_in_p, tn_target=896).reshape(bsz, s, n_pad)
    tq = 256
    ya = attention(proj, proj, proj, heads=A_HEADS, dq=HEAD_DIM, dv=HEAD_DIM,
                   q_off=0, k_off=A_HEADS, v_off=2 * A_HEADS, scale=HEAD_DIM ** -0.5,
                   bias=dilated_bias_table(rel_bias, s, tq), tq=tq)
    q_col = 3 * A_WIDTH
    yb = gla_mixer(proj, w2_f, w2_b, b_f, b_b, onorm, q_col=q_col, k_col=q_col + B_KEYW,
                   v_col=q_col + 2 * B_KEYW, g_col=q_col + 2 * B_KEYW + B_WIDTH,
                   z_col=q_col + 2 * B_KEYW + 2 * B_WIDTH)
    t = bsz * s
    return out_proj(ya.reshape(t, A_WIDTH), yb.reshape(t, B_WIDTH), w_out.astype(BF16), x)


def _odd_columns(w_in, mu):
    c0 = C_IN
    cut = lambda u, lo, n: u[..., lo:lo + n]
    zpad = lambda u, n: jnp.pad(u, [(0, 0)] * (u.ndim - 1) + [(0, n)])
    off = np.cumsum((0,) + D_SPLITS)
    def rwkv_cols(u):
        parts = [cut(u, off[0], 3 * D_WIDTH), cut(u, off[6], D_G_RANK), cut(u, off[3], 2 * D_W_RANK),
                 cut(u, off[5], D_A_RANK)]
        u = jnp.concatenate(parts, axis=-1)
        return zpad(u, DC_PAD - u.shape[-1])
    w_kr = w_in[:, C_Q_RANK + C_KV_RANK:C_IN]
    w_all = jnp.concatenate([rwkv_cols(w_in[:, c0:]), w_in[:, :C_IN], _rot_half_cols(w_kr)], axis=1)
    n_pad = _round_up(w_all.shape[1], 9 * LANE)
    return zpad(w_all, n_pad - w_all.shape[1]).astype(BF16), rwkv_cols(mu)


def odd_layer(x, bsz, s, norm_g, w_in, w_out, q_norm, w_uq, kv_norm, w_ukv, mu, w0_f, w2_f, w0_b, w2_b,
              a0, a2, g2, k_k, k_a, r_k, ln_g, ln_b):
    t = bsz * s
    w_all, mu_cols = _odd_columns(w_in, mu)
    proj = norm_linear(x, norm_g, w_all, tn_target=1152)
    inv = 1.0 / (ROPE_THETA ** (jnp.arange(0, C_ROPE, 2, dtype=F32) / C_ROPE))
    ang = jnp.arange(s, dtype=F32)[:, None] * inv[None, :]
    cos = jnp.pad(jnp.concatenate([jnp.cos(ang)] * 2, axis=1), ((0, 0), (0, LANE - C_ROPE)), constant_values=1.0)
    sin = jnp.pad(jnp.concatenate([jnp.sin(ang)] * 2, axis=1), ((0, 0), (0, LANE - C_ROPE)))
    q, k, v = mla_up(proj, q_norm, w_uq, kv_norm, w_ukv, jnp.tile(cos, (bsz, 1)), jnp.tile(sin, (bsz, 1)),
                     col0=DC_PAD)
    r3 = lambda u: u.reshape(bsz, s, -1)
    yc = attention(r3(q), r3(k), r3(v), heads=C_HEADS, dq=C_QK, dv=C_V, q_off=0, k_off=0, v_off=0,
                   scale=(C_NOPE + C_ROPE) ** -0.5)
    yd = rwkv7_mixer(proj, bsz, s, mu_cols, w0_f, w2_f, w0_b, w2_b, a0, a2, g2, k_k, k_a, r_k, ln_g, ln_b)
    return out_proj(yc.reshape(t, C_WIDTH), yd, w_out.astype(BF16), x)


def kernel(x_prompt, x_sample, rel_bias, norm_mix, norm_ffn, norm_final, ev_w_in, ev_w_out, ev_gla_w2_f, ev_gla_b_f, ev_gla_w2_b, ev_gla_b_b, ev_gla_onorm, od_w_in, od_w_out, od_q_norm, od_w_uq, od_kv_norm, od_w_ukv, od_mu, od_w0_f, od_w2_f, od_w0_b, od_w2_b, od_a0, od_a2, od_g2, od_k_k, od_k_a, od_r_k, od_ln_g, od_ln_b, moe_w_grp, moe_b_grp, moe_w_exp, moe_b_exp, moe_w_gate, moe_w_up, moe_w_down):
    nb_p = x_prompt.shape[0]
    x = jnp.concatenate([x_prompt, x_sample], axis=0)
    bsz, s, d = x.shape
    x = x.reshape(bsz * s, d)
    for i in range(DEPTH):
        j = i // 2
        if i % 2 == 0:
            x = even_layer(x, bsz, s, norm_mix[i], rel_bias, ev_w_in[j], ev_w_out[j], ev_gla_w2_f[j],
                           ev_gla_b_f[j], ev_gla_w2_b[j], ev_gla_b_b[j], ev_gla_onorm[j])
        else:
            x = odd_layer(x, bsz, s, norm_mix[i], od_w_in[j], od_w_out[j], od_q_norm[j], od_w_uq[j],
                          od_kv_norm[j], od_w_ukv[j], od_mu[j], od_w0_f[j], od_w2_f[j], od_w0_b[j],
                          od_w2_b[j], od_a0[j], od_a2[j], od_g2[j], od_k_k[j], od_k_a[j], od_r_k[j],
                          od_ln_g[j], od_ln_b[j])
        x = hier_moe(x, norm_ffn[i], moe_w_grp[i], moe_b_grp[i], moe_w_exp[i], moe_b_exp[i],
                     moe_w_gate[i], moe_w_up[i], moe_w_down[i])
    y = final_norm(x, norm_final).reshape(bsz, s, d)
    return (y[:nb_p], y[nb_p:])
```

```python
import functools

import jax, jax.numpy as jnp
from jax import lax
import numpy as np
from jax.experimental import pallas as pl
from jax.experimental.pallas import tpu as pltpu

F32, BF16 = jnp.float32, jnp.bfloat16

D_MODEL = 2048
DEPTH = 2
MIX_HALF = D_MODEL // 2
HEAD_DIM = 128
EPS = 1e-6
NEG = -1e30

A_HEADS = MIX_HALF // HEAD_DIM
A_WIDTH = A_HEADS * HEAD_DIM
A_PATTERNS = ((128, 1), (512, 4), (2048, 16))
N_BUCKETS = 32
MAX_DISTANCE = 1024

B_HEADS = 4
B_DV = MIX_HALF // B_HEADS
B_DK = B_DV // 2
B_WIDTH = B_HEADS * B_DV
B_KEYW = B_HEADS * B_DK
B_GATE_RANK = 16
B_GATE_TAU = 16.0
B_CHUNK = 64

C_HEADS = MIX_HALF // 128
C_Q_RANK = 512
C_KV_RANK = 256
C_NOPE = 128
C_ROPE = 64
C_V = 128
C_WIDTH = C_HEADS * C_V
C_QK = 256
ROPE_THETA = 10000.0

D_HEAD = 64
D_HEADS = MIX_HALF // D_HEAD
D_WIDTH = D_HEADS * D_HEAD
D_W_RANK = 64
D_A_RANK = 64
D_G_RANK = 128
D_LN_EPS = 64e-5
D_SPLITS = (D_WIDTH, D_WIDTH, D_WIDTH, D_W_RANK, D_W_RANK, D_A_RANK, D_G_RANK)
D_SHIFT = 3 * D_WIDTH + 2 * D_W_RANK + D_A_RANK + D_G_RANK

N_GROUPS = 4
EXPERTS_PER_GROUP = 4
N_EXPERTS = N_GROUPS * EXPERTS_PER_GROUP
EXPERT_FF = 512

EVEN_IN = 3 * A_WIDTH + 2 * B_KEYW + 2 * B_WIDTH + 2 * B_GATE_RANK
C_IN = C_Q_RANK + C_KV_RANK + C_ROPE
ODD_IN = C_IN + D_SHIFT

LANE = 128
VMEM_LIMIT = 52 * 1024 * 1024


def _params(*sem):
    return pltpu.CompilerParams(dimension_semantics=sem, vmem_limit_bytes=VMEM_LIMIT)


def _round_up(n, m):
    return -(-n // m) * m


def _pick_tile(n, target):
    best = LANE
    for t in range(LANE, target + 1, LANE):
        if n % t == 0:
            best = t
    return best


def _norm_linear_kernel(x_ref, g_ref, w_ref, o_ref, xn_ref):
    @pl.when(pl.program_id(1) == 0)
    def _():
        x = x_ref[...]
        y = x * lax.rsqrt(jnp.mean(x * x, axis=-1, keepdims=True) + EPS) * g_ref[...]
        xn_ref[...] = y.astype(BF16)

    o_ref[...] = jnp.dot(xn_ref[...], w_ref[...], preferred_element_type=F32)


def norm_linear(x, g, w, *, tm=1024, tn_target=1024):
    t, k = x.shape
    n = w.shape[1]
    tn = _pick_tile(n, tn_target)
    return pl.pallas_call(
        _norm_linear_kernel,
        out_shape=jax.ShapeDtypeStruct((t, n), F32),
        grid=(t // tm, n // tn),
        in_specs=[pl.BlockSpec((tm, k), lambda i, j: (i, 0)),
                  pl.BlockSpec((1, k), lambda i, j: (0, 0)),
                  pl.BlockSpec((k, tn), lambda i, j: (0, j))],
        out_specs=pl.BlockSpec((tm, tn), lambda i, j: (i, j)),
        scratch_shapes=[pltpu.VMEM((tm, k), BF16)],
        compiler_params=_params("parallel", "arbitrary"),
        name="norm_linear",
    )(x, g.reshape(1, k), w)


def _out_proj_kernel(a_ref, b_ref, wa_ref, wb_ref, x_ref, o_ref):
    acc = jnp.dot(a_ref[...].astype(BF16), wa_ref[...], preferred_element_type=F32)
    acc += jnp.dot(b_ref[...].astype(BF16), wb_ref[...], preferred_element_type=F32)
    o_ref[...] = x_ref[...] + acc


def out_proj(a, b, w, x, *, tm=1024, tn=512):
    t, ka = a.shape
    kb = b.shape[1]
    n = w.shape[1]
    return pl.pallas_call(
        _out_proj_kernel,
        out_shape=jax.ShapeDtypeStruct((t, n), F32),
        grid=(t // tm, n // tn),
        in_specs=[pl.BlockSpec((tm, ka), lambda i, j: (i, 0)),
                  pl.BlockSpec((tm, kb), lambda i, j: (i, 0)),
                  pl.BlockSpec((ka, tn), lambda i, j: (0, j)),
                  pl.BlockSpec((kb, tn), lambda i, j: (0, j)),
                  pl.BlockSpec((tm, tn), lambda i, j: (i, j))],
        out_specs=pl.BlockSpec((tm, tn), lambda i, j: (i, j)),
        compiler_params=_params("parallel", "arbitrary"),
        name="out_proj",
    )(a, b, w[:ka], w[ka:], x)


def _final_norm_kernel(x_ref, g_ref, o_ref):
    x = x_ref[...]
    o_ref[...] = x * lax.rsqrt(jnp.mean(x * x, axis=-1, keepdims=True) + EPS) * g_ref[...]


def final_norm(x, g, *, tm=1024):
    t, d = x.shape
    return pl.pallas_call(
        _final_norm_kernel,
        out_shape=jax.ShapeDtypeStruct((t, d), F32),
        grid=(t // tm,),
        in_specs=[pl.BlockSpec((tm, d), lambda i: (i, 0)), pl.BlockSpec((1, d), lambda i: (0, 0))],
        out_specs=pl.BlockSpec((tm, d), lambda i: (i, 0)),
        compiler_params=_params("parallel"),
        name="final_norm",
    )(x, g.reshape(1, d))


def _attn_kernel(*refs, scale, nq, tq, has_bias):
    if has_bias:
        q_ref, k_ref, v_ref, bias_ref, o_ref = refs
    else:
        q_ref, k_ref, v_ref, o_ref = refs
    s_len = k_ref.shape[0]
    q = (q_ref[...] * scale).astype(BF16)
    s = lax.dot_general(q, k_ref[...].astype(BF16), (((1,), (1,)), ((), ())), preferred_element_type=F32)
    if has_bias:
        start = pl.multiple_of((nq - 1 - pl.program_id(2)) * tq, LANE)
        s = s + bias_ref[:, pl.ds(start, s_len)]
    m = jnp.max(s, axis=-1, keepdims=True)
    p = jnp.exp(s - m)
    den = jnp.sum(p, axis=-1, keepdims=True)
    o = jnp.dot(p.astype(BF16), v_ref[...].astype(BF16), preferred_element_type=F32)
    o_ref[...] = o / den


def attention(q, k, v, *, heads, dq, dv, q_off, k_off, v_off, scale, bias=None, tq=256):
    b, s, _ = q.shape
    nq = s // tq
    in_specs = [pl.BlockSpec((None, tq, dq), lambda bi, h, qi: (bi, qi, q_off + h)),
                pl.BlockSpec((None, s, dq), lambda bi, h, qi: (bi, 0, k_off + h)),
                pl.BlockSpec((None, s, dv), lambda bi, h, qi: (bi, 0, v_off + h))]
    args = [q, k, v]
    if bias is not None:
        in_specs.append(pl.BlockSpec((None, tq, 2 * s - tq), lambda bi, h, qi: (h, 0, 0)))
        args.append(bias)
    return pl.pallas_call(
        functools.partial(_attn_kernel, scale=scale, nq=nq, tq=tq, has_bias=bias is not None),
        out_shape=jax.ShapeDtypeStruct((b, s, heads * dv), F32),
        grid=(b, heads, nq),
        in_specs=in_specs,
        out_specs=pl.BlockSpec((None, tq, dv), lambda bi, h, qi: (bi, qi, h)),
        compiler_params=_params("parallel", "parallel", "arbitrary"),
        name="attention_bias" if bias is not None else "attention",
    )(*args)


def _t5_bucket(rel):
    half = N_BUCKETS // 2
    exact = half // 2
    n = np.abs(rel)
    large = exact + (np.log(np.maximum(n, 1) / exact) / np.log(MAX_DISTANCE / exact) * (half - exact)).astype(np.int64)
    large = np.minimum(large, half - 1)
    return ((rel > 0) * half + np.where(n < exact, n, large)).astype(np.int32)


def dilated_bias_table(rel_bias, s, tq):
    heads = rel_bias.shape[1]
    d = np.arange(-(s - 1), s)
    count = np.zeros(d.shape, np.float32)
    for window, dil in A_PATTERNS:
        count += ((d % dil == 0) & (np.abs(d) <= (window // (2 * dil)) * dil)).astype(np.float32)
    logc = np.where(count > 0, np.log(np.maximum(count, 1.0)), NEG).astype(np.float32)
    onehot = (_t5_bucket(d)[:, None] == np.arange(N_BUCKETS)[None, :]).astype(np.float32)
    line = jnp.transpose(jnp.dot(onehot, rel_bias.astype(F32), precision=lax.Precision.HIGHEST)) + logc[None]
    width = 2 * s
    line = jnp.pad(line, ((0, 0), (0, width - line.shape[1])))[:, None, :]
    return pl.pallas_call(
        functools.partial(_skew_kernel, tq=tq),
        out_shape=jax.ShapeDtypeStruct((heads, tq, 2 * s - tq), F32),
        grid=(heads,),
        in_specs=[pl.BlockSpec((None, 1, width), lambda h: (h, 0, 0))],
        out_specs=pl.BlockSpec((None, tq, 2 * s - tq), lambda h: (h, 0, 0)),
        compiler_params=_params("parallel"),
        name="bias_skew",
    )(line)


def _skew_kernel(line_ref, o_ref, *, tq):
    width = line_ref.shape[1]
    x = jnp.broadcast_to(line_ref[...], (tq, width))
    x = pltpu.roll(x, width - (tq - 1), 1, stride=1, stride_axis=0)
    o_ref[...] = x[:, :o_ref.shape[1]]


def _gla_kernel(q_ref, k_ref, v_ref, g_ref, z_ref, w2f_ref, w2b_ref, bf_ref, bb_ref, on_ref, o_ref,
                laf_ref, lab_ref, acc_ref, st_ref):
    s_len = q_ref.shape[0]
    c = B_CHUNK
    nchunk = s_len // c
    z = z_ref[...].astype(BF16)
    gate = lambda w2_ref, b_ref: jax.nn.log_sigmoid(
        jnp.dot(z, w2_ref[...], preferred_element_type=F32) + b_ref[...]) * (1.0 / B_GATE_TAU)
    laf_ref[...] = gate(w2f_ref, bf_ref)
    lab_ref[...] = gate(w2b_ref, bb_ref)

    ri = lax.broadcasted_iota(jnp.int32, (c, c), 0)
    ci = lax.broadcasted_iota(jnp.int32, (c, c), 1)
    hp = lax.Precision.HIGHEST

    def run(la_ref, backward, first):
        keep = (ri <= ci) if backward else (ri >= ci)
        tri = keep.astype(F32)
        st_ref[...] = jnp.zeros_like(st_ref)

        def chunk(n, carry):
            n = (nchunk - 1 - n) if backward else n
            rows = pl.ds(pl.multiple_of(n * c, c), c)
            la = la_ref[rows, :]
            gcum = jnp.dot(tri, la, preferred_element_type=F32, precision=hp)
            gend = jnp.sum(la, axis=0, keepdims=True)
            q_in = (q_ref[rows, :] * (B_DK ** -0.5) * jnp.exp(gcum)).astype(BF16)
            kc = k_ref[rows, :]
            k_in = (kc * jnp.exp(-gcum)).astype(BF16)
            k_out = (kc * jnp.exp(gend - gcum)).astype(BF16)
            vc = v_ref[rows, :].astype(BF16)
            att = lax.dot_general(q_in, k_in, (((1,), (1,)), ((), ())), preferred_element_type=F32)
            att = jnp.where(keep, att, 0.0).astype(BF16)
            state = st_ref[...]
            o = jnp.dot(att, vc, preferred_element_type=F32)
            o += lax.dot_general(q_in, state.astype(BF16), (((1,), (1,)), ((), ())), preferred_element_type=F32)
            upd = lax.dot_general(vc, k_out, (((0,), (0,)), ((), ())), preferred_element_type=F32)
            st_ref[...] = state * jnp.exp(gend) + upd
            if first:
                acc_ref[rows, :] = o
            else:
                acc_ref[rows, :] += o
            return carry

        lax.fori_loop(0, nchunk, chunk, 0)

    run(laf_ref, False, True)
    run(lab_ref, True, False)
    o = acc_ref[...]
    o = o * lax.rsqrt(jnp.mean(o * o, axis=-1, keepdims=True) + EPS) * on_ref[...]
    g = g_ref[...]
    o_ref[...] = o * (g * jax.nn.sigmoid(g))


def gla_mixer(proj, w2f, w2b, b_f, b_b, onorm, *, q_col, k_col, v_col, g_col, z_col):
    b, s, _ = proj.shape
    hm = lambda blk: (lambda bi, h: (bi, 0, blk + h))
    w2f_p = jnp.zeros((LANE, B_KEYW), F32).at[:B_GATE_RANK].set(w2f).astype(BF16)
    w2b_p = jnp.zeros((LANE, B_KEYW), F32).at[B_GATE_RANK:2 * B_GATE_RANK].set(w2b).astype(BF16)
    return pl.pallas_call(
        _gla_kernel,
        out_shape=jax.ShapeDtypeStruct((b, s, B_WIDTH), F32),
        grid=(b, B_HEADS),
        in_specs=[pl.BlockSpec((None, s, B_DK), hm(q_col // B_DK)),
                  pl.BlockSpec((None, s, B_DK), hm(k_col // B_DK)),
                  pl.BlockSpec((None, s, B_DV), hm(v_col // B_DV)),
                  pl.BlockSpec((None, s, B_DV), hm(g_col // B_DV)),
                  pl.BlockSpec((None, s, LANE), lambda bi, h: (bi, 0, z_col // LANE)),
                  pl.BlockSpec((LANE, B_DK), lambda bi, h: (0, h)),
                  pl.BlockSpec((LANE, B_DK), lambda bi, h: (0, h)),
                  pl.BlockSpec((1, B_DK), lambda bi, h: (0, h)),
                  pl.BlockSpec((1, B_DK), lambda bi, h: (0, h)),
                  pl.BlockSpec((1, B_DV), lambda bi, h: (0, 0))],
        out_specs=pl.BlockSpec((None, s, B_DV), lambda bi, h: (bi, 0, h)),
        scratch_shapes=[pltpu.VMEM((s, B_DK), F32), pltpu.VMEM((s, B_DK), F32),
                        pltpu.VMEM((s, B_DV), F32), pltpu.VMEM((B_DV, B_DK), F32)],
        compiler_params=_params("parallel", "arbitrary"),
        name="gla_mixer",
    )(proj, proj, proj, proj, proj, w2f_p, w2b_p, b_f.reshape(1, -1), b_b.reshape(1, -1), onorm.reshape(1, -1))


def _mla_up_kernel(cq_ref, ckv_ref, kr_ref, qn_ref, kvn_ref, wq_ref, wqr_ref, wkv_ref, cos_ref, sin_ref,
                   q_ref, k_ref, v_ref):
    def rms(x, g):
        return (x * lax.rsqrt(jnp.mean(x * x, axis=-1, keepdims=True) + EPS) * g).astype(BF16)

    cq = rms(cq_ref[...], qn_ref[...])
    ckv = rms(ckv_ref[...], kvn_ref[...])
    cos, sin = cos_ref[...], sin_ref[...]
    kr = kr_ref[...]
    k_rope = kr * cos + pltpu.roll(kr, LANE - C_ROPE, 1) * sin
    lane = lax.broadcasted_iota(jnp.int32, k_rope.shape, 1)
    k_rope = jnp.where(lane < C_ROPE, k_rope, 0.0)
    for h in range(C_HEADS):
        q = jnp.dot(cq, wq_ref[:, h * C_QK:(h + 1) * C_QK], preferred_element_type=F32)
        qp = jnp.dot(cq, wqr_ref[:, h * LANE:(h + 1) * LANE], preferred_element_type=F32)
        q_ref[:, h * C_QK:h * C_QK + C_NOPE] = q[:, :C_NOPE]
        q_ref[:, h * C_QK + C_NOPE:(h + 1) * C_QK] = q[:, C_NOPE:] * cos + qp * sin
        kv = jnp.dot(ckv, wkv_ref[:, h * 2 * LANE:(h + 1) * 2 * LANE], preferred_element_type=F32)
        k_ref[:, h * C_QK:h * C_QK + C_NOPE] = kv[:, :C_NOPE]
        k_ref[:, h * C_QK + C_NOPE:(h + 1) * C_QK] = k_rope
        v_ref[:, h * C_V:(h + 1) * C_V] = kv[:, C_NOPE:]


def _rot_half_cols(w):
    half = w.shape[-1] // 2
    return jnp.concatenate([-w[..., half:], w[..., :half]], axis=-1)


def mla_up(proj, q_norm, w_uq, kv_norm, w_ukv, cos, sin, *, col0, tm=512):
    t = proj.shape[0]
    wq = w_uq.reshape(C_Q_RANK, C_HEADS, C_NOPE + C_ROPE)
    wq_main = jnp.pad(wq, ((0, 0), (0, 0), (0, C_QK - C_NOPE - C_ROPE))).reshape(C_Q_RANK, C_HEADS * C_QK)
    wq_rot = jnp.pad(_rot_half_cols(wq[..., C_NOPE:]), ((0, 0), (0, 0), (0, LANE - C_ROPE)))
    wq_rot = wq_rot.reshape(C_Q_RANK, C_HEADS * LANE)
    row = lambda i: (i, 0)
    full = lambda arr: pl.BlockSpec(arr.shape, lambda i: (0, 0))
    g_q, g_kv = q_norm.reshape(1, -1), kv_norm.reshape(1, -1)
    wq_main, wq_rot, wkv = wq_main.astype(BF16), wq_rot.astype(BF16), w_ukv.astype(BF16)
    return pl.pallas_call(
        _mla_up_kernel,
        out_shape=(jax.ShapeDtypeStruct((t, C_HEADS * C_QK), F32),
                   jax.ShapeDtypeStruct((t, C_HEADS * C_QK), F32),
                   jax.ShapeDtypeStruct((t, C_WIDTH), F32)),
        grid=(t // tm,),
        in_specs=[pl.BlockSpec((tm, C_Q_RANK), lambda i: (i, col0 // C_Q_RANK)),
                  pl.BlockSpec((tm, C_KV_RANK), lambda i: (i, (col0 + C_Q_RANK) // C_KV_RANK)),
                  pl.BlockSpec((tm, LANE), lambda i: (i, (col0 + C_Q_RANK + C_KV_RANK) // LANE)),
                  full(g_q), full(g_kv), full(wq_main), full(wq_rot), full(wkv),
                  pl.BlockSpec((tm, LANE), row), pl.BlockSpec((tm, LANE), row)],
        out_specs=(pl.BlockSpec((tm, C_HEADS * C_QK), row),
                   pl.BlockSpec((tm, C_HEADS * C_QK), row),
                   pl.BlockSpec((tm, C_WIDTH), row)),
        compiler_params=_params("parallel"),
        name="mla_up",
    )(proj, proj, proj, g_q, g_kv, wq_main, wq_rot, wkv, cos, sin)


RG = 4
RGW = RG * D_HEAD
RCH = 64
NT_DIMS = (((1,), (1,)), ((), ()))
TN_DIMS = (((0,), (0,)), ((), ()))
DC_R, DC_K, DC_V = 0, D_WIDTH, 2 * D_WIDTH
DC_ZG = 3 * D_WIDTH
DC_ZW = DC_ZG + D_G_RANK
DC_ZA = DC_ZW + 2 * D_W_RANK
DC_PAD = 7 * 512


def _split3(x):
    hi = x.astype(BF16)
    r1 = x - hi.astype(F32)
    mid = r1.astype(BF16)
    lo = (r1 - mid.astype(F32)).astype(BF16)
    return hi, mid, lo


def _head_sums(x, bo):
    return jnp.concatenate(
        [jnp.dot(x[:, RGW * g:RGW * (g + 1)], bo, preferred_element_type=F32, precision=lax.Precision.HIGHEST)
         for g in range(x.shape[1] // RGW)], axis=1)


def _block_ones():
    i = np.arange(RGW)
    return jnp.asarray((i[:, None] // D_HEAD) == (i[None, :] // D_HEAD), F32)


def _rwkv_prep_kernel(x_ref, xp_ref, xn_ref, mu_ref, w2f_ref, w2b_ref, a2_ref, g2_ref, w0f_ref, w0b_ref,
                      a0_ref, kk_ref, ka_ref, rk_ref, bo_ref,
                      r_ref, k_ref, v_ref, a_ref, b_ref, lwf_ref, lwb_ref, g_ref, bonus_ref, *, tiles_per_seq):
    i = pl.program_id(0) % tiles_per_seq
    x = x_ref[...]
    tm = x.shape[0]
    row = lax.broadcasted_iota(jnp.int32, x.shape, 0)
    prev_row = jnp.where(i == 0, 0.0, xp_ref[7:8, :])
    next_row = jnp.where(i == tiles_per_seq - 1, 0.0, xn_ref[0:1, :])
    prev = jnp.where(row == 0, prev_row, pltpu.roll(x, 1, 0))
    nxt = jnp.where(row == tm - 1, next_row, pltpu.roll(x, tm - 1, 0))
    x = x + mu_ref[...] * (0.5 * (prev + nxt) - x)
    r, k, v = x[:, DC_R:DC_R + D_WIDTH], x[:, DC_K:DC_K + D_WIDTH], x[:, DC_V:DC_V + D_WIDTH]
    zg = x[:, DC_ZG:DC_ZG + LANE]
    zw = x[:, DC_ZW:DC_ZW + LANE]
    za = x[:, DC_ZA:DC_ZA + LANE]
    tz = jnp.tanh(zw).astype(BF16)
    log_decay = lambda w0_ref, w2_ref: -np.exp(-0.5).astype(np.float32) * jax.nn.sigmoid(
        w0_ref[...] + jnp.dot(tz, w2_ref[...], preferred_element_type=F32))
    lwf_ref[...] = log_decay(w0f_ref, w2f_ref)
    lwb_ref[...] = log_decay(w0b_ref, w2b_ref)
    ag = jax.nn.sigmoid(a0_ref[...] + jnp.dot(za.astype(BF16), a2_ref[...], preferred_element_type=F32))
    g_ref[...] = jnp.dot(jax.nn.sigmoid(zg).astype(BF16), g2_ref[...], preferred_element_type=F32)
    bo = bo_ref[...]
    kk = k * kk_ref[...]
    kk = kk / jnp.maximum(jnp.sqrt(_head_sums(kk * kk, bo)), 1e-12)
    k = k * (1.0 + (ag - 1.0) * ka_ref[...])
    r_ref[...] = r
    k_ref[...] = k
    v_ref[...] = v
    a_ref[...] = -kk
    b_ref[...] = kk * ag
    bonus_ref[...] = _head_sums(r * k * rk_ref[...], bo) * v


def rwkv_prep(proj, mu, w0_f, w2_f, w0_b, w2_b, a0, a2, g2, k_k, k_a, r_k, *, seq, tm=256):
    t = proj.shape[0]
    tiles_per_seq = seq // tm
    hb = tm // 8
    nblk8 = t // 8
    pad_rows = lambda w, lo: jnp.zeros((LANE, D_WIDTH), F32).at[lo:lo + w.shape[0]].set(w).astype(BF16)
    vec = lambda u: u.reshape(1, -1)
    consts = [vec(mu), pad_rows(w2_f, 0), pad_rows(w2_b, D_W_RANK), pad_rows(a2, 0), g2.astype(BF16),
              vec(w0_f), vec(w0_b), vec(a0), vec(k_k), vec(k_a), vec(r_k), _block_ones()]
    full = lambda arr: pl.BlockSpec(arr.shape, lambda i: (0, 0))
    out_spec = pl.BlockSpec((tm, D_WIDTH), lambda i: (i, 0))
    return pl.pallas_call(
        functools.partial(_rwkv_prep_kernel, tiles_per_seq=tiles_per_seq),
        out_shape=tuple(jax.ShapeDtypeStruct((t, D_WIDTH), F32) for _ in range(9)),
        grid=(t // tm,),
        in_specs=[pl.BlockSpec((tm, DC_PAD), lambda i: (i, 0)),
                  pl.BlockSpec((8, DC_PAD), lambda i: (jnp.maximum(i * hb - 1, 0), 0)),
                  pl.BlockSpec((8, DC_PAD), lambda i: (jnp.minimum((i + 1) * hb, nblk8 - 1), 0))]
                 + [full(c) for c in consts],
        out_specs=tuple(out_spec for _ in range(9)),
        compiler_params=_params("parallel"),
        name="rwkv_prep",
    )(proj, proj, proj, *consts)


def _rwkv_chunk_kernel(*refs, ngroups):
    ins, (yf_ref, yb_ref, mt_ref) = refs[:12], refs[12:]

    @pl.when(pl.program_id(1) == 0)
    def _():
        mt_ref[...] = jnp.zeros_like(mt_ref)

    row = lax.broadcasted_iota(jnp.int32, (RCH, RGW), 0)
    col = lax.broadcasted_iota(jnp.int32, (RCH, RGW), 1) & (RCH - 1)
    bdmask = (lax.broadcasted_iota(jnp.int32, (RGW, RGW), 0) // D_HEAD
              == lax.broadcasted_iota(jnp.int32, (RGW, RGW), 1) // D_HEAD)
    tr = lax.broadcasted_iota(jnp.int32, (RCH, RCH), 0)
    tc = lax.broadcasted_iota(jnp.int32, (RCH, RCH), 1)
    zero = jnp.zeros((), F32)

    def bd(z):
        zb = z.astype(BF16)
        return jnp.where(bdmask, jnp.concatenate([zb] * RG, axis=0), jnp.zeros((), BF16))

    def mm(x, y, dims=None):
        x = x.astype(BF16)
        if dims is None:
            return jnp.dot(x, y, preferred_element_type=F32)
        return lax.dot_general(x, y, dims, preferred_element_type=F32)

    chains = [(d, g) for d in range(2) for g in range(ngroups)]
    st = []
    for d, g in chains:
        backward = d == 1
        r_ref, k_ref, v_ref, a_ref, b_ref, lw_ref = ins[6 * d:6 * d + 6]
        tri = ((tc >= tr) if backward else (tc <= tr)).astype(BF16)
        sl = slice(RGW * g, RGW * (g + 1))
        r, k, v, a, b, lw = (ref[:, sl] for ref in (r_ref, k_ref, v_ref, a_ref, b_ref, lw_ref))
        lam = jnp.dot(jnp.concatenate([tri] * 3, axis=1), jnp.concatenate(_split3(lw), axis=0),
                      preferred_element_type=F32)
        lamc = lam[0:1] if backward else lam[RCH - 1:RCH]
        e_inv = jnp.exp(-lam)
        e_out = jnp.exp(lamc - lam)
        ar = jnp.concatenate([a * jnp.exp(lam - lw), r * jnp.exp(lam)], axis=0).astype(BF16)
        bk = jnp.concatenate([b * e_out, k * e_out], axis=0).astype(BF16)
        st.append(dict(ar=ar, bk=bk, v=v, lamc=lamc, sl=sl,
                       gb=mm(ar, bd(b * e_inv), NT_DIMS), gk=mm(ar, bd(k * e_inv), NT_DIMS)))
    for (d, g), c in zip(chains, st):
        strict = (col > row) if d == 1 else (col < row)
        incl = (col >= row) if d == 1 else (col <= row)
        c["lp"] = jnp.where(strict, c["gb"][:RCH], zero)
        lak = jnp.where(strict, c["gk"][:RCH], zero)
        c["grb"] = jnp.where(incl, c["gb"][RCH:], zero).astype(BF16)
        c["grk"] = jnp.where(incl, c["gk"][RCH:], zero).astype(BF16)
        c["mt"] = mt_ref[d, g]
        amrm = mm(c["ar"], c["mt"].astype(BF16), NT_DIMS)
        c["bdv"] = bd(c["v"])
        c["u"] = amrm[:RCH] + mm(lak, c["bdv"])
        c["rm"] = amrm[RCH:]
    for rnd in range(6):
        for c in st:
            lpb = c["lp"].astype(BF16)
            c["u"] = c["u"] + mm(lpb, bd(c["u"]))
            if rnd < 5:
                c["lp"] = mm(lpb, bd(c["lp"]))
    for (d, g), c in zip(chains, st):
        y_ref = yb_ref if d == 1 else yf_ref
        y_ref[:, c["sl"]] = c["rm"] + mm(c["grb"], bd(c["u"])) + mm(c["grk"], c["bdv"])
        uv = jnp.concatenate([c["u"], c["v"]], axis=0).astype(BF16)
        upd = lax.dot_general(uv, c["bk"], TN_DIMS, preferred_element_type=F32)
        mt_ref[d, g] = c["mt"] * jnp.exp(c["lamc"]) + jnp.where(bdmask, upd, zero)


def rwkv_chunked(r, k, v, a, b, lwf, lwb):
    bsz, s, wd = r.shape
    nc = s // RCH
    fspec = pl.BlockSpec((None, RCH, wd), lambda bi, n: (bi, n, 0))
    bspec = pl.BlockSpec((None, RCH, wd), lambda bi, n: (bi, nc - 1 - n, 0))
    return pl.pallas_call(
        functools.partial(_rwkv_chunk_kernel, ngroups=wd // RGW),
        out_shape=(jax.ShapeDtypeStruct((bsz, s, wd), F32), jax.ShapeDtypeStruct((bsz, s, wd), F32)),
        grid=(bsz, nc),
        in_specs=[fspec] * 6 + [bspec] * 6,
        out_specs=(fspec, bspec),
        scratch_shapes=[pltpu.VMEM((2, wd // RGW, RGW, RGW), F32)],
        compiler_params=_params("parallel", "arbitrary"),
        name="rwkv_chunked",
    )(r, k, v, a, b, lwf, r, k, v, a, b, lwb)


def _rwkv_post_kernel(yf_ref, yb_ref, bonus_ref, g_ref, lng_ref, lnb_ref, bo_ref, o_ref):
    bo = bo_ref[...]
    y = yf_ref[...] + yb_ref[...]
    yc = y - _head_sums(y, bo) * (1.0 / D_HEAD)
    var = _head_sums(yc * yc, bo) * (1.0 / D_HEAD)
    y = yc * lax.rsqrt(var + D_LN_EPS) * lng_ref[...] + lnb_ref[...]
    o_ref[...] = (y + bonus_ref[...]) * g_ref[...]


def rwkv_post(yf, yb, bonus, g, ln_g, ln_b, *, tm=512):
    t, wd = yf.shape
    row = pl.BlockSpec((tm, wd), lambda i: (i, 0))
    vec = pl.BlockSpec((1, wd), lambda i: (0, 0))
    bo = _block_ones()
    return pl.pallas_call(
        _rwkv_post_kernel,
        out_shape=jax.ShapeDtypeStruct((t, wd), F32),
        grid=(t // tm,),
        in_specs=[row, row, row, row, vec, vec, pl.BlockSpec(bo.shape, lambda i: (0, 0))],
        out_specs=row,
        compiler_params=_params("parallel"),
        name="rwkv_post",
    )(yf, yb, bonus, g, ln_g.reshape(1, wd), ln_b.reshape(1, wd), bo)


def rwkv7_mixer(proj, bsz, s, mu, w0_f, w2_f, w0_b, w2_b, a0, a2, g2, k_k, k_a, r_k, ln_g, ln_b):
    r, k, v, a, b, lwf, lwb, g, bonus = rwkv_prep(proj, mu, w0_f, w2_f, w0_b, w2_b, a0, a2, g2, k_k, k_a,
                                                  r_k.reshape(-1), seq=s)
    r3 = lambda u: u.reshape(bsz, s, D_WIDTH)
    yf, yb = rwkv_chunked(r3(r), r3(k), r3(v), r3(a), r3(b), r3(lwf), r3(lwb))
    return rwkv_post(yf.reshape(-1, D_WIDTH), yb.reshape(-1, D_WIDTH), bonus, g, ln_g, ln_b)


MOE_TILE = 1024
MOE_SUB = 128


def _route(logit):
    lane = lax.broadcasted_iota(jnp.int32, logit.shape, 1)
    first_at = lambda mask: jnp.min(jnp.where(mask, lane, jnp.int32(LANE)), axis=-1, keepdims=True)
    is_grp = lane < N_GROUPS
    gl = jnp.where(is_grp, logit, NEG)
    gmax = jnp.max(gl, axis=-1, keepdims=True)
    p_grp = 1.0 / jnp.sum(jnp.where(is_grp, jnp.exp(gl - gmax), 0.0), axis=-1, keepdims=True)
    i_grp = first_at(is_grp & (gl == gmax))
    lo = N_GROUPS + i_grp * EXPERTS_PER_GROUP
    in_grp = (lane >= lo) & (lane < lo + EXPERTS_PER_GROUP)
    el = jnp.where(in_grp, logit, NEG)
    l1 = jnp.max(el, axis=-1, keepdims=True)
    i1 = first_at(in_grp & (el == l1))
    rest = in_grp & (lane != i1)
    el2 = jnp.where(rest, logit, NEG)
    l2 = jnp.max(el2, axis=-1, keepdims=True)
    i2 = first_at(rest & (el2 == l2))
    e2 = jnp.exp(l2 - l1)
    w1 = p_grp / (1.0 + e2)
    w2 = p_grp * e2 / (1.0 + e2)
    return i_grp, jnp.where(lane == i1, w1, jnp.where(lane == i2, w2, 0.0))


def _moe_sort_kernel(x_ref, g_ref, wr_ref, br_ref, hn_ref, comb_ref, pos_ref, off_ref):
    x = x_ref[...]
    tm = x.shape[0]
    hn = (x * lax.rsqrt(jnp.mean(x * x, axis=-1, keepdims=True) + EPS) * g_ref[...]).astype(BF16)
    wr, br = wr_ref[...], br_ref[...]
    i_grp, _ = _route(jnp.dot(hn, wr, preferred_element_type=F32) + br)
    lane = lax.broadcasted_iota(jnp.int32, (tm, LANE), 1)
    onehot = (lane == i_grp).astype(F32)
    ri = lax.broadcasted_iota(jnp.int32, (tm, tm), 0)
    ci = lax.broadcasted_iota(jnp.int32, (tm, tm), 1)
    earlier = jnp.dot((ci < ri).astype(BF16), onehot.astype(BF16), preferred_element_type=F32)
    cnt = jnp.broadcast_to(jnp.sum(onehot, axis=0, keepdims=True), (8, LANE))
    lane8 = lax.broadcasted_iota(jnp.int32, (8, LANE), 1)
    start = jnp.zeros((8, LANE), F32)
    for sh in range(1, N_GROUPS):
        start += jnp.where(lane8 >= sh, pltpu.roll(cnt, sh, 1), 0.0)
    off_ref[...] = start[0:1].astype(jnp.int32)
    pos = jnp.sum(onehot * (start[0:1] + earlier), axis=-1, keepdims=True).astype(jnp.int32)
    pos_ref[...] = pos
    perm_t = (ci == pos).astype(BF16)
    hn_s = lax.dot_general(perm_t, hn, TN_DIMS, preferred_element_type=F32).astype(BF16)
    hn_ref[...] = hn_s
    _, comb = _route(jnp.dot(hn_s, wr, preferred_element_type=F32) + br)
    comb_ref[...] = comb


def moe_sort(x, g, w_grp, b_grp, w_exp, b_exp):
    t, d = x.shape
    tm = MOE_TILE
    nr = N_GROUPS + N_EXPERTS
    wr = jnp.pad(jnp.concatenate([w_grp, w_exp], axis=1), ((0, 0), (0, LANE - nr))).astype(BF16)
    br = jnp.pad(jnp.concatenate([b_grp, b_exp]), (0, LANE - nr)).reshape(1, LANE)
    row = lambda i: (i, 0)
    hn, comb, pos, off = pl.pallas_call(
        _moe_sort_kernel,
        out_shape=(jax.ShapeDtypeStruct((t, d), BF16), jax.ShapeDtypeStruct((t, LANE), F32),
                   jax.ShapeDtypeStruct((t, 1), jnp.int32), jax.ShapeDtypeStruct((t // tm, 1, LANE), jnp.int32)),
        grid=(t // tm,),
        in_specs=[pl.BlockSpec((tm, d), row), pl.BlockSpec((1, d), lambda i: (0, 0)),
                  pl.BlockSpec((d, LANE), lambda i: (0, 0)), pl.BlockSpec((1, LANE), lambda i: (0, 0))],
        out_specs=(pl.BlockSpec((tm, d), row), pl.BlockSpec((tm, LANE), row), pl.BlockSpec((tm, 1), row),
                   pl.BlockSpec((None, 1, LANE), lambda i: (i, 0, 0))),
        compiler_params=_params("parallel"),
        name="moe_sort",
    )(x, g.reshape(1, d), wr, br)
    bounds = jnp.concatenate([off[:, 0, :N_GROUPS], jnp.full((t // tm, 1), tm, jnp.int32)], axis=1)
    return hn, comb[:, N_GROUPS:nr], pos, bounds.reshape(-1)


def _moe_group_kernel(bounds_ref, hn_ref, c_ref, wg_ref, wu_ref, wd_ref, y_ref):
    i, g, j = pl.program_id(0), pl.program_id(1), pl.program_id(2)

    @pl.when((g == 0) & (j == 0))
    def _():
        y_ref[...] = jnp.zeros_like(y_ref)

    base = i * (N_GROUPS + 1) + g
    start, end = bounds_ref[base], bounds_ref[base + 1]

    def sub(k, carry):
        rows = pl.ds(pl.multiple_of(k * MOE_SUB, MOE_SUB), MOE_SUB)
        x = hn_ref[rows, :]
        hg = jnp.dot(x, wg_ref[...], preferred_element_type=F32)
        hu = jnp.dot(x, wu_ref[...], preferred_element_type=F32)
        hid = (hg * jax.nn.sigmoid(hg)) * hu * c_ref[rows, :]
        y_ref[rows, :] += jnp.dot(hid.astype(BF16), wd_ref[...], preferred_element_type=F32)
        return carry

    lax.fori_loop(start // MOE_SUB, (end + MOE_SUB - 1) // MOE_SUB, sub, 0)


def moe_group_experts(hn, comb, bounds, w_gate, w_up, w_down):
    t, d = hn.shape
    tm = MOE_TILE
    ne, _, ff = w_gate.shape
    comb_e = jnp.transpose(comb)[:, :, None]
    ex = lambda i, g, j, b: (g * EXPERTS_PER_GROUP + j, 0, 0)
    return pl.pallas_call(
        _moe_group_kernel,
        out_shape=jax.ShapeDtypeStruct((t, d), F32),
        grid_spec=pltpu.PrefetchScalarGridSpec(
            num_scalar_prefetch=1,
            grid=(t // tm, N_GROUPS, EXPERTS_PER_GROUP),
            in_specs=[pl.BlockSpec((tm, d), lambda i, g, j, b: (i, 0)),
                      pl.BlockSpec((None, tm, 1), lambda i, g, j, b: (g * EXPERTS_PER_GROUP + j, i, 0)),
                      pl.BlockSpec((None, d, ff), ex), pl.BlockSpec((None, d, ff), ex),
                      pl.BlockSpec((None, ff, d), ex)],
            out_specs=pl.BlockSpec((tm, d), lambda i, g, j, b: (i, 0))),
        compiler_params=_params("parallel", "arbitrary", "arbitrary"),
        name="moe_group_experts",
    )(bounds, hn, comb_e, w_gate, w_up, w_down)


def _moe_unsort_kernel(y_ref, pos_ref, x_ref, o_ref):
    tm = y_ref.shape[0]
    perm_t = (lax.broadcasted_iota(jnp.int32, (tm, tm), 1) == pos_ref[...]).astype(BF16)
    y = y_ref[...]
    hi = y.astype(BF16)
    lo = (y - hi.astype(F32)).astype(BF16)
    o_ref[...] = (x_ref[...] + jnp.dot(perm_t, hi, preferred_element_type=F32)
                  + jnp.dot(perm_t, lo, preferred_element_type=F32))


def moe_unsort(y, pos, x):
    t, d = x.shape
    tm, tn = MOE_TILE, d // 2
    blk = pl.BlockSpec((tm, tn), lambda i, j: (i, j))
    return pl.pallas_call(
        _moe_unsort_kernel,
        out_shape=jax.ShapeDtypeStruct((t, d), F32),
        grid=(t // tm, d // tn),
        in_specs=[blk, pl.BlockSpec((tm, 1), lambda i, j: (i, 0)), blk],
        out_specs=blk,
        compiler_params=_params("parallel", "arbitrary"),
        name="moe_unsort",
    )(y, pos, x)


def hier_moe(x, g, w_grp, b_grp, w_exp, b_exp, w_gate, w_up, w_down):
    hn, comb, pos, bounds = moe_sort(x, g, w_grp, b_grp, w_exp, b_exp)
    y = moe_group_experts(hn, comb, bounds, w_gate.astype(BF16), w_up.astype(BF16), w_down.astype(BF16))
    return moe_unsort(y, pos, x)


def even_layer(x, bsz, s, norm_g, rel_bias, w_in, w_out, w2_f, b_f, w2_b, b_b, onorm):
    n_pad = _round_up(EVEN_IN, LANE)
    w_in_p = jnp.pad(w_in, ((0, 0), (0, n_pad - EVEN_IN))).astype(BF16)
    proj = norm_linear(x, norm_g, w_in_p, tn_target=896).reshape(bsz, s, n_pad)
    tq = 256
    ya = attention(proj, proj, proj, heads=A_HEADS, dq=HEAD_DIM, dv=HEAD_DIM,
                   q_off=0, k_off=A_HEADS, v_off=2 * A_HEADS, scale=HEAD_DIM ** -0.5,
                   bias=dilated_bias_table(rel_bias, s, tq), tq=tq)
    q_col = 3 * A_WIDTH
    yb = gla_mixer(proj, w2_f, w2_b, b_f, b_b, onorm, q_col=q_col, k_col=q_col + B_KEYW,
                   v_col=q_col + 2 * B_KEYW, g_col=q_col + 2 * B_KEYW + B_WIDTH,
                   z_col=q_col + 2 * B_KEYW + 2 * B_WIDTH)
    t = bsz * s
    return out_proj(ya.reshape(t, A_WIDTH), yb.reshape(t, B_WIDTH), w_out.astype(BF16), x)


def _odd_columns(w_in, mu):
    c0 = C_IN
    cut = lambda u, lo, n: u[..., lo:lo + n]
    zpad = lambda u, n: jnp.pad(u, [(0, 0)] * (u.ndim - 1) + [(0, n)])
    off = np.cumsum((0,) + D_SPLITS)
    def rwkv_cols(u):
        parts = [cut(u, off[0], 3 * D_WIDTH), cut(u, off[6], D_G_RANK), cut(u, off[3], 2 * D_W_RANK),
                 cut(u, off[5], D_A_RANK)]
        u = jnp.concatenate(parts, axis=-1)
        return zpad(u, DC_PAD - u.shape[-1])
    w_kr = w_in[:, C_Q_RANK + C_KV_RANK:C_IN]
    w_all = jnp.concatenate([rwkv_cols(w_in[:, c0:]), w_in[:, :C_IN], _rot_half_cols(w_kr)], axis=1)
    n_pad = _round_up(w_all.shape[1], 9 * LANE)
    return zpad(w_all, n_pad - w_all.shape[1]).astype(BF16), rwkv_cols(mu)


def odd_layer(x, bsz, s, norm_g, w_in, w_out, q_norm, w_uq, kv_norm, w_ukv, mu, w0_f, w2_f, w0_b, w2_b,
              a0, a2, g2, k_k, k_a, r_k, ln_g, ln_b):
    t = bsz * s
    w_all, mu_cols = _odd_columns(w_in, mu)
    proj = norm_linear(x, norm_g, w_all, tn_target=1152)
    inv = 1.0 / (ROPE_THETA ** (jnp.arange(0, C_ROPE, 2, dtype=F32) / C_ROPE))
    ang = jnp.arange(s, dtype=F32)[:, None] * inv[None, :]
    cos = jnp.pad(jnp.concatenate([jnp.cos(ang)] * 2, axis=1), ((0, 0), (0, LANE - C_ROPE)), constant_values=1.0)
    sin = jnp.pad(jnp.concatenate([jnp.sin(ang)] * 2, axis=1), ((0, 0), (0, LANE - C_ROPE)))
    q, k, v = mla_up(proj, q_norm, w_uq, kv_norm, w_ukv, jnp.tile(cos, (bsz, 1)), jnp.tile(sin, (bsz, 1)),
                     col0=DC_PAD)
    r3 = lambda u: u.reshape(bsz, s, -1)
    yc = attention(r3(q), r3(k), r3(v), heads=C_HEADS, dq=C_QK, dv=C_V, q_off=0, k_off=0, v_off=0,
                   scale=(C_NOPE + C_ROPE) ** -0.5)
    yd = rwkv7_mixer(proj, bsz, s, mu_cols, w0_f, w2_f, w0_b, w2_b, a0, a2, g2, k_k, k_a, r_k, ln_g, ln_b)
    return out_proj(yc.reshape(t, C_WIDTH), yd, w_out.astype(BF16), x)


def kernel(x_prompt, x_sample, rel_bias, norm_mix, norm_ffn, norm_final, ev_w_in, ev_w_out, ev_gla_w2_f, ev_gla_b_f, ev_gla_w2_b, ev_gla_b_b, ev_gla_onorm, od_w_in, od_w_out, od_q_norm, od_w_uq, od_kv_norm, od_w_ukv, od_mu, od_w0_f, od_w2_f, od_w0_b, od_w2_b, od_a0, od_a2, od_g2, od_k_k, od_k_a, od_r_k, od_ln_g, od_ln_b, moe_w_grp, moe_b_grp, moe_w_exp, moe_b_exp, moe_w_gate, moe_w_up, moe_w_down):
    nb_p = x_prompt.shape[0]
    x = jnp.concatenate([x_prompt, x_sample], axis=0)
    bsz, s, d = x.shape
    x = x.reshape(bsz * s, d)
    for i in range(DEPTH):
        j = i // 2
        if i % 2 == 0:
            x = even_layer(x, bsz, s, norm_mix[i], rel_bias, ev_w_in[j], ev_w_out[j], ev_gla_w2_f[j],
                           ev_gla_b_f[j], ev_gla_w2_b[j], ev_gla_b_b[j], ev_gla_onorm[j])
        else:
            x = odd_layer(x, bsz, s, norm_mix[i], od_w_in[j], od_w_out[j], od_q_norm[j], od_w_uq[j],
                          od_kv_norm[j], od_w_ukv[j], od_mu[j], od_w0_f[j], od_w2_f[j], od_w0_b[j],
                          od_w2_b[j], od_a0[j], od_a2[j], od_g2[j], od_k_k[j], od_k_a[j], od_r_k[j],
                          od_ln_g[j], od_ln_b[j])
        x = hier_moe(x, norm_ffn[i], moe_w_grp[i], moe_b_grp[i], moe_w_exp[i], moe_b_exp[i],
                     moe_w_gate[i], moe_w_up[i], moe_w_down[i])
    y = final_norm(x, norm_final).reshape(bsz, s, d)
    return (y[:nb_p], y[nb_p:])
```

```python
import functools

import jax, jax.numpy as jnp
from jax import lax
import numpy as np
from jax.experimental import pallas as pl
from jax.experimental.pallas import tpu as pltpu

F32, BF16 = jnp.float32, jnp.bfloat16

D_MODEL = 2048
DEPTH = 2
MIX_HALF = D_MODEL // 2
HEAD_DIM = 128
EPS = 1e-6
NEG = -1e30

A_HEADS = MIX_HALF // HEAD_DIM
A_WIDTH = A_HEADS * HEAD_DIM
A_PATTERNS = ((128, 1), (512, 4), (2048, 16))
N_BUCKETS = 32
MAX_DISTANCE = 1024

B_HEADS = 4
B_DV = MIX_HALF // B_HEADS
B_DK = B_DV // 2
B_WIDTH = B_HEADS * B_DV
B_KEYW = B_HEADS * B_DK
B_GATE_RANK = 16
B_GATE_TAU = 16.0
B_CHUNK = 64

C_HEADS = MIX_HALF // 128
C_Q_RANK = 512
C_KV_RANK = 256
C_NOPE = 128
C_ROPE = 64
C_V = 128
C_WIDTH = C_HEADS * C_V
C_QK = 256
ROPE_THETA = 10000.0

D_HEAD = 64
D_HEADS = MIX_HALF // D_HEAD
D_WIDTH = D_HEADS * D_HEAD
D_W_RANK = 64
D_A_RANK = 64
D_G_RANK = 128
D_LN_EPS = 64e-5
D_SPLITS = (D_WIDTH, D_WIDTH, D_WIDTH, D_W_RANK, D_W_RANK, D_A_RANK, D_G_RANK)
D_SHIFT = 3 * D_WIDTH + 2 * D_W_RANK + D_A_RANK + D_G_RANK

N_GROUPS = 4
EXPERTS_PER_GROUP = 4
N_EXPERTS = N_GROUPS * EXPERTS_PER_GROUP
EXPERT_FF = 512

EVEN_IN = 3 * A_WIDTH + 2 * B_KEYW + 2 * B_WIDTH + 2 * B_GATE_RANK
C_IN = C_Q_RANK + C_KV_RANK + C_ROPE
ODD_IN = C_IN + D_SHIFT

LANE = 128
VMEM_LIMIT = 52 * 1024 * 1024


def _params(*sem):
    return pltpu.CompilerParams(dimension_semantics=sem, vmem_limit_bytes=VMEM_LIMIT)


def _round_up(n, m):
    return -(-n // m) * m


NT_DIMS = (((1,), (1,)), ((), ()))
TN_DIMS = (((0,), (0,)), ((), ()))


def _split3(x):
    hi = x.astype(BF16)
    r1 = x - hi.astype(F32)
    mid = r1.astype(BF16)
    lo = (r1 - mid.astype(F32)).astype(BF16)
    return hi, mid, lo


def _pick_tile(n, target):
    best = LANE
    for t in range(LANE, target + 1, LANE):
        if n % t == 0:
            best = t
    return best


def _norm_linear_kernel(x_ref, g_ref, w_ref, o_ref, xn_ref):
    @pl.when(pl.program_id(1) == 0)
    def _():
        x = x_ref[...]
        y = x * lax.rsqrt(jnp.mean(x * x, axis=-1, keepdims=True) + EPS) * g_ref[...]
        xn_ref[...] = y.astype(BF16)

    o_ref[...] = jnp.dot(xn_ref[...], w_ref[...], preferred_element_type=F32)


def norm_linear(x, g, w, *, tm=1024, tn_target=1024):
    t, k = x.shape
    n = w.shape[1]
    tn = _pick_tile(n, tn_target)
    return pl.pallas_call(
        _norm_linear_kernel,
        out_shape=jax.ShapeDtypeStruct((t, n), F32),
        grid=(t // tm, n // tn),
        in_specs=[pl.BlockSpec((tm, k), lambda i, j: (i, 0)),
                  pl.BlockSpec((1, k), lambda i, j: (0, 0)),
                  pl.BlockSpec((k, tn), lambda i, j: (0, j))],
        out_specs=pl.BlockSpec((tm, tn), lambda i, j: (i, j)),
        scratch_shapes=[pltpu.VMEM((tm, k), BF16)],
        compiler_params=_params("parallel", "arbitrary"),
        name="norm_linear",
    )(x, g.reshape(1, k), w)


def _out_proj_kernel(a_ref, b_ref, wa_ref, wb_ref, x_ref, o_ref):
    acc = jnp.dot(a_ref[...].astype(BF16), wa_ref[...], preferred_element_type=F32)
    acc += jnp.dot(b_ref[...].astype(BF16), wb_ref[...], preferred_element_type=F32)
    o_ref[...] = x_ref[...] + acc


def out_proj(a, b, w, x, *, tm=1024, tn=512):
    t, ka = a.shape
    kb = b.shape[1]
    n = w.shape[1]
    return pl.pallas_call(
        _out_proj_kernel,
        out_shape=jax.ShapeDtypeStruct((t, n), F32),
        grid=(t // tm, n // tn),
        in_specs=[pl.BlockSpec((tm, ka), lambda i, j: (i, 0)),
                  pl.BlockSpec((tm, kb), lambda i, j: (i, 0)),
                  pl.BlockSpec((ka, tn), lambda i, j: (0, j)),
                  pl.BlockSpec((kb, tn), lambda i, j: (0, j)),
                  pl.BlockSpec((tm, tn), lambda i, j: (i, j))],
        out_specs=pl.BlockSpec((tm, tn), lambda i, j: (i, j)),
        compiler_params=_params("parallel", "arbitrary"),
        name="out_proj",
    )(a, b, w[:ka], w[ka:], x)


def _final_norm_kernel(x_ref, g_ref, o_ref):
    x = x_ref[...]
    o_ref[...] = x * lax.rsqrt(jnp.mean(x * x, axis=-1, keepdims=True) + EPS) * g_ref[...]


def final_norm(x, g, *, tm=1024):
    t, d = x.shape
    return pl.pallas_call(
        _final_norm_kernel,
        out_shape=jax.ShapeDtypeStruct((t, d), F32),
        grid=(t // tm,),
        in_specs=[pl.BlockSpec((tm, d), lambda i: (i, 0)), pl.BlockSpec((1, d), lambda i: (0, 0))],
        out_specs=pl.BlockSpec((tm, d), lambda i: (i, 0)),
        compiler_params=_params("parallel"),
        name="final_norm",
    )(x, g.reshape(1, d))


def _attn_kernel(*refs, scale, nq, tq, has_bias):
    if has_bias:
        q_ref, k_ref, v_ref, bias_ref, o_ref = refs
    else:
        q_ref, k_ref, v_ref, o_ref = refs
    s_len = k_ref.shape[0]
    q = (q_ref[...] * scale).astype(BF16)
    s = lax.dot_general(q, k_ref[...].astype(BF16), (((1,), (1,)), ((), ())), preferred_element_type=F32)
    if has_bias:
        start = pl.multiple_of((nq - 1 - pl.program_id(2)) * tq, LANE)
        s = s + bias_ref[:, pl.ds(start, s_len)]
    m = jnp.max(s, axis=-1, keepdims=True)
    p = jnp.exp(s - m)
    den = jnp.sum(p, axis=-1, keepdims=True)
    o = jnp.dot(p.astype(BF16), v_ref[...].astype(BF16), preferred_element_type=F32)
    o_ref[...] = o / den


def attention(q, k, v, *, heads, dq, dv, q_off, k_off, v_off, scale, bias=None, tq=256):
    b, s, _ = q.shape
    nq = s // tq
    in_specs = [pl.BlockSpec((None, tq, dq), lambda bi, h, qi: (bi, qi, q_off + h)),
                pl.BlockSpec((None, s, dq), lambda bi, h, qi: (bi, 0, k_off + h)),
                pl.BlockSpec((None, s, dv), lambda bi, h, qi: (bi, 0, v_off + h))]
    args = [q, k, v]
    if bias is not None:
        in_specs.append(pl.BlockSpec((None, tq, 2 * s - tq), lambda bi, h, qi: (h, 0, 0)))
        args.append(bias)
    return pl.pallas_call(
        functools.partial(_attn_kernel, scale=scale, nq=nq, tq=tq, has_bias=bias is not None),
        out_shape=jax.ShapeDtypeStruct((b, s, heads * dv), F32),
        grid=(b, heads, nq),
        in_specs=in_specs,
        out_specs=pl.BlockSpec((None, tq, dv), lambda bi, h, qi: (bi, qi, h)),
        compiler_params=_params("parallel", "parallel", "arbitrary"),
        name="attention_bias" if bias is not None else "attention",
    )(*args)


def _t5_bucket(rel):
    half = N_BUCKETS // 2
    exact = half // 2
    n = np.abs(rel)
    large = exact + (np.log(np.maximum(n, 1) / exact) / np.log(MAX_DISTANCE / exact) * (half - exact)).astype(np.int64)
    large = np.minimum(large, half - 1)
    return ((rel > 0) * half + np.where(n < exact, n, large)).astype(np.int32)


def dilated_bias_table(rel_bias, s, tq):
    heads = rel_bias.shape[1]
    d = np.arange(-(s - 1), s)
    count = np.zeros(d.shape, np.float32)
    for window, dil in A_PATTERNS:
        count += ((d % dil == 0) & (np.abs(d) <= (window // (2 * dil)) * dil)).astype(np.float32)
    logc = np.where(count > 0, np.log(np.maximum(count, 1.0)), NEG).astype(np.float32)
    onehot = (_t5_bucket(d)[:, None] == np.arange(N_BUCKETS)[None, :]).astype(np.float32)
    line = jnp.transpose(jnp.dot(onehot, rel_bias.astype(F32), precision=lax.Precision.HIGHEST)) + logc[None]
    width = 2 * s
    line = jnp.pad(line, ((0, 0), (0, width - line.shape[1])))[:, None, :]
    return pl.pallas_call(
        functools.partial(_skew_kernel, tq=tq),
        out_shape=jax.ShapeDtypeStruct((heads, tq, 2 * s - tq), F32),
        grid=(heads,),
        in_specs=[pl.BlockSpec((None, 1, width), lambda h: (h, 0, 0))],
        out_specs=pl.BlockSpec((None, tq, 2 * s - tq), lambda h: (h, 0, 0)),
        compiler_params=_params("parallel"),
        name="bias_skew",
    )(line)


def _skew_kernel(line_ref, o_ref, *, tq):
    width = line_ref.shape[1]
    x = jnp.broadcast_to(line_ref[...], (tq, width))
    x = pltpu.roll(x, width - (tq - 1), 1, stride=1, stride_axis=0)
    o_ref[...] = x[:, :o_ref.shape[1]]


GLA_UNROLL = 4


def _gla_kernel(q_ref, k_ref, v_ref, g_ref, z_ref, w2f_ref, w2b_ref, bf_ref, bb_ref, on_ref, o_ref,
                la_ref, acc_ref, qcat_ref, upd_ref, dec_ref, scat_ref, st_ref):
    s_len = q_ref.shape[0]
    c = B_CHUNK
    nchunk = s_len // c
    z = z_ref[...].astype(BF16)
    gate = lambda w2_ref, b_ref: jax.nn.log_sigmoid(
        jnp.dot(z, w2_ref[...], preferred_element_type=F32) + b_ref[...]) * (1.0 / B_GATE_TAU)
    la_ref[0] = gate(w2f_ref, bf_ref)
    la_ref[1] = gate(w2b_ref, bb_ref)

    ri = lax.broadcasted_iota(jnp.int32, (c, c), 0)
    ci = lax.broadcasted_iota(jnp.int32, (c, c), 1)
    keep = (ri >= ci, ri <= ci)
    tri3 = tuple(jnp.concatenate([kp.astype(BF16)] * 3, axis=1) for kp in keep)

    def chunk_rows(n):
        return pl.ds(pl.multiple_of(n * c, c), c)

    def pass1(i, carry):
        for u in range(GLA_UNROLL):
            n = i * GLA_UNROLL + u
            rows = chunk_rows(n)
            qc = q_ref[rows, :] * (B_DK ** -0.5)
            kc = k_ref[rows, :]
            vc = v_ref[rows, :].astype(BF16)
            o = None
            for d in range(2):
                la = la_ref[d, rows, :]
                gcum = jnp.dot(tri3[d], jnp.concatenate(_split3(la), axis=0), preferred_element_type=F32)
                gend = gcum[0:1] if d == 1 else gcum[c - 1:c]
                q_in = (qc * jnp.exp(gcum)).astype(BF16)
                k_in = (kc * jnp.exp(-gcum)).astype(BF16)
                k_out = (kc * jnp.exp(gend - gcum)).astype(BF16)
                att = lax.dot_general(q_in, k_in, NT_DIMS, preferred_element_type=F32)
                att = jnp.where(keep[d], att, 0.0).astype(BF16)
                od = jnp.dot(att, vc, preferred_element_type=F32)
                o = od if o is None else o + od
                qcat_ref[rows, d * B_DK:(d + 1) * B_DK] = q_in
                upd_ref[d, n] = lax.dot_general(vc, k_out, TN_DIMS, preferred_element_type=F32)
                dec_ref[d, n] = jnp.broadcast_to(jnp.exp(gend), (8, B_DK))
            acc_ref[rows, :] = o
        return carry

    lax.fori_loop(0, nchunk // GLA_UNROLL, pass1, 0)

    st_ref[...] = jnp.zeros_like(st_ref)

    def pass2(n, carry):
        for d, m in ((0, n), (1, nchunk - 1 - n)):
            state = st_ref[d]
            scat_ref[m, :, d * B_DK:(d + 1) * B_DK] = state.astype(BF16)
            st_ref[d] = state * dec_ref[d, m][0:1] + upd_ref[d, m]
        return carry

    lax.fori_loop(0, nchunk, pass2, 0)

    def pass3(i, carry):
        for u in range(GLA_UNROLL):
            n = i * GLA_UNROLL + u
            rows = chunk_rows(n)
            o = acc_ref[rows, :] + lax.dot_general(qcat_ref[rows, :], scat_ref[n], NT_DIMS,
                                                   preferred_element_type=F32)
            o = o * lax.rsqrt(jnp.mean(o * o, axis=-1, keepdims=True) + EPS) * on_ref[...]
            g = g_ref[rows, :]
            o_ref[rows, :] = o * (g * jax.nn.sigmoid(g))
        return carry

    lax.fori_loop(0, nchunk // GLA_UNROLL, pass3, 0)


def gla_mixer(proj, w2f, w2b, b_f, b_b, onorm, *, q_col, k_col, v_col, g_col, z_col):
    b, s, _ = proj.shape
    hm = lambda blk: (lambda bi, h: (bi, 0, blk + h))
    w2f_p = jnp.zeros((LANE, B_KEYW), F32).at[:B_GATE_RANK].set(w2f).astype(BF16)
    w2b_p = jnp.zeros((LANE, B_KEYW), F32).at[B_GATE_RANK:2 * B_GATE_RANK].set(w2b).astype(BF16)
    return pl.pallas_call(
        _gla_kernel,
        out_shape=jax.ShapeDtypeStruct((b, s, B_WIDTH), F32),
        grid=(b, B_HEADS),
        in_specs=[pl.BlockSpec((None, s, B_DK), hm(q_col // B_DK)),
                  pl.BlockSpec((None, s, B_DK), hm(k_col // B_DK)),
                  pl.BlockSpec((None, s, B_DV), hm(v_col // B_DV)),
                  pl.BlockSpec((None, s, B_DV), hm(g_col // B_DV)),
                  pl.BlockSpec((None, s, LANE), lambda bi, h: (bi, 0, z_col // LANE)),
                  pl.BlockSpec((LANE, B_DK), lambda bi, h: (0, h)),
                  pl.BlockSpec((LANE, B_DK), lambda bi, h: (0, h)),
                  pl.BlockSpec((1, B_DK), lambda bi, h: (0, h)),
                  pl.BlockSpec((1, B_DK), lambda bi, h: (0, h)),
                  pl.BlockSpec((1, B_DV), lambda bi, h: (0, 0))],
        out_specs=pl.BlockSpec((None, s, B_DV), lambda bi, h: (bi, 0, h)),
        scratch_shapes=[pltpu.VMEM((2, s, B_DK), F32),
                        pltpu.VMEM((s, B_DV), F32),
                        pltpu.VMEM((s, 2 * B_DK), BF16),
                        pltpu.VMEM((2, s // B_CHUNK, B_DV, B_DK), F32),
                        pltpu.VMEM((2, s // B_CHUNK, 8, B_DK), F32),
                        pltpu.VMEM((s // B_CHUNK, B_DV, 2 * B_DK), BF16),
                        pltpu.VMEM((2, B_DV, B_DK), F32)],
        compiler_params=_params("parallel", "arbitrary"),
        name="gla_mixer",
    )(proj, proj, proj, proj, proj, w2f_p, w2b_p, b_f.reshape(1, -1), b_b.reshape(1, -1), onorm.reshape(1, -1))


def _mla_up_kernel(cq_ref, ckv_ref, kr_ref, qn_ref, kvn_ref, wq_ref, wqr_ref, wkv_ref, cos_ref, sin_ref,
                   q_ref, k_ref, v_ref):
    def rms(x, g):
        return (x * lax.rsqrt(jnp.mean(x * x, axis=-1, keepdims=True) + EPS) * g).astype(BF16)

    cq = rms(cq_ref[...], qn_ref[...])
    ckv = rms(ckv_ref[...], kvn_ref[...])
    cos, sin = cos_ref[...], sin_ref[...]
    kr = kr_ref[...]
    k_rope = kr * cos + pltpu.roll(kr, LANE - C_ROPE, 1) * sin
    lane = lax.broadcasted_iota(jnp.int32, k_rope.shape, 1)
    k_rope = jnp.where(lane < C_ROPE, k_rope, 0.0)
    for h in range(C_HEADS):
        q = jnp.dot(cq, wq_ref[:, h * C_QK:(h + 1) * C_QK], preferred_element_type=F32)
        qp = jnp.dot(cq, wqr_ref[:, h * LANE:(h + 1) * LANE], preferred_element_type=F32)
        q_ref[:, h * C_QK:h * C_QK + C_NOPE] = q[:, :C_NOPE]
        q_ref[:, h * C_QK + C_NOPE:(h + 1) * C_QK] = q[:, C_NOPE:] * cos + qp * sin
        kv = jnp.dot(ckv, wkv_ref[:, h * 2 * LANE:(h + 1) * 2 * LANE], preferred_element_type=F32)
        k_ref[:, h * C_QK:h * C_QK + C_NOPE] = kv[:, :C_NOPE]
        k_ref[:, h * C_QK + C_NOPE:(h + 1) * C_QK] = k_rope
        v_ref[:, h * C_V:(h + 1) * C_V] = kv[:, C_NOPE:]


def _rot_half_cols(w):
    half = w.shape[-1] // 2
    return jnp.concatenate([-w[..., half:], w[..., :half]], axis=-1)


def mla_up(proj, q_norm, w_uq, kv_norm, w_ukv, cos, sin, *, col0, tm=512):
    t = proj.shape[0]
    wq = w_uq.reshape(C_Q_RANK, C_HEADS, C_NOPE + C_ROPE)
    wq_main = jnp.pad(wq, ((0, 0), (0, 0), (0, C_QK - C_NOPE - C_ROPE))).reshape(C_Q_RANK, C_HEADS * C_QK)
    wq_rot = jnp.pad(_rot_half_cols(wq[..., C_NOPE:]), ((0, 0), (0, 0), (0, LANE - C_ROPE)))
    wq_rot = wq_rot.reshape(C_Q_RANK, C_HEADS * LANE)
    row = lambda i: (i, 0)
    full = lambda arr: pl.BlockSpec(arr.shape, lambda i: (0, 0))
    g_q, g_kv = q_norm.reshape(1, -1), kv_norm.reshape(1, -1)
    wq_main, wq_rot, wkv = wq_main.astype(BF16), wq_rot.astype(BF16), w_ukv.astype(BF16)
    return pl.pallas_call(
        _mla_up_kernel,
        out_shape=(jax.ShapeDtypeStruct((t, C_HEADS * C_QK), F32),
                   jax.ShapeDtypeStruct((t, C_HEADS * C_QK), F32),
                   jax.ShapeDtypeStruct((t, C_WIDTH), F32)),
        grid=(t // tm,),
        in_specs=[pl.BlockSpec((tm, C_Q_RANK), lambda i: (i, col0 // C_Q_RANK)),
                  pl.BlockSpec((tm, C_KV_RANK), lambda i: (i, (col0 + C_Q_RANK) // C_KV_RANK)),
                  pl.BlockSpec((tm, LANE), lambda i: (i, (col0 + C_Q_RANK + C_KV_RANK) // LANE)),
                  full(g_q), full(g_kv), full(wq_main), full(wq_rot), full(wkv),
                  pl.BlockSpec((tm, LANE), row), pl.BlockSpec((tm, LANE), row)],
        out_specs=(pl.BlockSpec((tm, C_HEADS * C_QK), row),
                   pl.BlockSpec((tm, C_HEADS * C_QK), row),
                   pl.BlockSpec((tm, C_WIDTH), row)),
        compiler_params=_params("parallel"),
        name="mla_up",
    )(proj, proj, proj, g_q, g_kv, wq_main, wq_rot, wkv, cos, sin)


RG = 4
RGW = RG * D_HEAD
RCH = 64
DC_R, DC_K, DC_V = 0, D_WIDTH, 2 * D_WIDTH
DC_ZG = 3 * D_WIDTH
DC_ZW = DC_ZG + D_G_RANK
DC_ZA = DC_ZW + 2 * D_W_RANK
DC_PAD = 7 * 512


def _head_sums(x, bo):
    return jnp.concatenate(
        [jnp.dot(x[:, RGW * g:RGW * (g + 1)], bo, preferred_element_type=F32, precision=lax.Precision.HIGHEST)
         for g in range(x.shape[1] // RGW)], axis=1)


def _block_ones():
    i = np.arange(RGW)
    return jnp.asarray((i[:, None] // D_HEAD) == (i[None, :] // D_HEAD), F32)


def _rwkv_prep_kernel(x_ref, xp_ref, xn_ref, mu_ref, w2f_ref, w2b_ref, a2_ref, g2_ref, w0f_ref, w0b_ref,
                      a0_ref, kk_ref, ka_ref, rk_ref, bo_ref,
                      r_ref, k_ref, v_ref, a_ref, b_ref, lwf_ref, lwb_ref, g_ref, bonus_ref, *, tiles_per_seq):
    i = pl.program_id(0) % tiles_per_seq
    x = x_ref[...]
    tm = x.shape[0]
    row = lax.broadcasted_iota(jnp.int32, x.shape, 0)
    prev_row = jnp.where(i == 0, 0.0, xp_ref[7:8, :])
    next_row = jnp.where(i == tiles_per_seq - 1, 0.0, xn_ref[0:1, :])
    prev = jnp.where(row == 0, prev_row, pltpu.roll(x, 1, 0))
    nxt = jnp.where(row == tm - 1, next_row, pltpu.roll(x, tm - 1, 0))
    x = x + mu_ref[...] * (0.5 * (prev + nxt) - x)
    r, k, v = x[:, DC_R:DC_R + D_WIDTH], x[:, DC_K:DC_K + D_WIDTH], x[:, DC_V:DC_V + D_WIDTH]
    zg = x[:, DC_ZG:DC_ZG + LANE]
    zw = x[:, DC_ZW:DC_ZW + LANE]
    za = x[:, DC_ZA:DC_ZA + LANE]
    tz = jnp.tanh(zw).astype(BF16)
    log_decay = lambda w0_ref, w2_ref: -np.exp(-0.5).astype(np.float32) * jax.nn.sigmoid(
        w0_ref[...] + jnp.dot(tz, w2_ref[...], preferred_element_type=F32))
    lwf_ref[...] = log_decay(w0f_ref, w2f_ref)
    lwb_ref[...] = log_decay(w0b_ref, w2b_ref)
    ag = jax.nn.sigmoid(a0_ref[...] + jnp.dot(za.astype(BF16), a2_ref[...], preferred_element_type=F32))
    g_ref[...] = jnp.dot(jax.nn.sigmoid(zg).astype(BF16), g2_ref[...], preferred_element_type=F32)
    bo = bo_ref[...]
    kk = k * kk_ref[...]
    kk = kk / jnp.maximum(jnp.sqrt(_head_sums(kk * kk, bo)), 1e-12)
    k = k * (1.0 + (ag - 1.0) * ka_ref[...])
    r_ref[...] = r
    k_ref[...] = k
    v_ref[...] = v
    a_ref[...] = -kk
    b_ref[...] = kk * ag
    bonus_ref[...] = _head_sums(r * k * rk_ref[...], bo) * v


def rwkv_prep(proj, mu, w0_f, w2_f, w0_b, w2_b, a0, a2, g2, k_k, k_a, r_k, *, seq, tm=256):
    t = proj.shape[0]
    tiles_per_seq = seq // tm
    hb = tm // 8
    nblk8 = t // 8
    pad_rows = lambda w, lo: jnp.zeros((LANE, D_WIDTH), F32).at[lo:lo + w.shape[0]].set(w).astype(BF16)
    vec = lambda u: u.reshape(1, -1)
    consts = [vec(mu), pad_rows(w2_f, 0), pad_rows(w2_b, D_W_RANK), pad_rows(a2, 0), g2.astype(BF16),
              vec(w0_f), vec(w0_b), vec(a0), vec(k_k), vec(k_a), vec(r_k), _block_ones()]
    full = lambda arr: pl.BlockSpec(arr.shape, lambda i: (0, 0))
    out_spec = pl.BlockSpec((tm, D_WIDTH), lambda i: (i, 0))
    return pl.pallas_call(
        functools.partial(_rwkv_prep_kernel, tiles_per_seq=tiles_per_seq),
        out_shape=tuple(jax.ShapeDtypeStruct((t, D_WIDTH), F32) for _ in range(9)),
        grid=(t // tm,),
        in_specs=[pl.BlockSpec((tm, DC_PAD), lambda i: (i, 0)),
                  pl.BlockSpec((8, DC_PAD), lambda i: (jnp.maximum(i * hb - 1, 0), 0)),
                  pl.BlockSpec((8, DC_PAD), lambda i: (jnp.minimum((i + 1) * hb, nblk8 - 1), 0))]
                 + [full(c) for c in consts],
        out_specs=tuple(out_spec for _ in range(9)),
        compiler_params=_params("parallel"),
        name="rwkv_prep",
    )(proj, proj, proj, *consts)


def _rwkv_chunk_kernel(*refs, ngroups):
    ins, (yf_ref, yb_ref, mt_ref) = refs[:12], refs[12:]

    @pl.when(pl.program_id(1) == 0)
    def _():
        mt_ref[...] = jnp.zeros_like(mt_ref)

    row = lax.broadcasted_iota(jnp.int32, (RCH, RGW), 0)
    col = lax.broadcasted_iota(jnp.int32, (RCH, RGW), 1) & (RCH - 1)
    bdmask = (lax.broadcasted_iota(jnp.int32, (RGW, RGW), 0) // D_HEAD
              == lax.broadcasted_iota(jnp.int32, (RGW, RGW), 1) // D_HEAD)
    tr = lax.broadcasted_iota(jnp.int32, (RCH, RCH), 0)
    tc = lax.broadcasted_iota(jnp.int32, (RCH, RCH), 1)
    zero = jnp.zeros((), F32)

    def bd(z):
        zb = z.astype(BF16)
        return jnp.where(bdmask, jnp.concatenate([zb] * RG, axis=0), jnp.zeros((), BF16))

    def mm(x, y, dims=None):
        x = x.astype(BF16)
        if dims is None:
            return jnp.dot(x, y, preferred_element_type=F32)
        return lax.dot_general(x, y, dims, preferred_element_type=F32)

    chains = [(d, g) for d in range(2) for g in range(ngroups)]
    st = []
    for d, g in chains:
        backward = d == 1
        r_ref, k_ref, v_ref, a_ref, b_ref, lw_ref = ins[6 * d:6 * d + 6]
        tri = ((tc >= tr) if backward else (tc <= tr)).astype(BF16)
        sl = slice(RGW * g, RGW * (g + 1))
        r, k, v, a, b, lw = (ref[:, sl] for ref in (r_ref, k_ref, v_ref, a_ref, b_ref, lw_ref))
        lam = jnp.dot(jnp.concatenate([tri] * 3, axis=1), jnp.concatenate(_split3(lw), axis=0),
                      preferred_element_type=F32)
        lamc = lam[0:1] if backward else lam[RCH - 1:RCH]
        e_inv = jnp.exp(-lam)
        e_out = jnp.exp(lamc - lam)
        ar = jnp.concatenate([a * jnp.exp(lam - lw), r * jnp.exp(lam)], axis=0).astype(BF16)
        bk = jnp.concatenate([b * e_out, k * e_out], axis=0).astype(BF16)
        st.append(dict(ar=ar, bk=bk, v=v, lamc=lamc, sl=sl,
                       gb=mm(ar, bd(b * e_inv), NT_DIMS), gk=mm(ar, bd(k * e_inv), NT_DIMS)))
    for (d, g), c in zip(chains, st):
        strict = (col > row) if d == 1 else (col < row)
        incl = (col >= row) if d == 1 else (col <= row)
        c["lp"] = jnp.where(strict, c["gb"][:RCH], zero)
        lak = jnp.where(strict, c["gk"][:RCH], zero)
        c["grb"] = jnp.where(incl, c["gb"][RCH:], zero).astype(BF16)
        c["grk"] = jnp.where(incl, c["gk"][RCH:], zero).astype(BF16)
        c["mt"] = mt_ref[d, g]
        amrm = mm(c["ar"], c["mt"].astype(BF16), NT_DIMS)
        c["bdv"] = bd(c["v"])
        c["u"] = amrm[:RCH] + mm(lak, c["bdv"])
        c["rm"] = amrm[RCH:]
    for rnd in range(6):
        for c in st:
            lpb = c["lp"].astype(BF16)
            c["u"] = c["u"] + mm(lpb, bd(c["u"]))
            if rnd < 5:
                c["lp"] = mm(lpb, bd(c["lp"]))
    for (d, g), c in zip(chains, st):
        y_ref = yb_ref if d == 1 else yf_ref
        y_ref[:, c["sl"]] = c["rm"] + mm(c["grb"], bd(c["u"])) + mm(c["grk"], c["bdv"])
        uv = jnp.concatenate([c["u"], c["v"]], axis=0).astype(BF16)
        upd = lax.dot_general(uv, c["bk"], TN_DIMS, preferred_element_type=F32)
        mt_ref[d, g] = c["mt"] * jnp.exp(c["lamc"]) + jnp.where(bdmask, upd, zero)


def rwkv_chunked(r, k, v, a, b, lwf, lwb):
    bsz, s, wd = r.shape
    nc = s // RCH
    fspec = pl.BlockSpec((None, RCH, wd), lambda bi, n: (bi, n, 0))
    bspec = pl.BlockSpec((None, RCH, wd), lambda bi, n: (bi, nc - 1 - n, 0))
    return pl.pallas_call(
        functools.partial(_rwkv_chunk_kernel, ngroups=wd // RGW),
        out_shape=(jax.ShapeDtypeStruct((bsz, s, wd), F32), jax.ShapeDtypeStruct((bsz, s, wd), F32)),
        grid=(bsz, nc),
        in_specs=[fspec] * 6 + [bspec] * 6,
        out_specs=(fspec, bspec),
        scratch_shapes=[pltpu.VMEM((2, wd // RGW, RGW, RGW), F32)],
        compiler_params=_params("parallel", "arbitrary"),
        name="rwkv_chunked",
    )(r, k, v, a, b, lwf, r, k, v, a, b, lwb)


def _rwkv_post_kernel(yf_ref, yb_ref, bonus_ref, g_ref, lng_ref, lnb_ref, bo_ref, o_ref):
    bo = bo_ref[...]
    y = yf_ref[...] + yb_ref[...]
    yc = y - _head_sums(y, bo) * (1.0 / D_HEAD)
    var = _head_sums(yc * yc, bo) * (1.0 / D_HEAD)
    y = yc * lax.rsqrt(var + D_LN_EPS) * lng_ref[...] + lnb_ref[...]
    o_ref[...] = (y + bonus_ref[...]) * g_ref[...]


def rwkv_post(yf, yb, bonus, g, ln_g, ln_b, *, tm=512):
    t, wd = yf.shape
    row = pl.BlockSpec((tm, wd), lambda i: (i, 0))
    vec = pl.BlockSpec((1, wd), lambda i: (0, 0))
    bo = _block_ones()
    return pl.pallas_call(
        _rwkv_post_kernel,
        out_shape=jax.ShapeDtypeStruct((t, wd), F32),
        grid=(t // tm,),
        in_specs=[row, row, row, row, vec, vec, pl.BlockSpec(bo.shape, lambda i: (0, 0))],
        out_specs=row,
        compiler_params=_params("parallel"),
        name="rwkv_post",
    )(yf, yb, bonus, g, ln_g.reshape(1, wd), ln_b.reshape(1, wd), bo)


def rwkv7_mixer(proj, bsz, s, mu, w0_f, w2_f, w0_b, w2_b, a0, a2, g2, k_k, k_a, r_k, ln_g, ln_b):
    r, k, v, a, b, lwf, lwb, g, bonus = rwkv_prep(proj, mu, w0_f, w2_f, w0_b, w2_b, a0, a2, g2, k_k, k_a,
                                                  r_k.reshape(-1), seq=s)
    r3 = lambda u: u.reshape(bsz, s, D_WIDTH)
    yf, yb = rwkv_chunked(r3(r), r3(k), r3(v), r3(a), r3(b), r3(lwf), r3(lwb))
    return rwkv_post(yf.reshape(-1, D_WIDTH), yb.reshape(-1, D_WIDTH), bonus, g, ln_g, ln_b)


MOE_TILE = 1024
MOE_SUB = 128


def _route(logit):
    lane = lax.broadcasted_iota(jnp.int32, logit.shape, 1)
    first_at = lambda mask: jnp.min(jnp.where(mask, lane, jnp.int32(LANE)), axis=-1, keepdims=True)
    is_grp = lane < N_GROUPS
    gl = jnp.where(is_grp, logit, NEG)
    gmax = jnp.max(gl, axis=-1, keepdims=True)
    p_grp = 1.0 / jnp.sum(jnp.where(is_grp, jnp.exp(gl - gmax), 0.0), axis=-1, keepdims=True)
    i_grp = first_at(is_grp & (gl == gmax))
    lo = N_GROUPS + i_grp * EXPERTS_PER_GROUP
    in_grp = (lane >= lo) & (lane < lo + EXPERTS_PER_GROUP)
    el = jnp.where(in_grp, logit, NEG)
    l1 = jnp.max(el, axis=-1, keepdims=True)
    i1 = first_at(in_grp & (el == l1))
    rest = in_grp & (lane != i1)
    el2 = jnp.where(rest, logit, NEG)
    l2 = jnp.max(el2, axis=-1, keepdims=True)
    i2 = first_at(rest & (el2 == l2))
    e2 = jnp.exp(l2 - l1)
    w1 = p_grp / (1.0 + e2)
    w2 = p_grp * e2 / (1.0 + e2)
    return i_grp, jnp.where(lane == i1, w1, jnp.where(lane == i2, w2, 0.0))


def _moe_sort_kernel(x_ref, g_ref, wr_ref, br_ref, hn_ref, comb_ref, pos_ref, off_ref):
    x = x_ref[...]
    tm = x.shape[0]
    hn = (x * lax.rsqrt(jnp.mean(x * x, axis=-1, keepdims=True) + EPS) * g_ref[...]).astype(BF16)
    wr, br = wr_ref[...], br_ref[...]
    i_grp, _ = _route(jnp.dot(hn, wr, preferred_element_type=F32) + br)
    lane = lax.broadcasted_iota(jnp.int32, (tm, LANE), 1)
    onehot = (lane == i_grp).astype(F32)
    ri = lax.broadcasted_iota(jnp.int32, (tm, tm), 0)
    ci = lax.broadcasted_iota(jnp.int32, (tm, tm), 1)
    earlier = jnp.dot((ci < ri).astype(BF16), onehot.astype(BF16), preferred_element_type=F32)
    cnt = jnp.broadcast_to(jnp.sum(onehot, axis=0, keepdims=True), (8, LANE))
    lane8 = lax.broadcasted_iota(jnp.int32, (8, LANE), 1)
    start = jnp.zeros((8, LANE), F32)
    for sh in range(1, N_GROUPS):
        start += jnp.where(lane8 >= sh, pltpu.roll(cnt, sh, 1), 0.0)
    off_ref[...] = start[0:1].astype(jnp.int32)
    pos = jnp.sum(onehot * (start[0:1] + earlier), axis=-1, keepdims=True).astype(jnp.int32)
    pos_ref[...] = pos
    perm_t = (ci == pos).astype(BF16)
    hn_s = lax.dot_general(perm_t, hn, TN_DIMS, preferred_element_type=F32).astype(BF16)
    hn_ref[...] = hn_s
    _, comb = _route(jnp.dot(hn_s, wr, preferred_element_type=F32) + br)
    comb_ref[...] = comb


def moe_sort(x, g, w_grp, b_grp, w_exp, b_exp):
    t, d = x.shape
    tm = MOE_TILE
    nr = N_GROUPS + N_EXPERTS
    wr = jnp.pad(jnp.concatenate([w_grp, w_exp], axis=1), ((0, 0), (0, LANE - nr))).astype(BF16)
    br = jnp.pad(jnp.concatenate([b_grp, b_exp]), (0, LANE - nr)).reshape(1, LANE)
    row = lambda i: (i, 0)
    hn, comb, pos, off = pl.pallas_call(
        _moe_sort_kernel,
        out_shape=(jax.ShapeDtypeStruct((t, d), BF16), jax.ShapeDtypeStruct((t, LANE), F32),
                   jax.ShapeDtypeStruct((t, 1), jnp.int32), jax.ShapeDtypeStruct((t // tm, 1, LANE), jnp.int32)),
        grid=(t // tm,),
        in_specs=[pl.BlockSpec((tm, d), row), pl.BlockSpec((1, d), lambda i: (0, 0)),
                  pl.BlockSpec((d, LANE), lambda i: (0, 0)), pl.BlockSpec((1, LANE), lambda i: (0, 0))],
        out_specs=(pl.BlockSpec((tm, d), row), pl.BlockSpec((tm, LANE), row), pl.BlockSpec((tm, 1), row),
                   pl.BlockSpec((None, 1, LANE), lambda i: (i, 0, 0))),
        compiler_params=_params("parallel"),
        name="moe_sort",
    )(x, g.reshape(1, d), wr, br)
    bounds = jnp.concatenate([off[:, 0, :N_GROUPS], jnp.full((t // tm, 1), tm, jnp.int32)], axis=1)
    return hn, comb[:, N_GROUPS:nr], pos, bounds.reshape(-1)


def _moe_group_kernel(bounds_ref, hn_ref, c_ref, wg_ref, wu_ref, wd_ref, y_ref):
    i, g, j = pl.program_id(0), pl.program_id(1), pl.program_id(2)

    @pl.when((g == 0) & (j == 0))
    def _():
        y_ref[...] = jnp.zeros_like(y_ref)

    base = i * (N_GROUPS + 1) + g
    start, end = bounds_ref[base], bounds_ref[base + 1]

    def sub(k, carry):
        rows = pl.ds(pl.multiple_of(k * MOE_SUB, MOE_SUB), MOE_SUB)
        x = hn_ref[rows, :]
        hg = jnp.dot(x, wg_ref[...], preferred_element_type=F32)
        hu = jnp.dot(x, wu_ref[...], preferred_element_type=F32)
        hid = (hg * jax.nn.sigmoid(hg)) * hu * c_ref[rows, :]
        y_ref[rows, :] += jnp.dot(hid.astype(BF16), wd_ref[...], preferred_element_type=F32)
        return carry

    lax.fori_loop(start // MOE_SUB, (end + MOE_SUB - 1) // MOE_SUB, sub, 0)


def moe_group_experts(hn, comb, bounds, w_gate, w_up, w_down):
    t, d = hn.shape
    tm = MOE_TILE
    ne, _, ff = w_gate.shape
    comb_e = jnp.transpose(comb)[:, :, None]
    ex = lambda i, g, j, b: (g * EXPERTS_PER_GROUP + j, 0, 0)
    return pl.pallas_call(
        _moe_group_kernel,
        out_shape=jax.ShapeDtypeStruct((t, d), F32),
        grid_spec=pltpu.PrefetchScalarGridSpec(
            num_scalar_prefetch=1,
            grid=(t // tm, N_GROUPS, EXPERTS_PER_GROUP),
            in_specs=[pl.BlockSpec((tm, d), lambda i, g, j, b: (i, 0)),
                      pl.BlockSpec((None, tm, 1), lambda i, g, j, b: (g * EXPERTS_PER_GROUP + j, i, 0)),
                      pl.BlockSpec((None, d, ff), ex), pl.BlockSpec((None, d, ff), ex),
                      pl.BlockSpec((None, ff, d), ex)],
            out_specs=pl.BlockSpec((tm, d), lambda i, g, j, b: (i, 0))),
        compiler_params=_params("parallel", "arbitrary", "arbitrary"),
        name="moe_group_experts",
    )(bounds, hn, comb_e, w_gate, w_up, w_down)


def _moe_unsort_kernel(y_ref, pos_ref, x_ref, o_ref):
    tm = y_ref.shape[0]
    perm_t = (lax.broadcasted_iota(jnp.int32, (tm, tm), 1) == pos_ref[...]).astype(BF16)
    y = y_ref[...]
    hi = y.astype(BF16)
    lo = (y - hi.astype(F32)).astype(BF16)
    o_ref[...] = (x_ref[...] + jnp.dot(perm_t, hi, preferred_element_type=F32)
                  + jnp.dot(perm_t, lo, preferred_element_type=F32))


def moe_unsort(y, pos, x):
    t, d = x.shape
    tm, tn = MOE_TILE, d // 2
    blk = pl.BlockSpec((tm, tn), lambda i, j: (i, j))
    return pl.pallas_call(
        _moe_unsort_kernel,
        out_shape=jax.ShapeDtypeStruct((t, d), F32),
        grid=(t // tm, d // tn),
        in_specs=[blk, pl.BlockSpec((tm, 1), lambda i, j: (i, 0)), blk],
        out_specs=blk,
        compiler_params=_params("parallel", "arbitrary"),
        name="moe_unsort",
    )(y, pos, x)


def hier_moe(x, g, w_grp, b_grp, w_exp, b_exp, w_gate, w_up, w_down):
    hn, comb, pos, bounds = moe_sort(x, g, w_grp, b_grp, w_exp, b_exp)
    y = moe_group_experts(hn, comb, bounds, w_gate.astype(BF16), w_up.astype(BF16), w_down.astype(BF16))
    return moe_unsort(y, pos, x)


def even_layer(x, bsz, s, norm_g, rel_bias, w_in, w_out, w2_f, b_f, w2_b, b_b, onorm):
    n_pad = _round_up(EVEN_IN, LANE)
    w_in_p = jnp.pad(w_in, ((0, 0), (0, n_pad - EVEN_IN))).astype(BF16)
    proj = norm_linear(x, norm_g, w_in_p, tn_target=896).reshape(bsz, s, n_pad)
    tq = 256
    ya = attention(proj, proj, proj, heads=A_HEADS, dq=HEAD_DIM, dv=HEAD_DIM,
                   q_off=0, k_off=A_HEADS, v_off=2 * A_HEADS, scale=HEAD_DIM ** -0.5,
                   bias=dilated_bias_table(rel_bias, s, tq), tq=tq)
    q_col = 3 * A_WIDTH
    yb = gla_mixer(proj, w2_f, w2_b, b_f, b_b, onorm, q_col=q_col, k_col=q_col + B_KEYW,
                   v_col=q_col + 2 * B_KEYW, g_col=q_col + 2 * B_KEYW + B_WIDTH,
                   z_col=q_col + 2 * B_KEYW + 2 * B_WIDTH)
    t = bsz * s
    return out_proj(ya.reshape(t, A_WIDTH), yb.reshape(t, B_WIDTH), w_out.astype(BF16), x)


def _odd_columns(w_in, mu):
    c0 = C_IN
    cut = lambda u, lo, n: u[..., lo:lo + n]
    zpad = lambda u, n: jnp.pad(u, [(0, 0)] * (u.ndim - 1) + [(0, n)])
    off = np.cumsum((0,) + D_SPLITS)
    def rwkv_cols(u):
        parts = [cut(u, off[0], 3 * D_WIDTH), cut(u, off[6], D_G_RANK), cut(u, off[3], 2 * D_W_RANK),
                 cut(u, off[5], D_A_RANK)]
        u = jnp.concatenate(parts, axis=-1)
        return zpad(u, DC_PAD - u.shape[-1])
    w_kr = w_in[:, C_Q_RANK + C_KV_RANK:C_IN]
    w_all = jnp.concatenate([rwkv_cols(w_in[:, c0:]), w_in[:, :C_IN], _rot_half_cols(w_kr)], axis=1)
    n_pad = _round_up(w_all.shape[1], 9 * LANE)
    return zpad(w_all, n_pad - w_all.shape[1]).astype(BF16), rwkv_cols(mu)


def odd_layer(x, bsz, s, norm_g, w_in, w_out, q_norm, w_uq, kv_norm, w_ukv, mu, w0_f, w2_f, w0_b, w2_b,
              a0, a2, g2, k_k, k_a, r_k, ln_g, ln_b):
    t = bsz * s
    w_all, mu_cols = _odd_columns(w_in, mu)
    proj = norm_linear(x, norm_g, w_all, tn_target=1152)
    inv = 1.0 / (ROPE_THETA ** (jnp.arange(0, C_ROPE, 2, dtype=F32) / C_ROPE))
    ang = jnp.arange(s, dtype=F32)[:, None] * inv[None, :]
    cos = jnp.pad(jnp.concatenate([jnp.cos(ang)] * 2, axis=1), ((0, 0), (0, LANE - C_ROPE)), constant_values=1.0)
    sin = jnp.pad(jnp.concatenate([jnp.sin(ang)] * 2, axis=1), ((0, 0), (0, LANE - C_ROPE)))
    q, k, v = mla_up(proj, q_norm, w_uq, kv_norm, w_ukv, jnp.tile(cos, (bsz, 1)), jnp.tile(sin, (bsz, 1)),
                     col0=DC_PAD)
    r3 = lambda u: u.reshape(bsz, s, -1)
    yc = attention(r3(q), r3(k), r3(v), heads=C_HEADS, dq=C_QK, dv=C_V, q_off=0, k_off=0, v_off=0,
                   scale=(C_NOPE + C_ROPE) ** -0.5)
    yd = rwkv7_mixer(proj, bsz, s, mu_cols, w0_f, w2_f, w0_b, w2_b, a0, a2, g2, k_k, k_a, r_k, ln_g, ln_b)
    return out_proj(yc.reshape(t, C_WIDTH), yd, w_out.astype(BF16), x)


def kernel(x_prompt, x_sample, rel_bias, norm_mix, norm_ffn, norm_final, ev_w_in, ev_w_out, ev_gla_w2_f, ev_gla_b_f, ev_gla_w2_b, ev_gla_b_b, ev_gla_onorm, od_w_in, od_w_out, od_q_norm, od_w_uq, od_kv_norm, od_w_ukv, od_mu, od_w0_f, od_w2_f, od_w0_b, od_w2_b, od_a0, od_a2, od_g2, od_k_k, od_k_a, od_r_k, od_ln_g, od_ln_b, moe_w_grp, moe_b_grp, moe_w_exp, moe_b_exp, moe_w_gate, moe_w_up, moe_w_down):
    nb_p = x_prompt.shape[0]
    x = jnp.concatenate([x_prompt, x_sample], axis=0)
    bsz, s, d = x.shape
    x = x.reshape(bsz * s, d)
    for i in range(DEPTH):
        j = i // 2
        if i % 2 == 0:
            x = even_layer(x, bsz, s, norm_mix[i], rel_bias, ev_w_in[j], ev_w_out[j], ev_gla_w2_f[j],
                           ev_gla_b_f[j], ev_gla_w2_b[j], ev_gla_b_b[j], ev_gla_onorm[j])
        else:
            x = odd_layer(x, bsz, s, norm_mix[i], od_w_in[j], od_w_out[j], od_q_norm[j], od_w_uq[j],
                          od_kv_norm[j], od_w_ukv[j], od_mu[j], od_w0_f[j], od_w2_f[j], od_w0_b[j],
                          od_w2_b[j], od_a0[j], od_a2[j], od_g2[j], od_k_k[j], od_k_a[j], od_r_k[j],
                          od_ln_g[j], od_ln_b[j])
        x = hier_moe(x, norm_ffn[i], moe_w_grp[i], moe_b_grp[i], moe_w_exp[i], moe_b_exp[i],
                     moe_w_gate[i], moe_w_up[i], moe_w_down[i])
    y = final_norm(x, norm_final).reshape(bsz, s, d)
    return (y[:nb_p], y[nb_p:])
```

```python
import functools

import jax, jax.numpy as jnp
from jax import lax
import numpy as np
from jax.experimental import pallas as pl
from jax.experimental.pallas import tpu as pltpu

F32, BF16 = jnp.float32, jnp.bfloat16

D_MODEL = 2048
DEPTH = 2
MIX_HALF = D_MODEL // 2
HEAD_DIM = 128
EPS = 1e-6
NEG = -1e30

A_HEADS = MIX_HALF // HEAD_DIM
A_WIDTH = A_HEADS * HEAD_DIM
A_PATTERNS = ((128, 1), (512, 4), (2048, 16))
N_BUCKETS = 32
MAX_DISTANCE = 1024

B_HEADS = 4
B_DV = MIX_HALF // B_HEADS
B_DK = B_DV // 2
B_WIDTH = B_HEADS * B_DV
B_KEYW = B_HEADS * B_DK
B_GATE_RANK = 16
B_GATE_TAU = 16.0
B_CHUNK = 64

C_HEADS = MIX_HALF // 128
C_Q_RANK = 512
C_KV_RANK = 256
C_NOPE = 128
C_ROPE = 64
C_V = 128
C_WIDTH = C_HEADS * C_V
C_QK = 256
ROPE_THETA = 10000.0

D_HEAD = 64
D_HEADS = MIX_HALF // D_HEAD
D_WIDTH = D_HEADS * D_HEAD
D_W_RANK = 64
D_A_RANK = 64
D_G_RANK = 128
D_LN_EPS = 64e-5
D_SPLITS = (D_WIDTH, D_WIDTH, D_WIDTH, D_W_RANK, D_W_RANK, D_A_RANK, D_G_RANK)
D_SHIFT = 3 * D_WIDTH + 2 * D_W_RANK + D_A_RANK + D_G_RANK

N_GROUPS = 4
EXPERTS_PER_GROUP = 4
N_EXPERTS = N_GROUPS * EXPERTS_PER_GROUP
EXPERT_FF = 512

EVEN_IN = 3 * A_WIDTH + 2 * B_KEYW + 2 * B_WIDTH + 2 * B_GATE_RANK
C_IN = C_Q_RANK + C_KV_RANK + C_ROPE
ODD_IN = C_IN + D_SHIFT

LANE = 128
VMEM_LIMIT = 52 * 1024 * 1024


def _params(*sem):
    return pltpu.CompilerParams(dimension_semantics=sem, vmem_limit_bytes=VMEM_LIMIT)


def _round_up(n, m):
    return -(-n // m) * m


NT_DIMS = (((1,), (1,)), ((), ()))
TN_DIMS = (((0,), (0,)), ((), ()))


def _split3(x):
    hi = x.astype(BF16)
    r1 = x - hi.astype(F32)
    mid = r1.astype(BF16)
    lo = (r1 - mid.astype(F32)).astype(BF16)
    return hi, mid, lo


def _pick_tile(n, target):
    best = LANE
    for t in range(LANE, target + 1, LANE):
        if n % t == 0:
            best = t
    return best


def _norm_linear_kernel(x_ref, g_ref, w_ref, o_ref, xn_ref):
    @pl.when(pl.program_id(1) == 0)
    def _():
        x = x_ref[...]
        y = x * lax.rsqrt(jnp.mean(x * x, axis=-1, keepdims=True) + EPS) * g_ref[...]
        xn_ref[...] = y.astype(BF16)

    o_ref[...] = jnp.dot(xn_ref[...], w_ref[...], preferred_element_type=F32)


def norm_linear(x, g, w, *, tm=1024, tn_target=1024):
    t, k = x.shape
    n = w.shape[1]
    tn = _pick_tile(n, tn_target)
    return pl.pallas_call(
        _norm_linear_kernel,
        out_shape=jax.ShapeDtypeStruct((t, n), F32),
        grid=(t // tm, n // tn),
        in_specs=[pl.BlockSpec((tm, k), lambda i, j: (i, 0)),
                  pl.BlockSpec((1, k), lambda i, j: (0, 0)),
                  pl.BlockSpec((k, tn), lambda i, j: (0, j))],
        out_specs=pl.BlockSpec((tm, tn), lambda i, j: (i, j)),
        scratch_shapes=[pltpu.VMEM((tm, k), BF16)],
        compiler_params=_params("parallel", "arbitrary"),
        name="norm_linear",
    )(x, g.reshape(1, k), w)


def _out_proj_kernel(a_ref, b_ref, wa_ref, wb_ref, x_ref, o_ref):
    acc = jnp.dot(a_ref[...], wa_ref[...], preferred_element_type=F32)
    acc += jnp.dot(b_ref[...], wb_ref[...], preferred_element_type=F32)
    o_ref[...] = x_ref[...] + acc


def out_proj(a, b, w, x, *, tm=1024, tn=512):
    t, ka = a.shape
    kb = b.shape[1]
    n = w.shape[1]
    return pl.pallas_call(
        _out_proj_kernel,
        out_shape=jax.ShapeDtypeStruct((t, n), F32),
        grid=(t // tm, n // tn),
        in_specs=[pl.BlockSpec((tm, ka), lambda i, j: (i, 0)),
                  pl.BlockSpec((tm, kb), lambda i, j: (i, 0)),
                  pl.BlockSpec((ka, tn), lambda i, j: (0, j)),
                  pl.BlockSpec((kb, tn), lambda i, j: (0, j)),
                  pl.BlockSpec((tm, tn), lambda i, j: (i, j))],
        out_specs=pl.BlockSpec((tm, tn), lambda i, j: (i, j)),
        compiler_params=_params("parallel", "arbitrary"),
        name="out_proj",
    )(a, b, w[:ka], w[ka:], x)


def _final_norm_kernel(x_ref, g_ref, o_ref):
    x = x_ref[...]
    o_ref[...] = x * lax.rsqrt(jnp.mean(x * x, axis=-1, keepdims=True) + EPS) * g_ref[...]


def final_norm(x, g, *, row0, rows, tm=1024):
    d = x.shape[1]
    return pl.pallas_call(
        _final_norm_kernel,
        out_shape=jax.ShapeDtypeStruct((rows, d), F32),
        grid=(rows // tm,),
        in_specs=[pl.BlockSpec((tm, d), lambda i: (i + row0 // tm, 0)), pl.BlockSpec((1, d), lambda i: (0, 0))],
        out_specs=pl.BlockSpec((tm, d), lambda i: (i, 0)),
        compiler_params=_params("parallel"),
        name="final_norm",
    )(x, g.reshape(1, d))


def _attn_kernel(*refs, scale, nq, tq, has_bias):
    if has_bias:
        q_ref, k_ref, v_ref, bias_ref, o_ref = refs
    else:
        q_ref, k_ref, v_ref, o_ref = refs
    s_len = k_ref.shape[0]
    q = (q_ref[...] * scale).astype(BF16)
    s = lax.dot_general(q, k_ref[...].astype(BF16), (((1,), (1,)), ((), ())), preferred_element_type=F32)
    if has_bias:
        start = pl.multiple_of((nq - 1 - pl.program_id(2)) * tq, LANE)
        s = s + bias_ref[:, pl.ds(start, s_len)]
    m = jnp.max(s, axis=-1, keepdims=True)
    p = jnp.exp(s - m)
    den = jnp.sum(p, axis=-1, keepdims=True)
    o = jnp.dot(p.astype(BF16), v_ref[...].astype(BF16), preferred_element_type=F32)
    o_ref[...] = (o / den).astype(o_ref.dtype)


def attention(q, k, v, *, heads, dq, dv, q_off, k_off, v_off, scale, bias=None, tq=256):
    b, s, _ = q.shape
    nq = s // tq
    in_specs = [pl.BlockSpec((None, tq, dq), lambda bi, h, qi: (bi, qi, q_off + h)),
                pl.BlockSpec((None, s, dq), lambda bi, h, qi: (bi, 0, k_off + h)),
                pl.BlockSpec((None, s, dv), lambda bi, h, qi: (bi, 0, v_off + h))]
    args = [q, k, v]
    if bias is not None:
        in_specs.append(pl.BlockSpec((None, tq, 2 * s - tq), lambda bi, h, qi: (h, 0, 0)))
        args.append(bias)
    return pl.pallas_call(
        functools.partial(_attn_kernel, scale=scale, nq=nq, tq=tq, has_bias=bias is not None),
        out_shape=jax.ShapeDtypeStruct((b, s, heads * dv), BF16),
        grid=(b, heads, nq),
        in_specs=in_specs,
        out_specs=pl.BlockSpec((None, tq, dv), lambda bi, h, qi: (bi, qi, h)),
        compiler_params=_params("parallel", "parallel", "arbitrary"),
        name="attention_bias" if bias is not None else "attention",
    )(*args)


def _t5_bucket(rel):
    half = N_BUCKETS // 2
    exact = half // 2
    n = np.abs(rel)
    large = exact + (np.log(np.maximum(n, 1) / exact) / np.log(MAX_DISTANCE / exact) * (half - exact)).astype(np.int64)
    large = np.minimum(large, half - 1)
    return ((rel > 0) * half + np.where(n < exact, n, large)).astype(np.int32)


def dilated_bias_table(rel_bias, s, tq):
    heads = rel_bias.shape[1]
    d = np.arange(-(s - 1), s)
    count = np.zeros(d.shape, np.float32)
    for window, dil in A_PATTERNS:
        count += ((d % dil == 0) & (np.abs(d) <= (window // (2 * dil)) * dil)).astype(np.float32)
    logc = np.where(count > 0, np.log(np.maximum(count, 1.0)), NEG).astype(np.float32)
    onehot = (_t5_bucket(d)[:, None] == np.arange(N_BUCKETS)[None, :]).astype(np.float32)
    line = jnp.transpose(jnp.dot(onehot, rel_bias.astype(F32), precision=lax.Precision.HIGHEST)) + logc[None]
    width = 2 * s
    line = jnp.pad(line, ((0, 0), (0, width - line.shape[1])))[:, None, :]
    return pl.pallas_call(
        functools.partial(_skew_kernel, tq=tq),
        out_shape=jax.ShapeDtypeStruct((heads, tq, 2 * s - tq), F32),
        grid=(heads,),
        in_specs=[pl.BlockSpec((None, 1, width), lambda h: (h, 0, 0))],
        out_specs=pl.BlockSpec((None, tq, 2 * s - tq), lambda h: (h, 0, 0)),
        compiler_params=_params("parallel"),
        name="bias_skew",
    )(line)


def _skew_kernel(line_ref, o_ref, *, tq):
    width = line_ref.shape[1]
    x = jnp.broadcast_to(line_ref[...], (tq, width))
    x = pltpu.roll(x, width - (tq - 1), 1, stride=1, stride_axis=0)
    o_ref[...] = x[:, :o_ref.shape[1]]


GLA_UNROLL = 8


def _gla_kernel(q_ref, k_ref, v_ref, g_ref, z_ref, w2f_ref, w2b_ref, bf_ref, bb_ref, on_ref, o_ref,
                la_ref, acc_ref, qcat_ref, upd_ref, dec_ref, scat_ref, st_ref):
    s_len = q_ref.shape[0]
    c = B_CHUNK
    nchunk = s_len // c
    z = z_ref[...].astype(BF16)
    gate = lambda w2_ref, b_ref: jax.nn.log_sigmoid(
        jnp.dot(z, w2_ref[...], preferred_element_type=F32) + b_ref[...]) * (1.0 / B_GATE_TAU)
    la_ref[0] = gate(w2f_ref, bf_ref)
    la_ref[1] = gate(w2b_ref, bb_ref)

    ri = lax.broadcasted_iota(jnp.int32, (c, c), 0)
    ci = lax.broadcasted_iota(jnp.int32, (c, c), 1)
    keep = (ri >= ci, ri <= ci)
    tri3 = tuple(jnp.concatenate([kp.astype(BF16)] * 3, axis=1) for kp in keep)

    def chunk_rows(n):
        return pl.ds(pl.multiple_of(n * c, c), c)

    def pass1(i, carry):
        for u in range(GLA_UNROLL):
            n = i * GLA_UNROLL + u
            rows = chunk_rows(n)
            qc = q_ref[rows, :] * (B_DK ** -0.5)
            kc = k_ref[rows, :]
            vc = v_ref[rows, :].astype(BF16)
            o = None
            for d in range(2):
                la = la_ref[d, rows, :]
                gcum = jnp.dot(tri3[d], jnp.concatenate(_split3(la), axis=0), preferred_element_type=F32)
                gend = gcum[0:1] if d == 1 else gcum[c - 1:c]
                q_in = (qc * jnp.exp(gcum)).astype(BF16)
                k_in = (kc * jnp.exp(-gcum)).astype(BF16)
                k_out = (kc * jnp.exp(gend - gcum)).astype(BF16)
                att = lax.dot_general(q_in, k_in, NT_DIMS, preferred_element_type=F32)
                att = jnp.where(keep[d], att, 0.0).astype(BF16)
                od = jnp.dot(att, vc, preferred_element_type=F32)
                o = od if o is None else o + od
                qcat_ref[rows, d * B_DK:(d + 1) * B_DK] = q_in
                upd_ref[d, n] = lax.dot_general(vc, k_out, TN_DIMS, preferred_element_type=F32)
                dec_ref[d, n] = jnp.broadcast_to(jnp.exp(gend), (8, B_DK))
            acc_ref[rows, :] = o
        return carry

    lax.fori_loop(0, nchunk // GLA_UNROLL, pass1, 0)

    st_ref[...] = jnp.zeros_like(st_ref)

    def pass2(n, carry):
        for d, m in ((0, n), (1, nchunk - 1 - n)):
            state = st_ref[d]
            scat_ref[m, :, d * B_DK:(d + 1) * B_DK] = state.astype(BF16)
            st_ref[d] = state * dec_ref[d, m][0:1] + upd_ref[d, m]
        return carry

    lax.fori_loop(0, nchunk, pass2, 0)

    def pass3(i, carry):
        for u in range(GLA_UNROLL):
            n = i * GLA_UNROLL + u
            rows = chunk_rows(n)
            o = acc_ref[rows, :] + lax.dot_general(qcat_ref[rows, :], scat_ref[n], NT_DIMS,
                                                   preferred_element_type=F32)
            o = o * lax.rsqrt(jnp.mean(o * o, axis=-1, keepdims=True) + EPS) * on_ref[...]
            g = g_ref[rows, :]
            o_ref[rows, :] = (o * (g * jax.nn.sigmoid(g))).astype(o_ref.dtype)
        return carry

    lax.fori_loop(0, nchunk // GLA_UNROLL, pass3, 0)


def gla_mixer(proj, w2f, w2b, b_f, b_b, onorm, *, q_col, k_col, v_col, g_col, z_col):
    b, s, _ = proj.shape
    hm = lambda blk: (lambda bi, h: (bi, 0, blk + h))
    w2f_p = jnp.zeros((LANE, B_KEYW), F32).at[:B_GATE_RANK].set(w2f).astype(BF16)
    w2b_p = jnp.zeros((LANE, B_KEYW), F32).at[B_GATE_RANK:2 * B_GATE_RANK].set(w2b).astype(BF16)
    return pl.pallas_call(
        _gla_kernel,
        out_shape=jax.ShapeDtypeStruct((b, s, B_WIDTH), BF16),
        grid=(b, B_HEADS),
        in_specs=[pl.BlockSpec((None, s, B_DK), hm(q_col // B_DK)),
                  pl.BlockSpec((None, s, B_DK), hm(k_col // B_DK)),
                  pl.BlockSpec((None, s, B_DV), hm(v_col // B_DV)),
                  pl.BlockSpec((None, s, B_DV), hm(g_col // B_DV)),
                  pl.BlockSpec((None, s, LANE), lambda bi, h: (bi, 0, z_col // LANE)),
                  pl.BlockSpec((LANE, B_DK), lambda bi, h: (0, h)),
                  pl.BlockSpec((LANE, B_DK), lambda bi, h: (0, h)),
                  pl.BlockSpec((1, B_DK), lambda bi, h: (0, h)),
                  pl.BlockSpec((1, B_DK), lambda bi, h: (0, h)),
                  pl.BlockSpec((1, B_DV), lambda bi, h: (0, 0))],
        out_specs=pl.BlockSpec((None, s, B_DV), lambda bi, h: (bi, 0, h)),
        scratch_shapes=[pltpu.VMEM((2, s, B_DK), F32),
                        pltpu.VMEM((s, B_DV), F32),
                        pltpu.VMEM((s, 2 * B_DK), BF16),
                        pltpu.VMEM((2, s // B_CHUNK, B_DV, B_DK), F32),
                        pltpu.VMEM((2, s // B_CHUNK, 8, B_DK), F32),
                        pltpu.VMEM((s // B_CHUNK, B_DV, 2 * B_DK), BF16),
                        pltpu.VMEM((2, B_DV, B_DK), F32)],
        compiler_params=_params("parallel", "arbitrary"),
        name="gla_mixer",
    )(proj, proj, proj, proj, proj, w2f_p, w2b_p, b_f.reshape(1, -1), b_b.reshape(1, -1), onorm.reshape(1, -1))


def _mla_up_kernel(cq_ref, ckv_ref, kr_ref, qn_ref, kvn_ref, wq_ref, wqr_ref, wkv_ref, cos_ref, sin_ref,
                   q_ref, k_ref, v_ref):
    def rms(x, g):
        return (x * lax.rsqrt(jnp.mean(x * x, axis=-1, keepdims=True) + EPS) * g).astype(BF16)

    cq = rms(cq_ref[...], qn_ref[...])
    ckv = rms(ckv_ref[...], kvn_ref[...])
    cos, sin = cos_ref[...], sin_ref[...]
    kr = kr_ref[...]
    k_rope = kr * cos + pltpu.roll(kr, LANE - C_ROPE, 1) * sin
    lane = lax.broadcasted_iota(jnp.int32, k_rope.shape, 1)
    k_rope = jnp.where(lane < C_ROPE, k_rope, 0.0)
    for h in range(C_HEADS):
        q = jnp.dot(cq, wq_ref[:, h * C_QK:(h + 1) * C_QK], preferred_element_type=F32)
        qp = jnp.dot(cq, wqr_ref[:, h * LANE:(h + 1) * LANE], preferred_element_type=F32)
        q_ref[:, h * C_QK:h * C_QK + C_NOPE] = q[:, :C_NOPE]
        q_ref[:, h * C_QK + C_NOPE:(h + 1) * C_QK] = q[:, C_NOPE:] * cos + qp * sin
        kv = jnp.dot(ckv, wkv_ref[:, h * 2 * LANE:(h + 1) * 2 * LANE], preferred_element_type=F32)
        k_ref[:, h * C_QK:h * C_QK + C_NOPE] = kv[:, :C_NOPE]
        k_ref[:, h * C_QK + C_NOPE:(h + 1) * C_QK] = k_rope
        v_ref[:, h * C_V:(h + 1) * C_V] = kv[:, C_NOPE:]


def _rot_half_cols(w):
    half = w.shape[-1] // 2
    return jnp.concatenate([-w[..., half:], w[..., :half]], axis=-1)


def mla_up(proj, q_norm, w_uq, kv_norm, w_ukv, cos, sin, *, col0, tm=512):
    t = proj.shape[0]
    wq = w_uq.reshape(C_Q_RANK, C_HEADS, C_NOPE + C_ROPE)
    wq_main = jnp.pad(wq, ((0, 0), (0, 0), (0, C_QK - C_NOPE - C_ROPE))).reshape(C_Q_RANK, C_HEADS * C_QK)
    wq_rot = jnp.pad(_rot_half_cols(wq[..., C_NOPE:]), ((0, 0), (0, 0), (0, LANE - C_ROPE)))
    wq_rot = wq_rot.reshape(C_Q_RANK, C_HEADS * LANE)
    row = lambda i: (i, 0)
    full = lambda arr: pl.BlockSpec(arr.shape, lambda i: (0, 0))
    g_q, g_kv = q_norm.reshape(1, -1), kv_norm.reshape(1, -1)
    wq_main, wq_rot, wkv = wq_main.astype(BF16), wq_rot.astype(BF16), w_ukv.astype(BF16)
    return pl.pallas_call(
        _mla_up_kernel,
        out_shape=(jax.ShapeDtypeStruct((t, C_HEADS * C_QK), F32),
                   jax.ShapeDtypeStruct((t, C_HEADS * C_QK), F32),
                   jax.ShapeDtypeStruct((t, C_WIDTH), F32)),
        grid=(t // tm,),
        in_specs=[pl.BlockSpec((tm, C_Q_RANK), lambda i: (i, col0 // C_Q_RANK)),
                  pl.BlockSpec((tm, C_KV_RANK), lambda i: (i, (col0 + C_Q_RANK) // C_KV_RANK)),
                  pl.BlockSpec((tm, LANE), lambda i: (i, (col0 + C_Q_RANK + C_KV_RANK) // LANE)),
                  full(g_q), full(g_kv), full(wq_main), full(wq_rot), full(wkv),
                  pl.BlockSpec((tm, LANE), row), pl.BlockSpec((tm, LANE), row)],
        out_specs=(pl.BlockSpec((tm, C_HEADS * C_QK), row),
                   pl.BlockSpec((tm, C_HEADS * C_QK), row),
                   pl.BlockSpec((tm, C_WIDTH), row)),
        compiler_params=_params("parallel"),
        name="mla_up",
    )(proj, proj, proj, g_q, g_kv, wq_main, wq_rot, wkv, cos, sin)


RG = 4
RGW = RG * D_HEAD
RCH = 64
DC_R, DC_K, DC_V = 0, D_WIDTH, 2 * D_WIDTH
DC_ZG = 3 * D_WIDTH
DC_ZW = DC_ZG + D_G_RANK
DC_ZA = DC_ZW + 2 * D_W_RANK
DC_PAD = 7 * 512


def _head_sums(x, bo):
    return jnp.concatenate(
        [jnp.dot(x[:, RGW * g:RGW * (g + 1)], bo, preferred_element_type=F32, precision=lax.Precision.HIGHEST)
         for g in range(x.shape[1] // RGW)], axis=1)


def _block_ones():
    i = np.arange(RGW)
    return jnp.asarray((i[:, None] // D_HEAD) == (i[None, :] // D_HEAD), F32)


def _rwkv_prep_kernel(x_ref, xp_ref, xn_ref, mu_ref, w2f_ref, w2b_ref, a2_ref, g2_ref, w0f_ref, w0b_ref,
                      a0_ref, kk_ref, ka_ref, rk_ref, bo_ref,
                      r_ref, k_ref, v_ref, a_ref, b_ref, lwf_ref, lwb_ref, g_ref, bonus_ref, *, tiles_per_seq):
    i = pl.program_id(0) % tiles_per_seq
    x = x_ref[...]
    tm = x.shape[0]
    row = lax.broadcasted_iota(jnp.int32, x.shape, 0)
    prev_row = jnp.where(i == 0, 0.0, xp_ref[7:8, :])
    next_row = jnp.where(i == tiles_per_seq - 1, 0.0, xn_ref[0:1, :])
    prev = jnp.where(row == 0, prev_row, pltpu.roll(x, 1, 0))
    nxt = jnp.where(row == tm - 1, next_row, pltpu.roll(x, tm - 1, 0))
    x = x + mu_ref[...] * (0.5 * (prev + nxt) - x)
    r, k, v = x[:, DC_R:DC_R + D_WIDTH], x[:, DC_K:DC_K + D_WIDTH], x[:, DC_V:DC_V + D_WIDTH]
    zg = x[:, DC_ZG:DC_ZG + LANE]
    zw = x[:, DC_ZW:DC_ZW + LANE]
    za = x[:, DC_ZA:DC_ZA + LANE]
    tz = jnp.tanh(zw).astype(BF16)
    log_decay = lambda w0_ref, w2_ref: -np.exp(-0.5).astype(np.float32) * jax.nn.sigmoid(
        w0_ref[...] + jnp.dot(tz, w2_ref[...], preferred_element_type=F32))
    lwf_ref[...] = log_decay(w0f_ref, w2f_ref)
    lwb_ref[...] = log_decay(w0b_ref, w2b_ref)
    ag = jax.nn.sigmoid(a0_ref[...] + jnp.dot(za.astype(BF16), a2_ref[...], preferred_element_type=F32))
    g_ref[...] = jnp.dot(jax.nn.sigmoid(zg).astype(BF16), g2_ref[...], preferred_element_type=F32)
    bo = bo_ref[...]
    kk = k * kk_ref[...]
    kk = kk / jnp.maximum(jnp.sqrt(_head_sums(kk * kk, bo)), 1e-12)
    k = k * (1.0 + (ag - 1.0) * ka_ref[...])
    r_ref[...] = r
    k_ref[...] = k
    v_ref[...] = v
    a_ref[...] = -kk
    b_ref[...] = kk * ag
    bonus_ref[...] = _head_sums(r * k * rk_ref[...], bo) * v


def rwkv_prep(proj, mu, w0_f, w2_f, w0_b, w2_b, a0, a2, g2, k_k, k_a, r_k, *, seq, tm=256):
    t = proj.shape[0]
    tiles_per_seq = seq // tm
    hb = tm // 8
    nblk8 = t // 8
    pad_rows = lambda w, lo: jnp.zeros((LANE, D_WIDTH), F32).at[lo:lo + w.shape[0]].set(w).astype(BF16)
    vec = lambda u: u.reshape(1, -1)
    consts = [vec(mu), pad_rows(w2_f, 0), pad_rows(w2_b, D_W_RANK), pad_rows(a2, 0), g2.astype(BF16),
              vec(w0_f), vec(w0_b), vec(a0), vec(k_k), vec(k_a), vec(r_k), _block_ones()]
    full = lambda arr: pl.BlockSpec(arr.shape, lambda i: (0, 0))
    out_spec = pl.BlockSpec((tm, D_WIDTH), lambda i: (i, 0))
    return pl.pallas_call(
        functools.partial(_rwkv_prep_kernel, tiles_per_seq=tiles_per_seq),
        out_shape=tuple(jax.ShapeDtypeStruct((t, D_WIDTH), F32) for _ in range(9)),
        grid=(t // tm,),
        in_specs=[pl.BlockSpec((tm, DC_PAD), lambda i: (i, 0)),
                  pl.BlockSpec((8, DC_PAD), lambda i: (jnp.maximum(i * hb - 1, 0), 0)),
                  pl.BlockSpec((8, DC_PAD), lambda i: (jnp.minimum((i + 1) * hb, nblk8 - 1), 0))]
                 + [full(c) for c in consts],
        out_specs=tuple(out_spec for _ in range(9)),
        compiler_params=_params("parallel"),
        name="rwkv_prep",
    )(proj, proj, proj, *consts)


def _rwkv_chunk_kernel(*refs, ngroups):
    ins, (yf_ref, yb_ref, mt_ref) = refs[:12], refs[12:]

    @pl.when(pl.program_id(1) == 0)
    def _():
        mt_ref[...] = jnp.zeros_like(mt_ref)

    row = lax.broadcasted_iota(jnp.int32, (RCH, RGW), 0)
    col = lax.broadcasted_iota(jnp.int32, (RCH, RGW), 1) & (RCH - 1)
    bdmask = (lax.broadcasted_iota(jnp.int32, (RGW, RGW), 0) // D_HEAD
              == lax.broadcasted_iota(jnp.int32, (RGW, RGW), 1) // D_HEAD)
    tr = lax.broadcasted_iota(jnp.int32, (RCH, RCH), 0)
    tc = lax.broadcasted_iota(jnp.int32, (RCH, RCH), 1)
    zero = jnp.zeros((), F32)

    def bd(z):
        zb = z.astype(BF16)
        return jnp.where(bdmask, jnp.concatenate([zb] * RG, axis=0), jnp.zeros((), BF16))

    def mm(x, y, dims=None):
        x = x.astype(BF16)
        if dims is None:
            return jnp.dot(x, y, preferred_element_type=F32)
        return lax.dot_general(x, y, dims, preferred_element_type=F32)

    chains = [(d, g) for d in range(2) for g in range(ngroups)]
    st = []
    for d, g in chains:
        backward = d == 1
        r_ref, k_ref, v_ref, a_ref, b_ref, lw_ref = ins[6 * d:6 * d + 6]
        tri = ((tc >= tr) if backward else (tc <= tr)).astype(BF16)
        sl = slice(RGW * g, RGW * (g + 1))
        r, k, v, a, b, lw = (ref[:, sl] for ref in (r_ref, k_ref, v_ref, a_ref, b_ref, lw_ref))
        lam = jnp.dot(jnp.concatenate([tri] * 3, axis=1), jnp.concatenate(_split3(lw), axis=0),
                      preferred_element_type=F32)
        lamc = lam[0:1] if backward else lam[RCH - 1:RCH]
        e_inv = jnp.exp(-lam)
        e_out = jnp.exp(lamc - lam)
        ar = jnp.concatenate([a * jnp.exp(lam - lw), r * jnp.exp(lam)], axis=0).astype(BF16)
        bk = jnp.concatenate([b * e_out, k * e_out], axis=0).astype(BF16)
        st.append(dict(ar=ar, bk=bk, v=v, lamc=lamc, sl=sl,
                       gb=mm(ar, bd(b * e_inv), NT_DIMS), gk=mm(ar, bd(k * e_inv), NT_DIMS)))
    for (d, g), c in zip(chains, st):
        strict = (col > row) if d == 1 else (col < row)
        incl = (col >= row) if d == 1 else (col <= row)
        c["lp"] = jnp.where(strict, c["gb"][:RCH], zero)
        lak = jnp.where(strict, c["gk"][:RCH], zero)
        c["grb"] = jnp.where(incl, c["gb"][RCH:], zero).astype(BF16)
        c["grk"] = jnp.where(incl, c["gk"][RCH:], zero).astype(BF16)
        c["mt"] = mt_ref[d, g]
        amrm = mm(c["ar"], c["mt"].astype(BF16), NT_DIMS)
        c["bdv"] = bd(c["v"])
        c["u"] = amrm[:RCH] + mm(lak, c["bdv"])
        c["rm"] = amrm[RCH:]
    for rnd in range(6):
        for c in st:
            lpb = c["lp"].astype(BF16)
            c["u"] = c["u"] + mm(lpb, bd(c["u"]))
            if rnd < 5:
                c["lp"] = mm(lpb, bd(c["lp"]))
    for (d, g), c in zip(chains, st):
        y_ref = yb_ref if d == 1 else yf_ref
        y_ref[:, c["sl"]] = c["rm"] + mm(c["grb"], bd(c["u"])) + mm(c["grk"], c["bdv"])
        uv = jnp.concatenate([c["u"], c["v"]], axis=0).astype(BF16)
        upd = lax.dot_general(uv, c["bk"], TN_DIMS, preferred_element_type=F32)
        mt_ref[d, g] = c["mt"] * jnp.exp(c["lamc"]) + jnp.where(bdmask, upd, zero)


def rwkv_chunked(r, k, v, a, b, lwf, lwb):
    bsz, s, wd = r.shape
    nc = s // RCH
    fspec = pl.BlockSpec((None, RCH, wd), lambda bi, n: (bi, n, 0))
    bspec = pl.BlockSpec((None, RCH, wd), lambda bi, n: (bi, nc - 1 - n, 0))
    return pl.pallas_call(
        functools.partial(_rwkv_chunk_kernel, ngroups=wd // RGW),
        out_shape=(jax.ShapeDtypeStruct((bsz, s, wd), F32), jax.ShapeDtypeStruct((bsz, s, wd), F32)),
        grid=(bsz, nc),
        in_specs=[fspec] * 6 + [bspec] * 6,
        out_specs=(fspec, bspec),
        scratch_shapes=[pltpu.VMEM((2, wd // RGW, RGW, RGW), F32)],
        compiler_params=_params("parallel", "arbitrary"),
        name="rwkv_chunked",
    )(r, k, v, a, b, lwf, r, k, v, a, b, lwb)


def _rwkv_post_kernel(yf_ref, yb_ref, bonus_ref, g_ref, lng_ref, lnb_ref, bo_ref, o_ref):
    bo = bo_ref[...]
    y = yf_ref[...] + yb_ref[...]
    yc = y - _head_sums(y, bo) * (1.0 / D_HEAD)
    var = _head_sums(yc * yc, bo) * (1.0 / D_HEAD)
    y = yc * lax.rsqrt(var + D_LN_EPS) * lng_ref[...] + lnb_ref[...]
    o_ref[...] = ((y + bonus_ref[...]) * g_ref[...]).astype(o_ref.dtype)


def rwkv_post(yf, yb, bonus, g, ln_g, ln_b, *, tm=512):
    t, wd = yf.shape
    row = pl.BlockSpec((tm, wd), lambda i: (i, 0))
    vec = pl.BlockSpec((1, wd), lambda i: (0, 0))
    bo = _block_ones()
    return pl.pallas_call(
        _rwkv_post_kernel,
        out_shape=jax.ShapeDtypeStruct((t, wd), BF16),
        grid=(t // tm,),
        in_specs=[row, row, row, row, vec, vec, pl.BlockSpec(bo.shape, lambda i: (0, 0))],
        out_specs=row,
        compiler_params=_params("parallel"),
        name="rwkv_post",
    )(yf, yb, bonus, g, ln_g.reshape(1, wd), ln_b.reshape(1, wd), bo)


def rwkv7_mixer(proj, bsz, s, mu, w0_f, w2_f, w0_b, w2_b, a0, a2, g2, k_k, k_a, r_k, ln_g, ln_b):
    r, k, v, a, b, lwf, lwb, g, bonus = rwkv_prep(proj, mu, w0_f, w2_f, w0_b, w2_b, a0, a2, g2, k_k, k_a,
                                                  r_k.reshape(-1), seq=s)
    r3 = lambda u: u.reshape(bsz, s, D_WIDTH)
    yf, yb = rwkv_chunked(r3(r), r3(k), r3(v), r3(a), r3(b), r3(lwf), r3(lwb))
    return rwkv_post(yf.reshape(-1, D_WIDTH), yb.reshape(-1, D_WIDTH), bonus, g, ln_g, ln_b)


MOE_TILE = 1024
MOE_SUB = 128
ROW_ALIGN = 16


def _route(logit):
    lane = lax.broadcasted_iota(jnp.int32, logit.shape, 1)
    first_at = lambda mask: jnp.min(jnp.where(mask, lane, jnp.int32(LANE)), axis=-1, keepdims=True)
    is_grp = lane < N_GROUPS
    gl = jnp.where(is_grp, logit, NEG)
    gmax = jnp.max(gl, axis=-1, keepdims=True)
    p_grp = 1.0 / jnp.sum(jnp.where(is_grp, jnp.exp(gl - gmax), 0.0), axis=-1, keepdims=True)
    i_grp = first_at(is_grp & (gl == gmax))
    lo = N_GROUPS + i_grp * EXPERTS_PER_GROUP
    in_grp = (lane >= lo) & (lane < lo + EXPERTS_PER_GROUP)
    el = jnp.where(in_grp, logit, NEG)
    l1 = jnp.max(el, axis=-1, keepdims=True)
    i1 = first_at(in_grp & (el == l1))
    rest = in_grp & (lane != i1)
    el2 = jnp.where(rest, logit, NEG)
    l2 = jnp.max(el2, axis=-1, keepdims=True)
    i2 = first_at(rest & (el2 == l2))
    e2 = jnp.exp(l2 - l1)
    w1 = p_grp / (1.0 + e2)
    w2 = p_grp * e2 / (1.0 + e2)
    return i_grp, jnp.where(lane == i1, w1, jnp.where(lane == i2, w2, 0.0))


def _moe_sort_kernel(x_ref, g_ref, wr_ref, br_ref, hn_ref, comb_ref, pos_ref, off_ref):
    x = x_ref[...]
    tm = x.shape[0]
    hn = (x * lax.rsqrt(jnp.mean(x * x, axis=-1, keepdims=True) + EPS) * g_ref[...]).astype(BF16)
    wr, br = wr_ref[...], br_ref[...]
    i_grp, _ = _route(jnp.dot(hn, wr, preferred_element_type=F32) + br)
    lane = lax.broadcasted_iota(jnp.int32, (tm, LANE), 1)
    onehot = (lane == i_grp).astype(F32)
    ri = lax.broadcasted_iota(jnp.int32, (tm, tm), 0)
    ci = lax.broadcasted_iota(jnp.int32, (tm, tm), 1)
    earlier = jnp.dot((ci < ri).astype(BF16), onehot.astype(BF16), preferred_element_type=F32)
    cnt = jnp.broadcast_to(jnp.sum(onehot, axis=0, keepdims=True), (8, LANE))
    lane8 = lax.broadcasted_iota(jnp.int32, (8, LANE), 1)
    start = jnp.zeros((8, LANE), F32)
    for sh in range(1, N_GROUPS):
        start += jnp.where(lane8 >= sh, pltpu.roll(cnt, sh, 1), 0.0)
    off_ref[...] = start[0:1].astype(jnp.int32)
    pos = jnp.sum(onehot * (start[0:1] + earlier), axis=-1, keepdims=True).astype(jnp.int32)
    pos_ref[...] = pos
    perm_t = (ci == pos).astype(BF16)
    hn_s = lax.dot_general(perm_t, hn, TN_DIMS, preferred_element_type=F32).astype(BF16)
    hn_ref[...] = hn_s
    _, comb = _route(jnp.dot(hn_s, wr, preferred_element_type=F32) + br)
    comb_ref[...] = comb


def moe_sort(x, g, w_grp, b_grp, w_exp, b_exp):
    t, d = x.shape
    tm = MOE_TILE
    nr = N_GROUPS + N_EXPERTS
    wr = jnp.pad(jnp.concatenate([w_grp, w_exp], axis=1), ((0, 0), (0, LANE - nr))).astype(BF16)
    br = jnp.pad(jnp.concatenate([b_grp, b_exp]), (0, LANE - nr)).reshape(1, LANE)
    row = lambda i: (i, 0)
    hn, comb, pos, off = pl.pallas_call(
        _moe_sort_kernel,
        out_shape=(jax.ShapeDtypeStruct((t, d), BF16), jax.ShapeDtypeStruct((t, LANE), F32),
                   jax.ShapeDtypeStruct((t, 1), jnp.int32), jax.ShapeDtypeStruct((t // tm, 1, LANE), jnp.int32)),
        grid=(t // tm,),
        in_specs=[pl.BlockSpec((tm, d), row), pl.BlockSpec((1, d), lambda i: (0, 0)),
                  pl.BlockSpec((d, LANE), lambda i: (0, 0)), pl.BlockSpec((1, LANE), lambda i: (0, 0))],
        out_specs=(pl.BlockSpec((tm, d), row), pl.BlockSpec((tm, LANE), row), pl.BlockSpec((tm, 1), row),
                   pl.BlockSpec((None, 1, LANE), lambda i: (i, 0, 0))),
        compiler_params=_params("parallel"),
        name="moe_sort",
    )(x, g.reshape(1, d), wr, br)
    bounds = jnp.concatenate([off[:, 0, :N_GROUPS], jnp.full((t // tm, 1), tm, jnp.int32)], axis=1)
    return hn, comb, pos, bounds.reshape(-1)


def _moe_group_kernel(bounds_ref, hn_ref, c_ref, wg_ref, wu_ref, wd_ref, y_ref):
    i, g, j = pl.program_id(0), pl.program_id(1), pl.program_id(2)

    @pl.when((g == 0) & (j == 0))
    def _():
        y_ref[...] = jnp.zeros_like(y_ref)

    base = i * (N_GROUPS + 1) + g
    start, end = bounds_ref[base], bounds_ref[base + 1]
    expert_lane = N_GROUPS + g * EXPERTS_PER_GROUP + j
    lane = lax.broadcasted_iota(jnp.int32, (MOE_SUB, LANE), 1)
    row = lax.broadcasted_iota(jnp.int32, (MOE_SUB, 1), 0)
    tm = hn_ref.shape[0]
    first = (start // ROW_ALIGN) * ROW_ALIGN

    def sub(k, carry):
        want = first + k * MOE_SUB
        r0 = pl.multiple_of(jnp.minimum(want, tm - MOE_SUB), ROW_ALIGN)
        rows = pl.ds(r0, MOE_SUB)
        x = hn_ref[rows, :]
        hg = jnp.dot(x, wg_ref[...], preferred_element_type=F32)
        hu = jnp.dot(x, wu_ref[...], preferred_element_type=F32)
        c = jnp.sum(jnp.where(lane == expert_lane, c_ref[rows, :], 0.0), axis=-1, keepdims=True)
        c = jnp.where(row + r0 >= want, c, 0.0)
        hid = (hg * jax.nn.sigmoid(hg)) * hu * c
        y_ref[rows, :] += jnp.dot(hid.astype(BF16), wd_ref[...], preferred_element_type=F32)
        return carry

    lax.fori_loop(0, (end - first + MOE_SUB - 1) // MOE_SUB, sub, 0)


def moe_group_experts(hn, comb, bounds, w_gate, w_up, w_down):
    t, d = hn.shape
    tm = MOE_TILE
    ne, _, ff = w_gate.shape
    ex = lambda i, g, j, b: (g * EXPERTS_PER_GROUP + j, 0, 0)
    return pl.pallas_call(
        _moe_group_kernel,
        out_shape=jax.ShapeDtypeStruct((t, d), F32),
        grid_spec=pltpu.PrefetchScalarGridSpec(
            num_scalar_prefetch=1,
            grid=(t // tm, N_GROUPS, EXPERTS_PER_GROUP),
            in_specs=[pl.BlockSpec((tm, d), lambda i, g, j, b: (i, 0)),
                      pl.BlockSpec((tm, LANE), lambda i, g, j, b: (i, 0)),
                      pl.BlockSpec((None, d, ff), ex), pl.BlockSpec((None, d, ff), ex),
                      pl.BlockSpec((None, ff, d), ex)],
            out_specs=pl.BlockSpec((tm, d), lambda i, g, j, b: (i, 0))),
        compiler_params=_params("parallel", "arbitrary", "arbitrary"),
        name="moe_group_experts",
    )(bounds, hn, comb, w_gate, w_up, w_down)


def _moe_unsort_kernel(y_ref, pos_ref, x_ref, o_ref):
    tm = y_ref.shape[0]
    perm_t = (lax.broadcasted_iota(jnp.int32, (tm, tm), 1) == pos_ref[...]).astype(BF16)
    y = y_ref[...]
    hi = y.astype(BF16)
    lo = (y - hi.astype(F32)).astype(BF16)
    o_ref[...] = (x_ref[...] + jnp.dot(perm_t, hi, preferred_element_type=F32)
                  + jnp.dot(perm_t, lo, preferred_element_type=F32))


def moe_unsort(y, pos, x):
    t, d = x.shape
    tm, tn = MOE_TILE, d // 2
    blk = pl.BlockSpec((tm, tn), lambda i, j: (i, j))
    return pl.pallas_call(
        _moe_unsort_kernel,
        out_shape=jax.ShapeDtypeStruct((t, d), F32),
        grid=(t // tm, d // tn),
        in_specs=[blk, pl.BlockSpec((tm, 1), lambda i, j: (i, 0)), blk],
        out_specs=blk,
        compiler_params=_params("parallel", "arbitrary"),
        name="moe_unsort",
    )(y, pos, x)


def hier_moe(x, g, w_grp, b_grp, w_exp, b_exp, w_gate, w_up, w_down):
    hn, comb, pos, bounds = moe_sort(x, g, w_grp, b_grp, w_exp, b_exp)
    y = moe_group_experts(hn, comb, bounds, w_gate.astype(BF16), w_up.astype(BF16), w_down.astype(BF16))
    return moe_unsort(y, pos, x)


def even_layer(x, bsz, s, norm_g, rel_bias, w_in, w_out, w2_f, b_f, w2_b, b_b, onorm):
    n_pad = _round_up(EVEN_IN, LANE)
    w_in_p = jnp.pad(w_in, ((0, 0), (0, n_pad - EVEN_IN))).astype(BF16)
    proj = norm_linear(x, norm_g, w_in_p, tn_target=896).reshape(bsz, s, n_pad)
    tq = 256
    ya = attention(proj, proj, proj, heads=A_HEADS, dq=HEAD_DIM, dv=HEAD_DIM,
                   q_off=0, k_off=A_HEADS, v_off=2 * A_HEADS, scale=HEAD_DIM ** -0.5,
                   bias=dilated_bias_table(rel_bias, s, tq), tq=tq)
    q_col = 3 * A_WIDTH
    yb = gla_mixer(proj, w2_f, w2_b, b_f, b_b, onorm, q_col=q_col, k_col=q_col + B_KEYW,
                   v_col=q_col + 2 * B_KEYW, g_col=q_col + 2 * B_KEYW + B_WIDTH,
                   z_col=q_col + 2 * B_KEYW + 2 * B_WIDTH)
    t = bsz * s
    return out_proj(ya.reshape(t, A_WIDTH), yb.reshape(t, B_WIDTH), w_out.astype(BF16), x)


def _odd_columns(w_in, mu):
    c0 = C_IN
    cut = lambda u, lo, n: u[..., lo:lo + n]
    zpad = lambda u, n: jnp.pad(u, [(0, 0)] * (u.ndim - 1) + [(0, n)])
    off = np.cumsum((0,) + D_SPLITS)
    def rwkv_cols(u):
        parts = [cut(u, off[0], 3 * D_WIDTH), cut(u, off[6], D_G_RANK), cut(u, off[3], 2 * D_W_RANK),
                 cut(u, off[5], D_A_RANK)]
        u = jnp.concatenate(parts, axis=-1)
        return zpad(u, DC_PAD - u.shape[-1])
    w_kr = w_in[:, C_Q_RANK + C_KV_RANK:C_IN]
    w_all = jnp.concatenate([rwkv_cols(w_in[:, c0:]), w_in[:, :C_IN], _rot_half_cols(w_kr)], axis=1)
    n_pad = _round_up(w_all.shape[1], 9 * LANE)
    return zpad(w_all, n_pad - w_all.shape[1]).astype(BF16), rwkv_cols(mu)


def odd_layer(x, bsz, s, norm_g, w_in, w_out, q_norm, w_uq, kv_norm, w_ukv, mu, w0_f, w2_f, w0_b, w2_b,
              a0, a2, g2, k_k, k_a, r_k, ln_g, ln_b):
    t = bsz * s
    w_all, mu_cols = _odd_columns(w_in, mu)
    proj = norm_linear(x, norm_g, w_all, tn_target=1152)
    inv = 1.0 / (ROPE_THETA ** (jnp.arange(0, C_ROPE, 2, dtype=F32) / C_ROPE))
    ang = jnp.arange(s, dtype=F32)[:, None] * inv[None, :]
    cos = jnp.pad(jnp.concatenate([jnp.cos(ang)] * 2, axis=1), ((0, 0), (0, LANE - C_ROPE)), constant_values=1.0)
    sin = jnp.pad(jnp.concatenate([jnp.sin(ang)] * 2, axis=1), ((0, 0), (0, LANE - C_ROPE)))
    q, k, v = mla_up(proj, q_norm, w_uq, kv_norm, w_ukv, jnp.tile(cos, (bsz, 1)), jnp.tile(sin, (bsz, 1)),
                     col0=DC_PAD)
    r3 = lambda u: u.reshape(bsz, s, -1)
    yc = attention(r3(q), r3(k), r3(v), heads=C_HEADS, dq=C_QK, dv=C_V, q_off=0, k_off=0, v_off=0,
                   scale=(C_NOPE + C_ROPE) ** -0.5)
    yd = rwkv7_mixer(proj, bsz, s, mu_cols, w0_f, w2_f, w0_b, w2_b, a0, a2, g2, k_k, k_a, r_k, ln_g, ln_b)
    return out_proj(yc.reshape(t, C_WIDTH), yd, w_out.astype(BF16), x)


def kernel(x_prompt, x_sample, rel_bias, norm_mix, norm_ffn, norm_final, ev_w_in, ev_w_out, ev_gla_w2_f, ev_gla_b_f, ev_gla_w2_b, ev_gla_b_b, ev_gla_onorm, od_w_in, od_w_out, od_q_norm, od_w_uq, od_kv_norm, od_w_ukv, od_mu, od_w0_f, od_w2_f, od_w0_b, od_w2_b, od_a0, od_a2, od_g2, od_k_k, od_k_a, od_r_k, od_ln_g, od_ln_b, moe_w_grp, moe_b_grp, moe_w_exp, moe_b_exp, moe_w_gate, moe_w_up, moe_w_down):
    nb_p = x_prompt.shape[0]
    x = jnp.concatenate([x_prompt, x_sample], axis=0)
    bsz, s, d = x.shape
    x = x.reshape(bsz * s, d)
    for i in range(DEPTH):
        j = i // 2
        if i % 2 == 0:
            x = even_layer(x, bsz, s, norm_mix[i], rel_bias, ev_w_in[j], ev_w_out[j], ev_gla_w2_f[j],
                           ev_gla_b_f[j], ev_gla_w2_b[j], ev_gla_b_b[j], ev_gla_onorm[j])
        else:
            x = odd_layer(x, bsz, s, norm_mix[i], od_w_in[j], od_w_out[j], od_q_norm[j], od_w_uq[j],
                          od_kv_norm[j], od_w_ukv[j], od_mu[j], od_w0_f[j], od_w2_f[j], od_w0_b[j],
                          od_w2_b[j], od_a0[j], od_a2[j], od_g2[j], od_k_k[j], od_k_a[j], od_r_k[j],
                          od_ln_g[j], od_ln_b[j])
        x = hier_moe(x, norm_ffn[i], moe_w_grp[i], moe_b_grp[i], moe_w_exp[i], moe_b_exp[i],
                     moe_w_gate[i], moe_w_up[i], moe_w_down[i])
    y_p = final_norm(x, norm_final, row0=0, rows=nb_p * s)
    y_s = final_norm(x, norm_final, row0=nb_p * s, rows=(bsz - nb_p) * s)
    return (y_p.reshape(nb_p, s, d), y_s.reshape(bsz - nb_p, s, d))
```

```python
import functools

import jax, jax.numpy as jnp
from jax import lax
import numpy as np
from jax.experimental import pallas as pl
from jax.experimental.pallas import tpu as pltpu

F32, BF16 = jnp.float32, jnp.bfloat16

D_MODEL = 2048
DEPTH = 2
MIX_HALF = D_MODEL // 2
HEAD_DIM = 128
EPS = 1e-6
NEG = -1e30

A_HEADS = MIX_HALF // HEAD_DIM
A_WIDTH = A_HEADS * HEAD_DIM
A_PATTERNS = ((128, 1), (512, 4), (2048, 16))
N_BUCKETS = 32
MAX_DISTANCE = 1024

B_HEADS = 4
B_DV = MIX_HALF // B_HEADS
B_DK = B_DV // 2
B_WIDTH = B_HEADS * B_DV
B_KEYW = B_HEADS * B_DK
B_GATE_RANK = 16
B_GATE_TAU = 16.0
B_CHUNK = 64

C_HEADS = MIX_HALF // 128
C_Q_RANK = 512
C_KV_RANK = 256
C_NOPE = 128
C_ROPE = 64
C_V = 128
C_WIDTH = C_HEADS * C_V
C_QK = 256
ROPE_THETA = 10000.0

D_HEAD = 64
D_HEADS = MIX_HALF // D_HEAD
D_WIDTH = D_HEADS * D_HEAD
D_W_RANK = 64
D_A_RANK = 64
D_G_RANK = 128
D_LN_EPS = 64e-5
D_SPLITS = (D_WIDTH, D_WIDTH, D_WIDTH, D_W_RANK, D_W_RANK, D_A_RANK, D_G_RANK)
D_SHIFT = 3 * D_WIDTH + 2 * D_W_RANK + D_A_RANK + D_G_RANK

N_GROUPS = 4
EXPERTS_PER_GROUP = 4
N_EXPERTS = N_GROUPS * EXPERTS_PER_GROUP
EXPERT_FF = 512

EVEN_IN = 3 * A_WIDTH + 2 * B_KEYW + 2 * B_WIDTH + 2 * B_GATE_RANK
C_IN = C_Q_RANK + C_KV_RANK + C_ROPE
ODD_IN = C_IN + D_SHIFT

LANE = 128
VMEM_LIMIT = 52 * 1024 * 1024


def _params(*sem):
    return pltpu.CompilerParams(dimension_semantics=sem, vmem_limit_bytes=VMEM_LIMIT)


def _round_up(n, m):
    return -(-n // m) * m


NT_DIMS = (((1,), (1,)), ((), ()))
TN_DIMS = (((0,), (0,)), ((), ()))


def _split3(x):
    hi = x.astype(BF16)
    r1 = x - hi.astype(F32)
    mid = r1.astype(BF16)
    lo = (r1 - mid.astype(F32)).astype(BF16)
    return hi, mid, lo


def _pick_tile(n, target):
    best = LANE
    for t in range(LANE, target + 1, LANE):
        if n % t == 0:
            best = t
    return best


def _norm_linear_kernel(x_ref, g_ref, w_ref, o_ref, xn_ref):
    @pl.when(pl.program_id(1) == 0)
    def _():
        x = x_ref[...]
        y = x * lax.rsqrt(jnp.mean(x * x, axis=-1, keepdims=True) + EPS) * g_ref[...]
        xn_ref[...] = y.astype(BF16)

    o_ref[...] = jnp.dot(xn_ref[...], w_ref[...], preferred_element_type=F32)


def norm_linear(x, g, w, *, tm=1024, tn_target=1024):
    t, k = x.shape
    n = w.shape[1]
    tn = _pick_tile(n, tn_target)
    return pl.pallas_call(
        _norm_linear_kernel,
        out_shape=jax.ShapeDtypeStruct((t, n), F32),
        grid=(t // tm, n // tn),
        in_specs=[pl.BlockSpec((tm, k), lambda i, j: (i, 0)),
                  pl.BlockSpec((1, k), lambda i, j: (0, 0)),
                  pl.BlockSpec((k, tn), lambda i, j: (0, j))],
        out_specs=pl.BlockSpec((tm, tn), lambda i, j: (i, j)),
        scratch_shapes=[pltpu.VMEM((tm, k), BF16)],
        compiler_params=_params("parallel", "arbitrary"),
        name="norm_linear",
    )(x, g.reshape(1, k), w)


def _out_proj_kernel(a_ref, b_ref, wa_ref, wb_ref, x_ref, o_ref):
    acc = jnp.dot(a_ref[...], wa_ref[...], preferred_element_type=F32)
    acc += jnp.dot(b_ref[...], wb_ref[...], preferred_element_type=F32)
    o_ref[...] = x_ref[...] + acc


def out_proj(a, b, w, x, *, tm=1024, tn=512):
    t, ka = a.shape
    kb = b.shape[1]
    n = w.shape[1]
    return pl.pallas_call(
        _out_proj_kernel,
        out_shape=jax.ShapeDtypeStruct((t, n), F32),
        grid=(t // tm, n // tn),
        in_specs=[pl.BlockSpec((tm, ka), lambda i, j: (i, 0)),
                  pl.BlockSpec((tm, kb), lambda i, j: (i, 0)),
                  pl.BlockSpec((ka, tn), lambda i, j: (0, j)),
                  pl.BlockSpec((kb, tn), lambda i, j: (0, j)),
                  pl.BlockSpec((tm, tn), lambda i, j: (i, j))],
        out_specs=pl.BlockSpec((tm, tn), lambda i, j: (i, j)),
        compiler_params=_params("parallel", "arbitrary"),
        name="out_proj",
    )(a, b, w[:ka], w[ka:], x)


def _final_norm_kernel(x_ref, g_ref, o_ref):
    x = x_ref[...]
    o_ref[...] = x * lax.rsqrt(jnp.mean(x * x, axis=-1, keepdims=True) + EPS) * g_ref[...]


def final_norm(x, g, *, row0, rows, tm=1024):
    d = x.shape[1]
    return pl.pallas_call(
        _final_norm_kernel,
        out_shape=jax.ShapeDtypeStruct((rows, d), F32),
        grid=(rows // tm,),
        in_specs=[pl.BlockSpec((tm, d), lambda i: (i + row0 // tm, 0)), pl.BlockSpec((1, d), lambda i: (0, 0))],
        out_specs=pl.BlockSpec((tm, d), lambda i: (i, 0)),
        compiler_params=_params("parallel"),
        name="final_norm",
    )(x, g.reshape(1, d))


ATTN_KB = 256


def _attn_kernel(*refs, scale, has_bias):
    if has_bias:
        q_ref, k_ref, v_ref, bias_ref, o_ref = refs
    else:
        q_ref, k_ref, v_ref, o_ref = refs
    s_len = k_ref.shape[0]
    q = (q_ref[...] * scale).astype(BF16)
    blocks = [slice(j * ATTN_KB, (j + 1) * ATTN_KB) for j in range(s_len // ATTN_KB)]
    scores = []
    m = None
    for blk in blocks:
        sj = lax.dot_general(q, k_ref[blk, :].astype(BF16), NT_DIMS, preferred_element_type=F32)
        if has_bias:
            sj = sj + bias_ref[:, blk]
        mj = jnp.max(sj, axis=-1, keepdims=True)
        m = mj if m is None else jnp.maximum(m, mj)
        scores.append(sj)
    o = den = None
    for blk, sj in zip(blocks, scores):
        p = jnp.exp(sj - m)
        dj = jnp.sum(p, axis=-1, keepdims=True)
        oj = jnp.dot(p.astype(BF16), v_ref[blk, :].astype(BF16), preferred_element_type=F32)
        o, den = (oj, dj) if o is None else (o + oj, den + dj)
    o_ref[...] = (o / den).astype(o_ref.dtype)


def attention(q, k, v, *, heads, dq, dv, q_off, k_off, v_off, scale, bias=None, tq=256):
    b, s, _ = q.shape
    nq = s // tq
    in_specs = [pl.BlockSpec((None, tq, dq), lambda bi, h, qi: (bi, qi, q_off + h)),
                pl.BlockSpec((None, s, dq), lambda bi, h, qi: (bi, 0, k_off + h)),
                pl.BlockSpec((None, s, dv), lambda bi, h, qi: (bi, 0, v_off + h))]
    args = [q, k, v]
    if bias is not None:
        in_specs.append(pl.BlockSpec((None, None, tq, s), lambda bi, h, qi: (h, qi, 0, 0)))
        args.append(bias)
    return pl.pallas_call(
        functools.partial(_attn_kernel, scale=scale, has_bias=bias is not None),
        out_shape=jax.ShapeDtypeStruct((b, s, heads * dv), BF16),
        grid=(b, heads, nq),
        in_specs=in_specs,
        out_specs=pl.BlockSpec((None, tq, dv), lambda bi, h, qi: (bi, qi, h)),
        compiler_params=_params("parallel", "parallel", "arbitrary"),
        name="attention_bias" if bias is not None else "attention",
    )(*args)


def _t5_bucket(rel):
    half = N_BUCKETS // 2
    exact = half // 2
    n = np.abs(rel)
    large = exact + (np.log(np.maximum(n, 1) / exact) / np.log(MAX_DISTANCE / exact) * (half - exact)).astype(np.int64)
    large = np.minimum(large, half - 1)
    return ((rel > 0) * half + np.where(n < exact, n, large)).astype(np.int32)


def dilated_bias_table(rel_bias, s, tq):
    heads = rel_bias.shape[1]
    d = np.arange(-(s - 1), s)
    count = np.zeros(d.shape, np.float32)
    for window, dil in A_PATTERNS:
        count += ((d % dil == 0) & (np.abs(d) <= (window // (2 * dil)) * dil)).astype(np.float32)
    logc = np.where(count > 0, np.log(np.maximum(count, 1.0)), NEG).astype(np.float32)
    onehot = (_t5_bucket(d)[:, None] == np.arange(N_BUCKETS)[None, :]).astype(np.float32)
    line = jnp.transpose(jnp.dot(onehot, rel_bias.astype(F32), precision=lax.Precision.HIGHEST)) + logc[None]
    width = 2 * s
    line = jnp.pad(line, ((0, 0), (0, width - line.shape[1])))[:, None, :]
    nq = s // tq
    return pl.pallas_call(
        functools.partial(_skew_kernel, tq=tq, nq=nq),
        out_shape=jax.ShapeDtypeStruct((heads, nq, tq, s), F32),
        grid=(heads, nq),
        in_specs=[pl.BlockSpec((None, 1, width), lambda h, qi: (h, 0, 0))],
        out_specs=pl.BlockSpec((None, None, tq, s), lambda h, qi: (h, qi, 0, 0)),
        compiler_params=_params("parallel", "arbitrary"),
        name="bias_skew",
    )(line)


def _skew_kernel(line_ref, o_ref, *, tq, nq):
    width = line_ref.shape[1]
    first = (nq - 1 - pl.program_id(1)) * tq
    x = jnp.broadcast_to(line_ref[...], (tq, width))
    x = pltpu.roll(x, width - (tq - 1) - first, 1, stride=1, stride_axis=0)
    o_ref[...] = x[:, :o_ref.shape[1]]


GLA_UNROLL = 8


def _gla_kernel(q_ref, k_ref, v_ref, g_ref, z_ref, w2f_ref, w2b_ref, bf_ref, bb_ref, on_ref, o_ref,
                la_ref, acc_ref, qcat_ref, upd_ref, dec_ref, scat_ref, st_ref):
    s_len = q_ref.shape[0]
    c = B_CHUNK
    nchunk = s_len // c
    z = z_ref[...].astype(BF16)
    gate = lambda w2_ref, b_ref: jax.nn.log_sigmoid(
        jnp.dot(z, w2_ref[...], preferred_element_type=F32) + b_ref[...]) * (1.0 / B_GATE_TAU)
    la_ref[0] = gate(w2f_ref, bf_ref)
    la_ref[1] = gate(w2b_ref, bb_ref)

    ri = lax.broadcasted_iota(jnp.int32, (c, c), 0)
    ci = lax.broadcasted_iota(jnp.int32, (c, c), 1)
    keep = (ri >= ci, ri <= ci)
    tri3 = tuple(jnp.concatenate([kp.astype(BF16)] * 3, axis=1) for kp in keep)

    def chunk_rows(n):
        return pl.ds(pl.multiple_of(n * c, c), c)

    def pass1(i, carry):
        for u in range(GLA_UNROLL):
            n = i * GLA_UNROLL + u
            rows = chunk_rows(n)
            qc = q_ref[rows, :] * (B_DK ** -0.5)
            kc = k_ref[rows, :]
            vc = v_ref[rows, :].astype(BF16)
            o = None
            for d in range(2):
                la = la_ref[d, rows, :]
                gcum = jnp.dot(tri3[d], jnp.concatenate(_split3(la), axis=0), preferred_element_type=F32)
                gend = gcum[0:1] if d == 1 else gcum[c - 1:c]
                q_in = (qc * jnp.exp(gcum)).astype(BF16)
                k_in = (kc * jnp.exp(-gcum)).astype(BF16)
                k_out = (kc * jnp.exp(gend - gcum)).astype(BF16)
                att = lax.dot_general(q_in, k_in, NT_DIMS, preferred_element_type=F32)
                att = jnp.where(keep[d], att, 0.0).astype(BF16)
                od = jnp.dot(att, vc, preferred_element_type=F32)
                o = od if o is None else o + od
                qcat_ref[rows, d * B_DK:(d + 1) * B_DK] = q_in
                upd_ref[d, n] = lax.dot_general(vc, k_out, TN_DIMS, preferred_element_type=F32)
                dec_ref[d, n] = jnp.broadcast_to(jnp.exp(gend), (8, B_DK))
            acc_ref[rows, :] = o
        return carry

    lax.fori_loop(0, nchunk // GLA_UNROLL, pass1, 0)

    st_ref[...] = jnp.zeros_like(st_ref)

    def pass2(n, carry):
        for d, m in ((0, n), (1, nchunk - 1 - n)):
            state = st_ref[d]
            scat_ref[m, :, d * B_DK:(d + 1) * B_DK] = state.astype(BF16)
            st_ref[d] = state * dec_ref[d, m][0:1] + upd_ref[d, m]
        return carry

    lax.fori_loop(0, nchunk, pass2, 0)

    def pass3(i, carry):
        for u in range(GLA_UNROLL):
            n = i * GLA_UNROLL + u
            rows = chunk_rows(n)
            o = acc_ref[rows, :] + lax.dot_general(qcat_ref[rows, :], scat_ref[n], NT_DIMS,
                                                   preferred_element_type=F32)
            o = o * lax.rsqrt(jnp.mean(o * o, axis=-1, keepdims=True) + EPS) * on_ref[...]
            g = g_ref[rows, :]
            o_ref[rows, :] = (o * (g * jax.nn.sigmoid(g))).astype(o_ref.dtype)
        return carry

    lax.fori_loop(0, nchunk // GLA_UNROLL, pass3, 0)


def gla_mixer(proj, w2f, w2b, b_f, b_b, onorm, *, q_col, k_col, v_col, g_col, z_col):
    b, s, _ = proj.shape
    hm = lambda blk: (lambda bi, h: (bi, 0, blk + h))
    w2f_p = jnp.zeros((LANE, B_KEYW), F32).at[:B_GATE_RANK].set(w2f).astype(BF16)
    w2b_p = jnp.zeros((LANE, B_KEYW), F32).at[B_GATE_RANK:2 * B_GATE_RANK].set(w2b).astype(BF16)
    return pl.pallas_call(
        _gla_kernel,
        out_shape=jax.ShapeDtypeStruct((b, s, B_WIDTH), BF16),
        grid=(b, B_HEADS),
        in_specs=[pl.BlockSpec((None, s, B_DK), hm(q_col // B_DK)),
                  pl.BlockSpec((None, s, B_DK), hm(k_col // B_DK)),
                  pl.BlockSpec((None, s, B_DV), hm(v_col // B_DV)),
                  pl.BlockSpec((None, s, B_DV), hm(g_col // B_DV)),
                  pl.BlockSpec((None, s, LANE), lambda bi, h: (bi, 0, z_col // LANE)),
                  pl.BlockSpec((LANE, B_DK), lambda bi, h: (0, h)),
                  pl.BlockSpec((LANE, B_DK), lambda bi, h: (0, h)),
                  pl.BlockSpec((1, B_DK), lambda bi, h: (0, h)),
                  pl.BlockSpec((1, B_DK), lambda bi, h: (0, h)),
                  pl.BlockSpec((1, B_DV), lambda bi, h: (0, 0))],
        out_specs=pl.BlockSpec((None, s, B_DV), lambda bi, h: (bi, 0, h)),
        scratch_shapes=[pltpu.VMEM((2, s, B_DK), F32),
                        pltpu.VMEM((s, B_DV), F32),
                        pltpu.VMEM((s, 2 * B_DK), BF16),
                        pltpu.VMEM((2, s // B_CHUNK, B_DV, B_DK), F32),
                        pltpu.VMEM((2, s // B_CHUNK, 8, B_DK), F32),
                        pltpu.VMEM((s // B_CHUNK, B_DV, 2 * B_DK), BF16),
                        pltpu.VMEM((2, B_DV, B_DK), F32)],
        compiler_params=_params("parallel", "arbitrary"),
        name="gla_mixer",
    )(proj, proj, proj, proj, proj, w2f_p, w2b_p, b_f.reshape(1, -1), b_b.reshape(1, -1), onorm.reshape(1, -1))


def _mla_up_kernel(cq_ref, ckv_ref, kr_ref, qn_ref, kvn_ref, wq_ref, wqr_ref, wkv_ref, cos_ref, sin_ref,
                   q_ref, k_ref, v_ref):
    def rms(x, g):
        return (x * lax.rsqrt(jnp.mean(x * x, axis=-1, keepdims=True) + EPS) * g).astype(BF16)

    cq = rms(cq_ref[...], qn_ref[...])
    ckv = rms(ckv_ref[...], kvn_ref[...])
    cos, sin = cos_ref[...], sin_ref[...]
    kr = kr_ref[...]
    k_rope = kr * cos + pltpu.roll(kr, LANE - C_ROPE, 1) * sin
    lane = lax.broadcasted_iota(jnp.int32, k_rope.shape, 1)
    k_rope = jnp.where(lane < C_ROPE, k_rope, 0.0)
    for h in range(C_HEADS):
        q = jnp.dot(cq, wq_ref[:, h * C_QK:(h + 1) * C_QK], preferred_element_type=F32)
        qp = jnp.dot(cq, wqr_ref[:, h * LANE:(h + 1) * LANE], preferred_element_type=F32)
        q_ref[:, h * C_QK:h * C_QK + C_NOPE] = q[:, :C_NOPE]
        q_ref[:, h * C_QK + C_NOPE:(h + 1) * C_QK] = q[:, C_NOPE:] * cos + qp * sin
        kv = jnp.dot(ckv, wkv_ref[:, h * 2 * LANE:(h + 1) * 2 * LANE], preferred_element_type=F32)
        k_ref[:, h * C_QK:h * C_QK + C_NOPE] = kv[:, :C_NOPE]
        k_ref[:, h * C_QK + C_NOPE:(h + 1) * C_QK] = k_rope
        v_ref[:, h * C_V:(h + 1) * C_V] = kv[:, C_NOPE:]


def _rot_half_cols(w):
    half = w.shape[-1] // 2
    return jnp.concatenate([-w[..., half:], w[..., :half]], axis=-1)


def mla_up(proj, q_norm, w_uq, kv_norm, w_ukv, cos, sin, *, col0, tm=512):
    t = proj.shape[0]
    wq = w_uq.reshape(C_Q_RANK, C_HEADS, C_NOPE + C_ROPE)
    wq_main = jnp.pad(wq, ((0, 0), (0, 0), (0, C_QK - C_NOPE - C_ROPE))).reshape(C_Q_RANK, C_HEADS * C_QK)
    wq_rot = jnp.pad(_rot_half_cols(wq[..., C_NOPE:]), ((0, 0), (0, 0), (0, LANE - C_ROPE)))
    wq_rot = wq_rot.reshape(C_Q_RANK, C_HEADS * LANE)
    row = lambda i: (i, 0)
    full = lambda arr: pl.BlockSpec(arr.shape, lambda i: (0, 0))
    g_q, g_kv = q_norm.reshape(1, -1), kv_norm.reshape(1, -1)
    wq_main, wq_rot, wkv = wq_main.astype(BF16), wq_rot.astype(BF16), w_ukv.astype(BF16)
    return pl.pallas_call(
        _mla_up_kernel,
        out_shape=(jax.ShapeDtypeStruct((t, C_HEADS * C_QK), F32),
                   jax.ShapeDtypeStruct((t, C_HEADS * C_QK), F32),
                   jax.ShapeDtypeStruct((t, C_WIDTH), F32)),
        grid=(t // tm,),
        in_specs=[pl.BlockSpec((tm, C_Q_RANK), lambda i: (i, col0 // C_Q_RANK)),
                  pl.BlockSpec((tm, C_KV_RANK), lambda i: (i, (col0 + C_Q_RANK) // C_KV_RANK)),
                  pl.BlockSpec((tm, LANE), lambda i: (i, (col0 + C_Q_RANK + C_KV_RANK) // LANE)),
                  full(g_q), full(g_kv), full(wq_main), full(wq_rot), full(wkv),
                  pl.BlockSpec((tm, LANE), row), pl.BlockSpec((tm, LANE), row)],
        out_specs=(pl.BlockSpec((tm, C_HEADS * C_QK), row),
                   pl.BlockSpec((tm, C_HEADS * C_QK), row),
                   pl.BlockSpec((tm, C_WIDTH), row)),
        compiler_params=_params("parallel"),
        name="mla_up",
    )(proj, proj, proj, g_q, g_kv, wq_main, wq_rot, wkv, cos, sin)


RG = 4
RGW = RG * D_HEAD
RCH = 64
DC_R, DC_K, DC_V = 0, D_WIDTH, 2 * D_WIDTH
DC_ZG = 3 * D_WIDTH
DC_ZW = DC_ZG + D_G_RANK
DC_ZA = DC_ZW + 2 * D_W_RANK
DC_PAD = 7 * 512


def _head_sums(x, bo):
    return jnp.concatenate(
        [jnp.dot(x[:, RGW * g:RGW * (g + 1)], bo, preferred_element_type=F32, precision=lax.Precision.HIGHEST)
         for g in range(x.shape[1] // RGW)], axis=1)


def _block_ones():
    i = np.arange(RGW)
    return jnp.asarray((i[:, None] // D_HEAD) == (i[None, :] // D_HEAD), F32)


def _rwkv_prep_kernel(x_ref, xp_ref, xn_ref, mu_ref, w2f_ref, w2b_ref, a2_ref, g2_ref, w0f_ref, w0b_ref,
                      a0_ref, kk_ref, ka_ref, rk_ref, bo_ref,
                      r_ref, k_ref, v_ref, a_ref, b_ref, lwf_ref, lwb_ref, g_ref, bonus_ref, *, tiles_per_seq):
    i = pl.program_id(0) % tiles_per_seq
    x = x_ref[...]
    tm = x.shape[0]
    row = lax.broadcasted_iota(jnp.int32, x.shape, 0)
    prev_row = jnp.where(i == 0, 0.0, xp_ref[7:8, :])
    next_row = jnp.where(i == tiles_per_seq - 1, 0.0, xn_ref[0:1, :])
    prev = jnp.where(row == 0, prev_row, pltpu.roll(x, 1, 0))
    nxt = jnp.where(row == tm - 1, next_row, pltpu.roll(x, tm - 1, 0))
    x = x + mu_ref[...] * (0.5 * (prev + nxt) - x)
    r, k, v = x[:, DC_R:DC_R + D_WIDTH], x[:, DC_K:DC_K + D_WIDTH], x[:, DC_V:DC_V + D_WIDTH]
    zg = x[:, DC_ZG:DC_ZG + LANE]
    zw = x[:, DC_ZW:DC_ZW + LANE]
    za = x[:, DC_ZA:DC_ZA + LANE]
    tz = jnp.tanh(zw).astype(BF16)
    log_decay = lambda w0_ref, w2_ref: -np.exp(-0.5).astype(np.float32) * jax.nn.sigmoid(
        w0_ref[...] + jnp.dot(tz, w2_ref[...], preferred_element_type=F32))
    lwf_ref[...] = log_decay(w0f_ref, w2f_ref)
    lwb_ref[...] = log_decay(w0b_ref, w2b_ref)
    ag = jax.nn.sigmoid(a0_ref[...] + jnp.dot(za.astype(BF16), a2_ref[...], preferred_element_type=F32))
    g_ref[...] = jnp.dot(jax.nn.sigmoid(zg).astype(BF16), g2_ref[...], preferred_element_type=F32)
    bo = bo_ref[...]
    kk = k * kk_ref[...]
    kk = kk / jnp.maximum(jnp.sqrt(_head_sums(kk * kk, bo)), 1e-12)
    k = k * (1.0 + (ag - 1.0) * ka_ref[...])
    r_ref[...] = r
    k_ref[...] = k
    v_ref[...] = v
    a_ref[...] = -kk
    b_ref[...] = kk * ag
    bonus_ref[...] = _head_sums(r * k * rk_ref[...], bo) * v


def rwkv_prep(proj, mu, w0_f, w2_f, w0_b, w2_b, a0, a2, g2, k_k, k_a, r_k, *, seq, tm=256):
    t = proj.shape[0]
    tiles_per_seq = seq // tm
    hb = tm // 8
    nblk8 = t // 8
    pad_rows = lambda w, lo: jnp.zeros((LANE, D_WIDTH), F32).at[lo:lo + w.shape[0]].set(w).astype(BF16)
    vec = lambda u: u.reshape(1, -1)
    consts = [vec(mu), pad_rows(w2_f, 0), pad_rows(w2_b, D_W_RANK), pad_rows(a2, 0), g2.astype(BF16),
              vec(w0_f), vec(w0_b), vec(a0), vec(k_k), vec(k_a), vec(r_k), _block_ones()]
    full = lambda arr: pl.BlockSpec(arr.shape, lambda i: (0, 0))
    out_spec = pl.BlockSpec((tm, D_WIDTH), lambda i: (i, 0))
    return pl.pallas_call(
        functools.partial(_rwkv_prep_kernel, tiles_per_seq=tiles_per_seq),
        out_shape=tuple(jax.ShapeDtypeStruct((t, D_WIDTH), F32) for _ in range(9)),
        grid=(t // tm,),
        in_specs=[pl.BlockSpec((tm, DC_PAD), lambda i: (i, 0)),
                  pl.BlockSpec((8, DC_PAD), lambda i: (jnp.maximum(i * hb - 1, 0), 0)),
                  pl.BlockSpec((8, DC_PAD), lambda i: (jnp.minimum((i + 1) * hb, nblk8 - 1), 0))]
                 + [full(c) for c in consts],
        out_specs=tuple(out_spec for _ in range(9)),
        compiler_params=_params("parallel"),
        name="rwkv_prep",
    )(proj, proj, proj, *consts)


def _rwkv_chunk_kernel(*refs, ngroups):
    ins, (yf_ref, yb_ref, mt_ref) = refs[:12], refs[12:]

    @pl.when(pl.program_id(1) == 0)
    def _():
        mt_ref[...] = jnp.zeros_like(mt_ref)

    row = lax.broadcasted_iota(jnp.int32, (RCH, RGW), 0)
    col = lax.broadcasted_iota(jnp.int32, (RCH, RGW), 1) & (RCH - 1)
    bdmask = (lax.broadcasted_iota(jnp.int32, (RGW, RGW), 0) // D_HEAD
              == lax.broadcasted_iota(jnp.int32, (RGW, RGW), 1) // D_HEAD)
    tr = lax.broadcasted_iota(jnp.int32, (RCH, RCH), 0)
    tc = lax.broadcasted_iota(jnp.int32, (RCH, RCH), 1)
    zero = jnp.zeros((), F32)

    def bd(z):
        zb = z.astype(BF16)
        return jnp.where(bdmask, jnp.concatenate([zb] * RG, axis=0), jnp.zeros((), BF16))

    def mm(x, y, dims=None):
        x = x.astype(BF16)
        if dims is None:
            return jnp.dot(x, y, preferred_element_type=F32)
        return lax.dot_general(x, y, dims, preferred_element_type=F32)

    chains = [(d, g) for d in range(2) for g in range(ngroups)]
    st = []
    for d, g in chains:
        backward = d == 1
        r_ref, k_ref, v_ref, a_ref, b_ref, lw_ref = ins[6 * d:6 * d + 6]
        tri = ((tc >= tr) if backward else (tc <= tr)).astype(BF16)
        sl = slice(RGW * g, RGW * (g + 1))
        r, k, v, a, b, lw = (ref[:, sl] for ref in (r_ref, k_ref, v_ref, a_ref, b_ref, lw_ref))
        lam = jnp.dot(jnp.concatenate([tri] * 3, axis=1), jnp.concatenate(_split3(lw), axis=0),
                      preferred_element_type=F32)
        lamc = lam[0:1] if backward else lam[RCH - 1:RCH]
        e_inv = jnp.exp(-lam)
        e_out = jnp.exp(lamc - lam)
        ar = jnp.concatenate([a * jnp.exp(lam - lw), r * jnp.exp(lam)], axis=0).astype(BF16)
        bk = jnp.concatenate([b * e_out, k * e_out], axis=0).astype(BF16)
        st.append(dict(ar=ar, bk=bk, v=v, lamc=lamc, sl=sl,
                       gb=mm(ar, bd(b * e_inv), NT_DIMS), gk=mm(ar, bd(k * e_inv), NT_DIMS)))
    for (d, g), c in zip(chains, st):
        strict = (col > row) if d == 1 else (col < row)
        incl = (col >= row) if d == 1 else (col <= row)
        c["lp"] = jnp.where(strict, c["gb"][:RCH], zero)
        lak = jnp.where(strict, c["gk"][:RCH], zero)
        c["grb"] = jnp.where(incl, c["gb"][RCH:], zero).astype(BF16)
        c["grk"] = jnp.where(incl, c["gk"][RCH:], zero).astype(BF16)
        c["mt"] = mt_ref[d, g]
        amrm = mm(c["ar"], c["mt"].astype(BF16), NT_DIMS)
        c["bdv"] = bd(c["v"])
        c["u"] = amrm[:RCH] + mm(lak, c["bdv"])
        c["rm"] = amrm[RCH:]
    for rnd in range(6):
        for c in st:
            lpb = c["lp"].astype(BF16)
            c["u"] = c["u"] + mm(lpb, bd(c["u"]))
            if rnd < 5:
                c["lp"] = mm(lpb, bd(c["lp"]))
    for (d, g), c in zip(chains, st):
        y_ref = yb_ref if d == 1 else yf_ref
        y_ref[:, c["sl"]] = c["rm"] + mm(c["grb"], bd(c["u"])) + mm(c["grk"], c["bdv"])
        uv = jnp.concatenate([c["u"], c["v"]], axis=0).astype(BF16)
        upd = lax.dot_general(uv, c["bk"], TN_DIMS, preferred_element_type=F32)
        mt_ref[d, g] = c["mt"] * jnp.exp(c["lamc"]) + jnp.where(bdmask, upd, zero)


def rwkv_chunked(r, k, v, a, b, lwf, lwb):
    bsz, s, wd = r.shape
    nc = s // RCH
    fspec = pl.BlockSpec((None, RCH, wd), lambda bi, n: (bi, n, 0))
    bspec = pl.BlockSpec((None, RCH, wd), lambda bi, n: (bi, nc - 1 - n, 0))
    return pl.pallas_call(
        functools.partial(_rwkv_chunk_kernel, ngroups=wd // RGW),
        out_shape=(jax.ShapeDtypeStruct((bsz, s, wd), F32), jax.ShapeDtypeStruct((bsz, s, wd), F32)),
        grid=(bsz, nc),
        in_specs=[fspec] * 6 + [bspec] * 6,
        out_specs=(fspec, bspec),
        scratch_shapes=[pltpu.VMEM((2, wd // RGW, RGW, RGW), F32)],
        compiler_params=_params("parallel", "arbitrary"),
        name="rwkv_chunked",
    )(r, k, v, a, b, lwf, r, k, v, a, b, lwb)


def _rwkv_post_kernel(yf_ref, yb_ref, bonus_ref, g_ref, lng_ref, lnb_ref, bo_ref, o_ref):
    bo = bo_ref[...]
    y = yf_ref[...] + yb_ref[...]
    yc = y - _head_sums(y, bo) * (1.0 / D_HEAD)
    var = _head_sums(yc * yc, bo) * (1.0 / D_HEAD)
    y = yc * lax.rsqrt(var + D_LN_EPS) * lng_ref[...] + lnb_ref[...]
    o_ref[...] = ((y + bonus_ref[...]) * g_ref[...]).astype(o_ref.dtype)


def rwkv_post(yf, yb, bonus, g, ln_g, ln_b, *, tm=512):
    t, wd = yf.shape
    row = pl.BlockSpec((tm, wd), lambda i: (i, 0))
    vec = pl.BlockSpec((1, wd), lambda i: (0, 0))
    bo = _block_ones()
    return pl.pallas_call(
        _rwkv_post_kernel,
        out_shape=jax.ShapeDtypeStruct((t, wd), BF16),
        grid=(t // tm,),
        in_specs=[row, row, row, row, vec, vec, pl.BlockSpec(bo.shape, lambda i: (0, 0))],
        out_specs=row,
        compiler_params=_params("parallel"),
        name="rwkv_post",
    )(yf, yb, bonus, g, ln_g.reshape(1, wd), ln_b.reshape(1, wd), bo)


def rwkv7_mixer(proj, bsz, s, mu, w0_f, w2_f, w0_b, w2_b, a0, a2, g2, k_k, k_a, r_k, ln_g, ln_b):
    r, k, v, a, b, lwf, lwb, g, bonus = rwkv_prep(proj, mu, w0_f, w2_f, w0_b, w2_b, a0, a2, g2, k_k, k_a,
                                                  r_k.reshape(-1), seq=s)
    r3 = lambda u: u.reshape(bsz, s, D_WIDTH)
    yf, yb = rwkv_chunked(r3(r), r3(k), r3(v), r3(a), r3(b), r3(lwf), r3(lwb))
    return rwkv_post(yf.reshape(-1, D_WIDTH), yb.reshape(-1, D_WIDTH), bonus, g, ln_g, ln_b)


MOE_TILE = 1024
MOE_SUB = 128
ROW_ALIGN = 16
MOE_PACK = 2


def _route(logit):
    lane = lax.broadcasted_iota(jnp.int32, logit.shape, 1)
    first_at = lambda mask: jnp.min(jnp.where(mask, lane, jnp.int32(LANE)), axis=-1, keepdims=True)
    is_grp = lane < N_GROUPS
    gl = jnp.where(is_grp, logit, NEG)
    gmax = jnp.max(gl, axis=-1, keepdims=True)
    p_grp = 1.0 / jnp.sum(jnp.where(is_grp, jnp.exp(gl - gmax), 0.0), axis=-1, keepdims=True)
    i_grp = first_at(is_grp & (gl == gmax))
    lo = N_GROUPS + i_grp * EXPERTS_PER_GROUP
    in_grp = (lane >= lo) & (lane < lo + EXPERTS_PER_GROUP)
    el = jnp.where(in_grp, logit, NEG)
    l1 = jnp.max(el, axis=-1, keepdims=True)
    i1 = first_at(in_grp & (el == l1))
    rest = in_grp & (lane != i1)
    el2 = jnp.where(rest, logit, NEG)
    l2 = jnp.max(el2, axis=-1, keepdims=True)
    i2 = first_at(rest & (el2 == l2))
    e2 = jnp.exp(l2 - l1)
    w1 = p_grp / (1.0 + e2)
    w2 = p_grp * e2 / (1.0 + e2)
    return i_grp, jnp.where(lane == i1, w1, jnp.where(lane == i2, w2, 0.0))


def _moe_sort_kernel(x_ref, g_ref, wr_ref, br_ref, hn_ref, comb_ref, pos_ref, off_ref):
    x = x_ref[...]
    tm = x.shape[0]
    hn = (x * lax.rsqrt(jnp.mean(x * x, axis=-1, keepdims=True) + EPS) * g_ref[...]).astype(BF16)
    wr, br = wr_ref[...], br_ref[...]
    i_grp, _ = _route(jnp.dot(hn, wr, preferred_element_type=F32) + br)
    lane = lax.broadcasted_iota(jnp.int32, (tm, LANE), 1)
    onehot = (lane == i_grp).astype(F32)
    ri = lax.broadcasted_iota(jnp.int32, (tm, tm), 0)
    ci = lax.broadcasted_iota(jnp.int32, (tm, tm), 1)
    earlier = jnp.dot((ci < ri).astype(BF16), onehot.astype(BF16), preferred_element_type=F32)
    cnt = jnp.broadcast_to(jnp.sum(onehot, axis=0, keepdims=True), (8, LANE))
    lane8 = lax.broadcasted_iota(jnp.int32, (8, LANE), 1)
    start = jnp.zeros((8, LANE), F32)
    for sh in range(1, N_GROUPS):
        start += jnp.where(lane8 >= sh, pltpu.roll(cnt, sh, 1), 0.0)
    off_ref[...] = start[0:1].astype(jnp.int32)
    pos = jnp.sum(onehot * (start[0:1] + earlier), axis=-1, keepdims=True).astype(jnp.int32)
    pos_ref[...] = pos
    perm_t = (ci == pos).astype(BF16)
    hn_s = lax.dot_general(perm_t, hn, TN_DIMS, preferred_element_type=F32).astype(BF16)
    hn_ref[...] = hn_s
    _, comb = _route(jnp.dot(hn_s, wr, preferred_element_type=F32) + br)
    comb_ref[...] = comb


def moe_sort(x, g, w_grp, b_grp, w_exp, b_exp):
    t, d = x.shape
    tm = MOE_TILE
    nr = N_GROUPS + N_EXPERTS
    wr = jnp.pad(jnp.concatenate([w_grp, w_exp], axis=1), ((0, 0), (0, LANE - nr))).astype(BF16)
    br = jnp.pad(jnp.concatenate([b_grp, b_exp]), (0, LANE - nr)).reshape(1, LANE)
    row = lambda i: (i, 0)
    hn, comb, pos, off = pl.pallas_call(
        _moe_sort_kernel,
        out_shape=(jax.ShapeDtypeStruct((t, d), BF16), jax.ShapeDtypeStruct((t, LANE), F32),
                   jax.ShapeDtypeStruct((t, 1), jnp.int32), jax.ShapeDtypeStruct((t // tm, 1, LANE), jnp.int32)),
        grid=(t // tm,),
        in_specs=[pl.BlockSpec((tm, d), row), pl.BlockSpec((1, d), lambda i: (0, 0)),
                  pl.BlockSpec((d, LANE), lambda i: (0, 0)), pl.BlockSpec((1, LANE), lambda i: (0, 0))],
        out_specs=(pl.BlockSpec((tm, d), row), pl.BlockSpec((tm, LANE), row), pl.BlockSpec((tm, 1), row),
                   pl.BlockSpec((None, 1, LANE), lambda i: (i, 0, 0))),
        compiler_params=_params("parallel"),
        name="moe_sort",
    )(x, g.reshape(1, d), wr, br)
    bounds = jnp.concatenate([off[:, 0, :N_GROUPS], jnp.full((t // tm, 1), tm, jnp.int32)], axis=1)
    return hn, comb, pos, bounds.reshape(-1)


def _moe_group_kernel(bounds_ref, hn_ref, c_ref, wg_ref, wu_ref, wd_ref, y_ref):
    i, g, j = pl.program_id(0), pl.program_id(1), pl.program_id(2)

    @pl.when((g == 0) & (j == 0))
    def _():
        y_ref[...] = jnp.zeros_like(y_ref)

    expert_lane = N_GROUPS + g * EXPERTS_PER_GROUP + j
    lane = lax.broadcasted_iota(jnp.int32, (MOE_SUB, LANE), 1)
    row = lax.broadcasted_iota(jnp.int32, (MOE_SUB, 1), 0)
    for half in range(MOE_PACK):
        base = (i * MOE_PACK + half) * (N_GROUPS + 1) + g
        lo = half * MOE_TILE
        start, end = lo + bounds_ref[base], lo + bounds_ref[base + 1]
        first = (start // ROW_ALIGN) * ROW_ALIGN

        def sub(k, carry, first=first, lo=lo):
            want = first + k * MOE_SUB
            r0 = pl.multiple_of(jnp.minimum(want, lo + MOE_TILE - MOE_SUB), ROW_ALIGN)
            rows = pl.ds(r0, MOE_SUB)
            x = hn_ref[rows, :]
            hg = jnp.dot(x, wg_ref[...], preferred_element_type=F32)
            hu = jnp.dot(x, wu_ref[...], preferred_element_type=F32)
            c = jnp.sum(jnp.where(lane == expert_lane, c_ref[rows, :], 0.0), axis=-1, keepdims=True)
            c = jnp.where(row + r0 >= want, c, 0.0)
            hid = (hg * jax.nn.sigmoid(hg)) * hu * c
            y_ref[rows, :] += jnp.dot(hid.astype(BF16), wd_ref[...], preferred_element_type=F32)
            return carry

        lax.fori_loop(0, (end - first + MOE_SUB - 1) // MOE_SUB, sub, 0)


def moe_group_experts(hn, comb, bounds, w_gate, w_up, w_down):
    t, d = hn.shape
    tm = MOE_TILE * MOE_PACK
    ne, _, ff = w_gate.shape
    ex = lambda i, g, j, b: (g * EXPERTS_PER_GROUP + j, 0, 0)
    once = pl.Buffered(1)
    return pl.pallas_call(
        _moe_group_kernel,
        out_shape=jax.ShapeDtypeStruct((t, d), F32),
        grid_spec=pltpu.PrefetchScalarGridSpec(
            num_scalar_prefetch=1,
            grid=(t // tm, N_GROUPS, EXPERTS_PER_GROUP),
            in_specs=[pl.BlockSpec((tm, d), lambda i, g, j, b: (i, 0), pipeline_mode=once),
                      pl.BlockSpec((tm, LANE), lambda i, g, j, b: (i, 0)),
                      pl.BlockSpec((None, d, ff), ex), pl.BlockSpec((None, d, ff), ex),
                      pl.BlockSpec((None, ff, d), ex)],
            out_specs=pl.BlockSpec((tm, d), lambda i, g, j, b: (i, 0), pipeline_mode=once)),
        compiler_params=_params("parallel", "arbitrary", "arbitrary"),
        name="moe_group_experts",
    )(bounds, hn, comb, w_gate, w_up, w_down)


def _moe_unsort_kernel(y_ref, pos_ref, x_ref, o_ref):
    tm = y_ref.shape[0]
    perm_t = (lax.broadcasted_iota(jnp.int32, (tm, tm), 1) == pos_ref[...]).astype(BF16)
    y = y_ref[...]
    hi = y.astype(BF16)
    lo = (y - hi.astype(F32)).astype(BF16)
    o_ref[...] = (x_ref[...] + jnp.dot(perm_t, hi, preferred_element_type=F32)
                  + jnp.dot(perm_t, lo, preferred_element_type=F32))


def moe_unsort(y, pos, x):
    t, d = x.shape
    tm, tn = MOE_TILE, d // 2
    blk = pl.BlockSpec((tm, tn), lambda i, j: (i, j))
    return pl.pallas_call(
        _moe_unsort_kernel,
        out_shape=jax.ShapeDtypeStruct((t, d), F32),
        grid=(t // tm, d // tn),
        in_specs=[blk, pl.BlockSpec((tm, 1), lambda i, j: (i, 0)), blk],
        out_specs=blk,
        compiler_params=_params("parallel", "arbitrary"),
        name="moe_unsort",
    )(y, pos, x)


def hier_moe(x, g, w_grp, b_grp, w_exp, b_exp, w_gate, w_up, w_down):
    hn, comb, pos, bounds = moe_sort(x, g, w_grp, b_grp, w_exp, b_exp)
    y = moe_group_experts(hn, comb, bounds, w_gate.astype(BF16), w_up.astype(BF16), w_down.astype(BF16))
    return moe_unsort(y, pos, x)


def even_layer(x, bsz, s, norm_g, rel_bias, w_in, w_out, w2_f, b_f, w2_b, b_b, onorm):
    n_pad = _round_up(EVEN_IN, LANE)
    w_in_p = jnp.pad(w_in, ((0, 0), (0, n_pad - EVEN_IN))).astype(BF16)
    proj = norm_linear(x, norm_g, w_in_p, tn_target=896).reshape(bsz, s, n_pad)
    tq = 256
    ya = attention(proj, proj, proj, heads=A_HEADS, dq=HEAD_DIM, dv=HEAD_DIM,
                   q_off=0, k_off=A_HEADS, v_off=2 * A_HEADS, scale=HEAD_DIM ** -0.5,
                   bias=dilated_bias_table(rel_bias, s, tq), tq=tq)
    q_col = 3 * A_WIDTH
    yb = gla_mixer(proj, w2_f, w2_b, b_f, b_b, onorm, q_col=q_col, k_col=q_col + B_KEYW,
                   v_col=q_col + 2 * B_KEYW, g_col=q_col + 2 * B_KEYW + B_WIDTH,
                   z_col=q_col + 2 * B_KEYW + 2 * B_WIDTH)
    t = bsz * s
    return out_proj(ya.reshape(t, A_WIDTH), yb.reshape(t, B_WIDTH), w_out.astype(BF16), x)


def _odd_columns(w_in, mu):
    c0 = C_IN
    cut = lambda u, lo, n: u[..., lo:lo + n]
    zpad = lambda u, n: jnp.pad(u, [(0, 0)] * (u.ndim - 1) + [(0, n)])
    off = np.cumsum((0,) + D_SPLITS)
    def rwkv_cols(u):
        parts = [cut(u, off[0], 3 * D_WIDTH), cut(u, off[6], D_G_RANK), cut(u, off[3], 2 * D_W_RANK),
                 cut(u, off[5], D_A_RANK)]
        u = jnp.concatenate(parts, axis=-1)
        return zpad(u, DC_PAD - u.shape[-1])
    w_kr = w_in[:, C_Q_RANK + C_KV_RANK:C_IN]
    w_all = jnp.concatenate([rwkv_cols(w_in[:, c0:]), w_in[:, :C_IN], _rot_half_cols(w_kr)], axis=1)
    n_pad = _round_up(w_all.shape[1], 9 * LANE)
    return zpad(w_all, n_pad - w_all.shape[1]).astype(BF16), rwkv_cols(mu)


def odd_layer(x, bsz, s, norm_g, w_in, w_out, q_norm, w_uq, kv_norm, w_ukv, mu, w0_f, w2_f, w0_b, w2_b,
              a0, a2, g2, k_k, k_a, r_k, ln_g, ln_b):
    t = bsz * s
    w_all, mu_cols = _odd_columns(w_in, mu)
    proj = norm_linear(x, norm_g, w_all, tn_target=1152)
    inv = 1.0 / (ROPE_THETA ** (jnp.arange(0, C_ROPE, 2, dtype=F32) / C_ROPE))
    ang = jnp.arange(s, dtype=F32)[:, None] * inv[None, :]
    cos = jnp.pad(jnp.concatenate([jnp.cos(ang)] * 2, axis=1), ((0, 0), (0, LANE - C_ROPE)), constant_values=1.0)
    sin = jnp.pad(jnp.concatenate([jnp.sin(ang)] * 2, axis=1), ((0, 0), (0, LANE - C_ROPE)))
    q, k, v = mla_up(proj, q_norm, w_uq, kv_norm, w_ukv, jnp.tile(cos, (bsz, 1)), jnp.tile(sin, (bsz, 1)),
                     col0=DC_PAD)
    r3 = lambda u: u.reshape(bsz, s, -1)
    yc = attention(r3(q), r3(k), r3(v), heads=C_HEADS, dq=C_QK, dv=C_V, q_off=0, k_off=0, v_off=0,
                   scale=(C_NOPE + C_ROPE) ** -0.5)
    yd = rwkv7_mixer(proj, bsz, s, mu_cols, w0_f, w2_f, w0_b, w2_b, a0, a2, g2, k_k, k_a, r_k, ln_g, ln_b)
    return out_proj(yc.reshape(t, C_WIDTH), yd, w_out.astype(BF16), x)


def kernel(x_prompt, x_sample, rel_bias, norm_mix, norm_ffn, norm_final, ev_w_in, ev_w_out, ev_gla_w2_f, ev_gla_b_f, ev_gla_w2_b, ev_gla_b_b, ev_gla_onorm, od_w_in, od_w_out, od_q_norm, od_w_uq, od_kv_norm, od_w_ukv, od_mu, od_w0_f, od_w2_f, od_w0_b, od_w2_b, od_a0, od_a2, od_g2, od_k_k, od_k_a, od_r_k, od_ln_g, od_ln_b, moe_w_grp, moe_b_grp, moe_w_exp, moe_b_exp, moe_w_gate, moe_w_up, moe_w_down):
    nb_p = x_prompt.shape[0]
    x = jnp.concatenate([x_prompt, x_sample], axis=0)
    bsz, s, d = x.shape
    x = x.reshape(bsz * s, d)
    for i in range(DEPTH):
        j = i // 2
        if i % 2 == 0:
            x = even_layer(x, bsz, s, norm_mix[i], rel_bias, ev_w_in[j], ev_w_out[j], ev_gla_w2_f[j],
                           ev_gla_b_f[j], ev_gla_w2_b[j], ev_gla_b_b[j], ev_gla_onorm[j])
        else:
            x = odd_layer(x, bsz, s, norm_mix[i], od_w_in[j], od_w_out[j], od_q_norm[j], od_w_uq[j],
                          od_kv_norm[j], od_w_ukv[j], od_mu[j], od_w0_f[j], od_w2_f[j], od_w0_b[j],
                          od_w2_b[j], od_a0[j], od_a2[j], od_g2[j], od_k_k[j], od_k_a[j], od_r_k[j],
                          od_ln_g[j], od_ln_b[j])
        x = hier_moe(x, norm_ffn[i], moe_w_grp[i], moe_b_grp[i], moe_w_exp[i], moe_b_exp[i],
                     moe_w_gate[i], moe_w_up[i], moe_w_down[i])
    y_p = final_norm(x, norm_final, row0=0, rows=nb_p * s)
    y_s = final_norm(x, norm_final, row0=nb_p * s, rows=(bsz - nb_p) * s)
    return (y_p.reshape(nb_p, s, d), y_s.reshape(bsz - nb_p, s, d))
```

```python
import functools

import jax, jax.numpy as jnp
from jax import lax
import numpy as np
from jax.experimental import pallas as pl
from jax.experimental.pallas import tpu as pltpu

F32, BF16 = jnp.float32, jnp.bfloat16

D_MODEL = 2048
DEPTH = 2
MIX_HALF = D_MODEL // 2
HEAD_DIM = 128
EPS = 1e-6
NEG = -1e30

A_HEADS = MIX_HALF // HEAD_DIM
A_WIDTH = A_HEADS * HEAD_DIM
A_PATTERNS = ((128, 1), (512, 4), (2048, 16))
N_BUCKETS = 32
MAX_DISTANCE = 1024

B_HEADS = 4
B_DV = MIX_HALF // B_HEADS
B_DK = B_DV // 2
B_WIDTH = B_HEADS * B_DV
B_KEYW = B_HEADS * B_DK
B_GATE_RANK = 16
B_GATE_TAU = 16.0
B_CHUNK = 64

C_HEADS = MIX_HALF // 128
C_Q_RANK = 512
C_KV_RANK = 256
C_NOPE = 128
C_ROPE = 64
C_V = 128
C_WIDTH = C_HEADS * C_V
C_QK = 256
ROPE_THETA = 10000.0

D_HEAD = 64
D_HEADS = MIX_HALF // D_HEAD
D_WIDTH = D_HEADS * D_HEAD
D_W_RANK = 64
D_A_RANK = 64
D_G_RANK = 128
D_LN_EPS = 64e-5
D_SPLITS = (D_WIDTH, D_WIDTH, D_WIDTH, D_W_RANK, D_W_RANK, D_A_RANK, D_G_RANK)
D_SHIFT = 3 * D_WIDTH + 2 * D_W_RANK + D_A_RANK + D_G_RANK

N_GROUPS = 4
EXPERTS_PER_GROUP = 4
N_EXPERTS = N_GROUPS * EXPERTS_PER_GROUP
EXPERT_FF = 512

EVEN_IN = 3 * A_WIDTH + 2 * B_KEYW + 2 * B_WIDTH + 2 * B_GATE_RANK
C_IN = C_Q_RANK + C_KV_RANK + C_ROPE
ODD_IN = C_IN + D_SHIFT

LANE = 128
VMEM_LIMIT = 52 * 1024 * 1024


def _params(*sem):
    return pltpu.CompilerParams(dimension_semantics=sem, vmem_limit_bytes=VMEM_LIMIT)


def _round_up(n, m):
    return -(-n // m) * m


NT_DIMS = (((1,), (1,)), ((), ()))
TN_DIMS = (((0,), (0,)), ((), ()))


def _split3(x):
    hi = x.astype(BF16)
    r1 = x - hi.astype(F32)
    mid = r1.astype(BF16)
    lo = (r1 - mid.astype(F32)).astype(BF16)
    return hi, mid, lo


def _pick_tile(n, target):
    best = LANE
    for t in range(LANE, target + 1, LANE):
        if n % t == 0:
            best = t
    return best


def _norm_linear_kernel(x_ref, g_ref, w_ref, o_ref, xn_ref):
    @pl.when(pl.program_id(1) == 0)
    def _():
        x = x_ref[...]
        y = x * lax.rsqrt(jnp.mean(x * x, axis=-1, keepdims=True) + EPS) * g_ref[...]
        xn_ref[...] = y.astype(BF16)

    o_ref[...] = jnp.dot(xn_ref[...], w_ref[...], preferred_element_type=F32)


def norm_linear(x, g, w, *, tm=1024, tn_target=1024):
    t, k = x.shape
    n = w.shape[1]
    tn = _pick_tile(n, tn_target)
    return pl.pallas_call(
        _norm_linear_kernel,
        out_shape=jax.ShapeDtypeStruct((t, n), F32),
        grid=(t // tm, n // tn),
        in_specs=[pl.BlockSpec((tm, k), lambda i, j: (i, 0)),
                  pl.BlockSpec((1, k), lambda i, j: (0, 0)),
                  pl.BlockSpec((k, tn), lambda i, j: (0, j))],
        out_specs=pl.BlockSpec((tm, tn), lambda i, j: (i, j)),
        scratch_shapes=[pltpu.VMEM((tm, k), BF16)],
        compiler_params=_params("parallel", "arbitrary"),
        name="norm_linear",
    )(x, g.reshape(1, k), w)


def _out_proj_kernel(a_ref, b_ref, wa_ref, wb_ref, x_ref, o_ref):
    acc = jnp.dot(a_ref[...], wa_ref[...], preferred_element_type=F32)
    acc += jnp.dot(b_ref[...], wb_ref[...], preferred_element_type=F32)
    o_ref[...] = x_ref[...] + acc


def out_proj(a, b, w, x, *, tm=1024, tn=512):
    t, ka = a.shape
    kb = b.shape[1]
    n = w.shape[1]
    return pl.pallas_call(
        _out_proj_kernel,
        out_shape=jax.ShapeDtypeStruct((t, n), F32),
        grid=(t // tm, n // tn),
        in_specs=[pl.BlockSpec((tm, ka), lambda i, j: (i, 0)),
                  pl.BlockSpec((tm, kb), lambda i, j: (i, 0)),
                  pl.BlockSpec((ka, tn), lambda i, j: (0, j)),
                  pl.BlockSpec((kb, tn), lambda i, j: (0, j)),
                  pl.BlockSpec((tm, tn), lambda i, j: (i, j))],
        out_specs=pl.BlockSpec((tm, tn), lambda i, j: (i, j)),
        compiler_params=_params("parallel", "arbitrary"),
        name="out_proj",
    )(a, b, w[:ka], w[ka:], x)


def _final_norm_kernel(x_ref, g_ref, o_ref):
    x = x_ref[...]
    o_ref[...] = x * lax.rsqrt(jnp.mean(x * x, axis=-1, keepdims=True) + EPS) * g_ref[...]


def final_norm(x, g, *, row0, rows, tm=1024):
    d = x.shape[1]
    return pl.pallas_call(
        _final_norm_kernel,
        out_shape=jax.ShapeDtypeStruct((rows, d), F32),
        grid=(rows // tm,),
        in_specs=[pl.BlockSpec((tm, d), lambda i: (i + row0 // tm, 0)), pl.BlockSpec((1, d), lambda i: (0, 0))],
        out_specs=pl.BlockSpec((tm, d), lambda i: (i, 0)),
        compiler_params=_params("parallel"),
        name="final_norm",
    )(x, g.reshape(1, d))


ATTN_KB = 256
ATTN_TQ = 512


def _attn_kernel(*refs, scale, has_bias):
    if has_bias:
        q_ref, k_ref, v_ref, bias_ref, o_ref = refs
    else:
        q_ref, k_ref, v_ref, o_ref = refs
    s_len = k_ref.shape[0]
    q = (q_ref[...] * scale).astype(BF16)
    blocks = [slice(j * ATTN_KB, (j + 1) * ATTN_KB) for j in range(s_len // ATTN_KB)]
    scores = []
    m = None
    for blk in blocks:
        sj = lax.dot_general(q, k_ref[blk, :].astype(BF16), NT_DIMS, preferred_element_type=F32)
        if has_bias:
            sj = sj + bias_ref[:, blk]
        mj = jnp.max(sj, axis=-1, keepdims=True)
        m = mj if m is None else jnp.maximum(m, mj)
        scores.append(sj)
    o = den = None
    for blk, sj in zip(blocks, scores):
        p = jnp.exp(sj - m)
        dj = jnp.sum(p, axis=-1, keepdims=True)
        oj = jnp.dot(p.astype(BF16), v_ref[blk, :].astype(BF16), preferred_element_type=F32)
        o, den = (oj, dj) if o is None else (o + oj, den + dj)
    o_ref[...] = (o / den).astype(o_ref.dtype)


def attention(q, k, v, *, heads, dq, dv, q_off, k_off, v_off, scale, bias=None, tq=ATTN_TQ):
    b, s, _ = q.shape
    nq = s // tq
    in_specs = [pl.BlockSpec((None, tq, dq), lambda bi, h, qi: (bi, qi, q_off + h)),
                pl.BlockSpec((None, s, dq), lambda bi, h, qi: (bi, 0, k_off + h)),
                pl.BlockSpec((None, s, dv), lambda bi, h, qi: (bi, 0, v_off + h))]
    args = [q, k, v]
    if bias is not None:
        in_specs.append(pl.BlockSpec((None, None, tq, s), lambda bi, h, qi: (h, qi, 0, 0)))
        args.append(bias)
    return pl.pallas_call(
        functools.partial(_attn_kernel, scale=scale, has_bias=bias is not None),
        out_shape=jax.ShapeDtypeStruct((b, s, heads * dv), BF16),
        grid=(b, heads, nq),
        in_specs=in_specs,
        out_specs=pl.BlockSpec((None, tq, dv), lambda bi, h, qi: (bi, qi, h)),
        compiler_params=_params("parallel", "parallel", "arbitrary"),
        name="attention_bias" if bias is not None else "attention",
    )(*args)


def _t5_bucket(rel):
    half = N_BUCKETS // 2
    exact = half // 2
    n = np.abs(rel)
    large = exact + (np.log(np.maximum(n, 1) / exact) / np.log(MAX_DISTANCE / exact) * (half - exact)).astype(np.int64)
    large = np.minimum(large, half - 1)
    return ((rel > 0) * half + np.where(n < exact, n, large)).astype(np.int32)


def dilated_bias_table(rel_bias, s, tq):
    heads = rel_bias.shape[1]
    d = np.arange(-(s - 1), s)
    count = np.zeros(d.shape, np.float32)
    for window, dil in A_PATTERNS:
        count += ((d % dil == 0) & (np.abs(d) <= (window // (2 * dil)) * dil)).astype(np.float32)
    logc = np.where(count > 0, np.log(np.maximum(count, 1.0)), NEG).astype(np.float32)
    onehot = (_t5_bucket(d)[:, None] == np.arange(N_BUCKETS)[None, :]).astype(np.float32)
    line = jnp.transpose(jnp.dot(onehot, rel_bias.astype(F32), precision=lax.Precision.HIGHEST)) + logc[None]
    width = 2 * s
    line = jnp.pad(line, ((0, 0), (0, width - line.shape[1])))[:, None, :]
    nq = s // tq
    return pl.pallas_call(
        functools.partial(_skew_kernel, tq=tq, nq=nq),
        out_shape=jax.ShapeDtypeStruct((heads, nq, tq, s), F32),
        grid=(heads, nq),
        in_specs=[pl.BlockSpec((None, 1, width), lambda h, qi: (h, 0, 0))],
        out_specs=pl.BlockSpec((None, None, tq, s), lambda h, qi: (h, qi, 0, 0)),
        compiler_params=_params("parallel", "arbitrary"),
        name="bias_skew",
    )(line)


def _skew_kernel(line_ref, o_ref, *, tq, nq):
    width = line_ref.shape[1]
    first = (nq - 1 - pl.program_id(1)) * tq
    x = jnp.broadcast_to(line_ref[...], (tq, width))
    x = pltpu.roll(x, width - (tq - 1) - first, 1, stride=1, stride_axis=0)
    o_ref[...] = x[:, :o_ref.shape[1]]


GLA_UNROLL = 8


def _gla_kernel(q_ref, k_ref, v_ref, g_ref, z_ref, w2f_ref, w2b_ref, bf_ref, bb_ref, on_ref, o_ref,
                la_ref, acc_ref, qcat_ref, upd_ref, dec_ref, scat_ref, st_ref):
    s_len = q_ref.shape[0]
    c = B_CHUNK
    nchunk = s_len // c
    z = z_ref[...].astype(BF16)
    gate = lambda w2_ref, b_ref: jax.nn.log_sigmoid(
        jnp.dot(z, w2_ref[...], preferred_element_type=F32) + b_ref[...]) * (1.0 / B_GATE_TAU)
    la_ref[0] = gate(w2f_ref, bf_ref)
    la_ref[1] = gate(w2b_ref, bb_ref)

    ri = lax.broadcasted_iota(jnp.int32, (c, c), 0)
    ci = lax.broadcasted_iota(jnp.int32, (c, c), 1)
    keep = (ri >= ci, ri <= ci)
    tri3 = tuple(jnp.concatenate([kp.astype(BF16)] * 3, axis=1) for kp in keep)

    def chunk_rows(n):
        return pl.ds(pl.multiple_of(n * c, c), c)

    def pass1(i, carry):
        for u in range(GLA_UNROLL):
            n = i * GLA_UNROLL + u
            rows = chunk_rows(n)
            qc = q_ref[rows, :] * (B_DK ** -0.5)
            kc = k_ref[rows, :]
            vc = v_ref[rows, :].astype(BF16)
            o = None
            for d in range(2):
                la = la_ref[d, rows, :]
                gcum = jnp.dot(tri3[d], jnp.concatenate(_split3(la), axis=0), preferred_element_type=F32)
                gend = gcum[0:1] if d == 1 else gcum[c - 1:c]
                q_in = (qc * jnp.exp(gcum)).astype(BF16)
                k_in = (kc * jnp.exp(-gcum)).astype(BF16)
                k_out = (kc * jnp.exp(gend - gcum)).astype(BF16)
                att = lax.dot_general(q_in, k_in, NT_DIMS, preferred_element_type=F32)
                att = jnp.where(keep[d], att, 0.0).astype(BF16)
                od = jnp.dot(att, vc, preferred_element_type=F32)
                o = od if o is None else o + od
                qcat_ref[rows, d * B_DK:(d + 1) * B_DK] = q_in
                upd_ref[d, n] = lax.dot_general(vc, k_out, TN_DIMS, preferred_element_type=F32)
                dec_ref[d, n] = jnp.broadcast_to(jnp.exp(gend), (8, B_DK))
            acc_ref[rows, :] = o
        return carry

    lax.fori_loop(0, nchunk // GLA_UNROLL, pass1, 0)

    st_ref[...] = jnp.zeros_like(st_ref)

    def pass2(n, carry):
        for d, m in ((0, n), (1, nchunk - 1 - n)):
            state = st_ref[d]
            scat_ref[m, :, d * B_DK:(d + 1) * B_DK] = state.astype(BF16)
            st_ref[d] = state * dec_ref[d, m][0:1] + upd_ref[d, m]
        return carry

    lax.fori_loop(0, nchunk, pass2, 0)

    def pass3(i, carry):
        for u in range(GLA_UNROLL):
            n = i * GLA_UNROLL + u
            rows = chunk_rows(n)
            o = acc_ref[rows, :] + lax.dot_general(qcat_ref[rows, :], scat_ref[n], NT_DIMS,
                                                   preferred_element_type=F32)
            o = o * lax.rsqrt(jnp.mean(o * o, axis=-1, keepdims=True) + EPS) * on_ref[...]
            g = g_ref[rows, :]
            o_ref[rows, :] = (o * (g * jax.nn.sigmoid(g))).astype(o_ref.dtype)
        return carry

    lax.fori_loop(0, nchunk // GLA_UNROLL, pass3, 0)


def gla_mixer(proj, w2f, w2b, b_f, b_b, onorm, *, q_col, k_col, v_col, g_col, z_col):
    b, s, _ = proj.shape
    hm = lambda blk: (lambda bi, h: (bi, 0, blk + h))
    w2f_p = jnp.zeros((LANE, B_KEYW), F32).at[:B_GATE_RANK].set(w2f).astype(BF16)
    w2b_p = jnp.zeros((LANE, B_KEYW), F32).at[B_GATE_RANK:2 * B_GATE_RANK].set(w2b).astype(BF16)
    return pl.pallas_call(
        _gla_kernel,
        out_shape=jax.ShapeDtypeStruct((b, s, B_WIDTH), BF16),
        grid=(b, B_HEADS),
        in_specs=[pl.BlockSpec((None, s, B_DK), hm(q_col // B_DK)),
                  pl.BlockSpec((None, s, B_DK), hm(k_col // B_DK)),
                  pl.BlockSpec((None, s, B_DV), hm(v_col // B_DV)),
                  pl.BlockSpec((None, s, B_DV), hm(g_col // B_DV)),
                  pl.BlockSpec((None, s, LANE), lambda bi, h: (bi, 0, z_col // LANE)),
                  pl.BlockSpec((LANE, B_DK), lambda bi, h: (0, h)),
                  pl.BlockSpec((LANE, B_DK), lambda bi, h: (0, h)),
                  pl.BlockSpec((1, B_DK), lambda bi, h: (0, h)),
                  pl.BlockSpec((1, B_DK), lambda bi, h: (0, h)),
                  pl.BlockSpec((1, B_DV), lambda bi, h: (0, 0))],
        out_specs=pl.BlockSpec((None, s, B_DV), lambda bi, h: (bi, 0, h)),
        scratch_shapes=[pltpu.VMEM((2, s, B_DK), F32),
                        pltpu.VMEM((s, B_DV), F32),
                        pltpu.VMEM((s, 2 * B_DK), BF16),
                        pltpu.VMEM((2, s // B_CHUNK, B_DV, B_DK), F32),
                        pltpu.VMEM((2, s // B_CHUNK, 8, B_DK), F32),
                        pltpu.VMEM((s // B_CHUNK, B_DV, 2 * B_DK), BF16),
                        pltpu.VMEM((2, B_DV, B_DK), F32)],
        compiler_params=_params("parallel", "arbitrary"),
        name="gla_mixer",
    )(proj, proj, proj, proj, proj, w2f_p, w2b_p, b_f.reshape(1, -1), b_b.reshape(1, -1), onorm.reshape(1, -1))


def _mla_up_kernel(cq_ref, ckv_ref, kr_ref, qn_ref, kvn_ref, wq_ref, wqr_ref, wkv_ref, cos_ref, sin_ref,
                   q_ref, k_ref, v_ref):
    def rms(x, g):
        return (x * lax.rsqrt(jnp.mean(x * x, axis=-1, keepdims=True) + EPS) * g).astype(BF16)

    cq = rms(cq_ref[...], qn_ref[...])
    ckv = rms(ckv_ref[...], kvn_ref[...])
    cos, sin = cos_ref[...], sin_ref[...]
    kr = kr_ref[...]
    k_rope = kr * cos + pltpu.roll(kr, LANE - C_ROPE, 1) * sin
    lane = lax.broadcasted_iota(jnp.int32, k_rope.shape, 1)
    k_rope = jnp.where(lane < C_ROPE, k_rope, 0.0)
    for h in range(C_HEADS):
        q = jnp.dot(cq, wq_ref[:, h * C_QK:(h + 1) * C_QK], preferred_element_type=F32)
        qp = jnp.dot(cq, wqr_ref[:, h * LANE:(h + 1) * LANE], preferred_element_type=F32)
        q_ref[:, h * C_QK:h * C_QK + C_NOPE] = q[:, :C_NOPE]
        q_ref[:, h * C_QK + C_NOPE:(h + 1) * C_QK] = q[:, C_NOPE:] * cos + qp * sin
        kv = jnp.dot(ckv, wkv_ref[:, h * 2 * LANE:(h + 1) * 2 * LANE], preferred_element_type=F32)
        k_ref[:, h * C_QK:h * C_QK + C_NOPE] = kv[:, :C_NOPE]
        k_ref[:, h * C_QK + C_NOPE:(h + 1) * C_QK] = k_rope
        v_ref[:, h * C_V:(h + 1) * C_V] = kv[:, C_NOPE:]


def _rot_half_cols(w):
    half = w.shape[-1] // 2
    return jnp.concatenate([-w[..., half:], w[..., :half]], axis=-1)


def mla_up(proj, q_norm, w_uq, kv_norm, w_ukv, cos, sin, *, col0, tm=512):
    t = proj.shape[0]
    wq = w_uq.reshape(C_Q_RANK, C_HEADS, C_NOPE + C_ROPE)
    wq_main = jnp.pad(wq, ((0, 0), (0, 0), (0, C_QK - C_NOPE - C_ROPE))).reshape(C_Q_RANK, C_HEADS * C_QK)
    wq_rot = jnp.pad(_rot_half_cols(wq[..., C_NOPE:]), ((0, 0), (0, 0), (0, LANE - C_ROPE)))
    wq_rot = wq_rot.reshape(C_Q_RANK, C_HEADS * LANE)
    row = lambda i: (i, 0)
    full = lambda arr: pl.BlockSpec(arr.shape, lambda i: (0, 0))
    g_q, g_kv = q_norm.reshape(1, -1), kv_norm.reshape(1, -1)
    wq_main, wq_rot, wkv = wq_main.astype(BF16), wq_rot.astype(BF16), w_ukv.astype(BF16)
    return pl.pallas_call(
        _mla_up_kernel,
        out_shape=(jax.ShapeDtypeStruct((t, C_HEADS * C_QK), F32),
                   jax.ShapeDtypeStruct((t, C_HEADS * C_QK), F32),
                   jax.ShapeDtypeStruct((t, C_WIDTH), F32)),
        grid=(t // tm,),
        in_specs=[pl.BlockSpec((tm, C_Q_RANK), lambda i: (i, col0 // C_Q_RANK)),
                  pl.BlockSpec((tm, C_KV_RANK), lambda i: (i, (col0 + C_Q_RANK) // C_KV_RANK)),
                  pl.BlockSpec((tm, LANE), lambda i: (i, (col0 + C_Q_RANK + C_KV_RANK) // LANE)),
                  full(g_q), full(g_kv), full(wq_main), full(wq_rot), full(wkv),
                  pl.BlockSpec((tm, LANE), row), pl.BlockSpec((tm, LANE), row)],
        out_specs=(pl.BlockSpec((tm, C_HEADS * C_QK), row),
                   pl.BlockSpec((tm, C_HEADS * C_QK), row),
                   pl.BlockSpec((tm, C_WIDTH), row)),
        compiler_params=_params("parallel"),
        name="mla_up",
    )(proj, proj, proj, g_q, g_kv, wq_main, wq_rot, wkv, cos, sin)


RG = 4
RGW = RG * D_HEAD
RCH = 64
DC_R, DC_K, DC_V = 0, D_WIDTH, 2 * D_WIDTH
DC_ZG = 3 * D_WIDTH
DC_ZW = DC_ZG + D_G_RANK
DC_ZA = DC_ZW + 2 * D_W_RANK
DC_PAD = 7 * 512


def _head_sums(x, bo):
    return jnp.concatenate(
        [jnp.dot(x[:, RGW * g:RGW * (g + 1)], bo, preferred_element_type=F32, precision=lax.Precision.HIGHEST)
         for g in range(x.shape[1] // RGW)], axis=1)


def _block_ones():
    i = np.arange(RGW)
    return jnp.asarray((i[:, None] // D_HEAD) == (i[None, :] // D_HEAD), F32)


def _rwkv_prep_kernel(x_ref, xp_ref, xn_ref, mu_ref, w2f_ref, w2b_ref, a2_ref, g2_ref, w0f_ref, w0b_ref,
                      a0_ref, kk_ref, ka_ref, rk_ref, bo_ref,
                      r_ref, k_ref, v_ref, a_ref, b_ref, lwf_ref, lwb_ref, g_ref, bonus_ref, *, tiles_per_seq):
    i = pl.program_id(0) % tiles_per_seq
    x = x_ref[...]
    tm = x.shape[0]
    row = lax.broadcasted_iota(jnp.int32, x.shape, 0)
    prev_row = jnp.where(i == 0, 0.0, xp_ref[7:8, :])
    next_row = jnp.where(i == tiles_per_seq - 1, 0.0, xn_ref[0:1, :])
    prev = jnp.where(row == 0, prev_row, pltpu.roll(x, 1, 0))
    nxt = jnp.where(row == tm - 1, next_row, pltpu.roll(x, tm - 1, 0))
    x = x + mu_ref[...] * (0.5 * (prev + nxt) - x)
    r, k, v = x[:, DC_R:DC_R + D_WIDTH], x[:, DC_K:DC_K + D_WIDTH], x[:, DC_V:DC_V + D_WIDTH]
    zg = x[:, DC_ZG:DC_ZG + LANE]
    zw = x[:, DC_ZW:DC_ZW + LANE]
    za = x[:, DC_ZA:DC_ZA + LANE]
    tz = jnp.tanh(zw).astype(BF16)
    log_decay = lambda w0_ref, w2_ref: -np.exp(-0.5).astype(np.float32) * jax.nn.sigmoid(
        w0_ref[...] + jnp.dot(tz, w2_ref[...], preferred_element_type=F32))
    lwf_ref[...] = log_decay(w0f_ref, w2f_ref)
    lwb_ref[...] = log_decay(w0b_ref, w2b_ref)
    ag = jax.nn.sigmoid(a0_ref[...] + jnp.dot(za.astype(BF16), a2_ref[...], preferred_element_type=F32))
    g_ref[...] = jnp.dot(jax.nn.sigmoid(zg).astype(BF16), g2_ref[...], preferred_element_type=F32)
    bo = bo_ref[...]
    kk = k * kk_ref[...]
    kk = kk / jnp.maximum(jnp.sqrt(_head_sums(kk * kk, bo)), 1e-12)
    k = k * (1.0 + (ag - 1.0) * ka_ref[...])
    r_ref[...] = r
    k_ref[...] = k
    v_ref[...] = v
    a_ref[...] = -kk
    b_ref[...] = kk * ag
    bonus_ref[...] = _head_sums(r * k * rk_ref[...], bo) * v


def rwkv_prep(proj, mu, w0_f, w2_f, w0_b, w2_b, a0, a2, g2, k_k, k_a, r_k, *, seq, tm=256):
    t = proj.shape[0]
    tiles_per_seq = seq // tm
    hb = tm // 8
    nblk8 = t // 8
    pad_rows = lambda w, lo: jnp.zeros((LANE, D_WIDTH), F32).at[lo:lo + w.shape[0]].set(w).astype(BF16)
    vec = lambda u: u.reshape(1, -1)
    consts = [vec(mu), pad_rows(w2_f, 0), pad_rows(w2_b, D_W_RANK), pad_rows(a2, 0), g2.astype(BF16),
              vec(w0_f), vec(w0_b), vec(a0), vec(k_k), vec(k_a), vec(r_k), _block_ones()]
    full = lambda arr: pl.BlockSpec(arr.shape, lambda i: (0, 0))
    out_spec = pl.BlockSpec((tm, D_WIDTH), lambda i: (i, 0))
    return pl.pallas_call(
        functools.partial(_rwkv_prep_kernel, tiles_per_seq=tiles_per_seq),
        out_shape=tuple(jax.ShapeDtypeStruct((t, D_WIDTH), F32) for _ in range(9)),
        grid=(t // tm,),
        in_specs=[pl.BlockSpec((tm, DC_PAD), lambda i: (i, 0)),
                  pl.BlockSpec((8, DC_PAD), lambda i: (jnp.maximum(i * hb - 1, 0), 0)),
                  pl.BlockSpec((8, DC_PAD), lambda i: (jnp.minimum((i + 1) * hb, nblk8 - 1), 0))]
                 + [full(c) for c in consts],
        out_specs=tuple(out_spec for _ in range(9)),
        compiler_params=_params("parallel"),
        name="rwkv_prep",
    )(proj, proj, proj, *consts)


def _rwkv_chunk_kernel(*refs, ngroups):
    ins, (yf_ref, yb_ref, mt_ref) = refs[:12], refs[12:]

    @pl.when(pl.program_id(1) == 0)
    def _():
        mt_ref[...] = jnp.zeros_like(mt_ref)

    row = lax.broadcasted_iota(jnp.int32, (RCH, RGW), 0)
    col = lax.broadcasted_iota(jnp.int32, (RCH, RGW), 1) & (RCH - 1)
    bdmask = (lax.broadcasted_iota(jnp.int32, (RGW, RGW), 0) // D_HEAD
              == lax.broadcasted_iota(jnp.int32, (RGW, RGW), 1) // D_HEAD)
    tr = lax.broadcasted_iota(jnp.int32, (RCH, RCH), 0)
    tc = lax.broadcasted_iota(jnp.int32, (RCH, RCH), 1)
    zero = jnp.zeros((), F32)

    def bd(z):
        zb = z.astype(BF16)
        return jnp.where(bdmask, jnp.concatenate([zb] * RG, axis=0), jnp.zeros((), BF16))

    def mm(x, y, dims=None):
        x = x.astype(BF16)
        if dims is None:
            return jnp.dot(x, y, preferred_element_type=F32)
        return lax.dot_general(x, y, dims, preferred_element_type=F32)

    chains = [(d, g) for d in range(2) for g in range(ngroups)]
    st = []
    for d, g in chains:
        backward = d == 1
        r_ref, k_ref, v_ref, a_ref, b_ref, lw_ref = ins[6 * d:6 * d + 6]
        tri = ((tc >= tr) if backward else (tc <= tr)).astype(BF16)
        sl = slice(RGW * g, RGW * (g + 1))
        r, k, v, a, b, lw = (ref[:, sl] for ref in (r_ref, k_ref, v_ref, a_ref, b_ref, lw_ref))
        lam = jnp.dot(jnp.concatenate([tri] * 3, axis=1), jnp.concatenate(_split3(lw), axis=0),
                      preferred_element_type=F32)
        lamc = lam[0:1] if backward else lam[RCH - 1:RCH]
        e_inv = jnp.exp(-lam)
        e_out = jnp.exp(lamc - lam)
        ar = jnp.concatenate([a * jnp.exp(lam - lw), r * jnp.exp(lam)], axis=0).astype(BF16)
        bk = jnp.concatenate([b * e_out, k * e_out], axis=0).astype(BF16)
        st.append(dict(ar=ar, bk=bk, v=v, lamc=lamc, sl=sl,
                       gb=mm(ar, bd(b * e_inv), NT_DIMS), gk=mm(ar, bd(k * e_inv), NT_DIMS)))
    for (d, g), c in zip(chains, st):
        strict = (col > row) if d == 1 else (col < row)
        incl = (col >= row) if d == 1 else (col <= row)
        c["lp"] = jnp.where(strict, c["gb"][:RCH], zero)
        lak = jnp.where(strict, c["gk"][:RCH], zero)
        c["grb"] = jnp.where(incl, c["gb"][RCH:], zero).astype(BF16)
        c["grk"] = jnp.where(incl, c["gk"][RCH:], zero).astype(BF16)
        c["mt"] = mt_ref[d, g]
        amrm = mm(c["ar"], c["mt"].astype(BF16), NT_DIMS)
        c["bdv"] = bd(c["v"])
        c["u"] = amrm[:RCH] + mm(lak, c["bdv"])
        c["rm"] = amrm[RCH:]
    for rnd in range(6):
        for c in st:
            lpb = c["lp"].astype(BF16)
            c["u"] = c["u"] + mm(lpb, bd(c["u"]))
            if rnd < 5:
                c["lp"] = mm(lpb, bd(c["lp"]))
    for (d, g), c in zip(chains, st):
        y_ref = yb_ref if d == 1 else yf_ref
        y_ref[:, c["sl"]] = c["rm"] + mm(c["grb"], bd(c["u"])) + mm(c["grk"], c["bdv"])
        uv = jnp.concatenate([c["u"], c["v"]], axis=0).astype(BF16)
        upd = lax.dot_general(uv, c["bk"], TN_DIMS, preferred_element_type=F32)
        mt_ref[d, g] = c["mt"] * jnp.exp(c["lamc"]) + jnp.where(bdmask, upd, zero)


def rwkv_chunked(r, k, v, a, b, lwf, lwb):
    bsz, s, wd = r.shape
    nc = s // RCH
    fspec = pl.BlockSpec((None, RCH, wd), lambda bi, n: (bi, n, 0))
    bspec = pl.BlockSpec((None, RCH, wd), lambda bi, n: (bi, nc - 1 - n, 0))
    return pl.pallas_call(
        functools.partial(_rwkv_chunk_kernel, ngroups=wd // RGW),
        out_shape=(jax.ShapeDtypeStruct((bsz, s, wd), F32), jax.ShapeDtypeStruct((bsz, s, wd), F32)),
        grid=(bsz, nc),
        in_specs=[fspec] * 6 + [bspec] * 6,
        out_specs=(fspec, bspec),
        scratch_shapes=[pltpu.VMEM((2, wd // RGW, RGW, RGW), F32)],
        compiler_params=_params("parallel", "arbitrary"),
        name="rwkv_chunked",
    )(r, k, v, a, b, lwf, r, k, v, a, b, lwb)


def _rwkv_post_kernel(yf_ref, yb_ref, bonus_ref, g_ref, lng_ref, lnb_ref, bo_ref, o_ref):
    bo = bo_ref[...]
    y = yf_ref[...] + yb_ref[...]
    yc = y - _head_sums(y, bo) * (1.0 / D_HEAD)
    var = _head_sums(yc * yc, bo) * (1.0 / D_HEAD)
    y = yc * lax.rsqrt(var + D_LN_EPS) * lng_ref[...] + lnb_ref[...]
    o_ref[...] = ((y + bonus_ref[...]) * g_ref[...]).astype(o_ref.dtype)


def rwkv_post(yf, yb, bonus, g, ln_g, ln_b, *, tm=512):
    t, wd = yf.shape
    row = pl.BlockSpec((tm, wd), lambda i: (i, 0))
    vec = pl.BlockSpec((1, wd), lambda i: (0, 0))
    bo = _block_ones()
    return pl.pallas_call(
        _rwkv_post_kernel,
        out_shape=jax.ShapeDtypeStruct((t, wd), BF16),
        grid=(t // tm,),
        in_specs=[row, row, row, row, vec, vec, pl.BlockSpec(bo.shape, lambda i: (0, 0))],
        out_specs=row,
        compiler_params=_params("parallel"),
        name="rwkv_post",
    )(yf, yb, bonus, g, ln_g.reshape(1, wd), ln_b.reshape(1, wd), bo)


def rwkv7_mixer(proj, bsz, s, mu, w0_f, w2_f, w0_b, w2_b, a0, a2, g2, k_k, k_a, r_k, ln_g, ln_b):
    r, k, v, a, b, lwf, lwb, g, bonus = rwkv_prep(proj, mu, w0_f, w2_f, w0_b, w2_b, a0, a2, g2, k_k, k_a,
                                                  r_k.reshape(-1), seq=s)
    r3 = lambda u: u.reshape(bsz, s, D_WIDTH)
    yf, yb = rwkv_chunked(r3(r), r3(k), r3(v), r3(a), r3(b), r3(lwf), r3(lwb))
    return rwkv_post(yf.reshape(-1, D_WIDTH), yb.reshape(-1, D_WIDTH), bonus, g, ln_g, ln_b)


MOE_TILE = 1024
MOE_SUB = 256
ROW_ALIGN = 16
MOE_PACK = 1


def _route(logit):
    lane = lax.broadcasted_iota(jnp.int32, logit.shape, 1)
    first_at = lambda mask: jnp.min(jnp.where(mask, lane, jnp.int32(LANE)), axis=-1, keepdims=True)
    is_grp = lane < N_GROUPS
    gl = jnp.where(is_grp, logit, NEG)
    gmax = jnp.max(gl, axis=-1, keepdims=True)
    p_grp = 1.0 / jnp.sum(jnp.where(is_grp, jnp.exp(gl - gmax), 0.0), axis=-1, keepdims=True)
    i_grp = first_at(is_grp & (gl == gmax))
    lo = N_GROUPS + i_grp * EXPERTS_PER_GROUP
    in_grp = (lane >= lo) & (lane < lo + EXPERTS_PER_GROUP)
    el = jnp.where(in_grp, logit, NEG)
    l1 = jnp.max(el, axis=-1, keepdims=True)
    i1 = first_at(in_grp & (el == l1))
    rest = in_grp & (lane != i1)
    el2 = jnp.where(rest, logit, NEG)
    l2 = jnp.max(el2, axis=-1, keepdims=True)
    i2 = first_at(rest & (el2 == l2))
    e2 = jnp.exp(l2 - l1)
    w1 = p_grp / (1.0 + e2)
    w2 = p_grp * e2 / (1.0 + e2)
    return i_grp, jnp.where(lane == i1, w1, jnp.where(lane == i2, w2, 0.0))


def _moe_sort_kernel(x_ref, g_ref, wr_ref, br_ref, hn_ref, comb_ref, pos_ref, off_ref):
    x = x_ref[...]
    tm = x.shape[0]
    hn = (x * lax.rsqrt(jnp.mean(x * x, axis=-1, keepdims=True) + EPS) * g_ref[...]).astype(BF16)
    wr, br = wr_ref[...], br_ref[...]
    i_grp, _ = _route(jnp.dot(hn, wr, preferred_element_type=F32) + br)
    lane = lax.broadcasted_iota(jnp.int32, (tm, LANE), 1)
    onehot = (lane == i_grp).astype(F32)
    ri = lax.broadcasted_iota(jnp.int32, (tm, tm), 0)
    ci = lax.broadcasted_iota(jnp.int32, (tm, tm), 1)
    earlier = jnp.dot((ci < ri).astype(BF16), onehot.astype(BF16), preferred_element_type=F32)
    cnt = jnp.broadcast_to(jnp.sum(onehot, axis=0, keepdims=True), (8, LANE))
    lane8 = lax.broadcasted_iota(jnp.int32, (8, LANE), 1)
    start = jnp.zeros((8, LANE), F32)
    for sh in range(1, N_GROUPS):
        start += jnp.where(lane8 >= sh, pltpu.roll(cnt, sh, 1), 0.0)
    off_ref[...] = start[0:1].astype(jnp.int32)
    pos = jnp.sum(onehot * (start[0:1] + earlier), axis=-1, keepdims=True).astype(jnp.int32)
    pos_ref[...] = pos
    perm_t = (ci == pos).astype(BF16)
    hn_s = lax.dot_general(perm_t, hn, TN_DIMS, preferred_element_type=F32).astype(BF16)
    hn_ref[...] = hn_s
    _, comb = _route(jnp.dot(hn_s, wr, preferred_element_type=F32) + br)
    comb_ref[...] = comb


def moe_sort(x, g, w_grp, b_grp, w_exp, b_exp):
    t, d = x.shape
    tm = MOE_TILE
    nr = N_GROUPS + N_EXPERTS
    wr = jnp.pad(jnp.concatenate([w_grp, w_exp], axis=1), ((0, 0), (0, LANE - nr))).astype(BF16)
    br = jnp.pad(jnp.concatenate([b_grp, b_exp]), (0, LANE - nr)).reshape(1, LANE)
    row = lambda i: (i, 0)
    hn, comb, pos, off = pl.pallas_call(
        _moe_sort_kernel,
        out_shape=(jax.ShapeDtypeStruct((t, d), BF16), jax.ShapeDtypeStruct((t, LANE), F32),
                   jax.ShapeDtypeStruct((t, 1), jnp.int32), jax.ShapeDtypeStruct((t // tm, 1, LANE), jnp.int32)),
        grid=(t // tm,),
        in_specs=[pl.BlockSpec((tm, d), row), pl.BlockSpec((1, d), lambda i: (0, 0)),
                  pl.BlockSpec((d, LANE), lambda i: (0, 0)), pl.BlockSpec((1, LANE), lambda i: (0, 0))],
        out_specs=(pl.BlockSpec((tm, d), row), pl.BlockSpec((tm, LANE), row), pl.BlockSpec((tm, 1), row),
                   pl.BlockSpec((None, 1, LANE), lambda i: (i, 0, 0))),
        compiler_params=_params("parallel"),
        name="moe_sort",
    )(x, g.reshape(1, d), wr, br)
    bounds = jnp.concatenate([off[:, 0, :N_GROUPS], jnp.full((t // tm, 1), tm, jnp.int32)], axis=1)
    return hn, comb, pos, bounds.reshape(-1)


def _moe_group_kernel(bounds_ref, hn_ref, c_ref, wg_ref, wu_ref, wd_ref, y_ref):
    i, g, j = pl.program_id(0), pl.program_id(1), pl.program_id(2)

    @pl.when((g == 0) & (j == 0))
    def _():
        y_ref[...] = jnp.zeros_like(y_ref)

    expert_lane = N_GROUPS + g * EXPERTS_PER_GROUP + j
    lane = lax.broadcasted_iota(jnp.int32, (MOE_SUB, LANE), 1)
    row = lax.broadcasted_iota(jnp.int32, (MOE_SUB, 1), 0)
    for half in range(MOE_PACK):
        base = (i * MOE_PACK + half) * (N_GROUPS + 1) + g
        lo = half * MOE_TILE
        start, end = lo + bounds_ref[base], lo + bounds_ref[base + 1]
        first = (start // ROW_ALIGN) * ROW_ALIGN

        def sub(k, carry, first=first, lo=lo):
            want = first + k * MOE_SUB
            r0 = pl.multiple_of(jnp.minimum(want, lo + MOE_TILE - MOE_SUB), ROW_ALIGN)
            rows = pl.ds(r0, MOE_SUB)
            x = hn_ref[rows, :]
            hg = jnp.dot(x, wg_ref[...], preferred_element_type=F32)
            hu = jnp.dot(x, wu_ref[...], preferred_element_type=F32)
            c = jnp.sum(jnp.where(lane == expert_lane, c_ref[rows, :], 0.0), axis=-1, keepdims=True)
            c = jnp.where(row + r0 >= want, c, 0.0)
            hid = (hg * jax.nn.sigmoid(hg)) * hu * c
            y_ref[rows, :] += jnp.dot(hid.astype(BF16), wd_ref[...], preferred_element_type=F32)
            return carry

        lax.fori_loop(0, (end - first + MOE_SUB - 1) // MOE_SUB, sub, 0)


def moe_group_experts(hn, comb, bounds, w_gate, w_up, w_down):
    t, d = hn.shape
    tm = MOE_TILE * MOE_PACK
    ne, _, ff = w_gate.shape
    ex = lambda i, g, j, b: (g * EXPERTS_PER_GROUP + j, 0, 0)
    once = pl.Buffered(1) if MOE_PACK > 1 else None
    return pl.pallas_call(
        _moe_group_kernel,
        out_shape=jax.ShapeDtypeStruct((t, d), F32),
        grid_spec=pltpu.PrefetchScalarGridSpec(
            num_scalar_prefetch=1,
            grid=(t // tm, N_GROUPS, EXPERTS_PER_GROUP),
            in_specs=[pl.BlockSpec((tm, d), lambda i, g, j, b: (i, 0), pipeline_mode=once),
                      pl.BlockSpec((tm, LANE), lambda i, g, j, b: (i, 0)),
                      pl.BlockSpec((None, d, ff), ex), pl.BlockSpec((None, d, ff), ex),
                      pl.BlockSpec((None, ff, d), ex)],
            out_specs=pl.BlockSpec((tm, d), lambda i, g, j, b: (i, 0), pipeline_mode=once)),
        compiler_params=_params("parallel", "arbitrary", "arbitrary"),
        name="moe_group_experts",
    )(bounds, hn, comb, w_gate, w_up, w_down)


def _moe_unsort_kernel(y_ref, pos_ref, x_ref, o_ref):
    tm = y_ref.shape[0]
    perm_t = (lax.broadcasted_iota(jnp.int32, (tm, tm), 1) == pos_ref[...]).astype(BF16)
    y = y_ref[...]
    hi = y.astype(BF16)
    lo = (y - hi.astype(F32)).astype(BF16)
    o_ref[...] = (x_ref[...] + jnp.dot(perm_t, hi, preferred_element_type=F32)
                  + jnp.dot(perm_t, lo, preferred_element_type=F32))


def moe_unsort(y, pos, x):
    t, d = x.shape
    tm, tn = MOE_TILE, d // 2
    blk = pl.BlockSpec((tm, tn), lambda i, j: (i, j))
    return pl.pallas_call(
        _moe_unsort_kernel,
        out_shape=jax.ShapeDtypeStruct((t, d), F32),
        grid=(t // tm, d // tn),
        in_specs=[blk, pl.BlockSpec((tm, 1), lambda i, j: (i, 0)), blk],
        out_specs=blk,
        compiler_params=_params("parallel", "arbitrary"),
        name="moe_unsort",
    )(y, pos, x)


def hier_moe(x, g, w_grp, b_grp, w_exp, b_exp, w_gate, w_up, w_down):
    hn, comb, pos, bounds = moe_sort(x, g, w_grp, b_grp, w_exp, b_exp)
    y = moe_group_experts(hn, comb, bounds, w_gate.astype(BF16), w_up.astype(BF16), w_down.astype(BF16))
    return moe_unsort(y, pos, x)


def even_layer(x, bsz, s, norm_g, rel_bias, w_in, w_out, w2_f, b_f, w2_b, b_b, onorm):
    n_pad = _round_up(EVEN_IN, LANE)
    w_in_p = jnp.pad(w_in, ((0, 0), (0, n_pad - EVEN_IN))).astype(BF16)
    proj = norm_linear(x, norm_g, w_in_p, tn_target=896).reshape(bsz, s, n_pad)
    tq = ATTN_TQ
    ya = attention(proj, proj, proj, heads=A_HEADS, dq=HEAD_DIM, dv=HEAD_DIM,
                   q_off=0, k_off=A_HEADS, v_off=2 * A_HEADS, scale=HEAD_DIM ** -0.5,
                   bias=dilated_bias_table(rel_bias, s, tq), tq=tq)
    q_col = 3 * A_WIDTH
    yb = gla_mixer(proj, w2_f, w2_b, b_f, b_b, onorm, q_col=q_col, k_col=q_col + B_KEYW,
                   v_col=q_col + 2 * B_KEYW, g_col=q_col + 2 * B_KEYW + B_WIDTH,
                   z_col=q_col + 2 * B_KEYW + 2 * B_WIDTH)
    t = bsz * s
    return out_proj(ya.reshape(t, A_WIDTH), yb.reshape(t, B_WIDTH), w_out.astype(BF16), x)


def _odd_columns(w_in, mu):
    c0 = C_IN
    cut = lambda u, lo, n: u[..., lo:lo + n]
    zpad = lambda u, n: jnp.pad(u, [(0, 0)] * (u.ndim - 1) + [(0, n)])
    off = np.cumsum((0,) + D_SPLITS)
    def rwkv_cols(u):
        parts = [cut(u, off[0], 3 * D_WIDTH), cut(u, off[6], D_G_RANK), cut(u, off[3], 2 * D_W_RANK),
                 cut(u, off[5], D_A_RANK)]
        u = jnp.concatenate(parts, axis=-1)
        return zpad(u, DC_PAD - u.shape[-1])
    w_kr = w_in[:, C_Q_RANK + C_KV_RANK:C_IN]
    w_all = jnp.concatenate([rwkv_cols(w_in[:, c0:]), w_in[:, :C_IN], _rot_half_cols(w_kr)], axis=1)
    n_pad = _round_up(w_all.shape[1], 9 * LANE)
    return zpad(w_all, n_pad - w_all.shape[1]).astype(BF16), rwkv_cols(mu)


def odd_layer(x, bsz, s, norm_g, w_in, w_out, q_norm, w_uq, kv_norm, w_ukv, mu, w0_f, w2_f, w0_b, w2_b,
              a0, a2, g2, k_k, k_a, r_k, ln_g, ln_b):
    t = bsz * s
    w_all, mu_cols = _odd_columns(w_in, mu)
    proj = norm_linear(x, norm_g, w_all, tn_target=1152)
    inv = 1.0 / (ROPE_THETA ** (jnp.arange(0, C_ROPE, 2, dtype=F32) / C_ROPE))
    ang = jnp.arange(s, dtype=F32)[:, None] * inv[None, :]
    cos = jnp.pad(jnp.concatenate([jnp.cos(ang)] * 2, axis=1), ((0, 0), (0, LANE - C_ROPE)), constant_values=1.0)
    sin = jnp.pad(jnp.concatenate([jnp.sin(ang)] * 2, axis=1), ((0, 0), (0, LANE - C_ROPE)))
    q, k, v = mla_up(proj, q_norm, w_uq, kv_norm, w_ukv, jnp.tile(cos, (bsz, 1)), jnp.tile(sin, (bsz, 1)),
                     col0=DC_PAD)
    r3 = lambda u: u.reshape(bsz, s, -1)
    yc = attention(r3(q), r3(k), r3(v), heads=C_HEADS, dq=C_QK, dv=C_V, q_off=0, k_off=0, v_off=0,
                   scale=(C_NOPE + C_ROPE) ** -0.5)
    yd = rwkv7_mixer(proj, bsz, s, mu_cols, w0_f, w2_f, w0_b, w2_b, a0, a2, g2, k_k, k_a, r_k, ln_g, ln_b)
    return out_proj(yc.reshape(t, C_WIDTH), yd, w_out.astype(BF16), x)


def kernel(x_prompt, x_sample, rel_bias, norm_mix, norm_ffn, norm_final, ev_w_in, ev_w_out, ev_gla_w2_f, ev_gla_b_f, ev_gla_w2_b, ev_gla_b_b, ev_gla_onorm, od_w_in, od_w_out, od_q_norm, od_w_uq, od_kv_norm, od_w_ukv, od_mu, od_w0_f, od_w2_f, od_w0_b, od_w2_b, od_a0, od_a2, od_g2, od_k_k, od_k_a, od_r_k, od_ln_g, od_ln_b, moe_w_grp, moe_b_grp, moe_w_exp, moe_b_exp, moe_w_gate, moe_w_up, moe_w_down):
    nb_p = x_prompt.shape[0]
    x = jnp.concatenate([x_prompt, x_sample], axis=0)
    bsz, s, d = x.shape
    x = x.reshape(bsz * s, d)
    for i in range(DEPTH):
        j = i // 2
        if i % 2 == 0:
            x = even_layer(x, bsz, s, norm_mix[i], rel_bias, ev_w_in[j], ev_w_out[j], ev_gla_w2_f[j],
                           ev_gla_b_f[j], ev_gla_w2_b[j], ev_gla_b_b[j], ev_gla_onorm[j])
        else:
            x = odd_layer(x, bsz, s, norm_mix[i], od_w_in[j], od_w_out[j], od_q_norm[j], od_w_uq[j],
                          od_kv_norm[j], od_w_ukv[j], od_mu[j], od_w0_f[j], od_w2_f[j], od_w0_b[j],
                          od_w2_b[j], od_a0[j], od_a2[j], od_g2[j], od_k_k[j], od_k_a[j], od_r_k[j],
                          od_ln_g[j], od_ln_b[j])
        x = hier_moe(x, norm_ffn[i], moe_w_grp[i], moe_b_grp[i], moe_w_exp[i], moe_b_exp[i],
                     moe_w_gate[i], moe_w_up[i], moe_w_down[i])
    y_p = final_norm(x, norm_final, row0=0, rows=nb_p * s)
    y_s = final_norm(x, norm_final, row0=nb_p * s, rows=(bsz - nb_p) * s)
    return (y_p.reshape(nb_p, s, d), y_s.reshape(bsz - nb_p, s, d))
```

```python
import functools

import jax, jax.numpy as jnp
from jax import lax
import numpy as np
from jax.experimental import pallas as pl
from jax.experimental.pallas import tpu as pltpu

F32, BF16 = jnp.float32, jnp.bfloat16

D_MODEL = 2048
DEPTH = 2
MIX_HALF = D_MODEL // 2
HEAD_DIM = 128
EPS = 1e-6
NEG = -1e30

A_HEADS = MIX_HALF // HEAD_DIM
A_WIDTH = A_HEADS * HEAD_DIM
A_PATTERNS = ((128, 1), (512, 4), (2048, 16))
N_BUCKETS = 32
MAX_DISTANCE = 1024

B_HEADS = 4
B_DV = MIX_HALF // B_HEADS
B_DK = B_DV // 2
B_WIDTH = B_HEADS * B_DV
B_KEYW = B_HEADS * B_DK
B_GATE_RANK = 16
B_GATE_TAU = 16.0
B_CHUNK = 64

C_HEADS = MIX_HALF // 128
C_Q_RANK = 512
C_KV_RANK = 256
C_NOPE = 128
C_ROPE = 64
C_V = 128
C_WIDTH = C_HEADS * C_V
C_QK = 256
ROPE_THETA = 10000.0

D_HEAD = 64
D_HEADS = MIX_HALF // D_HEAD
D_WIDTH = D_HEADS * D_HEAD
D_W_RANK = 64
D_A_RANK = 64
D_G_RANK = 128
D_LN_EPS = 64e-5
D_SPLITS = (D_WIDTH, D_WIDTH, D_WIDTH, D_W_RANK, D_W_RANK, D_A_RANK, D_G_RANK)
D_SHIFT = 3 * D_WIDTH + 2 * D_W_RANK + D_A_RANK + D_G_RANK

N_GROUPS = 4
EXPERTS_PER_GROUP = 4
N_EXPERTS = N_GROUPS * EXPERTS_PER_GROUP
EXPERT_FF = 512

EVEN_IN = 3 * A_WIDTH + 2 * B_KEYW + 2 * B_WIDTH + 2 * B_GATE_RANK
C_IN = C_Q_RANK + C_KV_RANK + C_ROPE
ODD_IN = C_IN + D_SHIFT

LANE = 128
VMEM_LIMIT = 52 * 1024 * 1024


def _params(*sem):
    return pltpu.CompilerParams(dimension_semantics=sem, vmem_limit_bytes=VMEM_LIMIT)


def _round_up(n, m):
    return -(-n // m) * m


NT_DIMS = (((1,), (1,)), ((), ()))
TN_DIMS = (((0,), (0,)), ((), ()))


def _split3(x):
    hi = x.astype(BF16)
    r1 = x - hi.astype(F32)
    mid = r1.astype(BF16)
    lo = (r1 - mid.astype(F32)).astype(BF16)
    return hi, mid, lo


def _pick_tile(n, target):
    best = LANE
    for t in range(LANE, target + 1, LANE):
        if n % t == 0:
            best = t
    return best


def _norm_linear_kernel(x_ref, g_ref, w_ref, o_ref, xn_ref):
    @pl.when(pl.program_id(1) == 0)
    def _():
        x = x_ref[...]
        y = x * lax.rsqrt(jnp.mean(x * x, axis=-1, keepdims=True) + EPS) * g_ref[...]
        xn_ref[...] = y.astype(BF16)

    o_ref[...] = jnp.dot(xn_ref[...], w_ref[...], preferred_element_type=F32)


def norm_linear(x, g, w, *, tm=1024, tn_target=1024):
    t, k = x.shape
    n = w.shape[1]
    tn = _pick_tile(n, tn_target)
    return pl.pallas_call(
        _norm_linear_kernel,
        out_shape=jax.ShapeDtypeStruct((t, n), F32),
        grid=(t // tm, n // tn),
        in_specs=[pl.BlockSpec((tm, k), lambda i, j: (i, 0)),
                  pl.BlockSpec((1, k), lambda i, j: (0, 0)),
                  pl.BlockSpec((k, tn), lambda i, j: (0, j))],
        out_specs=pl.BlockSpec((tm, tn), lambda i, j: (i, j)),
        scratch_shapes=[pltpu.VMEM((tm, k), BF16)],
        compiler_params=_params("parallel", "arbitrary"),
        name="norm_linear",
    )(x, g.reshape(1, k), w)


def _out_proj_kernel(a_ref, b_ref, wa_ref, wb_ref, x_ref, o_ref):
    acc = jnp.dot(a_ref[...], wa_ref[...], preferred_element_type=F32)
    acc += jnp.dot(b_ref[...], wb_ref[...], preferred_element_type=F32)
    o_ref[...] = x_ref[...] + acc


def out_proj(a, b, w, x, *, tm=1024, tn=512):
    t, ka = a.shape
    kb = b.shape[1]
    n = w.shape[1]
    return pl.pallas_call(
        _out_proj_kernel,
        out_shape=jax.ShapeDtypeStruct((t, n), F32),
        grid=(t // tm, n // tn),
        in_specs=[pl.BlockSpec((tm, ka), lambda i, j: (i, 0)),
                  pl.BlockSpec((tm, kb), lambda i, j: (i, 0)),
                  pl.BlockSpec((ka, tn), lambda i, j: (0, j)),
                  pl.BlockSpec((kb, tn), lambda i, j: (0, j)),
                  pl.BlockSpec((tm, tn), lambda i, j: (i, j))],
        out_specs=pl.BlockSpec((tm, tn), lambda i, j: (i, j)),
        compiler_params=_params("parallel", "arbitrary"),
        name="out_proj",
    )(a, b, w[:ka], w[ka:], x)


def _final_norm_kernel(x_ref, g_ref, o_ref):
    x = x_ref[...]
    o_ref[...] = x * lax.rsqrt(jnp.mean(x * x, axis=-1, keepdims=True) + EPS) * g_ref[...]


def final_norm(x, g, *, row0, rows, tm=1024):
    d = x.shape[1]
    return pl.pallas_call(
        _final_norm_kernel,
        out_shape=jax.ShapeDtypeStruct((rows, d), F32),
        grid=(rows // tm,),
        in_specs=[pl.BlockSpec((tm, d), lambda i: (i + row0 // tm, 0)), pl.BlockSpec((1, d), lambda i: (0, 0))],
        out_specs=pl.BlockSpec((tm, d), lambda i: (i, 0)),
        compiler_params=_params("parallel"),
        name="final_norm",
    )(x, g.reshape(1, d))


ATTN_KB = 256
ATTN_TQ = 512


def _attn_kernel(*refs, scale, has_bias):
    if has_bias:
        q_ref, k_ref, v_ref, bias_ref, o_ref = refs
    else:
        q_ref, k_ref, v_ref, o_ref = refs
    s_len = k_ref.shape[0]
    q = q_ref[...] if scale is None else (q_ref[...] * scale).astype(BF16)
    blocks = [slice(j * ATTN_KB, (j + 1) * ATTN_KB) for j in range(s_len // ATTN_KB)]
    fold = lambda x, op: functools.reduce(op, [x[:, t * LANE:(t + 1) * LANE] for t in range(x.shape[1] // LANE)])
    scores = []
    mpart = None
    for blk in blocks:
        sj = lax.dot_general(q, k_ref[blk, :].astype(BF16), NT_DIMS, preferred_element_type=F32)
        if has_bias:
            sj = sj + bias_ref[:, blk]
        mj = fold(sj, jnp.maximum)
        mpart = mj if mpart is None else jnp.maximum(mpart, mj)
        scores.append(sj)
    m = jnp.max(mpart, axis=-1, keepdims=True)
    o = dpart = None
    for blk, sj in zip(blocks, scores):
        p = jnp.exp(sj - m)
        dj = fold(p, jnp.add)
        oj = jnp.dot(p.astype(BF16), v_ref[blk, :].astype(BF16), preferred_element_type=F32)
        o, dpart = (oj, dj) if o is None else (o + oj, dpart + dj)
    den = jnp.sum(dpart, axis=-1, keepdims=True)
    o_ref[...] = (o / den).astype(o_ref.dtype)


def attention(q, k, v, *, heads, dq, dv, q_off, k_off, v_off, scale, bias=None, tq=ATTN_TQ):
    b, s, _ = q.shape
    nq = s // tq
    in_specs = [pl.BlockSpec((None, tq, dq), lambda bi, h, qi: (bi, qi, q_off + h)),
                pl.BlockSpec((None, s, dq), lambda bi, h, qi: (bi, 0, k_off + h)),
                pl.BlockSpec((None, s, dv), lambda bi, h, qi: (bi, 0, v_off + h))]
    args = [q, k, v]
    if bias is not None:
        in_specs.append(pl.BlockSpec((None, None, tq, s), lambda bi, h, qi: (h, qi, 0, 0)))
        args.append(bias)
    return pl.pallas_call(
        functools.partial(_attn_kernel, scale=scale, has_bias=bias is not None),
        out_shape=jax.ShapeDtypeStruct((b, s, heads * dv), BF16),
        grid=(b, heads, nq),
        in_specs=in_specs,
        out_specs=pl.BlockSpec((None, tq, dv), lambda bi, h, qi: (bi, qi, h)),
        compiler_params=_params("parallel", "parallel", "arbitrary"),
        name="attention_bias" if bias is not None else "attention",
    )(*args)


def _t5_bucket(rel):
    half = N_BUCKETS // 2
    exact = half // 2
    n = np.abs(rel)
    large = exact + (np.log(np.maximum(n, 1) / exact) / np.log(MAX_DISTANCE / exact) * (half - exact)).astype(np.int64)
    large = np.minimum(large, half - 1)
    return ((rel > 0) * half + np.where(n < exact, n, large)).astype(np.int32)


def dilated_bias_table(rel_bias, s, tq):
    heads = rel_bias.shape[1]
    d = np.arange(-(s - 1), s)
    count = np.zeros(d.shape, np.float32)
    for window, dil in A_PATTERNS:
        count += ((d % dil == 0) & (np.abs(d) <= (window // (2 * dil)) * dil)).astype(np.float32)
    logc = np.where(count > 0, np.log(np.maximum(count, 1.0)), NEG).astype(np.float32)
    onehot = (_t5_bucket(d)[:, None] == np.arange(N_BUCKETS)[None, :]).astype(np.float32)
    line = jnp.transpose(jnp.dot(onehot, rel_bias.astype(F32), precision=lax.Precision.HIGHEST)) + logc[None]
    width = 2 * s
    line = jnp.pad(line, ((0, 0), (0, width - line.shape[1])))[:, None, :]
    nq = s // tq
    return pl.pallas_call(
        functools.partial(_skew_kernel, tq=tq, nq=nq),
        out_shape=jax.ShapeDtypeStruct((heads, nq, tq, s), F32),
        grid=(heads, nq),
        in_specs=[pl.BlockSpec((None, 1, width), lambda h, qi: (h, 0, 0))],
        out_specs=pl.BlockSpec((None, None, tq, s), lambda h, qi: (h, qi, 0, 0)),
        compiler_params=_params("parallel", "arbitrary"),
        name="bias_skew",
    )(line)


def _skew_kernel(line_ref, o_ref, *, tq, nq):
    width = line_ref.shape[1]
    first = (nq - 1 - pl.program_id(1)) * tq
    x = jnp.broadcast_to(line_ref[...], (tq, width))
    x = pltpu.roll(x, width - (tq - 1) - first, 1, stride=1, stride_axis=0)
    o_ref[...] = x[:, :o_ref.shape[1]]


GLA_UNROLL = 8


def _gla_kernel(q_ref, k_ref, v_ref, g_ref, z_ref, w2f_ref, w2b_ref, bf_ref, bb_ref, on_ref, o_ref,
                la_ref, acc_ref, qcat_ref, upd_ref, dec_ref, scat_ref, st_ref):
    s_len = q_ref.shape[0]
    c = B_CHUNK
    nchunk = s_len // c
    z = z_ref[...].astype(BF16)
    gate = lambda w2_ref, b_ref: jax.nn.log_sigmoid(
        jnp.dot(z, w2_ref[...], preferred_element_type=F32) + b_ref[...]) * (1.0 / B_GATE_TAU)
    la_ref[0] = gate(w2f_ref, bf_ref)
    la_ref[1] = gate(w2b_ref, bb_ref)

    ri = lax.broadcasted_iota(jnp.int32, (c, c), 0)
    ci = lax.broadcasted_iota(jnp.int32, (c, c), 1)
    keep = (ri >= ci, ri <= ci)
    tri3 = tuple(jnp.concatenate([kp.astype(BF16)] * 3, axis=1) for kp in keep)

    def chunk_rows(n):
        return pl.ds(pl.multiple_of(n * c, c), c)

    def pass1(i, carry):
        for u in range(GLA_UNROLL):
            n = i * GLA_UNROLL + u
            rows = chunk_rows(n)
            qc = q_ref[rows, :] * (B_DK ** -0.5)
            kc = k_ref[rows, :]
            vc = v_ref[rows, :].astype(BF16)
            o = None
            for d in range(2):
                la = la_ref[d, rows, :]
                gcum = jnp.dot(tri3[d], jnp.concatenate(_split3(la), axis=0), preferred_element_type=F32)
                gend = gcum[0:1] if d == 1 else gcum[c - 1:c]
                q_in = (qc * jnp.exp(gcum)).astype(BF16)
                k_in = (kc * jnp.exp(-gcum)).astype(BF16)
                k_out = (kc * jnp.exp(gend - gcum)).astype(BF16)
                att = lax.dot_general(q_in, k_in, NT_DIMS, preferred_element_type=F32)
                att = jnp.where(keep[d], att, 0.0).astype(BF16)
                od = jnp.dot(att, vc, preferred_element_type=F32)
                o = od if o is None else o + od
                qcat_ref[rows, d * B_DK:(d + 1) * B_DK] = q_in
                upd_ref[d, n] = lax.dot_general(vc, k_out, TN_DIMS, preferred_element_type=F32)
                dec_ref[d, n] = jnp.broadcast_to(jnp.exp(gend), (8, B_DK))
            acc_ref[rows, :] = o
        return carry

    lax.fori_loop(0, nchunk // GLA_UNROLL, pass1, 0)

    st_ref[...] = jnp.zeros_like(st_ref)

    def pass2(n, carry):
        for d, m in ((0, n), (1, nchunk - 1 - n)):
            state = st_ref[d]
            scat_ref[m, :, d * B_DK:(d + 1) * B_DK] = state.astype(BF16)
            st_ref[d] = state * dec_ref[d, m][0:1] + upd_ref[d, m]
        return carry

    lax.fori_loop(0, nchunk, pass2, 0)

    def pass3(i, carry):
        for u in range(GLA_UNROLL):
            n = i * GLA_UNROLL + u
            rows = chunk_rows(n)
            o = acc_ref[rows, :] + lax.dot_general(qcat_ref[rows, :], scat_ref[n], NT_DIMS,
                                                   preferred_element_type=F32)
            o = o * lax.rsqrt(jnp.mean(o * o, axis=-1, keepdims=True) + EPS) * on_ref[...]
            g = g_ref[rows, :]
            o_ref[rows, :] = (o * (g * jax.nn.sigmoid(g))).astype(o_ref.dtype)
        return carry

    lax.fori_loop(0, nchunk // GLA_UNROLL, pass3, 0)


def gla_mixer(proj, w2f, w2b, b_f, b_b, onorm, *, q_col, k_col, v_col, g_col, z_col):
    b, s, _ = proj.shape
    hm = lambda blk: (lambda bi, h: (bi, 0, blk + h))
    w2f_p = jnp.zeros((LANE, B_KEYW), F32).at[:B_GATE_RANK].set(w2f).astype(BF16)
    w2b_p = jnp.zeros((LANE, B_KEYW), F32).at[B_GATE_RANK:2 * B_GATE_RANK].set(w2b).astype(BF16)
    return pl.pallas_call(
        _gla_kernel,
        out_shape=jax.ShapeDtypeStruct((b, s, B_WIDTH), BF16),
        grid=(b, B_HEADS),
        in_specs=[pl.BlockSpec((None, s, B_DK), hm(q_col // B_DK)),
                  pl.BlockSpec((None, s, B_DK), hm(k_col // B_DK)),
                  pl.BlockSpec((None, s, B_DV), hm(v_col // B_DV)),
                  pl.BlockSpec((None, s, B_DV), hm(g_col // B_DV)),
                  pl.BlockSpec((None, s, LANE), lambda bi, h: (bi, 0, z_col // LANE)),
                  pl.BlockSpec((LANE, B_DK), lambda bi, h: (0, h)),
                  pl.BlockSpec((LANE, B_DK), lambda bi, h: (0, h)),
                  pl.BlockSpec((1, B_DK), lambda bi, h: (0, h)),
                  pl.BlockSpec((1, B_DK), lambda bi, h: (0, h)),
                  pl.BlockSpec((1, B_DV), lambda bi, h: (0, 0))],
        out_specs=pl.BlockSpec((None, s, B_DV), lambda bi, h: (bi, 0, h)),
        scratch_shapes=[pltpu.VMEM((2, s, B_DK), F32),
                        pltpu.VMEM((s, B_DV), F32),
                        pltpu.VMEM((s, 2 * B_DK), BF16),
                        pltpu.VMEM((2, s // B_CHUNK, B_DV, B_DK), F32),
                        pltpu.VMEM((2, s // B_CHUNK, 8, B_DK), F32),
                        pltpu.VMEM((s // B_CHUNK, B_DV, 2 * B_DK), BF16),
                        pltpu.VMEM((2, B_DV, B_DK), F32)],
        compiler_params=_params("parallel", "arbitrary"),
        name="gla_mixer",
    )(proj, proj, proj, proj, proj, w2f_p, w2b_p, b_f.reshape(1, -1), b_b.reshape(1, -1), onorm.reshape(1, -1))


def _mla_up_kernel(cq_ref, ckv_ref, kr_ref, qn_ref, kvn_ref, wq_ref, wqr_ref, wkv_ref, cos_ref, sin_ref,
                   q_ref, k_ref, v_ref, *, q_scale):
    def rms(x, g):
        return (x * lax.rsqrt(jnp.mean(x * x, axis=-1, keepdims=True) + EPS) * g).astype(BF16)

    cq = rms(cq_ref[...], qn_ref[...])
    ckv = rms(ckv_ref[...], kvn_ref[...])
    cos, sin = cos_ref[...], sin_ref[...]
    kr = kr_ref[...]
    k_rope = kr * cos + pltpu.roll(kr, LANE - C_ROPE, 1) * sin
    lane = lax.broadcasted_iota(jnp.int32, k_rope.shape, 1)
    k_rope = jnp.where(lane < C_ROPE, k_rope, 0.0).astype(BF16)
    for h in range(C_HEADS):
        q = jnp.dot(cq, wq_ref[:, h * C_QK:(h + 1) * C_QK], preferred_element_type=F32)
        qp = jnp.dot(cq, wqr_ref[:, h * LANE:(h + 1) * LANE], preferred_element_type=F32)
        q_ref[:, h * C_QK:h * C_QK + C_NOPE] = (q[:, :C_NOPE] * q_scale).astype(BF16)
        q_ref[:, h * C_QK + C_NOPE:(h + 1) * C_QK] = ((q[:, C_NOPE:] * cos + qp * sin) * q_scale).astype(BF16)
        kv = jnp.dot(ckv, wkv_ref[:, h * 2 * LANE:(h + 1) * 2 * LANE], preferred_element_type=F32)
        k_ref[:, h * C_QK:h * C_QK + C_NOPE] = kv[:, :C_NOPE].astype(BF16)
        k_ref[:, h * C_QK + C_NOPE:(h + 1) * C_QK] = k_rope
        v_ref[:, h * C_V:(h + 1) * C_V] = kv[:, C_NOPE:].astype(BF16)


def _rot_half_cols(w):
    half = w.shape[-1] // 2
    return jnp.concatenate([-w[..., half:], w[..., :half]], axis=-1)


def mla_up(proj, q_norm, w_uq, kv_norm, w_ukv, cos, sin, *, col0, q_scale, tm=512):
    t = proj.shape[0]
    wq = w_uq.reshape(C_Q_RANK, C_HEADS, C_NOPE + C_ROPE)
    wq_main = jnp.pad(wq, ((0, 0), (0, 0), (0, C_QK - C_NOPE - C_ROPE))).reshape(C_Q_RANK, C_HEADS * C_QK)
    wq_rot = jnp.pad(_rot_half_cols(wq[..., C_NOPE:]), ((0, 0), (0, 0), (0, LANE - C_ROPE)))
    wq_rot = wq_rot.reshape(C_Q_RANK, C_HEADS * LANE)
    row = lambda i: (i, 0)
    full = lambda arr: pl.BlockSpec(arr.shape, lambda i: (0, 0))
    g_q, g_kv = q_norm.reshape(1, -1), kv_norm.reshape(1, -1)
    wq_main, wq_rot, wkv = wq_main.astype(BF16), wq_rot.astype(BF16), w_ukv.astype(BF16)
    return pl.pallas_call(
        functools.partial(_mla_up_kernel, q_scale=q_scale),
        out_shape=(jax.ShapeDtypeStruct((t, C_HEADS * C_QK), BF16),
                   jax.ShapeDtypeStruct((t, C_HEADS * C_QK), BF16),
                   jax.ShapeDtypeStruct((t, C_WIDTH), BF16)),
        grid=(t // tm,),
        in_specs=[pl.BlockSpec((tm, C_Q_RANK), lambda i: (i, col0 // C_Q_RANK)),
                  pl.BlockSpec((tm, C_KV_RANK), lambda i: (i, (col0 + C_Q_RANK) // C_KV_RANK)),
                  pl.BlockSpec((tm, LANE), lambda i: (i, (col0 + C_Q_RANK + C_KV_RANK) // LANE)),
                  full(g_q), full(g_kv), full(wq_main), full(wq_rot), full(wkv),
                  pl.BlockSpec((tm, LANE), row), pl.BlockSpec((tm, LANE), row)],
        out_specs=(pl.BlockSpec((tm, C_HEADS * C_QK), row),
                   pl.BlockSpec((tm, C_HEADS * C_QK), row),
                   pl.BlockSpec((tm, C_WIDTH), row)),
        compiler_params=_params("parallel"),
        name="mla_up",
    )(proj, proj, proj, g_q, g_kv, wq_main, wq_rot, wkv, cos, sin)


RG = 4
RGW = RG * D_HEAD
RCH = 64
DC_R, DC_K, DC_V = 0, D_WIDTH, 2 * D_WIDTH
DC_ZG = 3 * D_WIDTH
DC_ZW = DC_ZG + D_G_RANK
DC_ZA = DC_ZW + 2 * D_W_RANK
DC_PAD = 7 * 512


def _head_sums(x, bo):
    return jnp.concatenate(
        [jnp.dot(x[:, RGW * g:RGW * (g + 1)], bo, preferred_element_type=F32, precision=lax.Precision.HIGHEST)
         for g in range(x.shape[1] // RGW)], axis=1)


def _block_ones():
    i = np.arange(RGW)
    return jnp.asarray((i[:, None] // D_HEAD) == (i[None, :] // D_HEAD), F32)


def _rwkv_prep_kernel(x_ref, xp_ref, xn_ref, mu_ref, w2f_ref, w2b_ref, a2_ref, g2_ref, w0f_ref, w0b_ref,
                      a0_ref, kk_ref, ka_ref, rk_ref, bo_ref,
                      r_ref, k_ref, v_ref, a_ref, b_ref, lwf_ref, lwb_ref, g_ref, bonus_ref, *, tiles_per_seq):
    i = pl.program_id(0) % tiles_per_seq
    x = x_ref[...]
    tm = x.shape[0]
    row = lax.broadcasted_iota(jnp.int32, x.shape, 0)
    prev_row = jnp.where(i == 0, 0.0, xp_ref[7:8, :])
    next_row = jnp.where(i == tiles_per_seq - 1, 0.0, xn_ref[0:1, :])
    prev = jnp.where(row == 0, prev_row, pltpu.roll(x, 1, 0))
    nxt = jnp.where(row == tm - 1, next_row, pltpu.roll(x, tm - 1, 0))
    x = x + mu_ref[...] * (0.5 * (prev + nxt) - x)
    r, k, v = x[:, DC_R:DC_R + D_WIDTH], x[:, DC_K:DC_K + D_WIDTH], x[:, DC_V:DC_V + D_WIDTH]
    zg = x[:, DC_ZG:DC_ZG + LANE]
    zw = x[:, DC_ZW:DC_ZW + LANE]
    za = x[:, DC_ZA:DC_ZA + LANE]
    tz = jnp.tanh(zw).astype(BF16)
    log_decay = lambda w0_ref, w2_ref: -np.exp(-0.5).astype(np.float32) * jax.nn.sigmoid(
        w0_ref[...] + jnp.dot(tz, w2_ref[...], preferred_element_type=F32))
    lwf_ref[...] = log_decay(w0f_ref, w2f_ref)
    lwb_ref[...] = log_decay(w0b_ref, w2b_ref)
    ag = jax.nn.sigmoid(a0_ref[...] + jnp.dot(za.astype(BF16), a2_ref[...], preferred_element_type=F32))
    g_ref[...] = jnp.dot(jax.nn.sigmoid(zg).astype(BF16), g2_ref[...], preferred_element_type=F32)
    bo = bo_ref[...]
    kk = k * kk_ref[...]
    kk = kk / jnp.maximum(jnp.sqrt(_head_sums(kk * kk, bo)), 1e-12)
    k = k * (1.0 + (ag - 1.0) * ka_ref[...])
    r_ref[...] = r
    k_ref[...] = k
    v_ref[...] = v
    a_ref[...] = -kk
    b_ref[...] = kk * ag
    bonus_ref[...] = _head_sums(r * k * rk_ref[...], bo) * v


def rwkv_prep(proj, mu, w0_f, w2_f, w0_b, w2_b, a0, a2, g2, k_k, k_a, r_k, *, seq, tm=256):
    t = proj.shape[0]
    tiles_per_seq = seq // tm
    hb = tm // 8
    nblk8 = t // 8
    pad_rows = lambda w, lo: jnp.zeros((LANE, D_WIDTH), F32).at[lo:lo + w.shape[0]].set(w).astype(BF16)
    vec = lambda u: u.reshape(1, -1)
    consts = [vec(mu), pad_rows(w2_f, 0), pad_rows(w2_b, D_W_RANK), pad_rows(a2, 0), g2.astype(BF16),
              vec(w0_f), vec(w0_b), vec(a0), vec(k_k), vec(k_a), vec(r_k), _block_ones()]
    full = lambda arr: pl.BlockSpec(arr.shape, lambda i: (0, 0))
    out_spec = pl.BlockSpec((tm, D_WIDTH), lambda i: (i, 0))
    return pl.pallas_call(
        functools.partial(_rwkv_prep_kernel, tiles_per_seq=tiles_per_seq),
        out_shape=tuple(jax.ShapeDtypeStruct((t, D_WIDTH), F32) for _ in range(9)),
        grid=(t // tm,),
        in_specs=[pl.BlockSpec((tm, DC_PAD), lambda i: (i, 0)),
                  pl.BlockSpec((8, DC_PAD), lambda i: (jnp.maximum(i * hb - 1, 0), 0)),
                  pl.BlockSpec((8, DC_PAD), lambda i: (jnp.minimum((i + 1) * hb, nblk8 - 1), 0))]
                 + [full(c) for c in consts],
        out_specs=tuple(out_spec for _ in range(9)),
        compiler_params=_params("parallel"),
        name="rwkv_prep",
    )(proj, proj, proj, *consts)


def _rwkv_chunk_kernel(*refs, ngroups):
    ins, (yf_ref, yb_ref, mt_ref) = refs[:12], refs[12:]

    @pl.when(pl.program_id(1) == 0)
    def _():
        mt_ref[...] = jnp.zeros_like(mt_ref)

    row = lax.broadcasted_iota(jnp.int32, (RCH, RGW), 0)
    col = lax.broadcasted_iota(jnp.int32, (RCH, RGW), 1) & (RCH - 1)
    bdmask = (lax.broadcasted_iota(jnp.int32, (RGW, RGW), 0) // D_HEAD
              == lax.broadcasted_iota(jnp.int32, (RGW, RGW), 1) // D_HEAD)
    tr = lax.broadcasted_iota(jnp.int32, (RCH, RCH), 0)
    tc = lax.broadcasted_iota(jnp.int32, (RCH, RCH), 1)
    zero = jnp.zeros((), F32)

    def bd(z):
        zb = z.astype(BF16)
        return jnp.where(bdmask, jnp.concatenate([zb] * RG, axis=0), jnp.zeros((), BF16))

    def mm(x, y, dims=None):
        x = x.astype(BF16)
        if dims is None:
            return jnp.dot(x, y, preferred_element_type=F32)
        return lax.dot_general(x, y, dims, preferred_element_type=F32)

    chains = [(d, g) for d in range(2) for g in range(ngroups)]
    st = []
    for d, g in chains:
        backward = d == 1
        r_ref, k_ref, v_ref, a_ref, b_ref, lw_ref = ins[6 * d:6 * d + 6]
        tri = ((tc >= tr) if backward else (tc <= tr)).astype(BF16)
        sl = slice(RGW * g, RGW * (g + 1))
        r, k, v, a, b, lw = (ref[:, sl] for ref in (r_ref, k_ref, v_ref, a_ref, b_ref, lw_ref))
        lam = jnp.dot(jnp.concatenate([tri] * 3, axis=1), jnp.concatenate(_split3(lw), axis=0),
                      preferred_element_type=F32)
        lamc = lam[0:1] if backward else lam[RCH - 1:RCH]
        e_inv = jnp.exp(-lam)
        e_out = jnp.exp(lamc - lam)
        ar = jnp.concatenate([a * jnp.exp(lam - lw), r * jnp.exp(lam)], axis=0).astype(BF16)
        bk = jnp.concatenate([b * e_out, k * e_out], axis=0).astype(BF16)
        st.append(dict(ar=ar, bk=bk, v=v, lamc=lamc, sl=sl,
                       gb=mm(ar, bd(b * e_inv), NT_DIMS), gk=mm(ar, bd(k * e_inv), NT_DIMS)))
    for (d, g), c in zip(chains, st):
        strict = (col > row) if d == 1 else (col < row)
        incl = (col >= row) if d == 1 else (col <= row)
        c["lp"] = jnp.where(strict, c["gb"][:RCH], zero)
        lak = jnp.where(strict, c["gk"][:RCH], zero)
        c["grb"] = jnp.where(incl, c["gb"][RCH:], zero).astype(BF16)
        c["grk"] = jnp.where(incl, c["gk"][RCH:], zero).astype(BF16)
        c["mt"] = mt_ref[d, g]
        amrm = mm(c["ar"], c["mt"].astype(BF16), NT_DIMS)
        c["bdv"] = bd(c["v"])
        c["u"] = amrm[:RCH] + mm(lak, c["bdv"])
        c["rm"] = amrm[RCH:]
    for rnd in range(6):
        for c in st:
            lpb = c["lp"].astype(BF16)
            c["u"] = c["u"] + mm(lpb, bd(c["u"]))
            if rnd < 5:
                c["lp"] = mm(lpb, bd(c["lp"]))
    for (d, g), c in zip(chains, st):
        y_ref = yb_ref if d == 1 else yf_ref
        y_ref[:, c["sl"]] = c["rm"] + mm(c["grb"], bd(c["u"])) + mm(c["grk"], c["bdv"])
        uv = jnp.concatenate([c["u"], c["v"]], axis=0).astype(BF16)
        upd = lax.dot_general(uv, c["bk"], TN_DIMS, preferred_element_type=F32)
        mt_ref[d, g] = c["mt"] * jnp.exp(c["lamc"]) + jnp.where(bdmask, upd, zero)


def rwkv_chunked(r, k, v, a, b, lwf, lwb):
    bsz, s, wd = r.shape
    nc = s // RCH
    fspec = pl.BlockSpec((None, RCH, wd), lambda bi, n: (bi, n, 0))
    bspec = pl.BlockSpec((None, RCH, wd), lambda bi, n: (bi, nc - 1 - n, 0))
    return pl.pallas_call(
        functools.partial(_rwkv_chunk_kernel, ngroups=wd // RGW),
        out_shape=(jax.ShapeDtypeStruct((bsz, s, wd), F32), jax.ShapeDtypeStruct((bsz, s, wd), F32)),
        grid=(bsz, nc),
        in_specs=[fspec] * 6 + [bspec] * 6,
        out_specs=(fspec, bspec),
        scratch_shapes=[pltpu.VMEM((2, wd // RGW, RGW, RGW), F32)],
        compiler_params=_params("parallel", "arbitrary"),
        name="rwkv_chunked",
    )(r, k, v, a, b, lwf, r, k, v, a, b, lwb)


def _rwkv_post_kernel(yf_ref, yb_ref, bonus_ref, g_ref, lng_ref, lnb_ref, bo_ref, o_ref):
    bo = bo_ref[...]
    y = yf_ref[...] + yb_ref[...]
    yc = y - _head_sums(y, bo) * (1.0 / D_HEAD)
    var = _head_sums(yc * yc, bo) * (1.0 / D_HEAD)
    y = yc * lax.rsqrt(var + D_LN_EPS) * lng_ref[...] + lnb_ref[...]
    o_ref[...] = ((y + bonus_ref[...]) * g_ref[...]).astype(o_ref.dtype)


def rwkv_post(yf, yb, bonus, g, ln_g, ln_b, *, tm=512):
    t, wd = yf.shape
    row = pl.BlockSpec((tm, wd), lambda i: (i, 0))
    vec = pl.BlockSpec((1, wd), lambda i: (0, 0))
    bo = _block_ones()
    return pl.pallas_call(
        _rwkv_post_kernel,
        out_shape=jax.ShapeDtypeStruct((t, wd), BF16),
        grid=(t // tm,),
        in_specs=[row, row, row, row, vec, vec, pl.BlockSpec(bo.shape, lambda i: (0, 0))],
        out_specs=row,
        compiler_params=_params("parallel"),
        name="rwkv_post",
    )(yf, yb, bonus, g, ln_g.reshape(1, wd), ln_b.reshape(1, wd), bo)


def rwkv7_mixer(proj, bsz, s, mu, w0_f, w2_f, w0_b, w2_b, a0, a2, g2, k_k, k_a, r_k, ln_g, ln_b):
    r, k, v, a, b, lwf, lwb, g, bonus = rwkv_prep(proj, mu, w0_f, w2_f, w0_b, w2_b, a0, a2, g2, k_k, k_a,
                                                  r_k.reshape(-1), seq=s)
    r3 = lambda u: u.reshape(bsz, s, D_WIDTH)
    yf, yb = rwkv_chunked(r3(r), r3(k), r3(v), r3(a), r3(b), r3(lwf), r3(lwb))
    return rwkv_post(yf.reshape(-1, D_WIDTH), yb.reshape(-1, D_WIDTH), bonus, g, ln_g, ln_b)


MOE_TILE = 1024
MOE_SUB = 256
ROW_ALIGN = 16
MOE_PACK = 1


def _route(logit):
    lane = lax.broadcasted_iota(jnp.int32, logit.shape, 1)
    first_at = lambda mask: jnp.min(jnp.where(mask, lane, jnp.int32(LANE)), axis=-1, keepdims=True)
    is_grp = lane < N_GROUPS
    gl = jnp.where(is_grp, logit, NEG)
    gmax = jnp.max(gl, axis=-1, keepdims=True)
    p_grp = 1.0 / jnp.sum(jnp.where(is_grp, jnp.exp(gl - gmax), 0.0), axis=-1, keepdims=True)
    i_grp = first_at(is_grp & (gl == gmax))
    lo = N_GROUPS + i_grp * EXPERTS_PER_GROUP
    in_grp = (lane >= lo) & (lane < lo + EXPERTS_PER_GROUP)
    el = jnp.where(in_grp, logit, NEG)
    l1 = jnp.max(el, axis=-1, keepdims=True)
    i1 = first_at(in_grp & (el == l1))
    rest = in_grp & (lane != i1)
    el2 = jnp.where(rest, logit, NEG)
    l2 = jnp.max(el2, axis=-1, keepdims=True)
    i2 = first_at(rest & (el2 == l2))
    e2 = jnp.exp(l2 - l1)
    w1 = p_grp / (1.0 + e2)
    w2 = p_grp * e2 / (1.0 + e2)
    return i_grp, jnp.where(lane == i1, w1, jnp.where(lane == i2, w2, 0.0))


def _moe_sort_kernel(x_ref, g_ref, wr_ref, br_ref, hn_ref, comb_ref, pos_ref, off_ref):
    x = x_ref[...]
    tm = x.shape[0]
    hn = (x * lax.rsqrt(jnp.mean(x * x, axis=-1, keepdims=True) + EPS) * g_ref[...]).astype(BF16)
    wr, br = wr_ref[...], br_ref[...]
    i_grp, _ = _route(jnp.dot(hn, wr, preferred_element_type=F32) + br)
    lane = lax.broadcasted_iota(jnp.int32, (tm, LANE), 1)
    onehot = (lane == i_grp).astype(F32)
    ri = lax.broadcasted_iota(jnp.int32, (tm, tm), 0)
    ci = lax.broadcasted_iota(jnp.int32, (tm, tm), 1)
    earlier = jnp.dot((ci < ri).astype(BF16), onehot.astype(BF16), preferred_element_type=F32)
    cnt = jnp.broadcast_to(jnp.sum(onehot, axis=0, keepdims=True), (8, LANE))
    lane8 = lax.broadcasted_iota(jnp.int32, (8, LANE), 1)
    start = jnp.zeros((8, LANE), F32)
    for sh in range(1, N_GROUPS):
        start += jnp.where(lane8 >= sh, pltpu.roll(cnt, sh, 1), 0.0)
    off_ref[...] = start[0:1].astype(jnp.int32)
    pos = jnp.sum(onehot * (start[0:1] + earlier), axis=-1, keepdims=True).astype(jnp.int32)
    pos_ref[...] = pos
    perm_t = (ci == pos).astype(BF16)
    hn_s = lax.dot_general(perm_t, hn, TN_DIMS, preferred_element_type=F32).astype(BF16)
    hn_ref[...] = hn_s
    _, comb = _route(jnp.dot(hn_s, wr, preferred_element_type=F32) + br)
    comb_ref[...] = comb


def moe_sort(x, g, w_grp, b_grp, w_exp, b_exp):
    t, d = x.shape
    tm = MOE_TILE
    nr = N_GROUPS + N_EXPERTS
    wr = jnp.pad(jnp.concatenate([w_grp, w_exp], axis=1), ((0, 0), (0, LANE - nr))).astype(BF16)
    br = jnp.pad(jnp.concatenate([b_grp, b_exp]), (0, LANE - nr)).reshape(1, LANE)
    row = lambda i: (i, 0)
    hn, comb, pos, off = pl.pallas_call(
        _moe_sort_kernel,
        out_shape=(jax.ShapeDtypeStruct((t, d), BF16), jax.ShapeDtypeStruct((t, LANE), F32),
                   jax.ShapeDtypeStruct((t, 1), jnp.int32), jax.ShapeDtypeStruct((t // tm, 1, LANE), jnp.int32)),
        grid=(t // tm,),
        in_specs=[pl.BlockSpec((tm, d), row), pl.BlockSpec((1, d), lambda i: (0, 0)),
                  pl.BlockSpec((d, LANE), lambda i: (0, 0)), pl.BlockSpec((1, LANE), lambda i: (0, 0))],
        out_specs=(pl.BlockSpec((tm, d), row), pl.BlockSpec((tm, LANE), row), pl.BlockSpec((tm, 1), row),
                   pl.BlockSpec((None, 1, LANE), lambda i: (i, 0, 0))),
        compiler_params=_params("parallel"),
        name="moe_sort",
    )(x, g.reshape(1, d), wr, br)
    bounds = jnp.concatenate([off[:, 0, :N_GROUPS], jnp.full((t // tm, 1), tm, jnp.int32)], axis=1)
    return hn, comb, pos, bounds.reshape(-1)


def _moe_group_kernel(bounds_ref, hn_ref, c_ref, wg_ref, wu_ref, wd_ref, y_ref):
    i, g, j = pl.program_id(0), pl.program_id(1), pl.program_id(2)

    @pl.when((g == 0) & (j == 0))
    def _():
        y_ref[...] = jnp.zeros_like(y_ref)

    expert_lane = N_GROUPS + g * EXPERTS_PER_GROUP + j
    lane = lax.broadcasted_iota(jnp.int32, (MOE_SUB, LANE), 1)
    row = lax.broadcasted_iota(jnp.int32, (MOE_SUB, 1), 0)
    for half in range(MOE_PACK):
        base = (i * MOE_PACK + half) * (N_GROUPS + 1) + g
        lo = half * MOE_TILE
        start, end = lo + bounds_ref[base], lo + bounds_ref[base + 1]
        first = (start // ROW_ALIGN) * ROW_ALIGN

        def sub(k, carry, first=first, lo=lo):
            want = first + k * MOE_SUB
            r0 = pl.multiple_of(jnp.minimum(want, lo + MOE_TILE - MOE_SUB), ROW_ALIGN)
            rows = pl.ds(r0, MOE_SUB)
            x = hn_ref[rows, :]
            hg = jnp.dot(x, wg_ref[...], preferred_element_type=F32)
            hu = jnp.dot(x, wu_ref[...], preferred_element_type=F32)
            c = jnp.sum(jnp.where(lane == expert_lane, c_ref[rows, :], 0.0), axis=-1, keepdims=True)
            c = jnp.where(row + r0 >= want, c, 0.0)
            hid = (hg * jax.nn.sigmoid(hg)) * hu * c
            y_ref[rows, :] += jnp.dot(hid.astype(BF16), wd_ref[...], preferred_element_type=F32)
            return carry

        lax.fori_loop(0, (end - first + MOE_SUB - 1) // MOE_SUB, sub, 0)


def moe_group_experts(hn, comb, bounds, w_gate, w_up, w_down):
    t, d = hn.shape
    tm = MOE_TILE * MOE_PACK
    ne, _, ff = w_gate.shape
    ex = lambda i, g, j, b: (g * EXPERTS_PER_GROUP + j, 0, 0)
    once = pl.Buffered(1) if MOE_PACK > 1 else None
    return pl.pallas_call(
        _moe_group_kernel,
        out_shape=jax.ShapeDtypeStruct((t, d), F32),
        grid_spec=pltpu.PrefetchScalarGridSpec(
            num_scalar_prefetch=1,
            grid=(t // tm, N_GROUPS, EXPERTS_PER_GROUP),
            in_specs=[pl.BlockSpec((tm, d), lambda i, g, j, b: (i, 0), pipeline_mode=once),
                      pl.BlockSpec((tm, LANE), lambda i, g, j, b: (i, 0)),
                      pl.BlockSpec((None, d, ff), ex), pl.BlockSpec((None, d, ff), ex),
                      pl.BlockSpec((None, ff, d), ex)],
            out_specs=pl.BlockSpec((tm, d), lambda i, g, j, b: (i, 0), pipeline_mode=once)),
        compiler_params=_params("parallel", "arbitrary", "arbitrary"),
        name="moe_group_experts",
    )(bounds, hn, comb, w_gate, w_up, w_down)


def _moe_unsort_kernel(y_ref, pos_ref, x_ref, o_ref):
    tm = y_ref.shape[0]
    perm_t = (lax.broadcasted_iota(jnp.int32, (tm, tm), 1) == pos_ref[...]).astype(BF16)
    y = y_ref[...]
    hi = y.astype(BF16)
    lo = (y - hi.astype(F32)).astype(BF16)
    o_ref[...] = (x_ref[...] + jnp.dot(perm_t, hi, preferred_element_type=F32)
                  + jnp.dot(perm_t, lo, preferred_element_type=F32))


def moe_unsort(y, pos, x):
    t, d = x.shape
    tm, tn = MOE_TILE, d // 2
    blk = pl.BlockSpec((tm, tn), lambda i, j: (i, j))
    return pl.pallas_call(
        _moe_unsort_kernel,
        out_shape=jax.ShapeDtypeStruct((t, d), F32),
        grid=(t // tm, d // tn),
        in_specs=[blk, pl.BlockSpec((tm, 1), lambda i, j: (i, 0)), blk],
        out_specs=blk,
        compiler_params=_params("parallel", "arbitrary"),
        name="moe_unsort",
    )(y, pos, x)


def hier_moe(x, g, w_grp, b_grp, w_exp, b_exp, w_gate, w_up, w_down):
    hn, comb, pos, bounds = moe_sort(x, g, w_grp, b_grp, w_exp, b_exp)
    y = moe_group_experts(hn, comb, bounds, w_gate.astype(BF16), w_up.astype(BF16), w_down.astype(BF16))
    return moe_unsort(y, pos, x)


def even_layer(x, bsz, s, norm_g, rel_bias, w_in, w_out, w2_f, b_f, w2_b, b_b, onorm):
    n_pad = _round_up(EVEN_IN, LANE)
    w_in_p = jnp.pad(w_in, ((0, 0), (0, n_pad - EVEN_IN))).astype(BF16)
    proj = norm_linear(x, norm_g, w_in_p, tn_target=896).reshape(bsz, s, n_pad)
    tq = ATTN_TQ
    ya = attention(proj, proj, proj, heads=A_HEADS, dq=HEAD_DIM, dv=HEAD_DIM,
                   q_off=0, k_off=A_HEADS, v_off=2 * A_HEADS, scale=HEAD_DIM ** -0.5,
                   bias=dilated_bias_table(rel_bias, s, tq), tq=tq)
    q_col = 3 * A_WIDTH
    yb = gla_mixer(proj, w2_f, w2_b, b_f, b_b, onorm, q_col=q_col, k_col=q_col + B_KEYW,
                   v_col=q_col + 2 * B_KEYW, g_col=q_col + 2 * B_KEYW + B_WIDTH,
                   z_col=q_col + 2 * B_KEYW + 2 * B_WIDTH)
    t = bsz * s
    return out_proj(ya.reshape(t, A_WIDTH), yb.reshape(t, B_WIDTH), w_out.astype(BF16), x)


def _odd_columns(w_in, mu):
    c0 = C_IN
    cut = lambda u, lo, n: u[..., lo:lo + n]
    zpad = lambda u, n: jnp.pad(u, [(0, 0)] * (u.ndim - 1) + [(0, n)])
    off = np.cumsum((0,) + D_SPLITS)
    def rwkv_cols(u):
        parts = [cut(u, off[0], 3 * D_WIDTH), cut(u, off[6], D_G_RANK), cut(u, off[3], 2 * D_W_RANK),
                 cut(u, off[5], D_A_RANK)]
        u = jnp.concatenate(parts, axis=-1)
        return zpad(u, DC_PAD - u.shape[-1])
    w_kr = w_in[:, C_Q_RANK + C_KV_RANK:C_IN]
    w_all = jnp.concatenate([rwkv_cols(w_in[:, c0:]), w_in[:, :C_IN], _rot_half_cols(w_kr)], axis=1)
    n_pad = _round_up(w_all.shape[1], 9 * LANE)
    return zpad(w_all, n_pad - w_all.shape[1]).astype(BF16), rwkv_cols(mu)


def odd_layer(x, bsz, s, norm_g, w_in, w_out, q_norm, w_uq, kv_norm, w_ukv, mu, w0_f, w2_f, w0_b, w2_b,
              a0, a2, g2, k_k, k_a, r_k, ln_g, ln_b):
    t = bsz * s
    w_all, mu_cols = _odd_columns(w_in, mu)
    proj = norm_linear(x, norm_g, w_all, tn_target=1152)
    inv = 1.0 / (ROPE_THETA ** (jnp.arange(0, C_ROPE, 2, dtype=F32) / C_ROPE))
    ang = jnp.arange(s, dtype=F32)[:, None] * inv[None, :]
    cos = jnp.pad(jnp.concatenate([jnp.cos(ang)] * 2, axis=1), ((0, 0), (0, LANE - C_ROPE)), constant_values=1.0)
    sin = jnp.pad(jnp.concatenate([jnp.sin(ang)] * 2, axis=1), ((0, 0), (0, LANE - C_ROPE)))
    q, k, v = mla_up(proj, q_norm, w_uq, kv_norm, w_ukv, jnp.tile(cos, (bsz, 1)), jnp.tile(sin, (bsz, 1)),
                     col0=DC_PAD, q_scale=(C_NOPE + C_ROPE) ** -0.5)
    r3 = lambda u: u.reshape(bsz, s, -1)
    yc = attention(r3(q), r3(k), r3(v), heads=C_HEADS, dq=C_QK, dv=C_V, q_off=0, k_off=0, v_off=0, scale=None)
    yd = rwkv7_mixer(proj, bsz, s, mu_cols, w0_f, w2_f, w0_b, w2_b, a0, a2, g2, k_k, k_a, r_k, ln_g, ln_b)
    return out_proj(yc.reshape(t, C_WIDTH), yd, w_out.astype(BF16), x)


def kernel(x_prompt, x_sample, rel_bias, norm_mix, norm_ffn, norm_final, ev_w_in, ev_w_out, ev_gla_w2_f, ev_gla_b_f, ev_gla_w2_b, ev_gla_b_b, ev_gla_onorm, od_w_in, od_w_out, od_q_norm, od_w_uq, od_kv_norm, od_w_ukv, od_mu, od_w0_f, od_w2_f, od_w0_b, od_w2_b, od_a0, od_a2, od_g2, od_k_k, od_k_a, od_r_k, od_ln_g, od_ln_b, moe_w_grp, moe_b_grp, moe_w_exp, moe_b_exp, moe_w_gate, moe_w_up, moe_w_down):
    nb_p = x_prompt.shape[0]
    x = jnp.concatenate([x_prompt, x_sample], axis=0)
    bsz, s, d = x.shape
    x = x.reshape(bsz * s, d)
    for i in range(DEPTH):
        j = i // 2
        if i % 2 == 0:
            x = even_layer(x, bsz, s, norm_mix[i], rel_bias, ev_w_in[j], ev_w_out[j], ev_gla_w2_f[j],
                           ev_gla_b_f[j], ev_gla_w2_b[j], ev_gla_b_b[j], ev_gla_onorm[j])
        else:
            x = odd_layer(x, bsz, s, norm_mix[i], od_w_in[j], od_w_out[j], od_q_norm[j], od_w_uq[j],
                          od_kv_norm[j], od_w_ukv[j], od_mu[j], od_w0_f[j], od_w2_f[j], od_w0_b[j],
                          od_w2_b[j], od_a0[j], od_a2[j], od_g2[j], od_k_k[j], od_k_a[j], od_r_k[j],
                          od_ln_g[j], od_ln_b[j])
        x = hier_moe(x, norm_ffn[i], moe_w_grp[i], moe_b_grp[i], moe_w_exp[i], moe_b_exp[i],
                     moe_w_gate[i], moe_w_up[i], moe_w_down[i])
    y_p = final_norm(x, norm_final, row0=0, rows=nb_p * s)
    y_s = final_norm(x, norm_final, row0=nb_p * s, rows=(bsz - nb_p) * s)
    return (y_p.reshape(nb_p, s, d), y_s.reshape(bsz - nb_p, s, d))
```

```python
import functools

import jax, jax.numpy as jnp
from jax import lax
import numpy as np
from jax.experimental import pallas as pl
from jax.experimental.pallas import tpu as pltpu

F32, BF16 = jnp.float32, jnp.bfloat16

D_MODEL = 2048
DEPTH = 2
MIX_HALF = D_MODEL // 2
HEAD_DIM = 128
EPS = 1e-6
NEG = -1e30

A_HEADS = MIX_HALF // HEAD_DIM
A_WIDTH = A_HEADS * HEAD_DIM
A_PATTERNS = ((128, 1), (512, 4), (2048, 16))
N_BUCKETS = 32
MAX_DISTANCE = 1024

B_HEADS = 4
B_DV = MIX_HALF // B_HEADS
B_DK = B_DV // 2
B_WIDTH = B_HEADS * B_DV
B_KEYW = B_HEADS * B_DK
B_GATE_RANK = 16
B_GATE_TAU = 16.0
B_CHUNK = 64

C_HEADS = MIX_HALF // 128
C_Q_RANK = 512
C_KV_RANK = 256
C_NOPE = 128
C_ROPE = 64
C_V = 128
C_WIDTH = C_HEADS * C_V
C_QK = 256
ROPE_THETA = 10000.0

D_HEAD = 64
D_HEADS = MIX_HALF // D_HEAD
D_WIDTH = D_HEADS * D_HEAD
D_W_RANK = 64
D_A_RANK = 64
D_G_RANK = 128
D_LN_EPS = 64e-5
D_SPLITS = (D_WIDTH, D_WIDTH, D_WIDTH, D_W_RANK, D_W_RANK, D_A_RANK, D_G_RANK)
D_SHIFT = 3 * D_WIDTH + 2 * D_W_RANK + D_A_RANK + D_G_RANK

N_GROUPS = 4
EXPERTS_PER_GROUP = 4
N_EXPERTS = N_GROUPS * EXPERTS_PER_GROUP

EVEN_IN = 3 * A_WIDTH + 2 * B_KEYW + 2 * B_WIDTH + 2 * B_GATE_RANK
C_IN = C_Q_RANK + C_KV_RANK + C_ROPE

LANE = 128
VMEM_LIMIT = 52 * 1024 * 1024


def _params(*sem):
    return pltpu.CompilerParams(dimension_semantics=sem, vmem_limit_bytes=VMEM_LIMIT)


def _round_up(n, m):
    return -(-n // m) * m


NT_DIMS = (((1,), (1,)), ((), ()))
TN_DIMS = (((0,), (0,)), ((), ()))


def _split3(x):
    hi = x.astype(BF16)
    r1 = x - hi.astype(F32)
    mid = r1.astype(BF16)
    lo = (r1 - mid.astype(F32)).astype(BF16)
    return hi, mid, lo


def _pick_tile(n, target):
    best = LANE
    for t in range(LANE, target + 1, LANE):
        if n % t == 0:
            best = t
    return best


def _norm_linear_kernel(x_ref, g_ref, w_ref, o_ref, xn_ref):
    @pl.when(pl.program_id(1) == 0)
    def _():
        x = x_ref[...]
        y = x * lax.rsqrt(jnp.mean(x * x, axis=-1, keepdims=True) + EPS) * g_ref[...]
        xn_ref[...] = y.astype(BF16)

    o_ref[...] = jnp.dot(xn_ref[...], w_ref[...], preferred_element_type=F32)


def norm_linear(x, g, w, *, tm=1024, tn_target=1024):
    t, k = x.shape
    n = w.shape[1]
    tn = _pick_tile(n, tn_target)
    return pl.pallas_call(
        _norm_linear_kernel,
        out_shape=jax.ShapeDtypeStruct((t, n), F32),
        grid=(t // tm, n // tn),
        in_specs=[pl.BlockSpec((tm, k), lambda i, j: (i, 0)),
                  pl.BlockSpec((1, k), lambda i, j: (0, 0)),
                  pl.BlockSpec((k, tn), lambda i, j: (0, j))],
        out_specs=pl.BlockSpec((tm, tn), lambda i, j: (i, j)),
        scratch_shapes=[pltpu.VMEM((tm, k), BF16)],
        compiler_params=_params("parallel", "arbitrary"),
        name="norm_linear",
    )(x, g.reshape(1, k), w)


def _out_proj_kernel(a_ref, b_ref, wa_ref, wb_ref, x_ref, o_ref):
    acc = jnp.dot(a_ref[...], wa_ref[...], preferred_element_type=F32)
    acc += jnp.dot(b_ref[...], wb_ref[...], preferred_element_type=F32)
    o_ref[...] = x_ref[...] + acc


def out_proj(a, b, w, x, *, tm=1024, tn=512):
    t, ka = a.shape
    kb = b.shape[1]
    n = w.shape[1]
    return pl.pallas_call(
        _out_proj_kernel,
        out_shape=jax.ShapeDtypeStruct((t, n), F32),
        grid=(t // tm, n // tn),
        in_specs=[pl.BlockSpec((tm, ka), lambda i, j: (i, 0)),
                  pl.BlockSpec((tm, kb), lambda i, j: (i, 0)),
                  pl.BlockSpec((ka, tn), lambda i, j: (0, j)),
                  pl.BlockSpec((kb, tn), lambda i, j: (0, j)),
                  pl.BlockSpec((tm, tn), lambda i, j: (i, j))],
        out_specs=pl.BlockSpec((tm, tn), lambda i, j: (i, j)),
        compiler_params=_params("parallel", "arbitrary"),
        name="out_proj",
    )(a, b, w[:ka], w[ka:], x)


def _final_norm_kernel(x_ref, g_ref, o_ref):
    x = x_ref[...]
    o_ref[...] = x * lax.rsqrt(jnp.mean(x * x, axis=-1, keepdims=True) + EPS) * g_ref[...]


def final_norm(x, g, *, row0, rows, tm=1024):
    d = x.shape[1]
    return pl.pallas_call(
        _final_norm_kernel,
        out_shape=jax.ShapeDtypeStruct((rows, d), F32),
        grid=(rows // tm,),
        in_specs=[pl.BlockSpec((tm, d), lambda i: (i + row0 // tm, 0)), pl.BlockSpec((1, d), lambda i: (0, 0))],
        out_specs=pl.BlockSpec((tm, d), lambda i: (i, 0)),
        compiler_params=_params("parallel"),
        name="final_norm",
    )(x, g.reshape(1, d))


ATTN_KB = 256
ATTN_TQ = 512


def _attn_kernel(*refs, scale, has_bias):
    if has_bias:
        q_ref, k_ref, v_ref, bias_ref, o_ref, kb_ref, vb_ref = refs
    else:
        q_ref, k_ref, v_ref, o_ref, kb_ref, vb_ref = refs
    s_len = k_ref.shape[0]

    @pl.when(pl.program_id(2) == 0)
    def _():
        kb_ref[...] = k_ref[...].astype(BF16)
        vb_ref[...] = v_ref[...].astype(BF16)

    q = (q_ref[...] * scale).astype(BF16)
    blocks = [slice(j * ATTN_KB, (j + 1) * ATTN_KB) for j in range(s_len // ATTN_KB)]
    scores = []
    m = None
    for blk in blocks:
        sj = lax.dot_general(q, kb_ref[blk, :], NT_DIMS, preferred_element_type=F32)
        if has_bias:
            sj = sj + bias_ref[:, blk]
        mj = jnp.max(sj, axis=-1, keepdims=True)
        m = mj if m is None else jnp.maximum(m, mj)
        scores.append(sj)
    o = den = None
    for blk, sj in zip(blocks, scores):
        p = jnp.exp(sj - m)
        dj = jnp.sum(p, axis=-1, keepdims=True)
        oj = jnp.dot(p.astype(BF16), vb_ref[blk, :], preferred_element_type=F32)
        o, den = (oj, dj) if o is None else (o + oj, den + dj)
    o_ref[...] = (o / den).astype(o_ref.dtype)


def attention(q, k, v, *, heads, dq, dv, q_off, k_off, v_off, scale, bias=None, tq=ATTN_TQ):
    b, s, _ = q.shape
    nq = s // tq
    in_specs = [pl.BlockSpec((None, tq, dq), lambda bi, h, qi: (bi, qi, q_off + h)),
                pl.BlockSpec((None, s, dq), lambda bi, h, qi: (bi, 0, k_off + h)),
                pl.BlockSpec((None, s, dv), lambda bi, h, qi: (bi, 0, v_off + h))]
    args = [q, k, v]
    if bias is not None:
        in_specs.append(pl.BlockSpec((None, None, tq, s), lambda bi, h, qi: (h, qi, 0, 0)))
        args.append(bias)
    return pl.pallas_call(
        functools.partial(_attn_kernel, scale=scale, has_bias=bias is not None),
        out_shape=jax.ShapeDtypeStruct((b, s, heads * dv), BF16),
        grid=(b, heads, nq),
        in_specs=in_specs,
        out_specs=pl.BlockSpec((None, tq, dv), lambda bi, h, qi: (bi, qi, h)),
        scratch_shapes=[pltpu.VMEM((s, dq), BF16), pltpu.VMEM((s, dv), BF16)],
        compiler_params=_params("parallel", "parallel", "arbitrary"),
        name="attention_bias" if bias is not None else "attention",
    )(*args)


def _t5_bucket(rel):
    half = N_BUCKETS // 2
    exact = half // 2
    n = np.abs(rel)
    large = exact + (np.log(np.maximum(n, 1) / exact) / np.log(MAX_DISTANCE / exact) * (half - exact)).astype(np.int64)
    large = np.minimum(large, half - 1)
    return ((rel > 0) * half + np.where(n < exact, n, large)).astype(np.int32)


def dilated_bias_table(rel_bias, s, tq):
    heads = rel_bias.shape[1]
    d = np.arange(-(s - 1), s)
    count = np.zeros(d.shape, np.float32)
    for window, dil in A_PATTERNS:
        count += ((d % dil == 0) & (np.abs(d) <= (window // (2 * dil)) * dil)).astype(np.float32)
    logc = np.where(count > 0, np.log(np.maximum(count, 1.0)), NEG).astype(np.float32)
    onehot = (_t5_bucket(d)[:, None] == np.arange(N_BUCKETS)[None, :]).astype(np.float32)
    line = jnp.transpose(jnp.dot(onehot, rel_bias.astype(F32), precision=lax.Precision.HIGHEST)) + logc[None]
    width = 2 * s
    line = jnp.pad(line, ((0, 0), (0, width - line.shape[1])))[:, None, :]
    nq = s // tq
    return pl.pallas_call(
        functools.partial(_skew_kernel, tq=tq, nq=nq),
        out_shape=jax.ShapeDtypeStruct((heads, nq, tq, s), F32),
        grid=(heads, nq),
        in_specs=[pl.BlockSpec((None, 1, width), lambda h, qi: (h, 0, 0))],
        out_specs=pl.BlockSpec((None, None, tq, s), lambda h, qi: (h, qi, 0, 0)),
        compiler_params=_params("parallel", "arbitrary"),
        name="bias_skew",
    )(line)


def _skew_kernel(line_ref, o_ref, *, tq, nq):
    width = line_ref.shape[1]
    first = (nq - 1 - pl.program_id(1)) * tq
    x = jnp.broadcast_to(line_ref[...], (tq, width))
    x = pltpu.roll(x, width - (tq - 1) - first, 1, stride=1, stride_axis=0)
    o_ref[...] = x[:, :o_ref.shape[1]]


GLA_UNROLL = 8


def _gla_kernel(q_ref, k_ref, v_ref, g_ref, z_ref, w2f_ref, w2b_ref, bf_ref, bb_ref, on_ref, o_ref,
                la_ref, acc_ref, qcat_ref, upd_ref, dec_ref, scat_ref, st_ref):
    s_len = q_ref.shape[0]
    c = B_CHUNK
    nchunk = s_len // c
    z = z_ref[...].astype(BF16)
    gate = lambda w2_ref, b_ref: jax.nn.log_sigmoid(
        jnp.dot(z, w2_ref[...], preferred_element_type=F32) + b_ref[...]) * (1.0 / B_GATE_TAU)
    la_ref[0] = gate(w2f_ref, bf_ref)
    la_ref[1] = gate(w2b_ref, bb_ref)

    ri = lax.broadcasted_iota(jnp.int32, (c, c), 0)
    ci = lax.broadcasted_iota(jnp.int32, (c, c), 1)
    keep = (ri >= ci, ri <= ci)
    tri3 = tuple(jnp.concatenate([kp.astype(BF16)] * 3, axis=1) for kp in keep)

    def chunk_rows(n):
        return pl.ds(pl.multiple_of(n * c, c), c)

    def pass1(i, carry):
        units = [(i * GLA_UNROLL + u, d) for u in range(GLA_UNROLL) for d in range(2)]
        st = []
        for n, d in units:
            rows = chunk_rows(n)
            gcum = jnp.dot(tri3[d], jnp.concatenate(_split3(la_ref[d, rows, :]), axis=0),
                           preferred_element_type=F32)
            st.append(dict(rows=rows, gcum=gcum))
        for (n, d), c in zip(units, st):
            gcum = c["gcum"]
            gend = gcum[0:1] if d == 1 else gcum[B_CHUNK - 1:B_CHUNK]
            kc = k_ref[c["rows"], :]
            c["q_in"] = (q_ref[c["rows"], :] * (B_DK ** -0.5) * jnp.exp(gcum)).astype(BF16)
            c["k_out"] = (kc * jnp.exp(gend - gcum)).astype(BF16)
            c["att"] = lax.dot_general(c["q_in"], (kc * jnp.exp(-gcum)).astype(BF16), NT_DIMS,
                                       preferred_element_type=F32)
            dec_ref[d, n] = jnp.broadcast_to(jnp.exp(gend), (8, B_DK))
        for (n, d), c in zip(units, st):
            vc = v_ref[c["rows"], :].astype(BF16)
            c["o"] = jnp.dot(jnp.where(keep[d], c["att"], 0.0).astype(BF16), vc, preferred_element_type=F32)
            upd_ref[d, n] = lax.dot_general(vc, c["k_out"], TN_DIMS, preferred_element_type=F32)
            qcat_ref[c["rows"], d * B_DK:(d + 1) * B_DK] = c["q_in"]
        for u in range(GLA_UNROLL):
            acc_ref[st[2 * u]["rows"], :] = st[2 * u]["o"] + st[2 * u + 1]["o"]
        return carry

    lax.fori_loop(0, nchunk // GLA_UNROLL, pass1, 0)

    st_ref[...] = jnp.zeros_like(st_ref)

    def pass2(n, carry):
        for d, m in ((0, n), (1, nchunk - 1 - n)):
            state = st_ref[d]
            scat_ref[m, :, d * B_DK:(d + 1) * B_DK] = state.astype(BF16)
            st_ref[d] = state * dec_ref[d, m][0:1] + upd_ref[d, m]
        return carry

    lax.fori_loop(0, nchunk, pass2, 0)

    def pass3(i, carry):
        rows = [chunk_rows(i * GLA_UNROLL + u) for u in range(GLA_UNROLL)]
        outs = [acc_ref[r, :] + lax.dot_general(qcat_ref[r, :], scat_ref[i * GLA_UNROLL + u], NT_DIMS,
                                                preferred_element_type=F32) for u, r in enumerate(rows)]
        for r, o in zip(rows, outs):
            o = o * lax.rsqrt(jnp.mean(o * o, axis=-1, keepdims=True) + EPS) * on_ref[...]
            g = g_ref[r, :]
            o_ref[r, :] = (o * (g * jax.nn.sigmoid(g))).astype(o_ref.dtype)
        return carry

    lax.fori_loop(0, nchunk // GLA_UNROLL, pass3, 0)


def gla_mixer(proj, w2f, w2b, b_f, b_b, onorm, *, q_col, k_col, v_col, g_col, z_col):
    b, s, _ = proj.shape
    hm = lambda blk: (lambda bi, h: (bi, 0, blk + h))
    w2f_p = jnp.zeros((LANE, B_KEYW), F32).at[:B_GATE_RANK].set(w2f).astype(BF16)
    w2b_p = jnp.zeros((LANE, B_KEYW), F32).at[B_GATE_RANK:2 * B_GATE_RANK].set(w2b).astype(BF16)
    return pl.pallas_call(
        _gla_kernel,
        out_shape=jax.ShapeDtypeStruct((b, s, B_WIDTH), BF16),
        grid=(b, B_HEADS),
        in_specs=[pl.BlockSpec((None, s, B_DK), hm(q_col // B_DK)),
                  pl.BlockSpec((None, s, B_DK), hm(k_col // B_DK)),
                  pl.BlockSpec((None, s, B_DV), hm(v_col // B_DV)),
                  pl.BlockSpec((None, s, B_DV), hm(g_col // B_DV)),
                  pl.BlockSpec((None, s, LANE), lambda bi, h: (bi, 0, z_col // LANE)),
                  pl.BlockSpec((LANE, B_DK), lambda bi, h: (0, h)),
                  pl.BlockSpec((LANE, B_DK), lambda bi, h: (0, h)),
                  pl.BlockSpec((1, B_DK), lambda bi, h: (0, h)),
                  pl.BlockSpec((1, B_DK), lambda bi, h: (0, h)),
                  pl.BlockSpec((1, B_DV), lambda bi, h: (0, 0))],
        out_specs=pl.BlockSpec((None, s, B_DV), lambda bi, h: (bi, 0, h)),
        scratch_shapes=[pltpu.VMEM((2, s, B_DK), F32),
                        pltpu.VMEM((s, B_DV), F32),
                        pltpu.VMEM((s, 2 * B_DK), BF16),
                        pltpu.VMEM((2, s // B_CHUNK, B_DV, B_DK), F32),
                        pltpu.VMEM((2, s // B_CHUNK, 8, B_DK), F32),
                        pltpu.VMEM((s // B_CHUNK, B_DV, 2 * B_DK), BF16),
                        pltpu.VMEM((2, B_DV, B_DK), F32)],
        compiler_params=_params("parallel", "arbitrary"),
        name="gla_mixer",
    )(proj, proj, proj, proj, proj, w2f_p, w2b_p, b_f.reshape(1, -1), b_b.reshape(1, -1), onorm.reshape(1, -1))


def _mla_up_kernel(cq_ref, ckv_ref, kr_ref, qn_ref, kvn_ref, wq_ref, wqr_ref, wkv_ref, cos_ref, sin_ref,
                   q_ref, k_ref, v_ref):
    def rms(x, g):
        return (x * lax.rsqrt(jnp.mean(x * x, axis=-1, keepdims=True) + EPS) * g).astype(BF16)

    cq = rms(cq_ref[...], qn_ref[...])
    ckv = rms(ckv_ref[...], kvn_ref[...])
    cos, sin = cos_ref[...], sin_ref[...]
    kr = kr_ref[...]
    k_rope = kr * cos + pltpu.roll(kr, LANE - C_ROPE, 1) * sin
    lane = lax.broadcasted_iota(jnp.int32, k_rope.shape, 1)
    k_rope = jnp.where(lane < C_ROPE, k_rope, 0.0)
    for h in range(C_HEADS):
        q = jnp.dot(cq, wq_ref[:, h * C_QK:(h + 1) * C_QK], preferred_element_type=F32)
        qp = jnp.dot(cq, wqr_ref[:, h * LANE:(h + 1) * LANE], preferred_element_type=F32)
        q_ref[:, h * C_QK:h * C_QK + C_NOPE] = q[:, :C_NOPE]
        q_ref[:, h * C_QK + C_NOPE:(h + 1) * C_QK] = q[:, C_NOPE:] * cos + qp * sin
        kv = jnp.dot(ckv, wkv_ref[:, h * 2 * LANE:(h + 1) * 2 * LANE], preferred_element_type=F32)
        k_ref[:, h * C_QK:h * C_QK + C_NOPE] = kv[:, :C_NOPE]
        k_ref[:, h * C_QK + C_NOPE:(h + 1) * C_QK] = k_rope
        v_ref[:, h * C_V:(h + 1) * C_V] = kv[:, C_NOPE:]


def _rot_half_cols(w):
    half = w.shape[-1] // 2
    return jnp.concatenate([-w[..., half:], w[..., :half]], axis=-1)


def mla_up(proj, q_norm, w_uq, kv_norm, w_ukv, cos, sin, *, col0, tm=512):
    t = proj.shape[0]
    wq = w_uq.reshape(C_Q_RANK, C_HEADS, C_NOPE + C_ROPE)
    wq_main = jnp.pad(wq, ((0, 0), (0, 0), (0, C_QK - C_NOPE - C_ROPE))).reshape(C_Q_RANK, C_HEADS * C_QK)
    wq_rot = jnp.pad(_rot_half_cols(wq[..., C_NOPE:]), ((0, 0), (0, 0), (0, LANE - C_ROPE)))
    wq_rot = wq_rot.reshape(C_Q_RANK, C_HEADS * LANE)
    row = lambda i: (i, 0)
    full = lambda arr: pl.BlockSpec(arr.shape, lambda i: (0, 0))
    g_q, g_kv = q_norm.reshape(1, -1), kv_norm.reshape(1, -1)
    wq_main, wq_rot, wkv = wq_main.astype(BF16), wq_rot.astype(BF16), w_ukv.astype(BF16)
    return pl.pallas_call(
        _mla_up_kernel,
        out_shape=(jax.ShapeDtypeStruct((t, C_HEADS * C_QK), F32),
                   jax.ShapeDtypeStruct((t, C_HEADS * C_QK), F32),
                   jax.ShapeDtypeStruct((t, C_WIDTH), F32)),
        grid=(t // tm,),
        in_specs=[pl.BlockSpec((tm, C_Q_RANK), lambda i: (i, col0 // C_Q_RANK)),
                  pl.BlockSpec((tm, C_KV_RANK), lambda i: (i, (col0 + C_Q_RANK) // C_KV_RANK)),
                  pl.BlockSpec((tm, LANE), lambda i: (i, (col0 + C_Q_RANK + C_KV_RANK) // LANE)),
                  full(g_q), full(g_kv), full(wq_main), full(wq_rot), full(wkv),
                  pl.BlockSpec((tm, LANE), row), pl.BlockSpec((tm, LANE), row)],
        out_specs=(pl.BlockSpec((tm, C_HEADS * C_QK), row),
                   pl.BlockSpec((tm, C_HEADS * C_QK), row),
                   pl.BlockSpec((tm, C_WIDTH), row)),
        compiler_params=_params("parallel"),
        name="mla_up",
    )(proj, proj, proj, g_q, g_kv, wq_main, wq_rot, wkv, cos, sin)


RG = 4
RGW = RG * D_HEAD
RCH = 64
DC_R, DC_K, DC_V = 0, D_WIDTH, 2 * D_WIDTH
DC_ZG = 3 * D_WIDTH
DC_ZW = DC_ZG + D_G_RANK
DC_ZA = DC_ZW + 2 * D_W_RANK
DC_PAD = 7 * 512


def _head_sums(x, bo):
    return jnp.concatenate(
        [jnp.dot(x[:, RGW * g:RGW * (g + 1)], bo, preferred_element_type=F32, precision=lax.Precision.HIGHEST)
         for g in range(x.shape[1] // RGW)], axis=1)


def _block_ones():
    i = np.arange(RGW)
    return jnp.asarray((i[:, None] // D_HEAD) == (i[None, :] // D_HEAD), F32)


def _rwkv_prep_kernel(x_ref, xp_ref, xn_ref, mu_ref, w2f_ref, w2b_ref, a2_ref, g2_ref, w0f_ref, w0b_ref,
                      a0_ref, kk_ref, ka_ref, rk_ref, bo_ref,
                      r_ref, k_ref, v_ref, a_ref, b_ref, lwf_ref, lwb_ref, g_ref, bonus_ref, *, tiles_per_seq):
    i = pl.program_id(0) % tiles_per_seq
    x = x_ref[...]
    tm = x.shape[0]
    row = lax.broadcasted_iota(jnp.int32, x.shape, 0)
    prev_row = jnp.where(i == 0, 0.0, xp_ref[7:8, :])
    next_row = jnp.where(i == tiles_per_seq - 1, 0.0, xn_ref[0:1, :])
    prev = jnp.where(row == 0, prev_row, pltpu.roll(x, 1, 0))
    nxt = jnp.where(row == tm - 1, next_row, pltpu.roll(x, tm - 1, 0))
    x = x + mu_ref[...] * (0.5 * (prev + nxt) - x)
    r, k, v = x[:, DC_R:DC_R + D_WIDTH], x[:, DC_K:DC_K + D_WIDTH], x[:, DC_V:DC_V + D_WIDTH]
    zg = x[:, DC_ZG:DC_ZG + LANE]
    zw = x[:, DC_ZW:DC_ZW + LANE]
    za = x[:, DC_ZA:DC_ZA + LANE]
    tz = jnp.tanh(zw).astype(BF16)
    log_decay = lambda w0_ref, w2_ref: -np.exp(-0.5).astype(np.float32) * jax.nn.sigmoid(
        w0_ref[...] + jnp.dot(tz, w2_ref[...], preferred_element_type=F32))
    lwf_ref[...] = log_decay(w0f_ref, w2f_ref)
    lwb_ref[...] = log_decay(w0b_ref, w2b_ref)
    ag = jax.nn.sigmoid(a0_ref[...] + jnp.dot(za.astype(BF16), a2_ref[...], preferred_element_type=F32))
    g_ref[...] = jnp.dot(jax.nn.sigmoid(zg).astype(BF16), g2_ref[...], preferred_element_type=F32)
    bo = bo_ref[...]
    kk = k * kk_ref[...]
    kk = kk / jnp.maximum(jnp.sqrt(_head_sums(kk * kk, bo)), 1e-12)
    k = k * (1.0 + (ag - 1.0) * ka_ref[...])
    r_ref[...] = r
    k_ref[...] = k
    v_ref[...] = v
    a_ref[...] = -kk
    b_ref[...] = kk * ag
    bonus_ref[...] = _head_sums(r * k * rk_ref[...], bo) * v


def rwkv_prep(proj, mu, w0_f, w2_f, w0_b, w2_b, a0, a2, g2, k_k, k_a, r_k, *, seq, tm=256):
    t = proj.shape[0]
    tiles_per_seq = seq // tm
    hb = tm // 8
    nblk8 = t // 8
    pad_rows = lambda w, lo: jnp.zeros((LANE, D_WIDTH), F32).at[lo:lo + w.shape[0]].set(w).astype(BF16)
    vec = lambda u: u.reshape(1, -1)
    consts = [vec(mu), pad_rows(w2_f, 0), pad_rows(w2_b, D_W_RANK), pad_rows(a2, 0), g2.astype(BF16),
              vec(w0_f), vec(w0_b), vec(a0), vec(k_k), vec(k_a), vec(r_k), _block_ones()]
    full = lambda arr: pl.BlockSpec(arr.shape, lambda i: (0, 0))
    out_spec = pl.BlockSpec((tm, D_WIDTH), lambda i: (i, 0))
    return pl.pallas_call(
        functools.partial(_rwkv_prep_kernel, tiles_per_seq=tiles_per_seq),
        out_shape=tuple(jax.ShapeDtypeStruct((t, D_WIDTH), F32) for _ in range(9)),
        grid=(t // tm,),
        in_specs=[pl.BlockSpec((tm, DC_PAD), lambda i: (i, 0)),
                  pl.BlockSpec((8, DC_PAD), lambda i: (jnp.maximum(i * hb - 1, 0), 0)),
                  pl.BlockSpec((8, DC_PAD), lambda i: (jnp.minimum((i + 1) * hb, nblk8 - 1), 0))]
                 + [full(c) for c in consts],
        out_specs=tuple(out_spec for _ in range(9)),
        compiler_params=_params("parallel"),
        name="rwkv_prep",
    )(proj, proj, proj, *consts)


def _rwkv_chunk_kernel(*refs, ngroups):
    ins, (yf_ref, yb_ref, mt_ref) = refs[:12], refs[12:]

    @pl.when(pl.program_id(1) == 0)
    def _():
        mt_ref[...] = jnp.zeros_like(mt_ref)

    row = lax.broadcasted_iota(jnp.int32, (RCH, RGW), 0)
    col = lax.broadcasted_iota(jnp.int32, (RCH, RGW), 1) & (RCH - 1)
    bdmask = (lax.broadcasted_iota(jnp.int32, (RGW, RGW), 0) // D_HEAD
              == lax.broadcasted_iota(jnp.int32, (RGW, RGW), 1) // D_HEAD)
    tr = lax.broadcasted_iota(jnp.int32, (RCH, RCH), 0)
    tc = lax.broadcasted_iota(jnp.int32, (RCH, RCH), 1)
    zero = jnp.zeros((), F32)

    def bd(z):
        zb = z.astype(BF16)
        return jnp.where(bdmask, jnp.concatenate([zb] * RG, axis=0), jnp.zeros((), BF16))

    def mm(x, y, dims=None):
        x = x.astype(BF16)
        if dims is None:
            return jnp.dot(x, y, preferred_element_type=F32)
        return lax.dot_general(x, y, dims, preferred_element_type=F32)

    chains = [(d, g) for d in range(2) for g in range(ngroups)]
    st = []
    for d, g in chains:
        backward = d == 1
        r_ref, k_ref, v_ref, a_ref, b_ref, lw_ref = ins[6 * d:6 * d + 6]
        tri = ((tc >= tr) if backward else (tc <= tr)).astype(BF16)
        sl = slice(RGW * g, RGW * (g + 1))
        r, k, v, a, b, lw = (ref[:, sl] for ref in (r_ref, k_ref, v_ref, a_ref, b_ref, lw_ref))
        lam = jnp.dot(jnp.concatenate([tri] * 3, axis=1), jnp.concatenate(_split3(lw), axis=0),
                      preferred_element_type=F32)
        lamc = lam[0:1] if backward else lam[RCH - 1:RCH]
        e_inv = jnp.exp(-lam)
        e_out = jnp.exp(lamc - lam)
        ar = jnp.concatenate([a * jnp.exp(lam - lw), r * jnp.exp(lam)], axis=0).astype(BF16)
        bk = jnp.concatenate([b * e_out, k * e_out], axis=0).astype(BF16)
        st.append(dict(ar=ar, bk=bk, v=v, lamc=lamc, sl=sl,
                       gb=mm(ar, bd(b * e_inv), NT_DIMS), gk=mm(ar, bd(k * e_inv), NT_DIMS)))
    for (d, g), c in zip(chains, st):
        strict = (col > row) if d == 1 else (col < row)
        incl = (col >= row) if d == 1 else (col <= row)
        c["lp"] = jnp.where(strict, c["gb"][:RCH], zero)
        lak = jnp.where(strict, c["gk"][:RCH], zero)
        c["grb"] = jnp.where(incl, c["gb"][RCH:], zero).astype(BF16)
        c["grk"] = jnp.where(incl, c["gk"][RCH:], zero).astype(BF16)
        c["mt"] = mt_ref[d, g]
        amrm = mm(c["ar"], c["mt"].astype(BF16), NT_DIMS)
        c["bdv"] = bd(c["v"])
        c["u"] = amrm[:RCH] + mm(lak, c["bdv"])
        c["rm"] = amrm[RCH:]
    for rnd in range(6):
        for c in st:
            lpb = c["lp"].astype(BF16)
            c["u"] = c["u"] + mm(lpb, bd(c["u"]))
            if rnd < 5:
                c["lp"] = mm(lpb, bd(c["lp"]))
    for (d, g), c in zip(chains, st):
        y_ref = yb_ref if d == 1 else yf_ref
        y_ref[:, c["sl"]] = c["rm"] + mm(c["grb"], bd(c["u"])) + mm(c["grk"], c["bdv"])
        uv = jnp.concatenate([c["u"], c["v"]], axis=0).astype(BF16)
        upd = lax.dot_general(uv, c["bk"], TN_DIMS, preferred_element_type=F32)
        mt_ref[d, g] = c["mt"] * jnp.exp(c["lamc"]) + jnp.where(bdmask, upd, zero)


def rwkv_chunked(r, k, v, a, b, lwf, lwb):
    bsz, s, wd = r.shape
    nc = s // RCH
    fspec = pl.BlockSpec((None, RCH, wd), lambda bi, n: (bi, n, 0))
    bspec = pl.BlockSpec((None, RCH, wd), lambda bi, n: (bi, nc - 1 - n, 0))
    return pl.pallas_call(
        functools.partial(_rwkv_chunk_kernel, ngroups=wd // RGW),
        out_shape=(jax.ShapeDtypeStruct((bsz, s, wd), F32), jax.ShapeDtypeStruct((bsz, s, wd), F32)),
        grid=(bsz, nc),
        in_specs=[fspec] * 6 + [bspec] * 6,
        out_specs=(fspec, bspec),
        scratch_shapes=[pltpu.VMEM((2, wd // RGW, RGW, RGW), F32)],
        compiler_params=_params("parallel", "arbitrary"),
        name="rwkv_chunked",
    )(r, k, v, a, b, lwf, r, k, v, a, b, lwb)


def _rwkv_post_kernel(yf_ref, yb_ref, bonus_ref, g_ref, lng_ref, lnb_ref, bo_ref, o_ref):
    bo = bo_ref[...]
    y = yf_ref[...] + yb_ref[...]
    yc = y - _head_sums(y, bo) * (1.0 / D_HEAD)
    var = _head_sums(yc * yc, bo) * (1.0 / D_HEAD)
    y = yc * lax.rsqrt(var + D_LN_EPS) * lng_ref[...] + lnb_ref[...]
    o_ref[...] = ((y + bonus_ref[...]) * g_ref[...]).astype(o_ref.dtype)


def rwkv_post(yf, yb, bonus, g, ln_g, ln_b, *, tm=512):
    t, wd = yf.shape
    row = pl.BlockSpec((tm, wd), lambda i: (i, 0))
    vec = pl.BlockSpec((1, wd), lambda i: (0, 0))
    bo = _block_ones()
    return pl.pallas_call(
        _rwkv_post_kernel,
        out_shape=jax.ShapeDtypeStruct((t, wd), BF16),
        grid=(t // tm,),
        in_specs=[row, row, row, row, vec, vec, pl.BlockSpec(bo.shape, lambda i: (0, 0))],
        out_specs=row,
        compiler_params=_params("parallel"),
        name="rwkv_post",
    )(yf, yb, bonus, g, ln_g.reshape(1, wd), ln_b.reshape(1, wd), bo)


def rwkv7_mixer(proj, bsz, s, mu, w0_f, w2_f, w0_b, w2_b, a0, a2, g2, k_k, k_a, r_k, ln_g, ln_b):
    r, k, v, a, b, lwf, lwb, g, bonus = rwkv_prep(proj, mu, w0_f, w2_f, w0_b, w2_b, a0, a2, g2, k_k, k_a,
                                                  r_k.reshape(-1), seq=s)
    r3 = lambda u: u.reshape(bsz, s, D_WIDTH)
    yf, yb = rwkv_chunked(r3(r), r3(k), r3(v), r3(a), r3(b), r3(lwf), r3(lwb))
    return rwkv_post(yf.reshape(-1, D_WIDTH), yb.reshape(-1, D_WIDTH), bonus, g, ln_g, ln_b)


MOE_TILE = 1024
MOE_SUB = 256
ROW_ALIGN = 16
MOE_PACK = 1


def _route(logit):
    lane = lax.broadcasted_iota(jnp.int32, logit.shape, 1)
    first_at = lambda mask: jnp.min(jnp.where(mask, lane, jnp.int32(LANE)), axis=-1, keepdims=True)
    is_grp = lane < N_GROUPS
    gl = jnp.where(is_grp, logit, NEG)
    gmax = jnp.max(gl, axis=-1, keepdims=True)
    p_grp = 1.0 / jnp.sum(jnp.where(is_grp, jnp.exp(gl - gmax), 0.0), axis=-1, keepdims=True)
    i_grp = first_at(is_grp & (gl == gmax))
    lo = N_GROUPS + i_grp * EXPERTS_PER_GROUP
    in_grp = (lane >= lo) & (lane < lo + EXPERTS_PER_GROUP)
    el = jnp.where(in_grp, logit, NEG)
    l1 = jnp.max(el, axis=-1, keepdims=True)
    i1 = first_at(in_grp & (el == l1))
    rest = in_grp & (lane != i1)
    el2 = jnp.where(rest, logit, NEG)
    l2 = jnp.max(el2, axis=-1, keepdims=True)
    i2 = first_at(rest & (el2 == l2))
    e2 = jnp.exp(l2 - l1)
    w1 = p_grp / (1.0 + e2)
    w2 = p_grp * e2 / (1.0 + e2)
    return i_grp, jnp.where(lane == i1, w1, jnp.where(lane == i2, w2, 0.0))


def _moe_sort_kernel(x_ref, g_ref, wr_ref, br_ref, hn_ref, comb_ref, pos_ref, off_ref):
    x = x_ref[...]
    tm = x.shape[0]
    hn = (x * lax.rsqrt(jnp.mean(x * x, axis=-1, keepdims=True) + EPS) * g_ref[...]).astype(BF16)
    wr, br = wr_ref[...], br_ref[...]
    i_grp, _ = _route(jnp.dot(hn, wr, preferred_element_type=F32) + br)
    lane = lax.broadcasted_iota(jnp.int32, (tm, LANE), 1)
    onehot = (lane == i_grp).astype(F32)
    ri = lax.broadcasted_iota(jnp.int32, (tm, tm), 0)
    ci = lax.broadcasted_iota(jnp.int32, (tm, tm), 1)
    earlier = jnp.dot((ci < ri).astype(BF16), onehot.astype(BF16), preferred_element_type=F32)
    cnt = jnp.broadcast_to(jnp.sum(onehot, axis=0, keepdims=True), (8, LANE))
    lane8 = lax.broadcasted_iota(jnp.int32, (8, LANE), 1)
    start = jnp.zeros((8, LANE), F32)
    for sh in range(1, N_GROUPS):
        start += jnp.where(lane8 >= sh, pltpu.roll(cnt, sh, 1), 0.0)
    off_ref[...] = start[0:1].astype(jnp.int32)
    pos = jnp.sum(onehot * (start[0:1] + earlier), axis=-1, keepdims=True).astype(jnp.int32)
    pos_ref[...] = pos
    perm_t = (ci == pos).astype(BF16)
    hn_s = lax.dot_general(perm_t, hn, TN_DIMS, preferred_element_type=F32).astype(BF16)
    hn_ref[...] = hn_s
    _, comb = _route(jnp.dot(hn_s, wr, preferred_element_type=F32) + br)
    comb_ref[...] = comb


def moe_sort(x, g, w_grp, b_grp, w_exp, b_exp):
    t, d = x.shape
    tm = MOE_TILE
    nr = N_GROUPS + N_EXPERTS
    wr = jnp.pad(jnp.concatenate([w_grp, w_exp], axis=1), ((0, 0), (0, LANE - nr))).astype(BF16)
    br = jnp.pad(jnp.concatenate([b_grp, b_exp]), (0, LANE - nr)).reshape(1, LANE)
    row = lambda i: (i, 0)
    hn, comb, pos, off = pl.pallas_call(
        _moe_sort_kernel,
        out_shape=(jax.ShapeDtypeStruct((t, d), BF16), jax.ShapeDtypeStruct((t, LANE), F32),
                   jax.ShapeDtypeStruct((t, 1), jnp.int32), jax.ShapeDtypeStruct((t // tm, 1, LANE), jnp.int32)),
        grid=(t // tm,),
        in_specs=[pl.BlockSpec((tm, d), row), pl.BlockSpec((1, d), lambda i: (0, 0)),
                  pl.BlockSpec((d, LANE), lambda i: (0, 0)), pl.BlockSpec((1, LANE), lambda i: (0, 0))],
        out_specs=(pl.BlockSpec((tm, d), row), pl.BlockSpec((tm, LANE), row), pl.BlockSpec((tm, 1), row),
                   pl.BlockSpec((None, 1, LANE), lambda i: (i, 0, 0))),
        compiler_params=_params("parallel"),
        name="moe_sort",
    )(x, g.reshape(1, d), wr, br)
    bounds = jnp.concatenate([off[:, 0, :N_GROUPS], jnp.full((t // tm, 1), tm, jnp.int32)], axis=1)
    return hn, comb, pos, bounds.reshape(-1)


def _moe_group_kernel(bounds_ref, hn_ref, c_ref, wg_ref, wu_ref, wd_ref, y_ref):
    i, g, j = pl.program_id(0), pl.program_id(1), pl.program_id(2)

    @pl.when((g == 0) & (j == 0))
    def _():
        y_ref[...] = jnp.zeros_like(y_ref)

    expert_lane = N_GROUPS + g * EXPERTS_PER_GROUP + j
    lane = lax.broadcasted_iota(jnp.int32, (MOE_SUB, LANE), 1)
    row = lax.broadcasted_iota(jnp.int32, (MOE_SUB, 1), 0)
    for half in range(MOE_PACK):
        base = (i * MOE_PACK + half) * (N_GROUPS + 1) + g
        lo = half * MOE_TILE
        start, end = lo + bounds_ref[base], lo + bounds_ref[base + 1]
        first = (start // ROW_ALIGN) * ROW_ALIGN

        def sub(k, carry, first=first, lo=lo):
            want = first + k * MOE_SUB
            r0 = pl.multiple_of(jnp.minimum(want, lo + MOE_TILE - MOE_SUB), ROW_ALIGN)
            rows = pl.ds(r0, MOE_SUB)
            x = hn_ref[rows, :]
            hg = jnp.dot(x, wg_ref[...], preferred_element_type=F32)
            hu = jnp.dot(x, wu_ref[...], preferred_element_type=F32)
            c = jnp.sum(jnp.where(lane == expert_lane, c_ref[rows, :], 0.0), axis=-1, keepdims=True)
            c = jnp.where(row + r0 >= want, c, 0.0)
            hid = (hg * jax.nn.sigmoid(hg)) * hu * c
            y_ref[rows, :] += jnp.dot(hid.astype(BF16), wd_ref[...], preferred_element_type=F32)
            return carry

        lax.fori_loop(0, (end - first + MOE_SUB - 1) // MOE_SUB, sub, 0)


def moe_group_experts(hn, comb, bounds, w_gate, w_up, w_down):
    t, d = hn.shape
    tm = MOE_TILE * MOE_PACK
    ne, _, ff = w_gate.shape
    ex = lambda i, g, j, b: (g * EXPERTS_PER_GROUP + j, 0, 0)
    once = pl.Buffered(1) if MOE_PACK > 1 else None
    return pl.pallas_call(
        _moe_group_kernel,
        out_shape=jax.ShapeDtypeStruct((t, d), F32),
        grid_spec=pltpu.PrefetchScalarGridSpec(
            num_scalar_prefetch=1,
            grid=(t // tm, N_GROUPS, EXPERTS_PER_GROUP),
            in_specs=[pl.BlockSpec((tm, d), lambda i, g, j, b: (i, 0), pipeline_mode=once),
                      pl.BlockSpec((tm, LANE), lambda i, g, j, b: (i, 0)),
                      pl.BlockSpec((None, d, ff), ex), pl.BlockSpec((None, d, ff), ex),
                      pl.BlockSpec((None, ff, d), ex)],
            out_specs=pl.BlockSpec((tm, d), lambda i, g, j, b: (i, 0), pipeline_mode=once)),
        compiler_params=_params("parallel", "arbitrary", "arbitrary"),
        name="moe_group_experts",
    )(bounds, hn, comb, w_gate, w_up, w_down)


def _moe_unsort_kernel(y_ref, pos_ref, x_ref, o_ref):
    tm = y_ref.shape[0]
    perm_t = (lax.broadcasted_iota(jnp.int32, (tm, tm), 1) == pos_ref[...]).astype(BF16)
    y = y_ref[...]
    hi = y.astype(BF16)
    lo = (y - hi.astype(F32)).astype(BF16)
    o_ref[...] = (x_ref[...] + jnp.dot(perm_t, hi, preferred_element_type=F32)
                  + jnp.dot(perm_t, lo, preferred_element_type=F32))


def moe_unsort(y, pos, x):
    t, d = x.shape
    tm, tn = MOE_TILE, d // 2
    blk = pl.BlockSpec((tm, tn), lambda i, j: (i, j))
    return pl.pallas_call(
        _moe_unsort_kernel,
        out_shape=jax.ShapeDtypeStruct((t, d), F32),
        grid=(t // tm, d // tn),
        in_specs=[blk, pl.BlockSpec((tm, 1), lambda i, j: (i, 0)), blk],
        out_specs=blk,
        compiler_params=_params("parallel", "arbitrary"),
        name="moe_unsort",
    )(y, pos, x)


def hier_moe(x, g, w_grp, b_grp, w_exp, b_exp, w_gate, w_up, w_down):
    hn, comb, pos, bounds = moe_sort(x, g, w_grp, b_grp, w_exp, b_exp)
    y = moe_group_experts(hn, comb, bounds, w_gate.astype(BF16), w_up.astype(BF16), w_down.astype(BF16))
    return moe_unsort(y, pos, x)


def even_layer(x, bsz, s, norm_g, rel_bias, w_in, w_out, w2_f, b_f, w2_b, b_b, onorm):
    n_pad = _round_up(EVEN_IN, LANE)
    w_in_p = jnp.pad(w_in, ((0, 0), (0, n_pad - EVEN_IN))).astype(BF16)
    proj = norm_linear(x, norm_g, w_in_p, tn_target=896).reshape(bsz, s, n_pad)
    tq = ATTN_TQ
    ya = attention(proj, proj, proj, heads=A_HEADS, dq=HEAD_DIM, dv=HEAD_DIM,
                   q_off=0, k_off=A_HEADS, v_off=2 * A_HEADS, scale=HEAD_DIM ** -0.5,
                   bias=dilated_bias_table(rel_bias, s, tq), tq=tq)
    q_col = 3 * A_WIDTH
    yb = gla_mixer(proj, w2_f, w2_b, b_f, b_b, onorm, q_col=q_col, k_col=q_col + B_KEYW,
                   v_col=q_col + 2 * B_KEYW, g_col=q_col + 2 * B_KEYW + B_WIDTH,
                   z_col=q_col + 2 * B_KEYW + 2 * B_WIDTH)
    t = bsz * s
    return out_proj(ya.reshape(t, A_WIDTH), yb.reshape(t, B_WIDTH), w_out.astype(BF16), x)


def _odd_columns(w_in, mu):
    c0 = C_IN
    cut = lambda u, lo, n: u[..., lo:lo + n]
    zpad = lambda u, n: jnp.pad(u, [(0, 0)] * (u.ndim - 1) + [(0, n)])
    off = np.cumsum((0,) + D_SPLITS)
    def rwkv_cols(u):
        parts = [cut(u, off[0], 3 * D_WIDTH), cut(u, off[6], D_G_RANK), cut(u, off[3], 2 * D_W_RANK),
                 cut(u, off[5], D_A_RANK)]
        u = jnp.concatenate(parts, axis=-1)
        return zpad(u, DC_PAD - u.shape[-1])
    w_kr = w_in[:, C_Q_RANK + C_KV_RANK:C_IN]
    w_all = jnp.concatenate([rwkv_cols(w_in[:, c0:]), w_in[:, :C_IN], _rot_half_cols(w_kr)], axis=1)
    n_pad = _round_up(w_all.shape[1], 9 * LANE)
    return zpad(w_all, n_pad - w_all.shape[1]).astype(BF16), rwkv_cols(mu)


def odd_layer(x, bsz, s, norm_g, w_in, w_out, q_norm, w_uq, kv_norm, w_ukv, mu, w0_f, w2_f, w0_b, w2_b,
              a0, a2, g2, k_k, k_a, r_k, ln_g, ln_b):
    t = bsz * s
    w_all, mu_cols = _odd_columns(w_in, mu)
    proj = norm_linear(x, norm_g, w_all, tn_target=1152)
    inv = 1.0 / (ROPE_THETA ** (jnp.arange(0, C_ROPE, 2, dtype=F32) / C_ROPE))
    ang = jnp.arange(s, dtype=F32)[:, None] * inv[None, :]
    cos = jnp.pad(jnp.concatenate([jnp.cos(ang)] * 2, axis=1), ((0, 0), (0, LANE - C_ROPE)), constant_values=1.0)
    sin = jnp.pad(jnp.concatenate([jnp.sin(ang)] * 2, axis=1), ((0, 0), (0, LANE - C_ROPE)))
    q, k, v = mla_up(proj, q_norm, w_uq, kv_norm, w_ukv, jnp.tile(cos, (bsz, 1)), jnp.tile(sin, (bsz, 1)),
                     col0=DC_PAD)
    r3 = lambda u: u.reshape(bsz, s, -1)
    yc = attention(r3(q), r3(k), r3(v), heads=C_HEADS, dq=C_QK, dv=C_V, q_off=0, k_off=0, v_off=0,
                   scale=(C_NOPE + C_ROPE) ** -0.5)
    yd = rwkv7_mixer(proj, bsz, s, mu_cols, w0_f, w2_f, w0_b, w2_b, a0, a2, g2, k_k, k_a, r_k, ln_g, ln_b)
    return out_proj(yc.reshape(t, C_WIDTH), yd, w_out.astype(BF16), x)


def kernel(x_prompt, x_sample, rel_bias, norm_mix, norm_ffn, norm_final, ev_w_in, ev_w_out, ev_gla_w2_f, ev_gla_b_f, ev_gla_w2_b, ev_gla_b_b, ev_gla_onorm, od_w_in, od_w_out, od_q_norm, od_w_uq, od_kv_norm, od_w_ukv, od_mu, od_w0_f, od_w2_f, od_w0_b, od_w2_b, od_a0, od_a2, od_g2, od_k_k, od_k_a, od_r_k, od_ln_g, od_ln_b, moe_w_grp, moe_b_grp, moe_w_exp, moe_b_exp, moe_w_gate, moe_w_up, moe_w_down):
    nb_p = x_prompt.shape[0]
    x = jnp.concatenate([x_prompt, x_sample], axis=0)
    bsz, s, d = x.shape
    x = x.reshape(bsz * s, d)
    for i in range(DEPTH):
        j = i // 2
        if i % 2 == 0:
            x = even_layer(x, bsz, s, norm_mix[i], rel_bias, ev_w_in[j], ev_w_out[j], ev_gla_w2_f[j],
                           ev_gla_b_f[j], ev_gla_w2_b[j], ev_gla_b_b[j], ev_gla_onorm[j])
        else:
            x = odd_layer(x, bsz, s, norm_mix[i], od_w_in[j], od_w_out[j], od_q_norm[j], od_w_uq[j],
                          od_kv_norm[j], od_w_ukv[j], od_mu[j], od_w0_f[j], od_w2_f[j], od_w0_b[j],
                          od_w2_b[j], od_a0[j], od_a2[j], od_g2[j], od_k_k[j], od_k_a[j], od_r_k[j],
                          od_ln_g[j], od_ln_b[j])
        x = hier_moe(x, norm_ffn[i], moe_w_grp[i], moe_b_grp[i], moe_w_exp[i], moe_b_exp[i],
                     moe_w_gate[i], moe_w_up[i], moe_w_down[i])
    y_p = final_norm(x, norm_final, row0=0, rows=nb_p * s)
    y_s = final_norm(x, norm_final, row0=nb_p * s, rows=(bsz - nb_p) * s)
    return (y_p.reshape(nb_p, s, d), y_s.reshape(bsz - nb_p, s, d))
```

```python
import functools

import jax, jax.numpy as jnp
from jax import lax
import numpy as np
from jax.experimental import pallas as pl
from jax.experimental.pallas import tpu as pltpu

F32, BF16 = jnp.float32, jnp.bfloat16

D_MODEL = 2048
DEPTH = 2
MIX_HALF = D_MODEL // 2
HEAD_DIM = 128
EPS = 1e-6
NEG = -1e30

A_HEADS = MIX_HALF // HEAD_DIM
A_WIDTH = A_HEADS * HEAD_DIM
A_PATTERNS = ((128, 1), (512, 4), (2048, 16))
N_BUCKETS = 32
MAX_DISTANCE = 1024

B_HEADS = 4
B_DV = MIX_HALF // B_HEADS
B_DK = B_DV // 2
B_WIDTH = B_HEADS * B_DV
B_KEYW = B_HEADS * B_DK
B_GATE_RANK = 16
B_GATE_TAU = 16.0
B_CHUNK = 64

C_HEADS = MIX_HALF // 128
C_Q_RANK = 512
C_KV_RANK = 256
C_NOPE = 128
C_ROPE = 64
C_V = 128
C_WIDTH = C_HEADS * C_V
C_QK = 256
ROPE_THETA = 10000.0

D_HEAD = 64
D_HEADS = MIX_HALF // D_HEAD
D_WIDTH = D_HEADS * D_HEAD
D_W_RANK = 64
D_A_RANK = 64
D_G_RANK = 128
D_LN_EPS = 64e-5
D_SPLITS = (D_WIDTH, D_WIDTH, D_WIDTH, D_W_RANK, D_W_RANK, D_A_RANK, D_G_RANK)
D_SHIFT = 3 * D_WIDTH + 2 * D_W_RANK + D_A_RANK + D_G_RANK

N_GROUPS = 4
EXPERTS_PER_GROUP = 4
N_EXPERTS = N_GROUPS * EXPERTS_PER_GROUP

EVEN_IN = 3 * A_WIDTH + 2 * B_KEYW + 2 * B_WIDTH + 2 * B_GATE_RANK
C_IN = C_Q_RANK + C_KV_RANK + C_ROPE

LANE = 128
VMEM_LIMIT = 52 * 1024 * 1024


def _params(*sem):
    return pltpu.CompilerParams(dimension_semantics=sem, vmem_limit_bytes=VMEM_LIMIT)


def _round_up(n, m):
    return -(-n // m) * m


NT_DIMS = (((1,), (1,)), ((), ()))
TN_DIMS = (((0,), (0,)), ((), ()))


def _split3(x):
    hi = x.astype(BF16)
    r1 = x - hi.astype(F32)
    mid = r1.astype(BF16)
    lo = (r1 - mid.astype(F32)).astype(BF16)
    return hi, mid, lo


def _pick_tile(n, target):
    best = LANE
    for t in range(LANE, target + 1, LANE):
        if n % t == 0:
            best = t
    return best


def _norm_linear_kernel(x_ref, g_ref, w_ref, o_ref, xn_ref):
    @pl.when(pl.program_id(1) == 0)
    def _():
        x = x_ref[...]
        y = x * lax.rsqrt(jnp.mean(x * x, axis=-1, keepdims=True) + EPS) * g_ref[...]
        xn_ref[...] = y.astype(BF16)

    o_ref[...] = jnp.dot(xn_ref[...], w_ref[...], preferred_element_type=F32)


def norm_linear(x, g, w, *, tm=1024, tn_target=1024):
    t, k = x.shape
    n = w.shape[1]
    tn = _pick_tile(n, tn_target)
    return pl.pallas_call(
        _norm_linear_kernel,
        out_shape=jax.ShapeDtypeStruct((t, n), F32),
        grid=(t // tm, n // tn),
        in_specs=[pl.BlockSpec((tm, k), lambda i, j: (i, 0)),
                  pl.BlockSpec((1, k), lambda i, j: (0, 0)),
                  pl.BlockSpec((k, tn), lambda i, j: (0, j))],
        out_specs=pl.BlockSpec((tm, tn), lambda i, j: (i, j)),
        scratch_shapes=[pltpu.VMEM((tm, k), BF16)],
        compiler_params=_params("parallel", "arbitrary"),
        name="norm_linear",
    )(x, g.reshape(1, k), w)


def _out_proj_kernel(a_ref, b_ref, wa_ref, wb_ref, x_ref, o_ref):
    acc = jnp.dot(a_ref[...], wa_ref[...], preferred_element_type=F32)
    acc += jnp.dot(b_ref[...], wb_ref[...], preferred_element_type=F32)
    o_ref[...] = x_ref[...] + acc


def out_proj(a, b, w, x, *, tm=1024, tn=512):
    t, ka = a.shape
    kb = b.shape[1]
    n = w.shape[1]
    return pl.pallas_call(
        _out_proj_kernel,
        out_shape=jax.ShapeDtypeStruct((t, n), F32),
        grid=(t // tm, n // tn),
        in_specs=[pl.BlockSpec((tm, ka), lambda i, j: (i, 0)),
                  pl.BlockSpec((tm, kb), lambda i, j: (i, 0)),
                  pl.BlockSpec((ka, tn), lambda i, j: (0, j)),
                  pl.BlockSpec((kb, tn), lambda i, j: (0, j)),
                  pl.BlockSpec((tm, tn), lambda i, j: (i, j))],
        out_specs=pl.BlockSpec((tm, tn), lambda i, j: (i, j)),
        compiler_params=_params("parallel", "arbitrary"),
        name="out_proj",
    )(a, b, w[:ka], w[ka:], x)


ATTN_KB = 256
ATTN_TQ = 512


def _attn_kernel(*refs, scale, has_bias):
    if has_bias:
        q_ref, k_ref, v_ref, bias_ref, o_ref, kb_ref, vb_ref = refs
    else:
        q_ref, k_ref, v_ref, o_ref, kb_ref, vb_ref = refs
    s_len = k_ref.shape[0]

    @pl.when(pl.program_id(2) == 0)
    def _():
        kb_ref[...] = k_ref[...].astype(BF16)
        vb_ref[...] = v_ref[...].astype(BF16)

    q = (q_ref[...] * scale).astype(BF16)
    blocks = [slice(j * ATTN_KB, (j + 1) * ATTN_KB) for j in range(s_len // ATTN_KB)]
    scores = []
    m = None
    for blk in blocks:
        sj = lax.dot_general(q, kb_ref[blk, :], NT_DIMS, preferred_element_type=F32)
        if has_bias:
            sj = sj + bias_ref[:, blk]
        mj = jnp.max(sj, axis=-1, keepdims=True)
        m = mj if m is None else jnp.maximum(m, mj)
        scores.append(sj)
    o = den = None
    for blk, sj in zip(blocks, scores):
        p = jnp.exp(sj - m)
        dj = jnp.sum(p, axis=-1, keepdims=True)
        oj = jnp.dot(p.astype(BF16), vb_ref[blk, :], preferred_element_type=F32)
        o, den = (oj, dj) if o is None else (o + oj, den + dj)
    o_ref[...] = (o / den).astype(o_ref.dtype)


def attention(q, k, v, *, heads, dq, dv, q_off, k_off, v_off, scale, bias=None, tq=ATTN_TQ):
    b, s, _ = q.shape
    nq = s // tq
    in_specs = [pl.BlockSpec((None, tq, dq), lambda bi, h, qi: (bi, qi, q_off + h)),
                pl.BlockSpec((None, s, dq), lambda bi, h, qi: (bi, 0, k_off + h)),
                pl.BlockSpec((None, s, dv), lambda bi, h, qi: (bi, 0, v_off + h))]
    args = [q, k, v]
    if bias is not None:
        in_specs.append(pl.BlockSpec((None, None, tq, s), lambda bi, h, qi: (h, qi, 0, 0)))
        args.append(bias)
    return pl.pallas_call(
        functools.partial(_attn_kernel, scale=scale, has_bias=bias is not None),
        out_shape=jax.ShapeDtypeStruct((b, s, heads * dv), BF16),
        grid=(b, heads, nq),
        in_specs=in_specs,
        out_specs=pl.BlockSpec((None, tq, dv), lambda bi, h, qi: (bi, qi, h)),
        scratch_shapes=[pltpu.VMEM((s, dq), BF16), pltpu.VMEM((s, dv), BF16)],
        compiler_params=_params("parallel", "parallel", "arbitrary"),
        name="attention_bias" if bias is not None else "attention",
    )(*args)


def _t5_bucket(rel):
    half = N_BUCKETS // 2
    exact = half // 2
    n = np.abs(rel)
    large = exact + (np.log(np.maximum(n, 1) / exact) / np.log(MAX_DISTANCE / exact) * (half - exact)).astype(np.int64)
    large = np.minimum(large, half - 1)
    return ((rel > 0) * half + np.where(n < exact, n, large)).astype(np.int32)


def dilated_bias_table(rel_bias, s, tq):
    heads = rel_bias.shape[1]
    d = np.arange(-(s - 1), s)
    count = np.zeros(d.shape, np.float32)
    for window, dil in A_PATTERNS:
        count += ((d % dil == 0) & (np.abs(d) <= (window // (2 * dil)) * dil)).astype(np.float32)
    logc = np.where(count > 0, np.log(np.maximum(count, 1.0)), NEG).astype(np.float32)
    onehot = (_t5_bucket(d)[:, None] == np.arange(N_BUCKETS)[None, :]).astype(np.float32)
    line = jnp.transpose(jnp.dot(onehot, rel_bias.astype(F32), precision=lax.Precision.HIGHEST)) + logc[None]
    width = 2 * s
    line = jnp.pad(line, ((0, 0), (0, width - line.shape[1])))[:, None, :]
    nq = s // tq
    return pl.pallas_call(
        functools.partial(_skew_kernel, tq=tq, nq=nq),
        out_shape=jax.ShapeDtypeStruct((heads, nq, tq, s), F32),
        grid=(heads, nq),
        in_specs=[pl.BlockSpec((None, 1, width), lambda h, qi: (h, 0, 0))],
        out_specs=pl.BlockSpec((None, None, tq, s), lambda h, qi: (h, qi, 0, 0)),
        compiler_params=_params("parallel", "arbitrary"),
        name="bias_skew",
    )(line)


def _skew_kernel(line_ref, o_ref, *, tq, nq):
    width = line_ref.shape[1]
    first = (nq - 1 - pl.program_id(1)) * tq
    x = jnp.broadcast_to(line_ref[...], (tq, width))
    x = pltpu.roll(x, width - (tq - 1) - first, 1, stride=1, stride_axis=0)
    o_ref[...] = x[:, :o_ref.shape[1]]


GLA_UNROLL = 8


def _gla_kernel(q_ref, k_ref, v_ref, g_ref, z_ref, w2f_ref, w2b_ref, bf_ref, bb_ref, on_ref, o_ref,
                la_ref, acc_ref, qcat_ref, upd_ref, dec_ref, scat_ref, st_ref):
    s_len = q_ref.shape[0]
    c = B_CHUNK
    nchunk = s_len // c
    z = z_ref[...].astype(BF16)
    gate = lambda w2_ref, b_ref: jax.nn.log_sigmoid(
        jnp.dot(z, w2_ref[...], preferred_element_type=F32) + b_ref[...]) * (1.0 / B_GATE_TAU)
    la_ref[0] = gate(w2f_ref, bf_ref)
    la_ref[1] = gate(w2b_ref, bb_ref)

    ri = lax.broadcasted_iota(jnp.int32, (c, c), 0)
    ci = lax.broadcasted_iota(jnp.int32, (c, c), 1)
    keep = (ri >= ci, ri <= ci)
    tri3 = tuple(jnp.concatenate([kp.astype(BF16)] * 3, axis=1) for kp in keep)

    def chunk_rows(n):
        return pl.ds(pl.multiple_of(n * c, c), c)

    def pass1(i, carry):
        units = [(i * GLA_UNROLL + u, d) for u in range(GLA_UNROLL) for d in range(2)]
        st = []
        for n, d in units:
            rows = chunk_rows(n)
            gcum = jnp.dot(tri3[d], jnp.concatenate(_split3(la_ref[d, rows, :]), axis=0),
                           preferred_element_type=F32)
            st.append(dict(rows=rows, gcum=gcum))
        for (n, d), c in zip(units, st):
            gcum = c["gcum"]
            gend = gcum[0:1] if d == 1 else gcum[B_CHUNK - 1:B_CHUNK]
            kc = k_ref[c["rows"], :]
            c["q_in"] = (q_ref[c["rows"], :] * (B_DK ** -0.5) * jnp.exp(gcum)).astype(BF16)
            c["k_out"] = (kc * jnp.exp(gend - gcum)).astype(BF16)
            c["att"] = lax.dot_general(c["q_in"], (kc * jnp.exp(-gcum)).astype(BF16), NT_DIMS,
                                       preferred_element_type=F32)
            dec_ref[d, n] = jnp.broadcast_to(jnp.exp(gend), (8, B_DK))
        for (n, d), c in zip(units, st):
            vc = v_ref[c["rows"], :].astype(BF16)
            c["o"] = jnp.dot(jnp.where(keep[d], c["att"], 0.0).astype(BF16), vc, preferred_element_type=F32)
            upd_ref[d, n] = lax.dot_general(vc, c["k_out"], TN_DIMS, preferred_element_type=F32)
            qcat_ref[c["rows"], d * B_DK:(d + 1) * B_DK] = c["q_in"]
        for u in range(GLA_UNROLL):
            acc_ref[st[2 * u]["rows"], :] = st[2 * u]["o"] + st[2 * u + 1]["o"]
        return carry

    lax.fori_loop(0, nchunk // GLA_UNROLL, pass1, 0)

    st_ref[...] = jnp.zeros_like(st_ref)

    def pass2(n, carry):
        for d, m in ((0, n), (1, nchunk - 1 - n)):
            state = st_ref[d]
            scat_ref[m, :, d * B_DK:(d + 1) * B_DK] = state.astype(BF16)
            st_ref[d] = state * dec_ref[d, m][0:1] + upd_ref[d, m]
        return carry

    lax.fori_loop(0, nchunk, pass2, 0)

    def pass3(i, carry):
        rows = [chunk_rows(i * GLA_UNROLL + u) for u in range(GLA_UNROLL)]
        outs = [acc_ref[r, :] + lax.dot_general(qcat_ref[r, :], scat_ref[i * GLA_UNROLL + u], NT_DIMS,
                                                preferred_element_type=F32) for u, r in enumerate(rows)]
        for r, o in zip(rows, outs):
            o = o * lax.rsqrt(jnp.mean(o * o, axis=-1, keepdims=True) + EPS) * on_ref[...]
            g = g_ref[r, :]
            o_ref[r, :] = (o * (g * jax.nn.sigmoid(g))).astype(o_ref.dtype)
        return carry

    lax.fori_loop(0, nchunk // GLA_UNROLL, pass3, 0)


def gla_mixer(proj, w2f, w2b, b_f, b_b, onorm, *, q_col, k_col, v_col, g_col, z_col):
    b, s, _ = proj.shape
    hm = lambda blk: (lambda bi, h: (bi, 0, blk + h))
    w2f_p = jnp.zeros((LANE, B_KEYW), F32).at[:B_GATE_RANK].set(w2f).astype(BF16)
    w2b_p = jnp.zeros((LANE, B_KEYW), F32).at[B_GATE_RANK:2 * B_GATE_RANK].set(w2b).astype(BF16)
    return pl.pallas_call(
        _gla_kernel,
        out_shape=jax.ShapeDtypeStruct((b, s, B_WIDTH), BF16),
        grid=(b, B_HEADS),
        in_specs=[pl.BlockSpec((None, s, B_DK), hm(q_col // B_DK)),
                  pl.BlockSpec((None, s, B_DK), hm(k_col // B_DK)),
                  pl.BlockSpec((None, s, B_DV), hm(v_col // B_DV)),
                  pl.BlockSpec((None, s, B_DV), hm(g_col // B_DV)),
                  pl.BlockSpec((None, s, LANE), lambda bi, h: (bi, 0, z_col // LANE)),
                  pl.BlockSpec((LANE, B_DK), lambda bi, h: (0, h)),
                  pl.BlockSpec((LANE, B_DK), lambda bi, h: (0, h)),
                  pl.BlockSpec((1, B_DK), lambda bi, h: (0, h)),
                  pl.BlockSpec((1, B_DK), lambda bi, h: (0, h)),
                  pl.BlockSpec((1, B_DV), lambda bi, h: (0, 0))],
        out_specs=pl.BlockSpec((None, s, B_DV), lambda bi, h: (bi, 0, h)),
        scratch_shapes=[pltpu.VMEM((2, s, B_DK), F32),
                        pltpu.VMEM((s, B_DV), F32),
                        pltpu.VMEM((s, 2 * B_DK), BF16),
                        pltpu.VMEM((2, s // B_CHUNK, B_DV, B_DK), F32),
                        pltpu.VMEM((2, s // B_CHUNK, 8, B_DK), F32),
                        pltpu.VMEM((s // B_CHUNK, B_DV, 2 * B_DK), BF16),
                        pltpu.VMEM((2, B_DV, B_DK), F32)],
        compiler_params=_params("parallel", "arbitrary"),
        name="gla_mixer",
    )(proj, proj, proj, proj, proj, w2f_p, w2b_p, b_f.reshape(1, -1), b_b.reshape(1, -1), onorm.reshape(1, -1))


def _mla_up_kernel(cq_ref, ckv_ref, kr_ref, qn_ref, kvn_ref, wq_ref, wqr_ref, wkv_ref, cos_ref, sin_ref,
                   q_ref, k_ref, v_ref):
    def rms(x, g):
        return (x * lax.rsqrt(jnp.mean(x * x, axis=-1, keepdims=True) + EPS) * g).astype(BF16)

    cq = rms(cq_ref[...], qn_ref[...])
    ckv = rms(ckv_ref[...], kvn_ref[...])
    cos, sin = cos_ref[...], sin_ref[...]
    kr = kr_ref[...]
    k_rope = kr * cos + pltpu.roll(kr, LANE - C_ROPE, 1) * sin
    lane = lax.broadcasted_iota(jnp.int32, k_rope.shape, 1)
    k_rope = jnp.where(lane < C_ROPE, k_rope, 0.0)
    for h in range(C_HEADS):
        q = jnp.dot(cq, wq_ref[:, h * C_QK:(h + 1) * C_QK], preferred_element_type=F32)
        qp = jnp.dot(cq, wqr_ref[:, h * LANE:(h + 1) * LANE], preferred_element_type=F32)
        q_ref[:, h * C_QK:h * C_QK + C_NOPE] = q[:, :C_NOPE]
        q_ref[:, h * C_QK + C_NOPE:(h + 1) * C_QK] = q[:, C_NOPE:] * cos + qp * sin
        kv = jnp.dot(ckv, wkv_ref[:, h * 2 * LANE:(h + 1) * 2 * LANE], preferred_element_type=F32)
        k_ref[:, h * C_QK:h * C_QK + C_NOPE] = kv[:, :C_NOPE]
        k_ref[:, h * C_QK + C_NOPE:(h + 1) * C_QK] = k_rope
        v_ref[:, h * C_V:(h + 1) * C_V] = kv[:, C_NOPE:]


def _rot_half_cols(w):
    half = w.shape[-1] // 2
    return jnp.concatenate([-w[..., half:], w[..., :half]], axis=-1)


def mla_up(proj, q_norm, w_uq, kv_norm, w_ukv, cos, sin, *, col0, tm=512):
    t = proj.shape[0]
    wq = w_uq.reshape(C_Q_RANK, C_HEADS, C_NOPE + C_ROPE)
    wq_main = jnp.pad(wq, ((0, 0), (0, 0), (0, C_QK - C_NOPE - C_ROPE))).reshape(C_Q_RANK, C_HEADS * C_QK)
    wq_rot = jnp.pad(_rot_half_cols(wq[..., C_NOPE:]), ((0, 0), (0, 0), (0, LANE - C_ROPE)))
    wq_rot = wq_rot.reshape(C_Q_RANK, C_HEADS * LANE)
    row = lambda i: (i, 0)
    full = lambda arr: pl.BlockSpec(arr.shape, lambda i: (0, 0))
    g_q, g_kv = q_norm.reshape(1, -1), kv_norm.reshape(1, -1)
    wq_main, wq_rot, wkv = wq_main.astype(BF16), wq_rot.astype(BF16), w_ukv.astype(BF16)
    return pl.pallas_call(
        _mla_up_kernel,
        out_shape=(jax.ShapeDtypeStruct((t, C_HEADS * C_QK), F32),
                   jax.ShapeDtypeStruct((t, C_HEADS * C_QK), F32),
                   jax.ShapeDtypeStruct((t, C_WIDTH), F32)),
        grid=(t // tm,),
        in_specs=[pl.BlockSpec((tm, C_Q_RANK), lambda i: (i, col0 // C_Q_RANK)),
                  pl.BlockSpec((tm, C_KV_RANK), lambda i: (i, (col0 + C_Q_RANK) // C_KV_RANK)),
                  pl.BlockSpec((tm, LANE), lambda i: (i, (col0 + C_Q_RANK + C_KV_RANK) // LANE)),
                  full(g_q), full(g_kv), full(wq_main), full(wq_rot), full(wkv),
                  pl.BlockSpec((tm, LANE), row), pl.BlockSpec((tm, LANE), row)],
        out_specs=(pl.BlockSpec((tm, C_HEADS * C_QK), row),
                   pl.BlockSpec((tm, C_HEADS * C_QK), row),
                   pl.BlockSpec((tm, C_WIDTH), row)),
        compiler_params=_params("parallel"),
        name="mla_up",
    )(proj, proj, proj, g_q, g_kv, wq_main, wq_rot, wkv, cos, sin)


RG = 4
RGW = RG * D_HEAD
RCH = 64
DC_R, DC_K, DC_V = 0, D_WIDTH, 2 * D_WIDTH
DC_ZG = 3 * D_WIDTH
DC_ZW = DC_ZG + D_G_RANK
DC_ZA = DC_ZW + 2 * D_W_RANK
DC_PAD = 7 * 512


def _head_sums(x, bo):
    return jnp.concatenate(
        [jnp.dot(x[:, RGW * g:RGW * (g + 1)], bo, preferred_element_type=F32, precision=lax.Precision.HIGHEST)
         for g in range(x.shape[1] // RGW)], axis=1)


def _block_ones():
    i = np.arange(RGW)
    return jnp.asarray((i[:, None] // D_HEAD) == (i[None, :] // D_HEAD), F32)


def _rwkv_prep_kernel(x_ref, xp_ref, xn_ref, mu_ref, w2f_ref, w2b_ref, a2_ref, g2_ref, w0f_ref, w0b_ref,
                      a0_ref, kk_ref, ka_ref, rk_ref, bo_ref,
                      r_ref, k_ref, v_ref, a_ref, b_ref, lwf_ref, lwb_ref, g_ref, bonus_ref, *, tiles_per_seq):
    i = pl.program_id(0) % tiles_per_seq
    x = x_ref[...]
    tm = x.shape[0]
    row = lax.broadcasted_iota(jnp.int32, x.shape, 0)
    prev_row = jnp.where(i == 0, 0.0, xp_ref[7:8, :])
    next_row = jnp.where(i == tiles_per_seq - 1, 0.0, xn_ref[0:1, :])
    prev = jnp.where(row == 0, prev_row, pltpu.roll(x, 1, 0))
    nxt = jnp.where(row == tm - 1, next_row, pltpu.roll(x, tm - 1, 0))
    x = x + mu_ref[...] * (0.5 * (prev + nxt) - x)
    r, k, v = x[:, DC_R:DC_R + D_WIDTH], x[:, DC_K:DC_K + D_WIDTH], x[:, DC_V:DC_V + D_WIDTH]
    zg = x[:, DC_ZG:DC_ZG + LANE]
    zw = x[:, DC_ZW:DC_ZW + LANE]
    za = x[:, DC_ZA:DC_ZA + LANE]
    tz = jnp.tanh(zw).astype(BF16)
    log_decay = lambda w0_ref, w2_ref: -np.exp(-0.5).astype(np.float32) * jax.nn.sigmoid(
        w0_ref[...] + jnp.dot(tz, w2_ref[...], preferred_element_type=F32))
    lwf_ref[...] = log_decay(w0f_ref, w2f_ref)
    lwb_ref[...] = log_decay(w0b_ref, w2b_ref)
    ag = jax.nn.sigmoid(a0_ref[...] + jnp.dot(za.astype(BF16), a2_ref[...], preferred_element_type=F32))
    g_ref[...] = jnp.dot(jax.nn.sigmoid(zg).astype(BF16), g2_ref[...], preferred_element_type=F32)
    bo = bo_ref[...]
    kk = k * kk_ref[...]
    kk = kk / jnp.maximum(jnp.sqrt(_head_sums(kk * kk, bo)), 1e-12)
    k = k * (1.0 + (ag - 1.0) * ka_ref[...])
    r_ref[...] = r
    k_ref[...] = k
    v_ref[...] = v
    a_ref[...] = -kk
    b_ref[...] = kk * ag
    bonus_ref[...] = _head_sums(r * k * rk_ref[...], bo) * v


def rwkv_prep(proj, mu, w0_f, w2_f, w0_b, w2_b, a0, a2, g2, k_k, k_a, r_k, *, seq, tm=256):
    t = proj.shape[0]
    tiles_per_seq = seq // tm
    hb = tm // 8
    nblk8 = t // 8
    pad_rows = lambda w, lo: jnp.zeros((LANE, D_WIDTH), F32).at[lo:lo + w.shape[0]].set(w).astype(BF16)
    vec = lambda u: u.reshape(1, -1)
    consts = [vec(mu), pad_rows(w2_f, 0), pad_rows(w2_b, D_W_RANK), pad_rows(a2, 0), g2.astype(BF16),
              vec(w0_f), vec(w0_b), vec(a0), vec(k_k), vec(k_a), vec(r_k), _block_ones()]
    full = lambda arr: pl.BlockSpec(arr.shape, lambda i: (0, 0))
    out_spec = pl.BlockSpec((tm, D_WIDTH), lambda i: (i, 0))
    return pl.pallas_call(
        functools.partial(_rwkv_prep_kernel, tiles_per_seq=tiles_per_seq),
        out_shape=tuple(jax.ShapeDtypeStruct((t, D_WIDTH), F32) for _ in range(9)),
        grid=(t // tm,),
        in_specs=[pl.BlockSpec((tm, DC_PAD), lambda i: (i, 0)),
                  pl.BlockSpec((8, DC_PAD), lambda i: (jnp.maximum(i * hb - 1, 0), 0)),
                  pl.BlockSpec((8, DC_PAD), lambda i: (jnp.minimum((i + 1) * hb, nblk8 - 1), 0))]
                 + [full(c) for c in consts],
        out_specs=tuple(out_spec for _ in range(9)),
        compiler_params=_params("parallel"),
        name="rwkv_prep",
    )(proj, proj, proj, *consts)


def _rwkv_chunk_kernel(*refs, ngroups):
    ins, (yf_ref, yb_ref, mt_ref) = refs[:12], refs[12:]

    @pl.when(pl.program_id(1) == 0)
    def _():
        mt_ref[...] = jnp.zeros_like(mt_ref)

    row = lax.broadcasted_iota(jnp.int32, (RCH, RGW), 0)
    col = lax.broadcasted_iota(jnp.int32, (RCH, RGW), 1) & (RCH - 1)
    bdmask = (lax.broadcasted_iota(jnp.int32, (RGW, RGW), 0) // D_HEAD
              == lax.broadcasted_iota(jnp.int32, (RGW, RGW), 1) // D_HEAD)
    tr = lax.broadcasted_iota(jnp.int32, (RCH, RCH), 0)
    tc = lax.broadcasted_iota(jnp.int32, (RCH, RCH), 1)
    zero = jnp.zeros((), F32)

    def bd(z):
        zb = z.astype(BF16)
        return jnp.where(bdmask, jnp.concatenate([zb] * RG, axis=0), jnp.zeros((), BF16))

    def mm(x, y, dims=None):
        x = x.astype(BF16)
        if dims is None:
            return jnp.dot(x, y, preferred_element_type=F32)
        return lax.dot_general(x, y, dims, preferred_element_type=F32)

    chains = [(d, g) for d in range(2) for g in range(ngroups)]
    st = []
    for d, g in chains:
        backward = d == 1
        r_ref, k_ref, v_ref, a_ref, b_ref, lw_ref = ins[6 * d:6 * d + 6]
        tri = ((tc >= tr) if backward else (tc <= tr)).astype(BF16)
        sl = slice(RGW * g, RGW * (g + 1))
        r, k, v, a, b, lw = (ref[:, sl] for ref in (r_ref, k_ref, v_ref, a_ref, b_ref, lw_ref))
        lam = jnp.dot(jnp.concatenate([tri] * 3, axis=1), jnp.concatenate(_split3(lw), axis=0),
                      preferred_element_type=F32)
        lamc = lam[0:1] if backward else lam[RCH - 1:RCH]
        e_inv = jnp.exp(-lam)
        e_out = jnp.exp(lamc - lam)
        ar = jnp.concatenate([a * jnp.exp(lam - lw), r * jnp.exp(lam)], axis=0).astype(BF16)
        bk = jnp.concatenate([b * e_out, k * e_out], axis=0).astype(BF16)
        st.append(dict(ar=ar, bk=bk, v=v, lamc=lamc, sl=sl,
                       gb=mm(ar, bd(b * e_inv), NT_DIMS), gk=mm(ar, bd(k * e_inv), NT_DIMS)))
    for (d, g), c in zip(chains, st):
        strict = (col > row) if d == 1 else (col < row)
        incl = (col >= row) if d == 1 else (col <= row)
        c["lp"] = jnp.where(strict, c["gb"][:RCH], zero)
        lak = jnp.where(strict, c["gk"][:RCH], zero)
        c["grb"] = jnp.where(incl, c["gb"][RCH:], zero).astype(BF16)
        c["grk"] = jnp.where(incl, c["gk"][RCH:], zero).astype(BF16)
        c["mt"] = mt_ref[d, g]
        amrm = mm(c["ar"], c["mt"].astype(BF16), NT_DIMS)
        c["bdv"] = bd(c["v"])
        c["u"] = amrm[:RCH] + mm(lak, c["bdv"])
        c["rm"] = amrm[RCH:]
    for rnd in range(6):
        for c in st:
            lpb = c["lp"].astype(BF16)
            c["u"] = c["u"] + mm(lpb, bd(c["u"]))
            if rnd < 5:
                c["lp"] = mm(lpb, bd(c["lp"]))
    for (d, g), c in zip(chains, st):
        y_ref = yb_ref if d == 1 else yf_ref
        y_ref[:, c["sl"]] = c["rm"] + mm(c["grb"], bd(c["u"])) + mm(c["grk"], c["bdv"])
        uv = jnp.concatenate([c["u"], c["v"]], axis=0).astype(BF16)
        upd = lax.dot_general(uv, c["bk"], TN_DIMS, preferred_element_type=F32)
        mt_ref[d, g] = c["mt"] * jnp.exp(c["lamc"]) + jnp.where(bdmask, upd, zero)


def rwkv_chunked(r, k, v, a, b, lwf, lwb):
    bsz, s, wd = r.shape
    nc = s // RCH
    fspec = pl.BlockSpec((None, RCH, wd), lambda bi, n: (bi, n, 0))
    bspec = pl.BlockSpec((None, RCH, wd), lambda bi, n: (bi, nc - 1 - n, 0))
    return pl.pallas_call(
        functools.partial(_rwkv_chunk_kernel, ngroups=wd // RGW),
        out_shape=(jax.ShapeDtypeStruct((bsz, s, wd), F32), jax.ShapeDtypeStruct((bsz, s, wd), F32)),
        grid=(bsz, nc),
        in_specs=[fspec] * 6 + [bspec] * 6,
        out_specs=(fspec, bspec),
        scratch_shapes=[pltpu.VMEM((2, wd // RGW, RGW, RGW), F32)],
        compiler_params=_params("parallel", "arbitrary"),
        name="rwkv_chunked",
    )(r, k, v, a, b, lwf, r, k, v, a, b, lwb)


def _rwkv_post_kernel(yf_ref, yb_ref, bonus_ref, g_ref, lng_ref, lnb_ref, bo_ref, o_ref):
    bo = bo_ref[...]
    y = yf_ref[...] + yb_ref[...]
    yc = y - _head_sums(y, bo) * (1.0 / D_HEAD)
    var = _head_sums(yc * yc, bo) * (1.0 / D_HEAD)
    y = yc * lax.rsqrt(var + D_LN_EPS) * lng_ref[...] + lnb_ref[...]
    o_ref[...] = ((y + bonus_ref[...]) * g_ref[...]).astype(o_ref.dtype)


def rwkv_post(yf, yb, bonus, g, ln_g, ln_b, *, tm=512):
    t, wd = yf.shape
    row = pl.BlockSpec((tm, wd), lambda i: (i, 0))
    vec = pl.BlockSpec((1, wd), lambda i: (0, 0))
    bo = _block_ones()
    return pl.pallas_call(
        _rwkv_post_kernel,
        out_shape=jax.ShapeDtypeStruct((t, wd), BF16),
        grid=(t // tm,),
        in_specs=[row, row, row, row, vec, vec, pl.BlockSpec(bo.shape, lambda i: (0, 0))],
        out_specs=row,
        compiler_params=_params("parallel"),
        name="rwkv_post",
    )(yf, yb, bonus, g, ln_g.reshape(1, wd), ln_b.reshape(1, wd), bo)


def rwkv7_mixer(proj, bsz, s, mu, w0_f, w2_f, w0_b, w2_b, a0, a2, g2, k_k, k_a, r_k, ln_g, ln_b):
    r, k, v, a, b, lwf, lwb, g, bonus = rwkv_prep(proj, mu, w0_f, w2_f, w0_b, w2_b, a0, a2, g2, k_k, k_a,
                                                  r_k.reshape(-1), seq=s)
    r3 = lambda u: u.reshape(bsz, s, D_WIDTH)
    yf, yb = rwkv_chunked(r3(r), r3(k), r3(v), r3(a), r3(b), r3(lwf), r3(lwb))
    return rwkv_post(yf.reshape(-1, D_WIDTH), yb.reshape(-1, D_WIDTH), bonus, g, ln_g, ln_b)


MOE_TILE = 1024
MOE_SUB = 256
ROW_ALIGN = 16
MOE_PACK = 1


def _route(logit):
    lane = lax.broadcasted_iota(jnp.int32, logit.shape, 1)
    first_at = lambda mask: jnp.min(jnp.where(mask, lane, jnp.int32(LANE)), axis=-1, keepdims=True)
    is_grp = lane < N_GROUPS
    gl = jnp.where(is_grp, logit, NEG)
    gmax = jnp.max(gl, axis=-1, keepdims=True)
    p_grp = 1.0 / jnp.sum(jnp.where(is_grp, jnp.exp(gl - gmax), 0.0), axis=-1, keepdims=True)
    i_grp = first_at(is_grp & (gl == gmax))
    lo = N_GROUPS + i_grp * EXPERTS_PER_GROUP
    in_grp = (lane >= lo) & (lane < lo + EXPERTS_PER_GROUP)
    el = jnp.where(in_grp, logit, NEG)
    l1 = jnp.max(el, axis=-1, keepdims=True)
    i1 = first_at(in_grp & (el == l1))
    rest = in_grp & (lane != i1)
    el2 = jnp.where(rest, logit, NEG)
    l2 = jnp.max(el2, axis=-1, keepdims=True)
    i2 = first_at(rest & (el2 == l2))
    e2 = jnp.exp(l2 - l1)
    w1 = p_grp / (1.0 + e2)
    w2 = p_grp * e2 / (1.0 + e2)
    return i_grp, jnp.where(lane == i1, w1, jnp.where(lane == i2, w2, 0.0))


def _moe_sort_kernel(x_ref, g_ref, wr_ref, br_ref, hn_ref, comb_ref, pos_ref, off_ref):
    x = x_ref[...]
    tm = x.shape[0]
    hn = (x * lax.rsqrt(jnp.mean(x * x, axis=-1, keepdims=True) + EPS) * g_ref[...]).astype(BF16)
    wr, br = wr_ref[...], br_ref[...]
    i_grp, _ = _route(jnp.dot(hn, wr, preferred_element_type=F32) + br)
    lane = lax.broadcasted_iota(jnp.int32, (tm, LANE), 1)
    onehot = (lane == i_grp).astype(F32)
    ri = lax.broadcasted_iota(jnp.int32, (tm, tm), 0)
    ci = lax.broadcasted_iota(jnp.int32, (tm, tm), 1)
    earlier = jnp.dot((ci < ri).astype(BF16), onehot.astype(BF16), preferred_element_type=F32)
    cnt = jnp.broadcast_to(jnp.sum(onehot, axis=0, keepdims=True), (8, LANE))
    lane8 = lax.broadcasted_iota(jnp.int32, (8, LANE), 1)
    start = jnp.zeros((8, LANE), F32)
    for sh in range(1, N_GROUPS):
        start += jnp.where(lane8 >= sh, pltpu.roll(cnt, sh, 1), 0.0)
    off_ref[...] = start[0:1].astype(jnp.int32)
    pos = jnp.sum(onehot * (start[0:1] + earlier), axis=-1, keepdims=True).astype(jnp.int32)
    pos_ref[...] = pos
    perm_t = (ci == pos).astype(BF16)
    hn_s = lax.dot_general(perm_t, hn, TN_DIMS, preferred_element_type=F32).astype(BF16)
    hn_ref[...] = hn_s
    _, comb = _route(jnp.dot(hn_s, wr, preferred_element_type=F32) + br)
    comb_ref[...] = comb


def moe_sort(x, g, w_grp, b_grp, w_exp, b_exp):
    t, d = x.shape
    tm = MOE_TILE
    nr = N_GROUPS + N_EXPERTS
    wr = jnp.pad(jnp.concatenate([w_grp, w_exp], axis=1), ((0, 0), (0, LANE - nr))).astype(BF16)
    br = jnp.pad(jnp.concatenate([b_grp, b_exp]), (0, LANE - nr)).reshape(1, LANE)
    row = lambda i: (i, 0)
    hn, comb, pos, off = pl.pallas_call(
        _moe_sort_kernel,
        out_shape=(jax.ShapeDtypeStruct((t, d), BF16), jax.ShapeDtypeStruct((t, LANE), F32),
                   jax.ShapeDtypeStruct((t, 1), jnp.int32), jax.ShapeDtypeStruct((t // tm, 1, LANE), jnp.int32)),
        grid=(t // tm,),
        in_specs=[pl.BlockSpec((tm, d), row), pl.BlockSpec((1, d), lambda i: (0, 0)),
                  pl.BlockSpec((d, LANE), lambda i: (0, 0)), pl.BlockSpec((1, LANE), lambda i: (0, 0))],
        out_specs=(pl.BlockSpec((tm, d), row), pl.BlockSpec((tm, LANE), row), pl.BlockSpec((tm, 1), row),
                   pl.BlockSpec((None, 1, LANE), lambda i: (i, 0, 0))),
        compiler_params=_params("parallel"),
        name="moe_sort",
    )(x, g.reshape(1, d), wr, br)
    bounds = jnp.concatenate([off[:, 0, :N_GROUPS], jnp.full((t // tm, 1), tm, jnp.int32)], axis=1)
    return hn, comb, pos, bounds.reshape(-1)


def _moe_group_kernel(bounds_ref, hn_ref, c_ref, wg_ref, wu_ref, wd_ref, y_ref):
    i, g, j = pl.program_id(0), pl.program_id(1), pl.program_id(2)

    @pl.when((g == 0) & (j == 0))
    def _():
        y_ref[...] = jnp.zeros_like(y_ref)

    expert_lane = N_GROUPS + g * EXPERTS_PER_GROUP + j

    def sub_tile(want, size):
        r0 = pl.multiple_of(jnp.minimum(want, lo + MOE_TILE - size), ROW_ALIGN)
        rows = pl.ds(r0, size)
        x = hn_ref[rows, :]
        hg = jnp.dot(x, wg_ref[...], preferred_element_type=F32)
        hu = jnp.dot(x, wu_ref[...], preferred_element_type=F32)
        lane = lax.broadcasted_iota(jnp.int32, (size, LANE), 1)
        c = jnp.sum(jnp.where(lane == expert_lane, c_ref[rows, :], 0.0), axis=-1, keepdims=True)
        row = lax.broadcasted_iota(jnp.int32, (size, 1), 0)
        c = jnp.where(row + r0 >= want, c, 0.0)
        hid = (hg * jax.nn.sigmoid(hg)) * hu * c
        y_ref[rows, :] += jnp.dot(hid.astype(BF16), wd_ref[...], preferred_element_type=F32)

    for half in range(MOE_PACK):
        base = (i * MOE_PACK + half) * (N_GROUPS + 1) + g
        lo = half * MOE_TILE
        start, end = lo + bounds_ref[base], lo + bounds_ref[base + 1]
        first = (start // ROW_ALIGN) * ROW_ALIGN
        n_full = (end - first) // MOE_SUB
        rest = end - first - n_full * MOE_SUB

        def full(k, carry, first=first):
            sub_tile(first + k * MOE_SUB, MOE_SUB)
            return carry

        lax.fori_loop(0, n_full, full, 0)
        tail = first + n_full * MOE_SUB

        @pl.when(rest > MOE_SUB // 2)
        def _():
            sub_tile(tail, MOE_SUB)

        @pl.when((rest > 0) & (rest <= MOE_SUB // 2))
        def _():
            sub_tile(tail, MOE_SUB // 2)


def moe_group_experts(hn, comb, bounds, w_gate, w_up, w_down):
    t, d = hn.shape
    tm = MOE_TILE * MOE_PACK
    ne, _, ff = w_gate.shape
    ex = lambda i, g, j, b: (g * EXPERTS_PER_GROUP + j, 0, 0)
    once = pl.Buffered(1) if MOE_PACK > 1 else None
    return pl.pallas_call(
        _moe_group_kernel,
        out_shape=jax.ShapeDtypeStruct((t, d), F32),
        grid_spec=pltpu.PrefetchScalarGridSpec(
            num_scalar_prefetch=1,
            grid=(t // tm, N_GROUPS, EXPERTS_PER_GROUP),
            in_specs=[pl.BlockSpec((tm, d), lambda i, g, j, b: (i, 0), pipeline_mode=once),
                      pl.BlockSpec((tm, LANE), lambda i, g, j, b: (i, 0)),
                      pl.BlockSpec((None, d, ff), ex), pl.BlockSpec((None, d, ff), ex),
                      pl.BlockSpec((None, ff, d), ex)],
            out_specs=pl.BlockSpec((tm, d), lambda i, g, j, b: (i, 0), pipeline_mode=once)),
        compiler_params=_params("parallel", "arbitrary", "arbitrary"),
        name="moe_group_experts",
    )(bounds, hn, comb, w_gate, w_up, w_down)


def _moe_unsort_kernel(y_ref, pos_ref, x_ref, o_ref):
    tm = y_ref.shape[0]
    perm_t = (lax.broadcasted_iota(jnp.int32, (tm, tm), 1) == pos_ref[...]).astype(BF16)
    y = y_ref[...]
    hi = y.astype(BF16)
    lo = (y - hi.astype(F32)).astype(BF16)
    o_ref[...] = (x_ref[...] + jnp.dot(perm_t, hi, preferred_element_type=F32)
                  + jnp.dot(perm_t, lo, preferred_element_type=F32))


def moe_unsort(y, pos, x):
    t, d = x.shape
    tm, tn = MOE_TILE, d // 2
    blk = pl.BlockSpec((tm, tn), lambda i, j: (i, j))
    return pl.pallas_call(
        _moe_unsort_kernel,
        out_shape=jax.ShapeDtypeStruct((t, d), F32),
        grid=(t // tm, d // tn),
        in_specs=[blk, pl.BlockSpec((tm, 1), lambda i, j: (i, 0)), blk],
        out_specs=blk,
        compiler_params=_params("parallel", "arbitrary"),
        name="moe_unsort",
    )(y, pos, x)


def _moe_unsort_norm_kernel(y_ref, pos_ref, x_ref, g_ref, o_ref):
    tm, ts = pos_ref.shape[0], y_ref.shape[0]
    perm_t = (lax.broadcasted_iota(jnp.int32, (tm, ts), 1) == pos_ref[...]).astype(BF16)
    y = y_ref[...]
    hi = y.astype(BF16)
    lo = (y - hi.astype(F32)).astype(BF16)
    x = (x_ref[...] + jnp.dot(perm_t, hi, preferred_element_type=F32)
         + jnp.dot(perm_t, lo, preferred_element_type=F32))
    o_ref[...] = x * lax.rsqrt(jnp.mean(x * x, axis=-1, keepdims=True) + EPS) * g_ref[...]


def moe_unsort_norm(y, pos, x, g, *, row0, rows, tm=512):
    d = x.shape[1]
    per = MOE_TILE // tm
    r0 = row0 // tm
    return pl.pallas_call(
        _moe_unsort_norm_kernel,
        out_shape=jax.ShapeDtypeStruct((rows, d), F32),
        grid=(rows // tm,),
        in_specs=[pl.BlockSpec((MOE_TILE, d), lambda i: ((i + r0) // per, 0)),
                  pl.BlockSpec((tm, 1), lambda i: (i + r0, 0)),
                  pl.BlockSpec((tm, d), lambda i: (i + r0, 0)),
                  pl.BlockSpec((1, d), lambda i: (0, 0))],
        out_specs=pl.BlockSpec((tm, d), lambda i: (i, 0)),
        compiler_params=_params("parallel"),
        name="moe_unsort_norm",
    )(y, pos, x, g.reshape(1, d))


def hier_moe(x, g, w_grp, b_grp, w_exp, b_exp, w_gate, w_up, w_down, *, final=None):
    hn, comb, pos, bounds = moe_sort(x, g, w_grp, b_grp, w_exp, b_exp)
    y = moe_group_experts(hn, comb, bounds, w_gate.astype(BF16), w_up.astype(BF16), w_down.astype(BF16))
    if final is None:
        return moe_unsort(y, pos, x)
    gain, counts = final
    starts = np.cumsum((0,) + tuple(counts))[:-1]
    return tuple(moe_unsort_norm(y, pos, x, gain, row0=int(r0), rows=int(n)) for r0, n in zip(starts, counts))


def even_layer(x, bsz, s, norm_g, rel_bias, w_in, w_out, w2_f, b_f, w2_b, b_b, onorm):
    n_pad = _round_up(EVEN_IN, LANE)
    w_in_p = jnp.pad(w_in, ((0, 0), (0, n_pad - EVEN_IN))).astype(BF16)
    proj = norm_linear(x, norm_g, w_in_p, tn_target=896).reshape(bsz, s, n_pad)
    tq = ATTN_TQ
    ya = attention(proj, proj, proj, heads=A_HEADS, dq=HEAD_DIM, dv=HEAD_DIM,
                   q_off=0, k_off=A_HEADS, v_off=2 * A_HEADS, scale=HEAD_DIM ** -0.5,
                   bias=dilated_bias_table(rel_bias, s, tq), tq=tq)
    q_col = 3 * A_WIDTH
    yb = gla_mixer(proj, w2_f, w2_b, b_f, b_b, onorm, q_col=q_col, k_col=q_col + B_KEYW,
                   v_col=q_col + 2 * B_KEYW, g_col=q_col + 2 * B_KEYW + B_WIDTH,
                   z_col=q_col + 2 * B_KEYW + 2 * B_WIDTH)
    t = bsz * s
    return out_proj(ya.reshape(t, A_WIDTH), yb.reshape(t, B_WIDTH), w_out.astype(BF16), x)


def _odd_columns(w_in, mu):
    c0 = C_IN
    cut = lambda u, lo, n: u[..., lo:lo + n]
    zpad = lambda u, n: jnp.pad(u, [(0, 0)] * (u.ndim - 1) + [(0, n)])
    off = np.cumsum((0,) + D_SPLITS)
    def rwkv_cols(u):
        parts = [cut(u, off[0], 3 * D_WIDTH), cut(u, off[6], D_G_RANK), cut(u, off[3], 2 * D_W_RANK),
                 cut(u, off[5], D_A_RANK)]
        u = jnp.concatenate(parts, axis=-1)
        return zpad(u, DC_PAD - u.shape[-1])
    w_kr = w_in[:, C_Q_RANK + C_KV_RANK:C_IN]
    w_all = jnp.concatenate([rwkv_cols(w_in[:, c0:]), w_in[:, :C_IN], _rot_half_cols(w_kr)], axis=1)
    n_pad = _round_up(w_all.shape[1], 9 * LANE)
    return zpad(w_all, n_pad - w_all.shape[1]).astype(BF16), rwkv_cols(mu)


def odd_layer(x, bsz, s, norm_g, w_in, w_out, q_norm, w_uq, kv_norm, w_ukv, mu, w0_f, w2_f, w0_b, w2_b,
              a0, a2, g2, k_k, k_a, r_k, ln_g, ln_b):
    t = bsz * s
    w_all, mu_cols = _odd_columns(w_in, mu)
    proj = norm_linear(x, norm_g, w_all, tn_target=1152)
    inv = 1.0 / (ROPE_THETA ** (jnp.arange(0, C_ROPE, 2, dtype=F32) / C_ROPE))
    ang = jnp.arange(s, dtype=F32)[:, None] * inv[None, :]
    cos = jnp.pad(jnp.concatenate([jnp.cos(ang)] * 2, axis=1), ((0, 0), (0, LANE - C_ROPE)), constant_values=1.0)
    sin = jnp.pad(jnp.concatenate([jnp.sin(ang)] * 2, axis=1), ((0, 0), (0, LANE - C_ROPE)))
    q, k, v = mla_up(proj, q_norm, w_uq, kv_norm, w_ukv, jnp.tile(cos, (bsz, 1)), jnp.tile(sin, (bsz, 1)),
                     col0=DC_PAD)
    r3 = lambda u: u.reshape(bsz, s, -1)
    yc = attention(r3(q), r3(k), r3(v), heads=C_HEADS, dq=C_QK, dv=C_V, q_off=0, k_off=0, v_off=0,
                   scale=(C_NOPE + C_ROPE) ** -0.5)
    yd = rwkv7_mixer(proj, bsz, s, mu_cols, w0_f, w2_f, w0_b, w2_b, a0, a2, g2, k_k, k_a, r_k, ln_g, ln_b)
    return out_proj(yc.reshape(t, C_WIDTH), yd, w_out.astype(BF16), x)


def kernel(x_prompt, x_sample, rel_bias, norm_mix, norm_ffn, norm_final, ev_w_in, ev_w_out, ev_gla_w2_f, ev_gla_b_f, ev_gla_w2_b, ev_gla_b_b, ev_gla_onorm, od_w_in, od_w_out, od_q_norm, od_w_uq, od_kv_norm, od_w_ukv, od_mu, od_w0_f, od_w2_f, od_w0_b, od_w2_b, od_a0, od_a2, od_g2, od_k_k, od_k_a, od_r_k, od_ln_g, od_ln_b, moe_w_grp, moe_b_grp, moe_w_exp, moe_b_exp, moe_w_gate, moe_w_up, moe_w_down):
    nb_p = x_prompt.shape[0]
    x = jnp.concatenate([x_prompt, x_sample], axis=0)
    bsz, s, d = x.shape
    x = x.reshape(bsz * s, d)
    for i in range(DEPTH):
        j = i // 2
        if i % 2 == 0:
            x = even_layer(x, bsz, s, norm_mix[i], rel_bias, ev_w_in[j], ev_w_out[j], ev_gla_w2_f[j],
                           ev_gla_b_f[j], ev_gla_w2_b[j], ev_gla_b_b[j], ev_gla_onorm[j])
        else:
            x = odd_layer(x, bsz, s, norm_mix[i], od_w_in[j], od_w_out[j], od_q_norm[j], od_w_uq[j],
                          od_kv_norm[j], od_w_ukv[j], od_mu[j], od_w0_f[j], od_w2_f[j], od_w0_b[j],
                          od_w2_b[j], od_a0[j], od_a2[j], od_g2[j], od_k_k[j], od_k_a[j], od_r_k[j],
                          od_ln_g[j], od_ln_b[j])
        final = (norm_final, (nb_p * s, (bsz - nb_p) * s)) if i == DEPTH - 1 else None
        x = hier_moe(x, norm_ffn[i], moe_w_grp[i], moe_b_grp[i], moe_w_exp[i], moe_b_exp[i],
                     moe_w_gate[i], moe_w_up[i], moe_w_down[i], final=final)
    y_p, y_s = x
    return (y_p.reshape(nb_p, s, d), y_s.reshape(bsz - nb_p, s, d))
```

```python
import functools

import jax, jax.numpy as jnp
from jax import lax
import numpy as np
from jax.experimental import pallas as pl
from jax.experimental.pallas import tpu as pltpu

F32, BF16 = jnp.float32, jnp.bfloat16

D_MODEL = 2048
DEPTH = 2
MIX_HALF = D_MODEL // 2
HEAD_DIM = 128
EPS = 1e-6
NEG = -1e30

A_HEADS = MIX_HALF // HEAD_DIM
A_WIDTH = A_HEADS * HEAD_DIM
A_PATTERNS = ((128, 1), (512, 4), (2048, 16))
N_BUCKETS = 32
MAX_DISTANCE = 1024

B_HEADS = 4
B_DV = MIX_HALF // B_HEADS
B_DK = B_DV // 2
B_WIDTH = B_HEADS * B_DV
B_KEYW = B_HEADS * B_DK
B_GATE_RANK = 16
B_GATE_TAU = 16.0
B_CHUNK = 64

C_HEADS = MIX_HALF // 128
C_Q_RANK = 512
C_KV_RANK = 256
C_NOPE = 128
C_ROPE = 64
C_V = 128
C_WIDTH = C_HEADS * C_V
C_QK = 256
ROPE_THETA = 10000.0

D_HEAD = 64
D_HEADS = MIX_HALF // D_HEAD
D_WIDTH = D_HEADS * D_HEAD
D_W_RANK = 64
D_A_RANK = 64
D_G_RANK = 128
D_LN_EPS = 64e-5
D_SPLITS = (D_WIDTH, D_WIDTH, D_WIDTH, D_W_RANK, D_W_RANK, D_A_RANK, D_G_RANK)
D_SHIFT = 3 * D_WIDTH + 2 * D_W_RANK + D_A_RANK + D_G_RANK

N_GROUPS = 4
EXPERTS_PER_GROUP = 4
N_EXPERTS = N_GROUPS * EXPERTS_PER_GROUP

EVEN_IN = 3 * A_WIDTH + 2 * B_KEYW + 2 * B_WIDTH + 2 * B_GATE_RANK
C_IN = C_Q_RANK + C_KV_RANK + C_ROPE

LANE = 128
VMEM_LIMIT = 52 * 1024 * 1024


def _params(*sem):
    return pltpu.CompilerParams(dimension_semantics=sem, vmem_limit_bytes=VMEM_LIMIT)


def _round_up(n, m):
    return -(-n // m) * m


NT_DIMS = (((1,), (1,)), ((), ()))
TN_DIMS = (((0,), (0,)), ((), ()))


def _split3(x):
    hi = x.astype(BF16)
    r1 = x - hi.astype(F32)
    mid = r1.astype(BF16)
    lo = (r1 - mid.astype(F32)).astype(BF16)
    return hi, mid, lo


def _pick_tile(n, target):
    best = LANE
    for t in range(LANE, target + 1, LANE):
        if n % t == 0:
            best = t
    return best


def _norm_linear_kernel(x_ref, g_ref, w_ref, o_ref, xn_ref):
    @pl.when(pl.program_id(1) == 0)
    def _():
        x = x_ref[...]
        y = x * lax.rsqrt(jnp.mean(x * x, axis=-1, keepdims=True) + EPS) * g_ref[...]
        xn_ref[...] = y.astype(BF16)

    o_ref[...] = jnp.dot(xn_ref[...], w_ref[...], preferred_element_type=F32)


def norm_linear(x, g, w, *, tm=1024, tn_target=1024):
    t, k = x.shape
    n = w.shape[1]
    tn = _pick_tile(n, tn_target)
    return pl.pallas_call(
        _norm_linear_kernel,
        out_shape=jax.ShapeDtypeStruct((t, n), F32),
        grid=(t // tm, n // tn),
        in_specs=[pl.BlockSpec((tm, k), lambda i, j: (i, 0)),
                  pl.BlockSpec((1, k), lambda i, j: (0, 0)),
                  pl.BlockSpec((k, tn), lambda i, j: (0, j))],
        out_specs=pl.BlockSpec((tm, tn), lambda i, j: (i, j)),
        scratch_shapes=[pltpu.VMEM((tm, k), BF16)],
        compiler_params=_params("parallel", "arbitrary"),
        name="norm_linear",
    )(x, g.reshape(1, k), w)


def _out_proj_kernel(a_ref, b_ref, wa_ref, wb_ref, x_ref, o_ref):
    acc = jnp.dot(a_ref[...], wa_ref[...], preferred_element_type=F32)
    acc += jnp.dot(b_ref[...], wb_ref[...], preferred_element_type=F32)
    o_ref[...] = x_ref[...] + acc


def out_proj(a, b, w, x, *, tm=1024, tn=512):
    t, ka = a.shape
    kb = b.shape[1]
    n = w.shape[1]
    return pl.pallas_call(
        _out_proj_kernel,
        out_shape=jax.ShapeDtypeStruct((t, n), F32),
        grid=(t // tm, n // tn),
        in_specs=[pl.BlockSpec((tm, ka), lambda i, j: (i, 0)),
                  pl.BlockSpec((tm, kb), lambda i, j: (i, 0)),
                  pl.BlockSpec((ka, tn), lambda i, j: (0, j)),
                  pl.BlockSpec((kb, tn), lambda i, j: (0, j)),
                  pl.BlockSpec((tm, tn), lambda i, j: (i, j))],
        out_specs=pl.BlockSpec((tm, tn), lambda i, j: (i, j)),
        compiler_params=_params("parallel", "arbitrary"),
        name="out_proj",
    )(a, b, w[:ka], w[ka:], x)


ATTN_KB = 256
ATTN_TQ = 512


def _attn_kernel(*refs, scale, has_bias):
    if has_bias:
        q_ref, k_ref, v_ref, bias_ref, o_ref, kb_ref, vb_ref = refs
    else:
        q_ref, k_ref, v_ref, o_ref, kb_ref, vb_ref = refs
    s_len = k_ref.shape[0]

    @pl.when(pl.program_id(2) == 0)
    def _():
        kb_ref[...] = k_ref[...].astype(BF16)
        vb_ref[...] = v_ref[...].astype(BF16)

    q = (q_ref[...] * scale).astype(BF16)
    blocks = [slice(j * ATTN_KB, (j + 1) * ATTN_KB) for j in range(s_len // ATTN_KB)]
    scores = []
    m = None
    for blk in blocks:
        sj = lax.dot_general(q, kb_ref[blk, :], NT_DIMS, preferred_element_type=F32)
        if has_bias:
            sj = sj + bias_ref[:, blk]
        mj = jnp.max(sj, axis=-1, keepdims=True)
        m = mj if m is None else jnp.maximum(m, mj)
        scores.append(sj)
    o = den = None
    for blk, sj in zip(blocks, scores):
        p = jnp.exp(sj - m)
        dj = jnp.sum(p, axis=-1, keepdims=True)
        oj = jnp.dot(p.astype(BF16), vb_ref[blk, :], preferred_element_type=F32)
        o, den = (oj, dj) if o is None else (o + oj, den + dj)
    o_ref[...] = (o / den).astype(o_ref.dtype)


def attention(q, k, v, *, heads, dq, dv, q_off, k_off, v_off, scale, bias=None, tq=ATTN_TQ):
    b, s, _ = q.shape
    nq = s // tq
    in_specs = [pl.BlockSpec((None, tq, dq), lambda bi, h, qi: (bi, qi, q_off + h)),
                pl.BlockSpec((None, s, dq), lambda bi, h, qi: (bi, 0, k_off + h)),
                pl.BlockSpec((None, s, dv), lambda bi, h, qi: (bi, 0, v_off + h))]
    args = [q, k, v]
    if bias is not None:
        in_specs.append(pl.BlockSpec((None, None, tq, s), lambda bi, h, qi: (h, qi, 0, 0)))
        args.append(bias)
    return pl.pallas_call(
        functools.partial(_attn_kernel, scale=scale, has_bias=bias is not None),
        out_shape=jax.ShapeDtypeStruct((b, s, heads * dv), BF16),
        grid=(b, heads, nq),
        in_specs=in_specs,
        out_specs=pl.BlockSpec((None, tq, dv), lambda bi, h, qi: (bi, qi, h)),
        scratch_shapes=[pltpu.VMEM((s, dq), BF16), pltpu.VMEM((s, dv), BF16)],
        compiler_params=_params("parallel", "parallel", "arbitrary"),
        name="attention_bias" if bias is not None else "attention",
    )(*args)


def _t5_bucket(rel):
    half = N_BUCKETS // 2
    exact = half // 2
    n = np.abs(rel)
    large = exact + (np.log(np.maximum(n, 1) / exact) / np.log(MAX_DISTANCE / exact) * (half - exact)).astype(np.int64)
    large = np.minimum(large, half - 1)
    return ((rel > 0) * half + np.where(n < exact, n, large)).astype(np.int32)


def dilated_bias_table(rel_bias, s, tq):
    heads = rel_bias.shape[1]
    d = np.arange(-(s - 1), s)
    count = np.zeros(d.shape, np.float32)
    for window, dil in A_PATTERNS:
        count += ((d % dil == 0) & (np.abs(d) <= (window // (2 * dil)) * dil)).astype(np.float32)
    logc = np.where(count > 0, np.log(np.maximum(count, 1.0)), NEG).astype(np.float32)
    onehot = (_t5_bucket(d)[:, None] == np.arange(N_BUCKETS)[None, :]).astype(np.float32)
    line = jnp.transpose(jnp.dot(onehot, rel_bias.astype(F32), precision=lax.Precision.HIGHEST)) + logc[None]
    width = 2 * s
    line = jnp.pad(line, ((0, 0), (0, width - line.shape[1])))[:, None, :]
    nq = s // tq
    return pl.pallas_call(
        functools.partial(_skew_kernel, tq=tq, nq=nq),
        out_shape=jax.ShapeDtypeStruct((heads, nq, tq, s), F32),
        grid=(heads, nq),
        in_specs=[pl.BlockSpec((None, 1, width), lambda h, qi: (h, 0, 0))],
        out_specs=pl.BlockSpec((None, None, tq, s), lambda h, qi: (h, qi, 0, 0)),
        compiler_params=_params("parallel", "arbitrary"),
        name="bias_skew",
    )(line)


def _skew_kernel(line_ref, o_ref, *, tq, nq):
    width = line_ref.shape[1]
    first = (nq - 1 - pl.program_id(1)) * tq
    x = jnp.broadcast_to(line_ref[...], (tq, width))
    x = pltpu.roll(x, width - (tq - 1) - first, 1, stride=1, stride_axis=0)
    o_ref[...] = x[:, :o_ref.shape[1]]


GLA_UNROLL = 8


def _gla_kernel(q_ref, k_ref, v_ref, g_ref, z_ref, w2f_ref, w2b_ref, bf_ref, bb_ref, on_ref, o_ref,
                la_ref, acc_ref, qcat_ref, upd_ref, dec_ref, scat_ref, st_ref):
    s_len = q_ref.shape[0]
    c = B_CHUNK
    nchunk = s_len // c
    z = z_ref[...].astype(BF16)
    gate = lambda w2_ref, b_ref: jax.nn.log_sigmoid(
        jnp.dot(z, w2_ref[...], preferred_element_type=F32) + b_ref[...]) * (1.0 / B_GATE_TAU)
    la_ref[0] = gate(w2f_ref, bf_ref)
    la_ref[1] = gate(w2b_ref, bb_ref)

    ri = lax.broadcasted_iota(jnp.int32, (c, c), 0)
    ci = lax.broadcasted_iota(jnp.int32, (c, c), 1)
    keep = (ri >= ci, ri <= ci)
    tri3 = tuple(jnp.concatenate([kp.astype(BF16)] * 3, axis=1) for kp in keep)

    def chunk_rows(n):
        return pl.ds(pl.multiple_of(n * c, c), c)

    def pass1(i, carry):
        units = [(i * GLA_UNROLL + u, d) for u in range(GLA_UNROLL) for d in range(2)]
        st = []
        for n, d in units:
            rows = chunk_rows(n)
            gcum = jnp.dot(tri3[d], jnp.concatenate(_split3(la_ref[d, rows, :]), axis=0),
                           preferred_element_type=F32)
            st.append(dict(rows=rows, gcum=gcum))
        for (n, d), c in zip(units, st):
            gcum = c["gcum"]
            gend = gcum[0:1] if d == 1 else gcum[B_CHUNK - 1:B_CHUNK]
            kc = k_ref[c["rows"], :]
            c["q_in"] = (q_ref[c["rows"], :] * (B_DK ** -0.5) * jnp.exp(gcum)).astype(BF16)
            c["k_out"] = (kc * jnp.exp(gend - gcum)).astype(BF16)
            c["att"] = lax.dot_general(c["q_in"], (kc * jnp.exp(-gcum)).astype(BF16), NT_DIMS,
                                       preferred_element_type=F32)
            dec_ref[d, n] = jnp.broadcast_to(jnp.exp(gend), (8, B_DK))
        for (n, d), c in zip(units, st):
            vc = v_ref[c["rows"], :].astype(BF16)
            c["o"] = jnp.dot(jnp.where(keep[d], c["att"], 0.0).astype(BF16), vc, preferred_element_type=F32)
            upd_ref[d, n] = lax.dot_general(vc, c["k_out"], TN_DIMS, preferred_element_type=F32)
            qcat_ref[c["rows"], d * B_DK:(d + 1) * B_DK] = c["q_in"]
        for u in range(GLA_UNROLL):
            acc_ref[st[2 * u]["rows"], :] = st[2 * u]["o"] + st[2 * u + 1]["o"]
        return carry

    lax.fori_loop(0, nchunk // GLA_UNROLL, pass1, 0)

    st_ref[...] = jnp.zeros_like(st_ref)

    def pass2(n, carry):
        for d, m in ((0, n), (1, nchunk - 1 - n)):
            state = st_ref[d]
            scat_ref[m, :, d * B_DK:(d + 1) * B_DK] = state.astype(BF16)
            st_ref[d] = state * dec_ref[d, m][0:1] + upd_ref[d, m]
        return carry

    lax.fori_loop(0, nchunk, pass2, 0)

    def pass3(i, carry):
        rows = [chunk_rows(i * GLA_UNROLL + u) for u in range(GLA_UNROLL)]
        outs = [acc_ref[r, :] + lax.dot_general(qcat_ref[r, :], scat_ref[i * GLA_UNROLL + u], NT_DIMS,
                                                preferred_element_type=F32) for u, r in enumerate(rows)]
        for r, o in zip(rows, outs):
            o = o * lax.rsqrt(jnp.mean(o * o, axis=-1, keepdims=True) + EPS) * on_ref[...]
            g = g_ref[r, :]
            o_ref[r, :] = (o * (g * jax.nn.sigmoid(g))).astype(o_ref.dtype)
        return carry

    lax.fori_loop(0, nchunk // GLA_UNROLL, pass3, 0)


def gla_mixer(proj, w2f, w2b, b_f, b_b, onorm, *, q_col, k_col, v_col, g_col, z_col):
    b, s, _ = proj.shape
    hm = lambda blk: (lambda bi, h: (bi, 0, blk + h))
    w2f_p = jnp.zeros((LANE, B_KEYW), F32).at[:B_GATE_RANK].set(w2f).astype(BF16)
    w2b_p = jnp.zeros((LANE, B_KEYW), F32).at[B_GATE_RANK:2 * B_GATE_RANK].set(w2b).astype(BF16)
    return pl.pallas_call(
        _gla_kernel,
        out_shape=jax.ShapeDtypeStruct((b, s, B_WIDTH), BF16),
        grid=(b, B_HEADS),
        in_specs=[pl.BlockSpec((None, s, B_DK), hm(q_col // B_DK)),
                  pl.BlockSpec((None, s, B_DK), hm(k_col // B_DK)),
                  pl.BlockSpec((None, s, B_DV), hm(v_col // B_DV)),
                  pl.BlockSpec((None, s, B_DV), hm(g_col // B_DV)),
                  pl.BlockSpec((None, s, LANE), lambda bi, h: (bi, 0, z_col // LANE)),
                  pl.BlockSpec((LANE, B_DK), lambda bi, h: (0, h)),
                  pl.BlockSpec((LANE, B_DK), lambda bi, h: (0, h)),
                  pl.BlockSpec((1, B_DK), lambda bi, h: (0, h)),
                  pl.BlockSpec((1, B_DK), lambda bi, h: (0, h)),
                  pl.BlockSpec((1, B_DV), lambda bi, h: (0, 0))],
        out_specs=pl.BlockSpec((None, s, B_DV), lambda bi, h: (bi, 0, h)),
        scratch_shapes=[pltpu.VMEM((2, s, B_DK), F32),
                        pltpu.VMEM((s, B_DV), F32),
                        pltpu.VMEM((s, 2 * B_DK), BF16),
                        pltpu.VMEM((2, s // B_CHUNK, B_DV, B_DK), F32),
                        pltpu.VMEM((2, s // B_CHUNK, 8, B_DK), F32),
                        pltpu.VMEM((s // B_CHUNK, B_DV, 2 * B_DK), BF16),
                        pltpu.VMEM((2, B_DV, B_DK), F32)],
        compiler_params=_params("parallel", "arbitrary"),
        name="gla_mixer",
    )(proj, proj, proj, proj, proj, w2f_p, w2b_p, b_f.reshape(1, -1), b_b.reshape(1, -1), onorm.reshape(1, -1))


def _mla_up_kernel(cq_ref, ckv_ref, kr_ref, qn_ref, kvn_ref, wq_ref, wqr_ref, wkv_ref, cos_ref, sin_ref,
                   q_ref, k_ref, v_ref):
    def rms(x, g):
        return (x * lax.rsqrt(jnp.mean(x * x, axis=-1, keepdims=True) + EPS) * g).astype(BF16)

    cq = rms(cq_ref[...], qn_ref[...])
    ckv = rms(ckv_ref[...], kvn_ref[...])
    cos, sin = cos_ref[...], sin_ref[...]
    kr = kr_ref[...]
    k_rope = kr * cos + pltpu.roll(kr, LANE - C_ROPE, 1) * sin
    lane = lax.broadcasted_iota(jnp.int32, k_rope.shape, 1)
    k_rope = jnp.where(lane < C_ROPE, k_rope, 0.0)
    for h in range(C_HEADS):
        q = jnp.dot(cq, wq_ref[:, h * C_QK:(h + 1) * C_QK], preferred_element_type=F32)
        qp = jnp.dot(cq, wqr_ref[:, h * LANE:(h + 1) * LANE], preferred_element_type=F32)
        q_ref[:, h * C_QK:h * C_QK + C_NOPE] = q[:, :C_NOPE]
        q_ref[:, h * C_QK + C_NOPE:(h + 1) * C_QK] = q[:, C_NOPE:] * cos + qp * sin
        kv = jnp.dot(ckv, wkv_ref[:, h * 2 * LANE:(h + 1) * 2 * LANE], preferred_element_type=F32)
        k_ref[:, h * C_QK:h * C_QK + C_NOPE] = kv[:, :C_NOPE]
        k_ref[:, h * C_QK + C_NOPE:(h + 1) * C_QK] = k_rope
        v_ref[:, h * C_V:(h + 1) * C_V] = kv[:, C_NOPE:]


def _rot_half_cols(w):
    half = w.shape[-1] // 2
    return jnp.concatenate([-w[..., half:], w[..., :half]], axis=-1)


def mla_up(proj, q_norm, w_uq, kv_norm, w_ukv, cos, sin, *, col0, tm=512):
    t = proj.shape[0]
    wq = w_uq.reshape(C_Q_RANK, C_HEADS, C_NOPE + C_ROPE)
    wq_main = jnp.pad(wq, ((0, 0), (0, 0), (0, C_QK - C_NOPE - C_ROPE))).reshape(C_Q_RANK, C_HEADS * C_QK)
    wq_rot = jnp.pad(_rot_half_cols(wq[..., C_NOPE:]), ((0, 0), (0, 0), (0, LANE - C_ROPE)))
    wq_rot = wq_rot.reshape(C_Q_RANK, C_HEADS * LANE)
    row = lambda i: (i, 0)
    full = lambda arr: pl.BlockSpec(arr.shape, lambda i: (0, 0))
    g_q, g_kv = q_norm.reshape(1, -1), kv_norm.reshape(1, -1)
    wq_main, wq_rot, wkv = wq_main.astype(BF16), wq_rot.astype(BF16), w_ukv.astype(BF16)
    return pl.pallas_call(
        _mla_up_kernel,
        out_shape=(jax.ShapeDtypeStruct((t, C_HEADS * C_QK), F32),
                   jax.ShapeDtypeStruct((t, C_HEADS * C_QK), F32),
                   jax.ShapeDtypeStruct((t, C_WIDTH), F32)),
        grid=(t // tm,),
        in_specs=[pl.BlockSpec((tm, C_Q_RANK), lambda i: (i, col0 // C_Q_RANK)),
                  pl.BlockSpec((tm, C_KV_RANK), lambda i: (i, (col0 + C_Q_RANK) // C_KV_RANK)),
                  pl.BlockSpec((tm, LANE), lambda i: (i, (col0 + C_Q_RANK + C_KV_RANK) // LANE)),
                  full(g_q), full(g_kv), full(wq_main), full(wq_rot), full(wkv),
                  pl.BlockSpec((tm, LANE), row), pl.BlockSpec((tm, LANE), row)],
        out_specs=(pl.BlockSpec((tm, C_HEADS * C_QK), row),
                   pl.BlockSpec((tm, C_HEADS * C_QK), row),
                   pl.BlockSpec((tm, C_WIDTH), row)),
        compiler_params=_params("parallel"),
        name="mla_up",
    )(proj, proj, proj, g_q, g_kv, wq_main, wq_rot, wkv, cos, sin)


RG = 4
RGW = RG * D_HEAD
RCH = 64
DC_R, DC_K, DC_V = 0, D_WIDTH, 2 * D_WIDTH
DC_ZG = 3 * D_WIDTH
DC_ZW = DC_ZG + D_G_RANK
DC_ZA = DC_ZW + 2 * D_W_RANK
DC_PAD = 7 * 512


def _head_sums(x, bo):
    return jnp.concatenate(
        [jnp.dot(x[:, RGW * g:RGW * (g + 1)], bo, preferred_element_type=F32, precision=lax.Precision.HIGHEST)
         for g in range(x.shape[1] // RGW)], axis=1)


def _block_ones():
    i = np.arange(RGW)
    return jnp.asarray((i[:, None] // D_HEAD) == (i[None, :] // D_HEAD), F32)


def _rwkv_prep_kernel(x_ref, xp_ref, xn_ref, mu_ref, w2f_ref, w2b_ref, a2_ref, g2_ref, w0f_ref, w0b_ref,
                      a0_ref, kk_ref, ka_ref, rk_ref, bo_ref,
                      r_ref, k_ref, v_ref, a_ref, b_ref, lwf_ref, lwb_ref, g_ref, bonus_ref, *, tiles_per_seq):
    i = pl.program_id(0) % tiles_per_seq
    x = x_ref[...]
    tm = x.shape[0]
    row = lax.broadcasted_iota(jnp.int32, x.shape, 0)
    prev_row = jnp.where(i == 0, 0.0, xp_ref[7:8, :])
    next_row = jnp.where(i == tiles_per_seq - 1, 0.0, xn_ref[0:1, :])
    prev = jnp.where(row == 0, prev_row, pltpu.roll(x, 1, 0))
    nxt = jnp.where(row == tm - 1, next_row, pltpu.roll(x, tm - 1, 0))
    x = x + mu_ref[...] * (0.5 * (prev + nxt) - x)
    r, k, v = x[:, DC_R:DC_R + D_WIDTH], x[:, DC_K:DC_K + D_WIDTH], x[:, DC_V:DC_V + D_WIDTH]
    zg = x[:, DC_ZG:DC_ZG + LANE]
    zw = x[:, DC_ZW:DC_ZW + LANE]
    za = x[:, DC_ZA:DC_ZA + LANE]
    tz = jnp.tanh(zw).astype(BF16)
    log_decay = lambda w0_ref, w2_ref: -np.exp(-0.5).astype(np.float32) * jax.nn.sigmoid(
        w0_ref[...] + jnp.dot(tz, w2_ref[...], preferred_element_type=F32))
    lwf_ref[...] = log_decay(w0f_ref, w2f_ref)
    lwb_ref[...] = log_decay(w0b_ref, w2b_ref)
    ag = jax.nn.sigmoid(a0_ref[...] + jnp.dot(za.astype(BF16), a2_ref[...], preferred_element_type=F32))
    g_ref[...] = jnp.dot(jax.nn.sigmoid(zg).astype(BF16), g2_ref[...], preferred_element_type=F32)
    bo = bo_ref[...]
    kk = k * kk_ref[...]
    kk = kk / jnp.maximum(jnp.sqrt(_head_sums(kk * kk, bo)), 1e-12)
    k = k * (1.0 + (ag - 1.0) * ka_ref[...])
    r_ref[...] = r
    k_ref[...] = k
    v_ref[...] = v
    a_ref[...] = -kk
    b_ref[...] = kk * ag
    bonus_ref[...] = _head_sums(r * k * rk_ref[...], bo) * v


def rwkv_prep(proj, mu, w0_f, w2_f, w0_b, w2_b, a0, a2, g2, k_k, k_a, r_k, *, seq, tm=256):
    t = proj.shape[0]
    tiles_per_seq = seq // tm
    hb = tm // 8
    nblk8 = t // 8
    pad_rows = lambda w, lo: jnp.zeros((LANE, D_WIDTH), F32).at[lo:lo + w.shape[0]].set(w).astype(BF16)
    vec = lambda u: u.reshape(1, -1)
    consts = [vec(mu), pad_rows(w2_f, 0), pad_rows(w2_b, D_W_RANK), pad_rows(a2, 0), g2.astype(BF16),
              vec(w0_f), vec(w0_b), vec(a0), vec(k_k), vec(k_a), vec(r_k), _block_ones()]
    full = lambda arr: pl.BlockSpec(arr.shape, lambda i: (0, 0))
    out_spec = pl.BlockSpec((tm, D_WIDTH), lambda i: (i, 0))
    return pl.pallas_call(
        functools.partial(_rwkv_prep_kernel, tiles_per_seq=tiles_per_seq),
        out_shape=tuple(jax.ShapeDtypeStruct((t, D_WIDTH), F32) for _ in range(9)),
        grid=(t // tm,),
        in_specs=[pl.BlockSpec((tm, DC_PAD), lambda i: (i, 0)),
                  pl.BlockSpec((8, DC_PAD), lambda i: (jnp.maximum(i * hb - 1, 0), 0)),
                  pl.BlockSpec((8, DC_PAD), lambda i: (jnp.minimum((i + 1) * hb, nblk8 - 1), 0))]
                 + [full(c) for c in consts],
        out_specs=tuple(out_spec for _ in range(9)),
        compiler_params=_params("parallel"),
        name="rwkv_prep",
    )(proj, proj, proj, *consts)


def _rwkv_chunk_kernel(*refs, ngroups):
    ins, (yf_ref, yb_ref, mt_ref) = refs[:12], refs[12:]

    @pl.when(pl.program_id(1) == 0)
    def _():
        mt_ref[...] = jnp.zeros_like(mt_ref)

    row = lax.broadcasted_iota(jnp.int32, (RCH, RGW), 0)
    col = lax.broadcasted_iota(jnp.int32, (RCH, RGW), 1) & (RCH - 1)
    bdmask = (lax.broadcasted_iota(jnp.int32, (RGW, RGW), 0) // D_HEAD
              == lax.broadcasted_iota(jnp.int32, (RGW, RGW), 1) // D_HEAD)
    tr = lax.broadcasted_iota(jnp.int32, (RCH, RCH), 0)
    tc = lax.broadcasted_iota(jnp.int32, (RCH, RCH), 1)
    zero = jnp.zeros((), F32)

    def bd(z):
        zb = z.astype(BF16)
        return jnp.where(bdmask, jnp.concatenate([zb] * RG, axis=0), jnp.zeros((), BF16))

    def mm(x, y, dims=None):
        x = x.astype(BF16)
        if dims is None:
            return jnp.dot(x, y, preferred_element_type=F32)
        return lax.dot_general(x, y, dims, preferred_element_type=F32)

    chains = [(d, g) for d in range(2) for g in range(ngroups)]
    st = []
    for d, g in chains:
        backward = d == 1
        r_ref, k_ref, v_ref, a_ref, b_ref, lw_ref = ins[6 * d:6 * d + 6]
        tri = ((tc >= tr) if backward else (tc <= tr)).astype(BF16)
        sl = slice(RGW * g, RGW * (g + 1))
        r, k, v, a, b, lw = (ref[:, sl] for ref in (r_ref, k_ref, v_ref, a_ref, b_ref, lw_ref))
        lam = jnp.dot(jnp.concatenate([tri] * 3, axis=1), jnp.concatenate(_split3(lw), axis=0),
                      preferred_element_type=F32)
        lamc = lam[0:1] if backward else lam[RCH - 1:RCH]
        e_inv = jnp.exp(-lam)
        e_out = jnp.exp(lamc - lam)
        ar = jnp.concatenate([a * jnp.exp(lam - lw), r * jnp.exp(lam)], axis=0).astype(BF16)
        bk = jnp.concatenate([b * e_out, k * e_out], axis=0).astype(BF16)
        st.append(dict(ar=ar, bk=bk, v=v, lamc=lamc, sl=sl,
                       gb=mm(ar, bd(b * e_inv), NT_DIMS), gk=mm(ar, bd(k * e_inv), NT_DIMS)))
    for (d, g), c in zip(chains, st):
        strict = (col > row) if d == 1 else (col < row)
        incl = (col >= row) if d == 1 else (col <= row)
        c["lp"] = jnp.where(strict, c["gb"][:RCH], zero)
        lak = jnp.where(strict, c["gk"][:RCH], zero)
        c["grb"] = jnp.where(incl, c["gb"][RCH:], zero).astype(BF16)
        c["grk"] = jnp.where(incl, c["gk"][RCH:], zero).astype(BF16)
        c["mt"] = mt_ref[d, g]
        amrm = mm(c["ar"], c["mt"].astype(BF16), NT_DIMS)
        c["bdv"] = bd(c["v"])
        c["u"] = amrm[:RCH] + mm(lak, c["bdv"])
        c["rm"] = amrm[RCH:]
    for rnd in range(6):
        for c in st:
            lpb = c["lp"].astype(BF16)
            c["u"] = c["u"] + mm(lpb, bd(c["u"]))
            if rnd < 5:
                c["lp"] = mm(lpb, bd(c["lp"]))
    for (d, g), c in zip(chains, st):
        y_ref = yb_ref if d == 1 else yf_ref
        y_ref[:, c["sl"]] = c["rm"] + mm(c["grb"], bd(c["u"])) + mm(c["grk"], c["bdv"])
        uv = jnp.concatenate([c["u"], c["v"]], axis=0).astype(BF16)
        upd = lax.dot_general(uv, c["bk"], TN_DIMS, preferred_element_type=F32)
        mt_ref[d, g] = c["mt"] * jnp.exp(c["lamc"]) + jnp.where(bdmask, upd, zero)


def rwkv_chunked(r, k, v, a, b, lwf, lwb):
    bsz, s, wd = r.shape
    nc = s // RCH
    fspec = pl.BlockSpec((None, RCH, wd), lambda bi, n: (bi, n, 0))
    bspec = pl.BlockSpec((None, RCH, wd), lambda bi, n: (bi, nc - 1 - n, 0))
    return pl.pallas_call(
        functools.partial(_rwkv_chunk_kernel, ngroups=wd // RGW),
        out_shape=(jax.ShapeDtypeStruct((bsz, s, wd), F32), jax.ShapeDtypeStruct((bsz, s, wd), F32)),
        grid=(bsz, nc),
        in_specs=[fspec] * 6 + [bspec] * 6,
        out_specs=(fspec, bspec),
        scratch_shapes=[pltpu.VMEM((2, wd // RGW, RGW, RGW), F32)],
        compiler_params=_params("parallel", "arbitrary"),
        name="rwkv_chunked",
    )(r, k, v, a, b, lwf, r, k, v, a, b, lwb)


def _rwkv_post_kernel(yf_ref, yb_ref, bonus_ref, g_ref, lng_ref, lnb_ref, bo_ref, o_ref):
    bo = bo_ref[...]
    y = yf_ref[...] + yb_ref[...]
    yc = y - _head_sums(y, bo) * (1.0 / D_HEAD)
    var = _head_sums(yc * yc, bo) * (1.0 / D_HEAD)
    y = yc * lax.rsqrt(var + D_LN_EPS) * lng_ref[...] + lnb_ref[...]
    o_ref[...] = ((y + bonus_ref[...]) * g_ref[...]).astype(o_ref.dtype)


def rwkv_post(yf, yb, bonus, g, ln_g, ln_b, *, tm=512):
    t, wd = yf.shape
    row = pl.BlockSpec((tm, wd), lambda i: (i, 0))
    vec = pl.BlockSpec((1, wd), lambda i: (0, 0))
    bo = _block_ones()
    return pl.pallas_call(
        _rwkv_post_kernel,
        out_shape=jax.ShapeDtypeStruct((t, wd), BF16),
        grid=(t // tm,),
        in_specs=[row, row, row, row, vec, vec, pl.BlockSpec(bo.shape, lambda i: (0, 0))],
        out_specs=row,
        compiler_params=_params("parallel"),
        name="rwkv_post",
    )(yf, yb, bonus, g, ln_g.reshape(1, wd), ln_b.reshape(1, wd), bo)


def rwkv7_mixer(proj, bsz, s, mu, w0_f, w2_f, w0_b, w2_b, a0, a2, g2, k_k, k_a, r_k, ln_g, ln_b):
    r, k, v, a, b, lwf, lwb, g, bonus = rwkv_prep(proj, mu, w0_f, w2_f, w0_b, w2_b, a0, a2, g2, k_k, k_a,
                                                  r_k.reshape(-1), seq=s)
    r3 = lambda u: u.reshape(bsz, s, D_WIDTH)
    yf, yb = rwkv_chunked(r3(r), r3(k), r3(v), r3(a), r3(b), r3(lwf), r3(lwb))
    return rwkv_post(yf.reshape(-1, D_WIDTH), yb.reshape(-1, D_WIDTH), bonus, g, ln_g, ln_b)


MOE_TILE = 1024
MOE_TOPK = 2
MOE_ROWS = MOE_TILE * MOE_TOPK
MOE_SUB = 128
ROW_ALIGN = 16


def _route(logit):
    lane = lax.broadcasted_iota(jnp.int32, logit.shape, 1)
    first_at = lambda mask: jnp.min(jnp.where(mask, lane, jnp.int32(LANE)), axis=-1, keepdims=True)
    is_grp = lane < N_GROUPS
    gl = jnp.where(is_grp, logit, NEG)
    gmax = jnp.max(gl, axis=-1, keepdims=True)
    p_grp = 1.0 / jnp.sum(jnp.where(is_grp, jnp.exp(gl - gmax), 0.0), axis=-1, keepdims=True)
    i_grp = first_at(is_grp & (gl == gmax))
    lo = N_GROUPS + i_grp * EXPERTS_PER_GROUP
    in_grp = (lane >= lo) & (lane < lo + EXPERTS_PER_GROUP)
    el = jnp.where(in_grp, logit, NEG)
    l1 = jnp.max(el, axis=-1, keepdims=True)
    i1 = first_at(in_grp & (el == l1))
    rest = in_grp & (lane != i1)
    el2 = jnp.where(rest, logit, NEG)
    l2 = jnp.max(el2, axis=-1, keepdims=True)
    i2 = first_at(rest & (el2 == l2))
    e2 = jnp.exp(l2 - l1)
    w1 = p_grp / (1.0 + e2)
    w2 = p_grp * e2 / (1.0 + e2)
    return (i1, i2), jnp.where(lane == i1, w1, jnp.where(lane == i2, w2, 0.0))


def _pair_perm(pos, col0, ncols):
    col = lax.broadcasted_iota(jnp.int32, (pos.shape[0], ncols), 1) + col0
    return ((col == pos[:, 0:1]) | (col == pos[:, 1:2])).astype(BF16)


def _moe_sort_kernel(x_ref, g_ref, wr_ref, br_ref, hn_ref, comb_ref, pos_ref, off_ref):
    x = x_ref[...]
    tm = x.shape[0]
    hn = (x * lax.rsqrt(jnp.mean(x * x, axis=-1, keepdims=True) + EPS) * g_ref[...]).astype(BF16)
    wr, br = wr_ref[...], br_ref[...]
    (i1, i2), _ = _route(jnp.dot(hn, wr, preferred_element_type=F32) + br)
    lane = lax.broadcasted_iota(jnp.int32, (tm, LANE), 1)
    sel = ((lane == i1) | (lane == i2)).astype(F32)
    ri = lax.broadcasted_iota(jnp.int32, (tm, tm), 0)
    ci = lax.broadcasted_iota(jnp.int32, (tm, tm), 1)
    earlier = jnp.dot((ci < ri).astype(BF16), sel.astype(BF16), preferred_element_type=F32)
    cnt = jnp.broadcast_to(jnp.sum(sel, axis=0, keepdims=True), (8, LANE))
    lane8 = lax.broadcasted_iota(jnp.int32, (8, LANE), 1)
    incl = cnt
    for sh in (1, 2, 4, 8, 16):
        incl = incl + jnp.where(lane8 >= sh, pltpu.roll(incl, sh, 1), 0.0)
    start = (incl - cnt)[0:1]
    off_ref[...] = start.astype(jnp.int32)
    row_of = start + earlier
    pick = lambda idx: jnp.sum(jnp.where(lane == idx, row_of, 0.0), axis=-1, keepdims=True).astype(jnp.int32)
    p1, p2 = pick(i1), pick(i2)
    pos = jnp.where(lane == 0, p1, jnp.where(lane == 1, p2, 0))
    pos_ref[...] = pos
    for h in range(MOE_TOPK):
        perm_t = _pair_perm(pos, h * tm, tm)
        hn_ref[h * tm:(h + 1) * tm, :] = lax.dot_general(perm_t, hn, TN_DIMS, preferred_element_type=F32).astype(BF16)
    _, comb = _route(jnp.dot(hn_ref[...], wr, preferred_element_type=F32) + br)
    comb_ref[...] = comb


def moe_sort(x, g, w_grp, b_grp, w_exp, b_exp):
    t, d = x.shape
    tm = MOE_TILE
    nr = N_GROUPS + N_EXPERTS
    wr = jnp.pad(jnp.concatenate([w_grp, w_exp], axis=1), ((0, 0), (0, LANE - nr))).astype(BF16)
    br = jnp.pad(jnp.concatenate([b_grp, b_exp]), (0, LANE - nr)).reshape(1, LANE)
    row = lambda i: (i, 0)
    hn, comb, pos, off = pl.pallas_call(
        _moe_sort_kernel,
        out_shape=(jax.ShapeDtypeStruct((t * MOE_TOPK, d), BF16), jax.ShapeDtypeStruct((t * MOE_TOPK, LANE), F32),
                   jax.ShapeDtypeStruct((t, LANE), jnp.int32), jax.ShapeDtypeStruct((t // tm, 1, LANE), jnp.int32)),
        grid=(t // tm,),
        in_specs=[pl.BlockSpec((tm, d), row), pl.BlockSpec((1, d), lambda i: (0, 0)),
                  pl.BlockSpec((d, LANE), lambda i: (0, 0)), pl.BlockSpec((1, LANE), lambda i: (0, 0))],
        out_specs=(pl.BlockSpec((MOE_ROWS, d), row), pl.BlockSpec((MOE_ROWS, LANE), row),
                   pl.BlockSpec((tm, LANE), row), pl.BlockSpec((None, 1, LANE), lambda i: (i, 0, 0))),
        compiler_params=_params("parallel"),
        name="moe_sort",
    )(x, g.reshape(1, d), wr, br)
    bounds = jnp.concatenate([off[:, 0, N_GROUPS:nr], jnp.full((t // tm, 1), MOE_ROWS, jnp.int32)], axis=1)
    return hn, comb, pos, bounds.reshape(-1)


def _moe_expert_kernel(bounds_ref, hn_ref, c_ref, wg_ref, wu_ref, wd_ref, y_ref):
    i, e = pl.program_id(0), pl.program_id(1)

    @pl.when(e == 0)
    def _():
        y_ref[...] = jnp.zeros_like(y_ref)

    expert_lane = N_GROUPS + e
    base = i * (N_EXPERTS + 1) + e
    start, end = bounds_ref[base], bounds_ref[base + 1]

    def sub_tile(want, size):
        r0 = pl.multiple_of(jnp.minimum(want, MOE_ROWS - size), ROW_ALIGN)
        rows = pl.ds(r0, size)
        x = hn_ref[rows, :]
        hg = jnp.dot(x, wg_ref[...], preferred_element_type=F32)
        hu = jnp.dot(x, wu_ref[...], preferred_element_type=F32)
        lane = lax.broadcasted_iota(jnp.int32, (size, LANE), 1)
        c = jnp.sum(jnp.where(lane == expert_lane, c_ref[rows, :], 0.0), axis=-1, keepdims=True)
        row = lax.broadcasted_iota(jnp.int32, (size, 1), 0) + r0
        c = jnp.where((row >= jnp.maximum(want, start)) & (row < end), c, 0.0)
        hid = (hg * jax.nn.sigmoid(hg)) * hu * c
        y_ref[rows, :] += jnp.dot(hid.astype(BF16), wd_ref[...], preferred_element_type=F32)

    first = (start // ROW_ALIGN) * ROW_ALIGN
    n_full = (end - first) // MOE_SUB
    rest = end - first - n_full * MOE_SUB

    def full(k, carry):
        sub_tile(first + k * MOE_SUB, MOE_SUB)
        return carry

    lax.fori_loop(0, n_full, full, 0)
    tail = first + n_full * MOE_SUB

    @pl.when(rest > MOE_SUB // 2)
    def _():
        sub_tile(tail, MOE_SUB)

    @pl.when((rest > 0) & (rest <= MOE_SUB // 2))
    def _():
        sub_tile(tail, MOE_SUB // 2)


def moe_experts(hn, comb, bounds, w_gate, w_up, w_down):
    rows, d = hn.shape
    ne, _, ff = w_gate.shape
    ex = lambda i, e, b: (e, 0, 0)
    once = pl.Buffered(1)
    return pl.pallas_call(
        _moe_expert_kernel,
        out_shape=jax.ShapeDtypeStruct((rows, d), F32),
        grid_spec=pltpu.PrefetchScalarGridSpec(
            num_scalar_prefetch=1,
            grid=(rows // MOE_ROWS, ne),
            in_specs=[pl.BlockSpec((MOE_ROWS, d), lambda i, e, b: (i, 0), pipeline_mode=once),
                      pl.BlockSpec((MOE_ROWS, LANE), lambda i, e, b: (i, 0)),
                      pl.BlockSpec((None, d, ff), ex), pl.BlockSpec((None, d, ff), ex),
                      pl.BlockSpec((None, ff, d), ex)],
            out_specs=pl.BlockSpec((MOE_ROWS, d), lambda i, e, b: (i, 0), pipeline_mode=once)),
        compiler_params=_params("parallel", "arbitrary"),
        name="moe_experts",
    )(bounds, hn, comb, w_gate, w_up, w_down)


def _moe_unsort_kernel(y_ref, pos_ref, x_ref, o_ref):
    perm_t = _pair_perm(pos_ref[...], 0, y_ref.shape[0])
    y = y_ref[...]
    hi = y.astype(BF16)
    lo = (y - hi.astype(F32)).astype(BF16)
    o_ref[...] = (x_ref[...] + jnp.dot(perm_t, hi, preferred_element_type=F32)
                  + jnp.dot(perm_t, lo, preferred_element_type=F32))


def moe_unsort(y, pos, x):
    t, d = x.shape
    tm, tn = MOE_TILE, d // 2
    blk = pl.BlockSpec((tm, tn), lambda i, j: (i, j))
    return pl.pallas_call(
        _moe_unsort_kernel,
        out_shape=jax.ShapeDtypeStruct((t, d), F32),
        grid=(t // tm, d // tn),
        in_specs=[pl.BlockSpec((MOE_ROWS, tn), lambda i, j: (i, j)),
                  pl.BlockSpec((tm, LANE), lambda i, j: (i, 0)), blk],
        out_specs=blk,
        compiler_params=_params("parallel", "arbitrary"),
        name="moe_unsort",
    )(y, pos, x)


def hier_moe(x, g, w_grp, b_grp, w_exp, b_exp, w_gate, w_up, w_down):
    hn, comb, pos, bounds = moe_sort(x, g, w_grp, b_grp, w_exp, b_exp)
    y = moe_experts(hn, comb, bounds, w_gate.astype(BF16), w_up.astype(BF16), w_down.astype(BF16))
    return moe_unsort(y, pos, x)


def _final_norm_kernel(x_ref, g_ref, o_ref):
    x = x_ref[...]
    o_ref[...] = x * lax.rsqrt(jnp.mean(x * x, axis=-1, keepdims=True) + EPS) * g_ref[...]


def final_norm(x, g, *, row0, rows, tm=1024):
    d = x.shape[1]
    return pl.pallas_call(
        _final_norm_kernel,
        out_shape=jax.ShapeDtypeStruct((rows, d), F32),
        grid=(rows // tm,),
        in_specs=[pl.BlockSpec((tm, d), lambda i: (i + row0 // tm, 0)), pl.BlockSpec((1, d), lambda i: (0, 0))],
        out_specs=pl.BlockSpec((tm, d), lambda i: (i, 0)),
        compiler_params=_params("parallel"),
        name="final_norm",
    )(x, g.reshape(1, d))


def even_layer(x, bsz, s, norm_g, rel_bias, w_in, w_out, w2_f, b_f, w2_b, b_b, onorm):
    n_pad = _round_up(EVEN_IN, LANE)
    w_in_p = jnp.pad(w_in, ((0, 0), (0, n_pad - EVEN_IN))).astype(BF16)
    proj = norm_linear(x, norm_g, w_in_p, tn_target=896).reshape(bsz, s, n_pad)
    tq = ATTN_TQ
    ya = attention(proj, proj, proj, heads=A_HEADS, dq=HEAD_DIM, dv=HEAD_DIM,
                   q_off=0, k_off=A_HEADS, v_off=2 * A_HEADS, scale=HEAD_DIM ** -0.5,
                   bias=dilated_bias_table(rel_bias, s, tq), tq=tq)
    q_col = 3 * A_WIDTH
    yb = gla_mixer(proj, w2_f, w2_b, b_f, b_b, onorm, q_col=q_col, k_col=q_col + B_KEYW,
                   v_col=q_col + 2 * B_KEYW, g_col=q_col + 2 * B_KEYW + B_WIDTH,
                   z_col=q_col + 2 * B_KEYW + 2 * B_WIDTH)
    t = bsz * s
    return out_proj(ya.reshape(t, A_WIDTH), yb.reshape(t, B_WIDTH), w_out.astype(BF16), x)


def _odd_columns(w_in, mu):
    c0 = C_IN
    cut = lambda u, lo, n: u[..., lo:lo + n]
    zpad = lambda u, n: jnp.pad(u, [(0, 0)] * (u.ndim - 1) + [(0, n)])
    off = np.cumsum((0,) + D_SPLITS)
    def rwkv_cols(u):
        parts = [cut(u, off[0], 3 * D_WIDTH), cut(u, off[6], D_G_RANK), cut(u, off[3], 2 * D_W_RANK),
                 cut(u, off[5], D_A_RANK)]
        u = jnp.concatenate(parts, axis=-1)
        return zpad(u, DC_PAD - u.shape[-1])
    w_kr = w_in[:, C_Q_RANK + C_KV_RANK:C_IN]
    w_all = jnp.concatenate([rwkv_cols(w_in[:, c0:]), w_in[:, :C_IN], _rot_half_cols(w_kr)], axis=1)
    n_pad = _round_up(w_all.shape[1], 9 * LANE)
    return zpad(w_all, n_pad - w_all.shape[1]).astype(BF16), rwkv_cols(mu)


def odd_layer(x, bsz, s, norm_g, w_in, w_out, q_norm, w_uq, kv_norm, w_ukv, mu, w0_f, w2_f, w0_b, w2_b,
              a0, a2, g2, k_k, k_a, r_k, ln_g, ln_b):
    t = bsz * s
    w_all, mu_cols = _odd_columns(w_in, mu)
    proj = norm_linear(x, norm_g, w_all, tn_target=1152)
    inv = 1.0 / (ROPE_THETA ** (jnp.arange(0, C_ROPE, 2, dtype=F32) / C_ROPE))
    ang = jnp.arange(s, dtype=F32)[:, None] * inv[None, :]
    cos = jnp.pad(jnp.concatenate([jnp.cos(ang)] * 2, axis=1), ((0, 0), (0, LANE - C_ROPE)), constant_values=1.0)
    sin = jnp.pad(jnp.concatenate([jnp.sin(ang)] * 2, axis=1), ((0, 0), (0, LANE - C_ROPE)))
    q, k, v = mla_up(proj, q_norm, w_uq, kv_norm, w_ukv, jnp.tile(cos, (bsz, 1)), jnp.tile(sin, (bsz, 1)),
                     col0=DC_PAD)
    r3 = lambda u: u.reshape(bsz, s, -1)
    yc = attention(r3(q), r3(k), r3(v), heads=C_HEADS, dq=C_QK, dv=C_V, q_off=0, k_off=0, v_off=0,
                   scale=(C_NOPE + C_ROPE) ** -0.5)
    yd = rwkv7_mixer(proj, bsz, s, mu_cols, w0_f, w2_f, w0_b, w2_b, a0, a2, g2, k_k, k_a, r_k, ln_g, ln_b)
    return out_proj(yc.reshape(t, C_WIDTH), yd, w_out.astype(BF16), x)


def kernel(x_prompt, x_sample, rel_bias, norm_mix, norm_ffn, norm_final, ev_w_in, ev_w_out, ev_gla_w2_f, ev_gla_b_f, ev_gla_w2_b, ev_gla_b_b, ev_gla_onorm, od_w_in, od_w_out, od_q_norm, od_w_uq, od_kv_norm, od_w_ukv, od_mu, od_w0_f, od_w2_f, od_w0_b, od_w2_b, od_a0, od_a2, od_g2, od_k_k, od_k_a, od_r_k, od_ln_g, od_ln_b, moe_w_grp, moe_b_grp, moe_w_exp, moe_b_exp, moe_w_gate, moe_w_up, moe_w_down):
    nb_p = x_prompt.shape[0]
    x = jnp.concatenate([x_prompt, x_sample], axis=0)
    bsz, s, d = x.shape
    x = x.reshape(bsz * s, d)
    for i in range(DEPTH):
        j = i // 2
        if i % 2 == 0:
            x = even_layer(x, bsz, s, norm_mix[i], rel_bias, ev_w_in[j], ev_w_out[j], ev_gla_w2_f[j],
                           ev_gla_b_f[j], ev_gla_w2_b[j], ev_gla_b_b[j], ev_gla_onorm[j])
        else:
            x = odd_layer(x, bsz, s, norm_mix[i], od_w_in[j], od_w_out[j], od_q_norm[j], od_w_uq[j],
                          od_kv_norm[j], od_w_ukv[j], od_mu[j], od_w0_f[j], od_w2_f[j], od_w0_b[j],
                          od_w2_b[j], od_a0[j], od_a2[j], od_g2[j], od_k_k[j], od_k_a[j], od_r_k[j],
                          od_ln_g[j], od_ln_b[j])
        x = hier_moe(x, norm_ffn[i], moe_w_grp[i], moe_b_grp[i], moe_w_exp[i], moe_b_exp[i],
                     moe_w_gate[i], moe_w_up[i], moe_w_down[i])
    y_p = final_norm(x, norm_final, row0=0, rows=nb_p * s)
    y_s = final_norm(x, norm_final, row0=nb_p * s, rows=(bsz - nb_p) * s)
    return (y_p.reshape(nb_p, s, d), y_s.reshape(bsz - nb_p, s, d))
```

```python
import functools

import jax, jax.numpy as jnp
from jax import lax
import numpy as np
from jax.experimental import pallas as pl
from jax.experimental.pallas import tpu as pltpu

F32, BF16 = jnp.float32, jnp.bfloat16

D_MODEL = 2048
DEPTH = 2
MIX_HALF = D_MODEL // 2
HEAD_DIM = 128
EPS = 1e-6
NEG = -1e30

A_HEADS = MIX_HALF // HEAD_DIM
A_WIDTH = A_HEADS * HEAD_DIM
A_PATTERNS = ((128, 1), (512, 4), (2048, 16))
N_BUCKETS = 32
MAX_DISTANCE = 1024

B_HEADS = 4
B_DV = MIX_HALF // B_HEADS
B_DK = B_DV // 2
B_WIDTH = B_HEADS * B_DV
B_KEYW = B_HEADS * B_DK
B_GATE_RANK = 16
B_GATE_TAU = 16.0
B_CHUNK = 64

C_HEADS = MIX_HALF // 128
C_Q_RANK = 512
C_KV_RANK = 256
C_NOPE = 128
C_ROPE = 64
C_V = 128
C_WIDTH = C_HEADS * C_V
C_QK = 256
ROPE_THETA = 10000.0

D_HEAD = 64
D_HEADS = MIX_HALF // D_HEAD
D_WIDTH = D_HEADS * D_HEAD
D_W_RANK = 64
D_A_RANK = 64
D_G_RANK = 128
D_LN_EPS = 64e-5
D_SPLITS = (D_WIDTH, D_WIDTH, D_WIDTH, D_W_RANK, D_W_RANK, D_A_RANK, D_G_RANK)
D_SHIFT = 3 * D_WIDTH + 2 * D_W_RANK + D_A_RANK + D_G_RANK

N_GROUPS = 4
EXPERTS_PER_GROUP = 4
N_EXPERTS = N_GROUPS * EXPERTS_PER_GROUP

EVEN_IN = 3 * A_WIDTH + 2 * B_KEYW + 2 * B_WIDTH + 2 * B_GATE_RANK
C_IN = C_Q_RANK + C_KV_RANK + C_ROPE

LANE = 128
VMEM_LIMIT = 52 * 1024 * 1024


def _params(*sem):
    return pltpu.CompilerParams(dimension_semantics=sem, vmem_limit_bytes=VMEM_LIMIT)


def _round_up(n, m):
    return -(-n // m) * m


NT_DIMS = (((1,), (1,)), ((), ()))
TN_DIMS = (((0,), (0,)), ((), ()))


def _split3(x):
    hi = x.astype(BF16)
    r1 = x - hi.astype(F32)
    mid = r1.astype(BF16)
    lo = (r1 - mid.astype(F32)).astype(BF16)
    return hi, mid, lo


def _pick_tile(n, target):
    best = LANE
    for t in range(LANE, target + 1, LANE):
        if n % t == 0:
            best = t
    return best


def _norm_linear_kernel(x_ref, g_ref, w_ref, o_ref, xn_ref):
    @pl.when(pl.program_id(1) == 0)
    def _():
        x = x_ref[...]
        y = x * lax.rsqrt(jnp.mean(x * x, axis=-1, keepdims=True) + EPS) * g_ref[...]
        xn_ref[...] = y.astype(BF16)

    o_ref[...] = jnp.dot(xn_ref[...], w_ref[...], preferred_element_type=F32)


def norm_linear(x, g, w, *, tm=1024, tn_target=1024):
    t, k = x.shape
    n = w.shape[1]
    tn = _pick_tile(n, tn_target)
    return pl.pallas_call(
        _norm_linear_kernel,
        out_shape=jax.ShapeDtypeStruct((t, n), F32),
        grid=(t // tm, n // tn),
        in_specs=[pl.BlockSpec((tm, k), lambda i, j: (i, 0)),
                  pl.BlockSpec((1, k), lambda i, j: (0, 0)),
                  pl.BlockSpec((k, tn), lambda i, j: (0, j))],
        out_specs=pl.BlockSpec((tm, tn), lambda i, j: (i, j)),
        scratch_shapes=[pltpu.VMEM((tm, k), BF16)],
        compiler_params=_params("parallel", "arbitrary"),
        name="norm_linear",
    )(x, g.reshape(1, k), w)


def _out_proj_kernel(a_ref, b_ref, wa_ref, wb_ref, x_ref, o_ref):
    acc = jnp.dot(a_ref[...], wa_ref[...], preferred_element_type=F32)
    acc += jnp.dot(b_ref[...], wb_ref[...], preferred_element_type=F32)
    o_ref[...] = x_ref[...] + acc


def out_proj(a, b, w, x, *, tm=1024, tn=512):
    t, ka = a.shape
    kb = b.shape[1]
    n = w.shape[1]
    return pl.pallas_call(
        _out_proj_kernel,
        out_shape=jax.ShapeDtypeStruct((t, n), F32),
        grid=(t // tm, n // tn),
        in_specs=[pl.BlockSpec((tm, ka), lambda i, j: (i, 0)),
                  pl.BlockSpec((tm, kb), lambda i, j: (i, 0)),
                  pl.BlockSpec((ka, tn), lambda i, j: (0, j)),
                  pl.BlockSpec((kb, tn), lambda i, j: (0, j)),
                  pl.BlockSpec((tm, tn), lambda i, j: (i, j))],
        out_specs=pl.BlockSpec((tm, tn), lambda i, j: (i, j)),
        compiler_params=_params("parallel", "arbitrary"),
        name="out_proj",
    )(a, b, w[:ka], w[ka:], x)


ATTN_KB = 256
ATTN_TQ = 512


def _attn_kernel(*refs, scale, has_bias):
    if has_bias:
        q_ref, k_ref, v_ref, bias_ref, o_ref, kb_ref, vb_ref = refs
    else:
        q_ref, k_ref, v_ref, o_ref, kb_ref, vb_ref = refs
    s_len = k_ref.shape[0]

    @pl.when(pl.program_id(2) == 0)
    def _():
        kb_ref[...] = k_ref[...].astype(BF16)
        vb_ref[...] = v_ref[...].astype(BF16)

    q = (q_ref[...] * scale).astype(BF16)
    blocks = [slice(j * ATTN_KB, (j + 1) * ATTN_KB) for j in range(s_len // ATTN_KB)]
    scores = []
    m = None
    for blk in blocks:
        sj = lax.dot_general(q, kb_ref[blk, :], NT_DIMS, preferred_element_type=F32)
        if has_bias:
            sj = sj + bias_ref[:, blk]
        mj = jnp.max(sj, axis=-1, keepdims=True)
        m = mj if m is None else jnp.maximum(m, mj)
        scores.append(sj)
    o = den = None
    for blk, sj in zip(blocks, scores):
        p = jnp.exp(sj - m)
        dj = jnp.sum(p, axis=-1, keepdims=True)
        oj = jnp.dot(p.astype(BF16), vb_ref[blk, :], preferred_element_type=F32)
        o, den = (oj, dj) if o is None else (o + oj, den + dj)
    o_ref[...] = (o / den).astype(o_ref.dtype)


def attention(q, k, v, *, heads, dq, dv, q_off, k_off, v_off, scale, bias=None, tq=ATTN_TQ):
    b, s, _ = q.shape
    nq = s // tq
    in_specs = [pl.BlockSpec((None, tq, dq), lambda bi, h, qi: (bi, qi, q_off + h)),
                pl.BlockSpec((None, s, dq), lambda bi, h, qi: (bi, 0, k_off + h)),
                pl.BlockSpec((None, s, dv), lambda bi, h, qi: (bi, 0, v_off + h))]
    args = [q, k, v]
    if bias is not None:
        in_specs.append(pl.BlockSpec((None, None, tq, s), lambda bi, h, qi: (h, qi, 0, 0)))
        args.append(bias)
    return pl.pallas_call(
        functools.partial(_attn_kernel, scale=scale, has_bias=bias is not None),
        out_shape=jax.ShapeDtypeStruct((b, s, heads * dv), BF16),
        grid=(b, heads, nq),
        in_specs=in_specs,
        out_specs=pl.BlockSpec((None, tq, dv), lambda bi, h, qi: (bi, qi, h)),
        scratch_shapes=[pltpu.VMEM((s, dq), BF16), pltpu.VMEM((s, dv), BF16)],
        compiler_params=_params("parallel", "parallel", "arbitrary"),
        name="attention_bias" if bias is not None else "attention",
    )(*args)


def _t5_bucket(rel):
    half = N_BUCKETS // 2
    exact = half // 2
    n = np.abs(rel)
    large = exact + (np.log(np.maximum(n, 1) / exact) / np.log(MAX_DISTANCE / exact) * (half - exact)).astype(np.int64)
    large = np.minimum(large, half - 1)
    return ((rel > 0) * half + np.where(n < exact, n, large)).astype(np.int32)


def dilated_bias_table(rel_bias, s, tq):
    heads = rel_bias.shape[1]
    d = np.arange(-(s - 1), s)
    count = np.zeros(d.shape, np.float32)
    for window, dil in A_PATTERNS:
        count += ((d % dil == 0) & (np.abs(d) <= (window // (2 * dil)) * dil)).astype(np.float32)
    logc = np.where(count > 0, np.log(np.maximum(count, 1.0)), NEG).astype(np.float32)
    onehot = (_t5_bucket(d)[:, None] == np.arange(N_BUCKETS)[None, :]).astype(np.float32)
    line = jnp.transpose(jnp.dot(onehot, rel_bias.astype(F32), precision=lax.Precision.HIGHEST)) + logc[None]
    width = 2 * s
    line = jnp.pad(line, ((0, 0), (0, width - line.shape[1])))[:, None, :]
    nq = s // tq
    return pl.pallas_call(
        functools.partial(_skew_kernel, tq=tq, nq=nq),
        out_shape=jax.ShapeDtypeStruct((heads, nq, tq, s), F32),
        grid=(heads, nq),
        in_specs=[pl.BlockSpec((None, 1, width), lambda h, qi: (h, 0, 0))],
        out_specs=pl.BlockSpec((None, None, tq, s), lambda h, qi: (h, qi, 0, 0)),
        compiler_params=_params("parallel", "arbitrary"),
        name="bias_skew",
    )(line)


def _skew_kernel(line_ref, o_ref, *, tq, nq):
    width = line_ref.shape[1]
    first = (nq - 1 - pl.program_id(1)) * tq
    x = jnp.broadcast_to(line_ref[...], (tq, width))
    x = pltpu.roll(x, width - (tq - 1) - first, 1, stride=1, stride_axis=0)
    o_ref[...] = x[:, :o_ref.shape[1]]


GLA_UNROLL = 8


def _gla_kernel(q_ref, k_ref, v_ref, g_ref, z_ref, w2f_ref, w2b_ref, bf_ref, bb_ref, on_ref, o_ref,
                la_ref, acc_ref, qcat_ref, upd_ref, dec_ref, scat_ref, st_ref):
    s_len = q_ref.shape[0]
    c = B_CHUNK
    nchunk = s_len // c
    z = z_ref[...].astype(BF16)
    gate = lambda w2_ref, b_ref: jax.nn.log_sigmoid(
        jnp.dot(z, w2_ref[...], preferred_element_type=F32) + b_ref[...]) * (1.0 / B_GATE_TAU)
    la_ref[0] = gate(w2f_ref, bf_ref)
    la_ref[1] = gate(w2b_ref, bb_ref)

    ri = lax.broadcasted_iota(jnp.int32, (c, c), 0)
    ci = lax.broadcasted_iota(jnp.int32, (c, c), 1)
    keep = (ri >= ci, ri <= ci)
    tri3 = tuple(jnp.concatenate([kp.astype(BF16)] * 3, axis=1) for kp in keep)

    def chunk_rows(n):
        return pl.ds(pl.multiple_of(n * c, c), c)

    def pass1(i, carry):
        units = [(i * GLA_UNROLL + u, d) for u in range(GLA_UNROLL) for d in range(2)]
        st = []
        for n, d in units:
            rows = chunk_rows(n)
            gcum = jnp.dot(tri3[d], jnp.concatenate(_split3(la_ref[d, rows, :]), axis=0),
                           preferred_element_type=F32)
            st.append(dict(rows=rows, gcum=gcum))
        for (n, d), c in zip(units, st):
            gcum = c["gcum"]
            gend = gcum[0:1] if d == 1 else gcum[B_CHUNK - 1:B_CHUNK]
            kc = k_ref[c["rows"], :]
            c["q_in"] = (q_ref[c["rows"], :] * (B_DK ** -0.5) * jnp.exp(gcum)).astype(BF16)
            c["k_out"] = (kc * jnp.exp(gend - gcum)).astype(BF16)
            c["att"] = lax.dot_general(c["q_in"], (kc * jnp.exp(-gcum)).astype(BF16), NT_DIMS,
                                       preferred_element_type=F32)
            dec_ref[d, n] = jnp.broadcast_to(jnp.exp(gend), (8, B_DK))
        for (n, d), c in zip(units, st):
            vc = v_ref[c["rows"], :].astype(BF16)
            c["o"] = jnp.dot(jnp.where(keep[d], c["att"], 0.0).astype(BF16), vc, preferred_element_type=F32)
            upd_ref[d, n] = lax.dot_general(vc, c["k_out"], TN_DIMS, preferred_element_type=F32)
            qcat_ref[c["rows"], d * B_DK:(d + 1) * B_DK] = c["q_in"]
        for u in range(GLA_UNROLL):
            acc_ref[st[2 * u]["rows"], :] = st[2 * u]["o"] + st[2 * u + 1]["o"]
        return carry

    lax.fori_loop(0, nchunk // GLA_UNROLL, pass1, 0)

    st_ref[...] = jnp.zeros_like(st_ref)

    def pass2(n, carry):
        for d, m in ((0, n), (1, nchunk - 1 - n)):
            state = st_ref[d]
            scat_ref[m, :, d * B_DK:(d + 1) * B_DK] = state.astype(BF16)
            st_ref[d] = state * dec_ref[d, m][0:1] + upd_ref[d, m]
        return carry

    lax.fori_loop(0, nchunk, pass2, 0)

    def pass3(i, carry):
        rows = [chunk_rows(i * GLA_UNROLL + u) for u in range(GLA_UNROLL)]
        outs = [acc_ref[r, :] + lax.dot_general(qcat_ref[r, :], scat_ref[i * GLA_UNROLL + u], NT_DIMS,
                                                preferred_element_type=F32) for u, r in enumerate(rows)]
        for r, o in zip(rows, outs):
            o = o * lax.rsqrt(jnp.mean(o * o, axis=-1, keepdims=True) + EPS) * on_ref[...]
            g = g_ref[r, :]
            o_ref[r, :] = (o * (g * jax.nn.sigmoid(g))).astype(o_ref.dtype)
        return carry

    lax.fori_loop(0, nchunk // GLA_UNROLL, pass3, 0)


def gla_mixer(proj, w2f, w2b, b_f, b_b, onorm, *, q_col, k_col, v_col, g_col, z_col):
    b, s, _ = proj.shape
    hm = lambda blk: (lambda bi, h: (bi, 0, blk + h))
    w2f_p = jnp.zeros((LANE, B_KEYW), F32).at[:B_GATE_RANK].set(w2f).astype(BF16)
    w2b_p = jnp.zeros((LANE, B_KEYW), F32).at[B_GATE_RANK:2 * B_GATE_RANK].set(w2b).astype(BF16)
    return pl.pallas_call(
        _gla_kernel,
        out_shape=jax.ShapeDtypeStruct((b, s, B_WIDTH), BF16),
        grid=(b, B_HEADS),
        in_specs=[pl.BlockSpec((None, s, B_DK), hm(q_col // B_DK)),
                  pl.BlockSpec((None, s, B_DK), hm(k_col // B_DK)),
                  pl.BlockSpec((None, s, B_DV), hm(v_col // B_DV)),
                  pl.BlockSpec((None, s, B_DV), hm(g_col // B_DV)),
                  pl.BlockSpec((None, s, LANE), lambda bi, h: (bi, 0, z_col // LANE)),
                  pl.BlockSpec((LANE, B_DK), lambda bi, h: (0, h)),
                  pl.BlockSpec((LANE, B_DK), lambda bi, h: (0, h)),
                  pl.BlockSpec((1, B_DK), lambda bi, h: (0, h)),
                  pl.BlockSpec((1, B_DK), lambda bi, h: (0, h)),
                  pl.BlockSpec((1, B_DV), lambda bi, h: (0, 0))],
        out_specs=pl.BlockSpec((None, s, B_DV), lambda bi, h: (bi, 0, h)),
        scratch_shapes=[pltpu.VMEM((2, s, B_DK), F32),
                        pltpu.VMEM((s, B_DV), F32),
                        pltpu.VMEM((s, 2 * B_DK), BF16),
                        pltpu.VMEM((2, s // B_CHUNK, B_DV, B_DK), F32),
                        pltpu.VMEM((2, s // B_CHUNK, 8, B_DK), F32),
                        pltpu.VMEM((s // B_CHUNK, B_DV, 2 * B_DK), BF16),
                        pltpu.VMEM((2, B_DV, B_DK), F32)],
        compiler_params=_params("parallel", "arbitrary"),
        name="gla_mixer",
    )(proj, proj, proj, proj, proj, w2f_p, w2b_p, b_f.reshape(1, -1), b_b.reshape(1, -1), onorm.reshape(1, -1))


def _mla_up_kernel(cq_ref, ckv_ref, kr_ref, qn_ref, kvn_ref, wq_ref, wqr_ref, wkv_ref, cos_ref, sin_ref,
                   q_ref, k_ref, v_ref):
    def rms(x, g):
        return (x * lax.rsqrt(jnp.mean(x * x, axis=-1, keepdims=True) + EPS) * g).astype(BF16)

    cq = rms(cq_ref[...], qn_ref[...])
    ckv = rms(ckv_ref[...], kvn_ref[...])
    cos, sin = cos_ref[...], sin_ref[...]
    kr = kr_ref[...]
    k_rope = kr * cos + pltpu.roll(kr, LANE - C_ROPE, 1) * sin
    lane = lax.broadcasted_iota(jnp.int32, k_rope.shape, 1)
    k_rope = jnp.where(lane < C_ROPE, k_rope, 0.0)
    for h in range(C_HEADS):
        q = jnp.dot(cq, wq_ref[:, h * C_QK:(h + 1) * C_QK], preferred_element_type=F32)
        qp = jnp.dot(cq, wqr_ref[:, h * LANE:(h + 1) * LANE], preferred_element_type=F32)
        q_ref[:, h * C_QK:h * C_QK + C_NOPE] = q[:, :C_NOPE]
        q_ref[:, h * C_QK + C_NOPE:(h + 1) * C_QK] = q[:, C_NOPE:] * cos + qp * sin
        kv = jnp.dot(ckv, wkv_ref[:, h * 2 * LANE:(h + 1) * 2 * LANE], preferred_element_type=F32)
        k_ref[:, h * C_QK:h * C_QK + C_NOPE] = kv[:, :C_NOPE]
        k_ref[:, h * C_QK + C_NOPE:(h + 1) * C_QK] = k_rope
        v_ref[:, h * C_V:(h + 1) * C_V] = kv[:, C_NOPE:]


def _rot_half_cols(w):
    half = w.shape[-1] // 2
    return jnp.concatenate([-w[..., half:], w[..., :half]], axis=-1)


def mla_up(proj, q_norm, w_uq, kv_norm, w_ukv, cos, sin, *, col0, tm=512):
    t = proj.shape[0]
    wq = w_uq.reshape(C_Q_RANK, C_HEADS, C_NOPE + C_ROPE)
    wq_main = jnp.pad(wq, ((0, 0), (0, 0), (0, C_QK - C_NOPE - C_ROPE))).reshape(C_Q_RANK, C_HEADS * C_QK)
    wq_rot = jnp.pad(_rot_half_cols(wq[..., C_NOPE:]), ((0, 0), (0, 0), (0, LANE - C_ROPE)))
    wq_rot = wq_rot.reshape(C_Q_RANK, C_HEADS * LANE)
    row = lambda i: (i, 0)
    full = lambda arr: pl.BlockSpec(arr.shape, lambda i: (0, 0))
    g_q, g_kv = q_norm.reshape(1, -1), kv_norm.reshape(1, -1)
    wq_main, wq_rot, wkv = wq_main.astype(BF16), wq_rot.astype(BF16), w_ukv.astype(BF16)
    return pl.pallas_call(
        _mla_up_kernel,
        out_shape=(jax.ShapeDtypeStruct((t, C_HEADS * C_QK), F32),
                   jax.ShapeDtypeStruct((t, C_HEADS * C_QK), F32),
                   jax.ShapeDtypeStruct((t, C_WIDTH), F32)),
        grid=(t // tm,),
        in_specs=[pl.BlockSpec((tm, C_Q_RANK), lambda i: (i, col0 // C_Q_RANK)),
                  pl.BlockSpec((tm, C_KV_RANK), lambda i: (i, (col0 + C_Q_RANK) // C_KV_RANK)),
                  pl.BlockSpec((tm, LANE), lambda i: (i, (col0 + C_Q_RANK + C_KV_RANK) // LANE)),
                  full(g_q), full(g_kv), full(wq_main), full(wq_rot), full(wkv),
                  pl.BlockSpec((tm, LANE), row), pl.BlockSpec((tm, LANE), row)],
        out_specs=(pl.BlockSpec((tm, C_HEADS * C_QK), row),
                   pl.BlockSpec((tm, C_HEADS * C_QK), row),
                   pl.BlockSpec((tm, C_WIDTH), row)),
        compiler_params=_params("parallel"),
        name="mla_up",
    )(proj, proj, proj, g_q, g_kv, wq_main, wq_rot, wkv, cos, sin)


RG = 4
RGW = RG * D_HEAD
RCH = 64
DC_R, DC_K, DC_V = 0, D_WIDTH, 2 * D_WIDTH
DC_ZG = 3 * D_WIDTH
DC_ZW = DC_ZG + D_G_RANK
DC_ZA = DC_ZW + 2 * D_W_RANK
DC_PAD = 7 * 512


def _head_sums(x, bo):
    return jnp.concatenate(
        [jnp.dot(x[:, RGW * g:RGW * (g + 1)], bo, preferred_element_type=F32, precision=lax.Precision.HIGHEST)
         for g in range(x.shape[1] // RGW)], axis=1)


def _block_ones():
    i = np.arange(RGW)
    return jnp.asarray((i[:, None] // D_HEAD) == (i[None, :] // D_HEAD), F32)


def _rwkv_prep_kernel(x_ref, xp_ref, xn_ref, mu_ref, w2f_ref, w2b_ref, a2_ref, g2_ref, w0f_ref, w0b_ref,
                      a0_ref, kk_ref, ka_ref, rk_ref, bo_ref,
                      r_ref, k_ref, v_ref, a_ref, b_ref, lwf_ref, lwb_ref, g_ref, bonus_ref, *, tiles_per_seq):
    i = pl.program_id(0) % tiles_per_seq
    x = x_ref[...]
    tm = x.shape[0]
    row = lax.broadcasted_iota(jnp.int32, x.shape, 0)
    prev_row = jnp.where(i == 0, 0.0, xp_ref[7:8, :])
    next_row = jnp.where(i == tiles_per_seq - 1, 0.0, xn_ref[0:1, :])
    prev = jnp.where(row == 0, prev_row, pltpu.roll(x, 1, 0))
    nxt = jnp.where(row == tm - 1, next_row, pltpu.roll(x, tm - 1, 0))
    x = x + mu_ref[...] * (0.5 * (prev + nxt) - x)
    r, k, v = x[:, DC_R:DC_R + D_WIDTH], x[:, DC_K:DC_K + D_WIDTH], x[:, DC_V:DC_V + D_WIDTH]
    zg = x[:, DC_ZG:DC_ZG + LANE]
    zw = x[:, DC_ZW:DC_ZW + LANE]
    za = x[:, DC_ZA:DC_ZA + LANE]
    tz = jnp.tanh(zw).astype(BF16)
    log_decay = lambda w0_ref, w2_ref: -np.exp(-0.5).astype(np.float32) * jax.nn.sigmoid(
        w0_ref[...] + jnp.dot(tz, w2_ref[...], preferred_element_type=F32))
    lwf_ref[...] = log_decay(w0f_ref, w2f_ref)
    lwb_ref[...] = log_decay(w0b_ref, w2b_ref)
    ag = jax.nn.sigmoid(a0_ref[...] + jnp.dot(za.astype(BF16), a2_ref[...], preferred_element_type=F32))
    g_ref[...] = jnp.dot(jax.nn.sigmoid(zg).astype(BF16), g2_ref[...], preferred_element_type=F32)
    bo = bo_ref[...]
    kk = k * kk_ref[...]
    kk = kk / jnp.maximum(jnp.sqrt(_head_sums(kk * kk, bo)), 1e-12)
    k = k * (1.0 + (ag - 1.0) * ka_ref[...])
    r_ref[...] = r
    k_ref[...] = k
    v_ref[...] = v
    a_ref[...] = -kk
    b_ref[...] = kk * ag
    bonus_ref[...] = _head_sums(r * k * rk_ref[...], bo) * v


def rwkv_prep(proj, mu, w0_f, w2_f, w0_b, w2_b, a0, a2, g2, k_k, k_a, r_k, *, seq, tm=256):
    t = proj.shape[0]
    tiles_per_seq = seq // tm
    hb = tm // 8
    nblk8 = t // 8
    pad_rows = lambda w, lo: jnp.zeros((LANE, D_WIDTH), F32).at[lo:lo + w.shape[0]].set(w).astype(BF16)
    vec = lambda u: u.reshape(1, -1)
    consts = [vec(mu), pad_rows(w2_f, 0), pad_rows(w2_b, D_W_RANK), pad_rows(a2, 0), g2.astype(BF16),
              vec(w0_f), vec(w0_b), vec(a0), vec(k_k), vec(k_a), vec(r_k), _block_ones()]
    full = lambda arr: pl.BlockSpec(arr.shape, lambda i: (0, 0))
    out_spec = pl.BlockSpec((tm, D_WIDTH), lambda i: (i, 0))
    return pl.pallas_call(
        functools.partial(_rwkv_prep_kernel, tiles_per_seq=tiles_per_seq),
        out_shape=tuple(jax.ShapeDtypeStruct((t, D_WIDTH), F32) for _ in range(9)),
        grid=(t // tm,),
        in_specs=[pl.BlockSpec((tm, DC_PAD), lambda i: (i, 0)),
                  pl.BlockSpec((8, DC_PAD), lambda i: (jnp.maximum(i * hb - 1, 0), 0)),
                  pl.BlockSpec((8, DC_PAD), lambda i: (jnp.minimum((i + 1) * hb, nblk8 - 1), 0))]
                 + [full(c) for c in consts],
        out_specs=tuple(out_spec for _ in range(9)),
        compiler_params=_params("parallel"),
        name="rwkv_prep",
    )(proj, proj, proj, *consts)


def _rwkv_chunk_kernel(*refs, ngroups):
    ins, (yf_ref, yb_ref, mt_ref) = refs[:12], refs[12:]

    @pl.when(pl.program_id(1) == 0)
    def _():
        mt_ref[...] = jnp.zeros_like(mt_ref)

    row = lax.broadcasted_iota(jnp.int32, (RCH, RGW), 0)
    col = lax.broadcasted_iota(jnp.int32, (RCH, RGW), 1) & (RCH - 1)
    bdmask = (lax.broadcasted_iota(jnp.int32, (RGW, RGW), 0) // D_HEAD
              == lax.broadcasted_iota(jnp.int32, (RGW, RGW), 1) // D_HEAD)
    tr = lax.broadcasted_iota(jnp.int32, (RCH, RCH), 0)
    tc = lax.broadcasted_iota(jnp.int32, (RCH, RCH), 1)
    zero = jnp.zeros((), F32)

    def bd(z):
        zb = z.astype(BF16)
        return jnp.where(bdmask, jnp.concatenate([zb] * RG, axis=0), jnp.zeros((), BF16))

    def mm(x, y, dims=None):
        x = x.astype(BF16)
        if dims is None:
            return jnp.dot(x, y, preferred_element_type=F32)
        return lax.dot_general(x, y, dims, preferred_element_type=F32)

    chains = [(d, g) for d in range(2) for g in range(ngroups)]
    st = []
    for d, g in chains:
        backward = d == 1
        r_ref, k_ref, v_ref, a_ref, b_ref, lw_ref = ins[6 * d:6 * d + 6]
        tri = ((tc >= tr) if backward else (tc <= tr)).astype(BF16)
        sl = slice(RGW * g, RGW * (g + 1))
        r, k, v, a, b, lw = (ref[:, sl] for ref in (r_ref, k_ref, v_ref, a_ref, b_ref, lw_ref))
        lam = jnp.dot(jnp.concatenate([tri] * 3, axis=1), jnp.concatenate(_split3(lw), axis=0),
                      preferred_element_type=F32)
        lamc = lam[0:1] if backward else lam[RCH - 1:RCH]
        e_inv = jnp.exp(-lam)
        e_out = jnp.exp(lamc - lam)
        ar = jnp.concatenate([a * jnp.exp(lam - lw), r * jnp.exp(lam)], axis=0).astype(BF16)
        bk = jnp.concatenate([b * e_out, k * e_out], axis=0).astype(BF16)
        st.append(dict(ar=ar, bk=bk, v=v, lamc=lamc, sl=sl,
                       gb=mm(ar, bd(b * e_inv), NT_DIMS), gk=mm(ar, bd(k * e_inv), NT_DIMS)))
    for (d, g), c in zip(chains, st):
        strict = (col > row) if d == 1 else (col < row)
        incl = (col >= row) if d == 1 else (col <= row)
        c["lp"] = jnp.where(strict, c["gb"][:RCH], zero)
        lak = jnp.where(strict, c["gk"][:RCH], zero)
        c["grb"] = jnp.where(incl, c["gb"][RCH:], zero).astype(BF16)
        c["grk"] = jnp.where(incl, c["gk"][RCH:], zero).astype(BF16)
        c["mt"] = mt_ref[d, g]
        amrm = mm(c["ar"], c["mt"].astype(BF16), NT_DIMS)
        c["bdv"] = bd(c["v"])
        c["u"] = amrm[:RCH] + mm(lak, c["bdv"])
        c["rm"] = amrm[RCH:]
    for rnd in range(6):
        for c in st:
            lpb = c["lp"].astype(BF16)
            c["u"] = c["u"] + mm(lpb, bd(c["u"]))
            if rnd < 5:
                c["lp"] = mm(lpb, bd(c["lp"]))
    for (d, g), c in zip(chains, st):
        y_ref = yb_ref if d == 1 else yf_ref
        y_ref[:, c["sl"]] = c["rm"] + mm(c["grb"], bd(c["u"])) + mm(c["grk"], c["bdv"])
        uv = jnp.concatenate([c["u"], c["v"]], axis=0).astype(BF16)
        upd = lax.dot_general(uv, c["bk"], TN_DIMS, preferred_element_type=F32)
        mt_ref[d, g] = c["mt"] * jnp.exp(c["lamc"]) + jnp.where(bdmask, upd, zero)


def rwkv_chunked(r, k, v, a, b, lwf, lwb):
    bsz, s, wd = r.shape
    nc = s // RCH
    fspec = pl.BlockSpec((None, RCH, wd), lambda bi, n: (bi, n, 0))
    bspec = pl.BlockSpec((None, RCH, wd), lambda bi, n: (bi, nc - 1 - n, 0))
    return pl.pallas_call(
        functools.partial(_rwkv_chunk_kernel, ngroups=wd // RGW),
        out_shape=(jax.ShapeDtypeStruct((bsz, s, wd), F32), jax.ShapeDtypeStruct((bsz, s, wd), F32)),
        grid=(bsz, nc),
        in_specs=[fspec] * 6 + [bspec] * 6,
        out_specs=(fspec, bspec),
        scratch_shapes=[pltpu.VMEM((2, wd // RGW, RGW, RGW), F32)],
        compiler_params=_params("parallel", "arbitrary"),
        name="rwkv_chunked",
    )(r, k, v, a, b, lwf, r, k, v, a, b, lwb)


def _rwkv_post_kernel(yf_ref, yb_ref, bonus_ref, g_ref, lng_ref, lnb_ref, bo_ref, o_ref):
    bo = bo_ref[...]
    y = yf_ref[...] + yb_ref[...]
    yc = y - _head_sums(y, bo) * (1.0 / D_HEAD)
    var = _head_sums(yc * yc, bo) * (1.0 / D_HEAD)
    y = yc * lax.rsqrt(var + D_LN_EPS) * lng_ref[...] + lnb_ref[...]
    o_ref[...] = ((y + bonus_ref[...]) * g_ref[...]).astype(o_ref.dtype)


def rwkv_post(yf, yb, bonus, g, ln_g, ln_b, *, tm=512):
    t, wd = yf.shape
    row = pl.BlockSpec((tm, wd), lambda i: (i, 0))
    vec = pl.BlockSpec((1, wd), lambda i: (0, 0))
    bo = _block_ones()
    return pl.pallas_call(
        _rwkv_post_kernel,
        out_shape=jax.ShapeDtypeStruct((t, wd), BF16),
        grid=(t // tm,),
        in_specs=[row, row, row, row, vec, vec, pl.BlockSpec(bo.shape, lambda i: (0, 0))],
        out_specs=row,
        compiler_params=_params("parallel"),
        name="rwkv_post",
    )(yf, yb, bonus, g, ln_g.reshape(1, wd), ln_b.reshape(1, wd), bo)


def rwkv7_mixer(proj, bsz, s, mu, w0_f, w2_f, w0_b, w2_b, a0, a2, g2, k_k, k_a, r_k, ln_g, ln_b):
    r, k, v, a, b, lwf, lwb, g, bonus = rwkv_prep(proj, mu, w0_f, w2_f, w0_b, w2_b, a0, a2, g2, k_k, k_a,
                                                  r_k.reshape(-1), seq=s)
    r3 = lambda u: u.reshape(bsz, s, D_WIDTH)
    yf, yb = rwkv_chunked(r3(r), r3(k), r3(v), r3(a), r3(b), r3(lwf), r3(lwb))
    return rwkv_post(yf.reshape(-1, D_WIDTH), yb.reshape(-1, D_WIDTH), bonus, g, ln_g, ln_b)


MOE_TILE = 1024
MOE_SUB = 256
ROW_ALIGN = 16
MOE_PACK = 2


def _route(logit):
    lane = lax.broadcasted_iota(jnp.int32, logit.shape, 1)
    first_at = lambda mask: jnp.min(jnp.where(mask, lane, jnp.int32(LANE)), axis=-1, keepdims=True)
    is_grp = lane < N_GROUPS
    gl = jnp.where(is_grp, logit, NEG)
    gmax = jnp.max(gl, axis=-1, keepdims=True)
    p_grp = 1.0 / jnp.sum(jnp.where(is_grp, jnp.exp(gl - gmax), 0.0), axis=-1, keepdims=True)
    i_grp = first_at(is_grp & (gl == gmax))
    lo = N_GROUPS + i_grp * EXPERTS_PER_GROUP
    in_grp = (lane >= lo) & (lane < lo + EXPERTS_PER_GROUP)
    el = jnp.where(in_grp, logit, NEG)
    l1 = jnp.max(el, axis=-1, keepdims=True)
    i1 = first_at(in_grp & (el == l1))
    rest = in_grp & (lane != i1)
    el2 = jnp.where(rest, logit, NEG)
    l2 = jnp.max(el2, axis=-1, keepdims=True)
    i2 = first_at(rest & (el2 == l2))
    e2 = jnp.exp(l2 - l1)
    w1 = p_grp / (1.0 + e2)
    w2 = p_grp * e2 / (1.0 + e2)
    return i_grp, jnp.where(lane == i1, w1, jnp.where(lane == i2, w2, 0.0))


def _moe_sort_kernel(x_ref, g_ref, wr_ref, br_ref, hn_ref, comb_ref, pos_ref, off_ref):
    x = x_ref[...]
    tm = x.shape[0]
    hn = (x * lax.rsqrt(jnp.mean(x * x, axis=-1, keepdims=True) + EPS) * g_ref[...]).astype(BF16)
    wr, br = wr_ref[...], br_ref[...]
    i_grp, _ = _route(jnp.dot(hn, wr, preferred_element_type=F32) + br)
    lane = lax.broadcasted_iota(jnp.int32, (tm, LANE), 1)
    onehot = (lane == i_grp).astype(F32)
    ri = lax.broadcasted_iota(jnp.int32, (tm, tm), 0)
    ci = lax.broadcasted_iota(jnp.int32, (tm, tm), 1)
    earlier = jnp.dot((ci < ri).astype(BF16), onehot.astype(BF16), preferred_element_type=F32)
    cnt = jnp.broadcast_to(jnp.sum(onehot, axis=0, keepdims=True), (8, LANE))
    lane8 = lax.broadcasted_iota(jnp.int32, (8, LANE), 1)
    start = jnp.zeros((8, LANE), F32)
    for sh in range(1, N_GROUPS):
        start += jnp.where(lane8 >= sh, pltpu.roll(cnt, sh, 1), 0.0)
    off_ref[...] = start[0:1].astype(jnp.int32)
    pos = jnp.sum(onehot * (start[0:1] + earlier), axis=-1, keepdims=True).astype(jnp.int32)
    pos_ref[...] = pos
    perm_t = (ci == pos).astype(BF16)
    hn_s = lax.dot_general(perm_t, hn, TN_DIMS, preferred_element_type=F32).astype(BF16)
    hn_ref[...] = hn_s
    _, comb = _route(jnp.dot(hn_s, wr, preferred_element_type=F32) + br)
    comb_ref[...] = comb


def moe_sort(x, g, w_grp, b_grp, w_exp, b_exp):
    t, d = x.shape
    tm = MOE_TILE
    nr = N_GROUPS + N_EXPERTS
    wr = jnp.pad(jnp.concatenate([w_grp, w_exp], axis=1), ((0, 0), (0, LANE - nr))).astype(BF16)
    br = jnp.pad(jnp.concatenate([b_grp, b_exp]), (0, LANE - nr)).reshape(1, LANE)
    row = lambda i: (i, 0)
    hn, comb, pos, off = pl.pallas_call(
        _moe_sort_kernel,
        out_shape=(jax.ShapeDtypeStruct((t, d), BF16), jax.ShapeDtypeStruct((t, LANE), F32),
                   jax.ShapeDtypeStruct((t, 1), jnp.int32), jax.ShapeDtypeStruct((t // tm, 1, LANE), jnp.int32)),
        grid=(t // tm,),
        in_specs=[pl.BlockSpec((tm, d), row), pl.BlockSpec((1, d), lambda i: (0, 0)),
                  pl.BlockSpec((d, LANE), lambda i: (0, 0)), pl.BlockSpec((1, LANE), lambda i: (0, 0))],
        out_specs=(pl.BlockSpec((tm, d), row), pl.BlockSpec((tm, LANE), row), pl.BlockSpec((tm, 1), row),
                   pl.BlockSpec((None, 1, LANE), lambda i: (i, 0, 0))),
        compiler_params=_params("parallel"),
        name="moe_sort",
    )(x, g.reshape(1, d), wr, br)
    bounds = jnp.concatenate([off[:, 0, :N_GROUPS], jnp.full((t // tm, 1), tm, jnp.int32)], axis=1)
    return hn, comb, pos, bounds.reshape(-1)


def _moe_group_kernel(bounds_ref, hn_ref, c_ref, wg_ref, wu_ref, wd_ref, y_ref):
    i, g, j = pl.program_id(0), pl.program_id(1), pl.program_id(2)

    @pl.when((g == 0) & (j == 0))
    def _():
        y_ref[...] = jnp.zeros_like(y_ref)

    expert_lane = N_GROUPS + g * EXPERTS_PER_GROUP + j

    def sub_tile(want, size):
        r0 = pl.multiple_of(jnp.minimum(want, lo + MOE_TILE - size), ROW_ALIGN)
        rows = pl.ds(r0, size)
        x = hn_ref[rows, :]
        hg = jnp.dot(x, wg_ref[...], preferred_element_type=F32)
        hu = jnp.dot(x, wu_ref[...], preferred_element_type=F32)
        lane = lax.broadcasted_iota(jnp.int32, (size, LANE), 1)
        c = jnp.sum(jnp.where(lane == expert_lane, c_ref[rows, :], 0.0), axis=-1, keepdims=True)
        row = lax.broadcasted_iota(jnp.int32, (size, 1), 0)
        c = jnp.where(row + r0 >= want, c, 0.0)
        hid = (hg * jax.nn.sigmoid(hg)) * hu * c
        y_ref[rows, :] += jnp.dot(hid.astype(BF16), wd_ref[...], preferred_element_type=F32)

    for half in range(MOE_PACK):
        base = (i * MOE_PACK + half) * (N_GROUPS + 1) + g
        lo = half * MOE_TILE
        start, end = lo + bounds_ref[base], lo + bounds_ref[base + 1]
        first = (start // ROW_ALIGN) * ROW_ALIGN
        n_full = (end - first) // MOE_SUB
        rest = end - first - n_full * MOE_SUB

        def full(k, carry, first=first):
            sub_tile(first + k * MOE_SUB, MOE_SUB)
            return carry

        lax.fori_loop(0, n_full, full, 0)
        tail = first + n_full * MOE_SUB

        @pl.when(rest > MOE_SUB // 2)
        def _():
            sub_tile(tail, MOE_SUB)

        @pl.when((rest > 0) & (rest <= MOE_SUB // 2))
        def _():
            sub_tile(tail, MOE_SUB // 2)


def moe_group_experts(hn, comb, bounds, w_gate, w_up, w_down):
    t, d = hn.shape
    tm = MOE_TILE * MOE_PACK
    ne, _, ff = w_gate.shape
    ex = lambda i, g, j, b: (g * EXPERTS_PER_GROUP + j, 0, 0)
    once = pl.Buffered(1) if MOE_PACK > 1 else None
    return pl.pallas_call(
        _moe_group_kernel,
        out_shape=jax.ShapeDtypeStruct((t, d), F32),
        grid_spec=pltpu.PrefetchScalarGridSpec(
            num_scalar_prefetch=1,
            grid=(t // tm, N_GROUPS, EXPERTS_PER_GROUP),
            in_specs=[pl.BlockSpec((tm, d), lambda i, g, j, b: (i, 0), pipeline_mode=once),
                      pl.BlockSpec((tm, LANE), lambda i, g, j, b: (i, 0)),
                      pl.BlockSpec((None, d, ff), ex), pl.BlockSpec((None, d, ff), ex),
                      pl.BlockSpec((None, ff, d), ex)],
            out_specs=pl.BlockSpec((tm, d), lambda i, g, j, b: (i, 0), pipeline_mode=once)),
        compiler_params=_params("parallel", "arbitrary", "arbitrary"),
        name="moe_group_experts",
    )(bounds, hn, comb, w_gate, w_up, w_down)


def _moe_unsort_kernel(y_ref, pos_ref, x_ref, o_ref):
    tm = y_ref.shape[0]
    perm_t = (lax.broadcasted_iota(jnp.int32, (tm, tm), 1) == pos_ref[...]).astype(BF16)
    y = y_ref[...]
    hi = y.astype(BF16)
    lo = (y - hi.astype(F32)).astype(BF16)
    o_ref[...] = (x_ref[...] + jnp.dot(perm_t, hi, preferred_element_type=F32)
                  + jnp.dot(perm_t, lo, preferred_element_type=F32))


def moe_unsort(y, pos, x):
    t, d = x.shape
    tm, tn = MOE_TILE, d // 2
    blk = pl.BlockSpec((tm, tn), lambda i, j: (i, j))
    return pl.pallas_call(
        _moe_unsort_kernel,
        out_shape=jax.ShapeDtypeStruct((t, d), F32),
        grid=(t // tm, d // tn),
        in_specs=[blk, pl.BlockSpec((tm, 1), lambda i, j: (i, 0)), blk],
        out_specs=blk,
        compiler_params=_params("parallel", "arbitrary"),
        name="moe_unsort",
    )(y, pos, x)


def _moe_unsort_norm_kernel(y_ref, pos_ref, x_ref, g_ref, o_ref):
    tm, ts = pos_ref.shape[0], y_ref.shape[0]
    perm_t = (lax.broadcasted_iota(jnp.int32, (tm, ts), 1) == pos_ref[...]).astype(BF16)
    y = y_ref[...]
    hi = y.astype(BF16)
    lo = (y - hi.astype(F32)).astype(BF16)
    x = (x_ref[...] + jnp.dot(perm_t, hi, preferred_element_type=F32)
         + jnp.dot(perm_t, lo, preferred_element_type=F32))
    o_ref[...] = x * lax.rsqrt(jnp.mean(x * x, axis=-1, keepdims=True) + EPS) * g_ref[...]


def moe_unsort_norm(y, pos, x, g, *, row0, rows, tm=512):
    d = x.shape[1]
    per = MOE_TILE // tm
    r0 = row0 // tm
    return pl.pallas_call(
        _moe_unsort_norm_kernel,
        out_shape=jax.ShapeDtypeStruct((rows, d), F32),
        grid=(rows // tm,),
        in_specs=[pl.BlockSpec((MOE_TILE, d), lambda i: ((i + r0) // per, 0)),
                  pl.BlockSpec((tm, 1), lambda i: (i + r0, 0)),
                  pl.BlockSpec((tm, d), lambda i: (i + r0, 0)),
                  pl.BlockSpec((1, d), lambda i: (0, 0))],
        out_specs=pl.BlockSpec((tm, d), lambda i: (i, 0)),
        compiler_params=_params("parallel"),
        name="moe_unsort_norm",
    )(y, pos, x, g.reshape(1, d))


def hier_moe(x, g, w_grp, b_grp, w_exp, b_exp, w_gate, w_up, w_down, *, final=None):
    hn, comb, pos, bounds = moe_sort(x, g, w_grp, b_grp, w_exp, b_exp)
    y = moe_group_experts(hn, comb, bounds, w_gate.astype(BF16), w_up.astype(BF16), w_down.astype(BF16))
    if final is None:
        return moe_unsort(y, pos, x)
    gain, counts = final
    starts = np.cumsum((0,) + tuple(counts))[:-1]
    return tuple(moe_unsort_norm(y, pos, x, gain, row0=int(r0), rows=int(n)) for r0, n in zip(starts, counts))


def even_layer(x, bsz, s, norm_g, rel_bias, w_in, w_out, w2_f, b_f, w2_b, b_b, onorm):
    n_pad = _round_up(EVEN_IN, LANE)
    w_in_p = jnp.pad(w_in, ((0, 0), (0, n_pad - EVEN_IN))).astype(BF16)
    proj = norm_linear(x, norm_g, w_in_p, tn_target=896).reshape(bsz, s, n_pad)
    tq = ATTN_TQ
    ya = attention(proj, proj, proj, heads=A_HEADS, dq=HEAD_DIM, dv=HEAD_DIM,
                   q_off=0, k_off=A_HEADS, v_off=2 * A_HEADS, scale=HEAD_DIM ** -0.5,
                   bias=dilated_bias_table(rel_bias, s, tq), tq=tq)
    q_col = 3 * A_WIDTH
    yb = gla_mixer(proj, w2_f, w2_b, b_f, b_b, onorm, q_col=q_col, k_col=q_col + B_KEYW,
                   v_col=q_col + 2 * B_KEYW, g_col=q_col + 2 * B_KEYW + B_WIDTH,
                   z_col=q_col + 2 * B_KEYW + 2 * B_WIDTH)
    t = bsz * s
    return out_proj(ya.reshape(t, A_WIDTH), yb.reshape(t, B_WIDTH), w_out.astype(BF16), x)


def _odd_columns(w_in, mu):
    c0 = C_IN
    cut = lambda u, lo, n: u[..., lo:lo + n]
    zpad = lambda u, n: jnp.pad(u, [(0, 0)] * (u.ndim - 1) + [(0, n)])
    off = np.cumsum((0,) + D_SPLITS)
    def rwkv_cols(u):
        parts = [cut(u, off[0], 3 * D_WIDTH), cut(u, off[6], D_G_RANK), cut(u, off[3], 2 * D_W_RANK),
                 cut(u, off[5], D_A_RANK)]
        u = jnp.concatenate(parts, axis=-1)
        return zpad(u, DC_PAD - u.shape[-1])
    w_kr = w_in[:, C_Q_RANK + C_KV_RANK:C_IN]
    w_all = jnp.concatenate([rwkv_cols(w_in[:, c0:]), w_in[:, :C_IN], _rot_half_cols(w_kr)], axis=1)
    n_pad = _round_up(w_all.shape[1], 9 * LANE)
    return zpad(w_all, n_pad - w_all.shape[1]).astype(BF16), rwkv_cols(mu)


def odd_layer(x, bsz, s, norm_g, w_in, w_out, q_norm, w_uq, kv_norm, w_ukv, mu, w0_f, w2_f, w0_b, w2_b,
              a0, a2, g2, k_k, k_a, r_k, ln_g, ln_b):
    t = bsz * s
    w_all, mu_cols = _odd_columns(w_in, mu)
    proj = norm_linear(x, norm_g, w_all, tn_target=1152)
    inv = 1.0 / (ROPE_THETA ** (jnp.arange(0, C_ROPE, 2, dtype=F32) / C_ROPE))
    ang = jnp.arange(s, dtype=F32)[:, None] * inv[None, :]
    cos = jnp.pad(jnp.concatenate([jnp.cos(ang)] * 2, axis=1), ((0, 0), (0, LANE - C_ROPE)), constant_values=1.0)
    sin = jnp.pad(jnp.concatenate([jnp.sin(ang)] * 2, axis=1), ((0, 0), (0, LANE - C_ROPE)))
    q, k, v = mla_up(proj, q_norm, w_uq, kv_norm, w_ukv, jnp.tile(cos, (bsz, 1)), jnp.tile(sin, (bsz, 1)),
                     col0=DC_PAD)
    r3 = lambda u: u.reshape(bsz, s, -1)
    yc = attention(r3(q), r3(k), r3(v), heads=C_HEADS, dq=C_QK, dv=C_V, q_off=0, k_off=0, v_off=0,
                   scale=(C_NOPE + C_ROPE) ** -0.5)
    yd = rwkv7_mixer(proj, bsz, s, mu_cols, w0_f, w2_f, w0_b, w2_b, a0, a2, g2, k_k, k_a, r_k, ln_g, ln_b)
    return out_proj(yc.reshape(t, C_WIDTH), yd, w_out.astype(BF16), x)


def kernel(x_prompt, x_sample, rel_bias, norm_mix, norm_ffn, norm_final, ev_w_in, ev_w_out, ev_gla_w2_f, ev_gla_b_f, ev_gla_w2_b, ev_gla_b_b, ev_gla_onorm, od_w_in, od_w_out, od_q_norm, od_w_uq, od_kv_norm, od_w_ukv, od_mu, od_w0_f, od_w2_f, od_w0_b, od_w2_b, od_a0, od_a2, od_g2, od_k_k, od_k_a, od_r_k, od_ln_g, od_ln_b, moe_w_grp, moe_b_grp, moe_w_exp, moe_b_exp, moe_w_gate, moe_w_up, moe_w_down):
    nb_p = x_prompt.shape[0]
    x = jnp.concatenate([x_prompt, x_sample], axis=0)
    bsz, s, d = x.shape
    x = x.reshape(bsz * s, d)
    for i in range(DEPTH):
        j = i // 2
        if i % 2 == 0:
            x = even_layer(x, bsz, s, norm_mix[i], rel_bias, ev_w_in[j], ev_w_out[j], ev_gla_w2_f[j],
                           ev_gla_b_f[j], ev_gla_w2_b[j], ev_gla_b_b[j], ev_gla_onorm[j])
        else:
            x = odd_layer(x, bsz, s, norm_mix[i], od_w_in[j], od_w_out[j], od_q_norm[j], od_w_uq[j],
                          od_kv_norm[j], od_w_ukv[j], od_mu[j], od_w0_f[j], od_w2_f[j], od_w0_b[j],
                          od_w2_b[j], od_a0[j], od_a2[j], od_g2[j], od_k_k[j], od_k_a[j], od_r_k[j],
                          od_ln_g[j], od_ln_b[j])
        final = (norm_final, (nb_p * s, (bsz - nb_p) * s)) if i == DEPTH - 1 else None
        x = hier_moe(x, norm_ffn[i], moe_w_grp[i], moe_b_grp[i], moe_w_exp[i], moe_b_exp[i],
                     moe_w_gate[i], moe_w_up[i], moe_w_down[i], final=final)
    y_p, y_s = x
    return (y_p.reshape(nb_p, s, d), y_s.reshape(bsz - nb_p, s, d))
```

```python
import functools

import jax, jax.numpy as jnp
from jax import lax
import numpy as np
from jax.experimental import pallas as pl
from jax.experimental.pallas import tpu as pltpu

F32, BF16 = jnp.float32, jnp.bfloat16

D_MODEL = 2048
DEPTH = 2
MIX_HALF = D_MODEL // 2
HEAD_DIM = 128
EPS = 1e-6
NEG = -1e30

A_HEADS = MIX_HALF // HEAD_DIM
A_WIDTH = A_HEADS * HEAD_DIM
A_PATTERNS = ((128, 1), (512, 4), (2048, 16))
N_BUCKETS = 32
MAX_DISTANCE = 1024

B_HEADS = 4
B_DV = MIX_HALF // B_HEADS
B_DK = B_DV // 2
B_WIDTH = B_HEADS * B_DV
B_KEYW = B_HEADS * B_DK
B_GATE_RANK = 16
B_GATE_TAU = 16.0
B_CHUNK = 64

C_HEADS = MIX_HALF // 128
C_Q_RANK = 512
C_KV_RANK = 256
C_NOPE = 128
C_ROPE = 64
C_V = 128
C_WIDTH = C_HEADS * C_V
C_QK = 256
ROPE_THETA = 10000.0

D_HEAD = 64
D_HEADS = MIX_HALF // D_HEAD
D_WIDTH = D_HEADS * D_HEAD
D_W_RANK = 64
D_A_RANK = 64
D_G_RANK = 128
D_LN_EPS = 64e-5
D_SPLITS = (D_WIDTH, D_WIDTH, D_WIDTH, D_W_RANK, D_W_RANK, D_A_RANK, D_G_RANK)
D_SHIFT = 3 * D_WIDTH + 2 * D_W_RANK + D_A_RANK + D_G_RANK

N_GROUPS = 4
EXPERTS_PER_GROUP = 4
N_EXPERTS = N_GROUPS * EXPERTS_PER_GROUP

EVEN_IN = 3 * A_WIDTH + 2 * B_KEYW + 2 * B_WIDTH + 2 * B_GATE_RANK
C_IN = C_Q_RANK + C_KV_RANK + C_ROPE

LANE = 128
VMEM_LIMIT = 52 * 1024 * 1024


def _params(*sem):
    return pltpu.CompilerParams(dimension_semantics=sem, vmem_limit_bytes=VMEM_LIMIT)


def _round_up(n, m):
    return -(-n // m) * m


NT_DIMS = (((1,), (1,)), ((), ()))
TN_DIMS = (((0,), (0,)), ((), ()))


def _split3(x):
    hi = x.astype(BF16)
    r1 = x - hi.astype(F32)
    mid = r1.astype(BF16)
    lo = (r1 - mid.astype(F32)).astype(BF16)
    return hi, mid, lo


def _pick_tile(n, target):
    best = LANE
    for t in range(LANE, target + 1, LANE):
        if n % t == 0:
            best = t
    return best


def _norm_linear_kernel(x_ref, g_ref, w_ref, o_ref, xn_ref):
    @pl.when(pl.program_id(1) == 0)
    def _():
        x = x_ref[...]
        y = x * lax.rsqrt(jnp.mean(x * x, axis=-1, keepdims=True) + EPS) * g_ref[...]
        xn_ref[...] = y.astype(BF16)

    o_ref[...] = jnp.dot(xn_ref[...], w_ref[...], preferred_element_type=F32)


def norm_linear(x, g, w, *, tm=1024, tn_target=1024):
    t, k = x.shape
    n = w.shape[1]
    tn = _pick_tile(n, tn_target)
    return pl.pallas_call(
        _norm_linear_kernel,
        out_shape=jax.ShapeDtypeStruct((t, n), F32),
        grid=(t // tm, n // tn),
        in_specs=[pl.BlockSpec((tm, k), lambda i, j: (i, 0)),
                  pl.BlockSpec((1, k), lambda i, j: (0, 0)),
                  pl.BlockSpec((k, tn), lambda i, j: (0, j))],
        out_specs=pl.BlockSpec((tm, tn), lambda i, j: (i, j)),
        scratch_shapes=[pltpu.VMEM((tm, k), BF16)],
        compiler_params=_params("parallel", "arbitrary"),
        name="norm_linear",
    )(x, g.reshape(1, k), w)


def _out_proj_kernel(a_ref, b_ref, wa_ref, wb_ref, x_ref, o_ref):
    acc = jnp.dot(a_ref[...], wa_ref[...], preferred_element_type=F32)
    acc += jnp.dot(b_ref[...], wb_ref[...], preferred_element_type=F32)
    o_ref[...] = x_ref[...] + acc


def out_proj(a, b, w, x, *, tm=1024, tn=512):
    t, ka = a.shape
    kb = b.shape[1]
    n = w.shape[1]
    return pl.pallas_call(
        _out_proj_kernel,
        out_shape=jax.ShapeDtypeStruct((t, n), F32),
        grid=(t // tm, n // tn),
        in_specs=[pl.BlockSpec((tm, ka), lambda i, j: (i, 0)),
                  pl.BlockSpec((tm, kb), lambda i, j: (i, 0)),
                  pl.BlockSpec((ka, tn), lambda i, j: (0, j)),
                  pl.BlockSpec((kb, tn), lambda i, j: (0, j)),
                  pl.BlockSpec((tm, tn), lambda i, j: (i, j))],
        out_specs=pl.BlockSpec((tm, tn), lambda i, j: (i, j)),
        compiler_params=_params("parallel", "arbitrary"),
        name="out_proj",
    )(a, b, w[:ka], w[ka:], x)


ATTN_KB = 256
ATTN_TQ = 512


def _attn_kernel(*refs, scale, has_bias):
    if has_bias:
        q_ref, k_ref, v_ref, bias_ref, o_ref, kb_ref, vb_ref = refs
    else:
        q_ref, k_ref, v_ref, o_ref, kb_ref, vb_ref = refs
    s_len = k_ref.shape[0]

    @pl.when(pl.program_id(2) == 0)
    def _():
        kb_ref[...] = k_ref[...].astype(BF16)
        vb_ref[...] = v_ref[...].astype(BF16)

    q = (q_ref[...] * scale).astype(BF16)
    blocks = [slice(j * ATTN_KB, (j + 1) * ATTN_KB) for j in range(s_len // ATTN_KB)]
    scores = []
    m = None
    for blk in blocks:
        sj = lax.dot_general(q, kb_ref[blk, :], NT_DIMS, preferred_element_type=F32)
        if has_bias:
            sj = sj + bias_ref[:, blk]
        mj = jnp.max(sj, axis=-1, keepdims=True)
        m = mj if m is None else jnp.maximum(m, mj)
        scores.append(sj)
    o = den = None
    for blk, sj in zip(blocks, scores):
        p = jnp.exp(sj - m)
        dj = jnp.sum(p, axis=-1, keepdims=True)
        oj = jnp.dot(p.astype(BF16), vb_ref[blk, :], preferred_element_type=F32)
        o, den = (oj, dj) if o is None else (o + oj, den + dj)
    o_ref[...] = (o / den).astype(o_ref.dtype)


def attention(q, k, v, *, heads, dq, dv, q_off, k_off, v_off, scale, bias=None, tq=ATTN_TQ):
    b, s, _ = q.shape
    nq = s // tq
    in_specs = [pl.BlockSpec((None, tq, dq), lambda bi, h, qi: (bi, qi, q_off + h)),
                pl.BlockSpec((None, s, dq), lambda bi, h, qi: (bi, 0, k_off + h)),
                pl.BlockSpec((None, s, dv), lambda bi, h, qi: (bi, 0, v_off + h))]
    args = [q, k, v]
    if bias is not None:
        in_specs.append(pl.BlockSpec((None, None, tq, s), lambda bi, h, qi: (h, qi, 0, 0)))
        args.append(bias)
    return pl.pallas_call(
        functools.partial(_attn_kernel, scale=scale, has_bias=bias is not None),
        out_shape=jax.ShapeDtypeStruct((b, s, heads * dv), BF16),
        grid=(b, heads, nq),
        in_specs=in_specs,
        out_specs=pl.BlockSpec((None, tq, dv), lambda bi, h, qi: (bi, qi, h)),
        scratch_shapes=[pltpu.VMEM((s, dq), BF16), pltpu.VMEM((s, dv), BF16)],
        compiler_params=_params("parallel", "parallel", "arbitrary"),
        name="attention_bias" if bias is not None else "attention",
    )(*args)


def _t5_bucket(rel):
    half = N_BUCKETS // 2
    exact = half // 2
    n = np.abs(rel)
    large = exact + (np.log(np.maximum(n, 1) / exact) / np.log(MAX_DISTANCE / exact) * (half - exact)).astype(np.int64)
    large = np.minimum(large, half - 1)
    return ((rel > 0) * half + np.where(n < exact, n, large)).astype(np.int32)


def dilated_bias_table(rel_bias, s, tq):
    heads = rel_bias.shape[1]
    d = np.arange(-(s - 1), s)
    count = np.zeros(d.shape, np.float32)
    for window, dil in A_PATTERNS:
        count += ((d % dil == 0) & (np.abs(d) <= (window // (2 * dil)) * dil)).astype(np.float32)
    logc = np.where(count > 0, np.log(np.maximum(count, 1.0)), NEG).astype(np.float32)
    onehot = (_t5_bucket(d)[:, None] == np.arange(N_BUCKETS)[None, :]).astype(np.float32)
    line = jnp.transpose(jnp.dot(onehot, rel_bias.astype(F32), precision=lax.Precision.HIGHEST)) + logc[None]
    width = 2 * s
    line = jnp.pad(line, ((0, 0), (0, width - line.shape[1])))[:, None, :]
    nq = s // tq
    return pl.pallas_call(
        functools.partial(_skew_kernel, tq=tq, nq=nq),
        out_shape=jax.ShapeDtypeStruct((heads, nq, tq, s), F32),
        grid=(heads, nq),
        in_specs=[pl.BlockSpec((None, 1, width), lambda h, qi: (h, 0, 0))],
        out_specs=pl.BlockSpec((None, None, tq, s), lambda h, qi: (h, qi, 0, 0)),
        compiler_params=_params("parallel", "arbitrary"),
        name="bias_skew",
    )(line)


def _skew_kernel(line_ref, o_ref, *, tq, nq):
    width = line_ref.shape[1]
    first = (nq - 1 - pl.program_id(1)) * tq
    x = jnp.broadcast_to(line_ref[...], (tq, width))
    x = pltpu.roll(x, width - (tq - 1) - first, 1, stride=1, stride_axis=0)
    o_ref[...] = x[:, :o_ref.shape[1]]


GLA_UNROLL = 8


def _gla_kernel(q_ref, k_ref, v_ref, g_ref, z_ref, w2f_ref, w2b_ref, bf_ref, bb_ref, on_ref, o_ref,
                la_ref, acc_ref, qcat_ref, upd_ref, dec_ref, scat_ref, st_ref):
    s_len = q_ref.shape[0]
    c = B_CHUNK
    nchunk = s_len // c
    z = z_ref[...].astype(BF16)
    gate = lambda w2_ref, b_ref: jax.nn.log_sigmoid(
        jnp.dot(z, w2_ref[...], preferred_element_type=F32) + b_ref[...]) * (1.0 / B_GATE_TAU)
    la_ref[0] = gate(w2f_ref, bf_ref)
    la_ref[1] = gate(w2b_ref, bb_ref)

    ri = lax.broadcasted_iota(jnp.int32, (c, c), 0)
    ci = lax.broadcasted_iota(jnp.int32, (c, c), 1)
    keep = (ri >= ci, ri <= ci)
    tri3 = tuple(jnp.concatenate([kp.astype(BF16)] * 3, axis=1) for kp in keep)

    def chunk_rows(n):
        return pl.ds(pl.multiple_of(n * c, c), c)

    def pass1(i, carry):
        units = [(i * GLA_UNROLL + u, d) for u in range(GLA_UNROLL) for d in range(2)]
        st = []
        for n, d in units:
            rows = chunk_rows(n)
            gcum = jnp.dot(tri3[d], jnp.concatenate(_split3(la_ref[d, rows, :]), axis=0),
                           preferred_element_type=F32)
            st.append(dict(rows=rows, gcum=gcum))
        for (n, d), c in zip(units, st):
            gcum = c["gcum"]
            gend = gcum[0:1] if d == 1 else gcum[B_CHUNK - 1:B_CHUNK]
            kc = k_ref[c["rows"], :]
            c["q_in"] = (q_ref[c["rows"], :] * (B_DK ** -0.5) * jnp.exp(gcum)).astype(BF16)
            c["k_out"] = (kc * jnp.exp(gend - gcum)).astype(BF16)
            c["att"] = lax.dot_general(c["q_in"], (kc * jnp.exp(-gcum)).astype(BF16), NT_DIMS,
                                       preferred_element_type=F32)
            dec_ref[d, n] = jnp.broadcast_to(jnp.exp(gend), (8, B_DK))
        for (n, d), c in zip(units, st):
            vc = v_ref[c["rows"], :].astype(BF16)
            c["o"] = jnp.dot(jnp.where(keep[d], c["att"], 0.0).astype(BF16), vc, preferred_element_type=F32)
            upd_ref[d, n] = lax.dot_general(vc, c["k_out"], TN_DIMS, preferred_element_type=F32)
            qcat_ref[c["rows"], d * B_DK:(d + 1) * B_DK] = c["q_in"]
        for u in range(GLA_UNROLL):
            acc_ref[st[2 * u]["rows"], :] = st[2 * u]["o"] + st[2 * u + 1]["o"]
        return carry

    lax.fori_loop(0, nchunk // GLA_UNROLL, pass1, 0)

    st_ref[...] = jnp.zeros_like(st_ref)

    def pass2(n, carry):
        for d, m in ((0, n), (1, nchunk - 1 - n)):
            state = st_ref[d]
            scat_ref[m, :, d * B_DK:(d + 1) * B_DK] = state.astype(BF16)
            st_ref[d] = state * dec_ref[d, m][0:1] + upd_ref[d, m]
        return carry

    lax.fori_loop(0, nchunk, pass2, 0)

    def pass3(i, carry):
        rows = [chunk_rows(i * GLA_UNROLL + u) for u in range(GLA_UNROLL)]
        outs = [acc_ref[r, :] + lax.dot_general(qcat_ref[r, :], scat_ref[i * GLA_UNROLL + u], NT_DIMS,
                                                preferred_element_type=F32) for u, r in enumerate(rows)]
        for r, o in zip(rows, outs):
            o = o * lax.rsqrt(jnp.mean(o * o, axis=-1, keepdims=True) + EPS) * on_ref[...]
            g = g_ref[r, :]
            o_ref[r, :] = (o * (g * jax.nn.sigmoid(g))).astype(o_ref.dtype)
        return carry

    lax.fori_loop(0, nchunk // GLA_UNROLL, pass3, 0)


def gla_mixer(proj, w2f, w2b, b_f, b_b, onorm, *, q_col, k_col, v_col, g_col, z_col):
    b, s, _ = proj.shape
    hm = lambda blk: (lambda bi, h: (bi, 0, blk + h))
    w2f_p = jnp.zeros((LANE, B_KEYW), F32).at[:B_GATE_RANK].set(w2f).astype(BF16)
    w2b_p = jnp.zeros((LANE, B_KEYW), F32).at[B_GATE_RANK:2 * B_GATE_RANK].set(w2b).astype(BF16)
    return pl.pallas_call(
        _gla_kernel,
        out_shape=jax.ShapeDtypeStruct((b, s, B_WIDTH), BF16),
        grid=(b, B_HEADS),
        in_specs=[pl.BlockSpec((None, s, B_DK), hm(q_col // B_DK)),
                  pl.BlockSpec((None, s, B_DK), hm(k_col // B_DK)),
                  pl.BlockSpec((None, s, B_DV), hm(v_col // B_DV)),
                  pl.BlockSpec((None, s, B_DV), hm(g_col // B_DV)),
                  pl.BlockSpec((None, s, LANE), lambda bi, h: (bi, 0, z_col // LANE)),
                  pl.BlockSpec((LANE, B_DK), lambda bi, h: (0, h)),
                  pl.BlockSpec((LANE, B_DK), lambda bi, h: (0, h)),
                  pl.BlockSpec((1, B_DK), lambda bi, h: (0, h)),
                  pl.BlockSpec((1, B_DK), lambda bi, h: (0, h)),
                  pl.BlockSpec((1, B_DV), lambda bi, h: (0, 0))],
        out_specs=pl.BlockSpec((None, s, B_DV), lambda bi, h: (bi, 0, h)),
        scratch_shapes=[pltpu.VMEM((2, s, B_DK), F32),
                        pltpu.VMEM((s, B_DV), F32),
                        pltpu.VMEM((s, 2 * B_DK), BF16),
                        pltpu.VMEM((2, s // B_CHUNK, B_DV, B_DK), F32),
                        pltpu.VMEM((2, s // B_CHUNK, 8, B_DK), F32),
                        pltpu.VMEM((s // B_CHUNK, B_DV, 2 * B_DK), BF16),
                        pltpu.VMEM((2, B_DV, B_DK), F32)],
        compiler_params=_params("parallel", "arbitrary"),
        name="gla_mixer",
    )(proj, proj, proj, proj, proj, w2f_p, w2b_p, b_f.reshape(1, -1), b_b.reshape(1, -1), onorm.reshape(1, -1))


def _mla_up_kernel(cq_ref, ckv_ref, kr_ref, qn_ref, kvn_ref, wq_ref, wqr_ref, wkv_ref, cos_ref, sin_ref,
                   q_ref, k_ref, v_ref):
    def rms(x, g):
        return (x * lax.rsqrt(jnp.mean(x * x, axis=-1, keepdims=True) + EPS) * g).astype(BF16)

    cq = rms(cq_ref[...], qn_ref[...])
    ckv = rms(ckv_ref[...], kvn_ref[...])
    cos, sin = cos_ref[...], sin_ref[...]
    kr = kr_ref[...]
    k_rope = kr * cos + pltpu.roll(kr, LANE - C_ROPE, 1) * sin
    lane = lax.broadcasted_iota(jnp.int32, k_rope.shape, 1)
    k_rope = jnp.where(lane < C_ROPE, k_rope, 0.0)
    for h in range(C_HEADS):
        q = jnp.dot(cq, wq_ref[:, h * C_QK:(h + 1) * C_QK], preferred_element_type=F32)
        qp = jnp.dot(cq, wqr_ref[:, h * LANE:(h + 1) * LANE], preferred_element_type=F32)
        q_ref[:, h * C_QK:h * C_QK + C_NOPE] = q[:, :C_NOPE]
        q_ref[:, h * C_QK + C_NOPE:(h + 1) * C_QK] = q[:, C_NOPE:] * cos + qp * sin
        kv = jnp.dot(ckv, wkv_ref[:, h * 2 * LANE:(h + 1) * 2 * LANE], preferred_element_type=F32)
        k_ref[:, h * C_QK:h * C_QK + C_NOPE] = kv[:, :C_NOPE]
        k_ref[:, h * C_QK + C_NOPE:(h + 1) * C_QK] = k_rope
        v_ref[:, h * C_V:(h + 1) * C_V] = kv[:, C_NOPE:]


def _rot_half_cols(w):
    half = w.shape[-1] // 2
    return jnp.concatenate([-w[..., half:], w[..., :half]], axis=-1)


def mla_up(proj, q_norm, w_uq, kv_norm, w_ukv, cos, sin, *, col0, tm=512):
    t = proj.shape[0]
    wq = w_uq.reshape(C_Q_RANK, C_HEADS, C_NOPE + C_ROPE)
    wq_main = jnp.pad(wq, ((0, 0), (0, 0), (0, C_QK - C_NOPE - C_ROPE))).reshape(C_Q_RANK, C_HEADS * C_QK)
    wq_rot = jnp.pad(_rot_half_cols(wq[..., C_NOPE:]), ((0, 0), (0, 0), (0, LANE - C_ROPE)))
    wq_rot = wq_rot.reshape(C_Q_RANK, C_HEADS * LANE)
    row = lambda i: (i, 0)
    full = lambda arr: pl.BlockSpec(arr.shape, lambda i: (0, 0))
    g_q, g_kv = q_norm.reshape(1, -1), kv_norm.reshape(1, -1)
    wq_main, wq_rot, wkv = wq_main.astype(BF16), wq_rot.astype(BF16), w_ukv.astype(BF16)
    return pl.pallas_call(
        _mla_up_kernel,
        out_shape=(jax.ShapeDtypeStruct((t, C_HEADS * C_QK), F32),
                   jax.ShapeDtypeStruct((t, C_HEADS * C_QK), F32),
                   jax.ShapeDtypeStruct((t, C_WIDTH), F32)),
        grid=(t // tm,),
        in_specs=[pl.BlockSpec((tm, C_Q_RANK), lambda i: (i, col0 // C_Q_RANK)),
                  pl.BlockSpec((tm, C_KV_RANK), lambda i: (i, (col0 + C_Q_RANK) // C_KV_RANK)),
                  pl.BlockSpec((tm, LANE), lambda i: (i, (col0 + C_Q_RANK + C_KV_RANK) // LANE)),
                  full(g_q), full(g_kv), full(wq_main), full(wq_rot), full(wkv),
                  pl.BlockSpec((tm, LANE), row), pl.BlockSpec((tm, LANE), row)],
        out_specs=(pl.BlockSpec((tm, C_HEADS * C_QK), row),
                   pl.BlockSpec((tm, C_HEADS * C_QK), row),
                   pl.BlockSpec((tm, C_WIDTH), row)),
        compiler_params=_params("parallel"),
        name="mla_up",
    )(proj, proj, proj, g_q, g_kv, wq_main, wq_rot, wkv, cos, sin)


RG = 4
RGW = RG * D_HEAD
RCH = 64
DC_R, DC_K, DC_V = 0, D_WIDTH, 2 * D_WIDTH
DC_ZG = 3 * D_WIDTH
DC_ZW = DC_ZG + D_G_RANK
DC_ZA = DC_ZW + 2 * D_W_RANK
DC_PAD = 7 * 512


def _head_sums(x, bo):
    return jnp.concatenate(
        [jnp.dot(x[:, RGW * g:RGW * (g + 1)], bo, preferred_element_type=F32, precision=lax.Precision.HIGHEST)
         for g in range(x.shape[1] // RGW)], axis=1)


def _block_ones():
    i = np.arange(RGW)
    return jnp.asarray((i[:, None] // D_HEAD) == (i[None, :] // D_HEAD), F32)


def _rwkv_prep_kernel(x_ref, xp_ref, xn_ref, mu_ref, w2f_ref, w2b_ref, a2_ref, g2_ref, w0f_ref, w0b_ref,
                      a0_ref, kk_ref, ka_ref, rk_ref, bo_ref,
                      r_ref, k_ref, v_ref, a_ref, b_ref, lwf_ref, lwb_ref, g_ref, bonus_ref, *, tiles_per_seq):
    i = pl.program_id(0) % tiles_per_seq
    x = x_ref[...]
    tm = x.shape[0]
    row = lax.broadcasted_iota(jnp.int32, x.shape, 0)
    prev_row = jnp.where(i == 0, 0.0, xp_ref[7:8, :])
    next_row = jnp.where(i == tiles_per_seq - 1, 0.0, xn_ref[0:1, :])
    prev = jnp.where(row == 0, prev_row, pltpu.roll(x, 1, 0))
    nxt = jnp.where(row == tm - 1, next_row, pltpu.roll(x, tm - 1, 0))
    x = x + mu_ref[...] * (0.5 * (prev + nxt) - x)
    r, k, v = x[:, DC_R:DC_R + D_WIDTH], x[:, DC_K:DC_K + D_WIDTH], x[:, DC_V:DC_V + D_WIDTH]
    zg = x[:, DC_ZG:DC_ZG + LANE]
    zw = x[:, DC_ZW:DC_ZW + LANE]
    za = x[:, DC_ZA:DC_ZA + LANE]
    tz = jnp.tanh(zw).astype(BF16)
    log_decay = lambda w0_ref, w2_ref: -np.exp(-0.5).astype(np.float32) * jax.nn.sigmoid(
        w0_ref[...] + jnp.dot(tz, w2_ref[...], preferred_element_type=F32))
    lwf_ref[...] = log_decay(w0f_ref, w2f_ref)
    lwb_ref[...] = log_decay(w0b_ref, w2b_ref)
    ag = jax.nn.sigmoid(a0_ref[...] + jnp.dot(za.astype(BF16), a2_ref[...], preferred_element_type=F32))
    g_ref[...] = jnp.dot(jax.nn.sigmoid(zg).astype(BF16), g2_ref[...], preferred_element_type=F32)
    bo = bo_ref[...]
    kk = k * kk_ref[...]
    kk = kk / jnp.maximum(jnp.sqrt(_head_sums(kk * kk, bo)), 1e-12)
    k = k * (1.0 + (ag - 1.0) * ka_ref[...])
    r_ref[...] = r
    k_ref[...] = k
    v_ref[...] = v
    a_ref[...] = -kk
    b_ref[...] = kk * ag
    bonus_ref[...] = _head_sums(r * k * rk_ref[...], bo) * v


def rwkv_prep(proj, mu, w0_f, w2_f, w0_b, w2_b, a0, a2, g2, k_k, k_a, r_k, *, seq, tm=256):
    t = proj.shape[0]
    tiles_per_seq = seq // tm
    hb = tm // 8
    nblk8 = t // 8
    pad_rows = lambda w, lo: jnp.zeros((LANE, D_WIDTH), F32).at[lo:lo + w.shape[0]].set(w).astype(BF16)
    vec = lambda u: u.reshape(1, -1)
    consts = [vec(mu), pad_rows(w2_f, 0), pad_rows(w2_b, D_W_RANK), pad_rows(a2, 0), g2.astype(BF16),
              vec(w0_f), vec(w0_b), vec(a0), vec(k_k), vec(k_a), vec(r_k), _block_ones()]
    full = lambda arr: pl.BlockSpec(arr.shape, lambda i: (0, 0))
    out_spec = pl.BlockSpec((tm, D_WIDTH), lambda i: (i, 0))
    return pl.pallas_call(
        functools.partial(_rwkv_prep_kernel, tiles_per_seq=tiles_per_seq),
        out_shape=tuple(jax.ShapeDtypeStruct((t, D_WIDTH), F32) for _ in range(9)),
        grid=(t // tm,),
        in_specs=[pl.BlockSpec((tm, DC_PAD), lambda i: (i, 0)),
                  pl.BlockSpec((8, DC_PAD), lambda i: (jnp.maximum(i * hb - 1, 0), 0)),
                  pl.BlockSpec((8, DC_PAD), lambda i: (jnp.minimum((i + 1) * hb, nblk8 - 1), 0))]
                 + [full(c) for c in consts],
        out_specs=tuple(out_spec for _ in range(9)),
        compiler_params=_params("parallel"),
        name="rwkv_prep",
    )(proj, proj, proj, *consts)


def _rwkv_chunk_kernel(*refs, ngroups):
    ins, (yf_ref, yb_ref, mt_ref) = refs[:12], refs[12:]

    @pl.when(pl.program_id(1) == 0)
    def _():
        mt_ref[...] = jnp.zeros_like(mt_ref)

    row = lax.broadcasted_iota(jnp.int32, (RCH, RGW), 0)
    col = lax.broadcasted_iota(jnp.int32, (RCH, RGW), 1) & (RCH - 1)
    bdmask = (lax.broadcasted_iota(jnp.int32, (RGW, RGW), 0) // D_HEAD
              == lax.broadcasted_iota(jnp.int32, (RGW, RGW), 1) // D_HEAD)
    tr = lax.broadcasted_iota(jnp.int32, (RCH, RCH), 0)
    tc = lax.broadcasted_iota(jnp.int32, (RCH, RCH), 1)
    zero = jnp.zeros((), F32)

    def bd(z):
        zb = z.astype(BF16)
        return jnp.where(bdmask, jnp.concatenate([zb] * RG, axis=0), jnp.zeros((), BF16))

    def mm(x, y, dims=None):
        x = x.astype(BF16)
        if dims is None:
            return jnp.dot(x, y, preferred_element_type=F32)
        return lax.dot_general(x, y, dims, preferred_element_type=F32)

    chains = [(d, g) for d in range(2) for g in range(ngroups)]
    st = []
    for d, g in chains:
        backward = d == 1
        r_ref, k_ref, v_ref, a_ref, b_ref, lw_ref = ins[6 * d:6 * d + 6]
        tri = ((tc >= tr) if backward else (tc <= tr)).astype(BF16)
        sl = slice(RGW * g, RGW * (g + 1))
        r, k, v, a, b, lw = (ref[:, sl] for ref in (r_ref, k_ref, v_ref, a_ref, b_ref, lw_ref))
        lam = jnp.dot(jnp.concatenate([tri] * 3, axis=1), jnp.concatenate(_split3(lw), axis=0),
                      preferred_element_type=F32)
        lamc = lam[0:1] if backward else lam[RCH - 1:RCH]
        e_inv = jnp.exp(-lam)
        e_out = jnp.exp(lamc - lam)
        ar = jnp.concatenate([a * jnp.exp(lam - lw), r * jnp.exp(lam)], axis=0).astype(BF16)
        bk = jnp.concatenate([b * e_out, k * e_out], axis=0).astype(BF16)
        st.append(dict(ar=ar, bk=bk, v=v, lamc=lamc, sl=sl,
                       gb=mm(ar, bd(b * e_inv), NT_DIMS), gk=mm(ar, bd(k * e_inv), NT_DIMS)))
    for (d, g), c in zip(chains, st):
        strict = (col > row) if d == 1 else (col < row)
        incl = (col >= row) if d == 1 else (col <= row)
        c["lp"] = jnp.where(strict, c["gb"][:RCH], zero)
        lak = jnp.where(strict, c["gk"][:RCH], zero)
        c["grb"] = jnp.where(incl, c["gb"][RCH:], zero).astype(BF16)
        c["grk"] = jnp.where(incl, c["gk"][RCH:], zero).astype(BF16)
        c["mt"] = mt_ref[d, g]
        amrm = mm(c["ar"], c["mt"].astype(BF16), NT_DIMS)
        c["bdv"] = bd(c["v"])
        c["u"] = amrm[:RCH] + mm(lak, c["bdv"])
        c["rm"] = amrm[RCH:]
    for rnd in range(6):
        for c in st:
            lpb = c["lp"].astype(BF16)
            c["u"] = c["u"] + mm(lpb, bd(c["u"]))
            if rnd < 5:
                c["lp"] = mm(lpb, bd(c["lp"]))
    for (d, g), c in zip(chains, st):
        y_ref = yb_ref if d == 1 else yf_ref
        y_ref[:, c["sl"]] = c["rm"] + mm(c["grb"], bd(c["u"])) + mm(c["grk"], c["bdv"])
        uv = jnp.concatenate([c["u"], c["v"]], axis=0).astype(BF16)
        upd = lax.dot_general(uv, c["bk"], TN_DIMS, preferred_element_type=F32)
        mt_ref[d, g] = c["mt"] * jnp.exp(c["lamc"]) + jnp.where(bdmask, upd, zero)


def rwkv_chunked(r, k, v, a, b, lwf, lwb):
    bsz, s, wd = r.shape
    nc = s // RCH
    fspec = pl.BlockSpec((None, RCH, wd), lambda bi, n: (bi, n, 0))
    bspec = pl.BlockSpec((None, RCH, wd), lambda bi, n: (bi, nc - 1 - n, 0))
    return pl.pallas_call(
        functools.partial(_rwkv_chunk_kernel, ngroups=wd // RGW),
        out_shape=(jax.ShapeDtypeStruct((bsz, s, wd), F32), jax.ShapeDtypeStruct((bsz, s, wd), F32)),
        grid=(bsz, nc),
        in_specs=[fspec] * 6 + [bspec] * 6,
        out_specs=(fspec, bspec),
        scratch_shapes=[pltpu.VMEM((2, wd // RGW, RGW, RGW), F32)],
        compiler_params=_params("parallel", "arbitrary"),
        name="rwkv_chunked",
    )(r, k, v, a, b, lwf, r, k, v, a, b, lwb)


def _rwkv_post_kernel(yf_ref, yb_ref, bonus_ref, g_ref, lng_ref, lnb_ref, bo_ref, o_ref):
    bo = bo_ref[...]
    y = yf_ref[...] + yb_ref[...]
    yc = y - _head_sums(y, bo) * (1.0 / D_HEAD)
    var = _head_sums(yc * yc, bo) * (1.0 / D_HEAD)
    y = yc * lax.rsqrt(var + D_LN_EPS) * lng_ref[...] + lnb_ref[...]
    o_ref[...] = ((y + bonus_ref[...]) * g_ref[...]).astype(o_ref.dtype)


def rwkv_post(yf, yb, bonus, g, ln_g, ln_b, *, tm=512):
    t, wd = yf.shape
    row = pl.BlockSpec((tm, wd), lambda i: (i, 0))
    vec = pl.BlockSpec((1, wd), lambda i: (0, 0))
    bo = _block_ones()
    return pl.pallas_call(
        _rwkv_post_kernel,
        out_shape=jax.ShapeDtypeStruct((t, wd), BF16),
        grid=(t // tm,),
        in_specs=[row, row, row, row, vec, vec, pl.BlockSpec(bo.shape, lambda i: (0, 0))],
        out_specs=row,
        compiler_params=_params("parallel"),
        name="rwkv_post",
    )(yf, yb, bonus, g, ln_g.reshape(1, wd), ln_b.reshape(1, wd), bo)


def rwkv7_mixer(proj, bsz, s, mu, w0_f, w2_f, w0_b, w2_b, a0, a2, g2, k_k, k_a, r_k, ln_g, ln_b):
    r, k, v, a, b, lwf, lwb, g, bonus = rwkv_prep(proj, mu, w0_f, w2_f, w0_b, w2_b, a0, a2, g2, k_k, k_a,
                                                  r_k.reshape(-1), seq=s)
    r3 = lambda u: u.reshape(bsz, s, D_WIDTH)
    yf, yb = rwkv_chunked(r3(r), r3(k), r3(v), r3(a), r3(b), r3(lwf), r3(lwb))
    return rwkv_post(yf.reshape(-1, D_WIDTH), yb.reshape(-1, D_WIDTH), bonus, g, ln_g, ln_b)


MOE_TILE = 1024
MOE_SUB = 288
ROW_ALIGN = 16
MOE_PACK = 1


def _route(logit):
    lane = lax.broadcasted_iota(jnp.int32, logit.shape, 1)
    first_at = lambda mask: jnp.min(jnp.where(mask, lane, jnp.int32(LANE)), axis=-1, keepdims=True)
    is_grp = lane < N_GROUPS
    gl = jnp.where(is_grp, logit, NEG)
    gmax = jnp.max(gl, axis=-1, keepdims=True)
    p_grp = 1.0 / jnp.sum(jnp.where(is_grp, jnp.exp(gl - gmax), 0.0), axis=-1, keepdims=True)
    i_grp = first_at(is_grp & (gl == gmax))
    lo = N_GROUPS + i_grp * EXPERTS_PER_GROUP
    in_grp = (lane >= lo) & (lane < lo + EXPERTS_PER_GROUP)
    el = jnp.where(in_grp, logit, NEG)
    l1 = jnp.max(el, axis=-1, keepdims=True)
    i1 = first_at(in_grp & (el == l1))
    rest = in_grp & (lane != i1)
    el2 = jnp.where(rest, logit, NEG)
    l2 = jnp.max(el2, axis=-1, keepdims=True)
    i2 = first_at(rest & (el2 == l2))
    e2 = jnp.exp(l2 - l1)
    w1 = p_grp / (1.0 + e2)
    w2 = p_grp * e2 / (1.0 + e2)
    return i_grp, jnp.where(lane == i1, w1, jnp.where(lane == i2, w2, 0.0))


def _moe_sort_kernel(x_ref, g_ref, wr_ref, br_ref, hn_ref, comb_ref, pos_ref, off_ref):
    x = x_ref[...]
    tm = x.shape[0]
    hn = (x * lax.rsqrt(jnp.mean(x * x, axis=-1, keepdims=True) + EPS) * g_ref[...]).astype(BF16)
    wr, br = wr_ref[...], br_ref[...]
    i_grp, _ = _route(jnp.dot(hn, wr, preferred_element_type=F32) + br)
    lane = lax.broadcasted_iota(jnp.int32, (tm, LANE), 1)
    onehot = (lane == i_grp).astype(F32)
    ri = lax.broadcasted_iota(jnp.int32, (tm, tm), 0)
    ci = lax.broadcasted_iota(jnp.int32, (tm, tm), 1)
    earlier = jnp.dot((ci < ri).astype(BF16), onehot.astype(BF16), preferred_element_type=F32)
    cnt = jnp.broadcast_to(jnp.sum(onehot, axis=0, keepdims=True), (8, LANE))
    lane8 = lax.broadcasted_iota(jnp.int32, (8, LANE), 1)
    start = jnp.zeros((8, LANE), F32)
    for sh in range(1, N_GROUPS):
        start += jnp.where(lane8 >= sh, pltpu.roll(cnt, sh, 1), 0.0)
    off_ref[...] = start[0:1].astype(jnp.int32)
    pos = jnp.sum(onehot * (start[0:1] + earlier), axis=-1, keepdims=True).astype(jnp.int32)
    pos_ref[...] = pos
    perm_t = (ci == pos).astype(BF16)
    hn_s = lax.dot_general(perm_t, hn, TN_DIMS, preferred_element_type=F32).astype(BF16)
    hn_ref[...] = hn_s
    _, comb = _route(jnp.dot(hn_s, wr, preferred_element_type=F32) + br)
    comb_ref[...] = comb


def moe_sort(x, g, w_grp, b_grp, w_exp, b_exp):
    t, d = x.shape
    tm = MOE_TILE
    nr = N_GROUPS + N_EXPERTS
    wr = jnp.pad(jnp.concatenate([w_grp, w_exp], axis=1), ((0, 0), (0, LANE - nr))).astype(BF16)
    br = jnp.pad(jnp.concatenate([b_grp, b_exp]), (0, LANE - nr)).reshape(1, LANE)
    row = lambda i: (i, 0)
    hn, comb, pos, off = pl.pallas_call(
        _moe_sort_kernel,
        out_shape=(jax.ShapeDtypeStruct((t, d), BF16), jax.ShapeDtypeStruct((t, LANE), F32),
                   jax.ShapeDtypeStruct((t, 1), jnp.int32), jax.ShapeDtypeStruct((t // tm, 1, LANE), jnp.int32)),
        grid=(t // tm,),
        in_specs=[pl.BlockSpec((tm, d), row), pl.BlockSpec((1, d), lambda i: (0, 0)),
                  pl.BlockSpec((d, LANE), lambda i: (0, 0)), pl.BlockSpec((1, LANE), lambda i: (0, 0))],
        out_specs=(pl.BlockSpec((tm, d), row), pl.BlockSpec((tm, LANE), row), pl.BlockSpec((tm, 1), row),
                   pl.BlockSpec((None, 1, LANE), lambda i: (i, 0, 0))),
        compiler_params=_params("parallel"),
        name="moe_sort",
    )(x, g.reshape(1, d), wr, br)
    bounds = jnp.concatenate([off[:, 0, :N_GROUPS], jnp.full((t // tm, 1), tm, jnp.int32)], axis=1)
    return hn, comb, pos, bounds.reshape(-1)


def _moe_group_kernel(bounds_ref, hn_ref, c_ref, wg_ref, wu_ref, wd_ref, y_ref):
    i, g, j = pl.program_id(0), pl.program_id(1), pl.program_id(2)

    @pl.when((g == 0) & (j == 0))
    def _():
        y_ref[...] = jnp.zeros_like(y_ref)

    expert_lane = N_GROUPS + g * EXPERTS_PER_GROUP + j

    def sub_tile(want, size):
        r0 = pl.multiple_of(jnp.minimum(want, lo + MOE_TILE - size), ROW_ALIGN)
        rows = pl.ds(r0, size)
        x = hn_ref[rows, :]
        hg = jnp.dot(x, wg_ref[...], preferred_element_type=F32)
        hu = jnp.dot(x, wu_ref[...], preferred_element_type=F32)
        lane = lax.broadcasted_iota(jnp.int32, (size, LANE), 1)
        c = jnp.sum(jnp.where(lane == expert_lane, c_ref[rows, :], 0.0), axis=-1, keepdims=True)
        row = lax.broadcasted_iota(jnp.int32, (size, 1), 0)
        c = jnp.where(row + r0 >= want, c, 0.0)
        hid = (hg * jax.nn.sigmoid(hg)) * hu * c
        y_ref[rows, :] += jnp.dot(hid.astype(BF16), wd_ref[...], preferred_element_type=F32)

    for half in range(MOE_PACK):
        base = (i * MOE_PACK + half) * (N_GROUPS + 1) + g
        lo = half * MOE_TILE
        start, end = lo + bounds_ref[base], lo + bounds_ref[base + 1]
        first = (start // ROW_ALIGN) * ROW_ALIGN
        n_full = (end - first) // MOE_SUB
        rest = end - first - n_full * MOE_SUB

        def full(k, carry, first=first):
            sub_tile(first + k * MOE_SUB, MOE_SUB)
            return carry

        lax.fori_loop(0, n_full, full, 0)
        tail = first + n_full * MOE_SUB

        @pl.when(rest > MOE_SUB // 2)
        def _():
            sub_tile(tail, MOE_SUB)

        @pl.when((rest > 0) & (rest <= MOE_SUB // 2))
        def _():
            sub_tile(tail, MOE_SUB // 2)


def moe_group_experts(hn, comb, bounds, w_gate, w_up, w_down):
    t, d = hn.shape
    tm = MOE_TILE * MOE_PACK
    ne, _, ff = w_gate.shape
    ex = lambda i, g, j, b: (g * EXPERTS_PER_GROUP + j, 0, 0)
    once = pl.Buffered(1) if MOE_PACK > 1 else None
    return pl.pallas_call(
        _moe_group_kernel,
        out_shape=jax.ShapeDtypeStruct((t, d), F32),
        grid_spec=pltpu.PrefetchScalarGridSpec(
            num_scalar_prefetch=1,
            grid=(t // tm, N_GROUPS, EXPERTS_PER_GROUP),
            in_specs=[pl.BlockSpec((tm, d), lambda i, g, j, b: (i, 0), pipeline_mode=once),
                      pl.BlockSpec((tm, LANE), lambda i, g, j, b: (i, 0)),
                      pl.BlockSpec((None, d, ff), ex), pl.BlockSpec((None, d, ff), ex),
                      pl.BlockSpec((None, ff, d), ex)],
            out_specs=pl.BlockSpec((tm, d), lambda i, g, j, b: (i, 0), pipeline_mode=once)),
        compiler_params=_params("parallel", "arbitrary", "arbitrary"),
        name="moe_group_experts",
    )(bounds, hn, comb, w_gate, w_up, w_down)


def _moe_unsort_kernel(y_ref, pos_ref, x_ref, o_ref):
    tm = y_ref.shape[0]
    perm_t = (lax.broadcasted_iota(jnp.int32, (tm, tm), 1) == pos_ref[...]).astype(BF16)
    y = y_ref[...]
    hi = y.astype(BF16)
    lo = (y - hi.astype(F32)).astype(BF16)
    o_ref[...] = (x_ref[...] + jnp.dot(perm_t, hi, preferred_element_type=F32)
                  + jnp.dot(perm_t, lo, preferred_element_type=F32))


def moe_unsort(y, pos, x):
    t, d = x.shape
    tm, tn = MOE_TILE, d // 2
    blk = pl.BlockSpec((tm, tn), lambda i, j: (i, j))
    return pl.pallas_call(
        _moe_unsort_kernel,
        out_shape=jax.ShapeDtypeStruct((t, d), F32),
        grid=(t // tm, d // tn),
        in_specs=[blk, pl.BlockSpec((tm, 1), lambda i, j: (i, 0)), blk],
        out_specs=blk,
        compiler_params=_params("parallel", "arbitrary"),
        name="moe_unsort",
    )(y, pos, x)


def _moe_unsort_norm_kernel(y_ref, pos_ref, x_ref, g_ref, o_ref):
    tm, ts = pos_ref.shape[0], y_ref.shape[0]
    perm_t = (lax.broadcasted_iota(jnp.int32, (tm, ts), 1) == pos_ref[...]).astype(BF16)
    y = y_ref[...]
    hi = y.astype(BF16)
    lo = (y - hi.astype(F32)).astype(BF16)
    x = (x_ref[...] + jnp.dot(perm_t, hi, preferred_element_type=F32)
         + jnp.dot(perm_t, lo, preferred_element_type=F32))
    o_ref[...] = x * lax.rsqrt(jnp.mean(x * x, axis=-1, keepdims=True) + EPS) * g_ref[...]


def moe_unsort_norm(y, pos, x, g, *, row0, rows, tm=512):
    d = x.shape[1]
    per = MOE_TILE // tm
    r0 = row0 // tm
    return pl.pallas_call(
        _moe_unsort_norm_kernel,
        out_shape=jax.ShapeDtypeStruct((rows, d), F32),
        grid=(rows // tm,),
        in_specs=[pl.BlockSpec((MOE_TILE, d), lambda i: ((i + r0) // per, 0)),
                  pl.BlockSpec((tm, 1), lambda i: (i + r0, 0)),
                  pl.BlockSpec((tm, d), lambda i: (i + r0, 0)),
                  pl.BlockSpec((1, d), lambda i: (0, 0))],
        out_specs=pl.BlockSpec((tm, d), lambda i: (i, 0)),
        compiler_params=_params("parallel"),
        name="moe_unsort_norm",
    )(y, pos, x, g.reshape(1, d))


def hier_moe(x, g, w_grp, b_grp, w_exp, b_exp, w_gate, w_up, w_down, *, final=None):
    hn, comb, pos, bounds = moe_sort(x, g, w_grp, b_grp, w_exp, b_exp)
    y = moe_group_experts(hn, comb, bounds, w_gate.astype(BF16), w_up.astype(BF16), w_down.astype(BF16))
    if final is None:
        return moe_unsort(y, pos, x)
    gain, counts = final
    starts = np.cumsum((0,) + tuple(counts))[:-1]
    return tuple(moe_unsort_norm(y, pos, x, gain, row0=int(r0), rows=int(n)) for r0, n in zip(starts, counts))


def even_layer(x, bsz, s, norm_g, rel_bias, w_in, w_out, w2_f, b_f, w2_b, b_b, onorm):
    n_pad = _round_up(EVEN_IN, LANE)
    w_in_p = jnp.pad(w_in, ((0, 0), (0, n_pad - EVEN_IN))).astype(BF16)
    proj = norm_linear(x, norm_g, w_in_p, tn_target=896).reshape(bsz, s, n_pad)
    tq = ATTN_TQ
    ya = attention(proj, proj, proj, heads=A_HEADS, dq=HEAD_DIM, dv=HEAD_DIM,
                   q_off=0, k_off=A_HEADS, v_off=2 * A_HEADS, scale=HEAD_DIM ** -0.5,
                   bias=dilated_bias_table(rel_bias, s, tq), tq=tq)
    q_col = 3 * A_WIDTH
    yb = gla_mixer(proj, w2_f, w2_b, b_f, b_b, onorm, q_col=q_col, k_col=q_col + B_KEYW,
                   v_col=q_col + 2 * B_KEYW, g_col=q_col + 2 * B_KEYW + B_WIDTH,
                   z_col=q_col + 2 * B_KEYW + 2 * B_WIDTH)
    t = bsz * s
    return out_proj(ya.reshape(t, A_WIDTH), yb.reshape(t, B_WIDTH), w_out.astype(BF16), x)


def _odd_columns(w_in, mu):
    c0 = C_IN
    cut = lambda u, lo, n: u[..., lo:lo + n]
    zpad = lambda u, n: jnp.pad(u, [(0, 0)] * (u.ndim - 1) + [(0, n)])
    off = np.cumsum((0,) + D_SPLITS)
    def rwkv_cols(u):
        parts = [cut(u, off[0], 3 * D_WIDTH), cut(u, off[6], D_G_RANK), cut(u, off[3], 2 * D_W_RANK),
                 cut(u, off[5], D_A_RANK)]
        u = jnp.concatenate(parts, axis=-1)
        return zpad(u, DC_PAD - u.shape[-1])
    w_kr = w_in[:, C_Q_RANK + C_KV_RANK:C_IN]
    w_all = jnp.concatenate([rwkv_cols(w_in[:, c0:]), w_in[:, :C_IN], _rot_half_cols(w_kr)], axis=1)
    n_pad = _round_up(w_all.shape[1], 9 * LANE)
    return zpad(w_all, n_pad - w_all.shape[1]).astype(BF16), rwkv_cols(mu)


def odd_layer(x, bsz, s, norm_g, w_in, w_out, q_norm, w_uq, kv_norm, w_ukv, mu, w0_f, w2_f, w0_b, w2_b,
              a0, a2, g2, k_k, k_a, r_k, ln_g, ln_b):
    t = bsz * s
    w_all, mu_cols = _odd_columns(w_in, mu)
    proj = norm_linear(x, norm_g, w_all, tn_target=1152)
    inv = 1.0 / (ROPE_THETA ** (jnp.arange(0, C_ROPE, 2, dtype=F32) / C_ROPE))
    ang = jnp.arange(s, dtype=F32)[:, None] * inv[None, :]
    cos = jnp.pad(jnp.concatenate([jnp.cos(ang)] * 2, axis=1), ((0, 0), (0, LANE - C_ROPE)), constant_values=1.0)
    sin = jnp.pad(jnp.concatenate([jnp.sin(ang)] * 2, axis=1), ((0, 0), (0, LANE - C_ROPE)))
    q, k, v = mla_up(proj, q_norm, w_uq, kv_norm, w_ukv, jnp.tile(cos, (bsz, 1)), jnp.tile(sin, (bsz, 1)),
                     col0=DC_PAD)
    r3 = lambda u: u.reshape(bsz, s, -1)
    yc = attention(r3(q), r3(k), r3(v), heads=C_HEADS, dq=C_QK, dv=C_V, q_off=0, k_off=0, v_off=0,
                   scale=(C_NOPE + C_ROPE) ** -0.5)
    yd = rwkv7_mixer(proj, bsz, s, mu_cols, w0_f, w2_f, w0_b, w2_b, a0, a2, g2, k_k, k_a, r_k, ln_g, ln_b)
    return out_proj(yc.reshape(t, C_WIDTH), yd, w_out.astype(BF16), x)


def kernel(x_prompt, x_sample, rel_bias, norm_mix, norm_ffn, norm_final, ev_w_in, ev_w_out, ev_gla_w2_f, ev_gla_b_f, ev_gla_w2_b, ev_gla_b_b, ev_gla_onorm, od_w_in, od_w_out, od_q_norm, od_w_uq, od_kv_norm, od_w_ukv, od_mu, od_w0_f, od_w2_f, od_w0_b, od_w2_b, od_a0, od_a2, od_g2, od_k_k, od_k_a, od_r_k, od_ln_g, od_ln_b, moe_w_grp, moe_b_grp, moe_w_exp, moe_b_exp, moe_w_gate, moe_w_up, moe_w_down):
    nb_p = x_prompt.shape[0]
    x = jnp.concatenate([x_prompt, x_sample], axis=0)
    bsz, s, d = x.shape
    x = x.reshape(bsz * s, d)
    for i in range(DEPTH):
        j = i // 2
        if i % 2 == 0:
            x = even_layer(x, bsz, s, norm_mix[i], rel_bias, ev_w_in[j], ev_w_out[j], ev_gla_w2_f[j],
                           ev_gla_b_f[j], ev_gla_w2_b[j], ev_gla_b_b[j], ev_gla_onorm[j])
        else:
            x = odd_layer(x, bsz, s, norm_mix[i], od_w_in[j], od_w_out[j], od_q_norm[j], od_w_uq[j],
                          od_kv_norm[j], od_w_ukv[j], od_mu[j], od_w0_f[j], od_w2_f[j], od_w0_b[j],
                          od_w2_b[j], od_a0[j], od_a2[j], od_g2[j], od_k_k[j], od_k_a[j], od_r_k[j],
                          od_ln_g[j], od_ln_b[j])
        final = (norm_final, (nb_p * s, (bsz - nb_p) * s)) if i == DEPTH - 1 else None
        x = hier_moe(x, norm_ffn[i], moe_w_grp[i], moe_b_grp[i], moe_w_exp[i], moe_b_exp[i],
                     moe_w_gate[i], moe_w_up[i], moe_w_down[i], final=final)
    y_p, y_s = x
    return (y_p.reshape(nb_p, s, d), y_s.reshape(bsz - nb_p, s, d))
```

```python
import functools

import jax, jax.numpy as jnp
from jax import lax
import numpy as np
from jax.experimental import pallas as pl
from jax.experimental.pallas import tpu as pltpu

F32, BF16 = jnp.float32, jnp.bfloat16

D_MODEL = 2048
DEPTH = 2
MIX_HALF = D_MODEL // 2
HEAD_DIM = 128
EPS = 1e-6
NEG = -1e30

A_HEADS = MIX_HALF // HEAD_DIM
A_WIDTH = A_HEADS * HEAD_DIM
A_PATTERNS = ((128, 1), (512, 4), (2048, 16))
N_BUCKETS = 32
MAX_DISTANCE = 1024

B_HEADS = 4
B_DV = MIX_HALF // B_HEADS
B_DK = B_DV // 2
B_WIDTH = B_HEADS * B_DV
B_KEYW = B_HEADS * B_DK
B_GATE_RANK = 16
B_GATE_TAU = 16.0
B_CHUNK = 64

C_HEADS = MIX_HALF // 128
C_Q_RANK = 512
C_KV_RANK = 256
C_NOPE = 128
C_ROPE = 64
C_V = 128
C_WIDTH = C_HEADS * C_V
C_QK = 256
ROPE_THETA = 10000.0

D_HEAD = 64
D_HEADS = MIX_HALF // D_HEAD
D_WIDTH = D_HEADS * D_HEAD
D_W_RANK = 64
D_A_RANK = 64
D_G_RANK = 128
D_LN_EPS = 64e-5
D_SPLITS = (D_WIDTH, D_WIDTH, D_WIDTH, D_W_RANK, D_W_RANK, D_A_RANK, D_G_RANK)
D_SHIFT = 3 * D_WIDTH + 2 * D_W_RANK + D_A_RANK + D_G_RANK

N_GROUPS = 4
EXPERTS_PER_GROUP = 4
N_EXPERTS = N_GROUPS * EXPERTS_PER_GROUP

EVEN_IN = 3 * A_WIDTH + 2 * B_KEYW + 2 * B_WIDTH + 2 * B_GATE_RANK
C_IN = C_Q_RANK + C_KV_RANK + C_ROPE

LANE = 128
VMEM_LIMIT = 52 * 1024 * 1024


def _params(*sem):
    return pltpu.CompilerParams(dimension_semantics=sem, vmem_limit_bytes=VMEM_LIMIT)


def _round_up(n, m):
    return -(-n // m) * m


NT_DIMS = (((1,), (1,)), ((), ()))
TN_DIMS = (((0,), (0,)), ((), ()))


def _split3(x):
    hi = x.astype(BF16)
    r1 = x - hi.astype(F32)
    mid = r1.astype(BF16)
    lo = (r1 - mid.astype(F32)).astype(BF16)
    return hi, mid, lo


def _pick_tile(n, target):
    best = LANE
    for t in range(LANE, target + 1, LANE):
        if n % t == 0:
            best = t
    return best


def _norm_linear_kernel(x_ref, g_ref, w_ref, o_ref, xn_ref):
    @pl.when(pl.program_id(1) == 0)
    def _():
        x = x_ref[...]
        y = x * lax.rsqrt(jnp.mean(x * x, axis=-1, keepdims=True) + EPS) * g_ref[...]
        xn_ref[...] = y.astype(BF16)

    o_ref[...] = jnp.dot(xn_ref[...], w_ref[...], preferred_element_type=F32)


def norm_linear(x, g, w, *, tm=1024, tn_target=1024):
    t, k = x.shape
    n = w.shape[1]
    tn = _pick_tile(n, tn_target)
    return pl.pallas_call(
        _norm_linear_kernel,
        out_shape=jax.ShapeDtypeStruct((t, n), F32),
        grid=(t // tm, n // tn),
        in_specs=[pl.BlockSpec((tm, k), lambda i, j: (i, 0)),
                  pl.BlockSpec((1, k), lambda i, j: (0, 0)),
                  pl.BlockSpec((k, tn), lambda i, j: (0, j))],
        out_specs=pl.BlockSpec((tm, tn), lambda i, j: (i, j)),
        scratch_shapes=[pltpu.VMEM((tm, k), BF16)],
        compiler_params=_params("parallel", "arbitrary"),
        name="norm_linear",
    )(x, g.reshape(1, k), w)


def _out_proj_kernel(a_ref, b_ref, wa_ref, wb_ref, x_ref, o_ref):
    acc = jnp.dot(a_ref[...], wa_ref[...], preferred_element_type=F32)
    acc += jnp.dot(b_ref[...], wb_ref[...], preferred_element_type=F32)
    o_ref[...] = x_ref[...] + acc


def out_proj(a, b, w, x, *, tm=1024, tn=512):
    t, ka = a.shape
    kb = b.shape[1]
    n = w.shape[1]
    return pl.pallas_call(
        _out_proj_kernel,
        out_shape=jax.ShapeDtypeStruct((t, n), F32),
        grid=(t // tm, n // tn),
        in_specs=[pl.BlockSpec((tm, ka), lambda i, j: (i, 0)),
                  pl.BlockSpec((tm, kb), lambda i, j: (i, 0)),
                  pl.BlockSpec((ka, tn), lambda i, j: (0, j)),
                  pl.BlockSpec((kb, tn), lambda i, j: (0, j)),
                  pl.BlockSpec((tm, tn), lambda i, j: (i, j))],
        out_specs=pl.BlockSpec((tm, tn), lambda i, j: (i, j)),
        compiler_params=_params("parallel", "arbitrary"),
        name="out_proj",
    )(a, b, w[:ka], w[ka:], x)


ATTN_KB = 256
ATTN_TQ = 512


def _attn_kernel(*refs, scale, has_bias):
    if has_bias:
        q_ref, k_ref, v_ref, bias_ref, o_ref, kb_ref, vb_ref = refs
    else:
        q_ref, k_ref, v_ref, o_ref, kb_ref, vb_ref = refs
    s_len = k_ref.shape[0]

    @pl.when(pl.program_id(2) == 0)
    def _():
        kb_ref[...] = k_ref[...].astype(BF16)
        vb_ref[...] = v_ref[...].astype(BF16)

    q = (q_ref[...] * scale).astype(BF16)
    blocks = [slice(j * ATTN_KB, (j + 1) * ATTN_KB) for j in range(s_len // ATTN_KB)]
    scores = []
    m = None
    for blk in blocks:
        sj = lax.dot_general(q, kb_ref[blk, :], NT_DIMS, preferred_element_type=F32)
        if has_bias:
            sj = sj + bias_ref[:, blk]
        mj = jnp.max(sj, axis=-1, keepdims=True)
        m = mj if m is None else jnp.maximum(m, mj)
        scores.append(sj)
    o = den = None
    for blk, sj in zip(blocks, scores):
        p = jnp.exp(sj - m)
        dj = jnp.sum(p, axis=-1, keepdims=True)
        oj = jnp.dot(p.astype(BF16), vb_ref[blk, :], preferred_element_type=F32)
        o, den = (oj, dj) if o is None else (o + oj, den + dj)
    o_ref[...] = (o / den).astype(o_ref.dtype)


def attention(q, k, v, *, heads, dq, dv, q_off, k_off, v_off, scale, bias=None, tq=ATTN_TQ):
    b, s, _ = q.shape
    nq = s // tq
    in_specs = [pl.BlockSpec((None, tq, dq), lambda bi, h, qi: (bi, qi, q_off + h)),
                pl.BlockSpec((None, s, dq), lambda bi, h, qi: (bi, 0, k_off + h)),
                pl.BlockSpec((None, s, dv), lambda bi, h, qi: (bi, 0, v_off + h))]
    args = [q, k, v]
    if bias is not None:
        in_specs.append(pl.BlockSpec((None, None, tq, s), lambda bi, h, qi: (h, qi, 0, 0)))
        args.append(bias)
    return pl.pallas_call(
        functools.partial(_attn_kernel, scale=scale, has_bias=bias is not None),
        out_shape=jax.ShapeDtypeStruct((b, s, heads * dv), BF16),
        grid=(b, heads, nq),
        in_specs=in_specs,
        out_specs=pl.BlockSpec((None, tq, dv), lambda bi, h, qi: (bi, qi, h)),
        scratch_shapes=[pltpu.VMEM((s, dq), BF16), pltpu.VMEM((s, dv), BF16)],
        compiler_params=_params("parallel", "parallel", "arbitrary"),
        name="attention_bias" if bias is not None else "attention",
    )(*args)


def _t5_bucket(rel):
    half = N_BUCKETS // 2
    exact = half // 2
    n = np.abs(rel)
    large = exact + (np.log(np.maximum(n, 1) / exact) / np.log(MAX_DISTANCE / exact) * (half - exact)).astype(np.int64)
    large = np.minimum(large, half - 1)
    return ((rel > 0) * half + np.where(n < exact, n, large)).astype(np.int32)


def dilated_bias_table(rel_bias, s, tq):
    heads = rel_bias.shape[1]
    d = np.arange(-(s - 1), s)
    count = np.zeros(d.shape, np.float32)
    for window, dil in A_PATTERNS:
        count += ((d % dil == 0) & (np.abs(d) <= (window // (2 * dil)) * dil)).astype(np.float32)
    logc = np.where(count > 0, np.log(np.maximum(count, 1.0)), NEG).astype(np.float32)
    onehot = (_t5_bucket(d)[:, None] == np.arange(N_BUCKETS)[None, :]).astype(np.float32)
    line = jnp.transpose(jnp.dot(onehot, rel_bias.astype(F32), precision=lax.Precision.HIGHEST)) + logc[None]
    width = 2 * s
    line = jnp.pad(line, ((0, 0), (0, width - line.shape[1])))[:, None, :]
    nq = s // tq
    return pl.pallas_call(
        functools.partial(_skew_kernel, tq=tq, nq=nq),
        out_shape=jax.ShapeDtypeStruct((heads, nq, tq, s), F32),
        grid=(heads, nq),
        in_specs=[pl.BlockSpec((None, 1, width), lambda h, qi: (h, 0, 0))],
        out_specs=pl.BlockSpec((None, None, tq, s), lambda h, qi: (h, qi, 0, 0)),
        compiler_params=_params("parallel", "arbitrary"),
        name="bias_skew",
    )(line)


def _skew_kernel(line_ref, o_ref, *, tq, nq):
    width = line_ref.shape[1]
    first = (nq - 1 - pl.program_id(1)) * tq
    x = jnp.broadcast_to(line_ref[...], (tq, width))
    x = pltpu.roll(x, width - (tq - 1) - first, 1, stride=1, stride_axis=0)
    o_ref[...] = x[:, :o_ref.shape[1]]


GLA_UNROLL = 8


def _gla_kernel(q_ref, k_ref, v_ref, g_ref, z_ref, w2f_ref, w2b_ref, bf_ref, bb_ref, on_ref, o_ref,
                la_ref, acc_ref, qcat_ref, upd_ref, dec_ref, scat_ref, st_ref):
    s_len = q_ref.shape[0]
    c = B_CHUNK
    nchunk = s_len // c
    z = z_ref[...].astype(BF16)
    gate = lambda w2_ref, b_ref: jax.nn.log_sigmoid(
        jnp.dot(z, w2_ref[...], preferred_element_type=F32) + b_ref[...]) * (1.0 / B_GATE_TAU)
    la_ref[0] = gate(w2f_ref, bf_ref)
    la_ref[1] = gate(w2b_ref, bb_ref)

    ri = lax.broadcasted_iota(jnp.int32, (c, c), 0)
    ci = lax.broadcasted_iota(jnp.int32, (c, c), 1)
    keep = (ri >= ci, ri <= ci)
    tri3 = tuple(jnp.concatenate([kp.astype(BF16)] * 3, axis=1) for kp in keep)

    def chunk_rows(n):
        return pl.ds(pl.multiple_of(n * c, c), c)

    def pass1(i, carry):
        units = [(i * GLA_UNROLL + u, d) for u in range(GLA_UNROLL) for d in range(2)]
        st = []
        for n, d in units:
            rows = chunk_rows(n)
            gcum = jnp.dot(tri3[d], jnp.concatenate(_split3(la_ref[d, rows, :]), axis=0),
                           preferred_element_type=F32)
            st.append(dict(rows=rows, gcum=gcum))
        for (n, d), c in zip(units, st):
            gcum = c["gcum"]
            gend = gcum[0:1] if d == 1 else gcum[B_CHUNK - 1:B_CHUNK]
            kc = k_ref[c["rows"], :]
            c["q_in"] = (q_ref[c["rows"], :] * (B_DK ** -0.5) * jnp.exp(gcum)).astype(BF16)
            c["k_out"] = (kc * jnp.exp(gend - gcum)).astype(BF16)
            c["att"] = lax.dot_general(c["q_in"], (kc * jnp.exp(-gcum)).astype(BF16), NT_DIMS,
                                       preferred_element_type=F32)
            dec_ref[d, n] = jnp.broadcast_to(jnp.exp(gend), (8, B_DK))
        for (n, d), c in zip(units, st):
            vc = v_ref[c["rows"], :].astype(BF16)
            c["o"] = jnp.dot(jnp.where(keep[d], c["att"], 0.0).astype(BF16), vc, preferred_element_type=F32)
            upd_ref[d, n] = lax.dot_general(vc, c["k_out"], TN_DIMS, preferred_element_type=F32)
            qcat_ref[c["rows"], d * B_DK:(d + 1) * B_DK] = c["q_in"]
        for u in range(GLA_UNROLL):
            acc_ref[st[2 * u]["rows"], :] = st[2 * u]["o"] + st[2 * u + 1]["o"]
        return carry

    lax.fori_loop(0, nchunk // GLA_UNROLL, pass1, 0)

    st_ref[...] = jnp.zeros_like(st_ref)

    def pass2(n, carry):
        for d, m in ((0, n), (1, nchunk - 1 - n)):
            state = st_ref[d]
            scat_ref[m, :, d * B_DK:(d + 1) * B_DK] = state.astype(BF16)
            st_ref[d] = state * dec_ref[d, m][0:1] + upd_ref[d, m]
        return carry

    lax.fori_loop(0, nchunk, pass2, 0)

    def pass3(i, carry):
        rows = [chunk_rows(i * GLA_UNROLL + u) for u in range(GLA_UNROLL)]
        outs = [acc_ref[r, :] + lax.dot_general(qcat_ref[r, :], scat_ref[i * GLA_UNROLL + u], NT_DIMS,
                                                preferred_element_type=F32) for u, r in enumerate(rows)]
        for r, o in zip(rows, outs):
            o = o * lax.rsqrt(jnp.mean(o * o, axis=-1, keepdims=True) + EPS) * on_ref[...]
            g = g_ref[r, :]
            o_ref[r, :] = (o * (g * jax.nn.sigmoid(g))).astype(o_ref.dtype)
        return carry

    lax.fori_loop(0, nchunk // GLA_UNROLL, pass3, 0)


def gla_mixer(proj, w2f, w2b, b_f, b_b, onorm, *, q_col, k_col, v_col, g_col, z_col):
    b, s, _ = proj.shape
    hm = lambda blk: (lambda bi, h: (bi, 0, blk + h))
    w2f_p = jnp.zeros((LANE, B_KEYW), F32).at[:B_GATE_RANK].set(w2f).astype(BF16)
    w2b_p = jnp.zeros((LANE, B_KEYW), F32).at[B_GATE_RANK:2 * B_GATE_RANK].set(w2b).astype(BF16)
    return pl.pallas_call(
        _gla_kernel,
        out_shape=jax.ShapeDtypeStruct((b, s, B_WIDTH), BF16),
        grid=(b, B_HEADS),
        in_specs=[pl.BlockSpec((None, s, B_DK), hm(q_col // B_DK)),
                  pl.BlockSpec((None, s, B_DK), hm(k_col // B_DK)),
                  pl.BlockSpec((None, s, B_DV), hm(v_col // B_DV)),
                  pl.BlockSpec((None, s, B_DV), hm(g_col // B_DV)),
                  pl.BlockSpec((None, s, LANE), lambda bi, h: (bi, 0, z_col // LANE)),
                  pl.BlockSpec((LANE, B_DK), lambda bi, h: (0, h)),
                  pl.BlockSpec((LANE, B_DK), lambda bi, h: (0, h)),
                  pl.BlockSpec((1, B_DK), lambda bi, h: (0, h)),
                  pl.BlockSpec((1, B_DK), lambda bi, h: (0, h)),
                  pl.BlockSpec((1, B_DV), lambda bi, h: (0, 0))],
        out_specs=pl.BlockSpec((None, s, B_DV), lambda bi, h: (bi, 0, h)),
        scratch_shapes=[pltpu.VMEM((2, s, B_DK), F32),
                        pltpu.VMEM((s, B_DV), F32),
                        pltpu.VMEM((s, 2 * B_DK), BF16),
                        pltpu.VMEM((2, s // B_CHUNK, B_DV, B_DK), F32),
                        pltpu.VMEM((2, s // B_CHUNK, 8, B_DK), F32),
                        pltpu.VMEM((s // B_CHUNK, B_DV, 2 * B_DK), BF16),
                        pltpu.VMEM((2, B_DV, B_DK), F32)],
        compiler_params=_params("parallel", "arbitrary"),
        name="gla_mixer",
    )(proj, proj, proj, proj, proj, w2f_p, w2b_p, b_f.reshape(1, -1), b_b.reshape(1, -1), onorm.reshape(1, -1))


def _mla_up_kernel(cq_ref, ckv_ref, kr_ref, qn_ref, kvn_ref, wq_ref, wqr_ref, wkv_ref, cos_ref, sin_ref,
                   q_ref, k_ref, v_ref):
    def rms(x, g):
        return (x * lax.rsqrt(jnp.mean(x * x, axis=-1, keepdims=True) + EPS) * g).astype(BF16)

    cq = rms(cq_ref[...], qn_ref[...])
    ckv = rms(ckv_ref[...], kvn_ref[...])
    cos, sin = cos_ref[...], sin_ref[...]
    kr = kr_ref[...]
    k_rope = kr * cos + pltpu.roll(kr, LANE - C_ROPE, 1) * sin
    lane = lax.broadcasted_iota(jnp.int32, k_rope.shape, 1)
    k_rope = jnp.where(lane < C_ROPE, k_rope, 0.0)
    for h in range(C_HEADS):
        q = jnp.dot(cq, wq_ref[:, h * C_QK:(h + 1) * C_QK], preferred_element_type=F32)
        qp = jnp.dot(cq, wqr_ref[:, h * LANE:(h + 1) * LANE], preferred_element_type=F32)
        q_ref[:, h * C_QK:h * C_QK + C_NOPE] = q[:, :C_NOPE]
        q_ref[:, h * C_QK + C_NOPE:(h + 1) * C_QK] = q[:, C_NOPE:] * cos + qp * sin
        kv = jnp.dot(ckv, wkv_ref[:, h * 2 * LANE:(h + 1) * 2 * LANE], preferred_element_type=F32)
        k_ref[:, h * C_QK:h * C_QK + C_NOPE] = kv[:, :C_NOPE]
        k_ref[:, h * C_QK + C_NOPE:(h + 1) * C_QK] = k_rope
        v_ref[:, h * C_V:(h + 1) * C_V] = kv[:, C_NOPE:]


def _rot_half_cols(w):
    half = w.shape[-1] // 2
    return jnp.concatenate([-w[..., half:], w[..., :half]], axis=-1)


def mla_up(proj, q_norm, w_uq, kv_norm, w_ukv, cos, sin, *, col0, tm=512):
    t = proj.shape[0]
    wq = w_uq.reshape(C_Q_RANK, C_HEADS, C_NOPE + C_ROPE)
    wq_main = jnp.pad(wq, ((0, 0), (0, 0), (0, C_QK - C_NOPE - C_ROPE))).reshape(C_Q_RANK, C_HEADS * C_QK)
    wq_rot = jnp.pad(_rot_half_cols(wq[..., C_NOPE:]), ((0, 0), (0, 0), (0, LANE - C_ROPE)))
    wq_rot = wq_rot.reshape(C_Q_RANK, C_HEADS * LANE)
    row = lambda i: (i, 0)
    full = lambda arr: pl.BlockSpec(arr.shape, lambda i: (0, 0))
    g_q, g_kv = q_norm.reshape(1, -1), kv_norm.reshape(1, -1)
    wq_main, wq_rot, wkv = wq_main.astype(BF16), wq_rot.astype(BF16), w_ukv.astype(BF16)
    return pl.pallas_call(
        _mla_up_kernel,
        out_shape=(jax.ShapeDtypeStruct((t, C_HEADS * C_QK), F32),
                   jax.ShapeDtypeStruct((t, C_HEADS * C_QK), F32),
                   jax.ShapeDtypeStruct((t, C_WIDTH), F32)),
        grid=(t // tm,),
        in_specs=[pl.BlockSpec((tm, C_Q_RANK), lambda i: (i, col0 // C_Q_RANK)),
                  pl.BlockSpec((tm, C_KV_RANK), lambda i: (i, (col0 + C_Q_RANK) // C_KV_RANK)),
                  pl.BlockSpec((tm, LANE), lambda i: (i, (col0 + C_Q_RANK + C_KV_RANK) // LANE)),
                  full(g_q), full(g_kv), full(wq_main), full(wq_rot), full(wkv),
                  pl.BlockSpec((tm, LANE), row), pl.BlockSpec((tm, LANE), row)],
        out_specs=(pl.BlockSpec((tm, C_HEADS * C_QK), row),
                   pl.BlockSpec((tm, C_HEADS * C_QK), row),
                   pl.BlockSpec((tm, C_WIDTH), row)),
        compiler_params=_params("parallel"),
        name="mla_up",
    )(proj, proj, proj, g_q, g_kv, wq_main, wq_rot, wkv, cos, sin)


RG = 4
RGW = RG * D_HEAD
RCH = 64
DC_R, DC_K, DC_V = 0, D_WIDTH, 2 * D_WIDTH
DC_ZG = 3 * D_WIDTH
DC_ZW = DC_ZG + D_G_RANK
DC_ZA = DC_ZW + 2 * D_W_RANK
DC_PAD = 7 * 512


def _head_sums(x, bo):
    return jnp.concatenate(
        [jnp.dot(x[:, RGW * g:RGW * (g + 1)], bo, preferred_element_type=F32, precision=lax.Precision.HIGHEST)
         for g in range(x.shape[1] // RGW)], axis=1)


def _block_ones():
    i = np.arange(RGW)
    return jnp.asarray((i[:, None] // D_HEAD) == (i[None, :] // D_HEAD), F32)


def _rwkv_prep_kernel(x_ref, xp_ref, xn_ref, mu_ref, w2f_ref, w2b_ref, a2_ref, g2_ref, w0f_ref, w0b_ref,
                      a0_ref, kk_ref, ka_ref, rk_ref, bo_ref,
                      r_ref, k_ref, v_ref, a_ref, b_ref, lwf_ref, lwb_ref, g_ref, bonus_ref, *, tiles_per_seq):
    i = pl.program_id(0) % tiles_per_seq
    x = x_ref[...]
    tm = x.shape[0]
    row = lax.broadcasted_iota(jnp.int32, x.shape, 0)
    prev_row = jnp.where(i == 0, 0.0, xp_ref[7:8, :])
    next_row = jnp.where(i == tiles_per_seq - 1, 0.0, xn_ref[0:1, :])
    prev = jnp.where(row == 0, prev_row, pltpu.roll(x, 1, 0))
    nxt = jnp.where(row == tm - 1, next_row, pltpu.roll(x, tm - 1, 0))
    x = x + mu_ref[...] * (0.5 * (prev + nxt) - x)
    r, k, v = x[:, DC_R:DC_R + D_WIDTH], x[:, DC_K:DC_K + D_WIDTH], x[:, DC_V:DC_V + D_WIDTH]
    zg = x[:, DC_ZG:DC_ZG + LANE]
    zw = x[:, DC_ZW:DC_ZW + LANE]
    za = x[:, DC_ZA:DC_ZA + LANE]
    tz = jnp.tanh(zw).astype(BF16)
    log_decay = lambda w0_ref, w2_ref: -np.exp(-0.5).astype(np.float32) * jax.nn.sigmoid(
        w0_ref[...] + jnp.dot(tz, w2_ref[...], preferred_element_type=F32))
    lwf_ref[...] = log_decay(w0f_ref, w2f_ref)
    lwb_ref[...] = log_decay(w0b_ref, w2b_ref)
    ag = jax.nn.sigmoid(a0_ref[...] + jnp.dot(za.astype(BF16), a2_ref[...], preferred_element_type=F32))
    g_ref[...] = jnp.dot(jax.nn.sigmoid(zg).astype(BF16), g2_ref[...], preferred_element_type=F32)
    bo = bo_ref[...]
    kk = k * kk_ref[...]
    kk = kk / jnp.maximum(jnp.sqrt(_head_sums(kk * kk, bo)), 1e-12)
    k = k * (1.0 + (ag - 1.0) * ka_ref[...])
    r_ref[...] = r
    k_ref[...] = k
    v_ref[...] = v
    a_ref[...] = -kk
    b_ref[...] = kk * ag
    bonus_ref[...] = _head_sums(r * k * rk_ref[...], bo) * v


def rwkv_prep(proj, mu, w0_f, w2_f, w0_b, w2_b, a0, a2, g2, k_k, k_a, r_k, *, seq, tm=256):
    t = proj.shape[0]
    tiles_per_seq = seq // tm
    hb = tm // 8
    nblk8 = t // 8
    pad_rows = lambda w, lo: jnp.zeros((LANE, D_WIDTH), F32).at[lo:lo + w.shape[0]].set(w).astype(BF16)
    vec = lambda u: u.reshape(1, -1)
    consts = [vec(mu), pad_rows(w2_f, 0), pad_rows(w2_b, D_W_RANK), pad_rows(a2, 0), g2.astype(BF16),
              vec(w0_f), vec(w0_b), vec(a0), vec(k_k), vec(k_a), vec(r_k), _block_ones()]
    full = lambda arr: pl.BlockSpec(arr.shape, lambda i: (0, 0))
    out_spec = pl.BlockSpec((tm, D_WIDTH), lambda i: (i, 0))
    return pl.pallas_call(
        functools.partial(_rwkv_prep_kernel, tiles_per_seq=tiles_per_seq),
        out_shape=tuple(jax.ShapeDtypeStruct((t, D_WIDTH), F32) for _ in range(9)),
        grid=(t // tm,),
        in_specs=[pl.BlockSpec((tm, DC_PAD), lambda i: (i, 0)),
                  pl.BlockSpec((8, DC_PAD), lambda i: (jnp.maximum(i * hb - 1, 0), 0)),
                  pl.BlockSpec((8, DC_PAD), lambda i: (jnp.minimum((i + 1) * hb, nblk8 - 1), 0))]
                 + [full(c) for c in consts],
        out_specs=tuple(out_spec for _ in range(9)),
        compiler_params=_params("parallel"),
        name="rwkv_prep",
    )(proj, proj, proj, *consts)


def _rwkv_chunk_kernel(*refs, ngroups):
    ins, (yf_ref, yb_ref, mt_ref) = refs[:12], refs[12:]

    @pl.when(pl.program_id(1) == 0)
    def _():
        mt_ref[...] = jnp.zeros_like(mt_ref)

    row = lax.broadcasted_iota(jnp.int32, (RCH, RGW), 0)
    col = lax.broadcasted_iota(jnp.int32, (RCH, RGW), 1) & (RCH - 1)
    bdmask = (lax.broadcasted_iota(jnp.int32, (RGW, RGW), 0) // D_HEAD
              == lax.broadcasted_iota(jnp.int32, (RGW, RGW), 1) // D_HEAD)
    tr = lax.broadcasted_iota(jnp.int32, (RCH, RCH), 0)
    tc = lax.broadcasted_iota(jnp.int32, (RCH, RCH), 1)
    zero = jnp.zeros((), F32)

    def bd(z):
        zb = z.astype(BF16)
        return jnp.where(bdmask, jnp.concatenate([zb] * RG, axis=0), jnp.zeros((), BF16))

    def mm(x, y, dims=None):
        x = x.astype(BF16)
        if dims is None:
            return jnp.dot(x, y, preferred_element_type=F32)
        return lax.dot_general(x, y, dims, preferred_element_type=F32)

    chains = [(d, g) for d in range(2) for g in range(ngroups)]
    st = []
    for d, g in chains:
        backward = d == 1
        r_ref, k_ref, v_ref, a_ref, b_ref, lw_ref = ins[6 * d:6 * d + 6]
        tri = ((tc >= tr) if backward else (tc <= tr)).astype(BF16)
        sl = slice(RGW * g, RGW * (g + 1))
        r, k, v, a, b, lw = (ref[:, sl] for ref in (r_ref, k_ref, v_ref, a_ref, b_ref, lw_ref))
        lam = jnp.dot(jnp.concatenate([tri] * 3, axis=1), jnp.concatenate(_split3(lw), axis=0),
                      preferred_element_type=F32)
        lamc = lam[0:1] if backward else lam[RCH - 1:RCH]
        e_inv = jnp.exp(-lam)
        e_out = jnp.exp(lamc - lam)
        ar = jnp.concatenate([a * jnp.exp(lam - lw), r * jnp.exp(lam)], axis=0).astype(BF16)
        bk = jnp.concatenate([b * e_out, k * e_out], axis=0).astype(BF16)
        st.append(dict(ar=ar, bk=bk, v=v, lamc=lamc, sl=sl,
                       gb=mm(ar, bd(b * e_inv), NT_DIMS), gk=mm(ar, bd(k * e_inv), NT_DIMS)))
    for (d, g), c in zip(chains, st):
        strict = (col > row) if d == 1 else (col < row)
        incl = (col >= row) if d == 1 else (col <= row)
        c["lp"] = jnp.where(strict, c["gb"][:RCH], zero)
        lak = jnp.where(strict, c["gk"][:RCH], zero)
        c["grb"] = jnp.where(incl, c["gb"][RCH:], zero).astype(BF16)
        c["grk"] = jnp.where(incl, c["gk"][RCH:], zero).astype(BF16)
        c["mt"] = mt_ref[d, g]
        amrm = mm(c["ar"], c["mt"].astype(BF16), NT_DIMS)
        c["bdv"] = bd(c["v"])
        c["u"] = amrm[:RCH] + mm(lak, c["bdv"])
        c["rm"] = amrm[RCH:]
    for rnd in range(6):
        for c in st:
            lpb = c["lp"].astype(BF16)
            c["u"] = c["u"] + mm(lpb, bd(c["u"]))
            if rnd < 5:
                c["lp"] = mm(lpb, bd(c["lp"]))
    for (d, g), c in zip(chains, st):
        y_ref = yb_ref if d == 1 else yf_ref
        y_ref[:, c["sl"]] = c["rm"] + mm(c["grb"], bd(c["u"])) + mm(c["grk"], c["bdv"])
        uv = jnp.concatenate([c["u"], c["v"]], axis=0).astype(BF16)
        upd = lax.dot_general(uv, c["bk"], TN_DIMS, preferred_element_type=F32)
        mt_ref[d, g] = c["mt"] * jnp.exp(c["lamc"]) + jnp.where(bdmask, upd, zero)


def rwkv_chunked(r, k, v, a, b, lwf, lwb):
    bsz, s, wd = r.shape
    nc = s // RCH
    fspec = pl.BlockSpec((None, RCH, wd), lambda bi, n: (bi, n, 0))
    bspec = pl.BlockSpec((None, RCH, wd), lambda bi, n: (bi, nc - 1 - n, 0))
    return pl.pallas_call(
        functools.partial(_rwkv_chunk_kernel, ngroups=wd // RGW),
        out_shape=(jax.ShapeDtypeStruct((bsz, s, wd), F32), jax.ShapeDtypeStruct((bsz, s, wd), F32)),
        grid=(bsz, nc),
        in_specs=[fspec] * 6 + [bspec] * 6,
        out_specs=(fspec, bspec),
        scratch_shapes=[pltpu.VMEM((2, wd // RGW, RGW, RGW), F32)],
        compiler_params=_params("parallel", "arbitrary"),
        name="rwkv_chunked",
    )(r, k, v, a, b, lwf, r, k, v, a, b, lwb)


def _rwkv_post_kernel(yf_ref, yb_ref, bonus_ref, g_ref, lng_ref, lnb_ref, bo_ref, o_ref):
    bo = bo_ref[...]
    y = yf_ref[...] + yb_ref[...]
    yc = y - _head_sums(y, bo) * (1.0 / D_HEAD)
    var = _head_sums(yc * yc, bo) * (1.0 / D_HEAD)
    y = yc * lax.rsqrt(var + D_LN_EPS) * lng_ref[...] + lnb_ref[...]
    o_ref[...] = ((y + bonus_ref[...]) * g_ref[...]).astype(o_ref.dtype)


def rwkv_post(yf, yb, bonus, g, ln_g, ln_b, *, tm=512):
    t, wd = yf.shape
    row = pl.BlockSpec((tm, wd), lambda i: (i, 0))
    vec = pl.BlockSpec((1, wd), lambda i: (0, 0))
    bo = _block_ones()
    return pl.pallas_call(
        _rwkv_post_kernel,
        out_shape=jax.ShapeDtypeStruct((t, wd), BF16),
        grid=(t // tm,),
        in_specs=[row, row, row, row, vec, vec, pl.BlockSpec(bo.shape, lambda i: (0, 0))],
        out_specs=row,
        compiler_params=_params("parallel"),
        name="rwkv_post",
    )(yf, yb, bonus, g, ln_g.reshape(1, wd), ln_b.reshape(1, wd), bo)


def rwkv7_mixer(proj, bsz, s, mu, w0_f, w2_f, w0_b, w2_b, a0, a2, g2, k_k, k_a, r_k, ln_g, ln_b):
    r, k, v, a, b, lwf, lwb, g, bonus = rwkv_prep(proj, mu, w0_f, w2_f, w0_b, w2_b, a0, a2, g2, k_k, k_a,
                                                  r_k.reshape(-1), seq=s)
    r3 = lambda u: u.reshape(bsz, s, D_WIDTH)
    yf, yb = rwkv_chunked(r3(r), r3(k), r3(v), r3(a), r3(b), r3(lwf), r3(lwb))
    return rwkv_post(yf.reshape(-1, D_WIDTH), yb.reshape(-1, D_WIDTH), bonus, g, ln_g, ln_b)


MOE_TILE = 1024
MOE_SUB = 288
ROW_ALIGN = 16
MOE_PACK = 2


def _route(logit):
    lane = lax.broadcasted_iota(jnp.int32, logit.shape, 1)
    first_at = lambda mask: jnp.min(jnp.where(mask, lane, jnp.int32(LANE)), axis=-1, keepdims=True)
    is_grp = lane < N_GROUPS
    gl = jnp.where(is_grp, logit, NEG)
    gmax = jnp.max(gl, axis=-1, keepdims=True)
    p_grp = 1.0 / jnp.sum(jnp.where(is_grp, jnp.exp(gl - gmax), 0.0), axis=-1, keepdims=True)
    i_grp = first_at(is_grp & (gl == gmax))
    lo = N_GROUPS + i_grp * EXPERTS_PER_GROUP
    in_grp = (lane >= lo) & (lane < lo + EXPERTS_PER_GROUP)
    el = jnp.where(in_grp, logit, NEG)
    l1 = jnp.max(el, axis=-1, keepdims=True)
    i1 = first_at(in_grp & (el == l1))
    rest = in_grp & (lane != i1)
    el2 = jnp.where(rest, logit, NEG)
    l2 = jnp.max(el2, axis=-1, keepdims=True)
    i2 = first_at(rest & (el2 == l2))
    e2 = jnp.exp(l2 - l1)
    w1 = p_grp / (1.0 + e2)
    w2 = p_grp * e2 / (1.0 + e2)
    return i_grp, jnp.where(lane == i1, w1, jnp.where(lane == i2, w2, 0.0))


def _moe_sort_kernel(x_ref, g_ref, wr_ref, br_ref, hn_ref, comb_ref, pos_ref, off_ref):
    x = x_ref[...]
    tm = x.shape[0]
    hn = (x * lax.rsqrt(jnp.mean(x * x, axis=-1, keepdims=True) + EPS) * g_ref[...]).astype(BF16)
    wr, br = wr_ref[...], br_ref[...]
    i_grp, _ = _route(jnp.dot(hn, wr, preferred_element_type=F32) + br)
    lane = lax.broadcasted_iota(jnp.int32, (tm, LANE), 1)
    onehot = (lane == i_grp).astype(F32)
    ri = lax.broadcasted_iota(jnp.int32, (tm, tm), 0)
    ci = lax.broadcasted_iota(jnp.int32, (tm, tm), 1)
    earlier = jnp.dot((ci < ri).astype(BF16), onehot.astype(BF16), preferred_element_type=F32)
    cnt = jnp.broadcast_to(jnp.sum(onehot, axis=0, keepdims=True), (8, LANE))
    lane8 = lax.broadcasted_iota(jnp.int32, (8, LANE), 1)
    start = jnp.zeros((8, LANE), F32)
    for sh in range(1, N_GROUPS):
        start += jnp.where(lane8 >= sh, pltpu.roll(cnt, sh, 1), 0.0)
    off_ref[...] = start[0:1].astype(jnp.int32)
    pos = jnp.sum(onehot * (start[0:1] + earlier), axis=-1, keepdims=True).astype(jnp.int32)
    pos_ref[...] = pos
    perm_t = (ci == pos).astype(BF16)
    hn_s = lax.dot_general(perm_t, hn, TN_DIMS, preferred_element_type=F32).astype(BF16)
    hn_ref[...] = hn_s
    _, comb = _route(jnp.dot(hn_s, wr, preferred_element_type=F32) + br)
    comb_ref[...] = comb


def moe_sort(x, g, w_grp, b_grp, w_exp, b_exp):
    t, d = x.shape
    tm = MOE_TILE
    nr = N_GROUPS + N_EXPERTS
    wr = jnp.pad(jnp.concatenate([w_grp, w_exp], axis=1), ((0, 0), (0, LANE - nr))).astype(BF16)
    br = jnp.pad(jnp.concatenate([b_grp, b_exp]), (0, LANE - nr)).reshape(1, LANE)
    row = lambda i: (i, 0)
    hn, comb, pos, off = pl.pallas_call(
        _moe_sort_kernel,
        out_shape=(jax.ShapeDtypeStruct((t, d), BF16), jax.ShapeDtypeStruct((t, LANE), F32),
                   jax.ShapeDtypeStruct((t, 1), jnp.int32), jax.ShapeDtypeStruct((t // tm, 1, LANE), jnp.int32)),
        grid=(t // tm,),
        in_specs=[pl.BlockSpec((tm, d), row), pl.BlockSpec((1, d), lambda i: (0, 0)),
                  pl.BlockSpec((d, LANE), lambda i: (0, 0)), pl.BlockSpec((1, LANE), lambda i: (0, 0))],
        out_specs=(pl.BlockSpec((tm, d), row), pl.BlockSpec((tm, LANE), row), pl.BlockSpec((tm, 1), row),
                   pl.BlockSpec((None, 1, LANE), lambda i: (i, 0, 0))),
        compiler_params=_params("parallel"),
        name="moe_sort",
    )(x, g.reshape(1, d), wr, br)
    bounds = jnp.concatenate([off[:, 0, :N_GROUPS], jnp.full((t // tm, 1), tm, jnp.int32)], axis=1)
    return hn, comb, pos, bounds.reshape(-1)


def _moe_group_kernel(bounds_ref, hn_ref, c_ref, wg_ref, wu_ref, wd_ref, y_ref):
    i, g, j = pl.program_id(0), pl.program_id(1), pl.program_id(2)

    @pl.when((g == 0) & (j == 0))
    def _():
        y_ref[...] = jnp.zeros_like(y_ref)

    expert_lane = N_GROUPS + g * EXPERTS_PER_GROUP + j

    def sub_tile(want, size):
        r0 = pl.multiple_of(jnp.minimum(want, lo + MOE_TILE - size), ROW_ALIGN)
        rows = pl.ds(r0, size)
        x = hn_ref[rows, :]
        hg = jnp.dot(x, wg_ref[...], preferred_element_type=F32)
        hu = jnp.dot(x, wu_ref[...], preferred_element_type=F32)
        lane = lax.broadcasted_iota(jnp.int32, (size, LANE), 1)
        c = jnp.sum(jnp.where(lane == expert_lane, c_ref[rows, :], 0.0), axis=-1, keepdims=True)
        row = lax.broadcasted_iota(jnp.int32, (size, 1), 0)
        c = jnp.where(row + r0 >= want, c, 0.0)
        hid = (hg * jax.nn.sigmoid(hg)) * hu * c
        y_ref[rows, :] += jnp.dot(hid.astype(BF16), wd_ref[...], preferred_element_type=F32)

    for half in range(MOE_PACK):
        base = (i * MOE_PACK + half) * (N_GROUPS + 1) + g
        lo = half * MOE_TILE
        start, end = lo + bounds_ref[base], lo + bounds_ref[base + 1]
        first = (start // ROW_ALIGN) * ROW_ALIGN
        n_full = (end - first) // MOE_SUB
        rest = end - first - n_full * MOE_SUB

        def full(k, carry, first=first):
            sub_tile(first + k * MOE_SUB, MOE_SUB)
            return carry

        lax.fori_loop(0, n_full, full, 0)
        tail = first + n_full * MOE_SUB

        @pl.when(rest > MOE_SUB // 2)
        def _():
            sub_tile(tail, MOE_SUB)

        @pl.when((rest > 0) & (rest <= MOE_SUB // 2))
        def _():
            sub_tile(tail, MOE_SUB // 2)


def moe_group_experts(hn, comb, bounds, w_gate, w_up, w_down):
    t, d = hn.shape
    tm = MOE_TILE * MOE_PACK
    ne, _, ff = w_gate.shape
    ex = lambda i, g, j, b: (g * EXPERTS_PER_GROUP + j, 0, 0)
    once = pl.Buffered(1) if MOE_PACK > 1 else None
    return pl.pallas_call(
        _moe_group_kernel,
        out_shape=jax.ShapeDtypeStruct((t, d), F32),
        grid_spec=pltpu.PrefetchScalarGridSpec(
            num_scalar_prefetch=1,
            grid=(t // tm, N_GROUPS, EXPERTS_PER_GROUP),
            in_specs=[pl.BlockSpec((tm, d), lambda i, g, j, b: (i, 0), pipeline_mode=once),
                      pl.BlockSpec((tm, LANE), lambda i, g, j, b: (i, 0)),
                      pl.BlockSpec((None, d, ff), ex), pl.BlockSpec((None, d, ff), ex),
                      pl.BlockSpec((None, ff, d), ex)],
            out_specs=pl.BlockSpec((tm, d), lambda i, g, j, b: (i, 0), pipeline_mode=once)),
        compiler_params=_params("parallel", "arbitrary", "arbitrary"),
        name="moe_group_experts",
    )(bounds, hn, comb, w_gate, w_up, w_down)


def _moe_unsort_kernel(y_ref, pos_ref, x_ref, o_ref):
    tm = y_ref.shape[0]
    perm_t = (lax.broadcasted_iota(jnp.int32, (tm, tm), 1) == pos_ref[...]).astype(BF16)
    y = y_ref[...]
    hi = y.astype(BF16)
    lo = (y - hi.astype(F32)).astype(BF16)
    o_ref[...] = (x_ref[...] + jnp.dot(perm_t, hi, preferred_element_type=F32)
                  + jnp.dot(perm_t, lo, preferred_element_type=F32))


def moe_unsort(y, pos, x):
    t, d = x.shape
    tm, tn = MOE_TILE, d // 2
    blk = pl.BlockSpec((tm, tn), lambda i, j: (i, j))
    return pl.pallas_call(
        _moe_unsort_kernel,
        out_shape=jax.ShapeDtypeStruct((t, d), F32),
        grid=(t // tm, d // tn),
        in_specs=[blk, pl.BlockSpec((tm, 1), lambda i, j: (i, 0)), blk],
        out_specs=blk,
        compiler_params=_params("parallel", "arbitrary"),
        name="moe_unsort",
    )(y, pos, x)


def _moe_unsort_norm_kernel(y_ref, pos_ref, x_ref, g_ref, o_ref):
    tm, ts = pos_ref.shape[0], y_ref.shape[0]
    perm_t = (lax.broadcasted_iota(jnp.int32, (tm, ts), 1) == pos_ref[...]).astype(BF16)
    y = y_ref[...]
    hi = y.astype(BF16)
    lo = (y - hi.astype(F32)).astype(BF16)
    x = (x_ref[...] + jnp.dot(perm_t, hi, preferred_element_type=F32)
         + jnp.dot(perm_t, lo, preferred_element_type=F32))
    o_ref[...] = x * lax.rsqrt(jnp.mean(x * x, axis=-1, keepdims=True) + EPS) * g_ref[...]


def moe_unsort_norm(y, pos, x, g, *, row0, rows, tm=512):
    d = x.shape[1]
    per = MOE_TILE // tm
    r0 = row0 // tm
    return pl.pallas_call(
        _moe_unsort_norm_kernel,
        out_shape=jax.ShapeDtypeStruct((rows, d), F32),
        grid=(rows // tm,),
        in_specs=[pl.BlockSpec((MOE_TILE, d), lambda i: ((i + r0) // per, 0)),
                  pl.BlockSpec((tm, 1), lambda i: (i + r0, 0)),
                  pl.BlockSpec((tm, d), lambda i: (i + r0, 0)),
                  pl.BlockSpec((1, d), lambda i: (0, 0))],
        out_specs=pl.BlockSpec((tm, d), lambda i: (i, 0)),
        compiler_params=_params("parallel"),
        name="moe_unsort_norm",
    )(y, pos, x, g.reshape(1, d))


def hier_moe(x, g, w_grp, b_grp, w_exp, b_exp, w_gate, w_up, w_down, *, final=None):
    hn, comb, pos, bounds = moe_sort(x, g, w_grp, b_grp, w_exp, b_exp)
    y = moe_group_experts(hn, comb, bounds, w_gate.astype(BF16), w_up.astype(BF16), w_down.astype(BF16))
    if final is None:
        return moe_unsort(y, pos, x)
    gain, counts = final
    starts = np.cumsum((0,) + tuple(counts))[:-1]
    return tuple(moe_unsort_norm(y, pos, x, gain, row0=int(r0), rows=int(n)) for r0, n in zip(starts, counts))


def even_layer(x, bsz, s, norm_g, rel_bias, w_in, w_out, w2_f, b_f, w2_b, b_b, onorm):
    n_pad = _round_up(EVEN_IN, LANE)
    w_in_p = jnp.pad(w_in, ((0, 0), (0, n_pad - EVEN_IN))).astype(BF16)
    proj = norm_linear(x, norm_g, w_in_p, tn_target=896).reshape(bsz, s, n_pad)
    tq = ATTN_TQ
    ya = attention(proj, proj, proj, heads=A_HEADS, dq=HEAD_DIM, dv=HEAD_DIM,
                   q_off=0, k_off=A_HEADS, v_off=2 * A_HEADS, scale=HEAD_DIM ** -0.5,
                   bias=dilated_bias_table(rel_bias, s, tq), tq=tq)
    q_col = 3 * A_WIDTH
    yb = gla_mixer(proj, w2_f, w2_b, b_f, b_b, onorm, q_col=q_col, k_col=q_col + B_KEYW,
                   v_col=q_col + 2 * B_KEYW, g_col=q_col + 2 * B_KEYW + B_WIDTH,
                   z_col=q_col + 2 * B_KEYW + 2 * B_WIDTH)
    t = bsz * s
    return out_proj(ya.reshape(t, A_WIDTH), yb.reshape(t, B_WIDTH), w_out.astype(BF16), x)


def _odd_columns(w_in, mu):
    c0 = C_IN
    cut = lambda u, lo, n: u[..., lo:lo + n]
    zpad = lambda u, n: jnp.pad(u, [(0, 0)] * (u.ndim - 1) + [(0, n)])
    off = np.cumsum((0,) + D_SPLITS)
    def rwkv_cols(u):
        parts = [cut(u, off[0], 3 * D_WIDTH), cut(u, off[6], D_G_RANK), cut(u, off[3], 2 * D_W_RANK),
                 cut(u, off[5], D_A_RANK)]
        u = jnp.concatenate(parts, axis=-1)
        return zpad(u, DC_PAD - u.shape[-1])
    w_kr = w_in[:, C_Q_RANK + C_KV_RANK:C_IN]
    w_all = jnp.concatenate([rwkv_cols(w_in[:, c0:]), w_in[:, :C_IN], _rot_half_cols(w_kr)], axis=1)
    n_pad = _round_up(w_all.shape[1], 9 * LANE)
    return zpad(w_all, n_pad - w_all.shape[1]).astype(BF16), rwkv_cols(mu)


def odd_layer(x, bsz, s, norm_g, w_in, w_out, q_norm, w_uq, kv_norm, w_ukv, mu, w0_f, w2_f, w0_b, w2_b,
              a0, a2, g2, k_k, k_a, r_k, ln_g, ln_b):
    t = bsz * s
    w_all, mu_cols = _odd_columns(w_in, mu)
    proj = norm_linear(x, norm_g, w_all, tn_target=1152)
    inv = 1.0 / (ROPE_THETA ** (jnp.arange(0, C_ROPE, 2, dtype=F32) / C_ROPE))
    ang = jnp.arange(s, dtype=F32)[:, None] * inv[None, :]
    cos = jnp.pad(jnp.concatenate([jnp.cos(ang)] * 2, axis=1), ((0, 0), (0, LANE - C_ROPE)), constant_values=1.0)
    sin = jnp.pad(jnp.concatenate([jnp.sin(ang)] * 2, axis=1), ((0, 0), (0, LANE - C_ROPE)))
    q, k, v = mla_up(proj, q_norm, w_uq, kv_norm, w_ukv, jnp.tile(cos, (bsz, 1)), jnp.tile(sin, (bsz, 1)),
                     col0=DC_PAD)
    r3 = lambda u: u.reshape(bsz, s, -1)
    yc = attention(r3(q), r3(k), r3(v), heads=C_HEADS, dq=C_QK, dv=C_V, q_off=0, k_off=0, v_off=0,
                   scale=(C_NOPE + C_ROPE) ** -0.5)
    yd = rwkv7_mixer(proj, bsz, s, mu_cols, w0_f, w2_f, w0_b, w2_b, a0, a2, g2, k_k, k_a, r_k, ln_g, ln_b)
    return out_proj(yc.reshape(t, C_WIDTH), yd, w_out.astype(BF16), x)


def kernel(x_prompt, x_sample, rel_bias, norm_mix, norm_ffn, norm_final, ev_w_in, ev_w_out, ev_gla_w2_f, ev_gla_b_f, ev_gla_w2_b, ev_gla_b_b, ev_gla_onorm, od_w_in, od_w_out, od_q_norm, od_w_uq, od_kv_norm, od_w_ukv, od_mu, od_w0_f, od_w2_f, od_w0_b, od_w2_b, od_a0, od_a2, od_g2, od_k_k, od_k_a, od_r_k, od_ln_g, od_ln_b, moe_w_grp, moe_b_grp, moe_w_exp, moe_b_exp, moe_w_gate, moe_w_up, moe_w_down):
    nb_p = x_prompt.shape[0]
    x = jnp.concatenate([x_prompt, x_sample], axis=0)
    bsz, s, d = x.shape
    x = x.reshape(bsz * s, d)
    for i in range(DEPTH):
        j = i // 2
        if i % 2 == 0:
            x = even_layer(x, bsz, s, norm_mix[i], rel_bias, ev_w_in[j], ev_w_out[j], ev_gla_w2_f[j],
                           ev_gla_b_f[j], ev_gla_w2_b[j], ev_gla_b_b[j], ev_gla_onorm[j])
        else:
            x = odd_layer(x, bsz, s, norm_mix[i], od_w_in[j], od_w_out[j], od_q_norm[j], od_w_uq[j],
                          od_kv_norm[j], od_w_ukv[j], od_mu[j], od_w0_f[j], od_w2_f[j], od_w0_b[j],
                          od_w2_b[j], od_a0[j], od_a2[j], od_g2[j], od_k_k[j], od_k_a[j], od_r_k[j],
                          od_ln_g[j], od_ln_b[j])
        final = (norm_final, (nb_p * s, (bsz - nb_p) * s)) if i == DEPTH - 1 else None
        x = hier_moe(x, norm_ffn[i], moe_w_grp[i], moe_b_grp[i], moe_w_exp[i], moe_b_exp[i],
                     moe_w_gate[i], moe_w_up[i], moe_w_down[i], final=final)
    y_p, y_s = x
    return (y_p.reshape(nb_p, s, d), y_s.reshape(bsz - nb_p, s, d))
```

```python
import functools

import jax, jax.numpy as jnp
from jax import lax
import numpy as np
from jax.experimental import pallas as pl
from jax.experimental.pallas import tpu as pltpu

F32, BF16 = jnp.float32, jnp.bfloat16

D_MODEL = 2048
DEPTH = 2
MIX_HALF = D_MODEL // 2
HEAD_DIM = 128
EPS = 1e-6
NEG = -1e30

A_HEADS = MIX_HALF // HEAD_DIM
A_WIDTH = A_HEADS * HEAD_DIM
A_PATTERNS = ((128, 1), (512, 4), (2048, 16))
N_BUCKETS = 32
MAX_DISTANCE = 1024

B_HEADS = 4
B_DV = MIX_HALF // B_HEADS
B_DK = B_DV // 2
B_WIDTH = B_HEADS * B_DV
B_KEYW = B_HEADS * B_DK
B_GATE_RANK = 16
B_GATE_TAU = 16.0
B_CHUNK = 64

C_HEADS = MIX_HALF // 128
C_Q_RANK = 512
C_KV_RANK = 256
C_NOPE = 128
C_ROPE = 64
C_V = 128
C_WIDTH = C_HEADS * C_V
C_QK = 256
ROPE_THETA = 10000.0

D_HEAD = 64
D_HEADS = MIX_HALF // D_HEAD
D_WIDTH = D_HEADS * D_HEAD
D_W_RANK = 64
D_A_RANK = 64
D_G_RANK = 128
D_LN_EPS = 64e-5
D_SPLITS = (D_WIDTH, D_WIDTH, D_WIDTH, D_W_RANK, D_W_RANK, D_A_RANK, D_G_RANK)
D_SHIFT = 3 * D_WIDTH + 2 * D_W_RANK + D_A_RANK + D_G_RANK

N_GROUPS = 4
EXPERTS_PER_GROUP = 4
N_EXPERTS = N_GROUPS * EXPERTS_PER_GROUP

EVEN_IN = 3 * A_WIDTH + 2 * B_KEYW + 2 * B_WIDTH + 2 * B_GATE_RANK
C_IN = C_Q_RANK + C_KV_RANK + C_ROPE

LANE = 128
VMEM_LIMIT = 52 * 1024 * 1024


def _params(*sem):
    return pltpu.CompilerParams(dimension_semantics=sem, vmem_limit_bytes=VMEM_LIMIT)


def _round_up(n, m):
    return -(-n // m) * m


NT_DIMS = (((1,), (1,)), ((), ()))
TN_DIMS = (((0,), (0,)), ((), ()))


def _split3(x):
    hi = x.astype(BF16)
    r1 = x - hi.astype(F32)
    mid = r1.astype(BF16)
    lo = (r1 - mid.astype(F32)).astype(BF16)
    return hi, mid, lo


def _pick_tile(n, target):
    best = LANE
    for t in range(LANE, target + 1, LANE):
        if n % t == 0:
            best = t
    return best


def _norm_linear_kernel(x_ref, g_ref, w_ref, o_ref, xn_ref):
    @pl.when(pl.program_id(1) == 0)
    def _():
        x = x_ref[...]
        y = x * lax.rsqrt(jnp.mean(x * x, axis=-1, keepdims=True) + EPS) * g_ref[...]
        xn_ref[...] = y.astype(BF16)

    o_ref[...] = jnp.dot(xn_ref[...], w_ref[...], preferred_element_type=F32)


def norm_linear(x, g, w, *, tm=1024, tn_target=1024):
    t, k = x.shape
    n = w.shape[1]
    tn = _pick_tile(n, tn_target)
    return pl.pallas_call(
        _norm_linear_kernel,
        out_shape=jax.ShapeDtypeStruct((t, n), F32),
        grid=(t // tm, n // tn),
        in_specs=[pl.BlockSpec((tm, k), lambda i, j: (i, 0)),
                  pl.BlockSpec((1, k), lambda i, j: (0, 0)),
                  pl.BlockSpec((k, tn), lambda i, j: (0, j))],
        out_specs=pl.BlockSpec((tm, tn), lambda i, j: (i, j)),
        scratch_shapes=[pltpu.VMEM((tm, k), BF16)],
        compiler_params=_params("parallel", "arbitrary"),
        name="norm_linear",
    )(x, g.reshape(1, k), w)


def _out_proj_kernel(a_ref, b_ref, wa_ref, wb_ref, x_ref, o_ref):
    acc = jnp.dot(a_ref[...], wa_ref[...], preferred_element_type=F32)
    acc += jnp.dot(b_ref[...], wb_ref[...], preferred_element_type=F32)
    o_ref[...] = x_ref[...] + acc


def out_proj(a, b, w, x, *, tm=1024, tn=1024):
    t, ka = a.shape
    kb = b.shape[1]
    n = w.shape[1]
    return pl.pallas_call(
        _out_proj_kernel,
        out_shape=jax.ShapeDtypeStruct((t, n), F32),
        grid=(t // tm, n // tn),
        in_specs=[pl.BlockSpec((tm, ka), lambda i, j: (i, 0)),
                  pl.BlockSpec((tm, kb), lambda i, j: (i, 0)),
                  pl.BlockSpec((ka, tn), lambda i, j: (0, j)),
                  pl.BlockSpec((kb, tn), lambda i, j: (0, j)),
                  pl.BlockSpec((tm, tn), lambda i, j: (i, j))],
        out_specs=pl.BlockSpec((tm, tn), lambda i, j: (i, j)),
        compiler_params=_params("parallel", "arbitrary"),
        name="out_proj",
    )(a, b, w[:ka], w[ka:], x)


LOG2E = float(np.log2(np.e))
ATTN_KB = 256
ATTN_TQ = 1024


def _attn_kernel(*refs, scale, has_bias):
    if has_bias:
        q_ref, k_ref, v_ref, bias_ref, o_ref, kb_ref, vb_ref = refs
    else:
        q_ref, k_ref, v_ref, o_ref, kb_ref, vb_ref = refs
    s_len = k_ref.shape[0]

    @pl.when(pl.program_id(2) == 0)
    def _():
        kb_ref[...] = k_ref[...].astype(BF16)
        vb_ref[...] = v_ref[...].astype(BF16)

    q = (q_ref[...] * (scale * LOG2E)).astype(BF16)
    blocks = [slice(j * ATTN_KB, (j + 1) * ATTN_KB) for j in range(s_len // ATTN_KB)]
    scores = []
    m = None
    for blk in blocks:
        sj = lax.dot_general(q, kb_ref[blk, :], NT_DIMS, preferred_element_type=F32)
        if has_bias:
            sj = sj + bias_ref[:, blk]
        mj = jnp.max(sj, axis=-1, keepdims=True)
        m = mj if m is None else jnp.maximum(m, mj)
        scores.append(sj)
    o = den = None
    for blk, sj in zip(blocks, scores):
        p = jnp.exp2(sj - m)
        dj = jnp.sum(p, axis=-1, keepdims=True)
        oj = jnp.dot(p.astype(BF16), vb_ref[blk, :], preferred_element_type=F32)
        o, den = (oj, dj) if o is None else (o + oj, den + dj)
    o_ref[...] = (o / den).astype(o_ref.dtype)


def attention(q, k, v, *, heads, dq, dv, q_off, k_off, v_off, scale, bias=None, tq=ATTN_TQ):
    b, s, _ = q.shape
    nq = s // tq
    assert s % tq == 0 and s % ATTN_KB == 0, (s, tq)
    in_specs = [pl.BlockSpec((None, tq, dq), lambda bi, h, qi: (bi, qi, q_off + h)),
                pl.BlockSpec((None, s, dq), lambda bi, h, qi: (bi, 0, k_off + h)),
                pl.BlockSpec((None, s, dv), lambda bi, h, qi: (bi, 0, v_off + h))]
    args = [q, k, v]
    if bias is not None:
        in_specs.append(pl.BlockSpec((None, None, tq, s), lambda bi, h, qi: (h, qi, 0, 0)))
        args.append(bias)
    return pl.pallas_call(
        functools.partial(_attn_kernel, scale=scale, has_bias=bias is not None),
        out_shape=jax.ShapeDtypeStruct((b, s, heads * dv), BF16),
        grid=(b, heads, nq),
        in_specs=in_specs,
        out_specs=pl.BlockSpec((None, tq, dv), lambda bi, h, qi: (bi, qi, h)),
        scratch_shapes=[pltpu.VMEM((s, dq), BF16), pltpu.VMEM((s, dv), BF16)],
        compiler_params=_params("parallel", "parallel", "arbitrary"),
        name="attention_bias" if bias is not None else "attention",
    )(*args)


def _t5_bucket(rel):
    half = N_BUCKETS // 2
    exact = half // 2
    n = np.abs(rel)
    large = exact + (np.log(np.maximum(n, 1) / exact) / np.log(MAX_DISTANCE / exact) * (half - exact)).astype(np.int64)
    large = np.minimum(large, half - 1)
    return ((rel > 0) * half + np.where(n < exact, n, large)).astype(np.int32)


def dilated_bias_table(rel_bias, s, tq):
    heads = rel_bias.shape[1]
    d = np.arange(-(s - 1), s)
    count = np.zeros(d.shape, np.float32)
    for window, dil in A_PATTERNS:
        count += ((d % dil == 0) & (np.abs(d) <= (window // (2 * dil)) * dil)).astype(np.float32)
    logc = np.where(count > 0, np.log(np.maximum(count, 1.0)), NEG).astype(np.float32)
    onehot = (_t5_bucket(d)[:, None] == np.arange(N_BUCKETS)[None, :]).astype(np.float32)
    line = jnp.transpose(jnp.dot(onehot, rel_bias.astype(F32), precision=lax.Precision.HIGHEST)) + logc[None]
    line = line * LOG2E
    width = 2 * s
    line = jnp.pad(line, ((0, 0), (0, width - line.shape[1])))[:, None, :]
    nq = s // tq
    return pl.pallas_call(
        functools.partial(_skew_kernel, tq=tq, nq=nq),
        out_shape=jax.ShapeDtypeStruct((heads, nq, tq, s), F32),
        grid=(heads, nq),
        in_specs=[pl.BlockSpec((None, 1, width), lambda h, qi: (h, 0, 0))],
        out_specs=pl.BlockSpec((None, None, tq, s), lambda h, qi: (h, qi, 0, 0)),
        compiler_params=_params("parallel", "arbitrary"),
        name="bias_skew",
    )(line)


def _skew_kernel(line_ref, o_ref, *, tq, nq):
    width = line_ref.shape[1]
    first = (nq - 1 - pl.program_id(1)) * tq
    x = jnp.broadcast_to(line_ref[...], (tq, width))
    x = pltpu.roll(x, width - (tq - 1) - first, 1, stride=1, stride_axis=0)
    o_ref[...] = x[:, :o_ref.shape[1]]


GLA_UNROLL = 16


def _gla_kernel(q_ref, k_ref, v_ref, g_ref, z_ref, w2f_ref, w2b_ref, bf_ref, bb_ref, on_ref, o_ref,
                la_ref, acc_ref, qcat_ref, upd_ref, dec_ref, scat_ref, st_ref):
    s_len = q_ref.shape[0]
    c = B_CHUNK
    nchunk = s_len // c
    z = z_ref[...].astype(BF16)
    gate = lambda w2_ref, b_ref: jax.nn.log_sigmoid(
        jnp.dot(z, w2_ref[...], preferred_element_type=F32) + b_ref[...]) * (1.0 / B_GATE_TAU)
    la_ref[0] = gate(w2f_ref, bf_ref)
    la_ref[1] = gate(w2b_ref, bb_ref)

    ri = lax.broadcasted_iota(jnp.int32, (c, c), 0)
    ci = lax.broadcasted_iota(jnp.int32, (c, c), 1)
    keep = (ri >= ci, ri <= ci)
    tri3 = tuple(jnp.concatenate([kp.astype(BF16)] * 3, axis=1) for kp in keep)

    def chunk_rows(n):
        return pl.ds(pl.multiple_of(n * c, c), c)

    def pass1(i, carry):
        units = [(i * GLA_UNROLL + u, d) for u in range(GLA_UNROLL) for d in range(2)]
        st = []
        for n, d in units:
            rows = chunk_rows(n)
            gcum = jnp.dot(tri3[d], jnp.concatenate(_split3(la_ref[d, rows, :]), axis=0),
                           preferred_element_type=F32)
            st.append(dict(rows=rows, gcum=gcum))
        for (n, d), c in zip(units, st):
            gcum = c["gcum"]
            gend = gcum[0:1] if d == 1 else gcum[B_CHUNK - 1:B_CHUNK]
            kc = k_ref[c["rows"], :]
            c["q_in"] = (q_ref[c["rows"], :] * (B_DK ** -0.5) * jnp.exp(gcum)).astype(BF16)
            c["k_out"] = (kc * jnp.exp(gend - gcum)).astype(BF16)
            c["att"] = lax.dot_general(c["q_in"], (kc * jnp.exp(-gcum)).astype(BF16), NT_DIMS,
                                       preferred_element_type=F32)
            dec_ref[d, n] = jnp.broadcast_to(jnp.exp(gend), (8, B_DK))
        for (n, d), c in zip(units, st):
            vc = v_ref[c["rows"], :].astype(BF16)
            c["o"] = jnp.dot(jnp.where(keep[d], c["att"], 0.0).astype(BF16), vc, preferred_element_type=F32)
            upd_ref[d, n] = lax.dot_general(vc, c["k_out"], TN_DIMS, preferred_element_type=F32)
            qcat_ref[c["rows"], d * B_DK:(d + 1) * B_DK] = c["q_in"]
        for u in range(GLA_UNROLL):
            acc_ref[st[2 * u]["rows"], :] = st[2 * u]["o"] + st[2 * u + 1]["o"]
        return carry

    lax.fori_loop(0, nchunk // GLA_UNROLL, pass1, 0)

    st_ref[...] = jnp.zeros_like(st_ref)

    def pass2(n, carry):
        for d, m in ((0, n), (1, nchunk - 1 - n)):
            state = st_ref[d]
            scat_ref[m, :, d * B_DK:(d + 1) * B_DK] = state.astype(BF16)
            st_ref[d] = state * dec_ref[d, m][0:1] + upd_ref[d, m]
        return carry

    lax.fori_loop(0, nchunk, pass2, 0)

    def pass3(i, carry):
        rows = [chunk_rows(i * GLA_UNROLL + u) for u in range(GLA_UNROLL)]
        outs = [acc_ref[r, :] + lax.dot_general(qcat_ref[r, :], scat_ref[i * GLA_UNROLL + u], NT_DIMS,
                                                preferred_element_type=F32) for u, r in enumerate(rows)]
        for r, o in zip(rows, outs):
            o = o * lax.rsqrt(jnp.mean(o * o, axis=-1, keepdims=True) + EPS) * on_ref[...]
            g = g_ref[r, :]
            o_ref[r, :] = (o * (g * jax.nn.sigmoid(g))).astype(o_ref.dtype)
        return carry

    lax.fori_loop(0, nchunk // GLA_UNROLL, pass3, 0)


def gla_mixer(proj, w2f, w2b, b_f, b_b, onorm, *, q_col, k_col, v_col, g_col, z_col):
    b, s, _ = proj.shape
    assert s % (B_CHUNK * GLA_UNROLL) == 0, s
    hm = lambda blk: (lambda bi, h: (bi, 0, blk + h))
    w2f_p = jnp.zeros((LANE, B_KEYW), F32).at[:B_GATE_RANK].set(w2f).astype(BF16)
    w2b_p = jnp.zeros((LANE, B_KEYW), F32).at[B_GATE_RANK:2 * B_GATE_RANK].set(w2b).astype(BF16)
    return pl.pallas_call(
        _gla_kernel,
        out_shape=jax.ShapeDtypeStruct((b, s, B_WIDTH), BF16),
        grid=(b, B_HEADS),
        in_specs=[pl.BlockSpec((None, s, B_DK), hm(q_col // B_DK)),
                  pl.BlockSpec((None, s, B_DK), hm(k_col // B_DK)),
                  pl.BlockSpec((None, s, B_DV), hm(v_col // B_DV)),
                  pl.BlockSpec((None, s, B_DV), hm(g_col // B_DV)),
                  pl.BlockSpec((None, s, LANE), lambda bi, h: (bi, 0, z_col // LANE)),
                  pl.BlockSpec((LANE, B_DK), lambda bi, h: (0, h)),
                  pl.BlockSpec((LANE, B_DK), lambda bi, h: (0, h)),
                  pl.BlockSpec((1, B_DK), lambda bi, h: (0, h)),
                  pl.BlockSpec((1, B_DK), lambda bi, h: (0, h)),
                  pl.BlockSpec((1, B_DV), lambda bi, h: (0, 0))],
        out_specs=pl.BlockSpec((None, s, B_DV), lambda bi, h: (bi, 0, h)),
        scratch_shapes=[pltpu.VMEM((2, s, B_DK), F32),
                        pltpu.VMEM((s, B_DV), F32),
                        pltpu.VMEM((s, 2 * B_DK), BF16),
                        pltpu.VMEM((2, s // B_CHUNK, B_DV, B_DK), F32),
                        pltpu.VMEM((2, s // B_CHUNK, 8, B_DK), F32),
                        pltpu.VMEM((s // B_CHUNK, B_DV, 2 * B_DK), BF16),
                        pltpu.VMEM((2, B_DV, B_DK), F32)],
        compiler_params=_params("parallel", "arbitrary"),
        name="gla_mixer",
    )(proj, proj, proj, proj, proj, w2f_p, w2b_p, b_f.reshape(1, -1), b_b.reshape(1, -1), onorm.reshape(1, -1))


def _mla_up_kernel(cq_ref, ckv_ref, kr_ref, qn_ref, kvn_ref, wq_ref, wqr_ref, wkv_ref, cos_ref, sin_ref,
                   q_ref, k_ref, v_ref):
    def rms(x, g):
        return (x * lax.rsqrt(jnp.mean(x * x, axis=-1, keepdims=True) + EPS) * g).astype(BF16)

    cq = rms(cq_ref[...], qn_ref[...])
    ckv = rms(ckv_ref[...], kvn_ref[...])
    cos, sin = cos_ref[...], sin_ref[...]
    kr = kr_ref[...]
    k_rope = kr * cos + pltpu.roll(kr, LANE - C_ROPE, 1) * sin
    lane = lax.broadcasted_iota(jnp.int32, k_rope.shape, 1)
    k_rope = jnp.where(lane < C_ROPE, k_rope, 0.0)
    for h in range(C_HEADS):
        q = jnp.dot(cq, wq_ref[:, h * C_QK:(h + 1) * C_QK], preferred_element_type=F32)
        qp = jnp.dot(cq, wqr_ref[:, h * LANE:(h + 1) * LANE], preferred_element_type=F32)
        q_ref[:, h * C_QK:h * C_QK + C_NOPE] = q[:, :C_NOPE]
        q_ref[:, h * C_QK + C_NOPE:(h + 1) * C_QK] = q[:, C_NOPE:] * cos + qp * sin
        kv = jnp.dot(ckv, wkv_ref[:, h * 2 * LANE:(h + 1) * 2 * LANE], preferred_element_type=F32)
        k_ref[:, h * C_QK:h * C_QK + C_NOPE] = kv[:, :C_NOPE]
        k_ref[:, h * C_QK + C_NOPE:(h + 1) * C_QK] = k_rope
        v_ref[:, h * C_V:(h + 1) * C_V] = kv[:, C_NOPE:]


def _rot_half_cols(w):
    half = w.shape[-1] // 2
    return jnp.concatenate([-w[..., half:], w[..., :half]], axis=-1)


def mla_up(proj, q_norm, w_uq, kv_norm, w_ukv, cos, sin, *, col0, tm=512):
    t = proj.shape[0]
    wq = w_uq.reshape(C_Q_RANK, C_HEADS, C_NOPE + C_ROPE)
    wq_main = jnp.pad(wq, ((0, 0), (0, 0), (0, C_QK - C_NOPE - C_ROPE))).reshape(C_Q_RANK, C_HEADS * C_QK)
    wq_rot = jnp.pad(_rot_half_cols(wq[..., C_NOPE:]), ((0, 0), (0, 0), (0, LANE - C_ROPE)))
    wq_rot = wq_rot.reshape(C_Q_RANK, C_HEADS * LANE)
    row = lambda i: (i, 0)
    full = lambda arr: pl.BlockSpec(arr.shape, lambda i: (0, 0))
    g_q, g_kv = q_norm.reshape(1, -1), kv_norm.reshape(1, -1)
    wq_main, wq_rot, wkv = wq_main.astype(BF16), wq_rot.astype(BF16), w_ukv.astype(BF16)
    return pl.pallas_call(
        _mla_up_kernel,
        out_shape=(jax.ShapeDtypeStruct((t, C_HEADS * C_QK), F32),
                   jax.ShapeDtypeStruct((t, C_HEADS * C_QK), F32),
                   jax.ShapeDtypeStruct((t, C_WIDTH), F32)),
        grid=(t // tm,),
        in_specs=[pl.BlockSpec((tm, C_Q_RANK), lambda i: (i, col0 // C_Q_RANK)),
                  pl.BlockSpec((tm, C_KV_RANK), lambda i: (i, (col0 + C_Q_RANK) // C_KV_RANK)),
                  pl.BlockSpec((tm, LANE), lambda i: (i, (col0 + C_Q_RANK + C_KV_RANK) // LANE)),
                  full(g_q), full(g_kv), full(wq_main), full(wq_rot), full(wkv),
                  pl.BlockSpec((tm, LANE), row), pl.BlockSpec((tm, LANE), row)],
        out_specs=(pl.BlockSpec((tm, C_HEADS * C_QK), row),
                   pl.BlockSpec((tm, C_HEADS * C_QK), row),
                   pl.BlockSpec((tm, C_WIDTH), row)),
        compiler_params=_params("parallel"),
        name="mla_up",
    )(proj, proj, proj, g_q, g_kv, wq_main, wq_rot, wkv, cos, sin)


RG = 4
RGW = RG * D_HEAD
RCH = 64
DC_R, DC_K, DC_V = 0, D_WIDTH, 2 * D_WIDTH
DC_ZG = 3 * D_WIDTH
DC_ZW = DC_ZG + D_G_RANK
DC_ZA = DC_ZW + 2 * D_W_RANK
DC_PAD = 7 * 512


def _head_sums(x, bo):
    return jnp.concatenate(
        [jnp.dot(x[:, RGW * g:RGW * (g + 1)], bo, preferred_element_type=F32, precision=lax.Precision.HIGHEST)
         for g in range(x.shape[1] // RGW)], axis=1)


def _block_ones():
    i = np.arange(RGW)
    return jnp.asarray((i[:, None] // D_HEAD) == (i[None, :] // D_HEAD), F32)


def _rwkv_prep_kernel(x_ref, xp_ref, xn_ref, mu_ref, w2f_ref, w2b_ref, a2_ref, g2_ref, w0f_ref, w0b_ref,
                      a0_ref, kk_ref, ka_ref, rk_ref, bo_ref,
                      r_ref, k_ref, v_ref, a_ref, b_ref, lwf_ref, lwb_ref, g_ref, bonus_ref, *, tiles_per_seq):
    i = pl.program_id(0) % tiles_per_seq
    x = x_ref[...]
    tm = x.shape[0]
    row = lax.broadcasted_iota(jnp.int32, x.shape, 0)
    prev_row = jnp.where(i == 0, 0.0, xp_ref[7:8, :])
    next_row = jnp.where(i == tiles_per_seq - 1, 0.0, xn_ref[0:1, :])
    prev = jnp.where(row == 0, prev_row, pltpu.roll(x, 1, 0))
    nxt = jnp.where(row == tm - 1, next_row, pltpu.roll(x, tm - 1, 0))
    x = x + mu_ref[...] * (0.5 * (prev + nxt) - x)
    r, k, v = x[:, DC_R:DC_R + D_WIDTH], x[:, DC_K:DC_K + D_WIDTH], x[:, DC_V:DC_V + D_WIDTH]
    zg = x[:, DC_ZG:DC_ZG + LANE]
    zw = x[:, DC_ZW:DC_ZW + LANE]
    za = x[:, DC_ZA:DC_ZA + LANE]
    tz = jnp.tanh(zw).astype(BF16)
    log_decay = lambda w0_ref, w2_ref: -np.exp(-0.5).astype(np.float32) * jax.nn.sigmoid(
        w0_ref[...] + jnp.dot(tz, w2_ref[...], preferred_element_type=F32))
    lwf_ref[...] = log_decay(w0f_ref, w2f_ref)
    lwb_ref[...] = log_decay(w0b_ref, w2b_ref)
    ag = jax.nn.sigmoid(a0_ref[...] + jnp.dot(za.astype(BF16), a2_ref[...], preferred_element_type=F32))
    g_ref[...] = jnp.dot(jax.nn.sigmoid(zg).astype(BF16), g2_ref[...], preferred_element_type=F32)
    bo = bo_ref[...]
    kk = k * kk_ref[...]
    kk = kk / jnp.maximum(jnp.sqrt(_head_sums(kk * kk, bo)), 1e-12)
    k = k * (1.0 + (ag - 1.0) * ka_ref[...])
    r_ref[...] = r
    k_ref[...] = k
    v_ref[...] = v
    a_ref[...] = -kk
    b_ref[...] = kk * ag
    bonus_ref[...] = _head_sums(r * k * rk_ref[...], bo) * v


def rwkv_prep(proj, mu, w0_f, w2_f, w0_b, w2_b, a0, a2, g2, k_k, k_a, r_k, *, seq, tm=256):
    t = proj.shape[0]
    tiles_per_seq = seq // tm
    assert seq % tm == 0 and t % seq == 0, (t, seq)
    hb = tm // 8
    nblk8 = t // 8
    pad_rows = lambda w, lo: jnp.zeros((LANE, D_WIDTH), F32).at[lo:lo + w.shape[0]].set(w).astype(BF16)
    vec = lambda u: u.reshape(1, -1)
    consts = [vec(mu), pad_rows(w2_f, 0), pad_rows(w2_b, D_W_RANK), pad_rows(a2, 0), g2.astype(BF16),
              vec(w0_f), vec(w0_b), vec(a0), vec(k_k), vec(k_a), vec(r_k), _block_ones()]
    full = lambda arr: pl.BlockSpec(arr.shape, lambda i: (0, 0))
    out_spec = pl.BlockSpec((tm, D_WIDTH), lambda i: (i, 0))
    return pl.pallas_call(
        functools.partial(_rwkv_prep_kernel, tiles_per_seq=tiles_per_seq),
        out_shape=tuple(jax.ShapeDtypeStruct((t, D_WIDTH), F32) for _ in range(9)),
        grid=(t // tm,),
        in_specs=[pl.BlockSpec((tm, DC_PAD), lambda i: (i, 0)),
                  pl.BlockSpec((8, DC_PAD), lambda i: (jnp.maximum(i * hb - 1, 0), 0)),
                  pl.BlockSpec((8, DC_PAD), lambda i: (jnp.minimum((i + 1) * hb, nblk8 - 1), 0))]
                 + [full(c) for c in consts],
        out_specs=tuple(out_spec for _ in range(9)),
        compiler_params=_params("parallel"),
        name="rwkv_prep",
    )(proj, proj, proj, *consts)


def _rwkv_chunk_kernel(*refs, ngroups):
    ins, (yf_ref, yb_ref, mt_ref) = refs[:12], refs[12:]

    @pl.when(pl.program_id(1) == 0)
    def _():
        mt_ref[...] = jnp.zeros_like(mt_ref)

    row = lax.broadcasted_iota(jnp.int32, (RCH, RGW), 0)
    col = lax.broadcasted_iota(jnp.int32, (RCH, RGW), 1) & (RCH - 1)
    bdmask = (lax.broadcasted_iota(jnp.int32, (RGW, RGW), 0) // D_HEAD
              == lax.broadcasted_iota(jnp.int32, (RGW, RGW), 1) // D_HEAD)
    tr = lax.broadcasted_iota(jnp.int32, (RCH, RCH), 0)
    tc = lax.broadcasted_iota(jnp.int32, (RCH, RCH), 1)
    zero = jnp.zeros((), F32)

    def bd(z):
        zb = z.astype(BF16)
        return jnp.where(bdmask, jnp.concatenate([zb] * RG, axis=0), jnp.zeros((), BF16))

    def mm(x, y, dims=None):
        x = x.astype(BF16)
        if dims is None:
            return jnp.dot(x, y, preferred_element_type=F32)
        return lax.dot_general(x, y, dims, preferred_element_type=F32)

    chains = [(d, g) for d in range(2) for g in range(ngroups)]
    st = []
    for d, g in chains:
        backward = d == 1
        r_ref, k_ref, v_ref, a_ref, b_ref, lw_ref = ins[6 * d:6 * d + 6]
        tri = ((tc >= tr) if backward else (tc <= tr)).astype(BF16)
        sl = slice(RGW * g, RGW * (g + 1))
        r, k, v, a, b, lw = (ref[:, sl] for ref in (r_ref, k_ref, v_ref, a_ref, b_ref, lw_ref))
        lam = jnp.dot(jnp.concatenate([tri] * 3, axis=1), jnp.concatenate(_split3(lw), axis=0),
                      preferred_element_type=F32)
        lamc = lam[0:1] if backward else lam[RCH - 1:RCH]
        e_inv = jnp.exp(-lam)
        e_out = jnp.exp(lamc - lam)
        ar = jnp.concatenate([a * jnp.exp(lam - lw), r * jnp.exp(lam)], axis=0).astype(BF16)
        bk = jnp.concatenate([b * e_out, k * e_out], axis=0).astype(BF16)
        st.append(dict(ar=ar, bk=bk, v=v, lamc=lamc, sl=sl,
                       gb=mm(ar, bd(b * e_inv), NT_DIMS), gk=mm(ar, bd(k * e_inv), NT_DIMS)))
    for (d, g), c in zip(chains, st):
        strict = (col > row) if d == 1 else (col < row)
        incl = (col >= row) if d == 1 else (col <= row)
        c["lp"] = jnp.where(strict, c["gb"][:RCH], zero)
        lak = jnp.where(strict, c["gk"][:RCH], zero)
        c["grb"] = jnp.where(incl, c["gb"][RCH:], zero).astype(BF16)
        c["grk"] = jnp.where(incl, c["gk"][RCH:], zero).astype(BF16)
        c["mt"] = mt_ref[d, g]
        amrm = mm(c["ar"], c["mt"].astype(BF16), NT_DIMS)
        c["bdv"] = bd(c["v"])
        c["u"] = amrm[:RCH] + mm(lak, c["bdv"])
        c["rm"] = amrm[RCH:]
    for rnd in range(6):
        for c in st:
            lpb = c["lp"].astype(BF16)
            c["u"] = c["u"] + mm(lpb, bd(c["u"]))
            if rnd < 5:
                c["lp"] = mm(lpb, bd(c["lp"]))
    for (d, g), c in zip(chains, st):
        y_ref = yb_ref if d == 1 else yf_ref
        y_ref[:, c["sl"]] = c["rm"] + mm(c["grb"], bd(c["u"])) + mm(c["grk"], c["bdv"])
        uv = jnp.concatenate([c["u"], c["v"]], axis=0).astype(BF16)
        upd = lax.dot_general(uv, c["bk"], TN_DIMS, preferred_element_type=F32)
        mt_ref[d, g] = c["mt"] * jnp.exp(c["lamc"]) + jnp.where(bdmask, upd, zero)


def rwkv_chunked(r, k, v, a, b, lwf, lwb):
    bsz, s, wd = r.shape
    nc = s // RCH
    assert s % RCH == 0 and wd % RGW == 0, (s, wd)
    fspec = pl.BlockSpec((None, RCH, wd), lambda bi, n: (bi, n, 0))
    bspec = pl.BlockSpec((None, RCH, wd), lambda bi, n: (bi, nc - 1 - n, 0))
    return pl.pallas_call(
        functools.partial(_rwkv_chunk_kernel, ngroups=wd // RGW),
        out_shape=(jax.ShapeDtypeStruct((bsz, s, wd), F32), jax.ShapeDtypeStruct((bsz, s, wd), F32)),
        grid=(bsz, nc),
        in_specs=[fspec] * 6 + [bspec] * 6,
        out_specs=(fspec, bspec),
        scratch_shapes=[pltpu.VMEM((2, wd // RGW, RGW, RGW), F32)],
        compiler_params=_params("parallel", "arbitrary"),
        name="rwkv_chunked",
    )(r, k, v, a, b, lwf, r, k, v, a, b, lwb)


def _rwkv_post_kernel(yf_ref, yb_ref, bonus_ref, g_ref, lng_ref, lnb_ref, bo_ref, o_ref):
    bo = bo_ref[...]
    y = yf_ref[...] + yb_ref[...]
    yc = y - _head_sums(y, bo) * (1.0 / D_HEAD)
    var = _head_sums(yc * yc, bo) * (1.0 / D_HEAD)
    y = yc * lax.rsqrt(var + D_LN_EPS) * lng_ref[...] + lnb_ref[...]
    o_ref[...] = ((y + bonus_ref[...]) * g_ref[...]).astype(o_ref.dtype)


def rwkv_post(yf, yb, bonus, g, ln_g, ln_b, *, tm=512):
    t, wd = yf.shape
    row = pl.BlockSpec((tm, wd), lambda i: (i, 0))
    vec = pl.BlockSpec((1, wd), lambda i: (0, 0))
    bo = _block_ones()
    return pl.pallas_call(
        _rwkv_post_kernel,
        out_shape=jax.ShapeDtypeStruct((t, wd), BF16),
        grid=(t // tm,),
        in_specs=[row, row, row, row, vec, vec, pl.BlockSpec(bo.shape, lambda i: (0, 0))],
        out_specs=row,
        compiler_params=_params("parallel"),
        name="rwkv_post",
    )(yf, yb, bonus, g, ln_g.reshape(1, wd), ln_b.reshape(1, wd), bo)


def rwkv7_mixer(proj, bsz, s, mu, w0_f, w2_f, w0_b, w2_b, a0, a2, g2, k_k, k_a, r_k, ln_g, ln_b):
    r, k, v, a, b, lwf, lwb, g, bonus = rwkv_prep(proj, mu, w0_f, w2_f, w0_b, w2_b, a0, a2, g2, k_k, k_a,
                                                  r_k.reshape(-1), seq=s)
    r3 = lambda u: u.reshape(bsz, s, D_WIDTH)
    yf, yb = rwkv_chunked(r3(r), r3(k), r3(v), r3(a), r3(b), r3(lwf), r3(lwb))
    return rwkv_post(yf.reshape(-1, D_WIDTH), yb.reshape(-1, D_WIDTH), bonus, g, ln_g, ln_b)


MOE_TILE = 1024
MOE_SUB = 288
ROW_ALIGN = 16
MOE_PACK = 2


def _route(logit):
    lane = lax.broadcasted_iota(jnp.int32, logit.shape, 1)
    first_at = lambda mask: jnp.min(jnp.where(mask, lane, jnp.int32(LANE)), axis=-1, keepdims=True)
    is_grp = lane < N_GROUPS
    gl = jnp.where(is_grp, logit, NEG)
    gmax = jnp.max(gl, axis=-1, keepdims=True)
    p_grp = 1.0 / jnp.sum(jnp.where(is_grp, jnp.exp(gl - gmax), 0.0), axis=-1, keepdims=True)
    i_grp = first_at(is_grp & (gl == gmax))
    lo = N_GROUPS + i_grp * EXPERTS_PER_GROUP
    in_grp = (lane >= lo) & (lane < lo + EXPERTS_PER_GROUP)
    el = jnp.where(in_grp, logit, NEG)
    l1 = jnp.max(el, axis=-1, keepdims=True)
    i1 = first_at(in_grp & (el == l1))
    rest = in_grp & (lane != i1)
    el2 = jnp.where(rest, logit, NEG)
    l2 = jnp.max(el2, axis=-1, keepdims=True)
    i2 = first_at(rest & (el2 == l2))
    e2 = jnp.exp(l2 - l1)
    w1 = p_grp / (1.0 + e2)
    w2 = p_grp * e2 / (1.0 + e2)
    return i_grp, jnp.where(lane == i1, w1, jnp.where(lane == i2, w2, 0.0))


def _moe_sort_kernel(x_ref, g_ref, wr_ref, br_ref, hn_ref, comb_ref, pos_ref, off_ref):
    x = x_ref[...]
    tm = x.shape[0]
    hn = (x * lax.rsqrt(jnp.mean(x * x, axis=-1, keepdims=True) + EPS) * g_ref[...]).astype(BF16)
    wr, br = wr_ref[...], br_ref[...]
    i_grp, _ = _route(jnp.dot(hn, wr, preferred_element_type=F32) + br)
    lane = lax.broadcasted_iota(jnp.int32, (tm, LANE), 1)
    onehot = (lane == i_grp).astype(F32)
    ri = lax.broadcasted_iota(jnp.int32, (tm, tm), 0)
    ci = lax.broadcasted_iota(jnp.int32, (tm, tm), 1)
    earlier = jnp.dot((ci < ri).astype(BF16), onehot.astype(BF16), preferred_element_type=F32)
    cnt = jnp.broadcast_to(jnp.sum(onehot, axis=0, keepdims=True), (8, LANE))
    lane8 = lax.broadcasted_iota(jnp.int32, (8, LANE), 1)
    start = jnp.zeros((8, LANE), F32)
    for sh in range(1, N_GROUPS):
        start += jnp.where(lane8 >= sh, pltpu.roll(cnt, sh, 1), 0.0)
    off_ref[...] = start[0:1].astype(jnp.int32)
    pos = jnp.sum(onehot * (start[0:1] + earlier), axis=-1, keepdims=True).astype(jnp.int32)
    pos_ref[...] = pos
    perm_t = (ci == pos).astype(BF16)
    hn_s = lax.dot_general(perm_t, hn, TN_DIMS, preferred_element_type=F32).astype(BF16)
    hn_ref[...] = hn_s
    _, comb = _route(jnp.dot(hn_s, wr, preferred_element_type=F32) + br)
    comb_ref[...] = comb


def moe_sort(x, g, w_grp, b_grp, w_exp, b_exp):
    t, d = x.shape
    tm = MOE_TILE
    assert t % (MOE_TILE * MOE_PACK) == 0, t
    nr = N_GROUPS + N_EXPERTS
    wr = jnp.pad(jnp.concatenate([w_grp, w_exp], axis=1), ((0, 0), (0, LANE - nr))).astype(BF16)
    br = jnp.pad(jnp.concatenate([b_grp, b_exp]), (0, LANE - nr)).reshape(1, LANE)
    row = lambda i: (i, 0)
    hn, comb, pos, off = pl.pallas_call(
        _moe_sort_kernel,
        out_shape=(jax.ShapeDtypeStruct((t, d), BF16), jax.ShapeDtypeStruct((t, LANE), F32),
                   jax.ShapeDtypeStruct((t, 1), jnp.int32), jax.ShapeDtypeStruct((t // tm, 1, LANE), jnp.int32)),
        grid=(t // tm,),
        in_specs=[pl.BlockSpec((tm, d), row), pl.BlockSpec((1, d), lambda i: (0, 0)),
                  pl.BlockSpec((d, LANE), lambda i: (0, 0)), pl.BlockSpec((1, LANE), lambda i: (0, 0))],
        out_specs=(pl.BlockSpec((tm, d), row), pl.BlockSpec((tm, LANE), row), pl.BlockSpec((tm, 1), row),
                   pl.BlockSpec((None, 1, LANE), lambda i: (i, 0, 0))),
        compiler_params=_params("parallel"),
        name="moe_sort",
    )(x, g.reshape(1, d), wr, br)
    bounds = jnp.concatenate([off[:, 0, :N_GROUPS], jnp.full((t // tm, 1), tm, jnp.int32)], axis=1)
    return hn, comb, pos, bounds.reshape(-1)


def _moe_group_kernel(bounds_ref, hn_ref, c_ref, wg_ref, wu_ref, wd_ref, y_ref):
    i, g, j = pl.program_id(0), pl.program_id(1), pl.program_id(2)

    @pl.when((g == 0) & (j == 0))
    def _():
        y_ref[...] = jnp.zeros_like(y_ref)

    expert_lane = N_GROUPS + g * EXPERTS_PER_GROUP + j

    def sub_tile(want, size):
        r0 = pl.multiple_of(jnp.minimum(want, lo + MOE_TILE - size), ROW_ALIGN)
        rows = pl.ds(r0, size)
        x = hn_ref[rows, :]
        hg = jnp.dot(x, wg_ref[...], preferred_element_type=F32)
        hu = jnp.dot(x, wu_ref[...], preferred_element_type=F32)
        lane = lax.broadcasted_iota(jnp.int32, (size, LANE), 1)
        c = jnp.sum(jnp.where(lane == expert_lane, c_ref[rows, :], 0.0), axis=-1, keepdims=True)
        row = lax.broadcasted_iota(jnp.int32, (size, 1), 0)
        c = jnp.where(row + r0 >= want, c, 0.0)
        hid = (hg * jax.nn.sigmoid(hg)) * hu * c
        y_ref[rows, :] += jnp.dot(hid.astype(BF16), wd_ref[...], preferred_element_type=F32)

    for half in range(MOE_PACK):
        base = (i * MOE_PACK + half) * (N_GROUPS + 1) + g
        lo = half * MOE_TILE
        start, end = lo + bounds_ref[base], lo + bounds_ref[base + 1]
        first = (start // ROW_ALIGN) * ROW_ALIGN
        n_full = (end - first) // MOE_SUB
        rest = end - first - n_full * MOE_SUB

        def full(k, carry, first=first):
            sub_tile(first + k * MOE_SUB, MOE_SUB)
            return carry

        lax.fori_loop(0, n_full, full, 0)
        tail = first + n_full * MOE_SUB

        @pl.when(rest > MOE_SUB // 2)
        def _():
            sub_tile(tail, MOE_SUB)

        @pl.when((rest > 0) & (rest <= MOE_SUB // 2))
        def _():
            sub_tile(tail, MOE_SUB // 2)


def moe_group_experts(hn, comb, bounds, w_gate, w_up, w_down):
    t, d = hn.shape
    tm = MOE_TILE * MOE_PACK
    ne, _, ff = w_gate.shape
    ex = lambda i, g, j, b: (g * EXPERTS_PER_GROUP + j, 0, 0)
    once = pl.Buffered(1) if MOE_PACK > 1 else None
    return pl.pallas_call(
        _moe_group_kernel,
        out_shape=jax.ShapeDtypeStruct((t, d), F32),
        grid_spec=pltpu.PrefetchScalarGridSpec(
            num_scalar_prefetch=1,
            grid=(t // tm, N_GROUPS, EXPERTS_PER_GROUP),
            in_specs=[pl.BlockSpec((tm, d), lambda i, g, j, b: (i, 0), pipeline_mode=once),
                      pl.BlockSpec((tm, LANE), lambda i, g, j, b: (i, 0)),
                      pl.BlockSpec((None, d, ff), ex), pl.BlockSpec((None, d, ff), ex),
                      pl.BlockSpec((None, ff, d), ex)],
            out_specs=pl.BlockSpec((tm, d), lambda i, g, j, b: (i, 0), pipeline_mode=once)),
        compiler_params=_params("parallel", "arbitrary", "arbitrary"),
        name="moe_group_experts",
    )(bounds, hn, comb, w_gate, w_up, w_down)


def _moe_unsort_kernel(y_ref, pos_ref, x_ref, o_ref):
    tm = y_ref.shape[0]
    perm_t = (lax.broadcasted_iota(jnp.int32, (tm, tm), 1) == pos_ref[...]).astype(BF16)
    y = y_ref[...]
    hi = y.astype(BF16)
    lo = (y - hi.astype(F32)).astype(BF16)
    o_ref[...] = (x_ref[...] + jnp.dot(perm_t, hi, preferred_element_type=F32)
                  + jnp.dot(perm_t, lo, preferred_element_type=F32))


def moe_unsort(y, pos, x):
    t, d = x.shape
    tm, tn = MOE_TILE, d // 2
    blk = pl.BlockSpec((tm, tn), lambda i, j: (i, j))
    return pl.pallas_call(
        _moe_unsort_kernel,
        out_shape=jax.ShapeDtypeStruct((t, d), F32),
        grid=(t // tm, d // tn),
        in_specs=[blk, pl.BlockSpec((tm, 1), lambda i, j: (i, 0)), blk],
        out_specs=blk,
        compiler_params=_params("parallel", "arbitrary"),
        name="moe_unsort",
    )(y, pos, x)


def _moe_unsort_norm_kernel(y_ref, pos_ref, x_ref, g_ref, o_ref):
    tm, ts = pos_ref.shape[0], y_ref.shape[0]
    perm_t = (lax.broadcasted_iota(jnp.int32, (tm, ts), 1) == pos_ref[...]).astype(BF16)
    y = y_ref[...]
    hi = y.astype(BF16)
    lo = (y - hi.astype(F32)).astype(BF16)
    x = (x_ref[...] + jnp.dot(perm_t, hi, preferred_element_type=F32)
         + jnp.dot(perm_t, lo, preferred_element_type=F32))
    o_ref[...] = x * lax.rsqrt(jnp.mean(x * x, axis=-1, keepdims=True) + EPS) * g_ref[...]


def moe_unsort_norm(y, pos, x, g, *, row0, rows, tm=512):
    d = x.shape[1]
    per = MOE_TILE // tm
    r0 = row0 // tm
    return pl.pallas_call(
        _moe_unsort_norm_kernel,
        out_shape=jax.ShapeDtypeStruct((rows, d), F32),
        grid=(rows // tm,),
        in_specs=[pl.BlockSpec((MOE_TILE, d), lambda i: ((i + r0) // per, 0)),
                  pl.BlockSpec((tm, 1), lambda i: (i + r0, 0)),
                  pl.BlockSpec((tm, d), lambda i: (i + r0, 0)),
                  pl.BlockSpec((1, d), lambda i: (0, 0))],
        out_specs=pl.BlockSpec((tm, d), lambda i: (i, 0)),
        compiler_params=_params("parallel"),
        name="moe_unsort_norm",
    )(y, pos, x, g.reshape(1, d))


def hier_moe(x, g, w_grp, b_grp, w_exp, b_exp, w_gate, w_up, w_down, *, final=None):
    hn, comb, pos, bounds = moe_sort(x, g, w_grp, b_grp, w_exp, b_exp)
    y = moe_group_experts(hn, comb, bounds, w_gate.astype(BF16), w_up.astype(BF16), w_down.astype(BF16))
    if final is None:
        return moe_unsort(y, pos, x)
    gain, counts = final
    starts = np.cumsum((0,) + tuple(counts))[:-1]
    return tuple(moe_unsort_norm(y, pos, x, gain, row0=int(r0), rows=int(n)) for r0, n in zip(starts, counts))


def even_layer(x, bsz, s, norm_g, rel_bias, w_in, w_out, w2_f, b_f, w2_b, b_b, onorm):
    n_pad = _round_up(EVEN_IN, LANE)
    w_in_p = jnp.pad(w_in, ((0, 0), (0, n_pad - EVEN_IN))).astype(BF16)
    proj = norm_linear(x, norm_g, w_in_p, tn_target=896).reshape(bsz, s, n_pad)
    tq = ATTN_TQ
    ya = attention(proj, proj, proj, heads=A_HEADS, dq=HEAD_DIM, dv=HEAD_DIM,
                   q_off=0, k_off=A_HEADS, v_off=2 * A_HEADS, scale=HEAD_DIM ** -0.5,
                   bias=dilated_bias_table(rel_bias, s, tq), tq=tq)
    q_col = 3 * A_WIDTH
    yb = gla_mixer(proj, w2_f, w2_b, b_f, b_b, onorm, q_col=q_col, k_col=q_col + B_KEYW,
                   v_col=q_col + 2 * B_KEYW, g_col=q_col + 2 * B_KEYW + B_WIDTH,
                   z_col=q_col + 2 * B_KEYW + 2 * B_WIDTH)
    t = bsz * s
    return out_proj(ya.reshape(t, A_WIDTH), yb.reshape(t, B_WIDTH), w_out.astype(BF16), x)


def _odd_columns(w_in, mu):
    c0 = C_IN
    cut = lambda u, lo, n: u[..., lo:lo + n]
    zpad = lambda u, n: jnp.pad(u, [(0, 0)] * (u.ndim - 1) + [(0, n)])
    off = np.cumsum((0,) + D_SPLITS)
    def rwkv_cols(u):
        parts = [cut(u, off[0], 3 * D_WIDTH), cut(u, off[6], D_G_RANK), cut(u, off[3], 2 * D_W_RANK),
                 cut(u, off[5], D_A_RANK)]
        u = jnp.concatenate(parts, axis=-1)
        return zpad(u, DC_PAD - u.shape[-1])
    w_kr = w_in[:, C_Q_RANK + C_KV_RANK:C_IN]
    w_all = jnp.concatenate([rwkv_cols(w_in[:, c0:]), w_in[:, :C_IN], _rot_half_cols(w_kr)], axis=1)
    n_pad = _round_up(w_all.shape[1], 9 * LANE)
    return zpad(w_all, n_pad - w_all.shape[1]).astype(BF16), rwkv_cols(mu)


def odd_layer(x, bsz, s, norm_g, w_in, w_out, q_norm, w_uq, kv_norm, w_ukv, mu, w0_f, w2_f, w0_b, w2_b,
              a0, a2, g2, k_k, k_a, r_k, ln_g, ln_b):
    t = bsz * s
    w_all, mu_cols = _odd_columns(w_in, mu)
    proj = norm_linear(x, norm_g, w_all, tn_target=1152)
    inv = 1.0 / (ROPE_THETA ** (jnp.arange(0, C_ROPE, 2, dtype=F32) / C_ROPE))
    ang = jnp.arange(s, dtype=F32)[:, None] * inv[None, :]
    cos = jnp.pad(jnp.concatenate([jnp.cos(ang)] * 2, axis=1), ((0, 0), (0, LANE - C_ROPE)), constant_values=1.0)
    sin = jnp.pad(jnp.concatenate([jnp.sin(ang)] * 2, axis=1), ((0, 0), (0, LANE - C_ROPE)))
    q, k, v = mla_up(proj, q_norm, w_uq, kv_norm, w_ukv, jnp.tile(cos, (bsz, 1)), jnp.tile(sin, (bsz, 1)),
                     col0=DC_PAD)
    r3 = lambda u: u.reshape(bsz, s, -1)
    yc = attention(r3(q), r3(k), r3(v), heads=C_HEADS, dq=C_QK, dv=C_V, q_off=0, k_off=0, v_off=0,
                   scale=(C_NOPE + C_ROPE) ** -0.5, tq=s)
    yd = rwkv7_mixer(proj, bsz, s, mu_cols, w0_f, w2_f, w0_b, w2_b, a0, a2, g2, k_k, k_a, r_k, ln_g, ln_b)
    return out_proj(yc.reshape(t, C_WIDTH), yd, w_out.astype(BF16), x)


def kernel(x_prompt, x_sample, rel_bias, norm_mix, norm_ffn, norm_final, ev_w_in, ev_w_out, ev_gla_w2_f, ev_gla_b_f, ev_gla_w2_b, ev_gla_b_b, ev_gla_onorm, od_w_in, od_w_out, od_q_norm, od_w_uq, od_kv_norm, od_w_ukv, od_mu, od_w0_f, od_w2_f, od_w0_b, od_w2_b, od_a0, od_a2, od_g2, od_k_k, od_k_a, od_r_k, od_ln_g, od_ln_b, moe_w_grp, moe_b_grp, moe_w_exp, moe_b_exp, moe_w_gate, moe_w_up, moe_w_down):
    nb_p = x_prompt.shape[0]
    x = jnp.concatenate([x_prompt, x_sample], axis=0)
    bsz, s, d = x.shape
    x = x.reshape(bsz * s, d)
    for i in range(DEPTH):
        j = i // 2
        if i % 2 == 0:
            x = even_layer(x, bsz, s, norm_mix[i], rel_bias, ev_w_in[j], ev_w_out[j], ev_gla_w2_f[j],
                           ev_gla_b_f[j], ev_gla_w2_b[j], ev_gla_b_b[j], ev_gla_onorm[j])
        else:
            x = odd_layer(x, bsz, s, norm_mix[i], od_w_in[j], od_w_out[j], od_q_norm[j], od_w_uq[j],
                          od_kv_norm[j], od_w_ukv[j], od_mu[j], od_w0_f[j], od_w2_f[j], od_w0_b[j],
                          od_w2_b[j], od_a0[j], od_a2[j], od_g2[j], od_k_k[j], od_k_a[j], od_r_k[j],
                          od_ln_g[j], od_ln_b[j])
        final = (norm_final, (nb_p * s, (bsz - nb_p) * s)) if i == DEPTH - 1 else None
        x = hier_moe(x, norm_ffn[i], moe_w_grp[i], moe_b_grp[i], moe_w_exp[i], moe_b_exp[i],
                     moe_w_gate[i], moe_w_up[i], moe_w_down[i], final=final)
    y_p, y_s = x
    return (y_p.reshape(nb_p, s, d), y_s.reshape(bsz - nb_p, s, d))
```

```python
import functools

import jax, jax.numpy as jnp
from jax import lax
import numpy as np
from jax.experimental import pallas as pl
from jax.experimental.pallas import tpu as pltpu

F32, BF16 = jnp.float32, jnp.bfloat16

D_MODEL = 2048
DEPTH = 2
MIX_HALF = D_MODEL // 2
HEAD_DIM = 128
EPS = 1e-6
NEG = -1e30

A_HEADS = MIX_HALF // HEAD_DIM
A_WIDTH = A_HEADS * HEAD_DIM
A_PATTERNS = ((128, 1), (512, 4), (2048, 16))
N_BUCKETS = 32
MAX_DISTANCE = 1024

B_HEADS = 4
B_DV = MIX_HALF // B_HEADS
B_DK = B_DV // 2
B_WIDTH = B_HEADS * B_DV
B_KEYW = B_HEADS * B_DK
B_GATE_RANK = 16
B_GATE_TAU = 16.0
B_CHUNK = 64

C_HEADS = MIX_HALF // 128
C_Q_RANK = 512
C_KV_RANK = 256
C_NOPE = 128
C_ROPE = 64
C_V = 128
C_WIDTH = C_HEADS * C_V
C_QK = 256
ROPE_THETA = 10000.0

D_HEAD = 64
D_HEADS = MIX_HALF // D_HEAD
D_WIDTH = D_HEADS * D_HEAD
D_W_RANK = 64
D_A_RANK = 64
D_G_RANK = 128
D_LN_EPS = 64e-5
D_SPLITS = (D_WIDTH, D_WIDTH, D_WIDTH, D_W_RANK, D_W_RANK, D_A_RANK, D_G_RANK)
D_SHIFT = 3 * D_WIDTH + 2 * D_W_RANK + D_A_RANK + D_G_RANK

N_GROUPS = 4
EXPERTS_PER_GROUP = 4
N_EXPERTS = N_GROUPS * EXPERTS_PER_GROUP

EVEN_IN = 3 * A_WIDTH + 2 * B_KEYW + 2 * B_WIDTH + 2 * B_GATE_RANK
C_IN = C_Q_RANK + C_KV_RANK + C_ROPE

LANE = 128
VMEM_LIMIT = 52 * 1024 * 1024


def _params(*sem):
    return pltpu.CompilerParams(dimension_semantics=sem, vmem_limit_bytes=VMEM_LIMIT)


def _round_up(n, m):
    return -(-n // m) * m


NT_DIMS = (((1,), (1,)), ((), ()))
TN_DIMS = (((0,), (0,)), ((), ()))


def _split3(x):
    hi = x.astype(BF16)
    r1 = x - hi.astype(F32)
    mid = r1.astype(BF16)
    lo = (r1 - mid.astype(F32)).astype(BF16)
    return hi, mid, lo


def _pick_tile(n, target):
    best = LANE
    for t in range(LANE, target + 1, LANE):
        if n % t == 0:
            best = t
    return best


def _norm_linear_kernel(x_ref, g_ref, w_ref, o_ref, xn_ref):
    @pl.when(pl.program_id(1) == 0)
    def _():
        x = x_ref[...]
        y = x * lax.rsqrt(jnp.mean(x * x, axis=-1, keepdims=True) + EPS) * g_ref[...]
        xn_ref[...] = y.astype(BF16)

    o_ref[...] = jnp.dot(xn_ref[...], w_ref[...], preferred_element_type=F32)


def norm_linear(x, g, w, *, tm=1024, tn_target=1024):
    t, k = x.shape
    n = w.shape[1]
    tn = _pick_tile(n, tn_target)
    return pl.pallas_call(
        _norm_linear_kernel,
        out_shape=jax.ShapeDtypeStruct((t, n), F32),
        grid=(t // tm, n // tn),
        in_specs=[pl.BlockSpec((tm, k), lambda i, j: (i, 0)),
                  pl.BlockSpec((1, k), lambda i, j: (0, 0)),
                  pl.BlockSpec((k, tn), lambda i, j: (0, j))],
        out_specs=pl.BlockSpec((tm, tn), lambda i, j: (i, j)),
        scratch_shapes=[pltpu.VMEM((tm, k), BF16)],
        compiler_params=_params("parallel", "arbitrary"),
        name="norm_linear",
    )(x, g.reshape(1, k), w)


def _out_proj_kernel(a_ref, b_ref, wa_ref, wb_ref, x_ref, o_ref):
    acc = jnp.dot(a_ref[...], wa_ref[...], preferred_element_type=F32)
    acc += jnp.dot(b_ref[...], wb_ref[...], preferred_element_type=F32)
    o_ref[...] = x_ref[...] + acc


def out_proj(a, b, w, x, *, tm=1024, tn=1024):
    t, ka = a.shape
    kb = b.shape[1]
    n = w.shape[1]
    return pl.pallas_call(
        _out_proj_kernel,
        out_shape=jax.ShapeDtypeStruct((t, n), F32),
        grid=(t // tm, n // tn),
        in_specs=[pl.BlockSpec((tm, ka), lambda i, j: (i, 0)),
                  pl.BlockSpec((tm, kb), lambda i, j: (i, 0)),
                  pl.BlockSpec((ka, tn), lambda i, j: (0, j)),
                  pl.BlockSpec((kb, tn), lambda i, j: (0, j)),
                  pl.BlockSpec((tm, tn), lambda i, j: (i, j))],
        out_specs=pl.BlockSpec((tm, tn), lambda i, j: (i, j)),
        compiler_params=_params("parallel", "arbitrary"),
        name="out_proj",
    )(a, b, w[:ka], w[ka:], x)


LOG2E = float(np.log2(np.e))
ATTN_KB = 256
ATTN_TQ = 1024


def _attn_kernel(*refs, scale, has_bias):
    if has_bias:
        q_ref, k_ref, v_ref, bias_ref, o_ref, kb_ref, vb_ref = refs
    else:
        q_ref, k_ref, v_ref, o_ref, kb_ref, vb_ref = refs
    s_len, dv = v_ref.shape

    @pl.when(pl.program_id(2) == 0)
    def _():
        kb_ref[...] = k_ref[...].astype(BF16)
        vb_ref[...] = jnp.concatenate([v_ref[...].astype(BF16), jnp.ones((s_len, LANE), BF16)], axis=1)

    q = (q_ref[...] * (scale * LOG2E)).astype(BF16)
    blocks = [slice(j * ATTN_KB, (j + 1) * ATTN_KB) for j in range(s_len // ATTN_KB)]
    scores = []
    m = None
    for blk in blocks:
        sj = lax.dot_general(q, kb_ref[blk, :], NT_DIMS, preferred_element_type=F32)
        if has_bias:
            sj = sj + bias_ref[:, blk]
        mj = jnp.max(sj, axis=-1, keepdims=True)
        m = mj if m is None else jnp.maximum(m, mj)
        scores.append(sj)
    o = None
    for blk, sj in zip(blocks, scores):
        oj = jnp.dot(jnp.exp2(sj - m).astype(BF16), vb_ref[blk, :], preferred_element_type=F32)
        o = oj if o is None else o + oj
    den = o[:, dv:]
    if dv > LANE:
        den = jnp.concatenate([den] * (dv // LANE), axis=1)
    o_ref[...] = (o[:, :dv] / den).astype(o_ref.dtype)


def attention(q, k, v, *, heads, dq, dv, q_off, k_off, v_off, scale, bias=None, tq=ATTN_TQ):
    b, s, _ = q.shape
    nq = s // tq
    assert s % tq == 0 and s % ATTN_KB == 0, (s, tq)
    in_specs = [pl.BlockSpec((None, tq, dq), lambda bi, h, qi: (bi, qi, q_off + h)),
                pl.BlockSpec((None, s, dq), lambda bi, h, qi: (bi, 0, k_off + h)),
                pl.BlockSpec((None, s, dv), lambda bi, h, qi: (bi, 0, v_off + h))]
    args = [q, k, v]
    if bias is not None:
        in_specs.append(pl.BlockSpec((None, None, tq, s), lambda bi, h, qi: (h, qi, 0, 0)))
        args.append(bias)
    return pl.pallas_call(
        functools.partial(_attn_kernel, scale=scale, has_bias=bias is not None),
        out_shape=jax.ShapeDtypeStruct((b, s, heads * dv), BF16),
        grid=(b, heads, nq),
        in_specs=in_specs,
        out_specs=pl.BlockSpec((None, tq, dv), lambda bi, h, qi: (bi, qi, h)),
        scratch_shapes=[pltpu.VMEM((s, dq), BF16), pltpu.VMEM((s, dv + LANE), BF16)],
        compiler_params=_params("parallel", "parallel", "arbitrary"),
        name="attention_bias" if bias is not None else "attention",
    )(*args)


def _t5_bucket(rel):
    half = N_BUCKETS // 2
    exact = half // 2
    n = np.abs(rel)
    large = exact + (np.log(np.maximum(n, 1) / exact) / np.log(MAX_DISTANCE / exact) * (half - exact)).astype(np.int64)
    large = np.minimum(large, half - 1)
    return ((rel > 0) * half + np.where(n < exact, n, large)).astype(np.int32)


def dilated_bias_table(rel_bias, s, tq):
    heads = rel_bias.shape[1]
    d = np.arange(-(s - 1), s)
    count = np.zeros(d.shape, np.float32)
    for window, dil in A_PATTERNS:
        count += ((d % dil == 0) & (np.abs(d) <= (window // (2 * dil)) * dil)).astype(np.float32)
    logc = np.where(count > 0, np.log(np.maximum(count, 1.0)), NEG).astype(np.float32)
    onehot = (_t5_bucket(d)[:, None] == np.arange(N_BUCKETS)[None, :]).astype(np.float32)
    line = jnp.transpose(jnp.dot(onehot, rel_bias.astype(F32), precision=lax.Precision.HIGHEST)) + logc[None]
    line = line * LOG2E
    width = 2 * s
    line = jnp.pad(line, ((0, 0), (0, width - line.shape[1])))[:, None, :]
    nq = s // tq
    return pl.pallas_call(
        functools.partial(_skew_kernel, tq=tq, nq=nq),
        out_shape=jax.ShapeDtypeStruct((heads, nq, tq, s), F32),
        grid=(heads, nq),
        in_specs=[pl.BlockSpec((None, 1, width), lambda h, qi: (h, 0, 0))],
        out_specs=pl.BlockSpec((None, None, tq, s), lambda h, qi: (h, qi, 0, 0)),
        compiler_params=_params("parallel", "arbitrary"),
        name="bias_skew",
    )(line)


def _skew_kernel(line_ref, o_ref, *, tq, nq):
    width = line_ref.shape[1]
    first = (nq - 1 - pl.program_id(1)) * tq
    x = jnp.broadcast_to(line_ref[...], (tq, width))
    x = pltpu.roll(x, width - (tq - 1) - first, 1, stride=1, stride_axis=0)
    o_ref[...] = x[:, :o_ref.shape[1]]


GLA_UNROLL = 16


def _gla_kernel(q_ref, k_ref, v_ref, g_ref, z_ref, w2f_ref, w2b_ref, bf_ref, bb_ref, on_ref, o_ref,
                la_ref, acc_ref, qcat_ref, upd_ref, dec_ref, scat_ref, st_ref):
    s_len = q_ref.shape[0]
    c = B_CHUNK
    nchunk = s_len // c
    z = z_ref[...].astype(BF16)
    gate = lambda w2_ref, b_ref: jax.nn.log_sigmoid(
        jnp.dot(z, w2_ref[...], preferred_element_type=F32) + b_ref[...]) * (1.0 / B_GATE_TAU)
    la_ref[0] = gate(w2f_ref, bf_ref)
    la_ref[1] = gate(w2b_ref, bb_ref)

    ri = lax.broadcasted_iota(jnp.int32, (c, c), 0)
    ci = lax.broadcasted_iota(jnp.int32, (c, c), 1)
    keep = (ri >= ci, ri <= ci)
    tri3 = tuple(jnp.concatenate([kp.astype(BF16)] * 3, axis=1) for kp in keep)

    def chunk_rows(n):
        return pl.ds(pl.multiple_of(n * c, c), c)

    def pass1(i, carry):
        units = [(i * GLA_UNROLL + u, d) for u in range(GLA_UNROLL) for d in range(2)]
        st = []
        for n, d in units:
            rows = chunk_rows(n)
            gcum = jnp.dot(tri3[d], jnp.concatenate(_split3(la_ref[d, rows, :]), axis=0),
                           preferred_element_type=F32)
            st.append(dict(rows=rows, gcum=gcum))
        for (n, d), c in zip(units, st):
            gcum = c["gcum"]
            gend = gcum[0:1] if d == 1 else gcum[B_CHUNK - 1:B_CHUNK]
            kc = k_ref[c["rows"], :]
            c["q_in"] = (q_ref[c["rows"], :] * (B_DK ** -0.5) * jnp.exp(gcum)).astype(BF16)
            c["k_out"] = (kc * jnp.exp(gend - gcum)).astype(BF16)
            c["att"] = lax.dot_general(c["q_in"], (kc * jnp.exp(-gcum)).astype(BF16), NT_DIMS,
                                       preferred_element_type=F32)
            dec_ref[d, n] = jnp.broadcast_to(jnp.exp(gend), (8, B_DK))
        for (n, d), c in zip(units, st):
            vc = v_ref[c["rows"], :].astype(BF16)
            c["o"] = jnp.dot(jnp.where(keep[d], c["att"], 0.0).astype(BF16), vc, preferred_element_type=F32)
            upd_ref[d, n] = lax.dot_general(vc, c["k_out"], TN_DIMS, preferred_element_type=F32)
            qcat_ref[c["rows"], d * B_DK:(d + 1) * B_DK] = c["q_in"]
        for u in range(GLA_UNROLL):
            acc_ref[st[2 * u]["rows"], :] = st[2 * u]["o"] + st[2 * u + 1]["o"]
        return carry

    lax.fori_loop(0, nchunk // GLA_UNROLL, pass1, 0)

    st_ref[...] = jnp.zeros_like(st_ref)

    def pass2(n, carry):
        for d, m in ((0, n), (1, nchunk - 1 - n)):
            state = st_ref[d]
            scat_ref[m, :, d * B_DK:(d + 1) * B_DK] = state.astype(BF16)
            st_ref[d] = state * dec_ref[d, m][0:1] + upd_ref[d, m]
        return carry

    lax.fori_loop(0, nchunk, pass2, 0)

    def pass3(i, carry):
        rows = [chunk_rows(i * GLA_UNROLL + u) for u in range(GLA_UNROLL)]
        outs = [acc_ref[r, :] + lax.dot_general(qcat_ref[r, :], scat_ref[i * GLA_UNROLL + u], NT_DIMS,
                                                preferred_element_type=F32) for u, r in enumerate(rows)]
        for r, o in zip(rows, outs):
            o = o * lax.rsqrt(jnp.mean(o * o, axis=-1, keepdims=True) + EPS) * on_ref[...]
            g = g_ref[r, :]
            o_ref[r, :] = (o * (g * jax.nn.sigmoid(g))).astype(o_ref.dtype)
        return carry

    lax.fori_loop(0, nchunk // GLA_UNROLL, pass3, 0)


def gla_mixer(proj, w2f, w2b, b_f, b_b, onorm, *, q_col, k_col, v_col, g_col, z_col):
    b, s, _ = proj.shape
    assert s % (B_CHUNK * GLA_UNROLL) == 0, s
    hm = lambda blk: (lambda bi, h: (bi, 0, blk + h))
    w2f_p = jnp.zeros((LANE, B_KEYW), F32).at[:B_GATE_RANK].set(w2f).astype(BF16)
    w2b_p = jnp.zeros((LANE, B_KEYW), F32).at[B_GATE_RANK:2 * B_GATE_RANK].set(w2b).astype(BF16)
    return pl.pallas_call(
        _gla_kernel,
        out_shape=jax.ShapeDtypeStruct((b, s, B_WIDTH), BF16),
        grid=(b, B_HEADS),
        in_specs=[pl.BlockSpec((None, s, B_DK), hm(q_col // B_DK)),
                  pl.BlockSpec((None, s, B_DK), hm(k_col // B_DK)),
                  pl.BlockSpec((None, s, B_DV), hm(v_col // B_DV)),
                  pl.BlockSpec((None, s, B_DV), hm(g_col // B_DV)),
                  pl.BlockSpec((None, s, LANE), lambda bi, h: (bi, 0, z_col // LANE)),
                  pl.BlockSpec((LANE, B_DK), lambda bi, h: (0, h)),
                  pl.BlockSpec((LANE, B_DK), lambda bi, h: (0, h)),
                  pl.BlockSpec((1, B_DK), lambda bi, h: (0, h)),
                  pl.BlockSpec((1, B_DK), lambda bi, h: (0, h)),
                  pl.BlockSpec((1, B_DV), lambda bi, h: (0, 0))],
        out_specs=pl.BlockSpec((None, s, B_DV), lambda bi, h: (bi, 0, h)),
        scratch_shapes=[pltpu.VMEM((2, s, B_DK), F32),
                        pltpu.VMEM((s, B_DV), F32),
                        pltpu.VMEM((s, 2 * B_DK), BF16),
                        pltpu.VMEM((2, s // B_CHUNK, B_DV, B_DK), F32),
                        pltpu.VMEM((2, s // B_CHUNK, 8, B_DK), F32),
                        pltpu.VMEM((s // B_CHUNK, B_DV, 2 * B_DK), BF16),
                        pltpu.VMEM((2, B_DV, B_DK), F32)],
        compiler_params=_params("parallel", "arbitrary"),
        name="gla_mixer",
    )(proj, proj, proj, proj, proj, w2f_p, w2b_p, b_f.reshape(1, -1), b_b.reshape(1, -1), onorm.reshape(1, -1))


def _mla_up_kernel(cq_ref, ckv_ref, kr_ref, qn_ref, kvn_ref, wq_ref, wqr_ref, wkv_ref, cos_ref, sin_ref,
                   q_ref, k_ref, v_ref):
    def rms(x, g):
        return (x * lax.rsqrt(jnp.mean(x * x, axis=-1, keepdims=True) + EPS) * g).astype(BF16)

    cq = rms(cq_ref[...], qn_ref[...])
    ckv = rms(ckv_ref[...], kvn_ref[...])
    cos, sin = cos_ref[...], sin_ref[...]
    kr = kr_ref[...]
    k_rope = kr * cos + pltpu.roll(kr, LANE - C_ROPE, 1) * sin
    lane = lax.broadcasted_iota(jnp.int32, k_rope.shape, 1)
    k_rope = jnp.where(lane < C_ROPE, k_rope, 0.0)
    for h in range(C_HEADS):
        q = jnp.dot(cq, wq_ref[:, h * C_QK:(h + 1) * C_QK], preferred_element_type=F32)
        qp = jnp.dot(cq, wqr_ref[:, h * LANE:(h + 1) * LANE], preferred_element_type=F32)
        q_ref[:, h * C_QK:h * C_QK + C_NOPE] = q[:, :C_NOPE]
        q_ref[:, h * C_QK + C_NOPE:(h + 1) * C_QK] = q[:, C_NOPE:] * cos + qp * sin
        kv = jnp.dot(ckv, wkv_ref[:, h * 2 * LANE:(h + 1) * 2 * LANE], preferred_element_type=F32)
        k_ref[:, h * C_QK:h * C_QK + C_NOPE] = kv[:, :C_NOPE]
        k_ref[:, h * C_QK + C_NOPE:(h + 1) * C_QK] = k_rope
        v_ref[:, h * C_V:(h + 1) * C_V] = kv[:, C_NOPE:]


def _rot_half_cols(w):
    half = w.shape[-1] // 2
    return jnp.concatenate([-w[..., half:], w[..., :half]], axis=-1)


def mla_up(proj, q_norm, w_uq, kv_norm, w_ukv, cos, sin, *, col0, tm=512):
    t = proj.shape[0]
    wq = w_uq.reshape(C_Q_RANK, C_HEADS, C_NOPE + C_ROPE)
    wq_main = jnp.pad(wq, ((0, 0), (0, 0), (0, C_QK - C_NOPE - C_ROPE))).reshape(C_Q_RANK, C_HEADS * C_QK)
    wq_rot = jnp.pad(_rot_half_cols(wq[..., C_NOPE:]), ((0, 0), (0, 0), (0, LANE - C_ROPE)))
    wq_rot = wq_rot.reshape(C_Q_RANK, C_HEADS * LANE)
    row = lambda i: (i, 0)
    full = lambda arr: pl.BlockSpec(arr.shape, lambda i: (0, 0))
    g_q, g_kv = q_norm.reshape(1, -1), kv_norm.reshape(1, -1)
    wq_main, wq_rot, wkv = wq_main.astype(BF16), wq_rot.astype(BF16), w_ukv.astype(BF16)
    return pl.pallas_call(
        _mla_up_kernel,
        out_shape=(jax.ShapeDtypeStruct((t, C_HEADS * C_QK), F32),
                   jax.ShapeDtypeStruct((t, C_HEADS * C_QK), F32),
                   jax.ShapeDtypeStruct((t, C_WIDTH), F32)),
        grid=(t // tm,),
        in_specs=[pl.BlockSpec((tm, C_Q_RANK), lambda i: (i, col0 // C_Q_RANK)),
                  pl.BlockSpec((tm, C_KV_RANK), lambda i: (i, (col0 + C_Q_RANK) // C_KV_RANK)),
                  pl.BlockSpec((tm, LANE), lambda i: (i, (col0 + C_Q_RANK + C_KV_RANK) // LANE)),
                  full(g_q), full(g_kv), full(wq_main), full(wq_rot), full(wkv),
                  pl.BlockSpec((tm, LANE), row), pl.BlockSpec((tm, LANE), row)],
        out_specs=(pl.BlockSpec((tm, C_HEADS * C_QK), row),
                   pl.BlockSpec((tm, C_HEADS * C_QK), row),
                   pl.BlockSpec((tm, C_WIDTH), row)),
        compiler_params=_params("parallel"),
        name="mla_up",
    )(proj, proj, proj, g_q, g_kv, wq_main, wq_rot, wkv, cos, sin)


RG = 4
RGW = RG * D_HEAD
RCH = 64
DC_R, DC_K, DC_V = 0, D_WIDTH, 2 * D_WIDTH
DC_ZG = 3 * D_WIDTH
DC_ZW = DC_ZG + D_G_RANK
DC_ZA = DC_ZW + 2 * D_W_RANK
DC_PAD = 7 * 512


def _head_sums(x, bo):
    return jnp.concatenate(
        [jnp.dot(x[:, RGW * g:RGW * (g + 1)], bo, preferred_element_type=F32, precision=lax.Precision.HIGHEST)
         for g in range(x.shape[1] // RGW)], axis=1)


def _block_ones():
    i = np.arange(RGW)
    return jnp.asarray((i[:, None] // D_HEAD) == (i[None, :] // D_HEAD), F32)


def _rwkv_prep_kernel(x_ref, xp_ref, xn_ref, mu_ref, w2f_ref, w2b_ref, a2_ref, g2_ref, w0f_ref, w0b_ref,
                      a0_ref, kk_ref, ka_ref, rk_ref, bo_ref,
                      r_ref, k_ref, v_ref, a_ref, b_ref, lwf_ref, lwb_ref, g_ref, bonus_ref, *, tiles_per_seq):
    i = pl.program_id(0) % tiles_per_seq
    x = x_ref[...]
    tm = x.shape[0]
    row = lax.broadcasted_iota(jnp.int32, x.shape, 0)
    prev_row = jnp.where(i == 0, 0.0, xp_ref[7:8, :])
    next_row = jnp.where(i == tiles_per_seq - 1, 0.0, xn_ref[0:1, :])
    prev = jnp.where(row == 0, prev_row, pltpu.roll(x, 1, 0))
    nxt = jnp.where(row == tm - 1, next_row, pltpu.roll(x, tm - 1, 0))
    x = x + mu_ref[...] * (0.5 * (prev + nxt) - x)
    r, k, v = x[:, DC_R:DC_R + D_WIDTH], x[:, DC_K:DC_K + D_WIDTH], x[:, DC_V:DC_V + D_WIDTH]
    zg = x[:, DC_ZG:DC_ZG + LANE]
    zw = x[:, DC_ZW:DC_ZW + LANE]
    za = x[:, DC_ZA:DC_ZA + LANE]
    tz = jnp.tanh(zw).astype(BF16)
    log_decay = lambda w0_ref, w2_ref: -np.exp(-0.5).astype(np.float32) * jax.nn.sigmoid(
        w0_ref[...] + jnp.dot(tz, w2_ref[...], preferred_element_type=F32))
    lwf_ref[...] = log_decay(w0f_ref, w2f_ref)
    lwb_ref[...] = log_decay(w0b_ref, w2b_ref)
    ag = jax.nn.sigmoid(a0_ref[...] + jnp.dot(za.astype(BF16), a2_ref[...], preferred_element_type=F32))
    g_ref[...] = jnp.dot(jax.nn.sigmoid(zg).astype(BF16), g2_ref[...], preferred_element_type=F32)
    bo = bo_ref[...]
    kk = k * kk_ref[...]
    kk = kk / jnp.maximum(jnp.sqrt(_head_sums(kk * kk, bo)), 1e-12)
    k = k * (1.0 + (ag - 1.0) * ka_ref[...])
    r_ref[...] = r
    k_ref[...] = k
    v_ref[...] = v
    a_ref[...] = -kk
    b_ref[...] = kk * ag
    bonus_ref[...] = _head_sums(r * k * rk_ref[...], bo) * v


def rwkv_prep(proj, mu, w0_f, w2_f, w0_b, w2_b, a0, a2, g2, k_k, k_a, r_k, *, seq, tm=256):
    t = proj.shape[0]
    tiles_per_seq = seq // tm
    assert seq % tm == 0 and t % seq == 0, (t, seq)
    hb = tm // 8
    nblk8 = t // 8
    pad_rows = lambda w, lo: jnp.zeros((LANE, D_WIDTH), F32).at[lo:lo + w.shape[0]].set(w).astype(BF16)
    vec = lambda u: u.reshape(1, -1)
    consts = [vec(mu), pad_rows(w2_f, 0), pad_rows(w2_b, D_W_RANK), pad_rows(a2, 0), g2.astype(BF16),
              vec(w0_f), vec(w0_b), vec(a0), vec(k_k), vec(k_a), vec(r_k), _block_ones()]
    full = lambda arr: pl.BlockSpec(arr.shape, lambda i: (0, 0))
    out_spec = pl.BlockSpec((tm, D_WIDTH), lambda i: (i, 0))
    return pl.pallas_call(
        functools.partial(_rwkv_prep_kernel, tiles_per_seq=tiles_per_seq),
        out_shape=tuple(jax.ShapeDtypeStruct((t, D_WIDTH), F32) for _ in range(9)),
        grid=(t // tm,),
        in_specs=[pl.BlockSpec((tm, DC_PAD), lambda i: (i, 0)),
                  pl.BlockSpec((8, DC_PAD), lambda i: (jnp.maximum(i * hb - 1, 0), 0)),
                  pl.BlockSpec((8, DC_PAD), lambda i: (jnp.minimum((i + 1) * hb, nblk8 - 1), 0))]
                 + [full(c) for c in consts],
        out_specs=tuple(out_spec for _ in range(9)),
        compiler_params=_params("parallel"),
        name="rwkv_prep",
    )(proj, proj, proj, *consts)


def _rwkv_chunk_kernel(*refs, ngroups):
    ins, (yf_ref, yb_ref, mt_ref) = refs[:12], refs[12:]

    @pl.when(pl.program_id(1) == 0)
    def _():
        mt_ref[...] = jnp.zeros_like(mt_ref)

    row = lax.broadcasted_iota(jnp.int32, (RCH, RGW), 0)
    col = lax.broadcasted_iota(jnp.int32, (RCH, RGW), 1) & (RCH - 1)
    bdmask = (lax.broadcasted_iota(jnp.int32, (RGW, RGW), 0) // D_HEAD
              == lax.broadcasted_iota(jnp.int32, (RGW, RGW), 1) // D_HEAD)
    tr = lax.broadcasted_iota(jnp.int32, (RCH, RCH), 0)
    tc = lax.broadcasted_iota(jnp.int32, (RCH, RCH), 1)
    zero = jnp.zeros((), F32)

    def bd(z):
        zb = z.astype(BF16)
        return jnp.where(bdmask, jnp.concatenate([zb] * RG, axis=0), jnp.zeros((), BF16))

    def mm(x, y, dims=None):
        x = x.astype(BF16)
        if dims is None:
            return jnp.dot(x, y, preferred_element_type=F32)
        return lax.dot_general(x, y, dims, preferred_element_type=F32)

    chains = [(d, g) for d in range(2) for g in range(ngroups)]
    st = []
    for d, g in chains:
        backward = d == 1
        r_ref, k_ref, v_ref, a_ref, b_ref, lw_ref = ins[6 * d:6 * d + 6]
        tri = ((tc >= tr) if backward else (tc <= tr)).astype(BF16)
        sl = slice(RGW * g, RGW * (g + 1))
        r, k, v, a, b, lw = (ref[:, sl] for ref in (r_ref, k_ref, v_ref, a_ref, b_ref, lw_ref))
        lam = jnp.dot(jnp.concatenate([tri] * 3, axis=1), jnp.concatenate(_split3(lw), axis=0),
                      preferred_element_type=F32)
        lamc = lam[0:1] if backward else lam[RCH - 1:RCH]
        e_inv = jnp.exp(-lam)
        e_out = jnp.exp(lamc - lam)
        ar = jnp.concatenate([a * jnp.exp(lam - lw), r * jnp.exp(lam)], axis=0).astype(BF16)
        bk = jnp.concatenate([b * e_out, k * e_out], axis=0).astype(BF16)
        st.append(dict(ar=ar, bk=bk, v=v, lamc=lamc, sl=sl,
                       gb=mm(ar, bd(b * e_inv), NT_DIMS), gk=mm(ar, bd(k * e_inv), NT_DIMS)))
    for (d, g), c in zip(chains, st):
        strict = (col > row) if d == 1 else (col < row)
        incl = (col >= row) if d == 1 else (col <= row)
        c["lp"] = jnp.where(strict, c["gb"][:RCH], zero)
        lak = jnp.where(strict, c["gk"][:RCH], zero)
        c["grb"] = jnp.where(incl, c["gb"][RCH:], zero).astype(BF16)
        c["grk"] = jnp.where(incl, c["gk"][RCH:], zero).astype(BF16)
        c["mt"] = mt_ref[d, g]
        amrm = mm(c["ar"], c["mt"].astype(BF16), NT_DIMS)
        c["bdv"] = bd(c["v"])
        c["u"] = amrm[:RCH] + mm(lak, c["bdv"])
        c["rm"] = amrm[RCH:]
    for rnd in range(6):
        for c in st:
            lpb = c["lp"].astype(BF16)
            c["u"] = c["u"] + mm(lpb, bd(c["u"]))
            if rnd < 5:
                c["lp"] = mm(lpb, bd(c["lp"]))
    for (d, g), c in zip(chains, st):
        y_ref = yb_ref if d == 1 else yf_ref
        y_ref[:, c["sl"]] = c["rm"] + mm(c["grb"], bd(c["u"])) + mm(c["grk"], c["bdv"])
        uv = jnp.concatenate([c["u"], c["v"]], axis=0).astype(BF16)
        upd = lax.dot_general(uv, c["bk"], TN_DIMS, preferred_element_type=F32)
        mt_ref[d, g] = c["mt"] * jnp.exp(c["lamc"]) + jnp.where(bdmask, upd, zero)


def rwkv_chunked(r, k, v, a, b, lwf, lwb):
    bsz, s, wd = r.shape
    nc = s // RCH
    assert s % RCH == 0 and wd % RGW == 0, (s, wd)
    fspec = pl.BlockSpec((None, RCH, wd), lambda bi, n: (bi, n, 0))
    bspec = pl.BlockSpec((None, RCH, wd), lambda bi, n: (bi, nc - 1 - n, 0))
    return pl.pallas_call(
        functools.partial(_rwkv_chunk_kernel, ngroups=wd // RGW),
        out_shape=(jax.ShapeDtypeStruct((bsz, s, wd), F32), jax.ShapeDtypeStruct((bsz, s, wd), F32)),
        grid=(bsz, nc),
        in_specs=[fspec] * 6 + [bspec] * 6,
        out_specs=(fspec, bspec),
        scratch_shapes=[pltpu.VMEM((2, wd // RGW, RGW, RGW), F32)],
        compiler_params=_params("parallel", "arbitrary"),
        name="rwkv_chunked",
    )(r, k, v, a, b, lwf, r, k, v, a, b, lwb)


def _rwkv_post_kernel(yf_ref, yb_ref, bonus_ref, g_ref, lng_ref, lnb_ref, bo_ref, o_ref):
    bo = bo_ref[...]
    y = yf_ref[...] + yb_ref[...]
    yc = y - _head_sums(y, bo) * (1.0 / D_HEAD)
    var = _head_sums(yc * yc, bo) * (1.0 / D_HEAD)
    y = yc * lax.rsqrt(var + D_LN_EPS) * lng_ref[...] + lnb_ref[...]
    o_ref[...] = ((y + bonus_ref[...]) * g_ref[...]).astype(o_ref.dtype)


def rwkv_post(yf, yb, bonus, g, ln_g, ln_b, *, tm=512):
    t, wd = yf.shape
    row = pl.BlockSpec((tm, wd), lambda i: (i, 0))
    vec = pl.BlockSpec((1, wd), lambda i: (0, 0))
    bo = _block_ones()
    return pl.pallas_call(
        _rwkv_post_kernel,
        out_shape=jax.ShapeDtypeStruct((t, wd), BF16),
        grid=(t // tm,),
        in_specs=[row, row, row, row, vec, vec, pl.BlockSpec(bo.shape, lambda i: (0, 0))],
        out_specs=row,
        compiler_params=_params("parallel"),
        name="rwkv_post",
    )(yf, yb, bonus, g, ln_g.reshape(1, wd), ln_b.reshape(1, wd), bo)


def rwkv7_mixer(proj, bsz, s, mu, w0_f, w2_f, w0_b, w2_b, a0, a2, g2, k_k, k_a, r_k, ln_g, ln_b):
    r, k, v, a, b, lwf, lwb, g, bonus = rwkv_prep(proj, mu, w0_f, w2_f, w0_b, w2_b, a0, a2, g2, k_k, k_a,
                                                  r_k.reshape(-1), seq=s)
    r3 = lambda u: u.reshape(bsz, s, D_WIDTH)
    yf, yb = rwkv_chunked(r3(r), r3(k), r3(v), r3(a), r3(b), r3(lwf), r3(lwb))
    return rwkv_post(yf.reshape(-1, D_WIDTH), yb.reshape(-1, D_WIDTH), bonus, g, ln_g, ln_b)


MOE_TILE = 1024
MOE_SUB = 288
ROW_ALIGN = 16
MOE_PACK = 2


def _route(logit):
    lane = lax.broadcasted_iota(jnp.int32, logit.shape, 1)
    first_at = lambda mask: jnp.min(jnp.where(mask, lane, jnp.int32(LANE)), axis=-1, keepdims=True)
    is_grp = lane < N_GROUPS
    gl = jnp.where(is_grp, logit, NEG)
    gmax = jnp.max(gl, axis=-1, keepdims=True)
    p_grp = 1.0 / jnp.sum(jnp.where(is_grp, jnp.exp(gl - gmax), 0.0), axis=-1, keepdims=True)
    i_grp = first_at(is_grp & (gl == gmax))
    lo = N_GROUPS + i_grp * EXPERTS_PER_GROUP
    in_grp = (lane >= lo) & (lane < lo + EXPERTS_PER_GROUP)
    el = jnp.where(in_grp, logit, NEG)
    l1 = jnp.max(el, axis=-1, keepdims=True)
    i1 = first_at(in_grp & (el == l1))
    rest = in_grp & (lane != i1)
    el2 = jnp.where(rest, logit, NEG)
    l2 = jnp.max(el2, axis=-1, keepdims=True)
    i2 = first_at(rest & (el2 == l2))
    e2 = jnp.exp(l2 - l1)
    w1 = p_grp / (1.0 + e2)
    w2 = p_grp * e2 / (1.0 + e2)
    return i_grp, jnp.where(lane == i1, w1, jnp.where(lane == i2, w2, 0.0))


def _moe_sort_kernel(x_ref, g_ref, wr_ref, br_ref, hn_ref, comb_ref, pos_ref, off_ref):
    x = x_ref[...]
    tm = x.shape[0]
    hn = (x * lax.rsqrt(jnp.mean(x * x, axis=-1, keepdims=True) + EPS) * g_ref[...]).astype(BF16)
    wr, br = wr_ref[...], br_ref[...]
    i_grp, _ = _route(jnp.dot(hn, wr, preferred_element_type=F32) + br)
    lane = lax.broadcasted_iota(jnp.int32, (tm, LANE), 1)
    onehot = (lane == i_grp).astype(F32)
    ri = lax.broadcasted_iota(jnp.int32, (tm, tm), 0)
    ci = lax.broadcasted_iota(jnp.int32, (tm, tm), 1)
    earlier = jnp.dot((ci < ri).astype(BF16), onehot.astype(BF16), preferred_element_type=F32)
    cnt = jnp.broadcast_to(jnp.sum(onehot, axis=0, keepdims=True), (8, LANE))
    lane8 = lax.broadcasted_iota(jnp.int32, (8, LANE), 1)
    start = jnp.zeros((8, LANE), F32)
    for sh in range(1, N_GROUPS):
        start += jnp.where(lane8 >= sh, pltpu.roll(cnt, sh, 1), 0.0)
    off_ref[...] = start[0:1].astype(jnp.int32)
    pos = jnp.sum(onehot * (start[0:1] + earlier), axis=-1, keepdims=True).astype(jnp.int32)
    pos_ref[...] = pos
    perm_t = (ci == pos).astype(BF16)
    hn_s = lax.dot_general(perm_t, hn, TN_DIMS, preferred_element_type=F32).astype(BF16)
    hn_ref[...] = hn_s
    _, comb = _route(jnp.dot(hn_s, wr, preferred_element_type=F32) + br)
    comb_ref[...] = comb


def moe_sort(x, g, w_grp, b_grp, w_exp, b_exp):
    t, d = x.shape
    tm = MOE_TILE
    assert t % (MOE_TILE * MOE_PACK) == 0, t
    nr = N_GROUPS + N_EXPERTS
    wr = jnp.pad(jnp.concatenate([w_grp, w_exp], axis=1), ((0, 0), (0, LANE - nr))).astype(BF16)
    br = jnp.pad(jnp.concatenate([b_grp, b_exp]), (0, LANE - nr)).reshape(1, LANE)
    row = lambda i: (i, 0)
    hn, comb, pos, off = pl.pallas_call(
        _moe_sort_kernel,
        out_shape=(jax.ShapeDtypeStruct((t, d), BF16), jax.ShapeDtypeStruct((t, LANE), F32),
                   jax.ShapeDtypeStruct((t, 1), jnp.int32), jax.ShapeDtypeStruct((t // tm, 1, LANE), jnp.int32)),
        grid=(t // tm,),
        in_specs=[pl.BlockSpec((tm, d), row), pl.BlockSpec((1, d), lambda i: (0, 0)),
                  pl.BlockSpec((d, LANE), lambda i: (0, 0)), pl.BlockSpec((1, LANE), lambda i: (0, 0))],
        out_specs=(pl.BlockSpec((tm, d), row), pl.BlockSpec((tm, LANE), row), pl.BlockSpec((tm, 1), row),
                   pl.BlockSpec((None, 1, LANE), lambda i: (i, 0, 0))),
        compiler_params=_params("parallel"),
        name="moe_sort",
    )(x, g.reshape(1, d), wr, br)
    bounds = jnp.concatenate([off[:, 0, :N_GROUPS], jnp.full((t // tm, 1), tm, jnp.int32)], axis=1)
    return hn, comb, pos, bounds.reshape(-1)


def _moe_group_kernel(bounds_ref, hn_ref, c_ref, wg_ref, wu_ref, wd_ref, y_ref):
    i, g, j = pl.program_id(0), pl.program_id(1), pl.program_id(2)

    @pl.when((g == 0) & (j == 0))
    def _():
        y_ref[...] = jnp.zeros_like(y_ref)

    expert_lane = N_GROUPS + g * EXPERTS_PER_GROUP + j

    def sub_tile(want, size):
        r0 = pl.multiple_of(jnp.minimum(want, lo + MOE_TILE - size), ROW_ALIGN)
        rows = pl.ds(r0, size)
        x = hn_ref[rows, :]
        hg = jnp.dot(x, wg_ref[...], preferred_element_type=F32)
        hu = jnp.dot(x, wu_ref[...], preferred_element_type=F32)
        lane = lax.broadcasted_iota(jnp.int32, (size, LANE), 1)
        c = jnp.sum(jnp.where(lane == expert_lane, c_ref[rows, :], 0.0), axis=-1, keepdims=True)
        row = lax.broadcasted_iota(jnp.int32, (size, 1), 0)
        c = jnp.where(row + r0 >= want, c, 0.0)
        hid = (hg * jax.nn.sigmoid(hg)) * hu * c
        y_ref[rows, :] += jnp.dot(hid.astype(BF16), wd_ref[...], preferred_element_type=F32)

    for half in range(MOE_PACK):
        base = (i * MOE_PACK + half) * (N_GROUPS + 1) + g
        lo = half * MOE_TILE
        start, end = lo + bounds_ref[base], lo + bounds_ref[base + 1]
        first = (start // ROW_ALIGN) * ROW_ALIGN
        n_full = (end - first) // MOE_SUB
        rest = end - first - n_full * MOE_SUB

        def full(k, carry, first=first):
            sub_tile(first + k * MOE_SUB, MOE_SUB)
            return carry

        lax.fori_loop(0, n_full, full, 0)
        tail = first + n_full * MOE_SUB

        @pl.when(rest > MOE_SUB // 2)
        def _():
            sub_tile(tail, MOE_SUB)

        @pl.when((rest > 0) & (rest <= MOE_SUB // 2))
        def _():
            sub_tile(tail, MOE_SUB // 2)


def moe_group_experts(hn, comb, bounds, w_gate, w_up, w_down):
    t, d = hn.shape
    tm = MOE_TILE * MOE_PACK
    ne, _, ff = w_gate.shape
    ex = lambda i, g, j, b: (g * EXPERTS_PER_GROUP + j, 0, 0)
    once = pl.Buffered(1) if MOE_PACK > 1 else None
    return pl.pallas_call(
        _moe_group_kernel,
        out_shape=jax.ShapeDtypeStruct((t, d), F32),
        grid_spec=pltpu.PrefetchScalarGridSpec(
            num_scalar_prefetch=1,
            grid=(t // tm, N_GROUPS, EXPERTS_PER_GROUP),
            in_specs=[pl.BlockSpec((tm, d), lambda i, g, j, b: (i, 0), pipeline_mode=once),
                      pl.BlockSpec((tm, LANE), lambda i, g, j, b: (i, 0)),
                      pl.BlockSpec((None, d, ff), ex), pl.BlockSpec((None, d, ff), ex),
                      pl.BlockSpec((None, ff, d), ex)],
            out_specs=pl.BlockSpec((tm, d), lambda i, g, j, b: (i, 0), pipeline_mode=once)),
        compiler_params=_params("parallel", "arbitrary", "arbitrary"),
        name="moe_group_experts",
    )(bounds, hn, comb, w_gate, w_up, w_down)


def _moe_unsort_kernel(y_ref, pos_ref, x_ref, o_ref):
    tm = y_ref.shape[0]
    perm_t = (lax.broadcasted_iota(jnp.int32, (tm, tm), 1) == pos_ref[...]).astype(BF16)
    y = y_ref[...]
    hi = y.astype(BF16)
    lo = (y - hi.astype(F32)).astype(BF16)
    o_ref[...] = (x_ref[...] + jnp.dot(perm_t, hi, preferred_element_type=F32)
                  + jnp.dot(perm_t, lo, preferred_element_type=F32))


def moe_unsort(y, pos, x):
    t, d = x.shape
    tm, tn = MOE_TILE, d // 2
    blk = pl.BlockSpec((tm, tn), lambda i, j: (i, j))
    return pl.pallas_call(
        _moe_unsort_kernel,
        out_shape=jax.ShapeDtypeStruct((t, d), F32),
        grid=(t // tm, d // tn),
        in_specs=[blk, pl.BlockSpec((tm, 1), lambda i, j: (i, 0)), blk],
        out_specs=blk,
        compiler_params=_params("parallel", "arbitrary"),
        name="moe_unsort",
    )(y, pos, x)


def _moe_unsort_norm_kernel(y_ref, pos_ref, x_ref, g_ref, o_ref):
    tm, ts = pos_ref.shape[0], y_ref.shape[0]
    perm_t = (lax.broadcasted_iota(jnp.int32, (tm, ts), 1) == pos_ref[...]).astype(BF16)
    y = y_ref[...]
    hi = y.astype(BF16)
    lo = (y - hi.astype(F32)).astype(BF16)
    x = (x_ref[...] + jnp.dot(perm_t, hi, preferred_element_type=F32)
         + jnp.dot(perm_t, lo, preferred_element_type=F32))
    o_ref[...] = x * lax.rsqrt(jnp.mean(x * x, axis=-1, keepdims=True) + EPS) * g_ref[...]


def moe_unsort_norm(y, pos, x, g, *, row0, rows, tm=512):
    d = x.shape[1]
    per = MOE_TILE // tm
    r0 = row0 // tm
    return pl.pallas_call(
        _moe_unsort_norm_kernel,
        out_shape=jax.ShapeDtypeStruct((rows, d), F32),
        grid=(rows // tm,),
        in_specs=[pl.BlockSpec((MOE_TILE, d), lambda i: ((i + r0) // per, 0)),
                  pl.BlockSpec((tm, 1), lambda i: (i + r0, 0)),
                  pl.BlockSpec((tm, d), lambda i: (i + r0, 0)),
                  pl.BlockSpec((1, d), lambda i: (0, 0))],
        out_specs=pl.BlockSpec((tm, d), lambda i: (i, 0)),
        compiler_params=_params("parallel"),
        name="moe_unsort_norm",
    )(y, pos, x, g.reshape(1, d))


def hier_moe(x, g, w_grp, b_grp, w_exp, b_exp, w_gate, w_up, w_down, *, final=None):
    hn, comb, pos, bounds = moe_sort(x, g, w_grp, b_grp, w_exp, b_exp)
    y = moe_group_experts(hn, comb, bounds, w_gate.astype(BF16), w_up.astype(BF16), w_down.astype(BF16))
    if final is None:
        return moe_unsort(y, pos, x)
    gain, counts = final
    starts = np.cumsum((0,) + tuple(counts))[:-1]
    return tuple(moe_unsort_norm(y, pos, x, gain, row0=int(r0), rows=int(n)) for r0, n in zip(starts, counts))


def even_layer(x, bsz, s, norm_g, rel_bias, w_in, w_out, w2_f, b_f, w2_b, b_b, onorm):
    n_pad = _round_up(EVEN_IN, LANE)
    w_in_p = jnp.pad(w_in, ((0, 0), (0, n_pad - EVEN_IN))).astype(BF16)
    proj = norm_linear(x, norm_g, w_in_p, tn_target=896).reshape(bsz, s, n_pad)
    tq = ATTN_TQ
    ya = attention(proj, proj, proj, heads=A_HEADS, dq=HEAD_DIM, dv=HEAD_DIM,
                   q_off=0, k_off=A_HEADS, v_off=2 * A_HEADS, scale=HEAD_DIM ** -0.5,
                   bias=dilated_bias_table(rel_bias, s, tq), tq=tq)
    q_col = 3 * A_WIDTH
    yb = gla_mixer(proj, w2_f, w2_b, b_f, b_b, onorm, q_col=q_col, k_col=q_col + B_KEYW,
                   v_col=q_col + 2 * B_KEYW, g_col=q_col + 2 * B_KEYW + B_WIDTH,
                   z_col=q_col + 2 * B_KEYW + 2 * B_WIDTH)
    t = bsz * s
    return out_proj(ya.reshape(t, A_WIDTH), yb.reshape(t, B_WIDTH), w_out.astype(BF16), x)


def _odd_columns(w_in, mu):
    c0 = C_IN
    cut = lambda u, lo, n: u[..., lo:lo + n]
    zpad = lambda u, n: jnp.pad(u, [(0, 0)] * (u.ndim - 1) + [(0, n)])
    off = np.cumsum((0,) + D_SPLITS)
    def rwkv_cols(u):
        parts = [cut(u, off[0], 3 * D_WIDTH), cut(u, off[6], D_G_RANK), cut(u, off[3], 2 * D_W_RANK),
                 cut(u, off[5], D_A_RANK)]
        u = jnp.concatenate(parts, axis=-1)
        return zpad(u, DC_PAD - u.shape[-1])
    w_kr = w_in[:, C_Q_RANK + C_KV_RANK:C_IN]
    w_all = jnp.concatenate([rwkv_cols(w_in[:, c0:]), w_in[:, :C_IN], _rot_half_cols(w_kr)], axis=1)
    n_pad = _round_up(w_all.shape[1], 9 * LANE)
    return zpad(w_all, n_pad - w_all.shape[1]).astype(BF16), rwkv_cols(mu)


def odd_layer(x, bsz, s, norm_g, w_in, w_out, q_norm, w_uq, kv_norm, w_ukv, mu, w0_f, w2_f, w0_b, w2_b,
              a0, a2, g2, k_k, k_a, r_k, ln_g, ln_b):
    t = bsz * s
    w_all, mu_cols = _odd_columns(w_in, mu)
    proj = norm_linear(x, norm_g, w_all, tn_target=1152)
    inv = 1.0 / (ROPE_THETA ** (jnp.arange(0, C_ROPE, 2, dtype=F32) / C_ROPE))
    ang = jnp.arange(s, dtype=F32)[:, None] * inv[None, :]
    cos = jnp.pad(jnp.concatenate([jnp.cos(ang)] * 2, axis=1), ((0, 0), (0, LANE - C_ROPE)), constant_values=1.0)
    sin = jnp.pad(jnp.concatenate([jnp.sin(ang)] * 2, axis=1), ((0, 0), (0, LANE - C_ROPE)))
    q, k, v = mla_up(proj, q_norm, w_uq, kv_norm, w_ukv, jnp.tile(cos, (bsz, 1)), jnp.tile(sin, (bsz, 1)),
                     col0=DC_PAD)
    r3 = lambda u: u.reshape(bsz, s, -1)
    yc = attention(r3(q), r3(k), r3(v), heads=C_HEADS, dq=C_QK, dv=C_V, q_off=0, k_off=0, v_off=0,
                   scale=(C_NOPE + C_ROPE) ** -0.5, tq=s)
    yd = rwkv7_mixer(proj, bsz, s, mu_cols, w0_f, w2_f, w0_b, w2_b, a0, a2, g2, k_k, k_a, r_k, ln_g, ln_b)
    return out_proj(yc.reshape(t, C_WIDTH), yd, w_out.astype(BF16), x)


def kernel(x_prompt, x_sample, rel_bias, norm_mix, norm_ffn, norm_final, ev_w_in, ev_w_out, ev_gla_w2_f, ev_gla_b_f, ev_gla_w2_b, ev_gla_b_b, ev_gla_onorm, od_w_in, od_w_out, od_q_norm, od_w_uq, od_kv_norm, od_w_ukv, od_mu, od_w0_f, od_w2_f, od_w0_b, od_w2_b, od_a0, od_a2, od_g2, od_k_k, od_k_a, od_r_k, od_ln_g, od_ln_b, moe_w_grp, moe_b_grp, moe_w_exp, moe_b_exp, moe_w_gate, moe_w_up, moe_w_down):
    nb_p = x_prompt.shape[0]
    x = jnp.concatenate([x_prompt, x_sample], axis=0)
    bsz, s, d = x.shape
    x = x.reshape(bsz * s, d)
    for i in range(DEPTH):
        j = i // 2
        if i % 2 == 0:
            x = even_layer(x, bsz, s, norm_mix[i], rel_bias, ev_w_in[j], ev_w_out[j], ev_gla_w2_f[j],
                           ev_gla_b_f[j], ev_gla_w2_b[j], ev_gla_b_b[j], ev_gla_onorm[j])
        else:
            x = odd_layer(x, bsz, s, norm_mix[i], od_w_in[j], od_w_out[j], od_q_norm[j], od_w_uq[j],
                          od_kv_norm[j], od_w_ukv[j], od_mu[j], od_w0_f[j], od_w2_f[j], od_w0_b[j],
                          od_w2_b[j], od_a0[j], od_a2[j], od_g2[j], od_k_k[j], od_k_a[j], od_r_k[j],
                          od_ln_g[j], od_ln_b[j])
        final = (norm_final, (nb_p * s, (bsz - nb_p) * s)) if i == DEPTH - 1 else None
        x = hier_moe(x, norm_ffn[i], moe_w_grp[i], moe_b_grp[i], moe_w_exp[i], moe_b_exp[i],
                     moe_w_gate[i], moe_w_up[i], moe_w_down[i], final=final)
    y_p, y_s = x
    return (y_p.reshape(nb_p, s, d), y_s.reshape(bsz - nb_p, s, d))
```

```python
import functools

import jax, jax.numpy as jnp
from jax import lax
import numpy as np
from jax.experimental import pallas as pl
from jax.experimental.pallas import tpu as pltpu

F32, BF16 = jnp.float32, jnp.bfloat16

D_MODEL = 2048
DEPTH = 2
MIX_HALF = D_MODEL // 2
HEAD_DIM = 128
EPS = 1e-6
NEG = -1e30

A_HEADS = MIX_HALF // HEAD_DIM
A_WIDTH = A_HEADS * HEAD_DIM
A_PATTERNS = ((128, 1), (512, 4), (2048, 16))
N_BUCKETS = 32
MAX_DISTANCE = 1024

B_HEADS = 4
B_DV = MIX_HALF // B_HEADS
B_DK = B_DV // 2
B_WIDTH = B_HEADS * B_DV
B_KEYW = B_HEADS * B_DK
B_GATE_RANK = 16
B_GATE_TAU = 16.0
B_CHUNK = 64

C_HEADS = MIX_HALF // 128
C_Q_RANK = 512
C_KV_RANK = 256
C_NOPE = 128
C_ROPE = 64
C_V = 128
C_WIDTH = C_HEADS * C_V
C_QK = 256
ROPE_THETA = 10000.0

D_HEAD = 64
D_HEADS = MIX_HALF // D_HEAD
D_WIDTH = D_HEADS * D_HEAD
D_W_RANK = 64
D_A_RANK = 64
D_G_RANK = 128
D_LN_EPS = 64e-5
D_SPLITS = (D_WIDTH, D_WIDTH, D_WIDTH, D_W_RANK, D_W_RANK, D_A_RANK, D_G_RANK)
D_SHIFT = 3 * D_WIDTH + 2 * D_W_RANK + D_A_RANK + D_G_RANK

N_GROUPS = 4
EXPERTS_PER_GROUP = 4
N_EXPERTS = N_GROUPS * EXPERTS_PER_GROUP

EVEN_IN = 3 * A_WIDTH + 2 * B_KEYW + 2 * B_WIDTH + 2 * B_GATE_RANK
C_IN = C_Q_RANK + C_KV_RANK + C_ROPE

LANE = 128
VMEM_LIMIT = 52 * 1024 * 1024


def _params(*sem):
    return pltpu.CompilerParams(dimension_semantics=sem, vmem_limit_bytes=VMEM_LIMIT)


def _round_up(n, m):
    return -(-n // m) * m


NT_DIMS = (((1,), (1,)), ((), ()))
TN_DIMS = (((0,), (0,)), ((), ()))


def _split3(x):
    hi = x.astype(BF16)
    r1 = x - hi.astype(F32)
    mid = r1.astype(BF16)
    lo = (r1 - mid.astype(F32)).astype(BF16)
    return hi, mid, lo


def _pick_tile(n, target):
    best = LANE
    for t in range(LANE, target + 1, LANE):
        if n % t == 0:
            best = t
    return best


def _norm_linear_kernel(x_ref, g_ref, w_ref, o_ref, xn_ref):
    @pl.when(pl.program_id(1) == 0)
    def _():
        x = x_ref[...]
        y = x * lax.rsqrt(jnp.mean(x * x, axis=-1, keepdims=True) + EPS) * g_ref[...]
        xn_ref[...] = y.astype(BF16)

    o_ref[...] = jnp.dot(xn_ref[...], w_ref[...], preferred_element_type=F32)


def norm_linear(x, g, w, *, tm=1024, tn_target=1024):
    t, k = x.shape
    n = w.shape[1]
    tn = _pick_tile(n, tn_target)
    return pl.pallas_call(
        _norm_linear_kernel,
        out_shape=jax.ShapeDtypeStruct((t, n), F32),
        grid=(t // tm, n // tn),
        in_specs=[pl.BlockSpec((tm, k), lambda i, j: (i, 0)),
                  pl.BlockSpec((1, k), lambda i, j: (0, 0)),
                  pl.BlockSpec((k, tn), lambda i, j: (0, j))],
        out_specs=pl.BlockSpec((tm, tn), lambda i, j: (i, j)),
        scratch_shapes=[pltpu.VMEM((tm, k), BF16)],
        compiler_params=_params("parallel", "arbitrary"),
        name="norm_linear",
    )(x, g.reshape(1, k), w)


def _out_proj_kernel(a_ref, b_ref, wa_ref, wb_ref, x_ref, o_ref):
    acc = jnp.dot(a_ref[...], wa_ref[...], preferred_element_type=F32)
    acc += jnp.dot(b_ref[...], wb_ref[...], preferred_element_type=F32)
    o_ref[...] = x_ref[...] + acc


def out_proj(a, b, w, x, *, tm=1024, tn=1024):
    t, ka = a.shape
    kb = b.shape[1]
    n = w.shape[1]
    return pl.pallas_call(
        _out_proj_kernel,
        out_shape=jax.ShapeDtypeStruct((t, n), F32),
        grid=(t // tm, n // tn),
        in_specs=[pl.BlockSpec((tm, ka), lambda i, j: (i, 0)),
                  pl.BlockSpec((tm, kb), lambda i, j: (i, 0)),
                  pl.BlockSpec((ka, tn), lambda i, j: (0, j)),
                  pl.BlockSpec((kb, tn), lambda i, j: (0, j)),
                  pl.BlockSpec((tm, tn), lambda i, j: (i, j))],
        out_specs=pl.BlockSpec((tm, tn), lambda i, j: (i, j)),
        compiler_params=_params("parallel", "arbitrary"),
        name="out_proj",
    )(a, b, w[:ka], w[ka:], x)


LOG2E = float(np.log2(np.e))
ATTN_KB = 256
ATTN_TQ = 1024


def _attn_kernel(*refs, scale, has_bias):
    if has_bias:
        q_ref, k_ref, v_ref, bias_ref, o_ref, kb_ref, vb_ref = refs
    else:
        q_ref, k_ref, v_ref, o_ref, kb_ref, vb_ref = refs
    s_len, dv = v_ref.shape

    @pl.when(pl.program_id(2) == 0)
    def _():
        kb_ref[...] = k_ref[...].astype(BF16)
        vb_ref[...] = jnp.concatenate([v_ref[...].astype(BF16), jnp.ones((s_len, LANE), BF16)], axis=1)

    q = (q_ref[...] * (scale * LOG2E)).astype(BF16)
    blocks = [slice(j * ATTN_KB, (j + 1) * ATTN_KB) for j in range(s_len // ATTN_KB)]
    scores = []
    m = None
    for blk in blocks:
        sj = lax.dot_general(q, kb_ref[blk, :], NT_DIMS, preferred_element_type=F32)
        if has_bias:
            sj = sj + bias_ref[:, blk]
        mj = jnp.max(sj, axis=-1, keepdims=True)
        m = mj if m is None else jnp.maximum(m, mj)
        scores.append(sj)
    o = None
    for blk, sj in zip(blocks, scores):
        oj = jnp.dot(jnp.exp2(sj - m).astype(BF16), vb_ref[blk, :], preferred_element_type=F32)
        o = oj if o is None else o + oj
    den = o[:, dv:]
    if dv > LANE:
        den = jnp.concatenate([den] * (dv // LANE), axis=1)
    o_ref[...] = (o[:, :dv] / den).astype(o_ref.dtype)


def attention(q, k, v, *, heads, dq, dv, q_off, k_off, v_off, scale, bias=None, tq=ATTN_TQ):
    b, s, _ = q.shape
    nq = s // tq
    assert s % tq == 0 and s % ATTN_KB == 0, (s, tq)
    in_specs = [pl.BlockSpec((None, tq, dq), lambda bi, h, qi: (bi, qi, q_off + h)),
                pl.BlockSpec((None, s, dq), lambda bi, h, qi: (bi, 0, k_off + h)),
                pl.BlockSpec((None, s, dv), lambda bi, h, qi: (bi, 0, v_off + h))]
    args = [q, k, v]
    if bias is not None:
        in_specs.append(pl.BlockSpec((None, None, tq, s), lambda bi, h, qi: (h, qi, 0, 0)))
        args.append(bias)
    return pl.pallas_call(
        functools.partial(_attn_kernel, scale=scale, has_bias=bias is not None),
        out_shape=jax.ShapeDtypeStruct((b, s, heads * dv), BF16),
        grid=(b, heads, nq),
        in_specs=in_specs,
        out_specs=pl.BlockSpec((None, tq, dv), lambda bi, h, qi: (bi, qi, h)),
        scratch_shapes=[pltpu.VMEM((s, dq), BF16), pltpu.VMEM((s, dv + LANE), BF16)],
        compiler_params=_params("parallel", "parallel", "arbitrary"),
        name="attention_bias" if bias is not None else "attention",
    )(*args)


def _t5_bucket(rel):
    half = N_BUCKETS // 2
    exact = half // 2
    n = np.abs(rel)
    large = exact + (np.log(np.maximum(n, 1) / exact) / np.log(MAX_DISTANCE / exact) * (half - exact)).astype(np.int64)
    large = np.minimum(large, half - 1)
    return ((rel > 0) * half + np.where(n < exact, n, large)).astype(np.int32)


def dilated_bias_table(rel_bias, s, tq):
    heads = rel_bias.shape[1]
    d = np.arange(-(s - 1), s)
    count = np.zeros(d.shape, np.float32)
    for window, dil in A_PATTERNS:
        count += ((d % dil == 0) & (np.abs(d) <= (window // (2 * dil)) * dil)).astype(np.float32)
    logc = np.where(count > 0, np.log(np.maximum(count, 1.0)), NEG).astype(np.float32)
    onehot = (_t5_bucket(d)[:, None] == np.arange(N_BUCKETS)[None, :]).astype(np.float32)
    line = jnp.transpose(jnp.dot(onehot, rel_bias.astype(F32), precision=lax.Precision.HIGHEST)) + logc[None]
    line = line * LOG2E
    width = 2 * s
    line = jnp.pad(line, ((0, 0), (0, width - line.shape[1])))[:, None, :]
    nq = s // tq
    return pl.pallas_call(
        functools.partial(_skew_kernel, tq=tq, nq=nq),
        out_shape=jax.ShapeDtypeStruct((heads, nq, tq, s), F32),
        grid=(heads, nq),
        in_specs=[pl.BlockSpec((None, 1, width), lambda h, qi: (h, 0, 0))],
        out_specs=pl.BlockSpec((None, None, tq, s), lambda h, qi: (h, qi, 0, 0)),
        compiler_params=_params("parallel", "arbitrary"),
        name="bias_skew",
    )(line)


def _skew_kernel(line_ref, o_ref, *, tq, nq):
    width = line_ref.shape[1]
    first = (nq - 1 - pl.program_id(1)) * tq
    x = jnp.broadcast_to(line_ref[...], (tq, width))
    x = pltpu.roll(x, width - (tq - 1) - first, 1, stride=1, stride_axis=0)
    o_ref[...] = x[:, :o_ref.shape[1]]


GLA_UNROLL = 16


def _gla_kernel(q_ref, k_ref, v_ref, g_ref, z_ref, w2f_ref, w2b_ref, bf_ref, bb_ref, on_ref, o_ref,
                la_ref, acc_ref, qcat_ref, upd_ref, dec_ref, scat_ref, st_ref):
    s_len = q_ref.shape[0]
    c = B_CHUNK
    nchunk = s_len // c
    z = z_ref[...].astype(BF16)
    gate = lambda w2_ref, b_ref: jax.nn.log_sigmoid(
        jnp.dot(z, w2_ref[...], preferred_element_type=F32) + b_ref[...]) * (1.0 / B_GATE_TAU)
    la_ref[0] = gate(w2f_ref, bf_ref)
    la_ref[1] = gate(w2b_ref, bb_ref)

    ri = lax.broadcasted_iota(jnp.int32, (c, c), 0)
    ci = lax.broadcasted_iota(jnp.int32, (c, c), 1)
    keep = (ri >= ci, ri <= ci)
    tri3 = tuple(jnp.concatenate([kp.astype(BF16)] * 3, axis=1) for kp in keep)

    def chunk_rows(n):
        return pl.ds(pl.multiple_of(n * c, c), c)

    def pass1(i, carry):
        units = [(i * GLA_UNROLL + u, d) for u in range(GLA_UNROLL) for d in range(2)]
        st = []
        for n, d in units:
            rows = chunk_rows(n)
            gcum = jnp.dot(tri3[d], jnp.concatenate(_split3(la_ref[d, rows, :]), axis=0),
                           preferred_element_type=F32)
            st.append(dict(rows=rows, gcum=gcum))
        for (n, d), c in zip(units, st):
            gcum = c["gcum"]
            gend = gcum[0:1] if d == 1 else gcum[B_CHUNK - 1:B_CHUNK]
            kc = k_ref[c["rows"], :]
            c["q_in"] = (q_ref[c["rows"], :] * (B_DK ** -0.5) * jnp.exp(gcum)).astype(BF16)
            c["k_out"] = (kc * jnp.exp(gend - gcum)).astype(BF16)
            c["att"] = lax.dot_general(c["q_in"], (kc * jnp.exp(-gcum)).astype(BF16), NT_DIMS,
                                       preferred_element_type=F32)
            dec_ref[d, n] = jnp.broadcast_to(jnp.exp(gend), (8, B_DK))
        for (n, d), c in zip(units, st):
            vc = v_ref[c["rows"], :].astype(BF16)
            c["o"] = jnp.dot(jnp.where(keep[d], c["att"], 0.0).astype(BF16), vc, preferred_element_type=F32)
            upd_ref[d, n] = lax.dot_general(vc, c["k_out"], TN_DIMS, preferred_element_type=F32)
            qcat_ref[c["rows"], d * B_DK:(d + 1) * B_DK] = c["q_in"]
        for u in range(GLA_UNROLL):
            acc_ref[st[2 * u]["rows"], :] = st[2 * u]["o"] + st[2 * u + 1]["o"]
        return carry

    lax.fori_loop(0, nchunk // GLA_UNROLL, pass1, 0)

    st_ref[...] = jnp.zeros_like(st_ref)

    def pass2(n, carry):
        for d, m in ((0, n), (1, nchunk - 1 - n)):
            state = st_ref[d]
            scat_ref[m, :, d * B_DK:(d + 1) * B_DK] = state.astype(BF16)
            st_ref[d] = state * dec_ref[d, m][0:1] + upd_ref[d, m]
        return carry

    lax.fori_loop(0, nchunk, pass2, 0)

    def pass3(i, carry):
        rows = [chunk_rows(i * GLA_UNROLL + u) for u in range(GLA_UNROLL)]
        outs = [acc_ref[r, :] + lax.dot_general(qcat_ref[r, :], scat_ref[i * GLA_UNROLL + u], NT_DIMS,
                                                preferred_element_type=F32) for u, r in enumerate(rows)]
        for r, o in zip(rows, outs):
            o = o * lax.rsqrt(jnp.mean(o * o, axis=-1, keepdims=True) + EPS) * on_ref[...]
            g = g_ref[r, :]
            o_ref[r, :] = (o * (g * jax.nn.sigmoid(g))).astype(o_ref.dtype)
        return carry

    lax.fori_loop(0, nchunk // GLA_UNROLL, pass3, 0)


def gla_mixer(proj, w2f, w2b, b_f, b_b, onorm, *, q_col, k_col, v_col, g_col, z_col):
    b, s, _ = proj.shape
    assert s % (B_CHUNK * GLA_UNROLL) == 0, s
    hm = lambda blk: (lambda bi, h: (bi, 0, blk + h))
    w2f_p = jnp.zeros((LANE, B_KEYW), F32).at[:B_GATE_RANK].set(w2f).astype(BF16)
    w2b_p = jnp.zeros((LANE, B_KEYW), F32).at[B_GATE_RANK:2 * B_GATE_RANK].set(w2b).astype(BF16)
    return pl.pallas_call(
        _gla_kernel,
        out_shape=jax.ShapeDtypeStruct((b, s, B_WIDTH), BF16),
        grid=(b, B_HEADS),
        in_specs=[pl.BlockSpec((None, s, B_DK), hm(q_col // B_DK)),
                  pl.BlockSpec((None, s, B_DK), hm(k_col // B_DK)),
                  pl.BlockSpec((None, s, B_DV), hm(v_col // B_DV)),
                  pl.BlockSpec((None, s, B_DV), hm(g_col // B_DV)),
                  pl.BlockSpec((None, s, LANE), lambda bi, h: (bi, 0, z_col // LANE)),
                  pl.BlockSpec((LANE, B_DK), lambda bi, h: (0, h)),
                  pl.BlockSpec((LANE, B_DK), lambda bi, h: (0, h)),
                  pl.BlockSpec((1, B_DK), lambda bi, h: (0, h)),
                  pl.BlockSpec((1, B_DK), lambda bi, h: (0, h)),
                  pl.BlockSpec((1, B_DV), lambda bi, h: (0, 0))],
        out_specs=pl.BlockSpec((None, s, B_DV), lambda bi, h: (bi, 0, h)),
        scratch_shapes=[pltpu.VMEM((2, s, B_DK), F32),
                        pltpu.VMEM((s, B_DV), F32),
                        pltpu.VMEM((s, 2 * B_DK), BF16),
                        pltpu.VMEM((2, s // B_CHUNK, B_DV, B_DK), F32),
                        pltpu.VMEM((2, s // B_CHUNK, 8, B_DK), F32),
                        pltpu.VMEM((s // B_CHUNK, B_DV, 2 * B_DK), BF16),
                        pltpu.VMEM((2, B_DV, B_DK), F32)],
        compiler_params=_params("parallel", "arbitrary"),
        name="gla_mixer",
    )(proj, proj, proj, proj, proj, w2f_p, w2b_p, b_f.reshape(1, -1), b_b.reshape(1, -1), onorm.reshape(1, -1))


def _mla_up_kernel(cq_ref, ckv_ref, kr_ref, qn_ref, kvn_ref, wq_ref, wqr_ref, wkv_ref, cos_ref, sin_ref,
                   q_ref, k_ref, v_ref):
    def rms(x, g):
        return (x * lax.rsqrt(jnp.mean(x * x, axis=-1, keepdims=True) + EPS) * g).astype(BF16)

    cq = rms(cq_ref[...], qn_ref[...])
    ckv = rms(ckv_ref[...], kvn_ref[...])
    cos, sin = cos_ref[...], sin_ref[...]
    kr = kr_ref[...]
    k_rope = kr * cos + pltpu.roll(kr, LANE - C_ROPE, 1) * sin
    lane = lax.broadcasted_iota(jnp.int32, k_rope.shape, 1)
    k_rope = jnp.where(lane < C_ROPE, k_rope, 0.0)
    for h in range(C_HEADS):
        q = jnp.dot(cq, wq_ref[:, h * C_QK:(h + 1) * C_QK], preferred_element_type=F32)
        qp = jnp.dot(cq, wqr_ref[:, h * LANE:(h + 1) * LANE], preferred_element_type=F32)
        q_ref[:, h * C_QK:h * C_QK + C_NOPE] = q[:, :C_NOPE]
        q_ref[:, h * C_QK + C_NOPE:(h + 1) * C_QK] = q[:, C_NOPE:] * cos + qp * sin
        kv = jnp.dot(ckv, wkv_ref[:, h * 2 * LANE:(h + 1) * 2 * LANE], preferred_element_type=F32)
        k_ref[:, h * C_QK:h * C_QK + C_NOPE] = kv[:, :C_NOPE]
        k_ref[:, h * C_QK + C_NOPE:(h + 1) * C_QK] = k_rope
        v_ref[:, h * C_V:(h + 1) * C_V] = kv[:, C_NOPE:]


def _rot_half_cols(w):
    half = w.shape[-1] // 2
    return jnp.concatenate([-w[..., half:], w[..., :half]], axis=-1)


def mla_up(proj, q_norm, w_uq, kv_norm, w_ukv, cos, sin, *, col0, tm=512):
    t = proj.shape[0]
    wq = w_uq.reshape(C_Q_RANK, C_HEADS, C_NOPE + C_ROPE)
    wq_main = jnp.pad(wq, ((0, 0), (0, 0), (0, C_QK - C_NOPE - C_ROPE))).reshape(C_Q_RANK, C_HEADS * C_QK)
    wq_rot = jnp.pad(_rot_half_cols(wq[..., C_NOPE:]), ((0, 0), (0, 0), (0, LANE - C_ROPE)))
    wq_rot = wq_rot.reshape(C_Q_RANK, C_HEADS * LANE)
    row = lambda i: (i, 0)
    full = lambda arr: pl.BlockSpec(arr.shape, lambda i: (0, 0))
    g_q, g_kv = q_norm.reshape(1, -1), kv_norm.reshape(1, -1)
    wq_main, wq_rot, wkv = wq_main.astype(BF16), wq_rot.astype(BF16), w_ukv.astype(BF16)
    return pl.pallas_call(
        _mla_up_kernel,
        out_shape=(jax.ShapeDtypeStruct((t, C_HEADS * C_QK), F32),
                   jax.ShapeDtypeStruct((t, C_HEADS * C_QK), F32),
                   jax.ShapeDtypeStruct((t, C_WIDTH), F32)),
        grid=(t // tm,),
        in_specs=[pl.BlockSpec((tm, C_Q_RANK), lambda i: (i, col0 // C_Q_RANK)),
                  pl.BlockSpec((tm, C_KV_RANK), lambda i: (i, (col0 + C_Q_RANK) // C_KV_RANK)),
                  pl.BlockSpec((tm, LANE), lambda i: (i, (col0 + C_Q_RANK + C_KV_RANK) // LANE)),
                  full(g_q), full(g_kv), full(wq_main), full(wq_rot), full(wkv),
                  pl.BlockSpec((tm, LANE), row), pl.BlockSpec((tm, LANE), row)],
        out_specs=(pl.BlockSpec((tm, C_HEADS * C_QK), row),
                   pl.BlockSpec((tm, C_HEADS * C_QK), row),
                   pl.BlockSpec((tm, C_WIDTH), row)),
        compiler_params=_params("parallel"),
        name="mla_up",
    )(proj, proj, proj, g_q, g_kv, wq_main, wq_rot, wkv, cos, sin)


RG = 4
RGW = RG * D_HEAD
RCH = 64
DC_R, DC_K, DC_V = 0, D_WIDTH, 2 * D_WIDTH
DC_ZG = 3 * D_WIDTH
DC_ZW = DC_ZG + D_G_RANK
DC_ZA = DC_ZW + 2 * D_W_RANK
DC_PAD = 7 * 512


def _head_sums(x, bo):
    return jnp.concatenate(
        [jnp.dot(x[:, RGW * g:RGW * (g + 1)], bo, preferred_element_type=F32, precision=lax.Precision.HIGHEST)
         for g in range(x.shape[1] // RGW)], axis=1)


def _block_ones():
    i = np.arange(RGW)
    return jnp.asarray((i[:, None] // D_HEAD) == (i[None, :] // D_HEAD), F32)


def _rwkv_prep_kernel(x_ref, xp_ref, xn_ref, mu_ref, w2f_ref, w2b_ref, a2_ref, g2_ref, w0f_ref, w0b_ref,
                      a0_ref, kk_ref, ka_ref, rk_ref, bo_ref,
                      r_ref, k_ref, v_ref, a_ref, b_ref, lwf_ref, lwb_ref, g_ref, bonus_ref, *, tiles_per_seq):
    i = pl.program_id(0) % tiles_per_seq
    x = x_ref[...]
    tm = x.shape[0]
    row = lax.broadcasted_iota(jnp.int32, x.shape, 0)
    prev_row = jnp.where(i == 0, 0.0, xp_ref[7:8, :])
    next_row = jnp.where(i == tiles_per_seq - 1, 0.0, xn_ref[0:1, :])
    prev = jnp.where(row == 0, prev_row, pltpu.roll(x, 1, 0))
    nxt = jnp.where(row == tm - 1, next_row, pltpu.roll(x, tm - 1, 0))
    x = x + mu_ref[...] * (0.5 * (prev + nxt) - x)
    r, k, v = x[:, DC_R:DC_R + D_WIDTH], x[:, DC_K:DC_K + D_WIDTH], x[:, DC_V:DC_V + D_WIDTH]
    zg = x[:, DC_ZG:DC_ZG + LANE]
    zw = x[:, DC_ZW:DC_ZW + LANE]
    za = x[:, DC_ZA:DC_ZA + LANE]
    tz = jnp.tanh(zw).astype(BF16)
    log_decay = lambda w0_ref, w2_ref: -np.exp(-0.5).astype(np.float32) * jax.nn.sigmoid(
        w0_ref[...] + jnp.dot(tz, w2_ref[...], preferred_element_type=F32))
    lwf_ref[...] = log_decay(w0f_ref, w2f_ref)
    lwb_ref[...] = log_decay(w0b_ref, w2b_ref)
    ag = jax.nn.sigmoid(a0_ref[...] + jnp.dot(za.astype(BF16), a2_ref[...], preferred_element_type=F32))
    g_ref[...] = jnp.dot(jax.nn.sigmoid(zg).astype(BF16), g2_ref[...], preferred_element_type=F32)
    bo = bo_ref[...]
    kk = k * kk_ref[...]
    kk = kk / jnp.maximum(jnp.sqrt(_head_sums(kk * kk, bo)), 1e-12)
    k = k * (1.0 + (ag - 1.0) * ka_ref[...])
    r_ref[...] = r
    k_ref[...] = k
    v_ref[...] = v
    a_ref[...] = -kk
    b_ref[...] = kk * ag
    bonus_ref[...] = _head_sums(r * k * rk_ref[...], bo) * v


def rwkv_prep(proj, mu, w0_f, w2_f, w0_b, w2_b, a0, a2, g2, k_k, k_a, r_k, *, seq, tm=256):
    t = proj.shape[0]
    tiles_per_seq = seq // tm
    assert seq % tm == 0 and t % seq == 0, (t, seq)
    hb = tm // 8
    nblk8 = t // 8
    pad_rows = lambda w, lo: jnp.zeros((LANE, D_WIDTH), F32).at[lo:lo + w.shape[0]].set(w).astype(BF16)
    vec = lambda u: u.reshape(1, -1)
    consts = [vec(mu), pad_rows(w2_f, 0), pad_rows(w2_b, D_W_RANK), pad_rows(a2, 0), g2.astype(BF16),
              vec(w0_f), vec(w0_b), vec(a0), vec(k_k), vec(k_a), vec(r_k), _block_ones()]
    full = lambda arr: pl.BlockSpec(arr.shape, lambda i: (0, 0))
    out_spec = pl.BlockSpec((tm, D_WIDTH), lambda i: (i, 0))
    return pl.pallas_call(
        functools.partial(_rwkv_prep_kernel, tiles_per_seq=tiles_per_seq),
        out_shape=tuple(jax.ShapeDtypeStruct((t, D_WIDTH), F32) for _ in range(9)),
        grid=(t // tm,),
        in_specs=[pl.BlockSpec((tm, DC_PAD), lambda i: (i, 0)),
                  pl.BlockSpec((8, DC_PAD), lambda i: (jnp.maximum(i * hb - 1, 0), 0)),
                  pl.BlockSpec((8, DC_PAD), lambda i: (jnp.minimum((i + 1) * hb, nblk8 - 1), 0))]
                 + [full(c) for c in consts],
        out_specs=tuple(out_spec for _ in range(9)),
        compiler_params=_params("parallel"),
        name="rwkv_prep",
    )(proj, proj, proj, *consts)


def _rwkv_chunk_kernel(*refs, ngroups):
    ins, (yf_ref, yb_ref, mt_ref) = refs[:12], refs[12:]

    @pl.when(pl.program_id(1) == 0)
    def _():
        mt_ref[...] = jnp.zeros_like(mt_ref)

    row = lax.broadcasted_iota(jnp.int32, (RCH, RGW), 0)
    col = lax.broadcasted_iota(jnp.int32, (RCH, RGW), 1) & (RCH - 1)
    bdmask = (lax.broadcasted_iota(jnp.int32, (RGW, RGW), 0) // D_HEAD
              == lax.broadcasted_iota(jnp.int32, (RGW, RGW), 1) // D_HEAD)
    tr = lax.broadcasted_iota(jnp.int32, (RCH, RCH), 0)
    tc = lax.broadcasted_iota(jnp.int32, (RCH, RCH), 1)
    zero = jnp.zeros((), F32)

    def bd(z):
        zb = z.astype(BF16)
        return jnp.where(bdmask, jnp.concatenate([zb] * RG, axis=0), jnp.zeros((), BF16))

    def mm(x, y, dims=None):
        x = x.astype(BF16)
        if dims is None:
            return jnp.dot(x, y, preferred_element_type=F32)
        return lax.dot_general(x, y, dims, preferred_element_type=F32)

    chains = [(d, g) for d in range(2) for g in range(ngroups)]
    st = []
    for d, g in chains:
        backward = d == 1
        r_ref, k_ref, v_ref, a_ref, b_ref, lw_ref = ins[6 * d:6 * d + 6]
        tri = ((tc >= tr) if backward else (tc <= tr)).astype(BF16)
        sl = slice(RGW * g, RGW * (g + 1))
        r, k, v, a, b, lw = (ref[:, sl] for ref in (r_ref, k_ref, v_ref, a_ref, b_ref, lw_ref))
        lam = jnp.dot(jnp.concatenate([tri] * 3, axis=1), jnp.concatenate(_split3(lw), axis=0),
                      preferred_element_type=F32)
        lamc = lam[0:1] if backward else lam[RCH - 1:RCH]
        e_inv = jnp.exp(-lam)
        e_out = jnp.exp(lamc - lam)
        ar = jnp.concatenate([a * jnp.exp(lam - lw), r * jnp.exp(lam)], axis=0).astype(BF16)
        bk = jnp.concatenate([b * e_out, k * e_out], axis=0).astype(BF16)
        st.append(dict(ar=ar, bk=bk, v=v, lamc=lamc, sl=sl,
                       gb=mm(ar, bd(b * e_inv), NT_DIMS), gk=mm(ar, bd(k * e_inv), NT_DIMS)))
    for (d, g), c in zip(chains, st):
        strict = (col > row) if d == 1 else (col < row)
        incl = (col >= row) if d == 1 else (col <= row)
        c["lp"] = jnp.where(strict, c["gb"][:RCH], zero)
        c["grb"] = jnp.where(incl, c["gb"][RCH:], zero).astype(BF16)
        akrk = jnp.where(jnp.concatenate([strict, incl], axis=0), c["gk"], zero)
        c["mt"] = mt_ref[d, g]
        amrm = mm(c["ar"], c["mt"].astype(BF16), NT_DIMS) + mm(akrk, bd(c["v"]))
        c["u"] = amrm[:RCH]
        c["rm"] = amrm[RCH:]
    for rnd in range(6):
        for c in st:
            lpb = c["lp"].astype(BF16)
            c["u"] = c["u"] + mm(lpb, bd(c["u"]))
            if rnd < 5:
                c["lp"] = mm(lpb, bd(c["lp"]))
    for (d, g), c in zip(chains, st):
        y_ref = yb_ref if d == 1 else yf_ref
        y_ref[:, c["sl"]] = c["rm"] + mm(c["grb"], bd(c["u"]))
        uv = jnp.concatenate([c["u"], c["v"]], axis=0).astype(BF16)
        upd = lax.dot_general(uv, c["bk"], TN_DIMS, preferred_element_type=F32)
        mt_ref[d, g] = c["mt"] * jnp.exp(c["lamc"]) + jnp.where(bdmask, upd, zero)


def rwkv_chunked(r, k, v, a, b, lwf, lwb):
    bsz, s, wd = r.shape
    nc = s // RCH
    assert s % RCH == 0 and wd % RGW == 0, (s, wd)
    fspec = pl.BlockSpec((None, RCH, wd), lambda bi, n: (bi, n, 0))
    bspec = pl.BlockSpec((None, RCH, wd), lambda bi, n: (bi, nc - 1 - n, 0))
    return pl.pallas_call(
        functools.partial(_rwkv_chunk_kernel, ngroups=wd // RGW),
        out_shape=(jax.ShapeDtypeStruct((bsz, s, wd), F32), jax.ShapeDtypeStruct((bsz, s, wd), F32)),
        grid=(bsz, nc),
        in_specs=[fspec] * 6 + [bspec] * 6,
        out_specs=(fspec, bspec),
        scratch_shapes=[pltpu.VMEM((2, wd // RGW, RGW, RGW), F32)],
        compiler_params=_params("parallel", "arbitrary"),
        name="rwkv_chunked",
    )(r, k, v, a, b, lwf, r, k, v, a, b, lwb)


def _rwkv_post_kernel(yf_ref, yb_ref, bonus_ref, g_ref, lng_ref, lnb_ref, bo_ref, o_ref):
    bo = bo_ref[...]
    y = yf_ref[...] + yb_ref[...]
    yc = y - _head_sums(y, bo) * (1.0 / D_HEAD)
    var = _head_sums(yc * yc, bo) * (1.0 / D_HEAD)
    y = yc * lax.rsqrt(var + D_LN_EPS) * lng_ref[...] + lnb_ref[...]
    o_ref[...] = ((y + bonus_ref[...]) * g_ref[...]).astype(o_ref.dtype)


def rwkv_post(yf, yb, bonus, g, ln_g, ln_b, *, tm=512):
    t, wd = yf.shape
    row = pl.BlockSpec((tm, wd), lambda i: (i, 0))
    vec = pl.BlockSpec((1, wd), lambda i: (0, 0))
    bo = _block_ones()
    return pl.pallas_call(
        _rwkv_post_kernel,
        out_shape=jax.ShapeDtypeStruct((t, wd), BF16),
        grid=(t // tm,),
        in_specs=[row, row, row, row, vec, vec, pl.BlockSpec(bo.shape, lambda i: (0, 0))],
        out_specs=row,
        compiler_params=_params("parallel"),
        name="rwkv_post",
    )(yf, yb, bonus, g, ln_g.reshape(1, wd), ln_b.reshape(1, wd), bo)


def rwkv7_mixer(proj, bsz, s, mu, w0_f, w2_f, w0_b, w2_b, a0, a2, g2, k_k, k_a, r_k, ln_g, ln_b):
    r, k, v, a, b, lwf, lwb, g, bonus = rwkv_prep(proj, mu, w0_f, w2_f, w0_b, w2_b, a0, a2, g2, k_k, k_a,
                                                  r_k.reshape(-1), seq=s)
    r3 = lambda u: u.reshape(bsz, s, D_WIDTH)
    yf, yb = rwkv_chunked(r3(r), r3(k), r3(v), r3(a), r3(b), r3(lwf), r3(lwb))
    return rwkv_post(yf.reshape(-1, D_WIDTH), yb.reshape(-1, D_WIDTH), bonus, g, ln_g, ln_b)


MOE_TILE = 1024
MOE_SUB = 288
ROW_ALIGN = 16
MOE_PACK = 2


def _route(logit):
    lane = lax.broadcasted_iota(jnp.int32, logit.shape, 1)
    first_at = lambda mask: jnp.min(jnp.where(mask, lane, jnp.int32(LANE)), axis=-1, keepdims=True)
    is_grp = lane < N_GROUPS
    gl = jnp.where(is_grp, logit, NEG)
    gmax = jnp.max(gl, axis=-1, keepdims=True)
    p_grp = 1.0 / jnp.sum(jnp.where(is_grp, jnp.exp(gl - gmax), 0.0), axis=-1, keepdims=True)
    i_grp = first_at(is_grp & (gl == gmax))
    lo = N_GROUPS + i_grp * EXPERTS_PER_GROUP
    in_grp = (lane >= lo) & (lane < lo + EXPERTS_PER_GROUP)
    el = jnp.where(in_grp, logit, NEG)
    l1 = jnp.max(el, axis=-1, keepdims=True)
    i1 = first_at(in_grp & (el == l1))
    rest = in_grp & (lane != i1)
    el2 = jnp.where(rest, logit, NEG)
    l2 = jnp.max(el2, axis=-1, keepdims=True)
    i2 = first_at(rest & (el2 == l2))
    e2 = jnp.exp(l2 - l1)
    w1 = p_grp / (1.0 + e2)
    w2 = p_grp * e2 / (1.0 + e2)
    return i_grp, jnp.where(lane == i1, w1, jnp.where(lane == i2, w2, 0.0))


def _moe_sort_kernel(x_ref, g_ref, wr_ref, br_ref, hn_ref, comb_ref, pos_ref, off_ref):
    x = x_ref[...]
    tm = x.shape[0]
    hn = (x * lax.rsqrt(jnp.mean(x * x, axis=-1, keepdims=True) + EPS) * g_ref[...]).astype(BF16)
    wr, br = wr_ref[...], br_ref[...]
    i_grp, _ = _route(jnp.dot(hn, wr, preferred_element_type=F32) + br)
    lane = lax.broadcasted_iota(jnp.int32, (tm, LANE), 1)
    onehot = (lane == i_grp).astype(F32)
    ri = lax.broadcasted_iota(jnp.int32, (tm, tm), 0)
    ci = lax.broadcasted_iota(jnp.int32, (tm, tm), 1)
    earlier = jnp.dot((ci < ri).astype(BF16), onehot.astype(BF16), preferred_element_type=F32)
    cnt = jnp.broadcast_to(jnp.sum(onehot, axis=0, keepdims=True), (8, LANE))
    lane8 = lax.broadcasted_iota(jnp.int32, (8, LANE), 1)
    start = jnp.zeros((8, LANE), F32)
    for sh in range(1, N_GROUPS):
        start += jnp.where(lane8 >= sh, pltpu.roll(cnt, sh, 1), 0.0)
    off_ref[...] = start[0:1].astype(jnp.int32)
    pos = jnp.sum(onehot * (start[0:1] + earlier), axis=-1, keepdims=True).astype(jnp.int32)
    pos_ref[...] = pos
    perm_t = (ci == pos).astype(BF16)
    hn_s = lax.dot_general(perm_t, hn, TN_DIMS, preferred_element_type=F32).astype(BF16)
    hn_ref[...] = hn_s
    _, comb = _route(jnp.dot(hn_s, wr, preferred_element_type=F32) + br)
    comb_ref[...] = comb


def moe_sort(x, g, w_grp, b_grp, w_exp, b_exp):
    t, d = x.shape
    tm = MOE_TILE
    assert t % (MOE_TILE * MOE_PACK) == 0, t
    nr = N_GROUPS + N_EXPERTS
    wr = jnp.pad(jnp.concatenate([w_grp, w_exp], axis=1), ((0, 0), (0, LANE - nr))).astype(BF16)
    br = jnp.pad(jnp.concatenate([b_grp, b_exp]), (0, LANE - nr)).reshape(1, LANE)
    row = lambda i: (i, 0)
    hn, comb, pos, off = pl.pallas_call(
        _moe_sort_kernel,
        out_shape=(jax.ShapeDtypeStruct((t, d), BF16), jax.ShapeDtypeStruct((t, LANE), F32),
                   jax.ShapeDtypeStruct((t, 1), jnp.int32), jax.ShapeDtypeStruct((t // tm, 1, LANE), jnp.int32)),
        grid=(t // tm,),
        in_specs=[pl.BlockSpec((tm, d), row), pl.BlockSpec((1, d), lambda i: (0, 0)),
                  pl.BlockSpec((d, LANE), lambda i: (0, 0)), pl.BlockSpec((1, LANE), lambda i: (0, 0))],
        out_specs=(pl.BlockSpec((tm, d), row), pl.BlockSpec((tm, LANE), row), pl.BlockSpec((tm, 1), row),
                   pl.BlockSpec((None, 1, LANE), lambda i: (i, 0, 0))),
        compiler_params=_params("parallel"),
        name="moe_sort",
    )(x, g.reshape(1, d), wr, br)
    bounds = jnp.concatenate([off[:, 0, :N_GROUPS], jnp.full((t // tm, 1), tm, jnp.int32)], axis=1)
    return hn, comb, pos, bounds.reshape(-1)


def _moe_group_kernel(bounds_ref, hn_ref, c_ref, wg_ref, wu_ref, wd_ref, y_ref):
    i, g, j = pl.program_id(0), pl.program_id(1), pl.program_id(2)

    @pl.when((g == 0) & (j == 0))
    def _():
        y_ref[...] = jnp.zeros_like(y_ref)

    expert_lane = N_GROUPS + g * EXPERTS_PER_GROUP + j

    def sub_tile(want, size):
        r0 = pl.multiple_of(jnp.minimum(want, lo + MOE_TILE - size), ROW_ALIGN)
        rows = pl.ds(r0, size)
        x = hn_ref[rows, :]
        hg = jnp.dot(x, wg_ref[...], preferred_element_type=F32)
        hu = jnp.dot(x, wu_ref[...], preferred_element_type=F32)
        lane = lax.broadcasted_iota(jnp.int32, (size, LANE), 1)
        c = jnp.sum(jnp.where(lane == expert_lane, c_ref[rows, :], 0.0), axis=-1, keepdims=True)
        row = lax.broadcasted_iota(jnp.int32, (size, 1), 0)
        c = jnp.where(row + r0 >= want, c, 0.0)
        hid = (hg * jax.nn.sigmoid(hg)) * hu * c
        y_ref[rows, :] += jnp.dot(hid.astype(BF16), wd_ref[...], preferred_element_type=F32)

    for half in range(MOE_PACK):
        base = (i * MOE_PACK + half) * (N_GROUPS + 1) + g
        lo = half * MOE_TILE
        start, end = lo + bounds_ref[base], lo + bounds_ref[base + 1]
        first = (start // ROW_ALIGN) * ROW_ALIGN
        n_full = (end - first) // MOE_SUB
        rest = end - first - n_full * MOE_SUB

        def full(k, carry, first=first):
            sub_tile(first + k * MOE_SUB, MOE_SUB)
            return carry

        lax.fori_loop(0, n_full, full, 0)
        tail = first + n_full * MOE_SUB

        @pl.when(rest > MOE_SUB // 2)
        def _():
            sub_tile(tail, MOE_SUB)

        @pl.when((rest > 0) & (rest <= MOE_SUB // 2))
        def _():
            sub_tile(tail, MOE_SUB // 2)


def moe_group_experts(hn, comb, bounds, w_gate, w_up, w_down):
    t, d = hn.shape
    tm = MOE_TILE * MOE_PACK
    ne, _, ff = w_gate.shape
    ex = lambda i, g, j, b: (g * EXPERTS_PER_GROUP + j, 0, 0)
    once = pl.Buffered(1) if MOE_PACK > 1 else None
    return pl.pallas_call(
        _moe_group_kernel,
        out_shape=jax.ShapeDtypeStruct((t, d), F32),
        grid_spec=pltpu.PrefetchScalarGridSpec(
            num_scalar_prefetch=1,
            grid=(t // tm, N_GROUPS, EXPERTS_PER_GROUP),
            in_specs=[pl.BlockSpec((tm, d), lambda i, g, j, b: (i, 0), pipeline_mode=once),
                      pl.BlockSpec((tm, LANE), lambda i, g, j, b: (i, 0)),
                      pl.BlockSpec((None, d, ff), ex), pl.BlockSpec((None, d, ff), ex),
                      pl.BlockSpec((None, ff, d), ex)],
            out_specs=pl.BlockSpec((tm, d), lambda i, g, j, b: (i, 0), pipeline_mode=once)),
        compiler_params=_params("parallel", "arbitrary", "arbitrary"),
        name="moe_group_experts",
    )(bounds, hn, comb, w_gate, w_up, w_down)


def _moe_unsort_kernel(y_ref, pos_ref, x_ref, o_ref):
    tm = y_ref.shape[0]
    perm_t = (lax.broadcasted_iota(jnp.int32, (tm, tm), 1) == pos_ref[...]).astype(BF16)
    y = y_ref[...]
    hi = y.astype(BF16)
    lo = (y - hi.astype(F32)).astype(BF16)
    o_ref[...] = (x_ref[...] + jnp.dot(perm_t, hi, preferred_element_type=F32)
                  + jnp.dot(perm_t, lo, preferred_element_type=F32))


def moe_unsort(y, pos, x):
    t, d = x.shape
    tm, tn = MOE_TILE, d // 2
    blk = pl.BlockSpec((tm, tn), lambda i, j: (i, j))
    return pl.pallas_call(
        _moe_unsort_kernel,
        out_shape=jax.ShapeDtypeStruct((t, d), F32),
        grid=(t // tm, d // tn),
        in_specs=[blk, pl.BlockSpec((tm, 1), lambda i, j: (i, 0)), blk],
        out_specs=blk,
        compiler_params=_params("parallel", "arbitrary"),
        name="moe_unsort",
    )(y, pos, x)


def _moe_unsort_norm_kernel(y_ref, pos_ref, x_ref, g_ref, o_ref):
    tm, ts = pos_ref.shape[0], y_ref.shape[0]
    perm_t = (lax.broadcasted_iota(jnp.int32, (tm, ts), 1) == pos_ref[...]).astype(BF16)
    y = y_ref[...]
    hi = y.astype(BF16)
    lo = (y - hi.astype(F32)).astype(BF16)
    x = (x_ref[...] + jnp.dot(perm_t, hi, preferred_element_type=F32)
         + jnp.dot(perm_t, lo, preferred_element_type=F32))
    o_ref[...] = x * lax.rsqrt(jnp.mean(x * x, axis=-1, keepdims=True) + EPS) * g_ref[...]


def moe_unsort_norm(y, pos, x, g, *, row0, rows, tm=512):
    d = x.shape[1]
    per = MOE_TILE // tm
    r0 = row0 // tm
    return pl.pallas_call(
        _moe_unsort_norm_kernel,
        out_shape=jax.ShapeDtypeStruct((rows, d), F32),
        grid=(rows // tm,),
        in_specs=[pl.BlockSpec((MOE_TILE, d), lambda i: ((i + r0) // per, 0)),
                  pl.BlockSpec((tm, 1), lambda i: (i + r0, 0)),
                  pl.BlockSpec((tm, d), lambda i: (i + r0, 0)),
                  pl.BlockSpec((1, d), lambda i: (0, 0))],
        out_specs=pl.BlockSpec((tm, d), lambda i: (i, 0)),
        compiler_params=_params("parallel"),
        name="moe_unsort_norm",
    )(y, pos, x, g.reshape(1, d))


def hier_moe(x, g, w_grp, b_grp, w_exp, b_exp, w_gate, w_up, w_down, *, final=None):
    hn, comb, pos, bounds = moe_sort(x, g, w_grp, b_grp, w_exp, b_exp)
    y = moe_group_experts(hn, comb, bounds, w_gate.astype(BF16), w_up.astype(BF16), w_down.astype(BF16))
    if final is None:
        return moe_unsort(y, pos, x)
    gain, counts = final
    starts = np.cumsum((0,) + tuple(counts))[:-1]
    return tuple(moe_unsort_norm(y, pos, x, gain, row0=int(r0), rows=int(n)) for r0, n in zip(starts, counts))


def even_layer(x, bsz, s, norm_g, rel_bias, w_in, w_out, w2_f, b_f, w2_b, b_b, onorm):
    n_pad = _round_up(EVEN_IN, LANE)
    w_in_p = jnp.pad(w_in, ((0, 0), (0, n_pad - EVEN_IN))).astype(BF16)
    proj = norm_linear(x, norm_g, w_in_p, tn_target=896).reshape(bsz, s, n_pad)
    tq = ATTN_TQ
    ya = attention(proj, proj, proj, heads=A_HEADS, dq=HEAD_DIM, dv=HEAD_DIM,
                   q_off=0, k_off=A_HEADS, v_off=2 * A_HEADS, scale=HEAD_DIM ** -0.5,
                   bias=dilated_bias_table(rel_bias, s, tq), tq=tq)
    q_col = 3 * A_WIDTH
    yb = gla_mixer(proj, w2_f, w2_b, b_f, b_b, onorm, q_col=q_col, k_col=q_col + B_KEYW,
                   v_col=q_col + 2 * B_KEYW, g_col=q_col + 2 * B_KEYW + B_WIDTH,
                   z_col=q_col + 2 * B_KEYW + 2 * B_WIDTH)
    t = bsz * s
    return out_proj(ya.reshape(t, A_WIDTH), yb.reshape(t, B_WIDTH), w_out.astype(BF16), x)


def _odd_columns(w_in, mu):
    c0 = C_IN
    cut = lambda u, lo, n: u[..., lo:lo + n]
    zpad = lambda u, n: jnp.pad(u, [(0, 0)] * (u.ndim - 1) + [(0, n)])
    off = np.cumsum((0,) + D_SPLITS)
    def rwkv_cols(u):
        parts = [cut(u, off[0], 3 * D_WIDTH), cut(u, off[6], D_G_RANK), cut(u, off[3], 2 * D_W_RANK),
                 cut(u, off[5], D_A_RANK)]
        u = jnp.concatenate(parts, axis=-1)
        return zpad(u, DC_PAD - u.shape[-1])
    w_kr = w_in[:, C_Q_RANK + C_KV_RANK:C_IN]
    w_all = jnp.concatenate([rwkv_cols(w_in[:, c0:]), w_in[:, :C_IN], _rot_half_cols(w_kr)], axis=1)
    n_pad = _round_up(w_all.shape[1], 9 * LANE)
    return zpad(w_all, n_pad - w_all.shape[1]).astype(BF16), rwkv_cols(mu)


def odd_layer(x, bsz, s, norm_g, w_in, w_out, q_norm, w_uq, kv_norm, w_ukv, mu, w0_f, w2_f, w0_b, w2_b,
              a0, a2, g2, k_k, k_a, r_k, ln_g, ln_b):
    t = bsz * s
    w_all, mu_cols = _odd_columns(w_in, mu)
    proj = norm_linear(x, norm_g, w_all, tn_target=1152)
    inv = 1.0 / (ROPE_THETA ** (jnp.arange(0, C_ROPE, 2, dtype=F32) / C_ROPE))
    ang = jnp.arange(s, dtype=F32)[:, None] * inv[None, :]
    cos = jnp.pad(jnp.concatenate([jnp.cos(ang)] * 2, axis=1), ((0, 0), (0, LANE - C_ROPE)), constant_values=1.0)
    sin = jnp.pad(jnp.concatenate([jnp.sin(ang)] * 2, axis=1), ((0, 0), (0, LANE - C_ROPE)))
    q, k, v = mla_up(proj, q_norm, w_uq, kv_norm, w_ukv, jnp.tile(cos, (bsz, 1)), jnp.tile(sin, (bsz, 1)),
                     col0=DC_PAD)
    r3 = lambda u: u.reshape(bsz, s, -1)
    yc = attention(r3(q), r3(k), r3(v), heads=C_HEADS, dq=C_QK, dv=C_V, q_off=0, k_off=0, v_off=0,
                   scale=(C_NOPE + C_ROPE) ** -0.5, tq=s)
    yd = rwkv7_mixer(proj, bsz, s, mu_cols, w0_f, w2_f, w0_b, w2_b, a0, a2, g2, k_k, k_a, r_k, ln_g, ln_b)
    return out_proj(yc.reshape(t, C_WIDTH), yd, w_out.astype(BF16), x)


def kernel(x_prompt, x_sample, rel_bias, norm_mix, norm_ffn, norm_final, ev_w_in, ev_w_out, ev_gla_w2_f, ev_gla_b_f, ev_gla_w2_b, ev_gla_b_b, ev_gla_onorm, od_w_in, od_w_out, od_q_norm, od_w_uq, od_kv_norm, od_w_ukv, od_mu, od_w0_f, od_w2_f, od_w0_b, od_w2_b, od_a0, od_a2, od_g2, od_k_k, od_k_a, od_r_k, od_ln_g, od_ln_b, moe_w_grp, moe_b_grp, moe_w_exp, moe_b_exp, moe_w_gate, moe_w_up, moe_w_down):
    nb_p = x_prompt.shape[0]
    x = jnp.concatenate([x_prompt, x_sample], axis=0)
    bsz, s, d = x.shape
    x = x.reshape(bsz * s, d)
    for i in range(DEPTH):
        j = i // 2
        if i % 2 == 0:
            x = even_layer(x, bsz, s, norm_mix[i], rel_bias, ev_w_in[j], ev_w_out[j], ev_gla_w2_f[j],
                           ev_gla_b_f[j], ev_gla_w2_b[j], ev_gla_b_b[j], ev_gla_onorm[j])
        else:
            x = odd_layer(x, bsz, s, norm_mix[i], od_w_in[j], od_w_out[j], od_q_norm[j], od_w_uq[j],
                          od_kv_norm[j], od_w_ukv[j], od_mu[j], od_w0_f[j], od_w2_f[j], od_w0_b[j],
                          od_w2_b[j], od_a0[j], od_a2[j], od_g2[j], od_k_k[j], od_k_a[j], od_r_k[j],
                          od_ln_g[j], od_ln_b[j])
        final = (norm_final, (nb_p * s, (bsz - nb_p) * s)) if i == DEPTH - 1 else None
        x = hier_moe(x, norm_ffn[i], moe_w_grp[i], moe_b_grp[i], moe_w_exp[i], moe_b_exp[i],
                     moe_w_gate[i], moe_w_up[i], moe_w_down[i], final=final)
    y_p, y_s = x
    return (y_p.reshape(nb_p, s, d), y_s.reshape(bsz - nb_p, s, d))
```

```python
import functools

import jax, jax.numpy as jnp
from jax import lax
import numpy as np
from jax.experimental import pallas as pl
from jax.experimental.pallas import tpu as pltpu

F32, BF16 = jnp.float32, jnp.bfloat16

D_MODEL = 2048
DEPTH = 2
MIX_HALF = D_MODEL // 2
HEAD_DIM = 128
EPS = 1e-6
NEG = -1e30

A_HEADS = MIX_HALF // HEAD_DIM
A_WIDTH = A_HEADS * HEAD_DIM
A_PATTERNS = ((128, 1), (512, 4), (2048, 16))
N_BUCKETS = 32
MAX_DISTANCE = 1024

B_HEADS = 4
B_DV = MIX_HALF // B_HEADS
B_DK = B_DV // 2
B_WIDTH = B_HEADS * B_DV
B_KEYW = B_HEADS * B_DK
B_GATE_RANK = 16
B_GATE_TAU = 16.0
B_CHUNK = 64

C_HEADS = MIX_HALF // 128
C_Q_RANK = 512
C_KV_RANK = 256
C_NOPE = 128
C_ROPE = 64
C_V = 128
C_WIDTH = C_HEADS * C_V
C_QK = 256
ROPE_THETA = 10000.0

D_HEAD = 64
D_HEADS = MIX_HALF // D_HEAD
D_WIDTH = D_HEADS * D_HEAD
D_W_RANK = 64
D_A_RANK = 64
D_G_RANK = 128
D_LN_EPS = 64e-5
D_SPLITS = (D_WIDTH, D_WIDTH, D_WIDTH, D_W_RANK, D_W_RANK, D_A_RANK, D_G_RANK)
D_SHIFT = 3 * D_WIDTH + 2 * D_W_RANK + D_A_RANK + D_G_RANK

N_GROUPS = 4
EXPERTS_PER_GROUP = 4
N_EXPERTS = N_GROUPS * EXPERTS_PER_GROUP

EVEN_IN = 3 * A_WIDTH + 2 * B_KEYW + 2 * B_WIDTH + 2 * B_GATE_RANK
C_IN = C_Q_RANK + C_KV_RANK + C_ROPE

LANE = 128
VMEM_LIMIT = 52 * 1024 * 1024


def _params(*sem):
    return pltpu.CompilerParams(dimension_semantics=sem, vmem_limit_bytes=VMEM_LIMIT)


def _round_up(n, m):
    return -(-n // m) * m


NT_DIMS = (((1,), (1,)), ((), ()))
TN_DIMS = (((0,), (0,)), ((), ()))


def _split3(x):
    hi = x.astype(BF16)
    r1 = x - hi.astype(F32)
    mid = r1.astype(BF16)
    lo = (r1 - mid.astype(F32)).astype(BF16)
    return hi, mid, lo


def _pick_tile(n, target):
    best = LANE
    for t in range(LANE, target + 1, LANE):
        if n % t == 0:
            best = t
    return best


def _norm_linear_kernel(x_ref, g_ref, w_ref, o_ref, xn_ref):
    @pl.when(pl.program_id(1) == 0)
    def _():
        x = x_ref[...]
        y = x * lax.rsqrt(jnp.mean(x * x, axis=-1, keepdims=True) + EPS) * g_ref[...]
        xn_ref[...] = y.astype(BF16)

    o_ref[...] = jnp.dot(xn_ref[...], w_ref[...], preferred_element_type=F32)


def norm_linear(x, g, w, *, tm=1024, tn_target=1024):
    t, k = x.shape
    n = w.shape[1]
    tn = _pick_tile(n, tn_target)
    return pl.pallas_call(
        _norm_linear_kernel,
        out_shape=jax.ShapeDtypeStruct((t, n), F32),
        grid=(t // tm, n // tn),
        in_specs=[pl.BlockSpec((tm, k), lambda i, j: (i, 0)),
                  pl.BlockSpec((1, k), lambda i, j: (0, 0)),
                  pl.BlockSpec((k, tn), lambda i, j: (0, j))],
        out_specs=pl.BlockSpec((tm, tn), lambda i, j: (i, j)),
        scratch_shapes=[pltpu.VMEM((tm, k), BF16)],
        compiler_params=_params("parallel", "arbitrary"),
        name="norm_linear",
    )(x, g.reshape(1, k), w)


def _out_proj_kernel(a_ref, b_ref, wa_ref, wb_ref, x_ref, o_ref):
    acc = jnp.dot(a_ref[...], wa_ref[...], preferred_element_type=F32)
    acc += jnp.dot(b_ref[...], wb_ref[...], preferred_element_type=F32)
    o_ref[...] = x_ref[...] + acc


def out_proj(a, b, w, x, *, tm=1024, tn=1024):
    t, ka = a.shape
    kb = b.shape[1]
    n = w.shape[1]
    return pl.pallas_call(
        _out_proj_kernel,
        out_shape=jax.ShapeDtypeStruct((t, n), F32),
        grid=(t // tm, n // tn),
        in_specs=[pl.BlockSpec((tm, ka), lambda i, j: (i, 0)),
                  pl.BlockSpec((tm, kb), lambda i, j: (i, 0)),
                  pl.BlockSpec((ka, tn), lambda i, j: (0, j)),
                  pl.BlockSpec((kb, tn), lambda i, j: (0, j)),
                  pl.BlockSpec((tm, tn), lambda i, j: (i, j))],
        out_specs=pl.BlockSpec((tm, tn), lambda i, j: (i, j)),
        compiler_params=_params("parallel", "arbitrary"),
        name="out_proj",
    )(a, b, w[:ka], w[ka:], x)


LOG2E = float(np.log2(np.e))
ATTN_KB = 256
ATTN_TQ = 1024


def _attn_kernel(*refs, scale, has_bias):
    if has_bias:
        q_ref, k_ref, v_ref, bias_ref, o_ref, kb_ref, vb_ref = refs
    else:
        q_ref, k_ref, v_ref, o_ref, kb_ref, vb_ref = refs
    s_len, dv = v_ref.shape

    @pl.when(pl.program_id(2) == 0)
    def _():
        kb_ref[...] = k_ref[...].astype(BF16)
        vb_ref[...] = jnp.concatenate([v_ref[...].astype(BF16), jnp.ones((s_len, LANE), BF16)], axis=1)

    q = (q_ref[...] * (scale * LOG2E)).astype(BF16)
    blocks = [slice(j * ATTN_KB, (j + 1) * ATTN_KB) for j in range(s_len // ATTN_KB)]
    scores = []
    m = None
    for blk in blocks:
        sj = lax.dot_general(q, kb_ref[blk, :], NT_DIMS, preferred_element_type=F32)
        if has_bias:
            sj = sj + bias_ref[:, blk]
        mj = jnp.max(sj, axis=-1, keepdims=True)
        m = mj if m is None else jnp.maximum(m, mj)
        scores.append(sj)
    o = None
    for blk, sj in zip(blocks, scores):
        oj = jnp.dot(jnp.exp2(sj - m).astype(BF16), vb_ref[blk, :], preferred_element_type=F32)
        o = oj if o is None else o + oj
    den = o[:, dv:]
    if dv > LANE:
        den = jnp.concatenate([den] * (dv // LANE), axis=1)
    o_ref[...] = (o[:, :dv] / den).astype(o_ref.dtype)


def attention(q, k, v, *, heads, dq, dv, q_off, k_off, v_off, scale, bias=None, tq=ATTN_TQ):
    b, s, _ = q.shape
    nq = s // tq
    assert s % tq == 0 and s % ATTN_KB == 0, (s, tq)
    in_specs = [pl.BlockSpec((None, tq, dq), lambda bi, h, qi: (bi, qi, q_off + h)),
                pl.BlockSpec((None, s, dq), lambda bi, h, qi: (bi, 0, k_off + h)),
                pl.BlockSpec((None, s, dv), lambda bi, h, qi: (bi, 0, v_off + h))]
    args = [q, k, v]
    if bias is not None:
        in_specs.append(pl.BlockSpec((None, None, tq, s), lambda bi, h, qi: (h, qi, 0, 0)))
        args.append(bias)
    return pl.pallas_call(
        functools.partial(_attn_kernel, scale=scale, has_bias=bias is not None),
        out_shape=jax.ShapeDtypeStruct((b, s, heads * dv), BF16),
        grid=(b, heads, nq),
        in_specs=in_specs,
        out_specs=pl.BlockSpec((None, tq, dv), lambda bi, h, qi: (bi, qi, h)),
        scratch_shapes=[pltpu.VMEM((s, dq), BF16), pltpu.VMEM((s, dv + LANE), BF16)],
        compiler_params=_params("parallel", "parallel", "arbitrary"),
        name="attention_bias" if bias is not None else "attention",
    )(*args)


def _t5_bucket(rel):
    half = N_BUCKETS // 2
    exact = half // 2
    n = np.abs(rel)
    large = exact + (np.log(np.maximum(n, 1) / exact) / np.log(MAX_DISTANCE / exact) * (half - exact)).astype(np.int64)
    large = np.minimum(large, half - 1)
    return ((rel > 0) * half + np.where(n < exact, n, large)).astype(np.int32)


def dilated_bias_table(rel_bias, s, tq):
    heads = rel_bias.shape[1]
    d = np.arange(-(s - 1), s)
    count = np.zeros(d.shape, np.float32)
    for window, dil in A_PATTERNS:
        count += ((d % dil == 0) & (np.abs(d) <= (window // (2 * dil)) * dil)).astype(np.float32)
    logc = np.where(count > 0, np.log(np.maximum(count, 1.0)), NEG).astype(np.float32)
    onehot = (_t5_bucket(d)[:, None] == np.arange(N_BUCKETS)[None, :]).astype(np.float32)
    line = jnp.transpose(jnp.dot(onehot, rel_bias.astype(F32), precision=lax.Precision.HIGHEST)) + logc[None]
    line = line * LOG2E
    width = 2 * s
    line = jnp.pad(line, ((0, 0), (0, width - line.shape[1])))[:, None, :]
    nq = s // tq
    return pl.pallas_call(
        functools.partial(_skew_kernel, tq=tq, nq=nq),
        out_shape=jax.ShapeDtypeStruct((heads, nq, tq, s), F32),
        grid=(heads, nq),
        in_specs=[pl.BlockSpec((None, 1, width), lambda h, qi: (h, 0, 0))],
        out_specs=pl.BlockSpec((None, None, tq, s), lambda h, qi: (h, qi, 0, 0)),
        compiler_params=_params("parallel", "arbitrary"),
        name="bias_skew",
    )(line)


def _skew_kernel(line_ref, o_ref, *, tq, nq):
    width = line_ref.shape[1]
    first = (nq - 1 - pl.program_id(1)) * tq
    x = jnp.broadcast_to(line_ref[...], (tq, width))
    x = pltpu.roll(x, width - (tq - 1) - first, 1, stride=1, stride_axis=0)
    o_ref[...] = x[:, :o_ref.shape[1]]


GLA_UNROLL = 16


def _gla_kernel(q_ref, k_ref, v_ref, g_ref, z_ref, w2f_ref, w2b_ref, bf_ref, bb_ref, on_ref, o_ref,
                la_ref, acc_ref, qcat_ref, upd_ref, dec_ref, scat_ref, st_ref):
    s_len = q_ref.shape[0]
    c = B_CHUNK
    nchunk = s_len // c
    z = z_ref[...].astype(BF16)
    gate = lambda w2_ref, b_ref: jax.nn.log_sigmoid(
        jnp.dot(z, w2_ref[...], preferred_element_type=F32) + b_ref[...]) * (1.0 / B_GATE_TAU)
    la_ref[0] = gate(w2f_ref, bf_ref)
    la_ref[1] = gate(w2b_ref, bb_ref)

    ri = lax.broadcasted_iota(jnp.int32, (c, c), 0)
    ci = lax.broadcasted_iota(jnp.int32, (c, c), 1)
    keep = (ri >= ci, ri <= ci)
    tri3 = tuple(jnp.concatenate([kp.astype(BF16)] * 3, axis=1) for kp in keep)

    def chunk_rows(n):
        return pl.ds(pl.multiple_of(n * c, c), c)

    def pass1(i, carry):
        units = [(i * GLA_UNROLL + u, d) for u in range(GLA_UNROLL) for d in range(2)]
        st = []
        for n, d in units:
            rows = chunk_rows(n)
            gcum = jnp.dot(tri3[d], jnp.concatenate(_split3(la_ref[d, rows, :]), axis=0),
                           preferred_element_type=F32)
            st.append(dict(rows=rows, gcum=gcum))
        for (n, d), c in zip(units, st):
            gcum = c["gcum"]
            gend = gcum[0:1] if d == 1 else gcum[B_CHUNK - 1:B_CHUNK]
            kc = k_ref[c["rows"], :]
            c["q_in"] = (q_ref[c["rows"], :] * (B_DK ** -0.5) * jnp.exp(gcum)).astype(BF16)
            c["k_out"] = (kc * jnp.exp(gend - gcum)).astype(BF16)
            c["att"] = lax.dot_general(c["q_in"], (kc * jnp.exp(-gcum)).astype(BF16), NT_DIMS,
                                       preferred_element_type=F32)
            dec_ref[d, n] = jnp.broadcast_to(jnp.exp(gend), (8, B_DK))
        for (n, d), c in zip(units, st):
            vc = v_ref[c["rows"], :].astype(BF16)
            c["o"] = jnp.dot(jnp.where(keep[d], c["att"], 0.0).astype(BF16), vc, preferred_element_type=F32)
            upd_ref[d, n] = lax.dot_general(vc, c["k_out"], TN_DIMS, preferred_element_type=F32)
            qcat_ref[c["rows"], d * B_DK:(d + 1) * B_DK] = c["q_in"]
        for u in range(GLA_UNROLL):
            acc_ref[st[2 * u]["rows"], :] = st[2 * u]["o"] + st[2 * u + 1]["o"]
        return carry

    lax.fori_loop(0, nchunk // GLA_UNROLL, pass1, 0)

    st_ref[...] = jnp.zeros_like(st_ref)

    def pass2(n, carry):
        for d, m in ((0, n), (1, nchunk - 1 - n)):
            state = st_ref[d]
            scat_ref[m, :, d * B_DK:(d + 1) * B_DK] = state.astype(BF16)
            st_ref[d] = state * dec_ref[d, m][0:1] + upd_ref[d, m]
        return carry

    lax.fori_loop(0, nchunk, pass2, 0)

    def pass3(i, carry):
        rows = [chunk_rows(i * GLA_UNROLL + u) for u in range(GLA_UNROLL)]
        outs = [acc_ref[r, :] + lax.dot_general(qcat_ref[r, :], scat_ref[i * GLA_UNROLL + u], NT_DIMS,
                                                preferred_element_type=F32) for u, r in enumerate(rows)]
        for r, o in zip(rows, outs):
            o = o * lax.rsqrt(jnp.mean(o * o, axis=-1, keepdims=True) + EPS) * on_ref[...]
            g = g_ref[r, :]
            o_ref[r, :] = (o * (g * jax.nn.sigmoid(g))).astype(o_ref.dtype)
        return carry

    lax.fori_loop(0, nchunk // GLA_UNROLL, pass3, 0)


def gla_mixer(proj, w2f, w2b, b_f, b_b, onorm, *, q_col, k_col, v_col, g_col, z_col):
    b, s, _ = proj.shape
    assert s % (B_CHUNK * GLA_UNROLL) == 0, s
    hm = lambda blk: (lambda bi, h: (bi, 0, blk + h))
    w2f_p = jnp.zeros((LANE, B_KEYW), F32).at[:B_GATE_RANK].set(w2f).astype(BF16)
    w2b_p = jnp.zeros((LANE, B_KEYW), F32).at[B_GATE_RANK:2 * B_GATE_RANK].set(w2b).astype(BF16)
    return pl.pallas_call(
        _gla_kernel,
        out_shape=jax.ShapeDtypeStruct((b, s, B_WIDTH), BF16),
        grid=(b, B_HEADS),
        in_specs=[pl.BlockSpec((None, s, B_DK), hm(q_col // B_DK)),
                  pl.BlockSpec((None, s, B_DK), hm(k_col // B_DK)),
                  pl.BlockSpec((None, s, B_DV), hm(v_col // B_DV)),
                  pl.BlockSpec((None, s, B_DV), hm(g_col // B_DV)),
                  pl.BlockSpec((None, s, LANE), lambda bi, h: (bi, 0, z_col // LANE)),
                  pl.BlockSpec((LANE, B_DK), lambda bi, h: (0, h)),
                  pl.BlockSpec((LANE, B_DK), lambda bi, h: (0, h)),
                  pl.BlockSpec((1, B_DK), lambda bi, h: (0, h)),
                  pl.BlockSpec((1, B_DK), lambda bi, h: (0, h)),
                  pl.BlockSpec((1, B_DV), lambda bi, h: (0, 0))],
        out_specs=pl.BlockSpec((None, s, B_DV), lambda bi, h: (bi, 0, h)),
        scratch_shapes=[pltpu.VMEM((2, s, B_DK), F32),
                        pltpu.VMEM((s, B_DV), F32),
                        pltpu.VMEM((s, 2 * B_DK), BF16),
                        pltpu.VMEM((2, s // B_CHUNK, B_DV, B_DK), F32),
                        pltpu.VMEM((2, s // B_CHUNK, 8, B_DK), F32),
                        pltpu.VMEM((s // B_CHUNK, B_DV, 2 * B_DK), BF16),
                        pltpu.VMEM((2, B_DV, B_DK), F32)],
        compiler_params=_params("parallel", "arbitrary"),
        name="gla_mixer",
    )(proj, proj, proj, proj, proj, w2f_p, w2b_p, b_f.reshape(1, -1), b_b.reshape(1, -1), onorm.reshape(1, -1))


def _mla_up_kernel(cq_ref, ckv_ref, kr_ref, qn_ref, kvn_ref, wq_ref, wqr_ref, wkv_ref, cos_ref, sin_ref,
                   q_ref, k_ref, v_ref):
    def rms(x, g):
        return (x * lax.rsqrt(jnp.mean(x * x, axis=-1, keepdims=True) + EPS) * g).astype(BF16)

    cq = rms(cq_ref[...], qn_ref[...])
    ckv = rms(ckv_ref[...], kvn_ref[...])
    cos, sin = cos_ref[...], sin_ref[...]
    kr = kr_ref[...]
    k_rope = kr * cos + pltpu.roll(kr, LANE - C_ROPE, 1) * sin
    lane = lax.broadcasted_iota(jnp.int32, k_rope.shape, 1)
    k_rope = jnp.where(lane < C_ROPE, k_rope, 0.0)
    for h in range(C_HEADS):
        q = jnp.dot(cq, wq_ref[:, h * C_QK:(h + 1) * C_QK], preferred_element_type=F32)
        qp = jnp.dot(cq, wqr_ref[:, h * LANE:(h + 1) * LANE], preferred_element_type=F32)
        q_ref[:, h * C_QK:h * C_QK + C_NOPE] = q[:, :C_NOPE]
        q_ref[:, h * C_QK + C_NOPE:(h + 1) * C_QK] = q[:, C_NOPE:] * cos + qp * sin
        kv = jnp.dot(ckv, wkv_ref[:, h * 2 * LANE:(h + 1) * 2 * LANE], preferred_element_type=F32)
        k_ref[:, h * C_QK:h * C_QK + C_NOPE] = kv[:, :C_NOPE]
        k_ref[:, h * C_QK + C_NOPE:(h + 1) * C_QK] = k_rope
        v_ref[:, h * C_V:(h + 1) * C_V] = kv[:, C_NOPE:]


def _rot_half_cols(w):
    half = w.shape[-1] // 2
    return jnp.concatenate([-w[..., half:], w[..., :half]], axis=-1)


def mla_up(proj, q_norm, w_uq, kv_norm, w_ukv, cos, sin, *, col0, tm=512):
    t = proj.shape[0]
    wq = w_uq.reshape(C_Q_RANK, C_HEADS, C_NOPE + C_ROPE)
    wq_main = jnp.pad(wq, ((0, 0), (0, 0), (0, C_QK - C_NOPE - C_ROPE))).reshape(C_Q_RANK, C_HEADS * C_QK)
    wq_rot = jnp.pad(_rot_half_cols(wq[..., C_NOPE:]), ((0, 0), (0, 0), (0, LANE - C_ROPE)))
    wq_rot = wq_rot.reshape(C_Q_RANK, C_HEADS * LANE)
    row = lambda i: (i, 0)
    full = lambda arr: pl.BlockSpec(arr.shape, lambda i: (0, 0))
    g_q, g_kv = q_norm.reshape(1, -1), kv_norm.reshape(1, -1)
    wq_main, wq_rot, wkv = wq_main.astype(BF16), wq_rot.astype(BF16), w_ukv.astype(BF16)
    return pl.pallas_call(
        _mla_up_kernel,
        out_shape=(jax.ShapeDtypeStruct((t, C_HEADS * C_QK), F32),
                   jax.ShapeDtypeStruct((t, C_HEADS * C_QK), F32),
                   jax.ShapeDtypeStruct((t, C_WIDTH), F32)),
        grid=(t // tm,),
        in_specs=[pl.BlockSpec((tm, C_Q_RANK), lambda i: (i, col0 // C_Q_RANK)),
                  pl.BlockSpec((tm, C_KV_RANK), lambda i: (i, (col0 + C_Q_RANK) // C_KV_RANK)),
                  pl.BlockSpec((tm, LANE), lambda i: (i, (col0 + C_Q_RANK + C_KV_RANK) // LANE)),
                  full(g_q), full(g_kv), full(wq_main), full(wq_rot), full(wkv),
                  pl.BlockSpec((tm, LANE), row), pl.BlockSpec((tm, LANE), row)],
        out_specs=(pl.BlockSpec((tm, C_HEADS * C_QK), row),
                   pl.BlockSpec((tm, C_HEADS * C_QK), row),
                   pl.BlockSpec((tm, C_WIDTH), row)),
        compiler_params=_params("parallel"),
        name="mla_up",
    )(proj, proj, proj, g_q, g_kv, wq_main, wq_rot, wkv, cos, sin)


RG = 4
RGW = RG * D_HEAD
RCH = 64
DC_R, DC_K, DC_V = 0, D_WIDTH, 2 * D_WIDTH
DC_ZG = 3 * D_WIDTH
DC_ZW = DC_ZG + D_G_RANK
DC_ZA = DC_ZW + 2 * D_W_RANK
DC_PAD = 7 * 512


def _head_sums(x, bo3):
    return jnp.concatenate(
        [jnp.dot(jnp.concatenate(_split3(x[:, RGW * g:RGW * (g + 1)]), axis=1), bo3, preferred_element_type=F32)
         for g in range(x.shape[1] // RGW)], axis=1)


def _block_ones():
    i = np.arange(RGW)
    bo = ((i[:, None] // D_HEAD) == (i[None, :] // D_HEAD)).astype(np.float32)
    return jnp.asarray(np.concatenate([bo, bo, bo], axis=0), BF16)


def _rwkv_prep_kernel(x_ref, xp_ref, xn_ref, mu_ref, w2f_ref, w2b_ref, a2_ref, g2_ref, w0f_ref, w0b_ref,
                      a0_ref, kk_ref, ka_ref, rk_ref, bo_ref,
                      r_ref, k_ref, v_ref, a_ref, b_ref, lwf_ref, lwb_ref, g_ref, bonus_ref, *, tiles_per_seq):
    i = pl.program_id(0) % tiles_per_seq
    x = x_ref[...]
    tm = x.shape[0]
    row = lax.broadcasted_iota(jnp.int32, x.shape, 0)
    prev_row = jnp.where(i == 0, 0.0, xp_ref[7:8, :])
    next_row = jnp.where(i == tiles_per_seq - 1, 0.0, xn_ref[0:1, :])
    prev = jnp.where(row == 0, prev_row, pltpu.roll(x, 1, 0))
    nxt = jnp.where(row == tm - 1, next_row, pltpu.roll(x, tm - 1, 0))
    x = x + mu_ref[...] * (0.5 * (prev + nxt) - x)
    r, k, v = x[:, DC_R:DC_R + D_WIDTH], x[:, DC_K:DC_K + D_WIDTH], x[:, DC_V:DC_V + D_WIDTH]
    zg = x[:, DC_ZG:DC_ZG + LANE]
    zw = x[:, DC_ZW:DC_ZW + LANE]
    za = x[:, DC_ZA:DC_ZA + LANE]
    tz = jnp.tanh(zw).astype(BF16)
    log_decay = lambda w0_ref, w2_ref: -np.exp(-0.5).astype(np.float32) * jax.nn.sigmoid(
        w0_ref[...] + jnp.dot(tz, w2_ref[...], preferred_element_type=F32))
    lwf_ref[...] = log_decay(w0f_ref, w2f_ref)
    lwb_ref[...] = log_decay(w0b_ref, w2b_ref)
    ag = jax.nn.sigmoid(a0_ref[...] + jnp.dot(za.astype(BF16), a2_ref[...], preferred_element_type=F32))
    g_ref[...] = jnp.dot(jax.nn.sigmoid(zg).astype(BF16), g2_ref[...], preferred_element_type=F32)
    bo = bo_ref[...]
    kk = k * kk_ref[...]
    kk = kk / jnp.maximum(jnp.sqrt(_head_sums(kk * kk, bo)), 1e-12)
    k = k * (1.0 + (ag - 1.0) * ka_ref[...])
    r_ref[...] = r
    k_ref[...] = k
    v_ref[...] = v
    a_ref[...] = -kk
    b_ref[...] = kk * ag
    bonus_ref[...] = _head_sums(r * k * rk_ref[...], bo) * v


def rwkv_prep(proj, mu, w0_f, w2_f, w0_b, w2_b, a0, a2, g2, k_k, k_a, r_k, *, seq, tm=256):
    t = proj.shape[0]
    tiles_per_seq = seq // tm
    assert seq % tm == 0 and t % seq == 0, (t, seq)
    hb = tm // 8
    nblk8 = t // 8
    pad_rows = lambda w, lo: jnp.zeros((LANE, D_WIDTH), F32).at[lo:lo + w.shape[0]].set(w).astype(BF16)
    vec = lambda u: u.reshape(1, -1)
    consts = [vec(mu), pad_rows(w2_f, 0), pad_rows(w2_b, D_W_RANK), pad_rows(a2, 0), g2.astype(BF16),
              vec(w0_f), vec(w0_b), vec(a0), vec(k_k), vec(k_a), vec(r_k), _block_ones()]
    full = lambda arr: pl.BlockSpec(arr.shape, lambda i: (0, 0))
    out_spec = pl.BlockSpec((tm, D_WIDTH), lambda i: (i, 0))
    return pl.pallas_call(
        functools.partial(_rwkv_prep_kernel, tiles_per_seq=tiles_per_seq),
        out_shape=tuple(jax.ShapeDtypeStruct((t, D_WIDTH), F32) for _ in range(9)),
        grid=(t // tm,),
        in_specs=[pl.BlockSpec((tm, DC_PAD), lambda i: (i, 0)),
                  pl.BlockSpec((8, DC_PAD), lambda i: (jnp.maximum(i * hb - 1, 0), 0)),
                  pl.BlockSpec((8, DC_PAD), lambda i: (jnp.minimum((i + 1) * hb, nblk8 - 1), 0))]
                 + [full(c) for c in consts],
        out_specs=tuple(out_spec for _ in range(9)),
        compiler_params=_params("parallel"),
        name="rwkv_prep",
    )(proj, proj, proj, *consts)


def _rwkv_chunk_kernel(*refs, ngroups):
    ins, (yf_ref, yb_ref, mt_ref) = refs[:12], refs[12:]

    @pl.when(pl.program_id(1) == 0)
    def _():
        mt_ref[...] = jnp.zeros_like(mt_ref)

    row = lax.broadcasted_iota(jnp.int32, (RCH, RGW), 0)
    col = lax.broadcasted_iota(jnp.int32, (RCH, RGW), 1) & (RCH - 1)
    bdmask = (lax.broadcasted_iota(jnp.int32, (RGW, RGW), 0) // D_HEAD
              == lax.broadcasted_iota(jnp.int32, (RGW, RGW), 1) // D_HEAD)
    tr = lax.broadcasted_iota(jnp.int32, (RCH, RCH), 0)
    tc = lax.broadcasted_iota(jnp.int32, (RCH, RCH), 1)
    zero = jnp.zeros((), F32)

    def bd(z):
        zb = z.astype(BF16)
        return jnp.where(bdmask, jnp.concatenate([zb] * RG, axis=0), jnp.zeros((), BF16))

    def mm(x, y, dims=None):
        x = x.astype(BF16)
        if dims is None:
            return jnp.dot(x, y, preferred_element_type=F32)
        return lax.dot_general(x, y, dims, preferred_element_type=F32)

    chains = [(d, g) for d in range(2) for g in range(ngroups)]
    st = []
    for d, g in chains:
        backward = d == 1
        r_ref, k_ref, v_ref, a_ref, b_ref, lw_ref = ins[6 * d:6 * d + 6]
        tri = ((tc >= tr) if backward else (tc <= tr)).astype(BF16)
        sl = slice(RGW * g, RGW * (g + 1))
        r, k, v, a, b, lw = (ref[:, sl] for ref in (r_ref, k_ref, v_ref, a_ref, b_ref, lw_ref))
        lam = jnp.dot(jnp.concatenate([tri] * 3, axis=1), jnp.concatenate(_split3(lw), axis=0),
                      preferred_element_type=F32)
        lamc = lam[0:1] if backward else lam[RCH - 1:RCH]
        e_inv = jnp.exp(-lam)
        e_out = jnp.exp(lamc - lam)
        ar = jnp.concatenate([a * jnp.exp(lam - lw), r * jnp.exp(lam)], axis=0).astype(BF16)
        bk = jnp.concatenate([b * e_out, k * e_out], axis=0).astype(BF16)
        st.append(dict(ar=ar, bk=bk, v=v, lamc=lamc, sl=sl,
                       gb=mm(ar, bd(b * e_inv), NT_DIMS), gk=mm(ar, bd(k * e_inv), NT_DIMS)))
    for (d, g), c in zip(chains, st):
        strict = (col > row) if d == 1 else (col < row)
        incl = (col >= row) if d == 1 else (col <= row)
        c["lp"] = jnp.where(strict, c["gb"][:RCH], zero)
        c["grb"] = jnp.where(incl, c["gb"][RCH:], zero).astype(BF16)
        akrk = jnp.where(jnp.concatenate([strict, incl], axis=0), c["gk"], zero)
        c["mt"] = mt_ref[d, g]
        amrm = mm(c["ar"], c["mt"].astype(BF16), NT_DIMS) + mm(akrk, bd(c["v"]))
        c["u"] = amrm[:RCH]
        c["rm"] = amrm[RCH:]
    for rnd in range(6):
        for c in st:
            lpb = c["lp"].astype(BF16)
            c["u"] = c["u"] + mm(lpb, bd(c["u"]))
            if rnd < 5:
                c["lp"] = mm(lpb, bd(c["lp"]))
    for (d, g), c in zip(chains, st):
        y_ref = yb_ref if d == 1 else yf_ref
        y_ref[:, c["sl"]] = c["rm"] + mm(c["grb"], bd(c["u"]))
        uv = jnp.concatenate([c["u"], c["v"]], axis=0).astype(BF16)
        upd = lax.dot_general(uv, c["bk"], TN_DIMS, preferred_element_type=F32)
        mt_ref[d, g] = c["mt"] * jnp.exp(c["lamc"]) + jnp.where(bdmask, upd, zero)


def rwkv_chunked(r, k, v, a, b, lwf, lwb):
    bsz, s, wd = r.shape
    nc = s // RCH
    assert s % RCH == 0 and wd % RGW == 0, (s, wd)
    fspec = pl.BlockSpec((None, RCH, wd), lambda bi, n: (bi, n, 0))
    bspec = pl.BlockSpec((None, RCH, wd), lambda bi, n: (bi, nc - 1 - n, 0))
    return pl.pallas_call(
        functools.partial(_rwkv_chunk_kernel, ngroups=wd // RGW),
        out_shape=(jax.ShapeDtypeStruct((bsz, s, wd), F32), jax.ShapeDtypeStruct((bsz, s, wd), F32)),
        grid=(bsz, nc),
        in_specs=[fspec] * 6 + [bspec] * 6,
        out_specs=(fspec, bspec),
        scratch_shapes=[pltpu.VMEM((2, wd // RGW, RGW, RGW), F32)],
        compiler_params=_params("parallel", "arbitrary"),
        name="rwkv_chunked",
    )(r, k, v, a, b, lwf, r, k, v, a, b, lwb)


def _rwkv_post_kernel(yf_ref, yb_ref, bonus_ref, g_ref, lng_ref, lnb_ref, bo_ref, o_ref):
    bo = bo_ref[...]
    y = yf_ref[...] + yb_ref[...]
    yc = y - _head_sums(y, bo) * (1.0 / D_HEAD)
    var = _head_sums(yc * yc, bo) * (1.0 / D_HEAD)
    y = yc * lax.rsqrt(var + D_LN_EPS) * lng_ref[...] + lnb_ref[...]
    o_ref[...] = ((y + bonus_ref[...]) * g_ref[...]).astype(o_ref.dtype)


def rwkv_post(yf, yb, bonus, g, ln_g, ln_b, *, tm=512):
    t, wd = yf.shape
    row = pl.BlockSpec((tm, wd), lambda i: (i, 0))
    vec = pl.BlockSpec((1, wd), lambda i: (0, 0))
    bo = _block_ones()
    return pl.pallas_call(
        _rwkv_post_kernel,
        out_shape=jax.ShapeDtypeStruct((t, wd), BF16),
        grid=(t // tm,),
        in_specs=[row, row, row, row, vec, vec, pl.BlockSpec(bo.shape, lambda i: (0, 0))],
        out_specs=row,
        compiler_params=_params("parallel"),
        name="rwkv_post",
    )(yf, yb, bonus, g, ln_g.reshape(1, wd), ln_b.reshape(1, wd), bo)


def rwkv7_mixer(proj, bsz, s, mu, w0_f, w2_f, w0_b, w2_b, a0, a2, g2, k_k, k_a, r_k, ln_g, ln_b):
    r, k, v, a, b, lwf, lwb, g, bonus = rwkv_prep(proj, mu, w0_f, w2_f, w0_b, w2_b, a0, a2, g2, k_k, k_a,
                                                  r_k.reshape(-1), seq=s)
    r3 = lambda u: u.reshape(bsz, s, D_WIDTH)
    yf, yb = rwkv_chunked(r3(r), r3(k), r3(v), r3(a), r3(b), r3(lwf), r3(lwb))
    return rwkv_post(yf.reshape(-1, D_WIDTH), yb.reshape(-1, D_WIDTH), bonus, g, ln_g, ln_b)


MOE_TILE = 1024
MOE_SUB = 288
ROW_ALIGN = 16
MOE_PACK = 2


def _route(logit):
    lane = lax.broadcasted_iota(jnp.int32, logit.shape, 1)
    first_at = lambda mask: jnp.min(jnp.where(mask, lane, jnp.int32(LANE)), axis=-1, keepdims=True)
    is_grp = lane < N_GROUPS
    gl = jnp.where(is_grp, logit, NEG)
    gmax = jnp.max(gl, axis=-1, keepdims=True)
    p_grp = 1.0 / jnp.sum(jnp.where(is_grp, jnp.exp(gl - gmax), 0.0), axis=-1, keepdims=True)
    i_grp = first_at(is_grp & (gl == gmax))
    lo = N_GROUPS + i_grp * EXPERTS_PER_GROUP
    in_grp = (lane >= lo) & (lane < lo + EXPERTS_PER_GROUP)
    el = jnp.where(in_grp, logit, NEG)
    l1 = jnp.max(el, axis=-1, keepdims=True)
    i1 = first_at(in_grp & (el == l1))
    rest = in_grp & (lane != i1)
    el2 = jnp.where(rest, logit, NEG)
    l2 = jnp.max(el2, axis=-1, keepdims=True)
    i2 = first_at(rest & (el2 == l2))
    e2 = jnp.exp(l2 - l1)
    w1 = p_grp / (1.0 + e2)
    w2 = p_grp * e2 / (1.0 + e2)
    return i_grp, jnp.where(lane == i1, w1, jnp.where(lane == i2, w2, 0.0))


def _moe_sort_kernel(x_ref, g_ref, wr_ref, br_ref, hn_ref, comb_ref, pos_ref, off_ref):
    x = x_ref[...]
    tm = x.shape[0]
    hn = (x * lax.rsqrt(jnp.mean(x * x, axis=-1, keepdims=True) + EPS) * g_ref[...]).astype(BF16)
    wr, br = wr_ref[...], br_ref[...]
    i_grp, _ = _route(jnp.dot(hn, wr, preferred_element_type=F32) + br)
    lane = lax.broadcasted_iota(jnp.int32, (tm, LANE), 1)
    onehot = (lane == i_grp).astype(F32)
    ri = lax.broadcasted_iota(jnp.int32, (tm, tm), 0)
    ci = lax.broadcasted_iota(jnp.int32, (tm, tm), 1)
    earlier = jnp.dot((ci < ri).astype(BF16), onehot.astype(BF16), preferred_element_type=F32)
    cnt = jnp.broadcast_to(jnp.sum(onehot, axis=0, keepdims=True), (8, LANE))
    lane8 = lax.broadcasted_iota(jnp.int32, (8, LANE), 1)
    start = jnp.zeros((8, LANE), F32)
    for sh in range(1, N_GROUPS):
        start += jnp.where(lane8 >= sh, pltpu.roll(cnt, sh, 1), 0.0)
    off_ref[...] = start[0:1].astype(jnp.int32)
    pos = jnp.sum(onehot * (start[0:1] + earlier), axis=-1, keepdims=True).astype(jnp.int32)
    pos_ref[...] = pos
    perm_t = (ci == pos).astype(BF16)
    hn_s = lax.dot_general(perm_t, hn, TN_DIMS, preferred_element_type=F32).astype(BF16)
    hn_ref[...] = hn_s
    _, comb = _route(jnp.dot(hn_s, wr, preferred_element_type=F32) + br)
    comb_ref[...] = comb


def moe_sort(x, g, w_grp, b_grp, w_exp, b_exp):
    t, d = x.shape
    tm = MOE_TILE
    assert t % (MOE_TILE * MOE_PACK) == 0, t
    nr = N_GROUPS + N_EXPERTS
    wr = jnp.pad(jnp.concatenate([w_grp, w_exp], axis=1), ((0, 0), (0, LANE - nr))).astype(BF16)
    br = jnp.pad(jnp.concatenate([b_grp, b_exp]), (0, LANE - nr)).reshape(1, LANE)
    row = lambda i: (i, 0)
    hn, comb, pos, off = pl.pallas_call(
        _moe_sort_kernel,
        out_shape=(jax.ShapeDtypeStruct((t, d), BF16), jax.ShapeDtypeStruct((t, LANE), F32),
                   jax.ShapeDtypeStruct((t, 1), jnp.int32), jax.ShapeDtypeStruct((t // tm, 1, LANE), jnp.int32)),
        grid=(t // tm,),
        in_specs=[pl.BlockSpec((tm, d), row), pl.BlockSpec((1, d), lambda i: (0, 0)),
                  pl.BlockSpec((d, LANE), lambda i: (0, 0)), pl.BlockSpec((1, LANE), lambda i: (0, 0))],
        out_specs=(pl.BlockSpec((tm, d), row), pl.BlockSpec((tm, LANE), row), pl.BlockSpec((tm, 1), row),
                   pl.BlockSpec((None, 1, LANE), lambda i: (i, 0, 0))),
        compiler_params=_params("parallel"),
        name="moe_sort",
    )(x, g.reshape(1, d), wr, br)
    bounds = jnp.concatenate([off[:, 0, :N_GROUPS], jnp.full((t // tm, 1), tm, jnp.int32)], axis=1)
    return hn, comb, pos, bounds.reshape(-1)


def _moe_group_kernel(bounds_ref, hn_ref, c_ref, wg_ref, wu_ref, wd_ref, y_ref):
    i, g, j = pl.program_id(0), pl.program_id(1), pl.program_id(2)

    @pl.when((g == 0) & (j == 0))
    def _():
        y_ref[...] = jnp.zeros_like(y_ref)

    expert_lane = N_GROUPS + g * EXPERTS_PER_GROUP + j

    def sub_tile(want, size):
        r0 = pl.multiple_of(jnp.minimum(want, lo + MOE_TILE - size), ROW_ALIGN)
        rows = pl.ds(r0, size)
        x = hn_ref[rows, :]
        hg = jnp.dot(x, wg_ref[...], preferred_element_type=F32)
        hu = jnp.dot(x, wu_ref[...], preferred_element_type=F32)
        lane = lax.broadcasted_iota(jnp.int32, (size, LANE), 1)
        c = jnp.sum(jnp.where(lane == expert_lane, c_ref[rows, :], 0.0), axis=-1, keepdims=True)
        row = lax.broadcasted_iota(jnp.int32, (size, 1), 0)
        c = jnp.where(row + r0 >= want, c, 0.0)
        hid = (hg * jax.nn.sigmoid(hg)) * hu * c
        y_ref[rows, :] += jnp.dot(hid.astype(BF16), wd_ref[...], preferred_element_type=F32)

    for half in range(MOE_PACK):
        base = (i * MOE_PACK + half) * (N_GROUPS + 1) + g
        lo = half * MOE_TILE
        start, end = lo + bounds_ref[base], lo + bounds_ref[base + 1]
        first = (start // ROW_ALIGN) * ROW_ALIGN
        n_full = (end - first) // MOE_SUB
        rest = end - first - n_full * MOE_SUB

        def full(k, carry, first=first):
            sub_tile(first + k * MOE_SUB, MOE_SUB)
            return carry

        lax.fori_loop(0, n_full, full, 0)
        tail = first + n_full * MOE_SUB

        @pl.when(rest > MOE_SUB // 2)
        def _():
            sub_tile(tail, MOE_SUB)

        @pl.when((rest > 0) & (rest <= MOE_SUB // 2))
        def _():
            sub_tile(tail, MOE_SUB // 2)


def moe_group_experts(hn, comb, bounds, w_gate, w_up, w_down):
    t, d = hn.shape
    tm = MOE_TILE * MOE_PACK
    ne, _, ff = w_gate.shape
    ex = lambda i, g, j, b: (g * EXPERTS_PER_GROUP + j, 0, 0)
    once = pl.Buffered(1) if MOE_PACK > 1 else None
    return pl.pallas_call(
        _moe_group_kernel,
        out_shape=jax.ShapeDtypeStruct((t, d), F32),
        grid_spec=pltpu.PrefetchScalarGridSpec(
            num_scalar_prefetch=1,
            grid=(t // tm, N_GROUPS, EXPERTS_PER_GROUP),
            in_specs=[pl.BlockSpec((tm, d), lambda i, g, j, b: (i, 0), pipeline_mode=once),
                      pl.BlockSpec((tm, LANE), lambda i, g, j, b: (i, 0)),
                      pl.BlockSpec((None, d, ff), ex), pl.BlockSpec((None, d, ff), ex),
                      pl.BlockSpec((None, ff, d), ex)],
            out_specs=pl.BlockSpec((tm, d), lambda i, g, j, b: (i, 0), pipeline_mode=once)),
        compiler_params=_params("parallel", "arbitrary", "arbitrary"),
        name="moe_group_experts",
    )(bounds, hn, comb, w_gate, w_up, w_down)


def _moe_unsort_kernel(y_ref, pos_ref, x_ref, o_ref):
    tm = y_ref.shape[0]
    perm_t = (lax.broadcasted_iota(jnp.int32, (tm, tm), 1) == pos_ref[...]).astype(BF16)
    y = y_ref[...]
    hi = y.astype(BF16)
    lo = (y - hi.astype(F32)).astype(BF16)
    o_ref[...] = (x_ref[...] + jnp.dot(perm_t, hi, preferred_element_type=F32)
                  + jnp.dot(perm_t, lo, preferred_element_type=F32))


def moe_unsort(y, pos, x):
    t, d = x.shape
    tm, tn = MOE_TILE, d // 2
    blk = pl.BlockSpec((tm, tn), lambda i, j: (i, j))
    return pl.pallas_call(
        _moe_unsort_kernel,
        out_shape=jax.ShapeDtypeStruct((t, d), F32),
        grid=(t // tm, d // tn),
        in_specs=[blk, pl.BlockSpec((tm, 1), lambda i, j: (i, 0)), blk],
        out_specs=blk,
        compiler_params=_params("parallel", "arbitrary"),
        name="moe_unsort",
    )(y, pos, x)


def _moe_unsort_norm_kernel(y_ref, pos_ref, x_ref, g_ref, o_ref):
    tm, ts = pos_ref.shape[0], y_ref.shape[0]
    perm_t = (lax.broadcasted_iota(jnp.int32, (tm, ts), 1) == pos_ref[...]).astype(BF16)
    y = y_ref[...]
    hi = y.astype(BF16)
    lo = (y - hi.astype(F32)).astype(BF16)
    x = (x_ref[...] + jnp.dot(perm_t, hi, preferred_element_type=F32)
         + jnp.dot(perm_t, lo, preferred_element_type=F32))
    o_ref[...] = x * lax.rsqrt(jnp.mean(x * x, axis=-1, keepdims=True) + EPS) * g_ref[...]


def moe_unsort_norm(y, pos, x, g, *, row0, rows, tm=512):
    d = x.shape[1]
    per = MOE_TILE // tm
    r0 = row0 // tm
    return pl.pallas_call(
        _moe_unsort_norm_kernel,
        out_shape=jax.ShapeDtypeStruct((rows, d), F32),
        grid=(rows // tm,),
        in_specs=[pl.BlockSpec((MOE_TILE, d), lambda i: ((i + r0) // per, 0)),
                  pl.BlockSpec((tm, 1), lambda i: (i + r0, 0)),
                  pl.BlockSpec((tm, d), lambda i: (i + r0, 0)),
                  pl.BlockSpec((1, d), lambda i: (0, 0))],
        out_specs=pl.BlockSpec((tm, d), lambda i: (i, 0)),
        compiler_params=_params("parallel"),
        name="moe_unsort_norm",
    )(y, pos, x, g.reshape(1, d))


def hier_moe(x, g, w_grp, b_grp, w_exp, b_exp, w_gate, w_up, w_down, *, final=None):
    hn, comb, pos, bounds = moe_sort(x, g, w_grp, b_grp, w_exp, b_exp)
    y = moe_group_experts(hn, comb, bounds, w_gate.astype(BF16), w_up.astype(BF16), w_down.astype(BF16))
    if final is None:
        return moe_unsort(y, pos, x)
    gain, counts = final
    starts = np.cumsum((0,) + tuple(counts))[:-1]
    return tuple(moe_unsort_norm(y, pos, x, gain, row0=int(r0), rows=int(n)) for r0, n in zip(starts, counts))


def even_layer(x, bsz, s, norm_g, rel_bias, w_in, w_out, w2_f, b_f, w2_b, b_b, onorm):
    n_pad = _round_up(EVEN_IN, LANE)
    w_in_p = jnp.pad(w_in, ((0, 0), (0, n_pad - EVEN_IN))).astype(BF16)
    proj = norm_linear(x, norm_g, w_in_p, tn_target=896).reshape(bsz, s, n_pad)
    tq = ATTN_TQ
    ya = attention(proj, proj, proj, heads=A_HEADS, dq=HEAD_DIM, dv=HEAD_DIM,
                   q_off=0, k_off=A_HEADS, v_off=2 * A_HEADS, scale=HEAD_DIM ** -0.5,
                   bias=dilated_bias_table(rel_bias, s, tq), tq=tq)
    q_col = 3 * A_WIDTH
    yb = gla_mixer(proj, w2_f, w2_b, b_f, b_b, onorm, q_col=q_col, k_col=q_col + B_KEYW,
                   v_col=q_col + 2 * B_KEYW, g_col=q_col + 2 * B_KEYW + B_WIDTH,
                   z_col=q_col + 2 * B_KEYW + 2 * B_WIDTH)
    t = bsz * s
    return out_proj(ya.reshape(t, A_WIDTH), yb.reshape(t, B_WIDTH), w_out.astype(BF16), x)


def _odd_columns(w_in, mu):
    c0 = C_IN
    cut = lambda u, lo, n: u[..., lo:lo + n]
    zpad = lambda u, n: jnp.pad(u, [(0, 0)] * (u.ndim - 1) + [(0, n)])
    off = np.cumsum((0,) + D_SPLITS)
    def rwkv_cols(u):
        parts = [cut(u, off[0], 3 * D_WIDTH), cut(u, off[6], D_G_RANK), cut(u, off[3], 2 * D_W_RANK),
                 cut(u, off[5], D_A_RANK)]
        u = jnp.concatenate(parts, axis=-1)
        return zpad(u, DC_PAD - u.shape[-1])
    w_kr = w_in[:, C_Q_RANK + C_KV_RANK:C_IN]
    w_all = jnp.concatenate([rwkv_cols(w_in[:, c0:]), w_in[:, :C_IN], _rot_half_cols(w_kr)], axis=1)
    n_pad = _round_up(w_all.shape[1], 9 * LANE)
    return zpad(w_all, n_pad - w_all.shape[1]).astype(BF16), rwkv_cols(mu)


def odd_layer(x, bsz, s, norm_g, w_in, w_out, q_norm, w_uq, kv_norm, w_ukv, mu, w0_f, w2_f, w0_b, w2_b,
              a0, a2, g2, k_k, k_a, r_k, ln_g, ln_b):
    t = bsz * s
    w_all, mu_cols = _odd_columns(w_in, mu)
    proj = norm_linear(x, norm_g, w_all, tn_target=1152)
    inv = 1.0 / (ROPE_THETA ** (jnp.arange(0, C_ROPE, 2, dtype=F32) / C_ROPE))
    ang = jnp.arange(s, dtype=F32)[:, None] * inv[None, :]
    cos = jnp.pad(jnp.concatenate([jnp.cos(ang)] * 2, axis=1), ((0, 0), (0, LANE - C_ROPE)), constant_values=1.0)
    sin = jnp.pad(jnp.concatenate([jnp.sin(ang)] * 2, axis=1), ((0, 0), (0, LANE - C_ROPE)))
    q, k, v = mla_up(proj, q_norm, w_uq, kv_norm, w_ukv, jnp.tile(cos, (bsz, 1)), jnp.tile(sin, (bsz, 1)),
                     col0=DC_PAD)
    r3 = lambda u: u.reshape(bsz, s, -1)
    yc = attention(r3(q), r3(k), r3(v), heads=C_HEADS, dq=C_QK, dv=C_V, q_off=0, k_off=0, v_off=0,
                   scale=(C_NOPE + C_ROPE) ** -0.5, tq=s)
    yd = rwkv7_mixer(proj, bsz, s, mu_cols, w0_f, w2_f, w0_b, w2_b, a0, a2, g2, k_k, k_a, r_k, ln_g, ln_b)
    return out_proj(yc.reshape(t, C_WIDTH), yd, w_out.astype(BF16), x)


def kernel(x_prompt, x_sample, rel_bias, norm_mix, norm_ffn, norm_final, ev_w_in, ev_w_out, ev_gla_w2_f, ev_gla_b_f, ev_gla_w2_b, ev_gla_b_b, ev_gla_onorm, od_w_in, od_w_out, od_q_norm, od_w_uq, od_kv_norm, od_w_ukv, od_mu, od_w0_f, od_w2_f, od_w0_b, od_w2_b, od_a0, od_a2, od_g2, od_k_k, od_k_a, od_r_k, od_ln_g, od_ln_b, moe_w_grp, moe_b_grp, moe_w_exp, moe_b_exp, moe_w_gate, moe_w_up, moe_w_down):
    nb_p = x_prompt.shape[0]
    x = jnp.concatenate([x_prompt, x_sample], axis=0)
    bsz, s, d = x.shape
    x = x.reshape(bsz * s, d)
    for i in range(DEPTH):
        j = i // 2
        if i % 2 == 0:
            x = even_layer(x, bsz, s, norm_mix[i], rel_bias, ev_w_in[j], ev_w_out[j], ev_gla_w2_f[j],
                           ev_gla_b_f[j], ev_gla_w2_b[j], ev_gla_b_b[j], ev_gla_onorm[j])
        else:
            x = odd_layer(x, bsz, s, norm_mix[i], od_w_in[j], od_w_out[j], od_q_norm[j], od_w_uq[j],
                          od_kv_norm[j], od_w_ukv[j], od_mu[j], od_w0_f[j], od_w2_f[j], od_w0_b[j],
                          od_w2_b[j], od_a0[j], od_a2[j], od_g2[j], od_k_k[j], od_k_a[j], od_r_k[j],
                          od_ln_g[j], od_ln_b[j])
        final = (norm_final, (nb_p * s, (bsz - nb_p) * s)) if i == DEPTH - 1 else None
        x = hier_moe(x, norm_ffn[i], moe_w_grp[i], moe_b_grp[i], moe_w_exp[i], moe_b_exp[i],
                     moe_w_gate[i], moe_w_up[i], moe_w_down[i], final=final)
    y_p, y_s = x
    return (y_p.reshape(nb_p, s, d), y_s.reshape(bsz - nb_p, s, d))
```

```python
import functools

import jax, jax.numpy as jnp
from jax import lax
import numpy as np
from jax.experimental import pallas as pl
from jax.experimental.pallas import tpu as pltpu

F32, BF16 = jnp.float32, jnp.bfloat16

D_MODEL = 2048
DEPTH = 2
MIX_HALF = D_MODEL // 2
HEAD_DIM = 128
EPS = 1e-6
NEG = -1e30

A_HEADS = MIX_HALF // HEAD_DIM
A_WIDTH = A_HEADS * HEAD_DIM
A_PATTERNS = ((128, 1), (512, 4), (2048, 16))
N_BUCKETS = 32
MAX_DISTANCE = 1024

B_HEADS = 4
B_DV = MIX_HALF // B_HEADS
B_DK = B_DV // 2
B_WIDTH = B_HEADS * B_DV
B_KEYW = B_HEADS * B_DK
B_GATE_RANK = 16
B_GATE_TAU = 16.0
B_CHUNK = 64

C_HEADS = MIX_HALF // 128
C_Q_RANK = 512
C_KV_RANK = 256
C_NOPE = 128
C_ROPE = 64
C_V = 128
C_WIDTH = C_HEADS * C_V
C_QK = 256
ROPE_THETA = 10000.0

D_HEAD = 64
D_HEADS = MIX_HALF // D_HEAD
D_WIDTH = D_HEADS * D_HEAD
D_W_RANK = 64
D_A_RANK = 64
D_G_RANK = 128
D_LN_EPS = 64e-5
D_SPLITS = (D_WIDTH, D_WIDTH, D_WIDTH, D_W_RANK, D_W_RANK, D_A_RANK, D_G_RANK)
D_SHIFT = 3 * D_WIDTH + 2 * D_W_RANK + D_A_RANK + D_G_RANK

N_GROUPS = 4
EXPERTS_PER_GROUP = 4
N_EXPERTS = N_GROUPS * EXPERTS_PER_GROUP

EVEN_IN = 3 * A_WIDTH + 2 * B_KEYW + 2 * B_WIDTH + 2 * B_GATE_RANK
C_IN = C_Q_RANK + C_KV_RANK + C_ROPE

LANE = 128
VMEM_LIMIT = 52 * 1024 * 1024


def _params(*sem):
    return pltpu.CompilerParams(dimension_semantics=sem, vmem_limit_bytes=VMEM_LIMIT)


def _round_up(n, m):
    return -(-n // m) * m


NT_DIMS = (((1,), (1,)), ((), ()))
TN_DIMS = (((0,), (0,)), ((), ()))


def _split3(x):
    hi = x.astype(BF16)
    r1 = x - hi.astype(F32)
    mid = r1.astype(BF16)
    lo = (r1 - mid.astype(F32)).astype(BF16)
    return hi, mid, lo


def _pick_tile(n, target):
    best = LANE
    for t in range(LANE, target + 1, LANE):
        if n % t == 0:
            best = t
    return best


def _norm_linear_kernel(x_ref, g_ref, w_ref, o_ref, xn_ref):
    @pl.when(pl.program_id(1) == 0)
    def _():
        x = x_ref[...]
        y = x * lax.rsqrt(jnp.mean(x * x, axis=-1, keepdims=True) + EPS) * g_ref[...]
        xn_ref[...] = y.astype(BF16)

    o_ref[...] = jnp.dot(xn_ref[...], w_ref[...], preferred_element_type=F32)


def norm_linear(x, g, w, *, tm=1024, tn_target=1024):
    t, k = x.shape
    n = w.shape[1]
    tn = _pick_tile(n, tn_target)
    return pl.pallas_call(
        _norm_linear_kernel,
        out_shape=jax.ShapeDtypeStruct((t, n), F32),
        grid=(t // tm, n // tn),
        in_specs=[pl.BlockSpec((tm, k), lambda i, j: (i, 0)),
                  pl.BlockSpec((1, k), lambda i, j: (0, 0)),
                  pl.BlockSpec((k, tn), lambda i, j: (0, j))],
        out_specs=pl.BlockSpec((tm, tn), lambda i, j: (i, j)),
        scratch_shapes=[pltpu.VMEM((tm, k), BF16)],
        compiler_params=_params("parallel", "arbitrary"),
        name="norm_linear",
    )(x, g.reshape(1, k), w)


def _out_proj_kernel(a_ref, b_ref, wa_ref, wb_ref, x_ref, o_ref):
    acc = jnp.dot(a_ref[...], wa_ref[...], preferred_element_type=F32)
    acc += jnp.dot(b_ref[...], wb_ref[...], preferred_element_type=F32)
    o_ref[...] = x_ref[...] + acc


def out_proj(a, b, w, x, *, tm=512, tn=2048):
    t, ka = a.shape
    kb = b.shape[1]
    n = w.shape[1]
    return pl.pallas_call(
        _out_proj_kernel,
        out_shape=jax.ShapeDtypeStruct((t, n), F32),
        grid=(t // tm, n // tn),
        in_specs=[pl.BlockSpec((tm, ka), lambda i, j: (i, 0)),
                  pl.BlockSpec((tm, kb), lambda i, j: (i, 0)),
                  pl.BlockSpec((ka, tn), lambda i, j: (0, j)),
                  pl.BlockSpec((kb, tn), lambda i, j: (0, j)),
                  pl.BlockSpec((tm, tn), lambda i, j: (i, j))],
        out_specs=pl.BlockSpec((tm, tn), lambda i, j: (i, j)),
        compiler_params=_params("parallel", "arbitrary"),
        name="out_proj",
    )(a, b, w[:ka], w[ka:], x)


LOG2E = float(np.log2(np.e))
ATTN_KB = 256
ATTN_TQ = 1024


def _attn_kernel(*refs, scale, has_bias):
    if has_bias:
        q_ref, k_ref, v_ref, bias_ref, o_ref, kb_ref, vb_ref = refs
    else:
        q_ref, k_ref, v_ref, o_ref, kb_ref, vb_ref = refs
    s_len, dv = v_ref.shape

    @pl.when(pl.program_id(2) == 0)
    def _():
        kb_ref[...] = k_ref[...].astype(BF16)
        vb_ref[...] = jnp.concatenate([v_ref[...].astype(BF16), jnp.ones((s_len, LANE), BF16)], axis=1)

    q = (q_ref[...] * (scale * LOG2E)).astype(BF16)
    blocks = [slice(j * ATTN_KB, (j + 1) * ATTN_KB) for j in range(s_len // ATTN_KB)]
    scores = []
    m = None
    for blk in blocks:
        sj = lax.dot_general(q, kb_ref[blk, :], NT_DIMS, preferred_element_type=F32)
        if has_bias:
            sj = sj + bias_ref[:, blk]
        mj = jnp.max(sj, axis=-1, keepdims=True)
        m = mj if m is None else jnp.maximum(m, mj)
        scores.append(sj)
    o = None
    for blk, sj in zip(blocks, scores):
        oj = jnp.dot(jnp.exp2(sj - m).astype(BF16), vb_ref[blk, :], preferred_element_type=F32)
        o = oj if o is None else o + oj
    den = o[:, dv:]
    if dv > LANE:
        den = jnp.concatenate([den] * (dv // LANE), axis=1)
    o_ref[...] = (o[:, :dv] / den).astype(o_ref.dtype)


def attention(q, k, v, *, heads, dq, dv, q_off, k_off, v_off, scale, bias=None, tq=ATTN_TQ):
    b, s, _ = q.shape
    nq = s // tq
    assert s % tq == 0 and s % ATTN_KB == 0, (s, tq)
    in_specs = [pl.BlockSpec((None, tq, dq), lambda bi, h, qi: (bi, qi, q_off + h)),
                pl.BlockSpec((None, s, dq), lambda bi, h, qi: (bi, 0, k_off + h)),
                pl.BlockSpec((None, s, dv), lambda bi, h, qi: (bi, 0, v_off + h))]
    args = [q, k, v]
    if bias is not None:
        in_specs.append(pl.BlockSpec((None, None, tq, s), lambda bi, h, qi: (h, qi, 0, 0)))
        args.append(bias)
    return pl.pallas_call(
        functools.partial(_attn_kernel, scale=scale, has_bias=bias is not None),
        out_shape=jax.ShapeDtypeStruct((b, s, heads * dv), BF16),
        grid=(b, heads, nq),
        in_specs=in_specs,
        out_specs=pl.BlockSpec((None, tq, dv), lambda bi, h, qi: (bi, qi, h)),
        scratch_shapes=[pltpu.VMEM((s, dq), BF16), pltpu.VMEM((s, dv + LANE), BF16)],
        compiler_params=_params("parallel", "parallel", "arbitrary"),
        name="attention_bias" if bias is not None else "attention",
    )(*args)


def _t5_bucket(rel):
    half = N_BUCKETS // 2
    exact = half // 2
    n = np.abs(rel)
    large = exact + (np.log(np.maximum(n, 1) / exact) / np.log(MAX_DISTANCE / exact) * (half - exact)).astype(np.int64)
    large = np.minimum(large, half - 1)
    return ((rel > 0) * half + np.where(n < exact, n, large)).astype(np.int32)


def dilated_bias_table(rel_bias, s, tq):
    heads = rel_bias.shape[1]
    d = np.arange(-(s - 1), s)
    count = np.zeros(d.shape, np.float32)
    for window, dil in A_PATTERNS:
        count += ((d % dil == 0) & (np.abs(d) <= (window // (2 * dil)) * dil)).astype(np.float32)
    logc = np.where(count > 0, np.log(np.maximum(count, 1.0)), NEG).astype(np.float32)
    onehot = (_t5_bucket(d)[:, None] == np.arange(N_BUCKETS)[None, :]).astype(np.float32)
    line = jnp.transpose(jnp.dot(onehot, rel_bias.astype(F32), precision=lax.Precision.HIGHEST)) + logc[None]
    line = line * LOG2E
    width = 2 * s
    line = jnp.pad(line, ((0, 0), (0, width - line.shape[1])))[:, None, :]
    nq = s // tq
    return pl.pallas_call(
        functools.partial(_skew_kernel, tq=tq, nq=nq),
        out_shape=jax.ShapeDtypeStruct((heads, nq, tq, s), F32),
        grid=(heads, nq),
        in_specs=[pl.BlockSpec((None, 1, width), lambda h, qi: (h, 0, 0))],
        out_specs=pl.BlockSpec((None, None, tq, s), lambda h, qi: (h, qi, 0, 0)),
        compiler_params=_params("parallel", "arbitrary"),
        name="bias_skew",
    )(line)


def _skew_kernel(line_ref, o_ref, *, tq, nq):
    width = line_ref.shape[1]
    first = (nq - 1 - pl.program_id(1)) * tq
    x = jnp.broadcast_to(line_ref[...], (tq, width))
    x = pltpu.roll(x, width - (tq - 1) - first, 1, stride=1, stride_axis=0)
    o_ref[...] = x[:, :o_ref.shape[1]]


GLA_UNROLL = 16


def _gla_kernel(q_ref, k_ref, v_ref, g_ref, z_ref, w2f_ref, w2b_ref, bf_ref, bb_ref, on_ref, o_ref,
                la_ref, acc_ref, qcat_ref, upd_ref, dec_ref, scat_ref, st_ref):
    s_len = q_ref.shape[0]
    c = B_CHUNK
    nchunk = s_len // c
    z = z_ref[...].astype(BF16)
    gate = lambda w2_ref, b_ref: jax.nn.log_sigmoid(
        jnp.dot(z, w2_ref[...], preferred_element_type=F32) + b_ref[...]) * (1.0 / B_GATE_TAU)
    la_ref[0] = gate(w2f_ref, bf_ref)
    la_ref[1] = gate(w2b_ref, bb_ref)

    ri = lax.broadcasted_iota(jnp.int32, (c, c), 0)
    ci = lax.broadcasted_iota(jnp.int32, (c, c), 1)
    keep = (ri >= ci, ri <= ci)
    tri3 = tuple(jnp.concatenate([kp.astype(BF16)] * 3, axis=1) for kp in keep)

    def chunk_rows(n):
        return pl.ds(pl.multiple_of(n * c, c), c)

    def pass1(i, carry):
        units = [(i * GLA_UNROLL + u, d) for u in range(GLA_UNROLL) for d in range(2)]
        st = []
        for n, d in units:
            rows = chunk_rows(n)
            gcum = jnp.dot(tri3[d], jnp.concatenate(_split3(la_ref[d, rows, :]), axis=0),
                           preferred_element_type=F32)
            st.append(dict(rows=rows, gcum=gcum))
        for (n, d), c in zip(units, st):
            gcum = c["gcum"]
            gend = gcum[0:1] if d == 1 else gcum[B_CHUNK - 1:B_CHUNK]
            kc = k_ref[c["rows"], :]
            c["q_in"] = (q_ref[c["rows"], :] * (B_DK ** -0.5) * jnp.exp(gcum)).astype(BF16)
            c["k_out"] = (kc * jnp.exp(gend - gcum)).astype(BF16)
            c["att"] = lax.dot_general(c["q_in"], (kc * jnp.exp(-gcum)).astype(BF16), NT_DIMS,
                                       preferred_element_type=F32)
            dec_ref[d, n] = jnp.broadcast_to(jnp.exp(gend), (8, B_DK))
        for (n, d), c in zip(units, st):
            vc = v_ref[c["rows"], :].astype(BF16)
            c["o"] = jnp.dot(jnp.where(keep[d], c["att"], 0.0).astype(BF16), vc, preferred_element_type=F32)
            upd_ref[d, n] = lax.dot_general(vc, c["k_out"], TN_DIMS, preferred_element_type=F32)
            qcat_ref[c["rows"], d * B_DK:(d + 1) * B_DK] = c["q_in"]
        for u in range(GLA_UNROLL):
            acc_ref[st[2 * u]["rows"], :] = st[2 * u]["o"] + st[2 * u + 1]["o"]
        return carry

    lax.fori_loop(0, nchunk // GLA_UNROLL, pass1, 0)

    st_ref[...] = jnp.zeros_like(st_ref)

    def pass2(n, carry):
        for d, m in ((0, n), (1, nchunk - 1 - n)):
            state = st_ref[d]
            scat_ref[m, :, d * B_DK:(d + 1) * B_DK] = state.astype(BF16)
            st_ref[d] = state * dec_ref[d, m][0:1] + upd_ref[d, m]
        return carry

    lax.fori_loop(0, nchunk, pass2, 0)

    def pass3(i, carry):
        rows = [chunk_rows(i * GLA_UNROLL + u) for u in range(GLA_UNROLL)]
        outs = [acc_ref[r, :] + lax.dot_general(qcat_ref[r, :], scat_ref[i * GLA_UNROLL + u], NT_DIMS,
                                                preferred_element_type=F32) for u, r in enumerate(rows)]
        for r, o in zip(rows, outs):
            o = o * lax.rsqrt(jnp.mean(o * o, axis=-1, keepdims=True) + EPS) * on_ref[...]
            g = g_ref[r, :]
            o_ref[r, :] = (o * (g * jax.nn.sigmoid(g))).astype(o_ref.dtype)
        return carry

    lax.fori_loop(0, nchunk // GLA_UNROLL, pass3, 0)


def gla_mixer(proj, w2f, w2b, b_f, b_b, onorm, *, q_col, k_col, v_col, g_col, z_col):
    b, s, _ = proj.shape
    assert s % (B_CHUNK * GLA_UNROLL) == 0, s
    hm = lambda blk: (lambda bi, h: (bi, 0, blk + h))
    w2f_p = jnp.zeros((LANE, B_KEYW), F32).at[:B_GATE_RANK].set(w2f).astype(BF16)
    w2b_p = jnp.zeros((LANE, B_KEYW), F32).at[B_GATE_RANK:2 * B_GATE_RANK].set(w2b).astype(BF16)
    return pl.pallas_call(
        _gla_kernel,
        out_shape=jax.ShapeDtypeStruct((b, s, B_WIDTH), BF16),
        grid=(b, B_HEADS),
        in_specs=[pl.BlockSpec((None, s, B_DK), hm(q_col // B_DK)),
                  pl.BlockSpec((None, s, B_DK), hm(k_col // B_DK)),
                  pl.BlockSpec((None, s, B_DV), hm(v_col // B_DV)),
                  pl.BlockSpec((None, s, B_DV), hm(g_col // B_DV)),
                  pl.BlockSpec((None, s, LANE), lambda bi, h: (bi, 0, z_col // LANE)),
                  pl.BlockSpec((LANE, B_DK), lambda bi, h: (0, h)),
                  pl.BlockSpec((LANE, B_DK), lambda bi, h: (0, h)),
                  pl.BlockSpec((1, B_DK), lambda bi, h: (0, h)),
                  pl.BlockSpec((1, B_DK), lambda bi, h: (0, h)),
                  pl.BlockSpec((1, B_DV), lambda bi, h: (0, 0))],
        out_specs=pl.BlockSpec((None, s, B_DV), lambda bi, h: (bi, 0, h)),
        scratch_shapes=[pltpu.VMEM((2, s, B_DK), F32),
                        pltpu.VMEM((s, B_DV), F32),
                        pltpu.VMEM((s, 2 * B_DK), BF16),
                        pltpu.VMEM((2, s // B_CHUNK, B_DV, B_DK), F32),
                        pltpu.VMEM((2, s // B_CHUNK, 8, B_DK), F32),
                        pltpu.VMEM((s // B_CHUNK, B_DV, 2 * B_DK), BF16),
                        pltpu.VMEM((2, B_DV, B_DK), F32)],
        compiler_params=_params("parallel", "arbitrary"),
        name="gla_mixer",
    )(proj, proj, proj, proj, proj, w2f_p, w2b_p, b_f.reshape(1, -1), b_b.reshape(1, -1), onorm.reshape(1, -1))


def _mla_up_kernel(cq_ref, ckv_ref, kr_ref, qn_ref, kvn_ref, wq_ref, wqr_ref, wkv_ref, cos_ref, sin_ref,
                   q_ref, k_ref, v_ref):
    def rms(x, g):
        return (x * lax.rsqrt(jnp.mean(x * x, axis=-1, keepdims=True) + EPS) * g).astype(BF16)

    cq = rms(cq_ref[...], qn_ref[...])
    ckv = rms(ckv_ref[...], kvn_ref[...])
    cos, sin = cos_ref[...], sin_ref[...]
    kr = kr_ref[...]
    k_rope = kr * cos + pltpu.roll(kr, LANE - C_ROPE, 1) * sin
    lane = lax.broadcasted_iota(jnp.int32, k_rope.shape, 1)
    k_rope = jnp.where(lane < C_ROPE, k_rope, 0.0)
    for h in range(C_HEADS):
        q = jnp.dot(cq, wq_ref[:, h * C_QK:(h + 1) * C_QK], preferred_element_type=F32)
        qp = jnp.dot(cq, wqr_ref[:, h * LANE:(h + 1) * LANE], preferred_element_type=F32)
        q_ref[:, h * C_QK:h * C_QK + C_NOPE] = q[:, :C_NOPE]
        q_ref[:, h * C_QK + C_NOPE:(h + 1) * C_QK] = q[:, C_NOPE:] * cos + qp * sin
        kv = jnp.dot(ckv, wkv_ref[:, h * 2 * LANE:(h + 1) * 2 * LANE], preferred_element_type=F32)
        k_ref[:, h * C_QK:h * C_QK + C_NOPE] = kv[:, :C_NOPE]
        k_ref[:, h * C_QK + C_NOPE:(h + 1) * C_QK] = k_rope
        v_ref[:, h * C_V:(h + 1) * C_V] = kv[:, C_NOPE:]


def _rot_half_cols(w):
    half = w.shape[-1] // 2
    return jnp.concatenate([-w[..., half:], w[..., :half]], axis=-1)


def mla_up(proj, q_norm, w_uq, kv_norm, w_ukv, cos, sin, *, col0, tm=512):
    t = proj.shape[0]
    wq = w_uq.reshape(C_Q_RANK, C_HEADS, C_NOPE + C_ROPE)
    wq_main = jnp.pad(wq, ((0, 0), (0, 0), (0, C_QK - C_NOPE - C_ROPE))).reshape(C_Q_RANK, C_HEADS * C_QK)
    wq_rot = jnp.pad(_rot_half_cols(wq[..., C_NOPE:]), ((0, 0), (0, 0), (0, LANE - C_ROPE)))
    wq_rot = wq_rot.reshape(C_Q_RANK, C_HEADS * LANE)
    row = lambda i: (i, 0)
    full = lambda arr: pl.BlockSpec(arr.shape, lambda i: (0, 0))
    g_q, g_kv = q_norm.reshape(1, -1), kv_norm.reshape(1, -1)
    wq_main, wq_rot, wkv = wq_main.astype(BF16), wq_rot.astype(BF16), w_ukv.astype(BF16)
    return pl.pallas_call(
        _mla_up_kernel,
        out_shape=(jax.ShapeDtypeStruct((t, C_HEADS * C_QK), F32),
                   jax.ShapeDtypeStruct((t, C_HEADS * C_QK), F32),
                   jax.ShapeDtypeStruct((t, C_WIDTH), F32)),
        grid=(t // tm,),
        in_specs=[pl.BlockSpec((tm, C_Q_RANK), lambda i: (i, col0 // C_Q_RANK)),
                  pl.BlockSpec((tm, C_KV_RANK), lambda i: (i, (col0 + C_Q_RANK) // C_KV_RANK)),
                  pl.BlockSpec((tm, LANE), lambda i: (i, (col0 + C_Q_RANK + C_KV_RANK) // LANE)),
                  full(g_q), full(g_kv), full(wq_main), full(wq_rot), full(wkv),
                  pl.BlockSpec((tm, LANE), row), pl.BlockSpec((tm, LANE), row)],
        out_specs=(pl.BlockSpec((tm, C_HEADS * C_QK), row),
                   pl.BlockSpec((tm, C_HEADS * C_QK), row),
                   pl.BlockSpec((tm, C_WIDTH), row)),
        compiler_params=_params("parallel"),
        name="mla_up",
    )(proj, proj, proj, g_q, g_kv, wq_main, wq_rot, wkv, cos, sin)


RG = 4
RGW = RG * D_HEAD
RCH = 64
DC_R, DC_K, DC_V = 0, D_WIDTH, 2 * D_WIDTH
DC_ZG = 3 * D_WIDTH
DC_ZW = DC_ZG + D_G_RANK
DC_ZA = DC_ZW + 2 * D_W_RANK
DC_PAD = 7 * 512


def _head_sums(x, bo3):
    return jnp.concatenate(
        [jnp.dot(jnp.concatenate(_split3(x[:, RGW * g:RGW * (g + 1)]), axis=1), bo3, preferred_element_type=F32)
         for g in range(x.shape[1] // RGW)], axis=1)


def _block_ones():
    i = np.arange(RGW)
    bo = ((i[:, None] // D_HEAD) == (i[None, :] // D_HEAD)).astype(np.float32)
    return jnp.asarray(np.concatenate([bo, bo, bo], axis=0), BF16)


def _rwkv_prep_kernel(x_ref, xp_ref, xn_ref, mu_ref, w2f_ref, w2b_ref, a2_ref, g2_ref, w0f_ref, w0b_ref,
                      a0_ref, kk_ref, ka_ref, rk_ref, bo_ref,
                      r_ref, k_ref, v_ref, a_ref, b_ref, lwf_ref, lwb_ref, g_ref, bonus_ref, *, tiles_per_seq):
    i = pl.program_id(0) % tiles_per_seq
    x = x_ref[...]
    tm = x.shape[0]
    row = lax.broadcasted_iota(jnp.int32, x.shape, 0)
    prev_row = jnp.where(i == 0, 0.0, xp_ref[7:8, :])
    next_row = jnp.where(i == tiles_per_seq - 1, 0.0, xn_ref[0:1, :])
    prev = jnp.where(row == 0, prev_row, pltpu.roll(x, 1, 0))
    nxt = jnp.where(row == tm - 1, next_row, pltpu.roll(x, tm - 1, 0))
    x = x + mu_ref[...] * (0.5 * (prev + nxt) - x)
    r, k, v = x[:, DC_R:DC_R + D_WIDTH], x[:, DC_K:DC_K + D_WIDTH], x[:, DC_V:DC_V + D_WIDTH]
    zg = x[:, DC_ZG:DC_ZG + LANE]
    zw = x[:, DC_ZW:DC_ZW + LANE]
    za = x[:, DC_ZA:DC_ZA + LANE]
    tz = jnp.tanh(zw).astype(BF16)
    log_decay = lambda w0_ref, w2_ref: -np.exp(-0.5).astype(np.float32) * jax.nn.sigmoid(
        w0_ref[...] + jnp.dot(tz, w2_ref[...], preferred_element_type=F32))
    lwf_ref[...] = log_decay(w0f_ref, w2f_ref)
    lwb_ref[...] = log_decay(w0b_ref, w2b_ref)
    ag = jax.nn.sigmoid(a0_ref[...] + jnp.dot(za.astype(BF16), a2_ref[...], preferred_element_type=F32))
    g_ref[...] = jnp.dot(jax.nn.sigmoid(zg).astype(BF16), g2_ref[...], preferred_element_type=F32)
    bo = bo_ref[...]
    kk = k * kk_ref[...]
    kk = kk / jnp.maximum(jnp.sqrt(_head_sums(kk * kk, bo)), 1e-12)
    k = k * (1.0 + (ag - 1.0) * ka_ref[...])
    r_ref[...] = r
    k_ref[...] = k
    v_ref[...] = v
    a_ref[...] = -kk
    b_ref[...] = kk * ag
    bonus_ref[...] = _head_sums(r * k * rk_ref[...], bo) * v


def rwkv_prep(proj, mu, w0_f, w2_f, w0_b, w2_b, a0, a2, g2, k_k, k_a, r_k, *, seq, tm=256):
    t = proj.shape[0]
    tiles_per_seq = seq // tm
    assert seq % tm == 0 and t % seq == 0, (t, seq)
    hb = tm // 8
    nblk8 = t // 8
    pad_rows = lambda w, lo: jnp.zeros((LANE, D_WIDTH), F32).at[lo:lo + w.shape[0]].set(w).astype(BF16)
    vec = lambda u: u.reshape(1, -1)
    consts = [vec(mu), pad_rows(w2_f, 0), pad_rows(w2_b, D_W_RANK), pad_rows(a2, 0), g2.astype(BF16),
              vec(w0_f), vec(w0_b), vec(a0), vec(k_k), vec(k_a), vec(r_k), _block_ones()]
    full = lambda arr: pl.BlockSpec(arr.shape, lambda i: (0, 0))
    out_spec = pl.BlockSpec((tm, D_WIDTH), lambda i: (i, 0))
    return pl.pallas_call(
        functools.partial(_rwkv_prep_kernel, tiles_per_seq=tiles_per_seq),
        out_shape=tuple(jax.ShapeDtypeStruct((t, D_WIDTH), F32) for _ in range(9)),
        grid=(t // tm,),
        in_specs=[pl.BlockSpec((tm, DC_PAD), lambda i: (i, 0)),
                  pl.BlockSpec((8, DC_PAD), lambda i: (jnp.maximum(i * hb - 1, 0), 0)),
                  pl.BlockSpec((8, DC_PAD), lambda i: (jnp.minimum((i + 1) * hb, nblk8 - 1), 0))]
                 + [full(c) for c in consts],
        out_specs=tuple(out_spec for _ in range(9)),
        compiler_params=_params("parallel"),
        name="rwkv_prep",
    )(proj, proj, proj, *consts)


def _rwkv_chunk_kernel(*refs, ngroups):
    ins, (yf_ref, yb_ref, mt_ref) = refs[:12], refs[12:]

    @pl.when(pl.program_id(1) == 0)
    def _():
        mt_ref[...] = jnp.zeros_like(mt_ref)

    row = lax.broadcasted_iota(jnp.int32, (RCH, RGW), 0)
    col = lax.broadcasted_iota(jnp.int32, (RCH, RGW), 1) & (RCH - 1)
    bdmask = (lax.broadcasted_iota(jnp.int32, (RGW, RGW), 0) // D_HEAD
              == lax.broadcasted_iota(jnp.int32, (RGW, RGW), 1) // D_HEAD)
    tr = lax.broadcasted_iota(jnp.int32, (RCH, RCH), 0)
    tc = lax.broadcasted_iota(jnp.int32, (RCH, RCH), 1)
    zero = jnp.zeros((), F32)

    def bd(z):
        zb = z.astype(BF16)
        return jnp.where(bdmask, jnp.concatenate([zb] * RG, axis=0), jnp.zeros((), BF16))

    def mm(x, y, dims=None):
        x = x.astype(BF16)
        if dims is None:
            return jnp.dot(x, y, preferred_element_type=F32)
        return lax.dot_general(x, y, dims, preferred_element_type=F32)

    chains = [(d, g) for d in range(2) for g in range(ngroups)]
    st = []
    for d, g in chains:
        backward = d == 1
        r_ref, k_ref, v_ref, a_ref, b_ref, lw_ref = ins[6 * d:6 * d + 6]
        tri = ((tc >= tr) if backward else (tc <= tr)).astype(BF16)
        sl = slice(RGW * g, RGW * (g + 1))
        r, k, v, a, b, lw = (ref[:, sl] for ref in (r_ref, k_ref, v_ref, a_ref, b_ref, lw_ref))
        lam = jnp.dot(jnp.concatenate([tri] * 3, axis=1), jnp.concatenate(_split3(lw), axis=0),
                      preferred_element_type=F32)
        lamc = lam[0:1] if backward else lam[RCH - 1:RCH]
        e_inv = jnp.exp(-lam)
        e_out = jnp.exp(lamc - lam)
        ar = jnp.concatenate([a * jnp.exp(lam - lw), r * jnp.exp(lam)], axis=0).astype(BF16)
        bk = jnp.concatenate([b * e_out, k * e_out], axis=0).astype(BF16)
        st.append(dict(ar=ar, bk=bk, v=v, lamc=lamc, sl=sl,
                       gb=mm(ar, bd(b * e_inv), NT_DIMS), gk=mm(ar, bd(k * e_inv), NT_DIMS)))
    for (d, g), c in zip(chains, st):
        strict = (col > row) if d == 1 else (col < row)
        incl = (col >= row) if d == 1 else (col <= row)
        c["lp"] = jnp.where(strict, c["gb"][:RCH], zero)
        c["grb"] = jnp.where(incl, c["gb"][RCH:], zero).astype(BF16)
        akrk = jnp.where(jnp.concatenate([strict, incl], axis=0), c["gk"], zero)
        c["mt"] = mt_ref[d, g]
        amrm = mm(c["ar"], c["mt"].astype(BF16), NT_DIMS) + mm(akrk, bd(c["v"]))
        c["u"] = amrm[:RCH]
        c["rm"] = amrm[RCH:]
    for rnd in range(6):
        for c in st:
            lpb = c["lp"].astype(BF16)
            c["u"] = c["u"] + mm(lpb, bd(c["u"]))
            if rnd < 5:
                c["lp"] = mm(lpb, bd(c["lp"]))
    for (d, g), c in zip(chains, st):
        y_ref = yb_ref if d == 1 else yf_ref
        y_ref[:, c["sl"]] = c["rm"] + mm(c["grb"], bd(c["u"]))
        uv = jnp.concatenate([c["u"], c["v"]], axis=0).astype(BF16)
        upd = lax.dot_general(uv, c["bk"], TN_DIMS, preferred_element_type=F32)
        mt_ref[d, g] = c["mt"] * jnp.exp(c["lamc"]) + jnp.where(bdmask, upd, zero)


def rwkv_chunked(r, k, v, a, b, lwf, lwb):
    bsz, s, wd = r.shape
    nc = s // RCH
    assert s % RCH == 0 and wd % RGW == 0, (s, wd)
    fspec = pl.BlockSpec((None, RCH, wd), lambda bi, n: (bi, n, 0))
    bspec = pl.BlockSpec((None, RCH, wd), lambda bi, n: (bi, nc - 1 - n, 0))
    return pl.pallas_call(
        functools.partial(_rwkv_chunk_kernel, ngroups=wd // RGW),
        out_shape=(jax.ShapeDtypeStruct((bsz, s, wd), F32), jax.ShapeDtypeStruct((bsz, s, wd), F32)),
        grid=(bsz, nc),
        in_specs=[fspec] * 6 + [bspec] * 6,
        out_specs=(fspec, bspec),
        scratch_shapes=[pltpu.VMEM((2, wd // RGW, RGW, RGW), F32)],
        compiler_params=_params("parallel", "arbitrary"),
        name="rwkv_chunked",
    )(r, k, v, a, b, lwf, r, k, v, a, b, lwb)


def _rwkv_post_kernel(yf_ref, yb_ref, bonus_ref, g_ref, lng_ref, lnb_ref, bo_ref, o_ref):
    bo = bo_ref[...]
    y = yf_ref[...] + yb_ref[...]
    yc = y - _head_sums(y, bo) * (1.0 / D_HEAD)
    var = _head_sums(yc * yc, bo) * (1.0 / D_HEAD)
    y = yc * lax.rsqrt(var + D_LN_EPS) * lng_ref[...] + lnb_ref[...]
    o_ref[...] = ((y + bonus_ref[...]) * g_ref[...]).astype(o_ref.dtype)


def rwkv_post(yf, yb, bonus, g, ln_g, ln_b, *, tm=512):
    t, wd = yf.shape
    row = pl.BlockSpec((tm, wd), lambda i: (i, 0))
    vec = pl.BlockSpec((1, wd), lambda i: (0, 0))
    bo = _block_ones()
    return pl.pallas_call(
        _rwkv_post_kernel,
        out_shape=jax.ShapeDtypeStruct((t, wd), BF16),
        grid=(t // tm,),
        in_specs=[row, row, row, row, vec, vec, pl.BlockSpec(bo.shape, lambda i: (0, 0))],
        out_specs=row,
        compiler_params=_params("parallel"),
        name="rwkv_post",
    )(yf, yb, bonus, g, ln_g.reshape(1, wd), ln_b.reshape(1, wd), bo)


def rwkv7_mixer(proj, bsz, s, mu, w0_f, w2_f, w0_b, w2_b, a0, a2, g2, k_k, k_a, r_k, ln_g, ln_b):
    r, k, v, a, b, lwf, lwb, g, bonus = rwkv_prep(proj, mu, w0_f, w2_f, w0_b, w2_b, a0, a2, g2, k_k, k_a,
                                                  r_k.reshape(-1), seq=s)
    r3 = lambda u: u.reshape(bsz, s, D_WIDTH)
    yf, yb = rwkv_chunked(r3(r), r3(k), r3(v), r3(a), r3(b), r3(lwf), r3(lwb))
    return rwkv_post(yf.reshape(-1, D_WIDTH), yb.reshape(-1, D_WIDTH), bonus, g, ln_g, ln_b)


MOE_TILE = 1024
MOE_SUB = 288
ROW_ALIGN = 16
MOE_PACK = 2


def _route(logit):
    lane = lax.broadcasted_iota(jnp.int32, logit.shape, 1)
    first_at = lambda mask: jnp.min(jnp.where(mask, lane, jnp.int32(LANE)), axis=-1, keepdims=True)
    is_grp = lane < N_GROUPS
    gl = jnp.where(is_grp, logit, NEG)
    gmax = jnp.max(gl, axis=-1, keepdims=True)
    p_grp = 1.0 / jnp.sum(jnp.where(is_grp, jnp.exp(gl - gmax), 0.0), axis=-1, keepdims=True)
    i_grp = first_at(is_grp & (gl == gmax))
    lo = N_GROUPS + i_grp * EXPERTS_PER_GROUP
    in_grp = (lane >= lo) & (lane < lo + EXPERTS_PER_GROUP)
    el = jnp.where(in_grp, logit, NEG)
    l1 = jnp.max(el, axis=-1, keepdims=True)
    i1 = first_at(in_grp & (el == l1))
    rest = in_grp & (lane != i1)
    el2 = jnp.where(rest, logit, NEG)
    l2 = jnp.max(el2, axis=-1, keepdims=True)
    i2 = first_at(rest & (el2 == l2))
    e2 = jnp.exp(l2 - l1)
    w1 = p_grp / (1.0 + e2)
    w2 = p_grp * e2 / (1.0 + e2)
    return i_grp, jnp.where(lane == i1, w1, jnp.where(lane == i2, w2, 0.0))


def _moe_sort_kernel(x_ref, g_ref, wr_ref, br_ref, hn_ref, comb_ref, pos_ref, off_ref):
    x = x_ref[...]
    tm = x.shape[0]
    hn = (x * lax.rsqrt(jnp.mean(x * x, axis=-1, keepdims=True) + EPS) * g_ref[...]).astype(BF16)
    wr, br = wr_ref[...], br_ref[...]
    i_grp, _ = _route(jnp.dot(hn, wr, preferred_element_type=F32) + br)
    lane = lax.broadcasted_iota(jnp.int32, (tm, LANE), 1)
    onehot = (lane == i_grp).astype(F32)
    ri = lax.broadcasted_iota(jnp.int32, (tm, tm), 0)
    ci = lax.broadcasted_iota(jnp.int32, (tm, tm), 1)
    earlier = jnp.dot((ci < ri).astype(BF16), onehot.astype(BF16), preferred_element_type=F32)
    cnt = jnp.broadcast_to(jnp.sum(onehot, axis=0, keepdims=True), (8, LANE))
    lane8 = lax.broadcasted_iota(jnp.int32, (8, LANE), 1)
    start = jnp.zeros((8, LANE), F32)
    for sh in range(1, N_GROUPS):
        start += jnp.where(lane8 >= sh, pltpu.roll(cnt, sh, 1), 0.0)
    off_ref[...] = start[0:1].astype(jnp.int32)
    pos = jnp.sum(onehot * (start[0:1] + earlier), axis=-1, keepdims=True).astype(jnp.int32)
    pos_ref[...] = pos
    perm_t = (ci == pos).astype(BF16)
    hn_s = lax.dot_general(perm_t, hn, TN_DIMS, preferred_element_type=F32).astype(BF16)
    hn_ref[...] = hn_s
    _, comb = _route(jnp.dot(hn_s, wr, preferred_element_type=F32) + br)
    comb_ref[...] = comb


def moe_sort(x, g, w_grp, b_grp, w_exp, b_exp):
    t, d = x.shape
    tm = MOE_TILE
    assert t % (MOE_TILE * MOE_PACK) == 0, t
    nr = N_GROUPS + N_EXPERTS
    wr = jnp.pad(jnp.concatenate([w_grp, w_exp], axis=1), ((0, 0), (0, LANE - nr))).astype(BF16)
    br = jnp.pad(jnp.concatenate([b_grp, b_exp]), (0, LANE - nr)).reshape(1, LANE)
    row = lambda i: (i, 0)
    hn, comb, pos, off = pl.pallas_call(
        _moe_sort_kernel,
        out_shape=(jax.ShapeDtypeStruct((t, d), BF16), jax.ShapeDtypeStruct((t, LANE), F32),
                   jax.ShapeDtypeStruct((t, 1), jnp.int32), jax.ShapeDtypeStruct((t // tm, 1, LANE), jnp.int32)),
        grid=(t // tm,),
        in_specs=[pl.BlockSpec((tm, d), row), pl.BlockSpec((1, d), lambda i: (0, 0)),
                  pl.BlockSpec((d, LANE), lambda i: (0, 0)), pl.BlockSpec((1, LANE), lambda i: (0, 0))],
        out_specs=(pl.BlockSpec((tm, d), row), pl.BlockSpec((tm, LANE), row), pl.BlockSpec((tm, 1), row),
                   pl.BlockSpec((None, 1, LANE), lambda i: (i, 0, 0))),
        compiler_params=_params("parallel"),
        name="moe_sort",
    )(x, g.reshape(1, d), wr, br)
    bounds = jnp.concatenate([off[:, 0, :N_GROUPS], jnp.full((t // tm, 1), tm, jnp.int32)], axis=1)
    return hn, comb, pos, bounds.reshape(-1)


def _moe_group_kernel(bounds_ref, hn_ref, c_ref, wg_ref, wu_ref, wd_ref, y_ref):
    i, g, j = pl.program_id(0), pl.program_id(1), pl.program_id(2)

    @pl.when((g == 0) & (j == 0))
    def _():
        y_ref[...] = jnp.zeros_like(y_ref)

    expert_lane = N_GROUPS + g * EXPERTS_PER_GROUP + j

    def sub_tile(want, size):
        r0 = pl.multiple_of(jnp.minimum(want, lo + MOE_TILE - size), ROW_ALIGN)
        rows = pl.ds(r0, size)
        x = hn_ref[rows, :]
        hg = jnp.dot(x, wg_ref[...], preferred_element_type=F32)
        hu = jnp.dot(x, wu_ref[...], preferred_element_type=F32)
        lane = lax.broadcasted_iota(jnp.int32, (size, LANE), 1)
        c = jnp.sum(jnp.where(lane == expert_lane, c_ref[rows, :], 0.0), axis=-1, keepdims=True)
        row = lax.broadcasted_iota(jnp.int32, (size, 1), 0)
        c = jnp.where(row + r0 >= want, c, 0.0)
        hid = (hg * jax.nn.sigmoid(hg)) * hu * c
        y_ref[rows, :] += jnp.dot(hid.astype(BF16), wd_ref[...], preferred_element_type=F32)

    for half in range(MOE_PACK):
        base = (i * MOE_PACK + half) * (N_GROUPS + 1) + g
        lo = half * MOE_TILE
        start, end = lo + bounds_ref[base], lo + bounds_ref[base + 1]
        first = (start // ROW_ALIGN) * ROW_ALIGN
        n_full = (end - first) // MOE_SUB
        rest = end - first - n_full * MOE_SUB

        def full(k, carry, first=first):
            sub_tile(first + k * MOE_SUB, MOE_SUB)
            return carry

        lax.fori_loop(0, n_full, full, 0)
        tail = first + n_full * MOE_SUB

        @pl.when(rest > MOE_SUB // 2)
        def _():
            sub_tile(tail, MOE_SUB)

        @pl.when((rest > 0) & (rest <= MOE_SUB // 2))
        def _():
            sub_tile(tail, MOE_SUB // 2)


def moe_group_experts(hn, comb, bounds, w_gate, w_up, w_down):
    t, d = hn.shape
    tm = MOE_TILE * MOE_PACK
    ne, _, ff = w_gate.shape
    ex = lambda i, g, j, b: (g * EXPERTS_PER_GROUP + j, 0, 0)
    once = pl.Buffered(1) if MOE_PACK > 1 else None
    return pl.pallas_call(
        _moe_group_kernel,
        out_shape=jax.ShapeDtypeStruct((t, d), F32),
        grid_spec=pltpu.PrefetchScalarGridSpec(
            num_scalar_prefetch=1,
            grid=(t // tm, N_GROUPS, EXPERTS_PER_GROUP),
            in_specs=[pl.BlockSpec((tm, d), lambda i, g, j, b: (i, 0), pipeline_mode=once),
                      pl.BlockSpec((tm, LANE), lambda i, g, j, b: (i, 0)),
                      pl.BlockSpec((None, d, ff), ex), pl.BlockSpec((None, d, ff), ex),
                      pl.BlockSpec((None, ff, d), ex)],
            out_specs=pl.BlockSpec((tm, d), lambda i, g, j, b: (i, 0), pipeline_mode=once)),
        compiler_params=_params("parallel", "arbitrary", "arbitrary"),
        name="moe_group_experts",
    )(bounds, hn, comb, w_gate, w_up, w_down)


def _moe_unsort_kernel(y_ref, pos_ref, x_ref, o_ref):
    tm = y_ref.shape[0]
    perm_t = (lax.broadcasted_iota(jnp.int32, (tm, tm), 1) == pos_ref[...]).astype(BF16)
    y = y_ref[...]
    hi = y.astype(BF16)
    lo = (y - hi.astype(F32)).astype(BF16)
    o_ref[...] = (x_ref[...] + jnp.dot(perm_t, hi, preferred_element_type=F32)
                  + jnp.dot(perm_t, lo, preferred_element_type=F32))


def moe_unsort(y, pos, x):
    t, d = x.shape
    tm, tn = MOE_TILE, d // 2
    blk = pl.BlockSpec((tm, tn), lambda i, j: (i, j))
    return pl.pallas_call(
        _moe_unsort_kernel,
        out_shape=jax.ShapeDtypeStruct((t, d), F32),
        grid=(t // tm, d // tn),
        in_specs=[blk, pl.BlockSpec((tm, 1), lambda i, j: (i, 0)), blk],
        out_specs=blk,
        compiler_params=_params("parallel", "arbitrary"),
        name="moe_unsort",
    )(y, pos, x)


def _moe_unsort_norm_kernel(y_ref, pos_ref, x_ref, g_ref, o_ref):
    tm, ts = pos_ref.shape[0], y_ref.shape[0]
    perm_t = (lax.broadcasted_iota(jnp.int32, (tm, ts), 1) == pos_ref[...]).astype(BF16)
    y = y_ref[...]
    hi = y.astype(BF16)
    lo = (y - hi.astype(F32)).astype(BF16)
    x = (x_ref[...] + jnp.dot(perm_t, hi, preferred_element_type=F32)
         + jnp.dot(perm_t, lo, preferred_element_type=F32))
    o_ref[...] = x * lax.rsqrt(jnp.mean(x * x, axis=-1, keepdims=True) + EPS) * g_ref[...]


def moe_unsort_norm(y, pos, x, g, *, row0, rows, tm=512):
    d = x.shape[1]
    per = MOE_TILE // tm
    r0 = row0 // tm
    return pl.pallas_call(
        _moe_unsort_norm_kernel,
        out_shape=jax.ShapeDtypeStruct((rows, d), F32),
        grid=(rows // tm,),
        in_specs=[pl.BlockSpec((MOE_TILE, d), lambda i: ((i + r0) // per, 0)),
                  pl.BlockSpec((tm, 1), lambda i: (i + r0, 0)),
                  pl.BlockSpec((tm, d), lambda i: (i + r0, 0)),
                  pl.BlockSpec((1, d), lambda i: (0, 0))],
        out_specs=pl.BlockSpec((tm, d), lambda i: (i, 0)),
        compiler_params=_params("parallel"),
        name="moe_unsort_norm",
    )(y, pos, x, g.reshape(1, d))


def hier_moe(x, g, w_grp, b_grp, w_exp, b_exp, w_gate, w_up, w_down, *, final=None):
    hn, comb, pos, bounds = moe_sort(x, g, w_grp, b_grp, w_exp, b_exp)
    y = moe_group_experts(hn, comb, bounds, w_gate.astype(BF16), w_up.astype(BF16), w_down.astype(BF16))
    if final is None:
        return moe_unsort(y, pos, x)
    gain, counts = final
    starts = np.cumsum((0,) + tuple(counts))[:-1]
    return tuple(moe_unsort_norm(y, pos, x, gain, row0=int(r0), rows=int(n)) for r0, n in zip(starts, counts))


def even_layer(x, bsz, s, norm_g, rel_bias, w_in, w_out, w2_f, b_f, w2_b, b_b, onorm):
    n_pad = _round_up(EVEN_IN, LANE)
    w_in_p = jnp.pad(w_in, ((0, 0), (0, n_pad - EVEN_IN))).astype(BF16)
    proj = norm_linear(x, norm_g, w_in_p, tn_target=896).reshape(bsz, s, n_pad)
    tq = ATTN_TQ
    ya = attention(proj, proj, proj, heads=A_HEADS, dq=HEAD_DIM, dv=HEAD_DIM,
                   q_off=0, k_off=A_HEADS, v_off=2 * A_HEADS, scale=HEAD_DIM ** -0.5,
                   bias=dilated_bias_table(rel_bias, s, tq), tq=tq)
    q_col = 3 * A_WIDTH
    yb = gla_mixer(proj, w2_f, w2_b, b_f, b_b, onorm, q_col=q_col, k_col=q_col + B_KEYW,
                   v_col=q_col + 2 * B_KEYW, g_col=q_col + 2 * B_KEYW + B_WIDTH,
                   z_col=q_col + 2 * B_KEYW + 2 * B_WIDTH)
    t = bsz * s
    return out_proj(ya.reshape(t, A_WIDTH), yb.reshape(t, B_WIDTH), w_out.astype(BF16), x)


def _odd_columns(w_in, mu):
    c0 = C_IN
    cut = lambda u, lo, n: u[..., lo:lo + n]
    zpad = lambda u, n: jnp.pad(u, [(0, 0)] * (u.ndim - 1) + [(0, n)])
    off = np.cumsum((0,) + D_SPLITS)
    def rwkv_cols(u):
        parts = [cut(u, off[0], 3 * D_WIDTH), cut(u, off[6], D_G_RANK), cut(u, off[3], 2 * D_W_RANK),
                 cut(u, off[5], D_A_RANK)]
        u = jnp.concatenate(parts, axis=-1)
        return zpad(u, DC_PAD - u.shape[-1])
    w_kr = w_in[:, C_Q_RANK + C_KV_RANK:C_IN]
    w_all = jnp.concatenate([rwkv_cols(w_in[:, c0:]), w_in[:, :C_IN], _rot_half_cols(w_kr)], axis=1)
    n_pad = _round_up(w_all.shape[1], 9 * LANE)
    return zpad(w_all, n_pad - w_all.shape[1]).astype(BF16), rwkv_cols(mu)


def odd_layer(x, bsz, s, norm_g, w_in, w_out, q_norm, w_uq, kv_norm, w_ukv, mu, w0_f, w2_f, w0_b, w2_b,
              a0, a2, g2, k_k, k_a, r_k, ln_g, ln_b):
    t = bsz * s
    w_all, mu_cols = _odd_columns(w_in, mu)
    proj = norm_linear(x, norm_g, w_all, tn_target=1152)
    inv = 1.0 / (ROPE_THETA ** (jnp.arange(0, C_ROPE, 2, dtype=F32) / C_ROPE))
    ang = jnp.arange(s, dtype=F32)[:, None] * inv[None, :]
    cos = jnp.pad(jnp.concatenate([jnp.cos(ang)] * 2, axis=1), ((0, 0), (0, LANE - C_ROPE)), constant_values=1.0)
    sin = jnp.pad(jnp.concatenate([jnp.sin(ang)] * 2, axis=1), ((0, 0), (0, LANE - C_ROPE)))
    q, k, v = mla_up(proj, q_norm, w_uq, kv_norm, w_ukv, jnp.tile(cos, (bsz, 1)), jnp.tile(sin, (bsz, 1)),
                     col0=DC_PAD)
    r3 = lambda u: u.reshape(bsz, s, -1)
    yc = attention(r3(q), r3(k), r3(v), heads=C_HEADS, dq=C_QK, dv=C_V, q_off=0, k_off=0, v_off=0,
                   scale=(C_NOPE + C_ROPE) ** -0.5, tq=s)
    yd = rwkv7_mixer(proj, bsz, s, mu_cols, w0_f, w2_f, w0_b, w2_b, a0, a2, g2, k_k, k_a, r_k, ln_g, ln_b)
    return out_proj(yc.reshape(t, C_WIDTH), yd, w_out.astype(BF16), x)


def kernel(x_prompt, x_sample, rel_bias, norm_mix, norm_ffn, norm_final, ev_w_in, ev_w_out, ev_gla_w2_f, ev_gla_b_f, ev_gla_w2_b, ev_gla_b_b, ev_gla_onorm, od_w_in, od_w_out, od_q_norm, od_w_uq, od_kv_norm, od_w_ukv, od_mu, od_w0_f, od_w2_f, od_w0_b, od_w2_b, od_a0, od_a2, od_g2, od_k_k, od_k_a, od_r_k, od_ln_g, od_ln_b, moe_w_grp, moe_b_grp, moe_w_exp, moe_b_exp, moe_w_gate, moe_w_up, moe_w_down):
    nb_p = x_prompt.shape[0]
    x = jnp.concatenate([x_prompt, x_sample], axis=0)
    bsz, s, d = x.shape
    x = x.reshape(bsz * s, d)
    for i in range(DEPTH):
        j = i // 2
        if i % 2 == 0:
            x = even_layer(x, bsz, s, norm_mix[i], rel_bias, ev_w_in[j], ev_w_out[j], ev_gla_w2_f[j],
                           ev_gla_b_f[j], ev_gla_w2_b[j], ev_gla_b_b[j], ev_gla_onorm[j])
        else:
            x = odd_layer(x, bsz, s, norm_mix[i], od_w_in[j], od_w_out[j], od_q_norm[j], od_w_uq[j],
                          od_kv_norm[j], od_w_ukv[j], od_mu[j], od_w0_f[j], od_w2_f[j], od_w0_b[j],
                          od_w2_b[j], od_a0[j], od_a2[j], od_g2[j], od_k_k[j], od_k_a[j], od_r_k[j],
                          od_ln_g[j], od_ln_b[j])
        final = (norm_final, (nb_p * s, (bsz - nb_p) * s)) if i == DEPTH - 1 else None
        x = hier_moe(x, norm_ffn[i], moe_w_grp[i], moe_b_grp[i], moe_w_exp[i], moe_b_exp[i],
                     moe_w_gate[i], moe_w_up[i], moe_w_down[i], final=final)
    y_p, y_s = x
    return (y_p.reshape(nb_p, s, d), y_s.reshape(bsz - nb_p, s, d))
```

```python
import functools

import jax, jax.numpy as jnp
from jax import lax
import numpy as np
from jax.experimental import pallas as pl
from jax.experimental.pallas import tpu as pltpu

F32, BF16 = jnp.float32, jnp.bfloat16

D_MODEL = 2048
DEPTH = 2
MIX_HALF = D_MODEL // 2
HEAD_DIM = 128
EPS = 1e-6
NEG = -1e30

A_HEADS = MIX_HALF // HEAD_DIM
A_WIDTH = A_HEADS * HEAD_DIM
A_PATTERNS = ((128, 1), (512, 4), (2048, 16))
N_BUCKETS = 32
MAX_DISTANCE = 1024

B_HEADS = 4
B_DV = MIX_HALF // B_HEADS
B_DK = B_DV // 2
B_WIDTH = B_HEADS * B_DV
B_KEYW = B_HEADS * B_DK
B_GATE_RANK = 16
B_GATE_TAU = 16.0
B_CHUNK = 64

C_HEADS = MIX_HALF // 128
C_Q_RANK = 512
C_KV_RANK = 256
C_NOPE = 128
C_ROPE = 64
C_V = 128
C_WIDTH = C_HEADS * C_V
C_QK = 256
ROPE_THETA = 10000.0

D_HEAD = 64
D_HEADS = MIX_HALF // D_HEAD
D_WIDTH = D_HEADS * D_HEAD
D_W_RANK = 64
D_A_RANK = 64
D_G_RANK = 128
D_LN_EPS = 64e-5
D_SPLITS = (D_WIDTH, D_WIDTH, D_WIDTH, D_W_RANK, D_W_RANK, D_A_RANK, D_G_RANK)
D_SHIFT = 3 * D_WIDTH + 2 * D_W_RANK + D_A_RANK + D_G_RANK

N_GROUPS = 4
EXPERTS_PER_GROUP = 4
N_EXPERTS = N_GROUPS * EXPERTS_PER_GROUP

EVEN_IN = 3 * A_WIDTH + 2 * B_KEYW + 2 * B_WIDTH + 2 * B_GATE_RANK
C_IN = C_Q_RANK + C_KV_RANK + C_ROPE

LANE = 128
VMEM_LIMIT = 52 * 1024 * 1024


def _params(*sem):
    return pltpu.CompilerParams(dimension_semantics=sem, vmem_limit_bytes=VMEM_LIMIT)


def _round_up(n, m):
    return -(-n // m) * m


NT_DIMS = (((1,), (1,)), ((), ()))
TN_DIMS = (((0,), (0,)), ((), ()))


def _split3(x):
    hi = x.astype(BF16)
    r1 = x - hi.astype(F32)
    mid = r1.astype(BF16)
    lo = (r1 - mid.astype(F32)).astype(BF16)
    return hi, mid, lo


def _pick_tile(n, target):
    best = LANE
    for t in range(LANE, target + 1, LANE):
        if n % t == 0:
            best = t
    return best


def _norm_linear_kernel(x_ref, g_ref, w_ref, o_ref, xn_ref):
    @pl.when(pl.program_id(1) == 0)
    def _():
        x = x_ref[...]
        y = x * lax.rsqrt(jnp.mean(x * x, axis=-1, keepdims=True) + EPS) * g_ref[...]
        xn_ref[...] = y.astype(BF16)

    o_ref[...] = jnp.dot(xn_ref[...], w_ref[...], preferred_element_type=F32)


def norm_linear(x, g, w, *, tm=1024, tn_target=1024):
    t, k = x.shape
    n = w.shape[1]
    tn = _pick_tile(n, tn_target)
    return pl.pallas_call(
        _norm_linear_kernel,
        out_shape=jax.ShapeDtypeStruct((t, n), F32),
        grid=(t // tm, n // tn),
        in_specs=[pl.BlockSpec((tm, k), lambda i, j: (i, 0)),
                  pl.BlockSpec((1, k), lambda i, j: (0, 0)),
                  pl.BlockSpec((k, tn), lambda i, j: (0, j))],
        out_specs=pl.BlockSpec((tm, tn), lambda i, j: (i, j)),
        scratch_shapes=[pltpu.VMEM((tm, k), BF16)],
        compiler_params=_params("parallel", "arbitrary"),
        name="norm_linear",
    )(x, g.reshape(1, k), w)


def _out_proj_kernel(a_ref, b_ref, wa_ref, wb_ref, x_ref, o_ref):
    acc = jnp.dot(a_ref[...], wa_ref[...], preferred_element_type=F32)
    acc += jnp.dot(b_ref[...], wb_ref[...], preferred_element_type=F32)
    o_ref[...] = x_ref[...] + acc


def out_proj(a, b, w, x, *, tm=512, tn=2048):
    t, ka = a.shape
    kb = b.shape[1]
    n = w.shape[1]
    return pl.pallas_call(
        _out_proj_kernel,
        out_shape=jax.ShapeDtypeStruct((t, n), F32),
        grid=(t // tm, n // tn),
        in_specs=[pl.BlockSpec((tm, ka), lambda i, j: (i, 0)),
                  pl.BlockSpec((tm, kb), lambda i, j: (i, 0)),
                  pl.BlockSpec((ka, tn), lambda i, j: (0, j)),
                  pl.BlockSpec((kb, tn), lambda i, j: (0, j)),
                  pl.BlockSpec((tm, tn), lambda i, j: (i, j))],
        out_specs=pl.BlockSpec((tm, tn), lambda i, j: (i, j)),
        compiler_params=_params("parallel", "arbitrary"),
        name="out_proj",
    )(a, b, w[:ka], w[ka:], x)


LOG2E = float(np.log2(np.e))
ATTN_KB = 256
ATTN_TQ = 1024


def _attn_kernel(*refs, scale, has_bias):
    if has_bias:
        q_ref, k_ref, v_ref, bias_ref, o_ref, kb_ref, vb_ref = refs
    else:
        q_ref, k_ref, v_ref, o_ref, kb_ref, vb_ref = refs
    s_len, dv = v_ref.shape

    @pl.when(pl.program_id(2) == 0)
    def _():
        kb_ref[...] = k_ref[...].astype(BF16)
        vb_ref[...] = jnp.concatenate([v_ref[...].astype(BF16), jnp.ones((s_len, LANE), BF16)], axis=1)

    q = (q_ref[...] * (scale * LOG2E)).astype(BF16)
    blocks = [slice(j * ATTN_KB, (j + 1) * ATTN_KB) for j in range(s_len // ATTN_KB)]
    scores = []
    m = None
    for blk in blocks:
        sj = lax.dot_general(q, kb_ref[blk, :], NT_DIMS, preferred_element_type=F32)
        if has_bias:
            sj = sj + bias_ref[:, blk]
        mj = jnp.max(sj, axis=-1, keepdims=True)
        m = mj if m is None else jnp.maximum(m, mj)
        scores.append(sj)
    o = None
    for blk, sj in zip(blocks, scores):
        oj = jnp.dot(jnp.exp2(sj - m).astype(BF16), vb_ref[blk, :], preferred_element_type=F32)
        o = oj if o is None else o + oj
    den = o[:, dv:]
    if dv > LANE:
        den = jnp.concatenate([den] * (dv // LANE), axis=1)
    o_ref[...] = (o[:, :dv] / den).astype(o_ref.dtype)


def attention(q, k, v, *, heads, dq, dv, q_off, k_off, v_off, scale, bias=None, tq=ATTN_TQ):
    b, s, _ = q.shape
    nq = s // tq
    assert s % tq == 0 and s % ATTN_KB == 0, (s, tq)
    in_specs = [pl.BlockSpec((None, tq, dq), lambda bi, h, qi: (bi, qi, q_off + h)),
                pl.BlockSpec((None, s, dq), lambda bi, h, qi: (bi, 0, k_off + h)),
                pl.BlockSpec((None, s, dv), lambda bi, h, qi: (bi, 0, v_off + h))]
    args = [q, k, v]
    if bias is not None:
        in_specs.append(pl.BlockSpec((None, None, tq, s), lambda bi, h, qi: (h, qi, 0, 0)))
        args.append(bias)
    return pl.pallas_call(
        functools.partial(_attn_kernel, scale=scale, has_bias=bias is not None),
        out_shape=jax.ShapeDtypeStruct((b, s, heads * dv), BF16),
        grid=(b, heads, nq),
        in_specs=in_specs,
        out_specs=pl.BlockSpec((None, tq, dv), lambda bi, h, qi: (bi, qi, h)),
        scratch_shapes=[pltpu.VMEM((s, dq), BF16), pltpu.VMEM((s, dv + LANE), BF16)],
        compiler_params=_params("parallel", "parallel", "arbitrary"),
        name="attention_bias" if bias is not None else "attention",
    )(*args)


def _t5_bucket(rel):
    half = N_BUCKETS // 2
    exact = half // 2
    n = np.abs(rel)
    large = exact + (np.log(np.maximum(n, 1) / exact) / np.log(MAX_DISTANCE / exact) * (half - exact)).astype(np.int64)
    large = np.minimum(large, half - 1)
    return ((rel > 0) * half + np.where(n < exact, n, large)).astype(np.int32)


def dilated_bias_table(rel_bias, s, tq):
    heads = rel_bias.shape[1]
    d = np.arange(-(s - 1), s)
    count = np.zeros(d.shape, np.float32)
    for window, dil in A_PATTERNS:
        count += ((d % dil == 0) & (np.abs(d) <= (window // (2 * dil)) * dil)).astype(np.float32)
    logc = np.where(count > 0, np.log(np.maximum(count, 1.0)), NEG).astype(np.float32)
    onehot = (_t5_bucket(d)[:, None] == np.arange(N_BUCKETS)[None, :]).astype(np.float32)
    line = jnp.transpose(jnp.dot(onehot, rel_bias.astype(F32), precision=lax.Precision.HIGHEST)) + logc[None]
    line = line * LOG2E
    width = 2 * s
    line = jnp.pad(line, ((0, 0), (0, width - line.shape[1])))[:, None, :]
    nq = s // tq
    return pl.pallas_call(
        functools.partial(_skew_kernel, tq=tq, nq=nq),
        out_shape=jax.ShapeDtypeStruct((heads, nq, tq, s), F32),
        grid=(heads, nq),
        in_specs=[pl.BlockSpec((None, 1, width), lambda h, qi: (h, 0, 0))],
        out_specs=pl.BlockSpec((None, None, tq, s), lambda h, qi: (h, qi, 0, 0)),
        compiler_params=_params("parallel", "arbitrary"),
        name="bias_skew",
    )(line)


def _skew_kernel(line_ref, o_ref, *, tq, nq):
    width = line_ref.shape[1]
    first = (nq - 1 - pl.program_id(1)) * tq
    x = jnp.broadcast_to(line_ref[...], (tq, width))
    x = pltpu.roll(x, width - (tq - 1) - first, 1, stride=1, stride_axis=0)
    o_ref[...] = x[:, :o_ref.shape[1]]


GLA_UNROLL = 16


def _gla_kernel(q_ref, k_ref, v_ref, g_ref, z_ref, w2f_ref, w2b_ref, bf_ref, bb_ref, on_ref, o_ref,
                la_ref, acc_ref, qcat_ref, upd_ref, dec_ref, scat_ref, st_ref):
    s_len = q_ref.shape[0]
    c = B_CHUNK
    nchunk = s_len // c
    z = z_ref[...].astype(BF16)
    gate = lambda w2_ref, b_ref: jax.nn.log_sigmoid(
        jnp.dot(z, w2_ref[...], preferred_element_type=F32) + b_ref[...]) * (1.0 / B_GATE_TAU)
    la_ref[0] = gate(w2f_ref, bf_ref)
    la_ref[1] = gate(w2b_ref, bb_ref)

    ri = lax.broadcasted_iota(jnp.int32, (c, c), 0)
    ci = lax.broadcasted_iota(jnp.int32, (c, c), 1)
    keep = (ri >= ci, ri <= ci)
    tri3 = tuple(jnp.concatenate([kp.astype(BF16)] * 3, axis=1) for kp in keep)

    def chunk_rows(n):
        return pl.ds(pl.multiple_of(n * c, c), c)

    def pass1(i, carry):
        units = [(i * GLA_UNROLL + u, d) for u in range(GLA_UNROLL) for d in range(2)]
        st = []
        for n, d in units:
            rows = chunk_rows(n)
            gcum = jnp.dot(tri3[d], jnp.concatenate(_split3(la_ref[d, rows, :]), axis=0),
                           preferred_element_type=F32)
            st.append(dict(rows=rows, gcum=gcum))
        for (n, d), c in zip(units, st):
            gcum = c["gcum"]
            gend = gcum[0:1] if d == 1 else gcum[B_CHUNK - 1:B_CHUNK]
            kc = k_ref[c["rows"], :]
            c["q_in"] = (q_ref[c["rows"], :] * (B_DK ** -0.5) * jnp.exp(gcum)).astype(BF16)
            c["k_out"] = (kc * jnp.exp(gend - gcum)).astype(BF16)
            c["att"] = lax.dot_general(c["q_in"], (kc * jnp.exp(-gcum)).astype(BF16), NT_DIMS,
                                       preferred_element_type=F32)
            dec_ref[d, n] = jnp.broadcast_to(jnp.exp(gend), (8, B_DK))
        for (n, d), c in zip(units, st):
            vc = v_ref[c["rows"], :].astype(BF16)
            c["o"] = jnp.dot(jnp.where(keep[d], c["att"], 0.0).astype(BF16), vc, preferred_element_type=F32)
            upd_ref[d, n] = lax.dot_general(vc, c["k_out"], TN_DIMS, preferred_element_type=F32)
            qcat_ref[c["rows"], d * B_DK:(d + 1) * B_DK] = c["q_in"]
        for u in range(GLA_UNROLL):
            acc_ref[st[2 * u]["rows"], :] = st[2 * u]["o"] + st[2 * u + 1]["o"]
        return carry

    lax.fori_loop(0, nchunk // GLA_UNROLL, pass1, 0)

    st_ref[...] = jnp.zeros_like(st_ref)

    def pass2(n, carry):
        for d, m in ((0, n), (1, nchunk - 1 - n)):
            state = st_ref[d]
            scat_ref[m, :, d * B_DK:(d + 1) * B_DK] = state.astype(BF16)
            st_ref[d] = state * dec_ref[d, m][0:1] + upd_ref[d, m]
        return carry

    lax.fori_loop(0, nchunk, pass2, 0)

    def pass3(i, carry):
        rows = [chunk_rows(i * GLA_UNROLL + u) for u in range(GLA_UNROLL)]
        outs = [acc_ref[r, :] + lax.dot_general(qcat_ref[r, :], scat_ref[i * GLA_UNROLL + u], NT_DIMS,
                                                preferred_element_type=F32) for u, r in enumerate(rows)]
        for r, o in zip(rows, outs):
            o = o * lax.rsqrt(jnp.mean(o * o, axis=-1, keepdims=True) + EPS) * on_ref[...]
            g = g_ref[r, :]
            o_ref[r, :] = (o * (g * jax.nn.sigmoid(g))).astype(o_ref.dtype)
        return carry

    lax.fori_loop(0, nchunk // GLA_UNROLL, pass3, 0)


def gla_mixer(proj, w2f, w2b, b_f, b_b, onorm, *, q_col, k_col, v_col, g_col, z_col):
    b, s, _ = proj.shape
    assert s % (B_CHUNK * GLA_UNROLL) == 0, s
    hm = lambda blk: (lambda bi, h: (bi, 0, blk + h))
    w2f_p = jnp.zeros((LANE, B_KEYW), F32).at[:B_GATE_RANK].set(w2f).astype(BF16)
    w2b_p = jnp.zeros((LANE, B_KEYW), F32).at[B_GATE_RANK:2 * B_GATE_RANK].set(w2b).astype(BF16)
    return pl.pallas_call(
        _gla_kernel,
        out_shape=jax.ShapeDtypeStruct((b, s, B_WIDTH), BF16),
        grid=(b, B_HEADS),
        in_specs=[pl.BlockSpec((None, s, B_DK), hm(q_col // B_DK)),
                  pl.BlockSpec((None, s, B_DK), hm(k_col // B_DK)),
                  pl.BlockSpec((None, s, B_DV), hm(v_col // B_DV)),
                  pl.BlockSpec((None, s, B_DV), hm(g_col // B_DV)),
                  pl.BlockSpec((None, s, LANE), lambda bi, h: (bi, 0, z_col // LANE)),
                  pl.BlockSpec((LANE, B_DK), lambda bi, h: (0, h)),
                  pl.BlockSpec((LANE, B_DK), lambda bi, h: (0, h)),
                  pl.BlockSpec((1, B_DK), lambda bi, h: (0, h)),
                  pl.BlockSpec((1, B_DK), lambda bi, h: (0, h)),
                  pl.BlockSpec((1, B_DV), lambda bi, h: (0, 0))],
        out_specs=pl.BlockSpec((None, s, B_DV), lambda bi, h: (bi, 0, h)),
        scratch_shapes=[pltpu.VMEM((2, s, B_DK), F32),
                        pltpu.VMEM((s, B_DV), F32),
                        pltpu.VMEM((s, 2 * B_DK), BF16),
                        pltpu.VMEM((2, s // B_CHUNK, B_DV, B_DK), F32),
                        pltpu.VMEM((2, s // B_CHUNK, 8, B_DK), F32),
                        pltpu.VMEM((s // B_CHUNK, B_DV, 2 * B_DK), BF16),
                        pltpu.VMEM((2, B_DV, B_DK), F32)],
        compiler_params=_params("parallel", "arbitrary"),
        name="gla_mixer",
    )(proj, proj, proj, proj, proj, w2f_p, w2b_p, b_f.reshape(1, -1), b_b.reshape(1, -1), onorm.reshape(1, -1))


def _mla_up_kernel(cq_ref, ckv_ref, kr_ref, qn_ref, kvn_ref, wq_ref, wqr_ref, wkv_ref, cos_ref, sin_ref,
                   q_ref, k_ref, v_ref):
    def rms(x, g):
        return (x * lax.rsqrt(jnp.mean(x * x, axis=-1, keepdims=True) + EPS) * g).astype(BF16)

    cq = rms(cq_ref[...], qn_ref[...])
    ckv = rms(ckv_ref[...], kvn_ref[...])
    cos, sin = cos_ref[...], sin_ref[...]
    kr = kr_ref[...]
    k_rope = kr * cos + pltpu.roll(kr, LANE - C_ROPE, 1) * sin
    lane = lax.broadcasted_iota(jnp.int32, k_rope.shape, 1)
    k_rope = jnp.where(lane < C_ROPE, k_rope, 0.0)
    for h in range(C_HEADS):
        q = jnp.dot(cq, wq_ref[:, h * C_QK:(h + 1) * C_QK], preferred_element_type=F32)
        qp = jnp.dot(cq, wqr_ref[:, h * LANE:(h + 1) * LANE], preferred_element_type=F32)
        q_ref[:, h * C_QK:h * C_QK + C_NOPE] = q[:, :C_NOPE]
        q_ref[:, h * C_QK + C_NOPE:(h + 1) * C_QK] = q[:, C_NOPE:] * cos + qp * sin
        kv = jnp.dot(ckv, wkv_ref[:, h * 2 * LANE:(h + 1) * 2 * LANE], preferred_element_type=F32)
        k_ref[:, h * C_QK:h * C_QK + C_NOPE] = kv[:, :C_NOPE]
        k_ref[:, h * C_QK + C_NOPE:(h + 1) * C_QK] = k_rope
        v_ref[:, h * C_V:(h + 1) * C_V] = kv[:, C_NOPE:]


def _rot_half_cols(w):
    half = w.shape[-1] // 2
    return jnp.concatenate([-w[..., half:], w[..., :half]], axis=-1)


def mla_up(proj, q_norm, w_uq, kv_norm, w_ukv, cos, sin, *, col0, tm=512):
    t = proj.shape[0]
    wq = w_uq.reshape(C_Q_RANK, C_HEADS, C_NOPE + C_ROPE)
    wq_main = jnp.pad(wq, ((0, 0), (0, 0), (0, C_QK - C_NOPE - C_ROPE))).reshape(C_Q_RANK, C_HEADS * C_QK)
    wq_rot = jnp.pad(_rot_half_cols(wq[..., C_NOPE:]), ((0, 0), (0, 0), (0, LANE - C_ROPE)))
    wq_rot = wq_rot.reshape(C_Q_RANK, C_HEADS * LANE)
    row = lambda i: (i, 0)
    full = lambda arr: pl.BlockSpec(arr.shape, lambda i: (0, 0))
    g_q, g_kv = q_norm.reshape(1, -1), kv_norm.reshape(1, -1)
    wq_main, wq_rot, wkv = wq_main.astype(BF16), wq_rot.astype(BF16), w_ukv.astype(BF16)
    return pl.pallas_call(
        _mla_up_kernel,
        out_shape=(jax.ShapeDtypeStruct((t, C_HEADS * C_QK), F32),
                   jax.ShapeDtypeStruct((t, C_HEADS * C_QK), F32),
                   jax.ShapeDtypeStruct((t, C_WIDTH), F32)),
        grid=(t // tm,),
        in_specs=[pl.BlockSpec((tm, C_Q_RANK), lambda i: (i, col0 // C_Q_RANK)),
                  pl.BlockSpec((tm, C_KV_RANK), lambda i: (i, (col0 + C_Q_RANK) // C_KV_RANK)),
                  pl.BlockSpec((tm, LANE), lambda i: (i, (col0 + C_Q_RANK + C_KV_RANK) // LANE)),
                  full(g_q), full(g_kv), full(wq_main), full(wq_rot), full(wkv),
                  pl.BlockSpec((tm, LANE), row), pl.BlockSpec((tm, LANE), row)],
        out_specs=(pl.BlockSpec((tm, C_HEADS * C_QK), row),
                   pl.BlockSpec((tm, C_HEADS * C_QK), row),
                   pl.BlockSpec((tm, C_WIDTH), row)),
        compiler_params=_params("parallel"),
        name="mla_up",
    )(proj, proj, proj, g_q, g_kv, wq_main, wq_rot, wkv, cos, sin)


RG = 4
RGW = RG * D_HEAD
RCH = 64
DC_R, DC_K, DC_V = 0, D_WIDTH, 2 * D_WIDTH
DC_ZG = 3 * D_WIDTH
DC_ZW = DC_ZG + D_G_RANK
DC_ZA = DC_ZW + 2 * D_W_RANK
DC_PAD = 7 * 512


def _head_sums(x, bo3):
    return jnp.concatenate(
        [jnp.dot(jnp.concatenate(_split3(x[:, RGW * g:RGW * (g + 1)]), axis=1), bo3, preferred_element_type=F32)
         for g in range(x.shape[1] // RGW)], axis=1)


def _block_ones():
    i = np.arange(RGW)
    bo = ((i[:, None] // D_HEAD) == (i[None, :] // D_HEAD)).astype(np.float32)
    return jnp.asarray(np.concatenate([bo, bo, bo], axis=0), BF16)


def _rwkv_prep_kernel(x_ref, xp_ref, xn_ref, mu_ref, w2f_ref, w2b_ref, a2_ref, g2_ref, w0f_ref, w0b_ref,
                      a0_ref, kk_ref, ka_ref, rk_ref, bo_ref,
                      r_ref, k_ref, v_ref, a_ref, b_ref, lwf_ref, lwb_ref, g_ref, bonus_ref, *, tiles_per_seq):
    i = pl.program_id(0) % tiles_per_seq
    x = x_ref[...]
    tm = x.shape[0]
    row = lax.broadcasted_iota(jnp.int32, x.shape, 0)
    prev_row = jnp.where(i == 0, 0.0, xp_ref[7:8, :])
    next_row = jnp.where(i == tiles_per_seq - 1, 0.0, xn_ref[0:1, :])
    prev = jnp.where(row == 0, prev_row, pltpu.roll(x, 1, 0))
    nxt = jnp.where(row == tm - 1, next_row, pltpu.roll(x, tm - 1, 0))
    x = x + mu_ref[...] * (0.5 * (prev + nxt) - x)
    r, k, v = x[:, DC_R:DC_R + D_WIDTH], x[:, DC_K:DC_K + D_WIDTH], x[:, DC_V:DC_V + D_WIDTH]
    zg = x[:, DC_ZG:DC_ZG + LANE]
    zw = x[:, DC_ZW:DC_ZW + LANE]
    za = x[:, DC_ZA:DC_ZA + LANE]
    tz = jnp.tanh(zw).astype(BF16)
    log_decay = lambda w0_ref, w2_ref: -np.exp(-0.5).astype(np.float32) * jax.nn.sigmoid(
        w0_ref[...] + jnp.dot(tz, w2_ref[...], preferred_element_type=F32))
    lwf_ref[...] = log_decay(w0f_ref, w2f_ref)
    lwb_ref[...] = log_decay(w0b_ref, w2b_ref)
    ag = jax.nn.sigmoid(a0_ref[...] + jnp.dot(za.astype(BF16), a2_ref[...], preferred_element_type=F32))
    g_ref[...] = jnp.dot(jax.nn.sigmoid(zg).astype(BF16), g2_ref[...], preferred_element_type=F32)
    bo = bo_ref[...]
    kk = k * kk_ref[...]
    kk = kk / jnp.maximum(jnp.sqrt(_head_sums(kk * kk, bo)), 1e-12)
    k = k * (1.0 + (ag - 1.0) * ka_ref[...])
    r_ref[...] = r
    k_ref[...] = k
    v_ref[...] = v
    a_ref[...] = -kk
    b_ref[...] = kk * ag
    bonus_ref[...] = _head_sums(r * k * rk_ref[...], bo) * v


def rwkv_prep(proj, mu, w0_f, w2_f, w0_b, w2_b, a0, a2, g2, k_k, k_a, r_k, *, seq, tm=256):
    t = proj.shape[0]
    tiles_per_seq = seq // tm
    assert seq % tm == 0 and t % seq == 0, (t, seq)
    hb = tm // 8
    nblk8 = t // 8
    pad_rows = lambda w, lo: jnp.zeros((LANE, D_WIDTH), F32).at[lo:lo + w.shape[0]].set(w).astype(BF16)
    vec = lambda u: u.reshape(1, -1)
    consts = [vec(mu), pad_rows(w2_f, 0), pad_rows(w2_b, D_W_RANK), pad_rows(a2, 0), g2.astype(BF16),
              vec(w0_f), vec(w0_b), vec(a0), vec(k_k), vec(k_a), vec(r_k), _block_ones()]
    full = lambda arr: pl.BlockSpec(arr.shape, lambda i: (0, 0))
    out_spec = pl.BlockSpec((tm, D_WIDTH), lambda i: (i, 0))
    return pl.pallas_call(
        functools.partial(_rwkv_prep_kernel, tiles_per_seq=tiles_per_seq),
        out_shape=tuple(jax.ShapeDtypeStruct((t, D_WIDTH), F32) for _ in range(9)),
        grid=(t // tm,),
        in_specs=[pl.BlockSpec((tm, DC_PAD), lambda i: (i, 0)),
                  pl.BlockSpec((8, DC_PAD), lambda i: (jnp.maximum(i * hb - 1, 0), 0)),
                  pl.BlockSpec((8, DC_PAD), lambda i: (jnp.minimum((i + 1) * hb, nblk8 - 1), 0))]
                 + [full(c) for c in consts],
        out_specs=tuple(out_spec for _ in range(9)),
        compiler_params=_params("parallel"),
        name="rwkv_prep",
    )(proj, proj, proj, *consts)


def _rwkv_chunk_kernel(*refs, ngroups):
    ins, (yf_ref, yb_ref, mt_ref) = refs[:12], refs[12:]

    @pl.when(pl.program_id(1) == 0)
    def _():
        mt_ref[...] = jnp.zeros_like(mt_ref)

    row = lax.broadcasted_iota(jnp.int32, (RCH, RGW), 0)
    col = lax.broadcasted_iota(jnp.int32, (RCH, RGW), 1) & (RCH - 1)
    bdmask = (lax.broadcasted_iota(jnp.int32, (RGW, RGW), 0) // D_HEAD
              == lax.broadcasted_iota(jnp.int32, (RGW, RGW), 1) // D_HEAD)
    tr = lax.broadcasted_iota(jnp.int32, (RCH, RCH), 0)
    tc = lax.broadcasted_iota(jnp.int32, (RCH, RCH), 1)
    zero = jnp.zeros((), F32)

    def bd(z):
        zb = z.astype(BF16)
        return jnp.where(bdmask, jnp.concatenate([zb] * RG, axis=0), jnp.zeros((), BF16))

    def mm(x, y, dims=None):
        x = x.astype(BF16)
        if dims is None:
            return jnp.dot(x, y, preferred_element_type=F32)
        return lax.dot_general(x, y, dims, preferred_element_type=F32)

    chains = [(d, g) for d in range(2) for g in range(ngroups)]
    st = []
    for d, g in chains:
        backward = d == 1
        r_ref, k_ref, v_ref, a_ref, b_ref, lw_ref = ins[6 * d:6 * d + 6]
        tri = ((tc >= tr) if backward else (tc <= tr)).astype(BF16)
        sl = slice(RGW * g, RGW * (g + 1))
        r, k, v, a, b, lw = (ref[:, sl] for ref in (r_ref, k_ref, v_ref, a_ref, b_ref, lw_ref))
        lam = jnp.dot(jnp.concatenate([tri] * 3, axis=1), jnp.concatenate(_split3(lw), axis=0),
                      preferred_element_type=F32)
        lamc = lam[0:1] if backward else lam[RCH - 1:RCH]
        e_inv = jnp.exp(-lam)
        e_out = jnp.exp(lamc - lam)
        ar = jnp.concatenate([a * jnp.exp(lam - lw), r * jnp.exp(lam)], axis=0).astype(BF16)
        bk = jnp.concatenate([b * e_out, k * e_out], axis=0).astype(BF16)
        st.append(dict(ar=ar, bk=bk, v=v, lamc=lamc, sl=sl,
                       gb=mm(ar, bd(b * e_inv), NT_DIMS), gk=mm(ar, bd(k * e_inv), NT_DIMS)))
    for (d, g), c in zip(chains, st):
        strict = (col > row) if d == 1 else (col < row)
        incl = (col >= row) if d == 1 else (col <= row)
        c["lp"] = jnp.where(strict, c["gb"][:RCH], zero)
        c["grb"] = jnp.where(incl, c["gb"][RCH:], zero).astype(BF16)
        akrk = jnp.where(jnp.concatenate([strict, incl], axis=0), c["gk"], zero)
        c["mt"] = mt_ref[d, g]
        amrm = mm(c["ar"], c["mt"].astype(BF16), NT_DIMS) + mm(akrk, bd(c["v"]))
        c["u"] = amrm[:RCH]
        c["rm"] = amrm[RCH:]
    for rnd in range(6):
        for c in st:
            lpb = c["lp"].astype(BF16)
            c["u"] = c["u"] + mm(lpb, bd(c["u"]))
            if rnd < 5:
                c["lp"] = mm(lpb, bd(c["lp"]))
    for (d, g), c in zip(chains, st):
        y_ref = yb_ref if d == 1 else yf_ref
        y_ref[:, c["sl"]] = c["rm"] + mm(c["grb"], bd(c["u"]))
        uv = jnp.concatenate([c["u"], c["v"]], axis=0).astype(BF16)
        upd = lax.dot_general(uv, c["bk"], TN_DIMS, preferred_element_type=F32)
        mt_ref[d, g] = c["mt"] * jnp.exp(c["lamc"]) + jnp.where(bdmask, upd, zero)


def rwkv_chunked(r, k, v, a, b, lwf, lwb):
    bsz, s, wd = r.shape
    nc = s // RCH
    assert s % RCH == 0 and wd % RGW == 0, (s, wd)
    fspec = pl.BlockSpec((None, RCH, wd), lambda bi, n: (bi, n, 0))
    bspec = pl.BlockSpec((None, RCH, wd), lambda bi, n: (bi, nc - 1 - n, 0))
    return pl.pallas_call(
        functools.partial(_rwkv_chunk_kernel, ngroups=wd // RGW),
        out_shape=(jax.ShapeDtypeStruct((bsz, s, wd), F32), jax.ShapeDtypeStruct((bsz, s, wd), F32)),
        grid=(bsz, nc),
        in_specs=[fspec] * 6 + [bspec] * 6,
        out_specs=(fspec, bspec),
        scratch_shapes=[pltpu.VMEM((2, wd // RGW, RGW, RGW), F32)],
        compiler_params=_params("parallel", "arbitrary"),
        name="rwkv_chunked",
    )(r, k, v, a, b, lwf, r, k, v, a, b, lwb)


def _rwkv_post_kernel(yf_ref, yb_ref, bonus_ref, g_ref, lng_ref, lnb_ref, bo_ref, o_ref):
    bo = bo_ref[...]
    y = yf_ref[...] + yb_ref[...]
    yc = y - _head_sums(y, bo) * (1.0 / D_HEAD)
    var = _head_sums(yc * yc, bo) * (1.0 / D_HEAD)
    y = yc * lax.rsqrt(var + D_LN_EPS) * lng_ref[...] + lnb_ref[...]
    o_ref[...] = ((y + bonus_ref[...]) * g_ref[...]).astype(o_ref.dtype)


def rwkv_post(yf, yb, bonus, g, ln_g, ln_b, *, tm=512):
    t, wd = yf.shape
    row = pl.BlockSpec((tm, wd), lambda i: (i, 0))
    vec = pl.BlockSpec((1, wd), lambda i: (0, 0))
    bo = _block_ones()
    return pl.pallas_call(
        _rwkv_post_kernel,
        out_shape=jax.ShapeDtypeStruct((t, wd), BF16),
        grid=(t // tm,),
        in_specs=[row, row, row, row, vec, vec, pl.BlockSpec(bo.shape, lambda i: (0, 0))],
        out_specs=row,
        compiler_params=_params("parallel"),
        name="rwkv_post",
    )(yf, yb, bonus, g, ln_g.reshape(1, wd), ln_b.reshape(1, wd), bo)


def rwkv7_mixer(proj, bsz, s, mu, w0_f, w2_f, w0_b, w2_b, a0, a2, g2, k_k, k_a, r_k, ln_g, ln_b):
    r, k, v, a, b, lwf, lwb, g, bonus = rwkv_prep(proj, mu, w0_f, w2_f, w0_b, w2_b, a0, a2, g2, k_k, k_a,
                                                  r_k.reshape(-1), seq=s)
    r3 = lambda u: u.reshape(bsz, s, D_WIDTH)
    yf, yb = rwkv_chunked(r3(r), r3(k), r3(v), r3(a), r3(b), r3(lwf), r3(lwb))
    return rwkv_post(yf.reshape(-1, D_WIDTH), yb.reshape(-1, D_WIDTH), bonus, g, ln_g, ln_b)


MOE_TILE = 1024
MOE_SUB = 288
ROW_ALIGN = 16
MOE_PACK = 2


def _route(logit):
    lane = lax.broadcasted_iota(jnp.int32, logit.shape, 1)
    first_at = lambda mask: jnp.min(jnp.where(mask, lane, jnp.int32(LANE)), axis=-1, keepdims=True)
    is_grp = lane < N_GROUPS
    gl = jnp.where(is_grp, logit, NEG)
    gmax = jnp.max(gl, axis=-1, keepdims=True)
    p_grp = 1.0 / jnp.sum(jnp.where(is_grp, jnp.exp(gl - gmax), 0.0), axis=-1, keepdims=True)
    i_grp = first_at(is_grp & (gl == gmax))
    lo = N_GROUPS + i_grp * EXPERTS_PER_GROUP
    in_grp = (lane >= lo) & (lane < lo + EXPERTS_PER_GROUP)
    el = jnp.where(in_grp, logit, NEG)
    l1 = jnp.max(el, axis=-1, keepdims=True)
    i1 = first_at(in_grp & (el == l1))
    rest = in_grp & (lane != i1)
    el2 = jnp.where(rest, logit, NEG)
    l2 = jnp.max(el2, axis=-1, keepdims=True)
    i2 = first_at(rest & (el2 == l2))
    e2 = jnp.exp(l2 - l1)
    w1 = p_grp / (1.0 + e2)
    w2 = p_grp * e2 / (1.0 + e2)
    return i_grp, jnp.where(lane == i1, w1, jnp.where(lane == i2, w2, 0.0))


def _moe_sort_kernel(x_ref, g_ref, wr_ref, br_ref, hn_ref, comb_ref, pos_ref, off_ref):
    x = x_ref[...]
    tm = x.shape[0]
    hn = (x * lax.rsqrt(jnp.mean(x * x, axis=-1, keepdims=True) + EPS) * g_ref[...]).astype(BF16)
    wr, br = wr_ref[...], br_ref[...]
    i_grp, _ = _route(jnp.dot(hn, wr, preferred_element_type=F32) + br)
    lane = lax.broadcasted_iota(jnp.int32, (tm, LANE), 1)
    onehot = (lane == i_grp).astype(F32)
    ri = lax.broadcasted_iota(jnp.int32, (tm, tm), 0)
    ci = lax.broadcasted_iota(jnp.int32, (tm, tm), 1)
    earlier = jnp.dot((ci < ri).astype(BF16), onehot.astype(BF16), preferred_element_type=F32)
    cnt = jnp.broadcast_to(jnp.sum(onehot, axis=0, keepdims=True), (8, LANE))
    lane8 = lax.broadcasted_iota(jnp.int32, (8, LANE), 1)
    start = jnp.zeros((8, LANE), F32)
    for sh in range(1, N_GROUPS):
        start += jnp.where(lane8 >= sh, pltpu.roll(cnt, sh, 1), 0.0)
    off_ref[...] = start[0:1].astype(jnp.int32)
    pos = jnp.sum(onehot * (start[0:1] + earlier), axis=-1, keepdims=True).astype(jnp.int32)
    pos_ref[...] = pos
    perm_t = (ci == pos).astype(BF16)
    hn_s = lax.dot_general(perm_t, hn, TN_DIMS, preferred_element_type=F32).astype(BF16)
    hn_ref[...] = hn_s
    _, comb = _route(jnp.dot(hn_s, wr, preferred_element_type=F32) + br)
    comb_ref[...] = comb


def moe_sort(x, g, w_grp, b_grp, w_exp, b_exp):
    t, d = x.shape
    tm = MOE_TILE
    assert t % (MOE_TILE * MOE_PACK) == 0, t
    nr = N_GROUPS + N_EXPERTS
    wr = jnp.pad(jnp.concatenate([w_grp, w_exp], axis=1), ((0, 0), (0, LANE - nr))).astype(BF16)
    br = jnp.pad(jnp.concatenate([b_grp, b_exp]), (0, LANE - nr)).reshape(1, LANE)
    row = lambda i: (i, 0)
    hn, comb, pos, off = pl.pallas_call(
        _moe_sort_kernel,
        out_shape=(jax.ShapeDtypeStruct((t, d), BF16), jax.ShapeDtypeStruct((t, LANE), F32),
                   jax.ShapeDtypeStruct((t, 1), jnp.int32), jax.ShapeDtypeStruct((t // tm, 1, LANE), jnp.int32)),
        grid=(t // tm,),
        in_specs=[pl.BlockSpec((tm, d), row), pl.BlockSpec((1, d), lambda i: (0, 0)),
                  pl.BlockSpec((d, LANE), lambda i: (0, 0)), pl.BlockSpec((1, LANE), lambda i: (0, 0))],
        out_specs=(pl.BlockSpec((tm, d), row), pl.BlockSpec((tm, LANE), row), pl.BlockSpec((tm, 1), row),
                   pl.BlockSpec((None, 1, LANE), lambda i: (i, 0, 0))),
        compiler_params=_params("parallel"),
        name="moe_sort",
    )(x, g.reshape(1, d), wr, br)
    bounds = jnp.concatenate([off[:, 0, :N_GROUPS], jnp.full((t // tm, 1), tm, jnp.int32)], axis=1)
    return hn, comb, pos, bounds.reshape(-1)


def _moe_group_kernel(bounds_ref, hn_ref, c_ref, wg_ref, wu_ref, wd_ref, y_ref):
    i, g, j = pl.program_id(0), pl.program_id(1), pl.program_id(2)

    @pl.when((g == 0) & (j == 0))
    def _():
        y_ref[...] = jnp.zeros_like(y_ref)

    expert_lane = N_GROUPS + g * EXPERTS_PER_GROUP + j

    def sub_tile(want, size):
        r0 = pl.multiple_of(jnp.minimum(want, lo + MOE_TILE - size), ROW_ALIGN)
        rows = pl.ds(r0, size)
        x = hn_ref[rows, :]
        hg = jnp.dot(x, wg_ref[...], preferred_element_type=F32)
        hu = jnp.dot(x, wu_ref[...], preferred_element_type=F32)
        lane = lax.broadcasted_iota(jnp.int32, (size, LANE), 1)
        c = jnp.sum(jnp.where(lane == expert_lane, c_ref[rows, :], 0.0), axis=-1, keepdims=True)
        row = lax.broadcasted_iota(jnp.int32, (size, 1), 0)
        c = jnp.where(row + r0 >= want, c, 0.0)
        hid = (hg * jax.nn.sigmoid(hg)) * hu * c
        y_ref[rows, :] += jnp.dot(hid.astype(BF16), wd_ref[...], preferred_element_type=F32)

    for half in range(MOE_PACK):
        base = (i * MOE_PACK + half) * (N_GROUPS + 1) + g
        lo = half * MOE_TILE
        start, end = lo + bounds_ref[base], lo + bounds_ref[base + 1]
        first = (start // ROW_ALIGN) * ROW_ALIGN
        n_full = (end - first) // MOE_SUB
        rest = end - first - n_full * MOE_SUB

        def full(k, carry, first=first):
            sub_tile(first + k * MOE_SUB, MOE_SUB)
            return carry

        lax.fori_loop(0, n_full, full, 0)
        tail = first + n_full * MOE_SUB

        @pl.when(rest > MOE_SUB // 2)
        def _():
            sub_tile(tail, MOE_SUB)

        @pl.when((rest > 0) & (rest <= MOE_SUB // 2))
        def _():
            sub_tile(tail, MOE_SUB // 2)


def moe_group_experts(hn, comb, bounds, w_gate, w_up, w_down):
    t, d = hn.shape
    tm = MOE_TILE * MOE_PACK
    ne, _, ff = w_gate.shape
    ex = lambda i, g, j, b: (g * EXPERTS_PER_GROUP + j, 0, 0)
    once = pl.Buffered(1) if MOE_PACK > 1 else None
    return pl.pallas_call(
        _moe_group_kernel,
        out_shape=jax.ShapeDtypeStruct((t, d), F32),
        grid_spec=pltpu.PrefetchScalarGridSpec(
            num_scalar_prefetch=1,
            grid=(t // tm, N_GROUPS, EXPERTS_PER_GROUP),
            in_specs=[pl.BlockSpec((tm, d), lambda i, g, j, b: (i, 0), pipeline_mode=once),
                      pl.BlockSpec((tm, LANE), lambda i, g, j, b: (i, 0)),
                      pl.BlockSpec((None, d, ff), ex), pl.BlockSpec((None, d, ff), ex),
                      pl.BlockSpec((None, ff, d), ex)],
            out_specs=pl.BlockSpec((tm, d), lambda i, g, j, b: (i, 0), pipeline_mode=once)),
        compiler_params=_params("parallel", "arbitrary", "arbitrary"),
        name="moe_group_experts",
    )(bounds, hn, comb, w_gate, w_up, w_down)


def _moe_unsort_kernel(y_ref, pos_ref, x_ref, o_ref):
    tm = y_ref.shape[0]
    perm_t = (lax.broadcasted_iota(jnp.int32, (tm, tm), 1) == pos_ref[...]).astype(BF16)
    y = y_ref[...]
    hi = y.astype(BF16)
    lo = (y - hi.astype(F32)).astype(BF16)
    o_ref[...] = (x_ref[...] + jnp.dot(perm_t, hi, preferred_element_type=F32)
                  + jnp.dot(perm_t, lo, preferred_element_type=F32))


def moe_unsort(y, pos, x):
    t, d = x.shape
    tm, tn = MOE_TILE, d // 2
    blk = pl.BlockSpec((tm, tn), lambda i, j: (i, j))
    return pl.pallas_call(
        _moe_unsort_kernel,
        out_shape=jax.ShapeDtypeStruct((t, d), F32),
        grid=(t // tm, d // tn),
        in_specs=[blk, pl.BlockSpec((tm, 1), lambda i, j: (i, 0)), blk],
        out_specs=blk,
        compiler_params=_params("parallel", "arbitrary"),
        name="moe_unsort",
    )(y, pos, x)


def _moe_unsort_norm_kernel(y_ref, pos_ref, x_ref, g_ref, o_ref):
    tm, ts = pos_ref.shape[0], y_ref.shape[0]
    perm_t = (lax.broadcasted_iota(jnp.int32, (tm, ts), 1) == pos_ref[...]).astype(BF16)
    y = y_ref[...]
    hi = y.astype(BF16)
    lo = (y - hi.astype(F32)).astype(BF16)
    x = (x_ref[...] + jnp.dot(perm_t, hi, preferred_element_type=F32)
         + jnp.dot(perm_t, lo, preferred_element_type=F32))
    o_ref[...] = x * lax.rsqrt(jnp.mean(x * x, axis=-1, keepdims=True) + EPS) * g_ref[...]


def moe_unsort_norm(y, pos, x, g, *, row0, rows, tm=512):
    d = x.shape[1]
    per = MOE_TILE // tm
    r0 = row0 // tm
    return pl.pallas_call(
        _moe_unsort_norm_kernel,
        out_shape=jax.ShapeDtypeStruct((rows, d), F32),
        grid=(rows // tm,),
        in_specs=[pl.BlockSpec((MOE_TILE, d), lambda i: ((i + r0) // per, 0)),
                  pl.BlockSpec((tm, 1), lambda i: (i + r0, 0)),
                  pl.BlockSpec((tm, d), lambda i: (i + r0, 0)),
                  pl.BlockSpec((1, d), lambda i: (0, 0))],
        out_specs=pl.BlockSpec((tm, d), lambda i: (i, 0)),
        compiler_params=_params("parallel"),
        name="moe_unsort_norm",
    )(y, pos, x, g.reshape(1, d))


def hier_moe(x, g, w_grp, b_grp, w_exp, b_exp, w_gate, w_up, w_down, *, final=None):
    hn, comb, pos, bounds = moe_sort(x, g, w_grp, b_grp, w_exp, b_exp)
    y = moe_group_experts(hn, comb, bounds, w_gate.astype(BF16), w_up.astype(BF16), w_down.astype(BF16))
    if final is None:
        return moe_unsort(y, pos, x)
    gain, counts = final
    starts = np.cumsum((0,) + tuple(counts))[:-1]
    return tuple(moe_unsort_norm(y, pos, x, gain, row0=int(r0), rows=int(n)) for r0, n in zip(starts, counts))


def even_layer(x, bsz, s, norm_g, rel_bias, w_in, w_out, w2_f, b_f, w2_b, b_b, onorm):
    n_pad = _round_up(EVEN_IN, LANE)
    w_in_p = jnp.pad(w_in, ((0, 0), (0, n_pad - EVEN_IN))).astype(BF16)
    proj = norm_linear(x, norm_g, w_in_p, tn_target=896).reshape(bsz, s, n_pad)
    tq = ATTN_TQ
    ya = attention(proj, proj, proj, heads=A_HEADS, dq=HEAD_DIM, dv=HEAD_DIM,
                   q_off=0, k_off=A_HEADS, v_off=2 * A_HEADS, scale=HEAD_DIM ** -0.5,
                   bias=dilated_bias_table(rel_bias, s, tq), tq=tq)
    q_col = 3 * A_WIDTH
    yb = gla_mixer(proj, w2_f, w2_b, b_f, b_b, onorm, q_col=q_col, k_col=q_col + B_KEYW,
                   v_col=q_col + 2 * B_KEYW, g_col=q_col + 2 * B_KEYW + B_WIDTH,
                   z_col=q_col + 2 * B_KEYW + 2 * B_WIDTH)
    t = bsz * s
    return out_proj(ya.reshape(t, A_WIDTH), yb.reshape(t, B_WIDTH), w_out.astype(BF16), x)


def _odd_columns(w_in, mu):
    c0 = C_IN
    cut = lambda u, lo, n: u[..., lo:lo + n]
    zpad = lambda u, n: jnp.pad(u, [(0, 0)] * (u.ndim - 1) + [(0, n)])
    off = np.cumsum((0,) + D_SPLITS)
    def rwkv_cols(u):
        parts = [cut(u, off[0], 3 * D_WIDTH), cut(u, off[6], D_G_RANK), cut(u, off[3], 2 * D_W_RANK),
                 cut(u, off[5], D_A_RANK)]
        u = jnp.concatenate(parts, axis=-1)
        return zpad(u, DC_PAD - u.shape[-1])
    w_kr = w_in[:, C_Q_RANK + C_KV_RANK:C_IN]
    w_all = jnp.concatenate([rwkv_cols(w_in[:, c0:]), w_in[:, :C_IN], _rot_half_cols(w_kr)], axis=1)
    n_pad = _round_up(w_all.shape[1], 9 * LANE)
    return zpad(w_all, n_pad - w_all.shape[1]).astype(BF16), rwkv_cols(mu)


def odd_layer(x, bsz, s, norm_g, w_in, w_out, q_norm, w_uq, kv_norm, w_ukv, mu, w0_f, w2_f, w0_b, w2_b,
              a0, a2, g2, k_k, k_a, r_k, ln_g, ln_b):
    t = bsz * s
    w_all, mu_cols = _odd_columns(w_in, mu)
    proj = norm_linear(x, norm_g, w_all, tn_target=1536)
    inv = 1.0 / (ROPE_THETA ** (jnp.arange(0, C_ROPE, 2, dtype=F32) / C_ROPE))
    ang = jnp.arange(s, dtype=F32)[:, None] * inv[None, :]
    cos = jnp.pad(jnp.concatenate([jnp.cos(ang)] * 2, axis=1), ((0, 0), (0, LANE - C_ROPE)), constant_values=1.0)
    sin = jnp.pad(jnp.concatenate([jnp.sin(ang)] * 2, axis=1), ((0, 0), (0, LANE - C_ROPE)))
    q, k, v = mla_up(proj, q_norm, w_uq, kv_norm, w_ukv, jnp.tile(cos, (bsz, 1)), jnp.tile(sin, (bsz, 1)),
                     col0=DC_PAD)
    r3 = lambda u: u.reshape(bsz, s, -1)
    yc = attention(r3(q), r3(k), r3(v), heads=C_HEADS, dq=C_QK, dv=C_V, q_off=0, k_off=0, v_off=0,
                   scale=(C_NOPE + C_ROPE) ** -0.5, tq=s)
    yd = rwkv7_mixer(proj, bsz, s, mu_cols, w0_f, w2_f, w0_b, w2_b, a0, a2, g2, k_k, k_a, r_k, ln_g, ln_b)
    return out_proj(yc.reshape(t, C_WIDTH), yd, w_out.astype(BF16), x)


def kernel(x_prompt, x_sample, rel_bias, norm_mix, norm_ffn, norm_final, ev_w_in, ev_w_out, ev_gla_w2_f, ev_gla_b_f, ev_gla_w2_b, ev_gla_b_b, ev_gla_onorm, od_w_in, od_w_out, od_q_norm, od_w_uq, od_kv_norm, od_w_ukv, od_mu, od_w0_f, od_w2_f, od_w0_b, od_w2_b, od_a0, od_a2, od_g2, od_k_k, od_k_a, od_r_k, od_ln_g, od_ln_b, moe_w_grp, moe_b_grp, moe_w_exp, moe_b_exp, moe_w_gate, moe_w_up, moe_w_down):
    nb_p = x_prompt.shape[0]
    x = jnp.concatenate([x_prompt, x_sample], axis=0)
    bsz, s, d = x.shape
    x = x.reshape(bsz * s, d)
    for i in range(DEPTH):
        j = i // 2
        if i % 2 == 0:
            x = even_layer(x, bsz, s, norm_mix[i], rel_bias, ev_w_in[j], ev_w_out[j], ev_gla_w2_f[j],
                           ev_gla_b_f[j], ev_gla_w2_b[j], ev_gla_b_b[j], ev_gla_onorm[j])
        else:
            x = odd_layer(x, bsz, s, norm_mix[i], od_w_in[j], od_w_out[j], od_q_norm[j], od_w_uq[j],
                          od_kv_norm[j], od_w_ukv[j], od_mu[j], od_w0_f[j], od_w2_f[j], od_w0_b[j],
                          od_w2_b[j], od_a0[j], od_a2[j], od_g2[j], od_k_k[j], od_k_a[j], od_r_k[j],
                          od_ln_g[j], od_ln_b[j])
        final = (norm_final, (nb_p * s, (bsz - nb_p) * s)) if i == DEPTH - 1 else None
        x = hier_moe(x, norm_ffn[i], moe_w_grp[i], moe_b_grp[i], moe_w_exp[i], moe_b_exp[i],
                     moe_w_gate[i], moe_w_up[i], moe_w_down[i], final=final)
    y_p, y_s = x
    return (y_p.reshape(nb_p, s, d), y_s.reshape(bsz - nb_p, s, d))
```
